```python
import jax, jax.numpy as jnp
from jax import lax
import numpy as np

D_MODEL = 1024
BATCH = 4
SEQ = 8192
DEPTH = 2

GRID_W = 64
CTX_LEN = 256
N_EVEN = (DEPTH + 1) // 2
N_ODD = DEPTH // 2
NORM_EPS = 1e-6

MLA_HEADS = 8
QK_NOPE = 64
QK_ROPE = 32
QK_HEAD = QK_NOPE + QK_ROPE
V_HEAD = 64
Q_LORA = 384
KV_LORA = 256
ROPE_FREQS = QK_ROPE // 4
ROPE_BASE = 10000.0
Q_BLOCK = 128
MLA_WIDTH = MLA_HEADS * V_HEAD
FNET_GROUPS = 4
FNET_GROUP_DIM = 128
FNET_WIDTH = FNET_GROUPS * FNET_GROUP_DIM
EVEN_WIDTH = MLA_WIDTH + FNET_WIDTH
E_KR0 = KV_LORA
E_Q0 = KV_LORA + QK_ROPE
E_F0 = E_Q0 + Q_LORA
E_G0 = E_F0 + FNET_WIDTH
EVEN_IN = E_G0 + EVEN_WIDTH
RWKV_HEAD = 64
RWKV_WIDTH = D_MODEL
RWKV_HEADS = RWKV_WIDTH // RWKV_HEAD
DECAY_LORA = 64
AAA_LORA = 64
N_DIR = 2
GN_EPS = 64e-5
SHIFT_W = 3
O_V0 = RWKV_WIDTH
O_WD0 = 2 * RWKV_WIDTH
O_AD0 = O_WD0 + N_DIR * DECAY_LORA
O_R0 = O_AD0 + N_DIR * AAA_LORA
STATE_CH = O_R0
CONV_CH = O_R0 + RWKV_WIDTH
ODD_IN = CONV_CH + RWKV_WIDTH

kernel_name = 'hybrid_mla_fnet_rwkv7_dit'


def rms_norm(x, g):
    x32 = x.astype(jnp.float32)
    y = x32 * lax.rsqrt(jnp.mean(x32 * x32, axis=-1, keepdims=True) + NORM_EPS)
    return (y * g.astype(jnp.float32)).astype(x.dtype)


def ada_chunks(cond, w, b, n):
    m = jax.nn.silu(cond) @ w[:, :n * D_MODEL] + b[:n * D_MODEL]
    return [m[..., i * D_MODEL:(i + 1) * D_MODEL] for i in range(n)]


def axial_rope(rows):
    row = jnp.repeat(jnp.arange(rows, dtype=jnp.float32), GRID_W)
    col = jnp.tile(jnp.arange(GRID_W, dtype=jnp.float32), rows)
    inv = 1.0 / (ROPE_BASE ** (jnp.arange(ROPE_FREQS, dtype=jnp.float32) / ROPE_FREQS))
    ang = jnp.stack([row[:, None] * inv, col[:, None] * inv], axis=1)
    return jnp.cos(ang), jnp.sin(ang)


def apply_rope_tail(x, cos, sin):
    b, t, h, _ = x.shape
    xp = x[..., QK_NOPE:].reshape(b, t, h, 2, 2, ROPE_FREQS)
    x1, x2 = xp[..., 0, :], xp[..., 1, :]
    cs = cos[None, :, None].astype(x.dtype)
    sn = sin[None, :, None].astype(x.dtype)
    rot = jnp.stack([x1 * cs - x2 * sn, x2 * cs + x1 * sn], axis=-2).reshape(b, t, h, QK_ROPE)
    return jnp.concatenate([x[..., :QK_NOPE], rot], axis=-1)


def mla_kv(u, kv_norm, w_ukv, k_head_norm, rope):
    b, t, _ = u.shape
    ckv = rms_norm(u[..., :KV_LORA], kv_norm)
    kv = (ckv @ w_ukv).reshape(b, t, MLA_HEADS, QK_NOPE + V_HEAD)
    k_pe = jnp.broadcast_to(u[..., E_KR0:E_Q0][:, :, None, :], (b, t, MLA_HEADS, QK_ROPE))
    k = rms_norm(jnp.concatenate([kv[..., :QK_NOPE], k_pe], axis=-1), k_head_norm)
    if rope is not None:
        k = apply_rope_tail(k, *rope)
    return k, kv[..., QK_NOPE:]


def mla_q(u, q_norm, w_uq, q_head_norm, rope):
    b, t, _ = u.shape
    cq = rms_norm(u[..., E_Q0:E_F0], q_norm)
    q = rms_norm((cq @ w_uq).reshape(b, t, MLA_HEADS, QK_HEAD), q_head_norm)
    if rope is not None:
        q = apply_rope_tail(q, *rope)
    return q


def attend_context(q, k, v):
    b, t, h, _ = q.shape
    s = jnp.einsum('bqhe,bkhe->bhqk', q, k).astype(jnp.float32) * (QK_HEAD ** -0.5)
    p = jax.nn.softmax(s, axis=-1).astype(v.dtype)
    return jnp.einsum('bhqk,bkhd->bqhd', p, v).reshape(b, t, h * V_HEAD)


def attend_latent(q, k_lat, v_lat, k_ctx, v_ctx):
    b, n, h, _ = q.shape
    nb = n // Q_BLOCK
    qb = q.reshape(b, nb, Q_BLOCK, h, QK_HEAD).transpose(1, 0, 3, 2, 4)
    kl = k_lat.transpose(0, 2, 1, 3)
    vl = v_lat.transpose(0, 2, 1, 3)
    kc = k_ctx.transpose(0, 2, 1, 3)
    vc = v_ctx.transpose(0, 2, 1, 3)

    def one_block(qi):
        s = jnp.concatenate([jnp.einsum('bhqe,bhke->bhqk', qi, kl),
                             jnp.einsum('bhqe,bhke->bhqk', qi, kc)], axis=-1)
        p = jax.nn.softmax(s.astype(jnp.float32) * (QK_HEAD ** -0.5), axis=-1).astype(vl.dtype)
        return (jnp.einsum('bhqk,bhkd->bhqd', p[..., :n], vl)
                + jnp.einsum('bhqk,bhkd->bhqd', p[..., n:], vc))

    o = lax.map(one_block, qb)
    return o.transpose(1, 0, 3, 2, 4).reshape(b, n, h * V_HEAD)


def fourier_mix(u, w_fnet):
    b, t, _ = u.shape
    ug = u.reshape(b, t, FNET_GROUPS, FNET_GROUP_DIM).astype(jnp.float32)
    f = jnp.fft.fft2(ug, axes=(1, 3), norm='ortho').real.astype(u.dtype)
    return jnp.einsum('btgc,gcd->btgd', f, w_fnet).reshape(b, t, FNET_WIDTH)


def even_mixer(h_lat, h_ctx, rope, need_ctx, w_in, kv_norm, q_norm, w_uq, w_ukv,
               q_head_norm, k_head_norm, w_fnet, w_out):
    u_lat = h_lat @ w_in
    u_ctx = h_ctx @ (w_in if need_ctx else w_in[:, :E_Q0])
    k_c, v_c = mla_kv(u_ctx, kv_norm, w_ukv, k_head_norm, None)
    k_l, v_l = mla_kv(u_lat, kv_norm, w_ukv, k_head_norm, rope)
    q_l = mla_q(u_lat, q_norm, w_uq, q_head_norm, rope)

    def merge(o, u):
        mix = jnp.concatenate([o, fourier_mix(u[..., E_F0:E_G0], w_fnet)], axis=-1)
        return (mix * jax.nn.silu(u[..., E_G0:])) @ w_out

    y_lat = merge(attend_latent(q_l, k_l, v_l, k_c, v_c), u_lat)
    y_ctx = None
    if need_ctx:
        q_c = mla_q(u_ctx, q_norm, w_uq, q_head_norm, None)
        y_ctx = merge(attend_context(q_c, k_c, v_c), u_ctx)
    return y_lat, y_ctx


def token_shift(z, w):
    zp = jnp.pad(z, ((0, 0), (1, 1), (0, 0)))
    return w[0] * zp[:, :-2] + w[1] * z + w[2] * zp[:, 2:]


def rwkv_scan(w, k, v, kk, a, r, s0, reverse):
    seqs = [jnp.moveaxis(t_, 1, 0) for t_ in (w, k, v, kk, a)]
    if r is not None:
        seqs.append(jnp.moveaxis(r, 1, 0))

    def step(s, inp):
        w_t, k_t, v_t, kk_t, a_t = inp[:5]
        sa = jnp.einsum('bhvk,bhk->bhv', s, kk_t)
        s = (s * w_t[:, :, None, :]
             - sa[..., None] * (kk_t * a_t)[:, :, None, :]
             + v_t[..., None] * k_t[:, :, None, :])
        y = jnp.einsum('bhvk,bhk->bhv', s, inp[5]) if len(inp) > 5 else None
        return s, y

    s_fin, ys = lax.scan(step, s0, tuple(seqs), reverse=reverse)
    return s_fin, (None if r is None else jnp.moveaxis(ys, 0, 1))


def head_group_norm(y, w, b):
    bsz, t, _, _ = y.shape
    mu = jnp.mean(y, axis=-1, keepdims=True)
    var = jnp.mean(jnp.square(y - mu), axis=-1, keepdims=True)
    yn = ((y - mu) * lax.rsqrt(var + GN_EPS)).reshape(bsz, t, RWKV_WIDTH)
    return yn * w + b


def rwkv_time_mix(z, gate, s0, w0, w2, a0, a2, k_k, k_a, r_k, gn_w, gn_b, w_out, with_out):
    b, t, _ = z.shape
    heads = lambda a_: a_.reshape(b, t, RWKV_HEADS, RWKV_HEAD)
    z32 = z.astype(jnp.float32)
    k = z32[..., :O_V0]
    v = heads(z32[..., O_V0:O_WD0])
    wd = z32[..., O_WD0:O_AD0].reshape(b, t, N_DIR, DECAY_LORA)
    ad = z32[..., O_AD0:STATE_CH].reshape(b, t, N_DIR, AAA_LORA)
    w_raw = w0 + jnp.einsum('btzr,zrc->btzc', jnp.tanh(wd), w2)
    decay = jnp.exp(-jnp.exp(-jax.nn.softplus(-w_raw) - 0.5))
    a = jax.nn.sigmoid(a0 + jnp.einsum('btzr,zrc->btzc', ad, a2))
    kk = heads(k * k_k)
    kk = kk / jnp.maximum(jnp.sqrt(jnp.sum(kk * kk, axis=-1, keepdims=True)), 1e-12)
    k_dir = k[:, :, None, :] * (1.0 + (a - 1.0) * k_a)
    r = heads(z32[..., O_R0:CONV_CH]) if with_out else None
    finals, outs, bonuses = [], [], []
    for d in range(N_DIR):
        kd = heads(k_dir[:, :, d])
        s_fin, ys = rwkv_scan(heads(decay[:, :, d]), kd, v, kk, heads(a[:, :, d]), r, s0[d], d == 1)
        finals.append(s_fin)
        if with_out:
            outs.append(ys)
            bonuses.append(jnp.sum(r * kd * r_k, axis=-1, keepdims=True) * v)
    if not with_out:
        return None, finals
    o = head_group_norm(outs[0] + outs[1], gn_w, gn_b) + (bonuses[0] + bonuses[1]).reshape(b, t, RWKV_WIDTH)
    return (o.astype(gate.dtype) * jax.nn.silu(gate)) @ w_out, finals


def odd_mixer(h_lat, h_ctx, need_ctx, w_in, shift_w, w0, w2, a0, a2, k_k, k_a, r_k, gn_w, gn_b, w_out):
    bsz = h_ctx.shape[0]
    s_zero = jnp.zeros((bsz, RWKV_HEADS, RWKV_HEAD, RWKV_HEAD), jnp.float32)
    if need_ctx:
        u_ctx = h_ctx @ w_in
        y_ctx, s_ctx = rwkv_time_mix(token_shift(u_ctx[..., :CONV_CH], shift_w), u_ctx[..., CONV_CH:],
                                     (s_zero, s_zero), w0, w2, a0, a2, k_k, k_a, r_k, gn_w, gn_b, w_out, True)
    else:
        u_ctx = h_ctx @ w_in[:, :STATE_CH]
        y_ctx, s_ctx = rwkv_time_mix(token_shift(u_ctx, shift_w[:, :STATE_CH]), None,
                                     (s_zero, s_zero), w0, w2, a0, a2, k_k, k_a, r_k, gn_w, gn_b, w_out, False)
    u_lat = h_lat @ w_in
    y_lat, _ = rwkv_time_mix(token_shift(u_lat[..., :CONV_CH], shift_w), u_lat[..., CONV_CH:],
                             s_ctx, w0, w2, a0, a2, k_k, k_a, r_k, gn_w, gn_b, w_out, True)
    return y_lat, y_ctx


def setup_inputs(seed: int = 0) -> dict:
    key = jax.random.key(seed)
    ks = iter(list(jax.random.split(key, 40)))
    nrm = lambda shape, scale: jax.random.normal(next(ks), shape, jnp.float32) * scale
    D = D_MODEL
    W = RWKV_WIDTH
    return {
        'x': nrm((BATCH, SEQ, D), 1.0),
        'c': nrm((BATCH, D), 1.0),
        'ctx': nrm((BATCH, CTX_LEN, D), 1.0),
        'c_ctx': nrm((D,), 1.0),
        'ada_w': nrm((DEPTH, D, 3 * D), 0.5 * D ** -0.5),
        'ada_b': nrm((DEPTH, 3 * D), 0.02),
        'norm_g': 1.0 + nrm((DEPTH, D), 0.02),
        'e_w_in': nrm((N_EVEN, D, EVEN_IN), D ** -0.5),
        'e_kv_norm': 1.0 + nrm((N_EVEN, KV_LORA), 0.02),
        'e_q_norm': 1.0 + nrm((N_EVEN, Q_LORA), 0.02),
        'e_w_uq': nrm((N_EVEN, Q_LORA, MLA_HEADS * QK_HEAD), Q_LORA ** -0.5),
        'e_w_ukv': nrm((N_EVEN, KV_LORA, MLA_HEADS * (QK_NOPE + V_HEAD)), KV_LORA ** -0.5),
        'e_q_head_norm': 1.0 + nrm((N_EVEN, QK_HEAD), 0.02),
        'e_k_head_norm': 1.0 + nrm((N_EVEN, QK_HEAD), 0.02),
        'e_w_fnet': nrm((N_EVEN, FNET_GROUPS, FNET_GROUP_DIM, FNET_GROUP_DIM), FNET_GROUP_DIM ** -0.5),
        'e_w_out': nrm((N_EVEN, EVEN_WIDTH, D), EVEN_WIDTH ** -0.5),
        'o_w_in': nrm((N_ODD, D, ODD_IN), D ** -0.5),
        'o_shift_w': jnp.array([0.3, 1.0, 0.3], jnp.float32)[None, :, None] + nrm((N_ODD, SHIFT_W, CONV_CH), 0.05),
        'o_w0': jnp.linspace(-5.0, -1.0, W, dtype=jnp.float32)[None, None, :] + nrm((N_ODD, N_DIR, W), 0.3),
        'o_w2': nrm((N_ODD, N_DIR, DECAY_LORA, W), 0.1 * DECAY_LORA ** -0.5),
        'o_a0': nrm((N_ODD, N_DIR, W), 0.1),
        'o_a2': nrm((N_ODD, N_DIR, AAA_LORA, W), AAA_LORA ** -0.5),
        'o_k_k': 0.85 + nrm((N_ODD, W), 0.02),
        'o_k_a': 1.0 + nrm((N_ODD, W), 0.02),
        'o_r_k': nrm((N_ODD, RWKV_HEADS, RWKV_HEAD), 0.1),
        'o_gn_w': 1.0 + nrm((N_ODD, W), 0.02),
        'o_gn_b': nrm((N_ODD, W), 0.02),
        'o_w_out': nrm((N_ODD, W, D), W ** -0.5),
    }


def reference(x, c, ctx, c_ctx, ada_w, ada_b, norm_g, e_w_in, e_kv_norm, e_q_norm, e_w_uq, e_w_ukv,
              e_q_head_norm, e_k_head_norm, e_w_fnet, e_w_out, o_w_in, o_shift_w, o_w0, o_w2, o_a0,
              o_a2, o_k_k, o_k_a, o_r_k, o_gn_w, o_gn_b, o_w_out):
    rows = x.shape[1] // GRID_W
    rope = axial_rope(rows)
    cond_lat = c[:, None, :]
    cond_ctx = c_ctx[None, None, :]
    for layer in range(DEPTH):
        need_ctx = layer < DEPTH - 1
        shift_l, scale_l, gate_l = ada_chunks(cond_lat, ada_w[layer], ada_b[layer], 3)
        mod_c = ada_chunks(cond_ctx, ada_w[layer], ada_b[layer], 3 if need_ctx else 2)
        h_lat = rms_norm(x, norm_g[layer]) * (1.0 + scale_l) + shift_l
        h_ctx = rms_norm(ctx, norm_g[layer]) * (1.0 + mod_c[1]) + mod_c[0]
        j = layer // 2
        if layer % 2 == 0:
            y_lat, y_ctx = even_mixer(h_lat, h_ctx, rope, need_ctx, e_w_in[j], e_kv_norm[j], e_q_norm[j],
                                      e_w_uq[j], e_w_ukv[j], e_q_head_norm[j], e_k_head_norm[j],
                                      e_w_fnet[j], e_w_out[j])
        else:
            y_lat, y_ctx = odd_mixer(h_lat, h_ctx, need_ctx, o_w_in[j], o_shift_w[j], o_w0[j], o_w2[j],
                                     o_a0[j], o_a2[j], o_k_k[j], o_k_a[j], o_r_k[j], o_gn_w[j],
                                     o_gn_b[j], o_w_out[j])
        x = x + gate_l * y_lat
        if need_ctx:
            ctx = ctx + mod_c[2] * y_ctx
    return x
```

```python
import functools
import math

import numpy as np
import jax
import jax.numpy as jnp
from jax import lax
from jax.experimental import pallas as pl
from jax.experimental.pallas import tpu as pltpu

F32 = jnp.float32
BF16 = jnp.bfloat16

GRID_W = 64
NORM_EPS = 1e-6
MLA_HEADS = 8
QK_NOPE = 64
QK_ROPE = 32
QK_HEAD = QK_NOPE + QK_ROPE
V_HEAD = 64
Q_LORA = 384
KV_LORA = 256
ROPE_FREQS = QK_ROPE // 4
ROPE_BASE = 10000.0
FNET_GROUPS = 4
FNET_GROUP_DIM = 128
RWKV_HEAD = 64
DECAY_LORA = 64
AAA_LORA = 64
GN_EPS = 64e-5

LANE = 128
CHUNK = 64
HEAD_SLOT = 128
VMEM_LIMIT = 56 * 1024 * 1024

NN = (((1,), (0,)), ((), ()))
NT = (((1,), (1,)), ((), ()))
TN = (((0,), (0,)), ((), ()))


def _cparams(sem):
    return pltpu.CompilerParams(dimension_semantics=sem, vmem_limit_bytes=VMEM_LIMIT)


def _mm(a, b, dn=NN):
    return lax.dot_general(a.astype(BF16), b.astype(BF16), dn, preferred_element_type=F32)


def _split(a):
    hi = a.astype(BF16)
    lo = (a - hi.astype(F32)).astype(BF16)
    return hi, lo


def _mm3(a, b, dn=NN):
    ah, al = _split(a)
    bh, bl = _split(b)
    d = lambda x, y: lax.dot_general(x, y, dn, preferred_element_type=F32)
    return d(ah, bh) + d(al, bh) + d(ah, bl)


def _mm2r(a, b_exact):
    ah, al = _split(a)
    bb = b_exact.astype(BF16)
    d = lambda x: lax.dot_general(x, bb, NN, preferred_element_type=F32)
    return d(ah) + d(al)


def _mm2l(a_exact, b):
    bh, bl = _split(b)
    aa = a_exact.astype(BF16)
    d = lambda y: lax.dot_general(aa, y, NN, preferred_element_type=F32)
    return d(bh) + d(bl)


def _sigmoid(x):
    return 1.0 / (1.0 + jnp.exp(-x))


def _modnorm(x, g, sc1, sh):
    y = x * lax.rsqrt(jnp.mean(x * x, axis=-1, keepdims=True) + NORM_EPS)
    return (y * g) * sc1 + sh


def _iota2(shape, dim):
    return lax.broadcasted_iota(jnp.int32, shape, dim)


def _ada_body(c_ref, w_ref, b_ref, o_ref):
    c = c_ref[...]
    s = c * _sigmoid(c)
    o_ref[0] = _mm3(s, w_ref[0]) + b_ref[0]


def _ada(cond8, ada_w, ada_b):
    depth, d, n = ada_w.shape
    tn = 512
    return pl.pallas_call(
        _ada_body,
        grid=(depth, n // tn),
        in_specs=[
            pl.BlockSpec((8, d), lambda l, j: (0, 0)),
            pl.BlockSpec((1, d, tn), lambda l, j: (l, 0, j)),
            pl.BlockSpec((1, 1, tn), lambda l, j: (l, 0, j)),
        ],
        out_specs=pl.BlockSpec((1, 8, tn), lambda l, j: (l, 0, j)),
        out_shape=jax.ShapeDtypeStruct((depth, 8, n), F32),
        compiler_params=_cparams(("parallel", "parallel")),
        name="ada",
    )(cond8, ada_w, ada_b.reshape(depth, 1, n))


COL_CHUNK = 512


def _proj_body(x_ref, g_ref, sc_ref, sh_ref, w_ref, *o_refs, splits):
    h = _modnorm(x_ref[0], g_ref[...], sc_ref[0], sh_ref[0]).astype(BF16)
    off = 0
    for o_ref, n in zip(o_refs, splits):
        for c0 in range(0, n, COL_CHUNK):
            c1 = min(n, c0 + COL_CHUNK)
            o_ref[0, :, c0:c1] = jnp.dot(h, w_ref[:, off + c0:off + c1], preferred_element_type=F32)
        off += n


def _proj(x, g, sc1, sh, w, splits, tm):
    b, t, d = x.shape
    tm = min(tm, t)
    n = w.shape[1]
    vec = pl.BlockSpec((1, 1, d), lambda bi, i: (bi, 0, 0))
    return pl.pallas_call(
        functools.partial(_proj_body, splits=splits),
        grid=(b, t // tm),
        in_specs=[
            pl.BlockSpec((1, tm, d), lambda bi, i: (bi, i, 0)),
            pl.BlockSpec((1, d), lambda bi, i: (0, 0)),
            vec, vec,
            pl.BlockSpec((d, n), lambda bi, i: (0, 0)),
        ],
        out_specs=[pl.BlockSpec((1, tm, s), lambda bi, i: (bi, i, 0)) for s in splits],
        out_shape=[jax.ShapeDtypeStruct((b, t, s), F32) for s in splits],
        compiler_params=_cparams(("parallel", "parallel")),
        name="proj",
    )(x, g, sc1, sh, w)


HALO = 16


def _proj_shift_body(x_ref, xp_ref, xn_ref, g_ref, sc_ref, sh_ref, w_ref, sw_ref, z_ref, gate_ref,
                     *, tm, n_conv):
    i = pl.program_id(1)
    last = pl.num_programs(1) - 1
    g, sc1, sh = g_ref[...], sc_ref[0], sh_ref[0]
    h = _modnorm(x_ref[0], g, sc1, sh)
    hp = _modnorm(xp_ref[0], g, sc1, sh) * (i > 0).astype(F32)
    hn = _modnorm(xn_ref[0], g, sc1, sh) * (i < last).astype(F32)
    hb = jnp.concatenate([hp, h, hn], axis=0).astype(BF16)
    rows = tm + 2 * HALO
    for c0 in range(0, n_conv, COL_CHUNK):
        c1 = min(n_conv, c0 + COL_CHUNK)
        u = jnp.dot(hb, w_ref[:, c0:c1], preferred_element_type=F32)
        up = pltpu.roll(u, 1, 0)[HALO:HALO + tm]
        un = pltpu.roll(u, rows - 1, 0)[HALO:HALO + tm]
        um = u[HALO:HALO + tm]
        z_ref[0, :, c0:c1] = (sw_ref[0:1, c0:c1] * up + sw_ref[1:2, c0:c1] * um
                              + sw_ref[2:3, c0:c1] * un)
    hc = hb[HALO:HALO + tm]
    n_all = w_ref.shape[1]
    for c0 in range(n_conv, n_all, COL_CHUNK):
        c1 = min(n_all, c0 + COL_CHUNK)
        gate_ref[0, :, c0 - n_conv:c1 - n_conv] = jnp.dot(hc, w_ref[:, c0:c1],
                                                          preferred_element_type=F32)


def _proj_shift(x, g, sc1, sh, w, sw, n_conv, tm):
    b, t, d = x.shape
    tm = min(tm, t)
    n = w.shape[1]
    hb = tm // HALO
    nhb = t // HALO
    vec = pl.BlockSpec((1, 1, d), lambda bi, i: (bi, 0, 0))
    return pl.pallas_call(
        functools.partial(_proj_shift_body, tm=tm, n_conv=n_conv),
        grid=(b, t // tm),
        in_specs=[
            pl.BlockSpec((1, tm, d), lambda bi, i: (bi, i, 0)),
            pl.BlockSpec((1, HALO, d), lambda bi, i: (bi, jnp.maximum(i * hb - 1, 0), 0)),
            pl.BlockSpec((1, HALO, d), lambda bi, i: (bi, jnp.minimum((i + 1) * hb, nhb - 1), 0)),
            pl.BlockSpec((1, d), lambda bi, i: (0, 0)),
            vec, vec,
            pl.BlockSpec((d, n), lambda bi, i: (0, 0)),
            pl.BlockSpec((3, n_conv), lambda bi, i: (0, 0)),
        ],
        out_specs=[pl.BlockSpec((1, tm, n_conv), lambda bi, i: (bi, i, 0)),
                   pl.BlockSpec((1, tm, n - n_conv), lambda bi, i: (bi, i, 0))],
        out_shape=[jax.ShapeDtypeStruct((b, t, n_conv), F32),
                   jax.ShapeDtypeStruct((b, t, n - n_conv), F32)],
        compiler_params=_cparams(("parallel", "parallel")),
        name="proj_shift",
    )(x, x, x, g, sc1, sh, w, sw)


def _rms(x, g):
    return x * lax.rsqrt(jnp.mean(x * x, axis=-1, keepdims=True) + NORM_EPS) * g


def _qkv_body(ua_ref, kvn_ref, qn_ref, wk_ref, wv_ref, wq_ref, kg_ref, qg_ref,
              cos_ref, sa_ref, sb_ref, q_ref, k_ref, v_ref, *, rope):
    ua = ua_ref[0]
    ckv = _rms(ua[:, :KV_LORA], kvn_ref[...]).astype(BF16)
    kr = ua[:, KV_LORA:KV_LORA + LANE]
    cq = _rms(ua[:, KV_LORA + LANE:], qn_ref[...]).astype(BF16)
    kn = jnp.dot(ckv, wk_ref[...], preferred_element_type=F32)
    v_ref[0] = jnp.dot(ckv, wv_ref[...], preferred_element_type=F32).astype(BF16)
    qq = jnp.dot(cq, wq_ref[...], preferred_element_type=F32)
    pe = pltpu.roll(kr, QK_NOPE, 1)
    kg, qg = kg_ref[...], qg_ref[...]
    inv_n = 1.0 / QK_HEAD

    def finish(xh, gain):
        xh = xh * lax.rsqrt(jnp.sum(xh * xh, axis=-1, keepdims=True) * inv_n + NORM_EPS) * gain
        if rope:
            xh = (xh * cos_ref[...] + pltpu.roll(xh, LANE - ROPE_FREQS, 1) * sa_ref[...]
                  + pltpu.roll(xh, ROPE_FREQS, 1) * sb_ref[...])
        return xh.astype(BF16)

    for h in range(MLA_HEADS):
        sl = slice(h * HEAD_SLOT, (h + 1) * HEAD_SLOT)
        k_ref[0, h] = finish(kn[:, sl] + pe, kg)
        q_ref[0, h] = finish(qq[:, sl], qg)


def _qkv(ua, kvn, qn, wk, wv, wq, kg, qg, tabs, rope, tm):
    b, t, wa = ua.shape
    tm = min(tm, t)
    full = lambda a: pl.BlockSpec(a.shape, lambda bi, i: (0,) * a.ndim)
    tab = pl.BlockSpec((tm, LANE), lambda bi, i: (i, 0))
    return pl.pallas_call(
        functools.partial(_qkv_body, rope=rope),
        grid=(b, t // tm),
        in_specs=[pl.BlockSpec((1, tm, wa), lambda bi, i: (bi, i, 0)),
                  full(kvn), full(qn), full(wk), full(wv), full(wq), full(kg), full(qg),
                  tab, tab, tab],
        out_specs=[pl.BlockSpec((1, MLA_HEADS, tm, HEAD_SLOT), lambda bi, i: (bi, 0, i, 0)),
                   pl.BlockSpec((1, MLA_HEADS, tm, HEAD_SLOT), lambda bi, i: (bi, 0, i, 0)),
                   pl.BlockSpec((1, tm, MLA_HEADS * V_HEAD), lambda bi, i: (bi, i, 0))],
        out_shape=[jax.ShapeDtypeStruct((b, MLA_HEADS, t, HEAD_SLOT), BF16),
                   jax.ShapeDtypeStruct((b, MLA_HEADS, t, HEAD_SLOT), BF16),
                   jax.ShapeDtypeStruct((b, t, MLA_HEADS * V_HEAD), BF16)],
        compiler_params=_cparams(("parallel", "parallel")),
        name="qkv",
    )(ua, kvn, qn, wk, wv, wq, kg, qg, *tabs)


def _attn_body(q_ref, k_ref, v_ref, o_ref, m_ref, l_ref, acc_ref):
    j = pl.program_id(3)

    @pl.when(j == 0)
    def _():
        m_ref[...] = jnp.full(m_ref.shape, -jnp.inf, F32)
        l_ref[...] = jnp.zeros(l_ref.shape, F32)
        acc_ref[...] = jnp.zeros(acc_ref.shape, F32)

    v = v_ref[0]
    for hh in range(2):
        s = lax.dot_general(q_ref[0, hh], k_ref[0, hh], NT, preferred_element_type=F32)
        m_prev = m_ref[hh]
        m_new = jnp.maximum(m_prev, jnp.max(s, axis=-1, keepdims=True))
        p = jnp.exp(s - m_new)
        alpha = jnp.exp(m_prev - m_new)
        l_ref[hh] = alpha * l_ref[hh] + jnp.sum(p, axis=-1, keepdims=True)
        acc_ref[hh] = alpha * acc_ref[hh] + jnp.dot(p.astype(BF16), v, preferred_element_type=F32)
        m_ref[hh] = m_new

    @pl.when(j == pl.num_programs(3) - 1)
    def _():
        lane = _iota2(acc_ref.shape[1:], 1)
        o_ref[0] = jnp.where(lane < V_HEAD, acc_ref[0] / l_ref[0], acc_ref[1] / l_ref[1])


def _attention(q, k, v, bq, bk):
    b, h, s, e = q.shape
    sk = k.shape[2]
    bq, bk = min(bq, s), min(bk, sk)
    return pl.pallas_call(
        _attn_body,
        grid=(b, h // 2, s // bq, sk // bk),
        in_specs=[pl.BlockSpec((1, 2, bq, e), lambda bi, p, i, j: (bi, p, i, 0)),
                  pl.BlockSpec((1, 2, bk, e), lambda bi, p, i, j: (bi, p, j, 0)),
                  pl.BlockSpec((1, bk, 2 * V_HEAD), lambda bi, p, i, j: (bi, j, p))],
        out_specs=pl.BlockSpec((1, bq, 2 * V_HEAD), lambda bi, p, i, j: (bi, i, p)),
        out_shape=jax.ShapeDtypeStruct((b, s, h * V_HEAD), F32),
        scratch_shapes=[pltpu.VMEM((2, bq, 1), F32), pltpu.VMEM((2, bq, 1), F32),
                        pltpu.VMEM((2, bq, 2 * V_HEAD), F32)],
        compiler_params=_cparams(("parallel", "parallel", "parallel", "arbitrary")),
        name="attention",
    )(q, k, v)


def _dft_mats(n):
    idx = np.arange(n)
    ang = 2.0 * np.pi * ((idx[:, None] * idx[None, :]) % n) / n
    return np.cos(ang), np.sin(ang)


def _hilo(a):
    a = jnp.asarray(a, F32)
    hi = a.astype(BF16)
    return hi, (a - hi.astype(F32)).astype(BF16)


def _mm3c(ah, al, b, dn=NN):
    bh, bl = _split(b)
    d = lambda x, y: lax.dot_general(x, y, dn, preferred_element_type=F32)
    return d(ah, bh) + d(al, bh) + d(ah, bl)


def _four1_body(x_ref, ch_ref, cl_ref, o_ref):
    x = x_ref[0]
    xh, xl = _split(x)
    d = lambda a, b: jnp.dot(a, b, preferred_element_type=F32)
    z = d(xh, ch_ref[...]) + d(xl, ch_ref[...]) + d(xh, cl_ref[...])
    o_ref[0, 0, 0] = z[:, :FNET_GROUP_DIM]
    o_ref[0, 0, 1] = z[:, FNET_GROUP_DIM:]


def _four2_body(z_ref, wh_ref, wl_ref, tc_ref, ts_ref, o_ref):
    r = tc_ref.shape[0]
    y = _mm3c(wh_ref[...], wl_ref[...], z_ref[0, 0])
    yc, ys = y[:r], y[r:]
    tc, ts = tc_ref[...], ts_ref[...]
    o_ref[0, 0, :r] = yc * tc - ys * ts
    o_ref[0, 0, r:] = yc * ts + ys * tc


def _four3_body(y_ref, ch_ref, cl_ref, sh_ref, sl_ref, wf_ref, o_ref, *, krt, scale):
    wf = wf_ref[0]
    for j in range(krt):
        yc = y_ref[0, 0, 0, j * GRID_W:(j + 1) * GRID_W, :]
        ys = y_ref[0, 0, 1, j * GRID_W:(j + 1) * GRID_W, :]
        res = (_mm3c(ch_ref[...], cl_ref[...], yc) - _mm3c(sh_ref[...], sl_ref[...], ys)) * scale
        o_ref[0, 0, j] = _mm3(res, wf)


def _fourier_latent(xf, w_fnet):
    b, t, _ = xf.shape
    g, gd = FNET_GROUPS, FNET_GROUP_DIM
    r = t // GRID_W
    cc, sc = _dft_mats(gd)
    ch, cl = _hilo(np.concatenate([cc, sc], axis=1))
    tm = min(512, t)
    z = pl.pallas_call(
        _four1_body,
        grid=(b, g, t // tm),
        in_specs=[pl.BlockSpec((1, tm, gd), lambda bi, gi, i: (bi, i, gi)),
                  pl.BlockSpec((gd, 2 * gd), lambda bi, gi, i: (0, 0)),
                  pl.BlockSpec((gd, 2 * gd), lambda bi, gi, i: (0, 0))],
        out_specs=pl.BlockSpec((1, 1, 2, tm, gd), lambda bi, gi, i: (bi, gi, 0, i, 0)),
        out_shape=jax.ShapeDtypeStruct((b, g, 2, t, gd), F32),
        compiler_params=_cparams(("parallel", "parallel", "parallel")),
        name="fourier_channels",
    )(xf, ch, cl)
    wide = GRID_W * gd
    zv = z.reshape(b, g, 2 * r, wide)
    cr, sr = _dft_mats(r)
    wh, wl = _hilo(np.block([[cr, -sr], [sr, cr]]))
    kr_i, c_i = np.arange(r)[:, None], np.arange(GRID_W)[None, :]
    ang = 2.0 * np.pi * ((kr_i * c_i) % t) / t
    twc = jnp.repeat(jnp.asarray(np.cos(ang), F32), gd, axis=1)
    tws = jnp.repeat(jnp.asarray(np.sin(ang), F32), gd, axis=1)
    tl = min(2048, wide)
    y2 = pl.pallas_call(
        _four2_body,
        grid=(b, g, wide // tl),
        in_specs=[pl.BlockSpec((1, 1, 2 * r, tl), lambda bi, gi, l: (bi, gi, 0, l)),
                  pl.BlockSpec((2 * r, 2 * r), lambda bi, gi, l: (0, 0)),
                  pl.BlockSpec((2 * r, 2 * r), lambda bi, gi, l: (0, 0)),
                  pl.BlockSpec((r, tl), lambda bi, gi, l: (0, l)),
                  pl.BlockSpec((r, tl), lambda bi, gi, l: (0, l))],
        out_specs=pl.BlockSpec((1, 1, 2 * r, tl), lambda bi, gi, l: (bi, gi, 0, l)),
        out_shape=jax.ShapeDtypeStruct((b, g, 2 * r, wide), F32),
        compiler_params=_cparams(("parallel", "parallel", "parallel")),
        name="fourier_rows",
    )(zv, wh, wl, twc, tws)
    y2v = y2.reshape(b, g, 2, r * GRID_W, gd)
    c64, s64 = _dft_mats(GRID_W)
    c64h, c64l = _hilo(c64)
    s64h, s64l = _hilo(s64)
    krt = min(8, r)
    small = pl.BlockSpec((GRID_W, GRID_W), lambda bi, gi, i: (0, 0))
    fo = pl.pallas_call(
        functools.partial(_four3_body, krt=krt, scale=1.0 / math.sqrt(t * gd)),
        grid=(b, g, r // krt),
        in_specs=[pl.BlockSpec((1, 1, 2, krt * GRID_W, gd), lambda bi, gi, i: (bi, gi, 0, i, 0)),
                  small, small, small, small,
                  pl.BlockSpec((1, gd, gd), lambda bi, gi, i: (gi, 0, 0))],
        out_specs=pl.BlockSpec((1, 1, krt, GRID_W, gd), lambda bi, gi, i: (bi, gi, i, 0, 0)),
        out_shape=jax.ShapeDtypeStruct((b, g, r, GRID_W, gd), F32),
        compiler_params=_cparams(("parallel", "parallel", "parallel")),
        name="fourier_cols",
    )(y2v, c64h, c64l, s64h, s64l, w_fnet)
    return fo.transpose(0, 3, 2, 1, 4).reshape(b, t, g * gd)


def _four_dense_body(x_ref, ch_ref, cl_ref, th_ref, tl_ref, sh_ref, sl_ref, wf_ref, o_ref, *, scale):
    x = x_ref[0]
    xh, xl = _split(x)
    d = lambda a, b: jnp.dot(a, b, preferred_element_type=F32)
    z = d(xh, ch_ref[...]) + d(xl, ch_ref[...]) + d(xh, cl_ref[...])
    zc, zs = z[:, :FNET_GROUP_DIM], z[:, FNET_GROUP_DIM:]
    f = (_mm3c(th_ref[...], tl_ref[...], zc) - _mm3c(sh_ref[...], sl_ref[...], zs)) * scale
    o_ref[0] = _mm3(f, wf_ref[0])


def _fourier_dense(xf, w_fnet):
    b, t, _ = xf.shape
    g, gd = FNET_GROUPS, FNET_GROUP_DIM
    cc, sc = _dft_mats(gd)
    ch, cl = _hilo(np.concatenate([cc, sc], axis=1))
    ct, st = _dft_mats(t)
    cth, ctl = _hilo(ct)
    sth, stl = _hilo(st)
    sq = pl.BlockSpec((t, t), lambda bi, gi: (0, 0))
    cs = pl.BlockSpec((gd, 2 * gd), lambda bi, gi: (0, 0))
    return pl.pallas_call(
        functools.partial(_four_dense_body, scale=1.0 / math.sqrt(t * gd)),
        grid=(b, g),
        in_specs=[pl.BlockSpec((1, t, gd), lambda bi, gi: (bi, 0, gi)), cs, cs, sq, sq, sq, sq,
                  pl.BlockSpec((1, gd, gd), lambda bi, gi: (gi, 0, 0))],
        out_specs=pl.BlockSpec((1, t, gd), lambda bi, gi: (bi, 0, gi)),
        out_shape=jax.ShapeDtypeStruct((b, t, g * gd), F32),
        compiler_params=_cparams(("parallel", "parallel")),
        name="fourier_dense",
    )(xf, ch, cl, cth, ctl, sth, stl, w_fnet)


def _merge_body(o_ref, f_ref, gate_ref, x_ref, gl_ref, w_ref, out_ref):
    gt = gate_ref[0]
    mix = jnp.concatenate([o_ref[0], f_ref[0]], axis=-1) * (gt * _sigmoid(gt))
    y = jnp.dot(mix.astype(BF16), w_ref[...], preferred_element_type=F32)
    out_ref[0] = x_ref[0] + gl_ref[0] * y


def _merge(o, f, gate, x, gl, w, tm):
    b, t, d = x.shape
    tm = min(tm, t)
    half = o.shape[2]
    tok = lambda n: pl.BlockSpec((1, tm, n), lambda bi, i: (bi, i, 0))
    return pl.pallas_call(
        _merge_body,
        grid=(b, t // tm),
        in_specs=[tok(half), tok(half), tok(d), tok(d),
                  pl.BlockSpec((1, 1, d), lambda bi, i: (bi, 0, 0)),
                  pl.BlockSpec(w.shape, lambda bi, i: (0, 0))],
        out_specs=tok(d),
        out_shape=jax.ShapeDtypeStruct((b, t, d), F32),
        compiler_params=_cparams(("parallel", "parallel")),
        name="merge",
    )(o, f, gate, x, gl, w)


EXP_M05 = math.exp(-0.5)


def _rwkv_local_body(zk_ref, zv_ref, zr_ref, zwa_ref, w0_ref, w2_ref, a0_ref, a2_ref,
                     kk_ref, ka_ref, rk_ref, m_ref, g_ref, qt_ref, yl_ref, bn_ref):
    c = CHUNK
    k, v, r = zk_ref[0], zv_ref[0], zr_ref[0]
    zwa = zwa_ref[0]
    wraw = _mm(jnp.tanh(zwa[:, :LANE]), w2_ref[0]) + w0_ref[0]
    araw = _mm(zwa[:, LANE:], a2_ref[0]) + a0_ref[0]
    logw = -EXP_M05 * _sigmoid(wraw)
    a_all = _sigmoid(araw)

    lane = _iota2((1, LANE), 1)
    m0 = (lane < RWKV_HEAD).astype(F32)
    m1 = 1.0 - m0
    r2 = _iota2((LANE, LANE), 0)
    c2 = _iota2((LANE, LANE), 1)
    same = (r2 // RWKV_HEAD) == (c2 // RWKV_HEAD)
    ones_bd = same.astype(F32)
    eye = r2 == c2
    rc = _iota2((c, c), 0)
    cc = _iota2((c, c), 1)

    kk0 = k * kk_ref[...]
    ss = _mm2r(kk0 * kk0, ones_bd)
    kk = kk0 / jnp.maximum(jnp.sqrt(ss), 1e-12)

    stack = lambda x: jnp.concatenate([x * m0, x * m1], axis=0)
    fold = lambda x: x[:c] + x[c:]
    vs = stack(v)
    yl_acc = jnp.zeros((c, LANE), F32)
    bn_acc = jnp.zeros((c, LANE), F32)
    for d in range(2):
        sl = slice(d * LANE, (d + 1) * LANE)
        lw, ad = logw[:, sl], a_all[:, sl]
        kd = k * (1.0 + (ad - 1.0) * ka_ref[...])
        bb = kk * ad
        bn_acc = bn_acc + _mm2r(r * kd * rk_ref[...], ones_bd) * v
        if d == 0:
            tri = (cc <= rc).astype(F32)
            strict, incl = same & (c2 < r2), same & (c2 <= r2)
        else:
            tri = (cc >= rc).astype(F32)
            strict, incl = same & (c2 > r2), same & (c2 >= r2)
        lh, ll = _split(lw)
        lll = (lw - lh.astype(F32) - ll.astype(F32)).astype(BF16)
        trib = tri.astype(BF16)
        dd = lambda y: jnp.dot(trib, y, preferred_element_type=F32)
        lc = dd(lh) + dd(ll) + dd(lll)
        ltot = lc[c - 1:c] if d == 0 else lc[0:1]
        e_in = jnp.exp(lc)
        e_prev = jnp.exp(lc - lw)
        e_inv = jnp.exp(-lc)
        e_end = jnp.exp(ltot - lc)
        kkd_s, rd_s = stack(kk * e_prev), stack(r * e_in)
        amat = _mm3(jnp.concatenate([kkd_s, rd_s], axis=0),
                    jnp.concatenate([stack(bb * e_inv), stack(kd * e_inv)], axis=0), NT)
        a_kb = jnp.where(strict, amat[:LANE, :LANE], 0.0)
        a_kk = jnp.where(strict, amat[:LANE, LANE:], 0.0)
        aq_b = jnp.where(incl, amat[LANE:, :LANE], 0.0)
        aq_k = jnp.where(incl, amat[LANE:, LANE:], 0.0)
        av = _mm3(jnp.concatenate([a_kk, aq_k], axis=0), vs)
        x = jnp.concatenate([kkd_s, av[:LANE]], axis=1)
        p = -a_kb
        for it in range(6):
            x = x + _mm3(p, x)
            if it < 5:
                p = _mm3(p, p)
        qy = jnp.concatenate([rd_s, av[LANE:]], axis=1) - _mm3(aq_b, x)
        bx = _mm3(stack(bb * e_end), x, TN)
        kv = _mm3(stack(kd * e_end), vs, TN)
        mbd = jnp.where(eye, jnp.exp(ltot), 0.0) - bx[:, :LANE]
        gbd = kv - bx[:, LANE:]
        m_ref[0, 0, d, 0] = fold(mbd)
        g_ref[0, 0, d, 0] = fold(gbd)
        qt_ref[0, d] = fold(qy[:, :LANE])
        yl_acc = yl_acc + fold(qy[:, LANE:])
    yl_ref[0] = yl_acc
    bn_ref[0] = bn_acc


def _rwkv_local(z, w0p, w2p, a0p, a2p, k_k, k_a, r_k):
    b, t, _ = z.shape
    w = k_k.shape[1]
    npair = w // LANE
    nc = t // CHUNK
    tokc = lambda base: pl.BlockSpec((1, CHUNK, LANE), lambda bi, ci, p: (bi, ci, base + p))
    perp3 = lambda n: pl.BlockSpec((1, n, 2 * LANE), lambda bi, ci, p: (p, 0, 0))
    vecp = pl.BlockSpec((1, LANE), lambda bi, ci, p: (0, p))
    mat = pl.BlockSpec((1, 1, 2, 1, CHUNK, LANE), lambda bi, ci, p: (bi, ci, 0, p, 0, 0))
    return pl.pallas_call(
        _rwkv_local_body,
        grid=(b, nc, npair),
        in_specs=[tokc(0), tokc(npair), tokc(2 * npair),
                  pl.BlockSpec((1, CHUNK, 2 * LANE), lambda bi, ci, p: (bi, ci, 3 * npair // 2)),
                  perp3(1), perp3(LANE), perp3(1), perp3(LANE), vecp, vecp, vecp],
        out_specs=[mat, mat,
                   pl.BlockSpec((1, 2, CHUNK, LANE), lambda bi, ci, p: (bi, 0, ci, p)),
                   pl.BlockSpec((1, CHUNK, LANE), lambda bi, ci, p: (bi, ci, p)),
                   pl.BlockSpec((1, CHUNK, LANE), lambda bi, ci, p: (bi, ci, p))],
        out_shape=[jax.ShapeDtypeStruct((b, nc, 2, npair, CHUNK, LANE), F32),
                   jax.ShapeDtypeStruct((b, nc, 2, npair, CHUNK, LANE), F32),
                   jax.ShapeDtypeStruct((b, 2, t, w), F32),
                   jax.ShapeDtypeStruct((b, t, w), F32),
                   jax.ShapeDtypeStruct((b, t, w), F32)],
        compiler_params=_cparams(("parallel", "parallel", "parallel")),
        name="rwkv_local",
    )(z, z, z, z, w0p, w2p, a0p, a2p, k_k, k_a, r_k)


def _rwkv_scan_body(m0_ref, g0_ref, q0_ref, m1_ref, g1_ref, q1_ref, h0_ref,
                    y0_ref, y1_ref, hfin_ref, h_scr, *, npair):
    ci = pl.program_id(1)

    @pl.when(ci == 0)
    def _():
        h_scr[...] = h0_ref[0]

    lane = _iota2((1, LANE), 1)
    m0 = (lane < RWKV_HEAD).astype(F32)
    m1 = 1.0 - m0
    expand = lambda x: jnp.concatenate([x * m0, x * m1], axis=0)
    for d, (m_ref, g_ref, q_ref, y_ref) in enumerate(((m0_ref, g0_ref, q0_ref, y0_ref),
                                                      (m1_ref, g1_ref, q1_ref, y1_ref))):
        for p in range(npair):
            sl = slice(p * LANE, (p + 1) * LANE)
            hst = h_scr[d, p]
            y_ref[0, :, sl] = _mm3(q_ref[0, 0, :, sl], hst)
            h_scr[d, p] = _mm3(expand(m_ref[0, 0, 0, p]), hst) + expand(g_ref[0, 0, 0, p])

    @pl.when(ci == pl.num_programs(1) - 1)
    def _():
        hfin_ref[0] = h_scr[...]


def _rwkv_scan(mm, gg, qt, h0):
    b, nc, _, npair, _, _ = mm.shape
    t, w = qt.shape[2], qt.shape[3]
    fwd = lambda bi, ci: (bi, ci, 0, 0, 0, 0)
    rev = lambda bi, ci: (bi, nc - 1 - ci, 1, 0, 0, 0)
    mblk = (1, 1, 1, npair, CHUNK, LANE)
    hspec = pl.BlockSpec((1, 2, npair, LANE, LANE), lambda bi, ci: (bi, 0, 0, 0, 0))
    return pl.pallas_call(
        functools.partial(_rwkv_scan_body, npair=npair),
        grid=(b, nc),
        in_specs=[pl.BlockSpec(mblk, fwd), pl.BlockSpec(mblk, fwd),
                  pl.BlockSpec((1, 1, CHUNK, w), lambda bi, ci: (bi, 0, ci, 0)),
                  pl.BlockSpec(mblk, rev), pl.BlockSpec(mblk, rev),
                  pl.BlockSpec((1, 1, CHUNK, w), lambda bi, ci: (bi, 1, nc - 1 - ci, 0)),
                  hspec],
        out_specs=[pl.BlockSpec((1, CHUNK, w), lambda bi, ci: (bi, ci, 0)),
                   pl.BlockSpec((1, CHUNK, w), lambda bi, ci: (bi, nc - 1 - ci, 0)),
                   hspec],
        out_shape=[jax.ShapeDtypeStruct((b, t, w), F32), jax.ShapeDtypeStruct((b, t, w), F32),
                   jax.ShapeDtypeStruct(h0.shape, F32)],
        scratch_shapes=[pltpu.VMEM((2, npair, LANE, LANE), F32)],
        compiler_params=_cparams(("parallel", "arbitrary")),
        name="rwkv_scan",
    )(mm, gg, qt, mm, gg, qt, h0)


def _rwkv_out_body(y0_ref, y1_ref, yl_ref, bn_ref, gate_ref, x_ref, gl_ref, gnw_ref, gnb_ref,
                   w_ref, o_ref):
    y = y0_ref[0] + y1_ref[0] + yl_ref[0]
    r2 = _iota2((LANE, LANE), 0)
    c2 = _iota2((LANE, LANE), 1)
    avg = ((r2 // RWKV_HEAD) == (c2 // RWKV_HEAD)).astype(F32) * (1.0 / RWKV_HEAD)
    parts = []
    for p in range(y.shape[1] // LANE):
        yp = y[:, p * LANE:(p + 1) * LANE]
        dl = yp - _mm2r(yp, avg)
        var = _mm2r(dl * dl, avg)
        parts.append(dl * lax.rsqrt(var + GN_EPS))
    yn = jnp.concatenate(parts, axis=1)
    gt = gate_ref[0]
    act = (yn * gnw_ref[...] + gnb_ref[...] + bn_ref[0]) * (gt * _sigmoid(gt))
    out = jnp.dot(act.astype(BF16), w_ref[...], preferred_element_type=F32)
    o_ref[0] = x_ref[0] + gl_ref[0] * out


def _rwkv_out(y0, y1, yl, bn, gate, x, gl, gnw, gnb, w, tm):
    b, t, d = x.shape
    tm = min(tm, t)
    wd = y0.shape[2]
    tok = lambda n: pl.BlockSpec((1, tm, n), lambda bi, i: (bi, i, 0))
    return pl.pallas_call(
        _rwkv_out_body,
        grid=(b, t // tm),
        in_specs=[tok(wd), tok(wd), tok(wd), tok(wd), tok(wd), tok(d),
                  pl.BlockSpec((1, 1, d), lambda bi, i: (bi, 0, 0)),
                  pl.BlockSpec((1, wd), lambda bi, i: (0, 0)),
                  pl.BlockSpec((1, wd), lambda bi, i: (0, 0)),
                  pl.BlockSpec(w.shape, lambda bi, i: (0, 0))],
        out_specs=tok(d),
        out_shape=jax.ShapeDtypeStruct((b, t, d), F32),
        compiler_params=_cparams(("parallel", "parallel")),
        name="rwkv_out",
    )(y0, y1, yl, bn, gate, x, gl, gnw, gnb, w)


def _rope_tables(t):
    rows = t // GRID_W
    row = jnp.repeat(jnp.arange(rows, dtype=F32), GRID_W)
    col = jnp.tile(jnp.arange(GRID_W, dtype=F32), rows)
    inv = 1.0 / (ROPE_BASE ** (jnp.arange(ROPE_FREQS, dtype=F32) / ROPE_FREQS))
    ang = jnp.stack([row[:, None] * inv, col[:, None] * inv], axis=1)
    cos, sin = jnp.cos(ang), jnp.sin(ang)
    zeros = jnp.zeros_like(sin)
    ones_lo = jnp.ones((t, QK_NOPE), F32)
    pad_hi = HEAD_SLOT - QK_HEAD
    cos_t = jnp.concatenate([ones_lo, jnp.concatenate([cos, cos], axis=2).reshape(t, QK_ROPE),
                             jnp.ones((t, pad_hi), F32)], axis=1)
    sa = jnp.concatenate([jnp.zeros((t, QK_NOPE), F32),
                          jnp.concatenate([-sin, zeros], axis=2).reshape(t, QK_ROPE),
                          jnp.zeros((t, pad_hi), F32)], axis=1)
    sb = jnp.concatenate([jnp.zeros((t, QK_NOPE), F32),
                          jnp.concatenate([zeros, sin], axis=2).reshape(t, QK_ROPE),
                          jnp.zeros((t, pad_hi), F32)], axis=1)
    return cos_t, sa, sb


def _even_layer(x, ctx, mod_l, mod_c, need_ctx, g, w_in, kv_norm, q_norm, w_uq, w_ukv,
                q_head_norm, k_head_norm, w_fnet, w_out):
    b, s, d = x.shape
    tc = ctx.shape[1]
    e_q0 = KV_LORA + QK_ROPE
    e_f0 = e_q0 + Q_LORA
    e_g0 = e_f0 + FNET_GROUPS * FNET_GROUP_DIM
    w_p = jnp.concatenate([w_in[:, e_g0:], w_in[:, e_f0:e_g0], w_in[:, :e_q0],
                           jnp.zeros((d, LANE - QK_ROPE), F32), w_in[:, e_q0:e_f0]],
                          axis=1).astype(BF16)
    splits = (d, FNET_GROUPS * FNET_GROUP_DIM, KV_LORA + LANE + Q_LORA)
    kvw = w_ukv.reshape(KV_LORA, MLA_HEADS, QK_NOPE + V_HEAD)
    wk = jnp.pad(kvw[:, :, :QK_NOPE], ((0, 0), (0, 0), (0, HEAD_SLOT - QK_NOPE)))
    wk = wk.reshape(KV_LORA, MLA_HEADS * HEAD_SLOT).astype(BF16)
    wv = kvw[:, :, QK_NOPE:].reshape(KV_LORA, MLA_HEADS * V_HEAD).astype(BF16)
    wq = jnp.pad(w_uq.reshape(Q_LORA, MLA_HEADS, QK_HEAD), ((0, 0), (0, 0), (0, HEAD_SLOT - QK_HEAD)))
    wq = wq.reshape(Q_LORA, MLA_HEADS * HEAD_SLOT).astype(BF16)
    kg = jnp.pad(k_head_norm, (0, HEAD_SLOT - QK_HEAD)).reshape(1, HEAD_SLOT)
    qg = (jnp.pad(q_head_norm, (0, HEAD_SLOT - QK_HEAD)) * (QK_HEAD ** -0.5)).reshape(1, HEAD_SLOT)
    kvn, qn = kv_norm.reshape(1, -1), q_norm.reshape(1, -1)
    g2 = g.reshape(1, d)

    gate_l, four_l, ua_l = _proj(x, g2, mod_l[1], mod_l[0], w_p, splits, 512)
    gate_c, four_c, ua_c = _proj(ctx, g2, mod_c[1], mod_c[0], w_p, splits, 512)
    q_l, k_l, v_l = _qkv(ua_l, kvn, qn, wk, wv, wq, kg, qg, _rope_tables(s), True, 256)
    dummy = jnp.zeros((tc, LANE), F32)
    q_c, k_c, v_c = _qkv(ua_c, kvn, qn, wk, wv, wq, kg, qg, (dummy, dummy, dummy), False, 256)
    k_all = jnp.concatenate([k_l, k_c], axis=2)
    v_all = jnp.concatenate([v_l, v_c], axis=1)
    sk = s + tc
    bk = 768 if sk % 768 == 0 else tc
    o_l = _attention(q_l, k_all, v_all, 512, bk)
    f_l = _fourier_latent(four_l, w_fnet)
    wo = w_out.astype(BF16)
    x_new = _merge(o_l, f_l, gate_l, x, mod_l[2], wo, 512)
    ctx_new = ctx
    if need_ctx:
        o_c = _attention(q_c, k_c, v_c, tc, tc)
        f_c = _fourier_dense(four_c, w_fnet)
        ctx_new = _merge(o_c, f_c, gate_c, ctx, mod_c[2], wo, 512)
    return x_new, ctx_new


def _odd_layer(x, ctx, mod_l, mod_c, need_ctx, g, w_in, shift_w, w0, w2, a0, a2, k_k, k_a, r_k,
               gn_w, gn_b, w_out):
    b, s, d = x.shape
    w = k_k.shape[0]
    npair = w // LANE
    o_wd0 = 2 * w
    o_r0 = o_wd0 + 2 * DECAY_LORA + 2 * AAA_LORA
    conv_ch = o_r0 + w
    perm = lambda m: jnp.concatenate([m[:, :o_wd0], m[:, o_r0:conv_ch], m[:, o_wd0:o_r0]], axis=1)
    w_p = jnp.concatenate([perm(w_in), w_in[:, conv_ch:]], axis=1).astype(BF16)
    sw = perm(shift_w)
    g2 = g.reshape(1, d)

    def pairs(vec2):
        return vec2.reshape(2, npair, LANE).transpose(1, 0, 2).reshape(npair, 1, 2 * LANE)

    def pair_mats(m):
        rr = m.shape[1]
        mp = m.reshape(2, rr, npair, LANE).transpose(2, 0, 1, 3)
        z = jnp.zeros_like(mp[:, 0])
        top = jnp.concatenate([mp[:, 0], z], axis=2)
        bot = jnp.concatenate([z, mp[:, 1]], axis=2)
        return jnp.concatenate([top, bot], axis=1).astype(BF16)

    w0p, a0p, w2p, a2p = pairs(w0), pairs(a0), pair_mats(w2), pair_mats(a2)
    kk2, ka2, rk2 = k_k.reshape(1, w), k_a.reshape(1, w), r_k.reshape(1, w)
    wo = w_out.astype(BF16)

    def mix(xin, mod, h0):
        z, gate = _proj_shift(xin, g2, mod[1], mod[0], w_p, sw, conv_ch, 256)
        mm, gg, qt, yl, bn = _rwkv_local(z, w0p, w2p, a0p, a2p, kk2, ka2, rk2)
        y0, y1, hfin = _rwkv_scan(mm, gg, qt, h0)
        return (y0, y1, yl, bn, gate), hfin

    h_zero = jnp.zeros((b, 2, npair, LANE, LANE), F32)
    parts_c, h_ctx = mix(ctx, mod_c, h_zero)
    parts_l, _ = mix(x, mod_l, h_ctx)
    x_new = _rwkv_out(*parts_l, x, mod_l[2], gn_w.reshape(1, w), gn_b.reshape(1, w), wo, 256)
    ctx_new = ctx
    if need_ctx:
        ctx_new = _rwkv_out(*parts_c, ctx, mod_c[2], gn_w.reshape(1, w), gn_b.reshape(1, w), wo, 256)
    return x_new, ctx_new


def kernel(x, c, ctx, c_ctx, ada_w, ada_b, norm_g, e_w_in, e_kv_norm, e_q_norm, e_w_uq, e_w_ukv,
           e_q_head_norm, e_k_head_norm, e_w_fnet, e_w_out, o_w_in, o_shift_w, o_w0, o_w2, o_a0,
           o_a2, o_k_k, o_k_a, o_r_k, o_gn_w, o_gn_b, o_w_out):
    b, s, d = x.shape
    depth = ada_w.shape[0]
    assert b + 1 <= 8
    cond8 = jnp.concatenate([c, c_ctx[None, :], jnp.zeros((8 - b - 1, d), F32)], axis=0)
    mod = _ada(cond8, ada_w, ada_b)
    for layer in range(depth):
        need_ctx = layer < depth - 1
        m = mod[layer]
        chunk = lambda rows, i: rows[:, None, i * d:(i + 1) * d]
        lat, cx = m[:b], jnp.broadcast_to(m[b:b + 1], (b, 3 * d))
        mod_l = (chunk(lat, 0), 1.0 + chunk(lat, 1), chunk(lat, 2))
        mod_c = (chunk(cx, 0), 1.0 + chunk(cx, 1), chunk(cx, 2))
        j = layer // 2
        if layer % 2 == 0:
            x, ctx = _even_layer(x, ctx, mod_l, mod_c, need_ctx, norm_g[layer], e_w_in[j],
                                 e_kv_norm[j], e_q_norm[j], e_w_uq[j], e_w_ukv[j],
                                 e_q_head_norm[j], e_k_head_norm[j], e_w_fnet[j], e_w_out[j])
        else:
            x, ctx = _odd_layer(x, ctx, mod_l, mod_c, need_ctx, norm_g[layer], o_w_in[j],
                                o_shift_w[j], o_w0[j], o_w2[j], o_a0[j], o_a2[j], o_k_k[j],
                                o_k_a[j], o_r_k[j].reshape(-1), o_gn_w[j], o_gn_b[j], o_w_out[j])
    return x
```

```python
import functools
import math

import numpy as np
import jax
import jax.numpy as jnp
from jax import lax
from jax.experimental import pallas as pl
from jax.experimental.pallas import tpu as pltpu

F32 = jnp.float32
BF16 = jnp.bfloat16

GRID_W = 64
NORM_EPS = 1e-6
MLA_HEADS = 8
QK_NOPE = 64
QK_ROPE = 32
QK_HEAD = QK_NOPE + QK_ROPE
V_HEAD = 64
Q_LORA = 384
KV_LORA = 256
ROPE_FREQS = QK_ROPE // 4
ROPE_BASE = 10000.0
FNET_GROUPS = 4
FNET_GROUP_DIM = 128
RWKV_HEAD = 64
DECAY_LORA = 64
AAA_LORA = 64
GN_EPS = 64e-5

LANE = 128
CHUNK = 64
HEAD_SLOT = 128
VMEM_LIMIT = 56 * 1024 * 1024

NN = (((1,), (0,)), ((), ()))
NT = (((1,), (1,)), ((), ()))
TN = (((0,), (0,)), ((), ()))


def _cparams(sem):
    return pltpu.CompilerParams(dimension_semantics=sem, vmem_limit_bytes=VMEM_LIMIT)


def _mm(a, b, dn=NN):
    return lax.dot_general(a.astype(BF16), b.astype(BF16), dn, preferred_element_type=F32)


def _split(a):
    hi = a.astype(BF16)
    lo = (a - hi.astype(F32)).astype(BF16)
    return hi, lo


def _mm3(a, b, dn=NN):
    ah, al = _split(a)
    bh, bl = _split(b)
    d = lambda x, y: lax.dot_general(x, y, dn, preferred_element_type=F32)
    return d(ah, bh) + d(al, bh) + d(ah, bl)


def _mm2r(a, b_exact):
    ah, al = _split(a)
    bb = b_exact.astype(BF16)
    d = lambda x: lax.dot_general(x, bb, NN, preferred_element_type=F32)
    return d(ah) + d(al)


def _mm2l(a_exact, b):
    bh, bl = _split(b)
    aa = a_exact.astype(BF16)
    d = lambda y: lax.dot_general(aa, y, NN, preferred_element_type=F32)
    return d(bh) + d(bl)


def _sigmoid(x):
    return 1.0 / (1.0 + jnp.exp(-x))


def _modnorm(x, g, sc1, sh):
    y = x * lax.rsqrt(jnp.mean(x * x, axis=-1, keepdims=True) + NORM_EPS)
    return (y * g) * sc1 + sh


def _iota2(shape, dim):
    return lax.broadcasted_iota(jnp.int32, shape, dim)


def _ada_body(c_ref, w_ref, b_ref, o_ref):
    c = c_ref[...]
    s = c * _sigmoid(c)
    o_ref[0] = _mm3(s, w_ref[0]) + b_ref[0]


def _ada(cond8, ada_w, ada_b):
    depth, d, n = ada_w.shape
    tn = 512
    return pl.pallas_call(
        _ada_body,
        grid=(depth, n // tn),
        in_specs=[
            pl.BlockSpec((8, d), lambda l, j: (0, 0)),
            pl.BlockSpec((1, d, tn), lambda l, j: (l, 0, j)),
            pl.BlockSpec((1, 1, tn), lambda l, j: (l, 0, j)),
        ],
        out_specs=pl.BlockSpec((1, 8, tn), lambda l, j: (l, 0, j)),
        out_shape=jax.ShapeDtypeStruct((depth, 8, n), F32),
        compiler_params=_cparams(("parallel", "parallel")),
        name="ada",
    )(cond8, ada_w, ada_b.reshape(depth, 1, n))


COL_CHUNK = 512


def _proj_body(x_ref, g_ref, sc_ref, sh_ref, w_ref, *o_refs, splits):
    h = _modnorm(x_ref[0], g_ref[...], sc_ref[0], sh_ref[0]).astype(BF16)
    off = 0
    for o_ref, n in zip(o_refs, splits):
        for c0 in range(0, n, COL_CHUNK):
            c1 = min(n, c0 + COL_CHUNK)
            o_ref[0, :, c0:c1] = jnp.dot(h, w_ref[:, off + c0:off + c1], preferred_element_type=F32)
        off += n


def _proj(x, g, sc1, sh, w, splits, tm):
    b, t, d = x.shape
    tm = min(tm, t)
    n = w.shape[1]
    vec = pl.BlockSpec((1, 1, d), lambda bi, i: (bi, 0, 0))
    return pl.pallas_call(
        functools.partial(_proj_body, splits=splits),
        grid=(b, t // tm),
        in_specs=[
            pl.BlockSpec((1, tm, d), lambda bi, i: (bi, i, 0)),
            pl.BlockSpec((1, d), lambda bi, i: (0, 0)),
            vec, vec,
            pl.BlockSpec((d, n), lambda bi, i: (0, 0)),
        ],
        out_specs=[pl.BlockSpec((1, tm, s), lambda bi, i: (bi, i, 0)) for s in splits],
        out_shape=[jax.ShapeDtypeStruct((b, t, s), F32) for s in splits],
        compiler_params=_cparams(("parallel", "parallel")),
        name="proj",
    )(x, g, sc1, sh, w)


HALO = 16


def _proj_shift_body(x_ref, xp_ref, xn_ref, g_ref, sc_ref, sh_ref, w_ref, sw_ref, z_ref, gate_ref,
                     *, tm, n_conv):
    i = pl.program_id(1)
    last = pl.num_programs(1) - 1
    g, sc1, sh = g_ref[...], sc_ref[0], sh_ref[0]
    h = _modnorm(x_ref[0], g, sc1, sh)
    hp = _modnorm(xp_ref[0], g, sc1, sh) * (i > 0).astype(F32)
    hn = _modnorm(xn_ref[0], g, sc1, sh) * (i < last).astype(F32)
    hb = jnp.concatenate([hp, h, hn], axis=0).astype(BF16)
    rows = tm + 2 * HALO
    for c0 in range(0, n_conv, COL_CHUNK):
        c1 = min(n_conv, c0 + COL_CHUNK)
        u = jnp.dot(hb, w_ref[:, c0:c1], preferred_element_type=F32)
        up = pltpu.roll(u, 1, 0)[HALO:HALO + tm]
        un = pltpu.roll(u, rows - 1, 0)[HALO:HALO + tm]
        um = u[HALO:HALO + tm]
        z_ref[0, :, c0:c1] = (sw_ref[0:1, c0:c1] * up + sw_ref[1:2, c0:c1] * um
                              + sw_ref[2:3, c0:c1] * un)
    hc = hb[HALO:HALO + tm]
    n_all = w_ref.shape[1]
    for c0 in range(n_conv, n_all, COL_CHUNK):
        c1 = min(n_all, c0 + COL_CHUNK)
        gate_ref[0, :, c0 - n_conv:c1 - n_conv] = jnp.dot(hc, w_ref[:, c0:c1],
                                                          preferred_element_type=F32)


def _proj_shift(x, g, sc1, sh, w, sw, n_conv, tm):
    b, t, d = x.shape
    tm = min(tm, t)
    n = w.shape[1]
    hb = tm // HALO
    nhb = t // HALO
    vec = pl.BlockSpec((1, 1, d), lambda bi, i: (bi, 0, 0))
    return pl.pallas_call(
        functools.partial(_proj_shift_body, tm=tm, n_conv=n_conv),
        grid=(b, t // tm),
        in_specs=[
            pl.BlockSpec((1, tm, d), lambda bi, i: (bi, i, 0)),
            pl.BlockSpec((1, HALO, d), lambda bi, i: (bi, jnp.maximum(i * hb - 1, 0), 0)),
            pl.BlockSpec((1, HALO, d), lambda bi, i: (bi, jnp.minimum((i + 1) * hb, nhb - 1), 0)),
            pl.BlockSpec((1, d), lambda bi, i: (0, 0)),
            vec, vec,
            pl.BlockSpec((d, n), lambda bi, i: (0, 0)),
            pl.BlockSpec((3, n_conv), lambda bi, i: (0, 0)),
        ],
        out_specs=[pl.BlockSpec((1, tm, n_conv), lambda bi, i: (bi, i, 0)),
                   pl.BlockSpec((1, tm, n - n_conv), lambda bi, i: (bi, i, 0))],
        out_shape=[jax.ShapeDtypeStruct((b, t, n_conv), F32),
                   jax.ShapeDtypeStruct((b, t, n - n_conv), F32)],
        compiler_params=_cparams(("parallel", "parallel")),
        name="proj_shift",
    )(x, x, x, g, sc1, sh, w, sw)


def _rms(x, g):
    return x * lax.rsqrt(jnp.mean(x * x, axis=-1, keepdims=True) + NORM_EPS) * g


def _qkv_body(ua_ref, kvn_ref, qn_ref, wk_ref, wv_ref, wq_ref, kg_ref, qg_ref, kb_ref, qb_ref,
              cos_ref, sa_ref, sb_ref, q_ref, k_ref, v_ref, *, rope):
    ua = ua_ref[0]
    ckv = _rms(ua[:, :KV_LORA], kvn_ref[...]).astype(BF16)
    kr = ua[:, KV_LORA:KV_LORA + LANE]
    cq = _rms(ua[:, KV_LORA + LANE:], qn_ref[...]).astype(BF16)
    kn = jnp.dot(ckv, wk_ref[...], preferred_element_type=F32)
    vv = jnp.dot(ckv, wv_ref[...], preferred_element_type=F32)
    qq = jnp.dot(cq, wq_ref[...], preferred_element_type=F32)
    ones_hi = (_iota2((1, HEAD_SLOT), 1) >= V_HEAD).astype(F32)
    pe = pltpu.roll(kr, QK_NOPE, 1)
    kg, qg = kg_ref[...], qg_ref[...]
    inv_n = 1.0 / QK_HEAD

    def finish(xh, gain):
        xh = xh * lax.rsqrt(jnp.sum(xh * xh, axis=-1, keepdims=True) * inv_n + NORM_EPS) * gain
        if rope:
            xh = (xh * cos_ref[...] + pltpu.roll(xh, LANE - ROPE_FREQS, 1) * sa_ref[...]
                  + pltpu.roll(xh, ROPE_FREQS, 1) * sb_ref[...])
        return xh

    for h in range(MLA_HEADS):
        sl = slice(h * HEAD_SLOT, (h + 1) * HEAD_SLOT)
        k_ref[0, h] = (finish(kn[:, sl] + pe, kg) + kb_ref[...]).astype(BF16)
        q_ref[0, h] = (finish(qq[:, sl], qg) + qb_ref[...]).astype(BF16)
        v_ref[0, h] = (vv[:, sl] + ones_hi).astype(BF16)


def _qkv(ua, kvn, qn, wk, wv, wq, kg, qg, kb, qb, tabs, rope, tm):
    b, t, wa = ua.shape
    tm = min(tm, t)
    full = lambda a: pl.BlockSpec(a.shape, lambda bi, i: (0,) * a.ndim)
    tab = pl.BlockSpec((tm, LANE), lambda bi, i: (i, 0))
    return pl.pallas_call(
        functools.partial(_qkv_body, rope=rope),
        grid=(b, t // tm),
        in_specs=[pl.BlockSpec((1, tm, wa), lambda bi, i: (bi, i, 0)),
                  full(kvn), full(qn), full(wk), full(wv), full(wq), full(kg), full(qg),
                  full(kb), full(qb), tab, tab, tab],
        out_specs=[pl.BlockSpec((1, MLA_HEADS, tm, HEAD_SLOT), lambda bi, i: (bi, 0, i, 0)),
                   pl.BlockSpec((1, MLA_HEADS, tm, HEAD_SLOT), lambda bi, i: (bi, 0, i, 0)),
                   pl.BlockSpec((1, MLA_HEADS, tm, HEAD_SLOT), lambda bi, i: (bi, 0, i, 0))],
        out_shape=[jax.ShapeDtypeStruct((b, MLA_HEADS, t, HEAD_SLOT), BF16),
                   jax.ShapeDtypeStruct((b, MLA_HEADS, t, HEAD_SLOT), BF16),
                   jax.ShapeDtypeStruct((b, MLA_HEADS, t, HEAD_SLOT), BF16)],
        compiler_params=_cparams(("parallel", "parallel")),
        name="qkv",
    )(ua, kvn, qn, wk, wv, wq, kg, qg, kb, qb, *tabs)


BIAS_LANE = QK_HEAD
MAX_STATIC_BOUND = 50.0


def _attn_finish(acc_ref, o_ref):
    bq = acc_ref.shape[1]
    lane = _iota2((bq, LANE), 1)
    o0 = acc_ref[0] / pltpu.roll(acc_ref[0], V_HEAD, 1)
    o1 = acc_ref[1] / pltpu.roll(acc_ref[1], V_HEAD, 1)
    o_ref[0] = jnp.where(lane < V_HEAD, o0, pltpu.roll(o1, V_HEAD, 1))


def _attn_static_body(q_ref, k_ref, v_ref, o_ref, acc_ref):
    j = pl.program_id(3)

    @pl.when(j == 0)
    def _():
        acc_ref[...] = jnp.zeros(acc_ref.shape, F32)

    for hh in range(2):
        s = lax.dot_general(q_ref[0, hh], k_ref[0, hh], NT, preferred_element_type=F32)
        p = jnp.exp2(s.astype(BF16))
        acc_ref[hh] += jnp.dot(p, v_ref[0, hh], preferred_element_type=F32)

    @pl.when(j == pl.num_programs(3) - 1)
    def _():
        _attn_finish(acc_ref, o_ref)


def _attn_online_body(q_ref, k_ref, v_ref, o_ref, m_ref, acc_ref):
    j = pl.program_id(3)

    @pl.when(j == 0)
    def _():
        m_ref[...] = jnp.full(m_ref.shape, -jnp.inf, F32)
        acc_ref[...] = jnp.zeros(acc_ref.shape, F32)

    for hh in range(2):
        s = lax.dot_general(q_ref[0, hh], k_ref[0, hh], NT, preferred_element_type=F32)
        m_prev = m_ref[hh]
        m_new = jnp.maximum(m_prev, jnp.max(s, axis=-1, keepdims=True))
        p = jnp.exp2(s - m_new)
        alpha = jnp.exp2(m_prev - m_new)
        acc_ref[hh] = alpha * acc_ref[hh] + jnp.dot(p.astype(BF16), v_ref[0, hh],
                                                     preferred_element_type=F32)
        m_ref[hh] = m_new

    @pl.when(j == pl.num_programs(3) - 1)
    def _():
        _attn_finish(acc_ref, o_ref)


def _attention(q, k, v, static_ok, bq, bk):
    b, h, s, e = q.shape
    sk = k.shape[2]
    bq, bk = min(bq, s), min(bk, sk)

    def call(body, scratch, name):
        return pl.pallas_call(
            body,
            grid=(b, h // 2, s // bq, sk // bk),
            in_specs=[pl.BlockSpec((1, 2, bq, e), lambda bi, p, i, j: (bi, p, i, 0)),
                      pl.BlockSpec((1, 2, bk, e), lambda bi, p, i, j: (bi, p, j, 0)),
                      pl.BlockSpec((1, 2, bk, e), lambda bi, p, i, j: (bi, p, j, 0))],
            out_specs=pl.BlockSpec((1, bq, 2 * V_HEAD), lambda bi, p, i, j: (bi, i, p)),
            out_shape=jax.ShapeDtypeStruct((b, s, h * V_HEAD), F32),
            scratch_shapes=scratch,
            compiler_params=_cparams(("parallel", "parallel", "parallel", "arbitrary")),
            name=name,
        )(q, k, v)

    acc = pltpu.VMEM((2, bq, LANE), F32)
    return lax.cond(
        static_ok,
        lambda: call(_attn_static_body, [acc], "attention"),
        lambda: call(_attn_online_body, [pltpu.VMEM((2, bq, 1), F32), acc], "attention_online"))


def _dft_mats(n):
    idx = np.arange(n)
    ang = 2.0 * np.pi * ((idx[:, None] * idx[None, :]) % n) / n
    return np.cos(ang), np.sin(ang)


def _hilo(a):
    a = jnp.asarray(a, F32)
    hi = a.astype(BF16)
    return hi, (a - hi.astype(F32)).astype(BF16)


def _mm3c(ah, al, b, dn=NN):
    bh, bl = _split(b)
    d = lambda x, y: lax.dot_general(x, y, dn, preferred_element_type=F32)
    return d(ah, bh) + d(al, bh) + d(ah, bl)


def _four1_body(x_ref, ch_ref, cl_ref, o_ref):
    x = x_ref[0]
    xh, xl = _split(x)
    d = lambda a, b: jnp.dot(a, b, preferred_element_type=F32)
    z = d(xh, ch_ref[...]) + d(xl, ch_ref[...]) + d(xh, cl_ref[...])
    o_ref[0, 0, 0] = z[:, :FNET_GROUP_DIM]
    o_ref[0, 0, 1] = z[:, FNET_GROUP_DIM:]


def _four2_body(z_ref, wh_ref, wl_ref, tc_ref, ts_ref, o_ref):
    r = tc_ref.shape[0]
    y = _mm3c(wh_ref[...], wl_ref[...], z_ref[0, 0])
    yc, ys = y[:r], y[r:]
    tc, ts = tc_ref[...], ts_ref[...]
    o_ref[0, 0, :r] = yc * tc - ys * ts
    o_ref[0, 0, r:] = yc * ts + ys * tc


def _four3_body(y_ref, ch_ref, cl_ref, sh_ref, sl_ref, wf_ref, o_ref, *, krt, scale):
    wf = wf_ref[0]
    for j in range(krt):
        yc = y_ref[0, 0, 0, j * GRID_W:(j + 1) * GRID_W, :]
        ys = y_ref[0, 0, 1, j * GRID_W:(j + 1) * GRID_W, :]
        res = (_mm3c(ch_ref[...], cl_ref[...], yc) - _mm3c(sh_ref[...], sl_ref[...], ys)) * scale
        o_ref[0, 0, j] = _mm3(res, wf)


def _fourier_latent(xf, w_fnet):
    b, t, _ = xf.shape
    g, gd = FNET_GROUPS, FNET_GROUP_DIM
    r = t // GRID_W
    cc, sc = _dft_mats(gd)
    ch, cl = _hilo(np.concatenate([cc, sc], axis=1))
    tm = min(512, t)
    z = pl.pallas_call(
        _four1_body,
        grid=(b, g, t // tm),
        in_specs=[pl.BlockSpec((1, tm, gd), lambda bi, gi, i: (bi, i, gi)),
                  pl.BlockSpec((gd, 2 * gd), lambda bi, gi, i: (0, 0)),
                  pl.BlockSpec((gd, 2 * gd), lambda bi, gi, i: (0, 0))],
        out_specs=pl.BlockSpec((1, 1, 2, tm, gd), lambda bi, gi, i: (bi, gi, 0, i, 0)),
        out_shape=jax.ShapeDtypeStruct((b, g, 2, t, gd), F32),
        compiler_params=_cparams(("parallel", "parallel", "parallel")),
        name="fourier_channels",
    )(xf, ch, cl)
    wide = GRID_W * gd
    zv = z.reshape(b, g, 2 * r, wide)
    cr, sr = _dft_mats(r)
    wh, wl = _hilo(np.block([[cr, -sr], [sr, cr]]))
    kr_i, c_i = np.arange(r)[:, None], np.arange(GRID_W)[None, :]
    ang = 2.0 * np.pi * ((kr_i * c_i) % t) / t
    twc = jnp.repeat(jnp.asarray(np.cos(ang), F32), gd, axis=1)
    tws = jnp.repeat(jnp.asarray(np.sin(ang), F32), gd, axis=1)
    tl = min(2048, wide)
    y2 = pl.pallas_call(
        _four2_body,
        grid=(b, g, wide // tl),
        in_specs=[pl.BlockSpec((1, 1, 2 * r, tl), lambda bi, gi, l: (bi, gi, 0, l)),
                  pl.BlockSpec((2 * r, 2 * r), lambda bi, gi, l: (0, 0)),
                  pl.BlockSpec((2 * r, 2 * r), lambda bi, gi, l: (0, 0)),
                  pl.BlockSpec((r, tl), lambda bi, gi, l: (0, l)),
                  pl.BlockSpec((r, tl), lambda bi, gi, l: (0, l))],
        out_specs=pl.BlockSpec((1, 1, 2 * r, tl), lambda bi, gi, l: (bi, gi, 0, l)),
        out_shape=jax.ShapeDtypeStruct((b, g, 2 * r, wide), F32),
        compiler_params=_cparams(("parallel", "parallel", "parallel")),
        name="fourier_rows",
    )(zv, wh, wl, twc, tws)
    y2v = y2.reshape(b, g, 2, r * GRID_W, gd)
    c64, s64 = _dft_mats(GRID_W)
    c64h, c64l = _hilo(c64)
    s64h, s64l = _hilo(s64)
    krt = min(8, r)
    small = pl.BlockSpec((GRID_W, GRID_W), lambda bi, gi, i: (0, 0))
    fo = pl.pallas_call(
        functools.partial(_four3_body, krt=krt, scale=1.0 / math.sqrt(t * gd)),
        grid=(b, g, r // krt),
        in_specs=[pl.BlockSpec((1, 1, 2, krt * GRID_W, gd), lambda bi, gi, i: (bi, gi, 0, i, 0)),
                  small, small, small, small,
                  pl.BlockSpec((1, gd, gd), lambda bi, gi, i: (gi, 0, 0))],
        out_specs=pl.BlockSpec((1, 1, krt, GRID_W, gd), lambda bi, gi, i: (bi, gi, i, 0, 0)),
        out_shape=jax.ShapeDtypeStruct((b, g, r, GRID_W, gd), F32),
        compiler_params=_cparams(("parallel", "parallel", "parallel")),
        name="fourier_cols",
    )(y2v, c64h, c64l, s64h, s64l, w_fnet)
    return fo.transpose(0, 3, 2, 1, 4).reshape(b, t, g * gd)


def _four_dense_body(x_ref, ch_ref, cl_ref, th_ref, tl_ref, sh_ref, sl_ref, wf_ref, o_ref, *, scale):
    x = x_ref[0]
    xh, xl = _split(x)
    d = lambda a, b: jnp.dot(a, b, preferred_element_type=F32)
    z = d(xh, ch_ref[...]) + d(xl, ch_ref[...]) + d(xh, cl_ref[...])
    zc, zs = z[:, :FNET_GROUP_DIM], z[:, FNET_GROUP_DIM:]
    f = (_mm3c(th_ref[...], tl_ref[...], zc) - _mm3c(sh_ref[...], sl_ref[...], zs)) * scale
    o_ref[0] = _mm3(f, wf_ref[0])


def _fourier_dense(xf, w_fnet):
    b, t, _ = xf.shape
    g, gd = FNET_GROUPS, FNET_GROUP_DIM
    cc, sc = _dft_mats(gd)
    ch, cl = _hilo(np.concatenate([cc, sc], axis=1))
    ct, st = _dft_mats(t)
    cth, ctl = _hilo(ct)
    sth, stl = _hilo(st)
    sq = pl.BlockSpec((t, t), lambda bi, gi: (0, 0))
    cs = pl.BlockSpec((gd, 2 * gd), lambda bi, gi: (0, 0))
    return pl.pallas_call(
        functools.partial(_four_dense_body, scale=1.0 / math.sqrt(t * gd)),
        grid=(b, g),
        in_specs=[pl.BlockSpec((1, t, gd), lambda bi, gi: (bi, 0, gi)), cs, cs, sq, sq, sq, sq,
                  pl.BlockSpec((1, gd, gd), lambda bi, gi: (gi, 0, 0))],
        out_specs=pl.BlockSpec((1, t, gd), lambda bi, gi: (bi, 0, gi)),
        out_shape=jax.ShapeDtypeStruct((b, t, g * gd), F32),
        compiler_params=_cparams(("parallel", "parallel")),
        name="fourier_dense",
    )(xf, ch, cl, cth, ctl, sth, stl, w_fnet)


def _merge_body(o_ref, f_ref, gate_ref, x_ref, gl_ref, w_ref, out_ref):
    gt = gate_ref[0]
    mix = jnp.concatenate([o_ref[0], f_ref[0]], axis=-1) * (gt * _sigmoid(gt))
    y = jnp.dot(mix.astype(BF16), w_ref[...], preferred_element_type=F32)
    out_ref[0] = x_ref[0] + gl_ref[0] * y


def _merge(o, f, gate, x, gl, w, tm):
    b, t, d = x.shape
    tm = min(tm, t)
    half = o.shape[2]
    tok = lambda n: pl.BlockSpec((1, tm, n), lambda bi, i: (bi, i, 0))
    return pl.pallas_call(
        _merge_body,
        grid=(b, t // tm),
        in_specs=[tok(half), tok(half), tok(d), tok(d),
                  pl.BlockSpec((1, 1, d), lambda bi, i: (bi, 0, 0)),
                  pl.BlockSpec(w.shape, lambda bi, i: (0, 0))],
        out_specs=tok(d),
        out_shape=jax.ShapeDtypeStruct((b, t, d), F32),
        compiler_params=_cparams(("parallel", "parallel")),
        name="merge",
    )(o, f, gate, x, gl, w)


EXP_M05 = math.exp(-0.5)


PAIRS_PER_STEP = 4


def _rwkv_local_body(zk_ref, zv_ref, zr_ref, zwa_ref, w0_ref, w2_ref, a0_ref, a2_ref,
                     kk_ref, ka_ref, rk_ref, m_ref, g_ref, qt_ref, yl_ref, bn_ref):
    c = CHUNK
    zwa = zwa_ref[0]
    lora_w, lora_a = jnp.tanh(zwa[:, :LANE]), zwa[:, LANE:]

    lane = _iota2((1, LANE), 1)
    m0 = (lane < RWKV_HEAD).astype(F32)
    m1 = 1.0 - m0
    r2 = _iota2((LANE, LANE), 0)
    c2 = _iota2((LANE, LANE), 1)
    same = (r2 // RWKV_HEAD) == (c2 // RWKV_HEAD)
    ones_bd = same.astype(F32)
    eye = r2 == c2
    eye_f = eye.astype(F32)
    rc = _iota2((c, c), 0)
    cc = _iota2((c, c), 1)
    stack = lambda x: jnp.concatenate([x * m0, x * m1], axis=0)
    fold = lambda x: x[:c] + x[c:]

    pairs = range(PAIRS_PER_STEP)
    chains = [(q, d) for q in pairs for d in range(2)]
    qls = [slice(q * LANE, (q + 1) * LANE) for q in pairs]
    ks = [zk_ref[0, :, ql] for ql in qls]
    vs_ = [zv_ref[0, :, ql] for ql in qls]
    rs = [zr_ref[0, :, ql] for ql in qls]
    wraw = [_mm(lora_w, w2_ref[q]) + w0_ref[q] for q in pairs]
    araw = [_mm(lora_a, a2_ref[q]) + a0_ref[q] for q in pairs]
    logw = [-EXP_M05 * _sigmoid(w) for w in wraw]
    a_all = [_sigmoid(a) for a in araw]
    kk0 = [ks[q] * kk_ref[:, qls[q]] for q in pairs]
    ss = [_mm2r(x * x, ones_bd) for x in kk0]
    kk = [kk0[q] / jnp.maximum(jnp.sqrt(ss[q]), 1e-12) for q in pairs]
    vstk = [stack(v) for v in vs_]

    dsl = [slice(d * LANE, (d + 1) * LANE) for d in range(2)]
    lw = [logw[q][:, dsl[d]] for q, d in chains]
    ad = [a_all[q][:, dsl[d]] for q, d in chains]
    kd = [ks[q] * (1.0 + (ad[i] - 1.0) * ka_ref[:, qls[q]]) for i, (q, d) in enumerate(chains)]
    bb = [kk[q] * ad[i] for i, (q, d) in enumerate(chains)]
    bonus = [_mm2r(rs[q] * kd[i] * rk_ref[:, qls[q]], ones_bd) * vs_[q]
             for i, (q, d) in enumerate(chains)]
    tri_f = (cc <= rc).astype(BF16)
    tri_r = (cc >= rc).astype(BF16)
    strict = [same & (c2 < r2), same & (c2 > r2)]
    incl = [same & (c2 <= r2), same & (c2 >= r2)]

    def cumsum(x, tri):
        xh, xl = _split(x)
        xll = (x - xh.astype(F32) - xl.astype(F32)).astype(BF16)
        dd = lambda y: jnp.dot(tri, y, preferred_element_type=F32)
        return dd(xh) + dd(xl) + dd(xll)

    lc = [cumsum(lw[i], tri_f if d == 0 else tri_r) for i, (q, d) in enumerate(chains)]
    ltot = [lc[i][c - 1:c] if d == 0 else lc[i][0:1] for i, (q, d) in enumerate(chains)]
    kkd_s = [stack(kk[q] * jnp.exp(lc[i] - lw[i])) for i, (q, d) in enumerate(chains)]
    rd_s = [stack(rs[q] * jnp.exp(lc[i])) for i, (q, d) in enumerate(chains)]
    e_inv = [jnp.exp(-x) for x in lc]
    inv_s = [jnp.concatenate([stack(bb[i] * e_inv[i]), stack(kd[i] * e_inv[i])], axis=0)
             for i in range(len(chains))]
    amat = [_mm(jnp.concatenate([kkd_s[i], rd_s[i]], axis=0), inv_s[i], NT)
            for i in range(len(chains))]
    a_kb = [jnp.where(strict[d], amat[i][:LANE, :LANE], 0.0) for i, (q, d) in enumerate(chains)]
    a_kk = [jnp.where(strict[d], amat[i][:LANE, LANE:], 0.0) for i, (q, d) in enumerate(chains)]
    aq_b = [jnp.where(incl[d], amat[i][LANE:, :LANE], 0.0) for i, (q, d) in enumerate(chains)]
    aq_k = [jnp.where(incl[d], amat[i][LANE:, LANE:], 0.0) for i, (q, d) in enumerate(chains)]
    av = [_mm(jnp.concatenate([a_kk[i], aq_k[i]], axis=0), vstk[q])
          for i, (q, d) in enumerate(chains)]
    tinv = [eye_f - a for a in a_kb]
    qpow = [_mm(a, a) for a in a_kb]
    for _ in range(4):
        prod = [_mm(qpow[i], jnp.concatenate([qpow[i], tinv[i]], axis=1))
                for i in range(len(chains))]
        qpow = [x[:, :LANE] for x in prod]
        tinv = [tinv[i] + prod[i][:, LANE:] for i in range(len(chains))]
    tinv = [tinv[i] + _mm(qpow[i], tinv[i]) for i in range(len(chains))]
    resid = [eye_f - _mm3(eye_f + a_kb[i], tinv[i]) for i in range(len(chains))]
    tinv = [tinv[i] + _mm(tinv[i], resid[i]) for i in range(len(chains))]
    x = [_mm(tinv[i], jnp.concatenate([kkd_s[i], av[i][:LANE]], axis=1))
         for i in range(len(chains))]
    qy = [jnp.concatenate([rd_s[i], av[i][LANE:]], axis=1) - _mm(aq_b[i], x[i])
          for i in range(len(chains))]
    e_end = [jnp.exp(ltot[i] - lc[i]) for i in range(len(chains))]
    bx = [_mm(stack(bb[i] * e_end[i]), x[i], TN) for i in range(len(chains))]
    kv = [_mm(stack(kd[i] * e_end[i]), vstk[q], TN) for i, (q, d) in enumerate(chains)]
    for i, (q, d) in enumerate(chains):
        m_ref[0, 0, d, q] = fold(jnp.where(eye, jnp.exp(ltot[i]), 0.0) - bx[i][:, :LANE])
        g_ref[0, 0, d, q] = fold(kv[i] - bx[i][:, LANE:])
        qt_ref[0, d, :, qls[q]] = fold(qy[i][:, :LANE])
    for q in pairs:
        yl_ref[0, :, qls[q]] = fold(qy[2 * q][:, LANE:]) + fold(qy[2 * q + 1][:, LANE:])
        bn_ref[0, :, qls[q]] = bonus[2 * q] + bonus[2 * q + 1]


def _rwkv_local(z, w0p, w2p, a0p, a2p, k_k, k_a, r_k):
    b, t, _ = z.shape
    w = k_k.shape[1]
    npair = w // LANE
    pp = PAIRS_PER_STEP
    ng = npair // pp
    wl = pp * LANE
    nc = t // CHUNK
    tokc = lambda base: pl.BlockSpec((1, CHUNK, wl), lambda bi, ci, p: (bi, ci, base + p))
    perp3 = lambda n: pl.BlockSpec((pp, n, 2 * LANE), lambda bi, ci, p: (p, 0, 0))
    vecp = pl.BlockSpec((1, wl), lambda bi, ci, p: (0, p))
    mat = pl.BlockSpec((1, 1, 2, pp, CHUNK, LANE), lambda bi, ci, p: (bi, ci, 0, p, 0, 0))
    return pl.pallas_call(
        _rwkv_local_body,
        grid=(b, nc, ng),
        in_specs=[tokc(0), tokc(ng), tokc(2 * ng),
                  pl.BlockSpec((1, CHUNK, 2 * LANE), lambda bi, ci, p: (bi, ci, 3 * npair // 2)),
                  perp3(1), perp3(LANE), perp3(1), perp3(LANE), vecp, vecp, vecp],
        out_specs=[mat, mat,
                   pl.BlockSpec((1, 2, CHUNK, wl), lambda bi, ci, p: (bi, 0, ci, p)),
                   pl.BlockSpec((1, CHUNK, wl), lambda bi, ci, p: (bi, ci, p)),
                   pl.BlockSpec((1, CHUNK, wl), lambda bi, ci, p: (bi, ci, p))],
        out_shape=[jax.ShapeDtypeStruct((b, nc, 2, npair, CHUNK, LANE), F32),
                   jax.ShapeDtypeStruct((b, nc, 2, npair, CHUNK, LANE), F32),
                   jax.ShapeDtypeStruct((b, 2, t, w), F32),
                   jax.ShapeDtypeStruct((b, t, w), F32),
                   jax.ShapeDtypeStruct((b, t, w), F32)],
        compiler_params=_cparams(("parallel", "parallel", "parallel")),
        name="rwkv_local",
    )(z, z, z, z, w0p, w2p, a0p, a2p, k_k, k_a, r_k)


def _rwkv_scan_body(m0_ref, g0_ref, q0_ref, m1_ref, g1_ref, q1_ref, h0_ref,
                    y0_ref, y1_ref, hfin_ref, h_scr, *, npair):
    ci = pl.program_id(1)

    @pl.when(ci == 0)
    def _():
        h_scr[...] = h0_ref[0]

    lane = _iota2((1, LANE), 1)
    m0 = (lane < RWKV_HEAD).astype(F32)
    m1 = 1.0 - m0
    expand = lambda x: jnp.concatenate([x * m0, x * m1], axis=0)
    for d, (m_ref, g_ref, q_ref, y_ref) in enumerate(((m0_ref, g0_ref, q0_ref, y0_ref),
                                                      (m1_ref, g1_ref, q1_ref, y1_ref))):
        for p in range(npair):
            sl = slice(p * LANE, (p + 1) * LANE)
            hst = h_scr[d, p]
            y_ref[0, :, sl] = _mm3(q_ref[0, 0, :, sl], hst)
            h_scr[d, p] = _mm3(expand(m_ref[0, 0, 0, p]), hst) + expand(g_ref[0, 0, 0, p])

    @pl.when(ci == pl.num_programs(1) - 1)
    def _():
        hfin_ref[0] = h_scr[...]


def _rwkv_scan(mm, gg, qt, h0):
    b, nc, _, npair, _, _ = mm.shape
    t, w = qt.shape[2], qt.shape[3]
    fwd = lambda bi, ci: (bi, ci, 0, 0, 0, 0)
    rev = lambda bi, ci: (bi, nc - 1 - ci, 1, 0, 0, 0)
    mblk = (1, 1, 1, npair, CHUNK, LANE)
    hspec = pl.BlockSpec((1, 2, npair, LANE, LANE), lambda bi, ci: (bi, 0, 0, 0, 0))
    return pl.pallas_call(
        functools.partial(_rwkv_scan_body, npair=npair),
        grid=(b, nc),
        in_specs=[pl.BlockSpec(mblk, fwd), pl.BlockSpec(mblk, fwd),
                  pl.BlockSpec((1, 1, CHUNK, w), lambda bi, ci: (bi, 0, ci, 0)),
                  pl.BlockSpec(mblk, rev), pl.BlockSpec(mblk, rev),
                  pl.BlockSpec((1, 1, CHUNK, w), lambda bi, ci: (bi, 1, nc - 1 - ci, 0)),
                  hspec],
        out_specs=[pl.BlockSpec((1, CHUNK, w), lambda bi, ci: (bi, ci, 0)),
                   pl.BlockSpec((1, CHUNK, w), lambda bi, ci: (bi, nc - 1 - ci, 0)),
                   hspec],
        out_shape=[jax.ShapeDtypeStruct((b, t, w), F32), jax.ShapeDtypeStruct((b, t, w), F32),
                   jax.ShapeDtypeStruct(h0.shape, F32)],
        scratch_shapes=[pltpu.VMEM((2, npair, LANE, LANE), F32)],
        compiler_params=_cparams(("parallel", "arbitrary")),
        name="rwkv_scan",
    )(mm, gg, qt, mm, gg, qt, h0)


def _rwkv_out_body(y0_ref, y1_ref, yl_ref, bn_ref, gate_ref, x_ref, gl_ref, gnw_ref, gnb_ref,
                   w_ref, o_ref):
    y = y0_ref[0] + y1_ref[0] + yl_ref[0]
    r2 = _iota2((LANE, LANE), 0)
    c2 = _iota2((LANE, LANE), 1)
    avg = ((r2 // RWKV_HEAD) == (c2 // RWKV_HEAD)).astype(F32) * (1.0 / RWKV_HEAD)
    parts = []
    for p in range(y.shape[1] // LANE):
        yp = y[:, p * LANE:(p + 1) * LANE]
        dl = yp - _mm2r(yp, avg)
        var = _mm2r(dl * dl, avg)
        parts.append(dl * lax.rsqrt(var + GN_EPS))
    yn = jnp.concatenate(parts, axis=1)
    gt = gate_ref[0]
    act = (yn * gnw_ref[...] + gnb_ref[...] + bn_ref[0]) * (gt * _sigmoid(gt))
    out = jnp.dot(act.astype(BF16), w_ref[...], preferred_element_type=F32)
    o_ref[0] = x_ref[0] + gl_ref[0] * out


def _rwkv_out(y0, y1, yl, bn, gate, x, gl, gnw, gnb, w, tm):
    b, t, d = x.shape
    tm = min(tm, t)
    wd = y0.shape[2]
    tok = lambda n: pl.BlockSpec((1, tm, n), lambda bi, i: (bi, i, 0))
    return pl.pallas_call(
        _rwkv_out_body,
        grid=(b, t // tm),
        in_specs=[tok(wd), tok(wd), tok(wd), tok(wd), tok(wd), tok(d),
                  pl.BlockSpec((1, 1, d), lambda bi, i: (bi, 0, 0)),
                  pl.BlockSpec((1, wd), lambda bi, i: (0, 0)),
                  pl.BlockSpec((1, wd), lambda bi, i: (0, 0)),
                  pl.BlockSpec(w.shape, lambda bi, i: (0, 0))],
        out_specs=tok(d),
        out_shape=jax.ShapeDtypeStruct((b, t, d), F32),
        compiler_params=_cparams(("parallel", "parallel")),
        name="rwkv_out",
    )(y0, y1, yl, bn, gate, x, gl, gnw, gnb, w)


def _rope_tables(t):
    rows = t // GRID_W
    row = jnp.repeat(jnp.arange(rows, dtype=F32), GRID_W)
    col = jnp.tile(jnp.arange(GRID_W, dtype=F32), rows)
    inv = 1.0 / (ROPE_BASE ** (jnp.arange(ROPE_FREQS, dtype=F32) / ROPE_FREQS))
    ang = jnp.stack([row[:, None] * inv, col[:, None] * inv], axis=1)
    cos, sin = jnp.cos(ang), jnp.sin(ang)
    zeros = jnp.zeros_like(sin)
    ones_lo = jnp.ones((t, QK_NOPE), F32)
    pad_hi = HEAD_SLOT - QK_HEAD
    cos_t = jnp.concatenate([ones_lo, jnp.concatenate([cos, cos], axis=2).reshape(t, QK_ROPE),
                             jnp.ones((t, pad_hi), F32)], axis=1)
    sa = jnp.concatenate([jnp.zeros((t, QK_NOPE), F32),
                          jnp.concatenate([-sin, zeros], axis=2).reshape(t, QK_ROPE),
                          jnp.zeros((t, pad_hi), F32)], axis=1)
    sb = jnp.concatenate([jnp.zeros((t, QK_NOPE), F32),
                          jnp.concatenate([zeros, sin], axis=2).reshape(t, QK_ROPE),
                          jnp.zeros((t, pad_hi), F32)], axis=1)
    return cos_t, sa, sb


def _even_layer(x, ctx, mod_l, mod_c, need_ctx, g, w_in, kv_norm, q_norm, w_uq, w_ukv,
                q_head_norm, k_head_norm, w_fnet, w_out):
    b, s, d = x.shape
    tc = ctx.shape[1]
    e_q0 = KV_LORA + QK_ROPE
    e_f0 = e_q0 + Q_LORA
    e_g0 = e_f0 + FNET_GROUPS * FNET_GROUP_DIM
    w_p = jnp.concatenate([w_in[:, e_g0:], w_in[:, e_f0:e_g0], w_in[:, :e_q0],
                           jnp.zeros((d, LANE - QK_ROPE), F32), w_in[:, e_q0:e_f0]],
                          axis=1).astype(BF16)
    splits = (d, FNET_GROUPS * FNET_GROUP_DIM, KV_LORA + LANE + Q_LORA)
    kvw = w_ukv.reshape(KV_LORA, MLA_HEADS, QK_NOPE + V_HEAD)
    wk = jnp.pad(kvw[:, :, :QK_NOPE], ((0, 0), (0, 0), (0, HEAD_SLOT - QK_NOPE)))
    wk = wk.reshape(KV_LORA, MLA_HEADS * HEAD_SLOT).astype(BF16)
    wv = jnp.pad(kvw[:, :, QK_NOPE:], ((0, 0), (0, 0), (0, HEAD_SLOT - V_HEAD)))
    wv = wv.reshape(KV_LORA, MLA_HEADS * HEAD_SLOT).astype(BF16)
    wq = jnp.pad(w_uq.reshape(Q_LORA, MLA_HEADS, QK_HEAD), ((0, 0), (0, 0), (0, HEAD_SLOT - QK_HEAD)))
    wq = wq.reshape(Q_LORA, MLA_HEADS * HEAD_SLOT).astype(BF16)
    kg = jnp.pad(k_head_norm, (0, HEAD_SLOT - QK_HEAD)).reshape(1, HEAD_SLOT)
    qg = (jnp.pad(q_head_norm, (0, HEAD_SLOT - QK_HEAD))
          * (QK_HEAD ** -0.5 * math.log2(math.e))).reshape(1, HEAD_SLOT)
    kvn, qn = kv_norm.reshape(1, -1), q_norm.reshape(1, -1)
    g2 = g.reshape(1, d)
    bound = (1.02 * QK_HEAD * jnp.max(jnp.abs(qg)) * jnp.max(jnp.abs(kg))).astype(BF16).astype(F32)
    static_ok = bound <= MAX_STATIC_BOUND
    bias_lane = (jnp.arange(HEAD_SLOT) == BIAS_LANE).astype(F32).reshape(1, HEAD_SLOT)
    kb = bias_lane * jnp.where(static_ok, -bound, 0.0)
    qb = bias_lane

    gate_l, four_l, ua_l = _proj(x, g2, mod_l[1], mod_l[0], w_p, splits, 512)
    gate_c, four_c, ua_c = _proj(ctx, g2, mod_c[1], mod_c[0], w_p, splits, 512)
    q_l, k_l, v_l = _qkv(ua_l, kvn, qn, wk, wv, wq, kg, qg, kb, qb, _rope_tables(s), True, 256)
    dummy = jnp.zeros((tc, LANE), F32)
    q_c, k_c, v_c = _qkv(ua_c, kvn, qn, wk, wv, wq, kg, qg, kb, qb, (dummy, dummy, dummy),
                         False, 256)
    k_all = jnp.concatenate([k_l, k_c], axis=2)
    v_all = jnp.concatenate([v_l, v_c], axis=2)
    sk = s + tc
    bk = 768 if sk % 768 == 0 else tc
    o_l = _attention(q_l, k_all, v_all, static_ok, 1024, bk)
    f_l = _fourier_latent(four_l, w_fnet)
    wo = w_out.astype(BF16)
    x_new = _merge(o_l, f_l, gate_l, x, mod_l[2], wo, 512)
    ctx_new = ctx
    if need_ctx:
        o_c = _attention(q_c, k_c, v_c, static_ok, tc, tc)
        f_c = _fourier_dense(four_c, w_fnet)
        ctx_new = _merge(o_c, f_c, gate_c, ctx, mod_c[2], wo, 512)
    return x_new, ctx_new


def _odd_layer(x, ctx, mod_l, mod_c, need_ctx, g, w_in, shift_w, w0, w2, a0, a2, k_k, k_a, r_k,
               gn_w, gn_b, w_out):
    b, s, d = x.shape
    w = k_k.shape[0]
    npair = w // LANE
    o_wd0 = 2 * w
    o_r0 = o_wd0 + 2 * DECAY_LORA + 2 * AAA_LORA
    conv_ch = o_r0 + w
    perm = lambda m: jnp.concatenate([m[:, :o_wd0], m[:, o_r0:conv_ch], m[:, o_wd0:o_r0]], axis=1)
    w_p = jnp.concatenate([perm(w_in), w_in[:, conv_ch:]], axis=1).astype(BF16)
    sw = perm(shift_w)
    g2 = g.reshape(1, d)

    def pairs(vec2):
        return vec2.reshape(2, npair, LANE).transpose(1, 0, 2).reshape(npair, 1, 2 * LANE)

    def pair_mats(m):
        rr = m.shape[1]
        mp = m.reshape(2, rr, npair, LANE).transpose(2, 0, 1, 3)
        z = jnp.zeros_like(mp[:, 0])
        top = jnp.concatenate([mp[:, 0], z], axis=2)
        bot = jnp.concatenate([z, mp[:, 1]], axis=2)
        return jnp.concatenate([top, bot], axis=1).astype(BF16)

    w0p, a0p, w2p, a2p = pairs(w0), pairs(a0), pair_mats(w2), pair_mats(a2)
    kk2, ka2, rk2 = k_k.reshape(1, w), k_a.reshape(1, w), r_k.reshape(1, w)
    wo = w_out.astype(BF16)

    def mix(xin, mod, h0):
        z, gate = _proj_shift(xin, g2, mod[1], mod[0], w_p, sw, conv_ch, 256)
        mm, gg, qt, yl, bn = _rwkv_local(z, w0p, w2p, a0p, a2p, kk2, ka2, rk2)
        y0, y1, hfin = _rwkv_scan(mm, gg, qt, h0)
        return (y0, y1, yl, bn, gate), hfin

    h_zero = jnp.zeros((b, 2, npair, LANE, LANE), F32)
    parts_c, h_ctx = mix(ctx, mod_c, h_zero)
    parts_l, _ = mix(x, mod_l, h_ctx)
    x_new = _rwkv_out(*parts_l, x, mod_l[2], gn_w.reshape(1, w), gn_b.reshape(1, w), wo, 256)
    ctx_new = ctx
    if need_ctx:
        ctx_new = _rwkv_out(*parts_c, ctx, mod_c[2], gn_w.reshape(1, w), gn_b.reshape(1, w), wo, 256)
    return x_new, ctx_new


def kernel(x, c, ctx, c_ctx, ada_w, ada_b, norm_g, e_w_in, e_kv_norm, e_q_norm, e_w_uq, e_w_ukv,
           e_q_head_norm, e_k_head_norm, e_w_fnet, e_w_out, o_w_in, o_shift_w, o_w0, o_w2, o_a0,
           o_a2, o_k_k, o_k_a, o_r_k, o_gn_w, o_gn_b, o_w_out):
    b, s, d = x.shape
    depth = ada_w.shape[0]
    assert b + 1 <= 8
    cond8 = jnp.concatenate([c, c_ctx[None, :], jnp.zeros((8 - b - 1, d), F32)], axis=0)
    mod = _ada(cond8, ada_w, ada_b)
    for layer in range(depth):
        need_ctx = layer < depth - 1
        m = mod[layer]
        chunk = lambda rows, i: rows[:, None, i * d:(i + 1) * d]
        lat, cx = m[:b], jnp.broadcast_to(m[b:b + 1], (b, 3 * d))
        mod_l = (chunk(lat, 0), 1.0 + chunk(lat, 1), chunk(lat, 2))
        mod_c = (chunk(cx, 0), 1.0 + chunk(cx, 1), chunk(cx, 2))
        j = layer // 2
        if layer % 2 == 0:
            x, ctx = _even_layer(x, ctx, mod_l, mod_c, need_ctx, norm_g[layer], e_w_in[j],
                                 e_kv_norm[j], e_q_norm[j], e_w_uq[j], e_w_ukv[j],
                                 e_q_head_norm[j], e_k_head_norm[j], e_w_fnet[j], e_w_out[j])
        else:
            x, ctx = _odd_layer(x, ctx, mod_l, mod_c, need_ctx, norm_g[layer], o_w_in[j],
                                o_shift_w[j], o_w0[j], o_w2[j], o_a0[j], o_a2[j], o_k_k[j],
                                o_k_a[j], o_r_k[j].reshape(-1), o_gn_w[j], o_gn_b[j], o_w_out[j])
    return x
```

```python
import functools
import math

import numpy as np
import jax
import jax.numpy as jnp
from jax import lax
from jax.experimental import pallas as pl
from jax.experimental.pallas import tpu as pltpu

F32 = jnp.float32
BF16 = jnp.bfloat16

GRID_W = 64
NORM_EPS = 1e-6
MLA_HEADS = 8
QK_NOPE = 64
QK_ROPE = 32
QK_HEAD = QK_NOPE + QK_ROPE
V_HEAD = 64
Q_LORA = 384
KV_LORA = 256
ROPE_FREQS = QK_ROPE // 4
ROPE_BASE = 10000.0
FNET_GROUPS = 4
FNET_GROUP_DIM = 128
RWKV_HEAD = 64
DECAY_LORA = 64
AAA_LORA = 64
GN_EPS = 64e-5

LANE = 128
CHUNK = 64
HEAD_SLOT = 128
VMEM_LIMIT = 56 * 1024 * 1024

NN = (((1,), (0,)), ((), ()))
NT = (((1,), (1,)), ((), ()))
TN = (((0,), (0,)), ((), ()))


def _cparams(sem):
    return pltpu.CompilerParams(dimension_semantics=sem, vmem_limit_bytes=VMEM_LIMIT)


def _mm(a, b, dn=NN):
    return lax.dot_general(a.astype(BF16), b.astype(BF16), dn, preferred_element_type=F32)


def _split(a):
    hi = a.astype(BF16)
    lo = (a - hi.astype(F32)).astype(BF16)
    return hi, lo


def _mm3(a, b, dn=NN):
    ah, al = _split(a)
    bh, bl = _split(b)
    d = lambda x, y: lax.dot_general(x, y, dn, preferred_element_type=F32)
    return d(ah, bh) + d(al, bh) + d(ah, bl)


def _mm2r(a, b_exact):
    ah, al = _split(a)
    bb = b_exact.astype(BF16)
    d = lambda x: lax.dot_general(x, bb, NN, preferred_element_type=F32)
    return d(ah) + d(al)


def _mm2l(a_exact, b):
    bh, bl = _split(b)
    aa = a_exact.astype(BF16)
    d = lambda y: lax.dot_general(aa, y, NN, preferred_element_type=F32)
    return d(bh) + d(bl)


def _sigmoid(x):
    return 1.0 / (1.0 + jnp.exp(-x))


def _modnorm(x, g, sc1, sh):
    y = x * lax.rsqrt(jnp.mean(x * x, axis=-1, keepdims=True) + NORM_EPS)
    return (y * g) * sc1 + sh


def _iota2(shape, dim):
    return lax.broadcasted_iota(jnp.int32, shape, dim)


def _ada_body(c_ref, w_ref, b_ref, o_ref):
    c = c_ref[...]
    s = c * _sigmoid(c)
    o_ref[0] = _mm3(s, w_ref[0]) + b_ref[0]


def _ada(cond8, ada_w, ada_b):
    depth, d, n = ada_w.shape
    tn = 512
    return pl.pallas_call(
        _ada_body,
        grid=(depth, n // tn),
        in_specs=[
            pl.BlockSpec((8, d), lambda l, j: (0, 0)),
            pl.BlockSpec((1, d, tn), lambda l, j: (l, 0, j)),
            pl.BlockSpec((1, 1, tn), lambda l, j: (l, 0, j)),
        ],
        out_specs=pl.BlockSpec((1, 8, tn), lambda l, j: (l, 0, j)),
        out_shape=jax.ShapeDtypeStruct((depth, 8, n), F32),
        compiler_params=_cparams(("parallel", "parallel")),
        name="ada",
    )(cond8, ada_w, ada_b.reshape(depth, 1, n))


COL_CHUNK = 512


def _proj_body(x_ref, g_ref, sc_ref, sh_ref, w_ref, *o_refs, splits):
    h = _modnorm(x_ref[0], g_ref[...], sc_ref[0], sh_ref[0]).astype(BF16)
    off = 0
    for o_ref, n in zip(o_refs, splits):
        for c0 in range(0, n, COL_CHUNK):
            c1 = min(n, c0 + COL_CHUNK)
            o_ref[0, :, c0:c1] = jnp.dot(h, w_ref[:, off + c0:off + c1], preferred_element_type=F32)
        off += n


def _proj(x, g, sc1, sh, w, splits, tm):
    b, t, d = x.shape
    tm = min(tm, t)
    n = w.shape[1]
    vec = pl.BlockSpec((1, 1, d), lambda bi, i: (bi, 0, 0))
    return pl.pallas_call(
        functools.partial(_proj_body, splits=splits),
        grid=(b, t // tm),
        in_specs=[
            pl.BlockSpec((1, tm, d), lambda bi, i: (bi, i, 0)),
            pl.BlockSpec((1, d), lambda bi, i: (0, 0)),
            vec, vec,
            pl.BlockSpec((d, n), lambda bi, i: (0, 0)),
        ],
        out_specs=[pl.BlockSpec((1, tm, s), lambda bi, i: (bi, i, 0)) for s in splits],
        out_shape=[jax.ShapeDtypeStruct((b, t, s), F32) for s in splits],
        compiler_params=_cparams(("parallel", "parallel")),
        name="proj",
    )(x, g, sc1, sh, w)


HALO = 16


def _proj_shift_body(x_ref, xp_ref, xn_ref, g_ref, sc_ref, sh_ref, w_ref, sw_ref, z_ref, gate_ref,
                     *, tm, n_conv):
    i = pl.program_id(1)
    last = pl.num_programs(1) - 1
    g, sc1, sh = g_ref[...], sc_ref[0], sh_ref[0]
    h = _modnorm(x_ref[0], g, sc1, sh)
    hp = _modnorm(xp_ref[0], g, sc1, sh) * (i > 0).astype(F32)
    hn = _modnorm(xn_ref[0], g, sc1, sh) * (i < last).astype(F32)
    hb = jnp.concatenate([hp, h, hn], axis=0).astype(BF16)
    rows = tm + 2 * HALO
    for c0 in range(0, n_conv, COL_CHUNK):
        c1 = min(n_conv, c0 + COL_CHUNK)
        u = jnp.dot(hb, w_ref[:, c0:c1], preferred_element_type=F32)
        up = pltpu.roll(u, 1, 0)[HALO:HALO + tm]
        un = pltpu.roll(u, rows - 1, 0)[HALO:HALO + tm]
        um = u[HALO:HALO + tm]
        z_ref[0, :, c0:c1] = (sw_ref[0:1, c0:c1] * up + sw_ref[1:2, c0:c1] * um
                              + sw_ref[2:3, c0:c1] * un)
    hc = hb[HALO:HALO + tm]
    n_all = w_ref.shape[1]
    for c0 in range(n_conv, n_all, COL_CHUNK):
        c1 = min(n_all, c0 + COL_CHUNK)
        gate_ref[0, :, c0 - n_conv:c1 - n_conv] = jnp.dot(hc, w_ref[:, c0:c1],
                                                          preferred_element_type=F32)


def _proj_shift(x, g, sc1, sh, w, sw, n_conv, tm):
    b, t, d = x.shape
    tm = min(tm, t)
    n = w.shape[1]
    hb = tm // HALO
    nhb = t // HALO
    vec = pl.BlockSpec((1, 1, d), lambda bi, i: (bi, 0, 0))
    return pl.pallas_call(
        functools.partial(_proj_shift_body, tm=tm, n_conv=n_conv),
        grid=(b, t // tm),
        in_specs=[
            pl.BlockSpec((1, tm, d), lambda bi, i: (bi, i, 0)),
            pl.BlockSpec((1, HALO, d), lambda bi, i: (bi, jnp.maximum(i * hb - 1, 0), 0)),
            pl.BlockSpec((1, HALO, d), lambda bi, i: (bi, jnp.minimum((i + 1) * hb, nhb - 1), 0)),
            pl.BlockSpec((1, d), lambda bi, i: (0, 0)),
            vec, vec,
            pl.BlockSpec((d, n), lambda bi, i: (0, 0)),
            pl.BlockSpec((3, n_conv), lambda bi, i: (0, 0)),
        ],
        out_specs=[pl.BlockSpec((1, tm, n_conv), lambda bi, i: (bi, i, 0)),
                   pl.BlockSpec((1, tm, n - n_conv), lambda bi, i: (bi, i, 0))],
        out_shape=[jax.ShapeDtypeStruct((b, t, n_conv), F32),
                   jax.ShapeDtypeStruct((b, t, n - n_conv), F32)],
        compiler_params=_cparams(("parallel", "parallel")),
        name="proj_shift",
    )(x, x, x, g, sc1, sh, w, sw)


def _rms(x, g):
    return x * lax.rsqrt(jnp.mean(x * x, axis=-1, keepdims=True) + NORM_EPS) * g


def _qkv_body(ua_ref, kvn_ref, qn_ref, wk_ref, wv_ref, wq_ref, kg_ref, qg_ref, kb_ref, qb_ref,
              cos_ref, sa_ref, sb_ref, *rest, rope):
    q_ref, k_ref, v_ref = rest[-3:]
    ua = ua_ref[0]
    ckv = _rms(ua[:, :KV_LORA], kvn_ref[...]).astype(BF16)
    kr = ua[:, KV_LORA:KV_LORA + LANE]
    cq = _rms(ua[:, KV_LORA + LANE:], qn_ref[...]).astype(BF16)
    kn = jnp.dot(ckv, wk_ref[...], preferred_element_type=F32)
    vv = jnp.dot(ckv, wv_ref[...], preferred_element_type=F32)
    qq = jnp.dot(cq, wq_ref[...], preferred_element_type=F32)
    ones_hi = (_iota2((1, HEAD_SLOT), 1) >= V_HEAD).astype(F32)
    pe = pltpu.roll(kr, QK_NOPE, 1)
    kg, qg = kg_ref[...], qg_ref[...]
    inv_n = 1.0 / QK_HEAD

    def finish(xh, gain):
        xh = xh * lax.rsqrt(jnp.sum(xh * xh, axis=-1, keepdims=True) * inv_n + NORM_EPS) * gain
        if rope:
            xh = (xh * cos_ref[...] + pltpu.roll(xh, LANE - ROPE_FREQS, 1) * sa_ref[...]
                  + pltpu.roll(xh, ROPE_FREQS, 1) * sb_ref[...])
        return xh

    for h in range(MLA_HEADS):
        sl = slice(h * HEAD_SLOT, (h + 1) * HEAD_SLOT)
        k_ref[0, h] = (finish(kn[:, sl] + pe, kg) + kb_ref[...]).astype(BF16)
        q_ref[0, h] = (finish(qq[:, sl], qg) + qb_ref[...]).astype(BF16)
        v_ref[0, h] = (vv[:, sl] + ones_hi).astype(BF16)


def _qkv(ua, kvn, qn, wk, wv, wq, kg, qg, kb, qb, tabs, rope, tm, kv_rows, kv_into=None):
    b, t, wa = ua.shape
    tm = min(tm, t)
    full = lambda a: pl.BlockSpec(a.shape, lambda bi, i: (0,) * a.ndim)
    tab = pl.BlockSpec((tm, LANE), lambda bi, i: (i, 0))
    off = 0 if kv_into is None else (kv_rows - t) // tm
    head_blk = (1, MLA_HEADS, tm, HEAD_SLOT)
    kv_spec = pl.BlockSpec(head_blk, lambda bi, i: (bi, 0, off + i, 0))
    kv_shape = jax.ShapeDtypeStruct((b, MLA_HEADS, kv_rows, HEAD_SLOT), BF16)
    n_in = 13
    extra, extra_specs, aliases = (), [], {}
    if kv_into is not None:
        extra = tuple(kv_into)
        extra_specs = [pl.BlockSpec(memory_space=pl.ANY)] * 2
        aliases = {n_in: 1, n_in + 1: 2}
    return pl.pallas_call(
        functools.partial(_qkv_body, rope=rope),
        grid=(b, t // tm),
        in_specs=[pl.BlockSpec((1, tm, wa), lambda bi, i: (bi, i, 0)),
                  full(kvn), full(qn), full(wk), full(wv), full(wq), full(kg), full(qg),
                  full(kb), full(qb), tab, tab, tab] + extra_specs,
        out_specs=[pl.BlockSpec(head_blk, lambda bi, i: (bi, 0, i, 0)), kv_spec, kv_spec],
        out_shape=[jax.ShapeDtypeStruct((b, MLA_HEADS, t, HEAD_SLOT), BF16), kv_shape, kv_shape],
        input_output_aliases=aliases,
        compiler_params=_cparams(("parallel", "parallel")),
        name="qkv",
    )(ua, kvn, qn, wk, wv, wq, kg, qg, kb, qb, *tabs, *extra)


BIAS_LANE = QK_HEAD
MAX_STATIC_BOUND = 50.0


def _attn_finish(acc_ref, o_ref):
    bq = acc_ref.shape[1]
    lane = _iota2((bq, LANE), 1)
    o0 = acc_ref[0] / pltpu.roll(acc_ref[0], V_HEAD, 1)
    o1 = acc_ref[1] / pltpu.roll(acc_ref[1], V_HEAD, 1)
    o_ref[0] = jnp.where(lane < V_HEAD, o0, pltpu.roll(o1, V_HEAD, 1))


def _attn_static_body(q_ref, k_ref, v_ref, o_ref, acc_ref):
    j = pl.program_id(3)

    @pl.when(j == 0)
    def _():
        acc_ref[...] = jnp.zeros(acc_ref.shape, F32)

    s = [lax.dot_general(q_ref[0, hh], k_ref[0, hh], NT, preferred_element_type=F32)
         for hh in range(2)]
    p = [jnp.exp2(x.astype(BF16)) for x in s]
    for hh in range(2):
        acc_ref[hh] += jnp.dot(p[hh], v_ref[0, hh], preferred_element_type=F32)

    @pl.when(j == pl.num_programs(3) - 1)
    def _():
        _attn_finish(acc_ref, o_ref)


def _attn_online_body(q_ref, k_ref, v_ref, o_ref, m_ref, acc_ref):
    j = pl.program_id(3)

    @pl.when(j == 0)
    def _():
        m_ref[...] = jnp.full(m_ref.shape, -jnp.inf, F32)
        acc_ref[...] = jnp.zeros(acc_ref.shape, F32)

    for hh in range(2):
        s = lax.dot_general(q_ref[0, hh], k_ref[0, hh], NT, preferred_element_type=F32)
        m_prev = m_ref[hh]
        m_new = jnp.maximum(m_prev, jnp.max(s, axis=-1, keepdims=True))
        p = jnp.exp2(s - m_new)
        alpha = jnp.exp2(m_prev - m_new)
        acc_ref[hh] = alpha * acc_ref[hh] + jnp.dot(p.astype(BF16), v_ref[0, hh],
                                                     preferred_element_type=F32)
        m_ref[hh] = m_new

    @pl.when(j == pl.num_programs(3) - 1)
    def _():
        _attn_finish(acc_ref, o_ref)


def _attention(q, k, v, static_ok, bq, bk, k_start=0, k_rows=None):
    b, h, s, e = q.shape
    sk = k.shape[2] if k_rows is None else k_rows
    bq, bk = min(bq, s), min(bk, sk)
    j0 = k_start // bk

    def call(body, scratch, name):
        return pl.pallas_call(
            body,
            grid=(b, h // 2, s // bq, sk // bk),
            in_specs=[pl.BlockSpec((1, 2, bq, e), lambda bi, p, i, j: (bi, p, i, 0)),
                      pl.BlockSpec((1, 2, bk, e), lambda bi, p, i, j: (bi, p, j0 + j, 0)),
                      pl.BlockSpec((1, 2, bk, e), lambda bi, p, i, j: (bi, p, j0 + j, 0))],
            out_specs=pl.BlockSpec((1, bq, 2 * V_HEAD), lambda bi, p, i, j: (bi, i, p)),
            out_shape=jax.ShapeDtypeStruct((b, s, h * V_HEAD), F32),
            scratch_shapes=scratch,
            compiler_params=_cparams(("parallel", "parallel", "parallel", "arbitrary")),
            name=name,
        )(q, k, v)

    acc = pltpu.VMEM((2, bq, LANE), F32)
    return lax.cond(
        static_ok,
        lambda: call(_attn_static_body, [acc], "attention"),
        lambda: call(_attn_online_body, [pltpu.VMEM((2, bq, 1), F32), acc], "attention_online"))


def _dft_mats(n):
    idx = np.arange(n)
    ang = 2.0 * np.pi * ((idx[:, None] * idx[None, :]) % n) / n
    return np.cos(ang), np.sin(ang)


def _hilo(a):
    a = jnp.asarray(a, F32)
    hi = a.astype(BF16)
    return hi, (a - hi.astype(F32)).astype(BF16)


def _mm3c(ah, al, b, dn=NN):
    bh, bl = _split(b)
    d = lambda x, y: lax.dot_general(x, y, dn, preferred_element_type=F32)
    return d(ah, bh) + d(al, bh) + d(ah, bl)


def _four1_body(x_ref, ch_ref, cl_ref, o_ref):
    x = x_ref[0]
    xh, xl = _split(x)
    d = lambda a, b: jnp.dot(a, b, preferred_element_type=F32)
    z = d(xh, ch_ref[...]) + d(xl, ch_ref[...]) + d(xh, cl_ref[...])
    o_ref[0, 0, 0] = z[:, :FNET_GROUP_DIM]
    o_ref[0, 0, 1] = z[:, FNET_GROUP_DIM:]


def _four2_body(z_ref, wh_ref, wl_ref, tc_ref, ts_ref, o_ref):
    r = tc_ref.shape[0]
    y = _mm3c(wh_ref[...], wl_ref[...], z_ref[0, 0])
    yc, ys = y[:r], y[r:]
    tc, ts = tc_ref[...], ts_ref[...]
    o_ref[0, 0, :r] = yc * tc - ys * ts
    o_ref[0, 0, r:] = yc * ts + ys * tc


def _four3_body(y_ref, ch_ref, cl_ref, sh_ref, sl_ref, wf_ref, o_ref, *, krt, scale):
    wf = wf_ref[0]
    for j in range(krt):
        yc = y_ref[0, 0, 0, j * GRID_W:(j + 1) * GRID_W, :]
        ys = y_ref[0, 0, 1, j * GRID_W:(j + 1) * GRID_W, :]
        res = (_mm3c(ch_ref[...], cl_ref[...], yc) - _mm3c(sh_ref[...], sl_ref[...], ys)) * scale
        o_ref[0, 0, j] = _mm3(res, wf)


def _fourier_latent(xf, w_fnet):
    b, t, _ = xf.shape
    g, gd = FNET_GROUPS, FNET_GROUP_DIM
    r = t // GRID_W
    cc, sc = _dft_mats(gd)
    ch, cl = _hilo(np.concatenate([cc, sc], axis=1))
    tm = min(512, t)
    z = pl.pallas_call(
        _four1_body,
        grid=(b, g, t // tm),
        in_specs=[pl.BlockSpec((1, tm, gd), lambda bi, gi, i: (bi, i, gi)),
                  pl.BlockSpec((gd, 2 * gd), lambda bi, gi, i: (0, 0)),
                  pl.BlockSpec((gd, 2 * gd), lambda bi, gi, i: (0, 0))],
        out_specs=pl.BlockSpec((1, 1, 2, tm, gd), lambda bi, gi, i: (bi, gi, 0, i, 0)),
        out_shape=jax.ShapeDtypeStruct((b, g, 2, t, gd), F32),
        compiler_params=_cparams(("parallel", "parallel", "parallel")),
        name="fourier_channels",
    )(xf, ch, cl)
    wide = GRID_W * gd
    zv = z.reshape(b, g, 2 * r, wide)
    cr, sr = _dft_mats(r)
    wh, wl = _hilo(np.block([[cr, -sr], [sr, cr]]))
    kr_i, c_i = np.arange(r)[:, None], np.arange(GRID_W)[None, :]
    ang = 2.0 * np.pi * ((kr_i * c_i) % t) / t
    twc = jnp.repeat(jnp.asarray(np.cos(ang), F32), gd, axis=1)
    tws = jnp.repeat(jnp.asarray(np.sin(ang), F32), gd, axis=1)
    tl = min(2048, wide)
    y2 = pl.pallas_call(
        _four2_body,
        grid=(b, g, wide // tl),
        in_specs=[pl.BlockSpec((1, 1, 2 * r, tl), lambda bi, gi, l: (bi, gi, 0, l)),
                  pl.BlockSpec((2 * r, 2 * r), lambda bi, gi, l: (0, 0)),
                  pl.BlockSpec((2 * r, 2 * r), lambda bi, gi, l: (0, 0)),
                  pl.BlockSpec((r, tl), lambda bi, gi, l: (0, l)),
                  pl.BlockSpec((r, tl), lambda bi, gi, l: (0, l))],
        out_specs=pl.BlockSpec((1, 1, 2 * r, tl), lambda bi, gi, l: (bi, gi, 0, l)),
        out_shape=jax.ShapeDtypeStruct((b, g, 2 * r, wide), F32),
        compiler_params=_cparams(("parallel", "parallel", "parallel")),
        name="fourier_rows",
    )(zv, wh, wl, twc, tws)
    y2v = y2.reshape(b, g, 2, r * GRID_W, gd)
    c64, s64 = _dft_mats(GRID_W)
    c64h, c64l = _hilo(c64)
    s64h, s64l = _hilo(s64)
    krt = min(8, r)
    small = pl.BlockSpec((GRID_W, GRID_W), lambda bi, gi, i: (0, 0))
    fo = pl.pallas_call(
        functools.partial(_four3_body, krt=krt, scale=1.0 / math.sqrt(t * gd)),
        grid=(b, g, r // krt),
        in_specs=[pl.BlockSpec((1, 1, 2, krt * GRID_W, gd), lambda bi, gi, i: (bi, gi, 0, i, 0)),
                  small, small, small, small,
                  pl.BlockSpec((1, gd, gd), lambda bi, gi, i: (gi, 0, 0))],
        out_specs=pl.BlockSpec((1, 1, krt, GRID_W, gd), lambda bi, gi, i: (bi, gi, i, 0, 0)),
        out_shape=jax.ShapeDtypeStruct((b, g, r, GRID_W, gd), F32),
        compiler_params=_cparams(("parallel", "parallel", "parallel")),
        name="fourier_cols",
    )(y2v, c64h, c64l, s64h, s64l, w_fnet)
    return fo.transpose(0, 3, 2, 1, 4).reshape(b, t, g * gd)


def _four_dense_body(x_ref, ch_ref, cl_ref, th_ref, tl_ref, sh_ref, sl_ref, wf_ref, o_ref, *, scale):
    x = x_ref[0]
    xh, xl = _split(x)
    d = lambda a, b: jnp.dot(a, b, preferred_element_type=F32)
    z = d(xh, ch_ref[...]) + d(xl, ch_ref[...]) + d(xh, cl_ref[...])
    zc, zs = z[:, :FNET_GROUP_DIM], z[:, FNET_GROUP_DIM:]
    f = (_mm3c(th_ref[...], tl_ref[...], zc) - _mm3c(sh_ref[...], sl_ref[...], zs)) * scale
    o_ref[0] = _mm3(f, wf_ref[0])


def _fourier_dense(xf, w_fnet):
    b, t, _ = xf.shape
    g, gd = FNET_GROUPS, FNET_GROUP_DIM
    cc, sc = _dft_mats(gd)
    ch, cl = _hilo(np.concatenate([cc, sc], axis=1))
    ct, st = _dft_mats(t)
    cth, ctl = _hilo(ct)
    sth, stl = _hilo(st)
    sq = pl.BlockSpec((t, t), lambda bi, gi: (0, 0))
    cs = pl.BlockSpec((gd, 2 * gd), lambda bi, gi: (0, 0))
    return pl.pallas_call(
        functools.partial(_four_dense_body, scale=1.0 / math.sqrt(t * gd)),
        grid=(b, g),
        in_specs=[pl.BlockSpec((1, t, gd), lambda bi, gi: (bi, 0, gi)), cs, cs, sq, sq, sq, sq,
                  pl.BlockSpec((1, gd, gd), lambda bi, gi: (gi, 0, 0))],
        out_specs=pl.BlockSpec((1, t, gd), lambda bi, gi: (bi, 0, gi)),
        out_shape=jax.ShapeDtypeStruct((b, t, g * gd), F32),
        compiler_params=_cparams(("parallel", "parallel")),
        name="fourier_dense",
    )(xf, ch, cl, cth, ctl, sth, stl, w_fnet)


def _merge_body(o_ref, f_ref, gate_ref, x_ref, gl_ref, w_ref, out_ref):
    gt = gate_ref[0]
    mix = jnp.concatenate([o_ref[0], f_ref[0]], axis=-1) * (gt * _sigmoid(gt))
    y = jnp.dot(mix.astype(BF16), w_ref[...], preferred_element_type=F32)
    out_ref[0] = x_ref[0] + gl_ref[0] * y


def _merge(o, f, gate, x, gl, w, tm):
    b, t, d = x.shape
    tm = min(tm, t)
    half = o.shape[2]
    tok = lambda n: pl.BlockSpec((1, tm, n), lambda bi, i: (bi, i, 0))
    return pl.pallas_call(
        _merge_body,
        grid=(b, t // tm),
        in_specs=[tok(half), tok(half), tok(d), tok(d),
                  pl.BlockSpec((1, 1, d), lambda bi, i: (bi, 0, 0)),
                  pl.BlockSpec(w.shape, lambda bi, i: (0, 0))],
        out_specs=tok(d),
        out_shape=jax.ShapeDtypeStruct((b, t, d), F32),
        compiler_params=_cparams(("parallel", "parallel")),
        name="merge",
    )(o, f, gate, x, gl, w)


EXP_M05 = math.exp(-0.5)


PAIRS_PER_STEP = 4


def _rwkv_local_body(zk_ref, zv_ref, zr_ref, zwa_ref, w0_ref, w2_ref, a0_ref, a2_ref,
                     kk_ref, ka_ref, rk_ref, m_ref, g_ref, qt_ref, yl_ref, bn_ref):
    c = CHUNK
    zwa = zwa_ref[0]
    lora_w, lora_a = jnp.tanh(zwa[:, :LANE]), zwa[:, LANE:]

    lane = _iota2((1, LANE), 1)
    m0 = (lane < RWKV_HEAD).astype(F32)
    m1 = 1.0 - m0
    r2 = _iota2((LANE, LANE), 0)
    c2 = _iota2((LANE, LANE), 1)
    same = (r2 // RWKV_HEAD) == (c2 // RWKV_HEAD)
    ones_bd = same.astype(F32)
    eye = r2 == c2
    eye_f = eye.astype(F32)
    rc = _iota2((c, c), 0)
    cc = _iota2((c, c), 1)
    stack = lambda x: jnp.concatenate([x * m0, x * m1], axis=0)
    fold = lambda x: x[:c] + x[c:]

    pairs = range(PAIRS_PER_STEP)
    chains = [(q, d) for q in pairs for d in range(2)]
    qls = [slice(q * LANE, (q + 1) * LANE) for q in pairs]
    ks = [zk_ref[0, :, ql] for ql in qls]
    vs_ = [zv_ref[0, :, ql] for ql in qls]
    rs = [zr_ref[0, :, ql] for ql in qls]
    wraw = [_mm(lora_w, w2_ref[q]) + w0_ref[q] for q in pairs]
    araw = [_mm(lora_a, a2_ref[q]) + a0_ref[q] for q in pairs]
    logw = [-EXP_M05 * _sigmoid(w) for w in wraw]
    a_all = [_sigmoid(a) for a in araw]
    kk0 = [ks[q] * kk_ref[:, qls[q]] for q in pairs]
    ss = [_mm2r(x * x, ones_bd) for x in kk0]
    kk = [kk0[q] / jnp.maximum(jnp.sqrt(ss[q]), 1e-12) for q in pairs]
    vstk = [stack(v) for v in vs_]

    dsl = [slice(d * LANE, (d + 1) * LANE) for d in range(2)]
    lw = [logw[q][:, dsl[d]] for q, d in chains]
    ad = [a_all[q][:, dsl[d]] for q, d in chains]
    kd = [ks[q] * (1.0 + (ad[i] - 1.0) * ka_ref[:, qls[q]]) for i, (q, d) in enumerate(chains)]
    bb = [kk[q] * ad[i] for i, (q, d) in enumerate(chains)]
    bonus = [_mm2r(rs[q] * kd[i] * rk_ref[:, qls[q]], ones_bd) * vs_[q]
             for i, (q, d) in enumerate(chains)]
    tri_f = (cc <= rc).astype(BF16)
    tri_r = (cc >= rc).astype(BF16)
    strict = [same & (c2 < r2), same & (c2 > r2)]
    incl = [same & (c2 <= r2), same & (c2 >= r2)]

    def cumsum(x, tri):
        xh, xl = _split(x)
        xll = (x - xh.astype(F32) - xl.astype(F32)).astype(BF16)
        dd = lambda y: jnp.dot(tri, y, preferred_element_type=F32)
        return dd(xh) + dd(xl) + dd(xll)

    lc = [cumsum(lw[i], tri_f if d == 0 else tri_r) for i, (q, d) in enumerate(chains)]
    ltot = [lc[i][c - 1:c] if d == 0 else lc[i][0:1] for i, (q, d) in enumerate(chains)]
    kkd_s = [stack(kk[q] * jnp.exp(lc[i] - lw[i])) for i, (q, d) in enumerate(chains)]
    rd_s = [stack(rs[q] * jnp.exp(lc[i])) for i, (q, d) in enumerate(chains)]
    e_inv = [jnp.exp(-x) for x in lc]
    inv_s = [jnp.concatenate([stack(bb[i] * e_inv[i]), stack(kd[i] * e_inv[i])], axis=0)
             for i in range(len(chains))]
    amat = [_mm(jnp.concatenate([kkd_s[i], rd_s[i]], axis=0), inv_s[i], NT)
            for i in range(len(chains))]
    a_kb = [jnp.where(strict[d], amat[i][:LANE, :LANE], 0.0).astype(BF16).astype(F32)
            for i, (q, d) in enumerate(chains)]
    a_kk = [jnp.where(strict[d], amat[i][:LANE, LANE:], 0.0) for i, (q, d) in enumerate(chains)]
    aq_b = [jnp.where(incl[d], amat[i][LANE:, :LANE], 0.0) for i, (q, d) in enumerate(chains)]
    aq_k = [jnp.where(incl[d], amat[i][LANE:, LANE:], 0.0) for i, (q, d) in enumerate(chains)]
    av = [_mm(jnp.concatenate([a_kk[i], aq_k[i]], axis=0), vstk[q])
          for i, (q, d) in enumerate(chains)]
    tinv = [eye_f - a for a in a_kb]
    qpow = [_mm(a, a) for a in a_kb]
    for _ in range(4):
        prod = [_mm(qpow[i], jnp.concatenate([qpow[i], tinv[i]], axis=1))
                for i in range(len(chains))]
        qpow = [x[:, :LANE] for x in prod]
        tinv = [tinv[i] + prod[i][:, LANE:] for i in range(len(chains))]
    tinv = [tinv[i] + _mm(qpow[i], tinv[i]) for i in range(len(chains))]
    resid = [eye_f - _mm2l(eye_f + a_kb[i], tinv[i]) for i in range(len(chains))]
    tinv = [tinv[i] + _mm(tinv[i], resid[i]) for i in range(len(chains))]
    x = [_mm(tinv[i], jnp.concatenate([kkd_s[i], av[i][:LANE]], axis=1))
         for i in range(len(chains))]
    qy = [jnp.concatenate([rd_s[i], av[i][LANE:]], axis=1) - _mm(aq_b[i], x[i])
          for i in range(len(chains))]
    e_end = [jnp.exp(ltot[i] - lc[i]) for i in range(len(chains))]
    bx = [_mm(stack(bb[i] * e_end[i]), x[i], TN) for i in range(len(chains))]
    kv = [_mm(stack(kd[i] * e_end[i]), vstk[q], TN) for i, (q, d) in enumerate(chains)]
    for i, (q, d) in enumerate(chains):
        m_ref[0, 0, d, q] = fold(jnp.where(eye, jnp.exp(ltot[i]), 0.0)
                                 - bx[i][:, :LANE]).astype(BF16)
        g_ref[0, 0, d, q] = fold(kv[i] - bx[i][:, LANE:])
        qt_ref[0, d, :, qls[q]] = fold(qy[i][:, :LANE]).astype(BF16)
    for q in pairs:
        yl_ref[0, :, qls[q]] = fold(qy[2 * q][:, LANE:]) + fold(qy[2 * q + 1][:, LANE:])
        bn_ref[0, :, qls[q]] = bonus[2 * q] + bonus[2 * q + 1]


def _rwkv_local(z, w0p, w2p, a0p, a2p, k_k, k_a, r_k):
    b, t, _ = z.shape
    w = k_k.shape[1]
    npair = w // LANE
    pp = PAIRS_PER_STEP
    ng = npair // pp
    wl = pp * LANE
    nc = t // CHUNK
    tokc = lambda base: pl.BlockSpec((1, CHUNK, wl), lambda bi, ci, p: (bi, ci, base + p))
    perp3 = lambda n: pl.BlockSpec((pp, n, 2 * LANE), lambda bi, ci, p: (p, 0, 0))
    vecp = pl.BlockSpec((1, wl), lambda bi, ci, p: (0, p))
    mat = pl.BlockSpec((1, 1, 2, pp, CHUNK, LANE), lambda bi, ci, p: (bi, ci, 0, p, 0, 0))
    return pl.pallas_call(
        _rwkv_local_body,
        grid=(b, nc, ng),
        in_specs=[tokc(0), tokc(ng), tokc(2 * ng),
                  pl.BlockSpec((1, CHUNK, 2 * LANE), lambda bi, ci, p: (bi, ci, 3 * npair // 2)),
                  perp3(1), perp3(LANE), perp3(1), perp3(LANE), vecp, vecp, vecp],
        out_specs=[mat, mat,
                   pl.BlockSpec((1, 2, CHUNK, wl), lambda bi, ci, p: (bi, 0, ci, p)),
                   pl.BlockSpec((1, CHUNK, wl), lambda bi, ci, p: (bi, ci, p)),
                   pl.BlockSpec((1, CHUNK, wl), lambda bi, ci, p: (bi, ci, p))],
        out_shape=[jax.ShapeDtypeStruct((b, nc, 2, npair, CHUNK, LANE), BF16),
                   jax.ShapeDtypeStruct((b, nc, 2, npair, CHUNK, LANE), F32),
                   jax.ShapeDtypeStruct((b, 2, t, w), BF16),
                   jax.ShapeDtypeStruct((b, t, w), F32),
                   jax.ShapeDtypeStruct((b, t, w), F32)],
        compiler_params=_cparams(("parallel", "parallel", "parallel")),
        name="rwkv_local",
    )(z, z, z, z, w0p, w2p, a0p, a2p, k_k, k_a, r_k)


SCAN_CHUNKS = 4


def _rwkv_scan_body(m0_ref, g0_ref, q0_ref, m1_ref, g1_ref, q1_ref, h0_ref,
                    y0_ref, y1_ref, hfin_ref, h_scr, *, npair, cs):
    ci = pl.program_id(1)

    @pl.when(ci == 0)
    def _():
        h_scr[...] = h0_ref[0]

    head0 = _iota2((1, LANE), 1) < RWKV_HEAD

    def expand(x):
        z = jnp.zeros_like(x)
        return jnp.concatenate([jnp.where(head0, x, z), jnp.where(head0, z, x)], axis=0)

    refs = ((m0_ref, g0_ref, q0_ref, y0_ref), (m1_ref, g1_ref, q1_ref, y1_ref))
    chains = [(d, p) for d in range(2) for p in range(npair)]
    lanes = [slice(p * LANE, (p + 1) * LANE) for p in range(npair)]
    h = [h_scr[d, p] for d, p in chains]
    for step in range(cs):
        ck = (step, cs - 1 - step)
        rows = [slice(c * CHUNK, (c + 1) * CHUNK) for c in ck]
        hb = [x.astype(BF16) for x in h]
        ys = [jnp.dot(refs[d][2][0, 0, rows[d], lanes[p]], hb[i], preferred_element_type=F32)
              for i, (d, p) in enumerate(chains)]
        for i, (d, p) in enumerate(chains):
            refs[d][3][0, rows[d], lanes[p]] = ys[i]
        h = [jnp.dot(expand(refs[d][0][0, ck[d], 0, p]), hb[i], preferred_element_type=F32)
             + expand(refs[d][1][0, ck[d], 0, p]) for i, (d, p) in enumerate(chains)]
    for i, (d, p) in enumerate(chains):
        h_scr[d, p] = h[i]

    @pl.when(ci == pl.num_programs(1) - 1)
    def _():
        hfin_ref[0] = h_scr[...]


def _rwkv_scan(mm, gg, qt, h0):
    b, nc, _, npair, _, _ = mm.shape
    t, w = qt.shape[2], qt.shape[3]
    cs = SCAN_CHUNKS if nc % SCAN_CHUNKS == 0 else 1
    nb = nc // cs
    fwd = lambda bi, ci: (bi, ci, 0, 0, 0, 0)
    rev = lambda bi, ci: (bi, nb - 1 - ci, 1, 0, 0, 0)
    mblk = (1, cs, 1, npair, CHUNK, LANE)
    hspec = pl.BlockSpec((1, 2, npair, LANE, LANE), lambda bi, ci: (bi, 0, 0, 0, 0))
    return pl.pallas_call(
        functools.partial(_rwkv_scan_body, npair=npair, cs=cs),
        grid=(b, nb),
        in_specs=[pl.BlockSpec(mblk, fwd), pl.BlockSpec(mblk, fwd),
                  pl.BlockSpec((1, 1, cs * CHUNK, w), lambda bi, ci: (bi, 0, ci, 0)),
                  pl.BlockSpec(mblk, rev), pl.BlockSpec(mblk, rev),
                  pl.BlockSpec((1, 1, cs * CHUNK, w), lambda bi, ci: (bi, 1, nb - 1 - ci, 0)),
                  hspec],
        out_specs=[pl.BlockSpec((1, cs * CHUNK, w), lambda bi, ci: (bi, ci, 0)),
                   pl.BlockSpec((1, cs * CHUNK, w), lambda bi, ci: (bi, nb - 1 - ci, 0)),
                   hspec],
        out_shape=[jax.ShapeDtypeStruct((b, t, w), F32), jax.ShapeDtypeStruct((b, t, w), F32),
                   jax.ShapeDtypeStruct(h0.shape, F32)],
        scratch_shapes=[pltpu.VMEM((2, npair, LANE, LANE), F32)],
        compiler_params=_cparams(("parallel", "arbitrary")),
        name="rwkv_scan",
    )(mm, gg, qt, mm, gg, qt, h0)


def _rwkv_out_body(y0_ref, y1_ref, yl_ref, bn_ref, gate_ref, x_ref, gl_ref, gnw_ref, gnb_ref,
                   w_ref, o_ref):
    y = y0_ref[0] + y1_ref[0] + yl_ref[0]
    r2 = _iota2((LANE, LANE), 0)
    c2 = _iota2((LANE, LANE), 1)
    avg = ((r2 // RWKV_HEAD) == (c2 // RWKV_HEAD)).astype(F32) * (1.0 / RWKV_HEAD)
    parts = []
    for p in range(y.shape[1] // LANE):
        yp = y[:, p * LANE:(p + 1) * LANE]
        dl = yp - _mm2r(yp, avg)
        var = _mm2r(dl * dl, avg)
        parts.append(dl * lax.rsqrt(var + GN_EPS))
    yn = jnp.concatenate(parts, axis=1)
    gt = gate_ref[0]
    act = (yn * gnw_ref[...] + gnb_ref[...] + bn_ref[0]) * (gt * _sigmoid(gt))
    out = jnp.dot(act.astype(BF16), w_ref[...], preferred_element_type=F32)
    o_ref[0] = x_ref[0] + gl_ref[0] * out


def _rwkv_out(y0, y1, yl, bn, gate, x, gl, gnw, gnb, w, tm):
    b, t, d = x.shape
    tm = min(tm, t)
    wd = y0.shape[2]
    tok = lambda n: pl.BlockSpec((1, tm, n), lambda bi, i: (bi, i, 0))
    return pl.pallas_call(
        _rwkv_out_body,
        grid=(b, t // tm),
        in_specs=[tok(wd), tok(wd), tok(wd), tok(wd), tok(wd), tok(d),
                  pl.BlockSpec((1, 1, d), lambda bi, i: (bi, 0, 0)),
                  pl.BlockSpec((1, wd), lambda bi, i: (0, 0)),
                  pl.BlockSpec((1, wd), lambda bi, i: (0, 0)),
                  pl.BlockSpec(w.shape, lambda bi, i: (0, 0))],
        out_specs=tok(d),
        out_shape=jax.ShapeDtypeStruct((b, t, d), F32),
        compiler_params=_cparams(("parallel", "parallel")),
        name="rwkv_out",
    )(y0, y1, yl, bn, gate, x, gl, gnw, gnb, w)


def _rope_tables(t):
    rows = t // GRID_W
    row = jnp.repeat(jnp.arange(rows, dtype=F32), GRID_W)
    col = jnp.tile(jnp.arange(GRID_W, dtype=F32), rows)
    inv = 1.0 / (ROPE_BASE ** (jnp.arange(ROPE_FREQS, dtype=F32) / ROPE_FREQS))
    ang = jnp.stack([row[:, None] * inv, col[:, None] * inv], axis=1)
    cos, sin = jnp.cos(ang), jnp.sin(ang)
    zeros = jnp.zeros_like(sin)
    ones_lo = jnp.ones((t, QK_NOPE), F32)
    pad_hi = HEAD_SLOT - QK_HEAD
    cos_t = jnp.concatenate([ones_lo, jnp.concatenate([cos, cos], axis=2).reshape(t, QK_ROPE),
                             jnp.ones((t, pad_hi), F32)], axis=1)
    sa = jnp.concatenate([jnp.zeros((t, QK_NOPE), F32),
                          jnp.concatenate([-sin, zeros], axis=2).reshape(t, QK_ROPE),
                          jnp.zeros((t, pad_hi), F32)], axis=1)
    sb = jnp.concatenate([jnp.zeros((t, QK_NOPE), F32),
                          jnp.concatenate([zeros, sin], axis=2).reshape(t, QK_ROPE),
                          jnp.zeros((t, pad_hi), F32)], axis=1)
    return cos_t, sa, sb


def _even_layer(x, ctx, mod_l, mod_c, need_ctx, g, w_in, kv_norm, q_norm, w_uq, w_ukv,
                q_head_norm, k_head_norm, w_fnet, w_out):
    b, s, d = x.shape
    tc = ctx.shape[1]
    e_q0 = KV_LORA + QK_ROPE
    e_f0 = e_q0 + Q_LORA
    e_g0 = e_f0 + FNET_GROUPS * FNET_GROUP_DIM
    w_p = jnp.concatenate([w_in[:, e_g0:], w_in[:, e_f0:e_g0], w_in[:, :e_q0],
                           jnp.zeros((d, LANE - QK_ROPE), F32), w_in[:, e_q0:e_f0]],
                          axis=1).astype(BF16)
    splits = (d, FNET_GROUPS * FNET_GROUP_DIM, KV_LORA + LANE + Q_LORA)
    kvw = w_ukv.reshape(KV_LORA, MLA_HEADS, QK_NOPE + V_HEAD)
    wk = jnp.pad(kvw[:, :, :QK_NOPE], ((0, 0), (0, 0), (0, HEAD_SLOT - QK_NOPE)))
    wk = wk.reshape(KV_LORA, MLA_HEADS * HEAD_SLOT).astype(BF16)
    wv = jnp.pad(kvw[:, :, QK_NOPE:], ((0, 0), (0, 0), (0, HEAD_SLOT - V_HEAD)))
    wv = wv.reshape(KV_LORA, MLA_HEADS * HEAD_SLOT).astype(BF16)
    wq = jnp.pad(w_uq.reshape(Q_LORA, MLA_HEADS, QK_HEAD), ((0, 0), (0, 0), (0, HEAD_SLOT - QK_HEAD)))
    wq = wq.reshape(Q_LORA, MLA_HEADS * HEAD_SLOT).astype(BF16)
    kg = jnp.pad(k_head_norm, (0, HEAD_SLOT - QK_HEAD)).reshape(1, HEAD_SLOT)
    qg = (jnp.pad(q_head_norm, (0, HEAD_SLOT - QK_HEAD))
          * (QK_HEAD ** -0.5 * math.log2(math.e))).reshape(1, HEAD_SLOT)
    kvn, qn = kv_norm.reshape(1, -1), q_norm.reshape(1, -1)
    g2 = g.reshape(1, d)
    bound = (1.02 * QK_HEAD * jnp.max(jnp.abs(qg)) * jnp.max(jnp.abs(kg))).astype(BF16).astype(F32)
    static_ok = bound <= MAX_STATIC_BOUND
    bias_lane = (jnp.arange(HEAD_SLOT) == BIAS_LANE).astype(F32).reshape(1, HEAD_SLOT)
    kb = bias_lane * jnp.where(static_ok, -bound, 0.0)
    qb = bias_lane

    gate_l, four_l, ua_l = _proj(x, g2, mod_l[1], mod_l[0], w_p, splits, 512)
    gate_c, four_c, ua_c = _proj(ctx, g2, mod_c[1], mod_c[0], w_p, splits, 512)
    sk = s + tc
    q_l, k_all, v_all = _qkv(ua_l, kvn, qn, wk, wv, wq, kg, qg, kb, qb, _rope_tables(s), True,
                             256, sk)
    dummy = jnp.zeros((tc, LANE), F32)
    q_c, k_all, v_all = _qkv(ua_c, kvn, qn, wk, wv, wq, kg, qg, kb, qb, (dummy, dummy, dummy),
                             False, 256, sk, kv_into=(k_all, v_all))
    bk = 768 if sk % 768 == 0 else tc
    o_l = _attention(q_l, k_all, v_all, static_ok, 2048, bk)
    f_l = _fourier_latent(four_l, w_fnet)
    wo = w_out.astype(BF16)
    x_new = _merge(o_l, f_l, gate_l, x, mod_l[2], wo, 512)
    ctx_new = ctx
    if need_ctx:
        o_c = _attention(q_c, k_all, v_all, static_ok, tc, tc, k_start=s, k_rows=tc)
        f_c = _fourier_dense(four_c, w_fnet)
        ctx_new = _merge(o_c, f_c, gate_c, ctx, mod_c[2], wo, 512)
    return x_new, ctx_new


def _odd_layer(x, ctx, mod_l, mod_c, need_ctx, g, w_in, shift_w, w0, w2, a0, a2, k_k, k_a, r_k,
               gn_w, gn_b, w_out):
    b, s, d = x.shape
    w = k_k.shape[0]
    npair = w // LANE
    o_wd0 = 2 * w
    o_r0 = o_wd0 + 2 * DECAY_LORA + 2 * AAA_LORA
    conv_ch = o_r0 + w
    perm = lambda m: jnp.concatenate([m[:, :o_wd0], m[:, o_r0:conv_ch], m[:, o_wd0:o_r0]], axis=1)
    w_p = jnp.concatenate([perm(w_in), w_in[:, conv_ch:]], axis=1).astype(BF16)
    sw = perm(shift_w)
    g2 = g.reshape(1, d)

    def pairs(vec2):
        return vec2.reshape(2, npair, LANE).transpose(1, 0, 2).reshape(npair, 1, 2 * LANE)

    def pair_mats(m):
        rr = m.shape[1]
        mp = m.reshape(2, rr, npair, LANE).transpose(2, 0, 1, 3)
        z = jnp.zeros_like(mp[:, 0])
        top = jnp.concatenate([mp[:, 0], z], axis=2)
        bot = jnp.concatenate([z, mp[:, 1]], axis=2)
        return jnp.concatenate([top, bot], axis=1).astype(BF16)

    w0p, a0p, w2p, a2p = pairs(w0), pairs(a0), pair_mats(w2), pair_mats(a2)
    kk2, ka2, rk2 = k_k.reshape(1, w), k_a.reshape(1, w), r_k.reshape(1, w)
    wo = w_out.astype(BF16)

    def mix(xin, mod, h0):
        z, gate = _proj_shift(xin, g2, mod[1], mod[0], w_p, sw, conv_ch, 256)
        mm, gg, qt, yl, bn = _rwkv_local(z, w0p, w2p, a0p, a2p, kk2, ka2, rk2)
        y0, y1, hfin = _rwkv_scan(mm, gg, qt, h0)
        return (y0, y1, yl, bn, gate), hfin

    h_zero = jnp.zeros((b, 2, npair, LANE, LANE), F32)
    parts_c, h_ctx = mix(ctx, mod_c, h_zero)
    parts_l, _ = mix(x, mod_l, h_ctx)
    x_new = _rwkv_out(*parts_l, x, mod_l[2], gn_w.reshape(1, w), gn_b.reshape(1, w), wo, 256)
    ctx_new = ctx
    if need_ctx:
        ctx_new = _rwkv_out(*parts_c, ctx, mod_c[2], gn_w.reshape(1, w), gn_b.reshape(1, w), wo, 256)
    return x_new, ctx_new


def kernel(x, c, ctx, c_ctx, ada_w, ada_b, norm_g, e_w_in, e_kv_norm, e_q_norm, e_w_uq, e_w_ukv,
           e_q_head_norm, e_k_head_norm, e_w_fnet, e_w_out, o_w_in, o_shift_w, o_w0, o_w2, o_a0,
           o_a2, o_k_k, o_k_a, o_r_k, o_gn_w, o_gn_b, o_w_out):
    b, s, d = x.shape
    depth = ada_w.shape[0]
    assert b + 1 <= 8
    cond8 = jnp.concatenate([c, c_ctx[None, :], jnp.zeros((8 - b - 1, d), F32)], axis=0)
    mod = _ada(cond8, ada_w, ada_b)
    for layer in range(depth):
        need_ctx = layer < depth - 1
        m = mod[layer]
        chunk = lambda rows, i: rows[:, None, i * d:(i + 1) * d]
        lat, cx = m[:b], jnp.broadcast_to(m[b:b + 1], (b, 3 * d))
        mod_l = (chunk(lat, 0), 1.0 + chunk(lat, 1), chunk(lat, 2))
        mod_c = (chunk(cx, 0), 1.0 + chunk(cx, 1), chunk(cx, 2))
        j = layer // 2
        if layer % 2 == 0:
            x, ctx = _even_layer(x, ctx, mod_l, mod_c, need_ctx, norm_g[layer], e_w_in[j],
                                 e_kv_norm[j], e_q_norm[j], e_w_uq[j], e_w_ukv[j],
                                 e_q_head_norm[j], e_k_head_norm[j], e_w_fnet[j], e_w_out[j])
        else:
            x, ctx = _odd_layer(x, ctx, mod_l, mod_c, need_ctx, norm_g[layer], o_w_in[j],
                                o_shift_w[j], o_w0[j], o_w2[j], o_a0[j], o_a2[j], o_k_k[j],
                                o_k_a[j], o_r_k[j].reshape(-1), o_gn_w[j], o_gn_b[j], o_w_out[j])
    return x
```

```python
import functools
import math

import numpy as np
import jax
import jax.numpy as jnp
from jax import lax
from jax.experimental import pallas as pl
from jax.experimental.pallas import tpu as pltpu

F32 = jnp.float32
BF16 = jnp.bfloat16
ACT = BF16

GRID_W = 64
NORM_EPS = 1e-6
MLA_HEADS = 8
QK_NOPE = 64
QK_ROPE = 32
QK_HEAD = QK_NOPE + QK_ROPE
V_HEAD = 64
Q_LORA = 384
KV_LORA = 256
ROPE_FREQS = QK_ROPE // 4
ROPE_BASE = 10000.0
FNET_GROUPS = 4
FNET_GROUP_DIM = 128
RWKV_HEAD = 64
DECAY_LORA = 64
AAA_LORA = 64
GN_EPS = 64e-5

LANE = 128
CHUNK = 64
HEAD_SLOT = 128
VMEM_LIMIT = 56 * 1024 * 1024

NN = (((1,), (0,)), ((), ()))
NT = (((1,), (1,)), ((), ()))
TN = (((0,), (0,)), ((), ()))


def _cparams(sem):
    return pltpu.CompilerParams(dimension_semantics=sem, vmem_limit_bytes=VMEM_LIMIT)


def _mm(a, b, dn=NN):
    return lax.dot_general(a.astype(BF16), b.astype(BF16), dn, preferred_element_type=F32)


def _split(a):
    hi = a.astype(BF16)
    lo = (a - hi.astype(F32)).astype(BF16)
    return hi, lo


def _mm3(a, b, dn=NN):
    ah, al = _split(a)
    bh, bl = _split(b)
    d = lambda x, y: lax.dot_general(x, y, dn, preferred_element_type=F32)
    return d(ah, bh) + d(al, bh) + d(ah, bl)


def _mm2r(a, b_exact):
    ah, al = _split(a)
    bb = b_exact.astype(BF16)
    d = lambda x: lax.dot_general(x, bb, NN, preferred_element_type=F32)
    return d(ah) + d(al)


def _mm2l(a_exact, b):
    bh, bl = _split(b)
    aa = a_exact.astype(BF16)
    d = lambda y: lax.dot_general(aa, y, NN, preferred_element_type=F32)
    return d(bh) + d(bl)


def _sigmoid(x):
    return 1.0 / (1.0 + jnp.exp(-x))


def _modnorm(x, g, sc1, sh):
    y = x * lax.rsqrt(jnp.mean(x * x, axis=-1, keepdims=True) + NORM_EPS)
    return (y * g) * sc1 + sh


def _iota2(shape, dim):
    return lax.broadcasted_iota(jnp.int32, shape, dim)


def _ada_body(c_ref, w_ref, b_ref, o_ref):
    c = c_ref[...]
    s = c * _sigmoid(c)
    o_ref[0] = _mm3(s, w_ref[0]) + b_ref[0]


def _ada(cond8, ada_w, ada_b):
    depth, d, n = ada_w.shape
    tn = 512
    return pl.pallas_call(
        _ada_body,
        grid=(depth, n // tn),
        in_specs=[
            pl.BlockSpec((8, d), lambda l, j: (0, 0)),
            pl.BlockSpec((1, d, tn), lambda l, j: (l, 0, j)),
            pl.BlockSpec((1, 1, tn), lambda l, j: (l, 0, j)),
        ],
        out_specs=pl.BlockSpec((1, 8, tn), lambda l, j: (l, 0, j)),
        out_shape=jax.ShapeDtypeStruct((depth, 8, n), F32),
        compiler_params=_cparams(("parallel", "parallel")),
        name="ada",
    )(cond8, ada_w, ada_b.reshape(depth, 1, n))


COL_CHUNK = 512


def _proj_body(x_ref, g_ref, sc_ref, sh_ref, w_ref, *o_refs, splits):
    h = _modnorm(x_ref[0], g_ref[...], sc_ref[0], sh_ref[0]).astype(BF16)
    off = 0
    for o_ref, n in zip(o_refs, splits):
        for c0 in range(0, n, COL_CHUNK):
            c1 = min(n, c0 + COL_CHUNK)
            o_ref[0, :, c0:c1] = jnp.dot(h, w_ref[:, off + c0:off + c1],
                                         preferred_element_type=F32).astype(o_ref.dtype)
        off += n


def _proj(x, g, sc1, sh, w, splits, tm):
    b, t, d = x.shape
    tm = min(tm, t)
    n = w.shape[1]
    vec = pl.BlockSpec((1, 1, d), lambda bi, i: (bi, 0, 0))
    return pl.pallas_call(
        functools.partial(_proj_body, splits=splits),
        grid=(b, t // tm),
        in_specs=[
            pl.BlockSpec((1, tm, d), lambda bi, i: (bi, i, 0)),
            pl.BlockSpec((1, d), lambda bi, i: (0, 0)),
            vec, vec,
            pl.BlockSpec((d, n), lambda bi, i: (0, 0)),
        ],
        out_specs=[pl.BlockSpec((1, tm, s), lambda bi, i: (bi, i, 0)) for s in splits],
        out_shape=[jax.ShapeDtypeStruct((b, t, s), ACT) for s in splits],
        compiler_params=_cparams(("parallel", "parallel")),
        name="proj",
    )(x, g, sc1, sh, w)


HALO = 16


def _proj_shift_body(x_ref, xp_ref, xn_ref, g_ref, sc_ref, sh_ref, w_ref, sw_ref, z_ref, gate_ref,
                     *, tm, n_conv):
    i = pl.program_id(1)
    last = pl.num_programs(1) - 1
    g, sc1, sh = g_ref[...], sc_ref[0], sh_ref[0]
    h = _modnorm(x_ref[0], g, sc1, sh)
    hp = _modnorm(xp_ref[0], g, sc1, sh) * (i > 0).astype(F32)
    hn = _modnorm(xn_ref[0], g, sc1, sh) * (i < last).astype(F32)
    hb = jnp.concatenate([hp, h, hn], axis=0).astype(BF16)
    rows = tm + 2 * HALO
    for c0 in range(0, n_conv, COL_CHUNK):
        c1 = min(n_conv, c0 + COL_CHUNK)
        u = jnp.dot(hb, w_ref[:, c0:c1], preferred_element_type=F32)
        up = pltpu.roll(u, 1, 0)[HALO:HALO + tm]
        un = pltpu.roll(u, rows - 1, 0)[HALO:HALO + tm]
        um = u[HALO:HALO + tm]
        z_ref[0, :, c0:c1] = (sw_ref[0:1, c0:c1] * up + sw_ref[1:2, c0:c1] * um
                              + sw_ref[2:3, c0:c1] * un).astype(z_ref.dtype)
    hc = hb[HALO:HALO + tm]
    n_all = w_ref.shape[1]
    for c0 in range(n_conv, n_all, COL_CHUNK):
        c1 = min(n_all, c0 + COL_CHUNK)
        gate_ref[0, :, c0 - n_conv:c1 - n_conv] = jnp.dot(
            hc, w_ref[:, c0:c1], preferred_element_type=F32).astype(gate_ref.dtype)


def _proj_shift(x, g, sc1, sh, w, sw, n_conv, tm):
    b, t, d = x.shape
    tm = min(tm, t)
    n = w.shape[1]
    hb = tm // HALO
    nhb = t // HALO
    vec = pl.BlockSpec((1, 1, d), lambda bi, i: (bi, 0, 0))
    return pl.pallas_call(
        functools.partial(_proj_shift_body, tm=tm, n_conv=n_conv),
        grid=(b, t // tm),
        in_specs=[
            pl.BlockSpec((1, tm, d), lambda bi, i: (bi, i, 0)),
            pl.BlockSpec((1, HALO, d), lambda bi, i: (bi, jnp.maximum(i * hb - 1, 0), 0)),
            pl.BlockSpec((1, HALO, d), lambda bi, i: (bi, jnp.minimum((i + 1) * hb, nhb - 1), 0)),
            pl.BlockSpec((1, d), lambda bi, i: (0, 0)),
            vec, vec,
            pl.BlockSpec((d, n), lambda bi, i: (0, 0)),
            pl.BlockSpec((3, n_conv), lambda bi, i: (0, 0)),
        ],
        out_specs=[pl.BlockSpec((1, tm, n_conv), lambda bi, i: (bi, i, 0)),
                   pl.BlockSpec((1, tm, n - n_conv), lambda bi, i: (bi, i, 0))],
        out_shape=[jax.ShapeDtypeStruct((b, t, n_conv), ACT),
                   jax.ShapeDtypeStruct((b, t, n - n_conv), ACT)],
        compiler_params=_cparams(("parallel", "parallel")),
        name="proj_shift",
    )(x, x, x, g, sc1, sh, w, sw)


def _rms(x, g):
    return x * lax.rsqrt(jnp.mean(x * x, axis=-1, keepdims=True) + NORM_EPS) * g


def _qkv_body(ua_ref, kvn_ref, qn_ref, wk_ref, wv_ref, wq_ref, kg_ref, qg_ref, kb_ref, qb_ref,
              cos_ref, sa_ref, sb_ref, *rest, rope):
    q_ref, k_ref, v_ref = rest[-3:]
    ua = ua_ref[0].astype(F32)
    ckv = _rms(ua[:, :KV_LORA], kvn_ref[...]).astype(BF16)
    kr = ua[:, KV_LORA:KV_LORA + LANE]
    cq = _rms(ua[:, KV_LORA + LANE:], qn_ref[...]).astype(BF16)
    kn = jnp.dot(ckv, wk_ref[...], preferred_element_type=F32)
    vv = jnp.dot(ckv, wv_ref[...], preferred_element_type=F32)
    qq = jnp.dot(cq, wq_ref[...], preferred_element_type=F32)
    ones_hi = (_iota2((1, HEAD_SLOT), 1) >= V_HEAD).astype(F32)
    pe = pltpu.roll(kr, QK_NOPE, 1)
    kg, qg = kg_ref[...], qg_ref[...]
    inv_n = 1.0 / QK_HEAD

    def finish(xh, gain):
        xh = xh * lax.rsqrt(jnp.sum(xh * xh, axis=-1, keepdims=True) * inv_n + NORM_EPS) * gain
        if rope:
            xh = (xh * cos_ref[...] + pltpu.roll(xh, LANE - ROPE_FREQS, 1) * sa_ref[...]
                  + pltpu.roll(xh, ROPE_FREQS, 1) * sb_ref[...])
        return xh

    for h in range(MLA_HEADS):
        sl = slice(h * HEAD_SLOT, (h + 1) * HEAD_SLOT)
        k_ref[0, h] = (finish(kn[:, sl] + pe, kg) + kb_ref[...]).astype(BF16)
        q_ref[0, h] = (finish(qq[:, sl], qg) + qb_ref[...]).astype(BF16)
        v_ref[0, h] = (vv[:, sl] + ones_hi).astype(BF16)


def _qkv(ua, kvn, qn, wk, wv, wq, kg, qg, kb, qb, tabs, rope, tm, kv_rows, kv_into=None):
    b, t, wa = ua.shape
    tm = min(tm, t)
    full = lambda a: pl.BlockSpec(a.shape, lambda bi, i: (0,) * a.ndim)
    tab = pl.BlockSpec((tm, LANE), lambda bi, i: (i, 0))
    off = 0 if kv_into is None else (kv_rows - t) // tm
    head_blk = (1, MLA_HEADS, tm, HEAD_SLOT)
    kv_spec = pl.BlockSpec(head_blk, lambda bi, i: (bi, 0, off + i, 0))
    kv_shape = jax.ShapeDtypeStruct((b, MLA_HEADS, kv_rows, HEAD_SLOT), BF16)
    n_in = 13
    extra, extra_specs, aliases = (), [], {}
    if kv_into is not None:
        extra = tuple(kv_into)
        extra_specs = [pl.BlockSpec(memory_space=pl.ANY)] * 2
        aliases = {n_in: 1, n_in + 1: 2}
    return pl.pallas_call(
        functools.partial(_qkv_body, rope=rope),
        grid=(b, t // tm),
        in_specs=[pl.BlockSpec((1, tm, wa), lambda bi, i: (bi, i, 0)),
                  full(kvn), full(qn), full(wk), full(wv), full(wq), full(kg), full(qg),
                  full(kb), full(qb), tab, tab, tab] + extra_specs,
        out_specs=[pl.BlockSpec(head_blk, lambda bi, i: (bi, 0, i, 0)), kv_spec, kv_spec],
        out_shape=[jax.ShapeDtypeStruct((b, MLA_HEADS, t, HEAD_SLOT), BF16), kv_shape, kv_shape],
        input_output_aliases=aliases,
        compiler_params=_cparams(("parallel", "parallel")),
        name="qkv",
    )(ua, kvn, qn, wk, wv, wq, kg, qg, kb, qb, *tabs, *extra)


BIAS_LANE = QK_HEAD
MAX_STATIC_BOUND = 50.0


def _attn_finish(acc_ref, o_ref):
    bq = acc_ref.shape[1]
    lane = _iota2((bq, LANE), 1)
    o0 = acc_ref[0] / pltpu.roll(acc_ref[0], V_HEAD, 1)
    o1 = acc_ref[1] / pltpu.roll(acc_ref[1], V_HEAD, 1)
    o_ref[0] = jnp.where(lane < V_HEAD, o0, pltpu.roll(o1, V_HEAD, 1)).astype(o_ref.dtype)


def _attn_static_body(q_ref, k_ref, v_ref, o_ref, acc_ref):
    j = pl.program_id(3)

    @pl.when(j == 0)
    def _():
        acc_ref[...] = jnp.zeros(acc_ref.shape, F32)

    s = [lax.dot_general(q_ref[0, hh], k_ref[0, hh], NT, preferred_element_type=F32)
         for hh in range(2)]
    p = [jnp.exp2(x.astype(BF16)) for x in s]
    for hh in range(2):
        acc_ref[hh] += jnp.dot(p[hh], v_ref[0, hh], preferred_element_type=F32)

    @pl.when(j == pl.num_programs(3) - 1)
    def _():
        _attn_finish(acc_ref, o_ref)


def _attn_online_body(q_ref, k_ref, v_ref, o_ref, m_ref, acc_ref):
    j = pl.program_id(3)

    @pl.when(j == 0)
    def _():
        m_ref[...] = jnp.full(m_ref.shape, -jnp.inf, F32)
        acc_ref[...] = jnp.zeros(acc_ref.shape, F32)

    for hh in range(2):
        s = lax.dot_general(q_ref[0, hh], k_ref[0, hh], NT, preferred_element_type=F32)
        m_prev = m_ref[hh]
        m_new = jnp.maximum(m_prev, jnp.max(s, axis=-1, keepdims=True))
        p = jnp.exp2(s - m_new)
        alpha = jnp.exp2(m_prev - m_new)
        acc_ref[hh] = alpha * acc_ref[hh] + jnp.dot(p.astype(BF16), v_ref[0, hh],
                                                     preferred_element_type=F32)
        m_ref[hh] = m_new

    @pl.when(j == pl.num_programs(3) - 1)
    def _():
        _attn_finish(acc_ref, o_ref)


def _attention(q, k, v, static_ok, bq, bk, k_start=0, k_rows=None):
    b, h, s, e = q.shape
    sk = k.shape[2] if k_rows is None else k_rows
    bq, bk = min(bq, s), min(bk, sk)
    j0 = k_start // bk

    def call(body, scratch, name):
        return pl.pallas_call(
            body,
            grid=(b, h // 2, s // bq, sk // bk),
            in_specs=[pl.BlockSpec((1, 2, bq, e), lambda bi, p, i, j: (bi, p, i, 0)),
                      pl.BlockSpec((1, 2, bk, e), lambda bi, p, i, j: (bi, p, j0 + j, 0)),
                      pl.BlockSpec((1, 2, bk, e), lambda bi, p, i, j: (bi, p, j0 + j, 0))],
            out_specs=pl.BlockSpec((1, bq, 2 * V_HEAD), lambda bi, p, i, j: (bi, i, p)),
            out_shape=jax.ShapeDtypeStruct((b, s, h * V_HEAD), ACT),
            scratch_shapes=scratch,
            compiler_params=_cparams(("parallel", "parallel", "parallel", "arbitrary")),
            name=name,
        )(q, k, v)

    acc = pltpu.VMEM((2, bq, LANE), F32)
    return lax.cond(
        static_ok,
        lambda: call(_attn_static_body, [acc], "attention"),
        lambda: call(_attn_online_body, [pltpu.VMEM((2, bq, 1), F32), acc], "attention_online"))


def _dft_mats(n):
    idx = np.arange(n)
    ang = 2.0 * np.pi * ((idx[:, None] * idx[None, :]) % n) / n
    return np.cos(ang), np.sin(ang)


def _hilo(a):
    a = jnp.asarray(a, F32)
    hi = a.astype(BF16)
    return hi, (a - hi.astype(F32)).astype(BF16)


def _mm3c(ah, al, b, dn=NN):
    bh, bl = _split(b)
    d = lambda x, y: lax.dot_general(x, y, dn, preferred_element_type=F32)
    return d(ah, bh) + d(al, bh) + d(ah, bl)


def _four_rows_body(x_ref, wh_ref, wl_ref, tc_ref, ts_ref, o_ref):
    r = tc_ref.shape[0]
    x = x_ref[0, 0]
    d = lambda a: jnp.dot(a, x, preferred_element_type=F32)
    y = d(wh_ref[...]) + d(wl_ref[...])
    yc, ys = y[:r], y[r:]
    tc, ts = tc_ref[...], ts_ref[...]
    o_ref[0, 0, :r] = yc * tc - ys * ts
    o_ref[0, 0, r:] = yc * ts + ys * tc


def _four_cols_body(y_ref, wh_ref, wl_ref, ch_ref, cl_ref, wf_ref, o_ref, y3_scr, *, krt, scale):
    gd = FNET_GROUP_DIM

    def one(j, carry):
        rows = pl.ds(pl.multiple_of(j * GRID_W, GRID_W), GRID_W)
        ycs = jnp.concatenate([y_ref[0, 0, 0, rows, :], y_ref[0, 0, 1, rows, :]], axis=0)
        y3 = _mm3c(wh_ref[...], wl_ref[...], ycs)
        y3_scr[rows, :] = jnp.concatenate([y3[:GRID_W], y3[GRID_W:]], axis=1)
        return carry

    lax.fori_loop(0, krt, one, 0)
    y3 = y3_scr[...]
    yh, yl = _split(y3)
    d = lambda a, b: jnp.dot(a, b, preferred_element_type=F32)
    f = (d(yh, ch_ref[...]) + d(yl, ch_ref[...]) + d(yh, cl_ref[...])) * scale
    o_ref[0, 0] = _mm3(f, wf_ref[0]).astype(o_ref.dtype)


def _fourier_latent(xf, w_fnet):
    b, t, _ = xf.shape
    g, gd = FNET_GROUPS, FNET_GROUP_DIM
    r = t // GRID_W
    wide = GRID_W * gd
    xv = xf.reshape(b, r, GRID_W, g, gd).transpose(0, 3, 1, 2, 4).reshape(b, g, r, wide)
    cr, sr = _dft_mats(r)
    wh, wl = _hilo(np.concatenate([cr, sr], axis=0))
    kr_i, c_i = np.arange(r)[:, None], np.arange(GRID_W)[None, :]
    ang = 2.0 * np.pi * ((kr_i * c_i) % t) / t
    twc = jnp.repeat(jnp.asarray(np.cos(ang), F32), gd, axis=1)
    tws = jnp.repeat(jnp.asarray(np.sin(ang), F32), gd, axis=1)
    tl = min(2048, wide)
    y2 = pl.pallas_call(
        _four_rows_body,
        grid=(b, g, wide // tl),
        in_specs=[pl.BlockSpec((1, 1, r, tl), lambda bi, gi, l: (bi, gi, 0, l)),
                  pl.BlockSpec((2 * r, r), lambda bi, gi, l: (0, 0)),
                  pl.BlockSpec((2 * r, r), lambda bi, gi, l: (0, 0)),
                  pl.BlockSpec((r, tl), lambda bi, gi, l: (0, l)),
                  pl.BlockSpec((r, tl), lambda bi, gi, l: (0, l))],
        out_specs=pl.BlockSpec((1, 1, 2 * r, tl), lambda bi, gi, l: (bi, gi, 0, l)),
        out_shape=jax.ShapeDtypeStruct((b, g, 2 * r, wide), F32),
        compiler_params=_cparams(("parallel", "parallel", "parallel")),
        name="fourier_rows",
    )(xv, wh, wl, twc, tws)
    y2v = y2.reshape(b, g, 2, r * GRID_W, gd)
    c64, s64 = _dft_mats(GRID_W)
    w64h, w64l = _hilo(np.block([[c64, -s64], [s64, c64]]))
    cc, sc = _dft_mats(gd)
    ch, cl = _hilo(np.concatenate([cc, -sc], axis=0))
    krt = min(32, r)
    const = lambda a: pl.BlockSpec(a.shape, lambda bi, gi, i: (0, 0))
    fo = pl.pallas_call(
        functools.partial(_four_cols_body, krt=krt, scale=1.0 / math.sqrt(t * gd)),
        grid=(b, g, r // krt),
        in_specs=[pl.BlockSpec((1, 1, 2, krt * GRID_W, gd), lambda bi, gi, i: (bi, gi, 0, i, 0)),
                  const(w64h), const(w64l), const(ch), const(cl),
                  pl.BlockSpec((1, gd, gd), lambda bi, gi, i: (gi, 0, 0))],
        out_specs=pl.BlockSpec((1, 1, krt * GRID_W, gd), lambda bi, gi, i: (bi, gi, i, 0)),
        out_shape=jax.ShapeDtypeStruct((b, g, r * GRID_W, gd), ACT),
        scratch_shapes=[pltpu.VMEM((krt * GRID_W, 2 * gd), F32)],
        compiler_params=_cparams(("parallel", "parallel", "parallel")),
        name="fourier_cols",
    )(y2v, w64h, w64l, ch, cl, w_fnet)
    return fo.reshape(b, g, r, GRID_W, gd).transpose(0, 3, 2, 1, 4).reshape(b, t, g * gd)


def _four_dense_body(x_ref, ch_ref, cl_ref, th_ref, tl_ref, sh_ref, sl_ref, wf_ref, o_ref, *, scale):
    x = x_ref[0]
    xh, xl = _split(x)
    d = lambda a, b: jnp.dot(a, b, preferred_element_type=F32)
    z = d(xh, ch_ref[...]) + d(xl, ch_ref[...]) + d(xh, cl_ref[...])
    zc, zs = z[:, :FNET_GROUP_DIM], z[:, FNET_GROUP_DIM:]
    f = (_mm3c(th_ref[...], tl_ref[...], zc) - _mm3c(sh_ref[...], sl_ref[...], zs)) * scale
    o_ref[0] = _mm3(f, wf_ref[0]).astype(o_ref.dtype)


def _fourier_dense(xf, w_fnet):
    b, t, _ = xf.shape
    g, gd = FNET_GROUPS, FNET_GROUP_DIM
    cc, sc = _dft_mats(gd)
    ch, cl = _hilo(np.concatenate([cc, sc], axis=1))
    ct, st = _dft_mats(t)
    cth, ctl = _hilo(ct)
    sth, stl = _hilo(st)
    sq = pl.BlockSpec((t, t), lambda bi, gi: (0, 0))
    cs = pl.BlockSpec((gd, 2 * gd), lambda bi, gi: (0, 0))
    return pl.pallas_call(
        functools.partial(_four_dense_body, scale=1.0 / math.sqrt(t * gd)),
        grid=(b, g),
        in_specs=[pl.BlockSpec((1, t, gd), lambda bi, gi: (bi, 0, gi)), cs, cs, sq, sq, sq, sq,
                  pl.BlockSpec((1, gd, gd), lambda bi, gi: (gi, 0, 0))],
        out_specs=pl.BlockSpec((1, t, gd), lambda bi, gi: (bi, 0, gi)),
        out_shape=jax.ShapeDtypeStruct((b, t, g * gd), ACT),
        compiler_params=_cparams(("parallel", "parallel")),
        name="fourier_dense",
    )(xf, ch, cl, cth, ctl, sth, stl, w_fnet)


def _merge_body(o_ref, f_ref, gate_ref, x_ref, gl_ref, w_ref, out_ref):
    gt = gate_ref[0].astype(F32)
    mix = jnp.concatenate([o_ref[0], f_ref[0]], axis=-1).astype(F32) * (gt * _sigmoid(gt))
    y = jnp.dot(mix.astype(BF16), w_ref[...], preferred_element_type=F32)
    out_ref[0] = x_ref[0] + gl_ref[0] * y


def _merge(o, f, gate, x, gl, w, tm):
    b, t, d = x.shape
    tm = min(tm, t)
    half = o.shape[2]
    tok = lambda n: pl.BlockSpec((1, tm, n), lambda bi, i: (bi, i, 0))
    return pl.pallas_call(
        _merge_body,
        grid=(b, t // tm),
        in_specs=[tok(half), tok(half), tok(d), tok(d),
                  pl.BlockSpec((1, 1, d), lambda bi, i: (bi, 0, 0)),
                  pl.BlockSpec(w.shape, lambda bi, i: (0, 0))],
        out_specs=tok(d),
        out_shape=jax.ShapeDtypeStruct((b, t, d), F32),
        compiler_params=_cparams(("parallel", "parallel")),
        name="merge",
    )(o, f, gate, x, gl, w)


EXP_M05 = math.exp(-0.5)


PAIRS_PER_STEP = 8


def _rwkv_local_body(zk_ref, zv_ref, zr_ref, zwa_ref, w0_ref, w2_ref, a0_ref, a2_ref,
                     kk_ref, ka_ref, rk_ref, m_ref, g_ref, qt_ref, yl_ref, bn_ref):
    c = CHUNK
    zwa = zwa_ref[0].astype(F32)
    lora_w, lora_a = jnp.tanh(zwa[:, :LANE]), zwa[:, LANE:]

    lane = _iota2((1, LANE), 1)
    m0 = (lane < RWKV_HEAD).astype(F32)
    m1 = 1.0 - m0
    r2 = _iota2((LANE, LANE), 0)
    c2 = _iota2((LANE, LANE), 1)
    same = (r2 // RWKV_HEAD) == (c2 // RWKV_HEAD)
    ones_bd = same.astype(F32)
    eye = r2 == c2
    eye_f = eye.astype(F32)
    rc = _iota2((c, c), 0)
    cc = _iota2((c, c), 1)
    stack = lambda x: jnp.concatenate([x * m0, x * m1], axis=0)
    fold = lambda x: x[:c] + x[c:]

    pairs = range(PAIRS_PER_STEP)
    chains = [(q, d) for q in pairs for d in range(2)]
    qls = [slice(q * LANE, (q + 1) * LANE) for q in pairs]
    ks = [zk_ref[0, :, ql].astype(F32) for ql in qls]
    vs_ = [zv_ref[0, :, ql].astype(F32) for ql in qls]
    rs = [zr_ref[0, :, ql].astype(F32) for ql in qls]
    wraw = [_mm(lora_w, w2_ref[q]) + w0_ref[q] for q in pairs]
    araw = [_mm(lora_a, a2_ref[q]) + a0_ref[q] for q in pairs]
    logw = [-EXP_M05 * _sigmoid(w) for w in wraw]
    a_all = [_sigmoid(a) for a in araw]
    kk0 = [ks[q] * kk_ref[:, qls[q]] for q in pairs]
    ss = [_mm2r(x * x, ones_bd) for x in kk0]
    kk = [kk0[q] / jnp.maximum(jnp.sqrt(ss[q]), 1e-12) for q in pairs]
    vstk = [stack(v) for v in vs_]

    dsl = [slice(d * LANE, (d + 1) * LANE) for d in range(2)]
    lw = [logw[q][:, dsl[d]] for q, d in chains]
    ad = [a_all[q][:, dsl[d]] for q, d in chains]
    kd = [ks[q] * (1.0 + (ad[i] - 1.0) * ka_ref[:, qls[q]]) for i, (q, d) in enumerate(chains)]
    bb = [kk[q] * ad[i] for i, (q, d) in enumerate(chains)]
    bonus = [_mm2r(rs[q] * kd[i] * rk_ref[:, qls[q]], ones_bd) * vs_[q]
             for i, (q, d) in enumerate(chains)]
    tri_f = (cc <= rc).astype(BF16)
    tri_r = (cc >= rc).astype(BF16)
    strict = [same & (c2 < r2), same & (c2 > r2)]
    incl = [same & (c2 <= r2), same & (c2 >= r2)]

    def cumsum(x, tri):
        xh, xl = _split(x)
        xll = (x - xh.astype(F32) - xl.astype(F32)).astype(BF16)
        dd = lambda y: jnp.dot(tri, y, preferred_element_type=F32)
        return dd(xh) + dd(xl) + dd(xll)

    lc = [cumsum(lw[i], tri_f if d == 0 else tri_r) for i, (q, d) in enumerate(chains)]
    ltot = [lc[i][c - 1:c] if d == 0 else lc[i][0:1] for i, (q, d) in enumerate(chains)]
    kkd_s = [stack(kk[q] * jnp.exp(lc[i] - lw[i])) for i, (q, d) in enumerate(chains)]
    rd_s = [stack(rs[q] * jnp.exp(lc[i])) for i, (q, d) in enumerate(chains)]
    e_inv = [jnp.exp(-x) for x in lc]
    inv_s = [jnp.concatenate([stack(bb[i] * e_inv[i]), stack(kd[i] * e_inv[i])], axis=0)
             for i in range(len(chains))]
    amat = [_mm(jnp.concatenate([kkd_s[i], rd_s[i]], axis=0), inv_s[i], NT)
            for i in range(len(chains))]
    a_kb = [jnp.where(strict[d], amat[i][:LANE, :LANE], 0.0).astype(BF16).astype(F32)
            for i, (q, d) in enumerate(chains)]
    a_kk = [jnp.where(strict[d], amat[i][:LANE, LANE:], 0.0) for i, (q, d) in enumerate(chains)]
    aq_b = [jnp.where(incl[d], amat[i][LANE:, :LANE], 0.0) for i, (q, d) in enumerate(chains)]
    aq_k = [jnp.where(incl[d], amat[i][LANE:, LANE:], 0.0) for i, (q, d) in enumerate(chains)]
    av = [_mm(jnp.concatenate([a_kk[i], aq_k[i]], axis=0), vstk[q])
          for i, (q, d) in enumerate(chains)]
    tinv = [eye_f - a for a in a_kb]
    qpow = [_mm(a, a) for a in a_kb]
    for _ in range(4):
        prod = [_mm(qpow[i], jnp.concatenate([qpow[i], tinv[i]], axis=1))
                for i in range(len(chains))]
        qpow = [x[:, :LANE] for x in prod]
        tinv = [tinv[i] + prod[i][:, LANE:] for i in range(len(chains))]
    tinv = [tinv[i] + _mm(qpow[i], tinv[i]) for i in range(len(chains))]
    resid = [eye_f - _mm2l(eye_f + a_kb[i], tinv[i]) for i in range(len(chains))]
    tinv = [tinv[i] + _mm(tinv[i], resid[i]) for i in range(len(chains))]
    x = [_mm(tinv[i], jnp.concatenate([kkd_s[i], av[i][:LANE]], axis=1))
         for i in range(len(chains))]
    qy = [jnp.concatenate([rd_s[i], av[i][LANE:]], axis=1) - _mm(aq_b[i], x[i])
          for i in range(len(chains))]
    e_end = [jnp.exp(ltot[i] - lc[i]) for i in range(len(chains))]
    bx = [_mm(stack(bb[i] * e_end[i]), x[i], TN) for i in range(len(chains))]
    kv = [_mm(stack(kd[i] * e_end[i]), vstk[q], TN) for i, (q, d) in enumerate(chains)]
    for i, (q, d) in enumerate(chains):
        m_ref[0, 0, d, q] = fold(jnp.where(eye, jnp.exp(ltot[i]), 0.0)
                                 - bx[i][:, :LANE]).astype(BF16)
        g_ref[0, 0, d, q] = fold(kv[i] - bx[i][:, LANE:])
        qt_ref[0, d, :, qls[q]] = fold(qy[i][:, :LANE]).astype(BF16)
    for q in pairs:
        yl_ref[0, :, qls[q]] = (fold(qy[2 * q][:, LANE:])
                                + fold(qy[2 * q + 1][:, LANE:])).astype(yl_ref.dtype)
        bn_ref[0, :, qls[q]] = (bonus[2 * q] + bonus[2 * q + 1]).astype(bn_ref.dtype)


def _rwkv_local(z, w0p, w2p, a0p, a2p, k_k, k_a, r_k):
    b, t, _ = z.shape
    w = k_k.shape[1]
    npair = w // LANE
    pp = PAIRS_PER_STEP
    ng = npair // pp
    wl = pp * LANE
    nc = t // CHUNK
    tokc = lambda base: pl.BlockSpec((1, CHUNK, wl), lambda bi, ci, p: (bi, ci, base + p))
    perp3 = lambda n: pl.BlockSpec((pp, n, 2 * LANE), lambda bi, ci, p: (p, 0, 0))
    vecp = pl.BlockSpec((1, wl), lambda bi, ci, p: (0, p))
    mat = pl.BlockSpec((1, 1, 2, pp, CHUNK, LANE), lambda bi, ci, p: (bi, ci, 0, p, 0, 0))
    return pl.pallas_call(
        _rwkv_local_body,
        grid=(b, nc, ng),
        in_specs=[tokc(0), tokc(ng), tokc(2 * ng),
                  pl.BlockSpec((1, CHUNK, 2 * LANE), lambda bi, ci, p: (bi, ci, 3 * npair // 2)),
                  perp3(1), perp3(LANE), perp3(1), perp3(LANE), vecp, vecp, vecp],
        out_specs=[mat, mat,
                   pl.BlockSpec((1, 2, CHUNK, wl), lambda bi, ci, p: (bi, 0, ci, p)),
                   pl.BlockSpec((1, CHUNK, wl), lambda bi, ci, p: (bi, ci, p)),
                   pl.BlockSpec((1, CHUNK, wl), lambda bi, ci, p: (bi, ci, p))],
        out_shape=[jax.ShapeDtypeStruct((b, nc, 2, npair, CHUNK, LANE), BF16),
                   jax.ShapeDtypeStruct((b, nc, 2, npair, CHUNK, LANE), F32),
                   jax.ShapeDtypeStruct((b, 2, t, w), BF16),
                   jax.ShapeDtypeStruct((b, t, w), ACT),
                   jax.ShapeDtypeStruct((b, t, w), ACT)],
        compiler_params=_cparams(("parallel", "parallel", "parallel")),
        name="rwkv_local",
    )(z, z, z, z, w0p, w2p, a0p, a2p, k_k, k_a, r_k)


SCAN_CHUNKS = 4


def _rwkv_scan_body(m0_ref, g0_ref, q0_ref, m1_ref, g1_ref, q1_ref, h0_ref,
                    y0_ref, y1_ref, hfin_ref, h_scr, *, npair, cs):
    ci = pl.program_id(1)

    @pl.when(ci == 0)
    def _():
        h_scr[...] = h0_ref[0]

    head0 = _iota2((1, LANE), 1) < RWKV_HEAD

    def expand(x):
        z = jnp.zeros_like(x)
        return jnp.concatenate([jnp.where(head0, x, z), jnp.where(head0, z, x)], axis=0)

    refs = ((m0_ref, g0_ref, q0_ref, y0_ref), (m1_ref, g1_ref, q1_ref, y1_ref))
    chains = [(d, p) for d in range(2) for p in range(npair)]
    lanes = [slice(p * LANE, (p + 1) * LANE) for p in range(npair)]
    h = [h_scr[d, p] for d, p in chains]
    for step in range(cs):
        ck = (step, cs - 1 - step)
        rows = [slice(c * CHUNK, (c + 1) * CHUNK) for c in ck]
        hb = [x.astype(BF16) for x in h]
        ys = [jnp.dot(refs[d][2][0, 0, rows[d], lanes[p]], hb[i], preferred_element_type=F32)
              for i, (d, p) in enumerate(chains)]
        for i, (d, p) in enumerate(chains):
            refs[d][3][0, rows[d], lanes[p]] = ys[i].astype(refs[d][3].dtype)
        h = [jnp.dot(expand(refs[d][0][0, ck[d], 0, p]), hb[i], preferred_element_type=F32)
             + expand(refs[d][1][0, ck[d], 0, p]) for i, (d, p) in enumerate(chains)]
    for i, (d, p) in enumerate(chains):
        h_scr[d, p] = h[i]

    @pl.when(ci == pl.num_programs(1) - 1)
    def _():
        hfin_ref[0] = h_scr[...]


def _rwkv_scan(mm, gg, qt, h0):
    b, nc, _, npair, _, _ = mm.shape
    t, w = qt.shape[2], qt.shape[3]
    cs = SCAN_CHUNKS if nc % SCAN_CHUNKS == 0 else 1
    nb = nc // cs
    fwd = lambda bi, ci: (bi, ci, 0, 0, 0, 0)
    rev = lambda bi, ci: (bi, nb - 1 - ci, 1, 0, 0, 0)
    mblk = (1, cs, 1, npair, CHUNK, LANE)
    hspec = pl.BlockSpec((1, 2, npair, LANE, LANE), lambda bi, ci: (bi, 0, 0, 0, 0))
    return pl.pallas_call(
        functools.partial(_rwkv_scan_body, npair=npair, cs=cs),
        grid=(b, nb),
        in_specs=[pl.BlockSpec(mblk, fwd), pl.BlockSpec(mblk, fwd),
                  pl.BlockSpec((1, 1, cs * CHUNK, w), lambda bi, ci: (bi, 0, ci, 0)),
                  pl.BlockSpec(mblk, rev), pl.BlockSpec(mblk, rev),
                  pl.BlockSpec((1, 1, cs * CHUNK, w), lambda bi, ci: (bi, 1, nb - 1 - ci, 0)),
                  hspec],
        out_specs=[pl.BlockSpec((1, cs * CHUNK, w), lambda bi, ci: (bi, ci, 0)),
                   pl.BlockSpec((1, cs * CHUNK, w), lambda bi, ci: (bi, nb - 1 - ci, 0)),
                   hspec],
        out_shape=[jax.ShapeDtypeStruct((b, t, w), ACT), jax.ShapeDtypeStruct((b, t, w), ACT),
                   jax.ShapeDtypeStruct(h0.shape, F32)],
        scratch_shapes=[pltpu.VMEM((2, npair, LANE, LANE), F32)],
        compiler_params=_cparams(("parallel", "arbitrary")),
        name="rwkv_scan",
    )(mm, gg, qt, mm, gg, qt, h0)


def _rwkv_out_body(y0_ref, y1_ref, yl_ref, bn_ref, gate_ref, x_ref, gl_ref, gnw_ref, gnb_ref,
                   w_ref, o_ref):
    y = y0_ref[0].astype(F32) + y1_ref[0].astype(F32) + yl_ref[0].astype(F32)
    r2 = _iota2((LANE, LANE), 0)
    c2 = _iota2((LANE, LANE), 1)
    avg = ((r2 // RWKV_HEAD) == (c2 // RWKV_HEAD)).astype(F32) * (1.0 / RWKV_HEAD)
    parts = []
    for p in range(y.shape[1] // LANE):
        yp = y[:, p * LANE:(p + 1) * LANE]
        dl = yp - _mm2r(yp, avg)
        var = _mm2r(dl * dl, avg)
        parts.append(dl * lax.rsqrt(var + GN_EPS))
    yn = jnp.concatenate(parts, axis=1)
    gt = gate_ref[0].astype(F32)
    act = (yn * gnw_ref[...] + gnb_ref[...] + bn_ref[0].astype(F32)) * (gt * _sigmoid(gt))
    out = jnp.dot(act.astype(BF16), w_ref[...], preferred_element_type=F32)
    o_ref[0] = x_ref[0] + gl_ref[0] * out


def _rwkv_out(y0, y1, yl, bn, gate, x, gl, gnw, gnb, w, tm):
    b, t, d = x.shape
    tm = min(tm, t)
    wd = y0.shape[2]
    tok = lambda n: pl.BlockSpec((1, tm, n), lambda bi, i: (bi, i, 0))
    return pl.pallas_call(
        _rwkv_out_body,
        grid=(b, t // tm),
        in_specs=[tok(wd), tok(wd), tok(wd), tok(wd), tok(wd), tok(d),
                  pl.BlockSpec((1, 1, d), lambda bi, i: (bi, 0, 0)),
                  pl.BlockSpec((1, wd), lambda bi, i: (0, 0)),
                  pl.BlockSpec((1, wd), lambda bi, i: (0, 0)),
                  pl.BlockSpec(w.shape, lambda bi, i: (0, 0))],
        out_specs=tok(d),
        out_shape=jax.ShapeDtypeStruct((b, t, d), F32),
        compiler_params=_cparams(("parallel", "parallel")),
        name="rwkv_out",
    )(y0, y1, yl, bn, gate, x, gl, gnw, gnb, w)


def _rope_tables(t):
    rows = t // GRID_W
    row = jnp.repeat(jnp.arange(rows, dtype=F32), GRID_W)
    col = jnp.tile(jnp.arange(GRID_W, dtype=F32), rows)
    inv = 1.0 / (ROPE_BASE ** (jnp.arange(ROPE_FREQS, dtype=F32) / ROPE_FREQS))
    ang = jnp.stack([row[:, None] * inv, col[:, None] * inv], axis=1)
    cos, sin = jnp.cos(ang), jnp.sin(ang)
    zeros = jnp.zeros_like(sin)
    ones_lo = jnp.ones((t, QK_NOPE), F32)
    pad_hi = HEAD_SLOT - QK_HEAD
    cos_t = jnp.concatenate([ones_lo, jnp.concatenate([cos, cos], axis=2).reshape(t, QK_ROPE),
                             jnp.ones((t, pad_hi), F32)], axis=1)
    sa = jnp.concatenate([jnp.zeros((t, QK_NOPE), F32),
                          jnp.concatenate([-sin, zeros], axis=2).reshape(t, QK_ROPE),
                          jnp.zeros((t, pad_hi), F32)], axis=1)
    sb = jnp.concatenate([jnp.zeros((t, QK_NOPE), F32),
                          jnp.concatenate([zeros, sin], axis=2).reshape(t, QK_ROPE),
                          jnp.zeros((t, pad_hi), F32)], axis=1)
    return cos_t, sa, sb


def _even_layer(x, ctx, mod_l, mod_c, need_ctx, g, w_in, kv_norm, q_norm, w_uq, w_ukv,
                q_head_norm, k_head_norm, w_fnet, w_out):
    b, s, d = x.shape
    tc = ctx.shape[1]
    e_q0 = KV_LORA + QK_ROPE
    e_f0 = e_q0 + Q_LORA
    e_g0 = e_f0 + FNET_GROUPS * FNET_GROUP_DIM
    w_p = jnp.concatenate([w_in[:, e_g0:], w_in[:, e_f0:e_g0], w_in[:, :e_q0],
                           jnp.zeros((d, LANE - QK_ROPE), F32), w_in[:, e_q0:e_f0]],
                          axis=1).astype(BF16)
    splits = (d, FNET_GROUPS * FNET_GROUP_DIM, KV_LORA + LANE + Q_LORA)
    kvw = w_ukv.reshape(KV_LORA, MLA_HEADS, QK_NOPE + V_HEAD)
    wk = jnp.pad(kvw[:, :, :QK_NOPE], ((0, 0), (0, 0), (0, HEAD_SLOT - QK_NOPE)))
    wk = wk.reshape(KV_LORA, MLA_HEADS * HEAD_SLOT).astype(BF16)
    wv = jnp.pad(kvw[:, :, QK_NOPE:], ((0, 0), (0, 0), (0, HEAD_SLOT - V_HEAD)))
    wv = wv.reshape(KV_LORA, MLA_HEADS * HEAD_SLOT).astype(BF16)
    wq = jnp.pad(w_uq.reshape(Q_LORA, MLA_HEADS, QK_HEAD), ((0, 0), (0, 0), (0, HEAD_SLOT - QK_HEAD)))
    wq = wq.reshape(Q_LORA, MLA_HEADS * HEAD_SLOT).astype(BF16)
    kg = jnp.pad(k_head_norm, (0, HEAD_SLOT - QK_HEAD)).reshape(1, HEAD_SLOT)
    qg = (jnp.pad(q_head_norm, (0, HEAD_SLOT - QK_HEAD))
          * (QK_HEAD ** -0.5 * math.log2(math.e))).reshape(1, HEAD_SLOT)
    kvn, qn = kv_norm.reshape(1, -1), q_norm.reshape(1, -1)
    g2 = g.reshape(1, d)
    bound = (1.02 * QK_HEAD * jnp.max(jnp.abs(qg)) * jnp.max(jnp.abs(kg))).astype(BF16).astype(F32)
    static_ok = bound <= MAX_STATIC_BOUND
    bias_lane = (jnp.arange(HEAD_SLOT) == BIAS_LANE).astype(F32).reshape(1, HEAD_SLOT)
    kb = bias_lane * jnp.where(static_ok, -bound, 0.0)
    qb = bias_lane

    gate_l, four_l, ua_l = _proj(x, g2, mod_l[1], mod_l[0], w_p, splits, 512)
    gate_c, four_c, ua_c = _proj(ctx, g2, mod_c[1], mod_c[0], w_p, splits, 512)
    sk = s + tc
    q_l, k_all, v_all = _qkv(ua_l, kvn, qn, wk, wv, wq, kg, qg, kb, qb, _rope_tables(s), True,
                             256, sk)
    dummy = jnp.zeros((tc, LANE), F32)
    q_c, k_all, v_all = _qkv(ua_c, kvn, qn, wk, wv, wq, kg, qg, kb, qb, (dummy, dummy, dummy),
                             False, 256, sk, kv_into=(k_all, v_all))
    bk = 768 if sk % 768 == 0 else tc
    o_l = _attention(q_l, k_all, v_all, static_ok, 2048, bk)
    f_l = _fourier_latent(four_l, w_fnet)
    wo = w_out.astype(BF16)
    x_new = _merge(o_l, f_l, gate_l, x, mod_l[2], wo, 512)
    ctx_new = ctx
    if need_ctx:
        o_c = _attention(q_c, k_all, v_all, static_ok, tc, tc, k_start=s, k_rows=tc)
        f_c = _fourier_dense(four_c, w_fnet)
        ctx_new = _merge(o_c, f_c, gate_c, ctx, mod_c[2], wo, 512)
    return x_new, ctx_new


def _odd_layer(x, ctx, mod_l, mod_c, need_ctx, g, w_in, shift_w, w0, w2, a0, a2, k_k, k_a, r_k,
               gn_w, gn_b, w_out):
    b, s, d = x.shape
    w = k_k.shape[0]
    npair = w // LANE
    o_wd0 = 2 * w
    o_r0 = o_wd0 + 2 * DECAY_LORA + 2 * AAA_LORA
    conv_ch = o_r0 + w
    perm = lambda m: jnp.concatenate([m[:, :o_wd0], m[:, o_r0:conv_ch], m[:, o_wd0:o_r0]], axis=1)
    w_p = jnp.concatenate([perm(w_in), w_in[:, conv_ch:]], axis=1).astype(BF16)
    sw = perm(shift_w)
    g2 = g.reshape(1, d)

    def pairs(vec2):
        return vec2.reshape(2, npair, LANE).transpose(1, 0, 2).reshape(npair, 1, 2 * LANE)

    def pair_mats(m):
        rr = m.shape[1]
        mp = m.reshape(2, rr, npair, LANE).transpose(2, 0, 1, 3)
        z = jnp.zeros_like(mp[:, 0])
        top = jnp.concatenate([mp[:, 0], z], axis=2)
        bot = jnp.concatenate([z, mp[:, 1]], axis=2)
        return jnp.concatenate([top, bot], axis=1).astype(BF16)

    w0p, a0p, w2p, a2p = pairs(w0), pairs(a0), pair_mats(w2), pair_mats(a2)
    kk2, ka2, rk2 = k_k.reshape(1, w), k_a.reshape(1, w), r_k.reshape(1, w)
    wo = w_out.astype(BF16)

    def mix(xin, mod, h0):
        z, gate = _proj_shift(xin, g2, mod[1], mod[0], w_p, sw, conv_ch, 512)
        mm, gg, qt, yl, bn = _rwkv_local(z, w0p, w2p, a0p, a2p, kk2, ka2, rk2)
        y0, y1, hfin = _rwkv_scan(mm, gg, qt, h0)
        return (y0, y1, yl, bn, gate), hfin

    h_zero = jnp.zeros((b, 2, npair, LANE, LANE), F32)
    parts_c, h_ctx = mix(ctx, mod_c, h_zero)
    parts_l, _ = mix(x, mod_l, h_ctx)
    x_new = _rwkv_out(*parts_l, x, mod_l[2], gn_w.reshape(1, w), gn_b.reshape(1, w), wo, 256)
    ctx_new = ctx
    if need_ctx:
        ctx_new = _rwkv_out(*parts_c, ctx, mod_c[2], gn_w.reshape(1, w), gn_b.reshape(1, w), wo, 256)
    return x_new, ctx_new


def kernel(x, c, ctx, c_ctx, ada_w, ada_b, norm_g, e_w_in, e_kv_norm, e_q_norm, e_w_uq, e_w_ukv,
           e_q_head_norm, e_k_head_norm, e_w_fnet, e_w_out, o_w_in, o_shift_w, o_w0, o_w2, o_a0,
           o_a2, o_k_k, o_k_a, o_r_k, o_gn_w, o_gn_b, o_w_out):
    b, s, d = x.shape
    depth = ada_w.shape[0]
    assert b + 1 <= 8
    cond8 = jnp.concatenate([c, c_ctx[None, :], jnp.zeros((8 - b - 1, d), F32)], axis=0)
    mod = _ada(cond8, ada_w, ada_b)
    for layer in range(depth):
        need_ctx = layer < depth - 1
        m = mod[layer]
        chunk = lambda rows, i: rows[:, None, i * d:(i + 1) * d]
        lat, cx = m[:b], jnp.broadcast_to(m[b:b + 1], (b, 3 * d))
        mod_l = (chunk(lat, 0), 1.0 + chunk(lat, 1), chunk(lat, 2))
        mod_c = (chunk(cx, 0), 1.0 + chunk(cx, 1), chunk(cx, 2))
        j = layer // 2
        if layer % 2 == 0:
            x, ctx = _even_layer(x, ctx, mod_l, mod_c, need_ctx, norm_g[layer], e_w_in[j],
                                 e_kv_norm[j], e_q_norm[j], e_w_uq[j], e_w_ukv[j],
                                 e_q_head_norm[j], e_k_head_norm[j], e_w_fnet[j], e_w_out[j])
        else:
            x, ctx = _odd_layer(x, ctx, mod_l, mod_c, need_ctx, norm_g[layer], o_w_in[j],
                                o_shift_w[j], o_w0[j], o_w2[j], o_a0[j], o_a2[j], o_k_k[j],
                                o_k_a[j], o_r_k[j].reshape(-1), o_gn_w[j], o_gn_b[j], o_w_out[j])
    return x
```

```python
import functools
import math

import numpy as np
import jax
import jax.numpy as jnp
from jax import lax
from jax.experimental import pallas as pl
from jax.experimental.pallas import tpu as pltpu

F32 = jnp.float32
BF16 = jnp.bfloat16
ACT = BF16

GRID_W = 64
NORM_EPS = 1e-6
MLA_HEADS = 8
QK_NOPE = 64
QK_ROPE = 32
QK_HEAD = QK_NOPE + QK_ROPE
V_HEAD = 64
Q_LORA = 384
KV_LORA = 256
ROPE_FREQS = QK_ROPE // 4
ROPE_BASE = 10000.0
FNET_GROUPS = 4
FNET_GROUP_DIM = 128
RWKV_HEAD = 64
DECAY_LORA = 64
AAA_LORA = 64
GN_EPS = 64e-5

LANE = 128
CHUNK = 64
HEAD_SLOT = 128
VMEM_LIMIT = 56 * 1024 * 1024

NN = (((1,), (0,)), ((), ()))
NT = (((1,), (1,)), ((), ()))
TN = (((0,), (0,)), ((), ()))


def _cparams(sem):
    return pltpu.CompilerParams(dimension_semantics=sem, vmem_limit_bytes=VMEM_LIMIT)


def _mm(a, b, dn=NN):
    return lax.dot_general(a.astype(BF16), b.astype(BF16), dn, preferred_element_type=F32)


def _split(a):
    hi = a.astype(BF16)
    lo = (a - hi.astype(F32)).astype(BF16)
    return hi, lo


def _mm3(a, b, dn=NN):
    ah, al = _split(a)
    bh, bl = _split(b)
    d = lambda x, y: lax.dot_general(x, y, dn, preferred_element_type=F32)
    return d(ah, bh) + d(al, bh) + d(ah, bl)


def _mm2r(a, b_exact):
    ah, al = _split(a)
    bb = b_exact.astype(BF16)
    return jnp.dot(jnp.concatenate([ah, al], axis=1), jnp.concatenate([bb, bb], axis=0),
                   preferred_element_type=F32)


def _mm2l(a_exact, b):
    bh, bl = _split(b)
    n = b.shape[1]
    y = jnp.dot(a_exact.astype(BF16), jnp.concatenate([bh, bl], axis=1),
                preferred_element_type=F32)
    return y[:, :n] + y[:, n:]


def _sigmoid(x):
    return 1.0 / (1.0 + jnp.exp(-x))


def _modnorm(x, g, sc1, sh):
    y = x * lax.rsqrt(jnp.mean(x * x, axis=-1, keepdims=True) + NORM_EPS)
    return (y * g) * sc1 + sh


def _iota2(shape, dim):
    return lax.broadcasted_iota(jnp.int32, shape, dim)


def _ada_body(c_ref, w_ref, b_ref, o_ref):
    c = c_ref[...]
    s = c * _sigmoid(c)
    o_ref[0] = _mm3(s, w_ref[0]) + b_ref[0]


def _ada(cond8, ada_w, ada_b):
    depth, d, n = ada_w.shape
    tn = 512
    return pl.pallas_call(
        _ada_body,
        grid=(depth, n // tn),
        in_specs=[
            pl.BlockSpec((8, d), lambda l, j: (0, 0)),
            pl.BlockSpec((1, d, tn), lambda l, j: (l, 0, j)),
            pl.BlockSpec((1, 1, tn), lambda l, j: (l, 0, j)),
        ],
        out_specs=pl.BlockSpec((1, 8, tn), lambda l, j: (l, 0, j)),
        out_shape=jax.ShapeDtypeStruct((depth, 8, n), F32),
        compiler_params=_cparams(("parallel", "parallel")),
        name="ada",
    )(cond8, ada_w, ada_b.reshape(depth, 1, n))


COL_CHUNK = 512


def _proj_body(x_ref, g_ref, sc_ref, sh_ref, w_ref, *o_refs, splits):
    h = _modnorm(x_ref[0], g_ref[...], sc_ref[0], sh_ref[0]).astype(BF16)
    off = 0
    for o_ref, n in zip(o_refs, splits):
        for c0 in range(0, n, COL_CHUNK):
            c1 = min(n, c0 + COL_CHUNK)
            o_ref[0, :, c0:c1] = jnp.dot(h, w_ref[:, off + c0:off + c1],
                                         preferred_element_type=F32).astype(o_ref.dtype)
        off += n


def _proj(x, g, sc1, sh, w, splits, tm):
    b, t, d = x.shape
    tm = min(tm, t)
    n = w.shape[1]
    vec = pl.BlockSpec((1, 1, d), lambda bi, i: (bi, 0, 0))
    return pl.pallas_call(
        functools.partial(_proj_body, splits=splits),
        grid=(b, t // tm),
        in_specs=[
            pl.BlockSpec((1, tm, d), lambda bi, i: (bi, i, 0)),
            pl.BlockSpec((1, d), lambda bi, i: (0, 0)),
            vec, vec,
            pl.BlockSpec((d, n), lambda bi, i: (0, 0)),
        ],
        out_specs=[pl.BlockSpec((1, tm, s), lambda bi, i: (bi, i, 0)) for s in splits],
        out_shape=[jax.ShapeDtypeStruct((b, t, s), ACT) for s in splits],
        compiler_params=_cparams(("parallel", "parallel")),
        name="proj",
    )(x, g, sc1, sh, w)


HALO = 16


def _proj_shift_body(x_ref, xp_ref, xn_ref, g_ref, sc_ref, sh_ref, w_ref, sw_ref, z_ref, gate_ref,
                     *, tm, n_conv):
    i = pl.program_id(1)
    last = pl.num_programs(1) - 1
    g, sc1, sh = g_ref[...], sc_ref[0], sh_ref[0]
    h = _modnorm(x_ref[0], g, sc1, sh)
    hp = _modnorm(xp_ref[0], g, sc1, sh) * (i > 0).astype(F32)
    hn = _modnorm(xn_ref[0], g, sc1, sh) * (i < last).astype(F32)
    hb = jnp.concatenate([hp, h, hn], axis=0).astype(BF16)
    rows = tm + 2 * HALO
    for c0 in range(0, n_conv, COL_CHUNK):
        c1 = min(n_conv, c0 + COL_CHUNK)
        u = jnp.dot(hb, w_ref[:, c0:c1], preferred_element_type=F32)
        up = pltpu.roll(u, 1, 0)[HALO:HALO + tm]
        un = pltpu.roll(u, rows - 1, 0)[HALO:HALO + tm]
        um = u[HALO:HALO + tm]
        z_ref[0, :, c0:c1] = (sw_ref[0:1, c0:c1] * up + sw_ref[1:2, c0:c1] * um
                              + sw_ref[2:3, c0:c1] * un).astype(z_ref.dtype)
    hc = hb[HALO:HALO + tm]
    n_all = w_ref.shape[1]
    for c0 in range(n_conv, n_all, COL_CHUNK):
        c1 = min(n_all, c0 + COL_CHUNK)
        gate_ref[0, :, c0 - n_conv:c1 - n_conv] = jnp.dot(
            hc, w_ref[:, c0:c1], preferred_element_type=F32).astype(gate_ref.dtype)


def _proj_shift(x, g, sc1, sh, w, sw, n_conv, tm):
    b, t, d = x.shape
    tm = min(tm, t)
    n = w.shape[1]
    hb = tm // HALO
    nhb = t // HALO
    vec = pl.BlockSpec((1, 1, d), lambda bi, i: (bi, 0, 0))
    return pl.pallas_call(
        functools.partial(_proj_shift_body, tm=tm, n_conv=n_conv),
        grid=(b, t // tm),
        in_specs=[
            pl.BlockSpec((1, tm, d), lambda bi, i: (bi, i, 0)),
            pl.BlockSpec((1, HALO, d), lambda bi, i: (bi, jnp.maximum(i * hb - 1, 0), 0)),
            pl.BlockSpec((1, HALO, d), lambda bi, i: (bi, jnp.minimum((i + 1) * hb, nhb - 1), 0)),
            pl.BlockSpec((1, d), lambda bi, i: (0, 0)),
            vec, vec,
            pl.BlockSpec((d, n), lambda bi, i: (0, 0)),
            pl.BlockSpec((3, n_conv), lambda bi, i: (0, 0)),
        ],
        out_specs=[pl.BlockSpec((1, tm, n_conv), lambda bi, i: (bi, i, 0)),
                   pl.BlockSpec((1, tm, n - n_conv), lambda bi, i: (bi, i, 0))],
        out_shape=[jax.ShapeDtypeStruct((b, t, n_conv), ACT),
                   jax.ShapeDtypeStruct((b, t, n - n_conv), ACT)],
        compiler_params=_cparams(("parallel", "parallel")),
        name="proj_shift",
    )(x, x, x, g, sc1, sh, w, sw)


def _rms(x, g):
    return x * lax.rsqrt(jnp.mean(x * x, axis=-1, keepdims=True) + NORM_EPS) * g


def _qkv_body(ua_ref, kvn_ref, qn_ref, wk_ref, wv_ref, wq_ref, kg_ref, qg_ref, kb_ref, qb_ref,
              cos_ref, sa_ref, sb_ref, *rest, rope):
    q_ref, k_ref, v_ref = rest[-3:]
    ua = ua_ref[0].astype(F32)
    ckv = _rms(ua[:, :KV_LORA], kvn_ref[...]).astype(BF16)
    kr = ua[:, KV_LORA:KV_LORA + LANE]
    cq = _rms(ua[:, KV_LORA + LANE:], qn_ref[...]).astype(BF16)
    kn = jnp.dot(ckv, wk_ref[...], preferred_element_type=F32)
    vv = jnp.dot(ckv, wv_ref[...], preferred_element_type=F32)
    qq = jnp.dot(cq, wq_ref[...], preferred_element_type=F32)
    ones_hi = (_iota2((1, HEAD_SLOT), 1) >= V_HEAD).astype(F32)
    pe = pltpu.roll(kr, QK_NOPE, 1)
    kg, qg = kg_ref[...], qg_ref[...]
    inv_n = 1.0 / QK_HEAD

    def finish(xh, gain):
        xh = xh * lax.rsqrt(jnp.sum(xh * xh, axis=-1, keepdims=True) * inv_n + NORM_EPS) * gain
        if rope:
            xh = (xh * cos_ref[...] + pltpu.roll(xh, LANE - ROPE_FREQS, 1) * sa_ref[...]
                  + pltpu.roll(xh, ROPE_FREQS, 1) * sb_ref[...])
        return xh

    for h in range(MLA_HEADS):
        sl = slice(h * HEAD_SLOT, (h + 1) * HEAD_SLOT)
        k_ref[0, h] = (finish(kn[:, sl] + pe, kg) + kb_ref[...]).astype(BF16)
        q_ref[0, h] = (finish(qq[:, sl], qg) + qb_ref[...]).astype(BF16)
        v_ref[0, h] = (vv[:, sl] + ones_hi).astype(BF16)


def _qkv(ua, kvn, qn, wk, wv, wq, kg, qg, kb, qb, tabs, rope, tm, kv_rows, kv_into=None):
    b, t, wa = ua.shape
    tm = min(tm, t)
    full = lambda a: pl.BlockSpec(a.shape, lambda bi, i: (0,) * a.ndim)
    tab = pl.BlockSpec((tm, LANE), lambda bi, i: (i, 0))
    off = 0 if kv_into is None else (kv_rows - t) // tm
    head_blk = (1, MLA_HEADS, tm, HEAD_SLOT)
    kv_spec = pl.BlockSpec(head_blk, lambda bi, i: (bi, 0, off + i, 0))
    kv_shape = jax.ShapeDtypeStruct((b, MLA_HEADS, kv_rows, HEAD_SLOT), BF16)
    n_in = 13
    extra, extra_specs, aliases = (), [], {}
    if kv_into is not None:
        extra = tuple(kv_into)
        extra_specs = [pl.BlockSpec(memory_space=pl.ANY)] * 2
        aliases = {n_in: 1, n_in + 1: 2}
    return pl.pallas_call(
        functools.partial(_qkv_body, rope=rope),
        grid=(b, t // tm),
        in_specs=[pl.BlockSpec((1, tm, wa), lambda bi, i: (bi, i, 0)),
                  full(kvn), full(qn), full(wk), full(wv), full(wq), full(kg), full(qg),
                  full(kb), full(qb), tab, tab, tab] + extra_specs,
        out_specs=[pl.BlockSpec(head_blk, lambda bi, i: (bi, 0, i, 0)), kv_spec, kv_spec],
        out_shape=[jax.ShapeDtypeStruct((b, MLA_HEADS, t, HEAD_SLOT), BF16), kv_shape, kv_shape],
        input_output_aliases=aliases,
        compiler_params=_cparams(("parallel", "parallel")),
        name="qkv",
    )(ua, kvn, qn, wk, wv, wq, kg, qg, kb, qb, *tabs, *extra)


BIAS_LANE = QK_HEAD
MAX_STATIC_BOUND = 50.0


def _attn_finish(acc_ref, o_ref):
    bq = acc_ref.shape[1]
    lane = _iota2((bq, LANE), 1)
    o0 = acc_ref[0] / pltpu.roll(acc_ref[0], V_HEAD, 1)
    o1 = acc_ref[1] / pltpu.roll(acc_ref[1], V_HEAD, 1)
    o_ref[0] = jnp.where(lane < V_HEAD, o0, pltpu.roll(o1, V_HEAD, 1)).astype(o_ref.dtype)


def _attn_static_body(q_ref, k_ref, v_ref, o_ref, acc_ref, *, bk):
    acc_ref[...] = jnp.zeros(acc_ref.shape, F32)

    def step(j, carry):
        rows = pl.ds(pl.multiple_of(j * bk, bk), bk)
        s = [lax.dot_general(q_ref[0, hh], k_ref[0, hh, rows, :], NT, preferred_element_type=F32)
             for hh in range(2)]
        p = [jnp.exp2(x.astype(BF16)) for x in s]
        for hh in range(2):
            acc_ref[hh] += jnp.dot(p[hh], v_ref[0, hh, rows, :], preferred_element_type=F32)
        return carry

    lax.fori_loop(0, k_ref.shape[2] // bk, step, 0)
    _attn_finish(acc_ref, o_ref)


def _attn_online_body(q_ref, k_ref, v_ref, o_ref, m_ref, acc_ref):
    j = pl.program_id(3)

    @pl.when(j == 0)
    def _():
        m_ref[...] = jnp.full(m_ref.shape, -jnp.inf, F32)
        acc_ref[...] = jnp.zeros(acc_ref.shape, F32)

    for hh in range(2):
        s = lax.dot_general(q_ref[0, hh], k_ref[0, hh], NT, preferred_element_type=F32)
        m_prev = m_ref[hh]
        m_new = jnp.maximum(m_prev, jnp.max(s, axis=-1, keepdims=True))
        p = jnp.exp2(s - m_new)
        alpha = jnp.exp2(m_prev - m_new)
        acc_ref[hh] = alpha * acc_ref[hh] + jnp.dot(p.astype(BF16), v_ref[0, hh],
                                                     preferred_element_type=F32)
        m_ref[hh] = m_new

    @pl.when(j == pl.num_programs(3) - 1)
    def _():
        _attn_finish(acc_ref, o_ref)


def _attention(q, k, v, static_ok, bq, bk, k_start=0, k_rows=None):
    b, h, s, e = q.shape
    sk = k.shape[2] if k_rows is None else k_rows
    bq, bk = min(bq, s), min(bk, sk)
    j0 = k_start // bk

    def call(body, scratch, name):
        return pl.pallas_call(
            body,
            grid=(b, h // 2, s // bq, sk // bk),
            in_specs=[pl.BlockSpec((1, 2, bq, e), lambda bi, p, i, j: (bi, p, i, 0)),
                      pl.BlockSpec((1, 2, bk, e), lambda bi, p, i, j: (bi, p, j0 + j, 0)),
                      pl.BlockSpec((1, 2, bk, e), lambda bi, p, i, j: (bi, p, j0 + j, 0))],
            out_specs=pl.BlockSpec((1, bq, 2 * V_HEAD), lambda bi, p, i, j: (bi, i, p)),
            out_shape=jax.ShapeDtypeStruct((b, s, h * V_HEAD), ACT),
            scratch_shapes=scratch,
            compiler_params=_cparams(("parallel", "parallel", "parallel", "arbitrary")),
            name=name,
        )(q, k, v)

    acc = pltpu.VMEM((2, bq, LANE), F32)

    def static_call():
        kv_blk = pl.BlockSpec((1, 2, sk, e), lambda bi, p, i: (bi, p, k_start // sk, 0))
        return pl.pallas_call(
            functools.partial(_attn_static_body, bk=bk),
            grid=(b, h // 2, s // bq),
            in_specs=[pl.BlockSpec((1, 2, bq, e), lambda bi, p, i: (bi, p, i, 0)), kv_blk, kv_blk],
            out_specs=pl.BlockSpec((1, bq, 2 * V_HEAD), lambda bi, p, i: (bi, i, p)),
            out_shape=jax.ShapeDtypeStruct((b, s, h * V_HEAD), ACT),
            scratch_shapes=[acc],
            compiler_params=_cparams(("parallel", "parallel", "arbitrary")),
            name="attention",
        )(q, k, v)

    return lax.cond(
        static_ok, static_call,
        lambda: call(_attn_online_body, [pltpu.VMEM((2, bq, 1), F32), acc], "attention_online"))


def _dft_mats(n):
    idx = np.arange(n)
    ang = 2.0 * np.pi * ((idx[:, None] * idx[None, :]) % n) / n
    return np.cos(ang), np.sin(ang)


def _hilo(a):
    a = jnp.asarray(a, F32)
    hi = a.astype(BF16)
    return hi, (a - hi.astype(F32)).astype(BF16)


def _mm3c(ah, al, b, dn=NN):
    bh, bl = _split(b)
    d = lambda x, y: lax.dot_general(x, y, dn, preferred_element_type=F32)
    return d(ah, bh) + d(al, bh) + d(ah, bl)


def _four_rows_body(x_ref, wh_ref, wl_ref, tc_ref, ts_ref, o_ref):
    r = tc_ref.shape[0]
    x = x_ref[0, 0]
    d = lambda a: jnp.dot(a, x, preferred_element_type=F32)
    y = d(wh_ref[...]) + d(wl_ref[...])
    yc, ys = y[:r], y[r:]
    tc, ts = tc_ref[...], ts_ref[...]
    o_ref[0, 0, :r] = yc * tc - ys * ts
    o_ref[0, 0, r:] = yc * ts + ys * tc


def _four_cols_body(y_ref, wh_ref, wl_ref, ch_ref, cl_ref, wf_ref, o_ref, y3_scr, *, krt, scale):
    gd = FNET_GROUP_DIM

    def one(j, carry):
        rows = pl.ds(pl.multiple_of(j * GRID_W, GRID_W), GRID_W)
        ycs = jnp.concatenate([y_ref[0, 0, 0, rows, :], y_ref[0, 0, 1, rows, :]], axis=0)
        y3 = _mm3c(wh_ref[...], wl_ref[...], ycs)
        y3_scr[rows, :] = jnp.concatenate([y3[:GRID_W], y3[GRID_W:]], axis=1)
        return carry

    lax.fori_loop(0, krt, one, 0, unroll=8)
    y3 = y3_scr[...]
    yh, yl = _split(y3)
    d = lambda a, b: jnp.dot(a, b, preferred_element_type=F32)
    f = (d(yh, ch_ref[...]) + d(yl, ch_ref[...]) + d(yh, cl_ref[...])) * scale
    o_ref[0, 0] = _mm3(f, wf_ref[0]).astype(o_ref.dtype)


def _fourier_latent(xf, w_fnet):
    b, t, _ = xf.shape
    g, gd = FNET_GROUPS, FNET_GROUP_DIM
    r = t // GRID_W
    wide = GRID_W * gd
    xv = xf.reshape(b, r, GRID_W, g, gd).transpose(0, 3, 1, 2, 4).reshape(b, g, r, wide)
    cr, sr = _dft_mats(r)
    wh, wl = _hilo(np.concatenate([cr, sr], axis=0))
    kr_i, c_i = np.arange(r)[:, None], np.arange(GRID_W)[None, :]
    ang = 2.0 * np.pi * ((kr_i * c_i) % t) / t
    twc = jnp.repeat(jnp.asarray(np.cos(ang), F32), gd, axis=1)
    tws = jnp.repeat(jnp.asarray(np.sin(ang), F32), gd, axis=1)
    tl = min(2048, wide)
    y2 = pl.pallas_call(
        _four_rows_body,
        grid=(b, g, wide // tl),
        in_specs=[pl.BlockSpec((1, 1, r, tl), lambda bi, gi, l: (bi, gi, 0, l)),
                  pl.BlockSpec((2 * r, r), lambda bi, gi, l: (0, 0)),
                  pl.BlockSpec((2 * r, r), lambda bi, gi, l: (0, 0)),
                  pl.BlockSpec((r, tl), lambda bi, gi, l: (0, l)),
                  pl.BlockSpec((r, tl), lambda bi, gi, l: (0, l))],
        out_specs=pl.BlockSpec((1, 1, 2 * r, tl), lambda bi, gi, l: (bi, gi, 0, l)),
        out_shape=jax.ShapeDtypeStruct((b, g, 2 * r, wide), F32),
        compiler_params=_cparams(("parallel", "parallel", "parallel")),
        name="fourier_rows",
    )(xv, wh, wl, twc, tws)
    y2v = y2.reshape(b, g, 2, r * GRID_W, gd)
    c64, s64 = _dft_mats(GRID_W)
    w64h, w64l = _hilo(np.block([[c64, -s64], [s64, c64]]))
    cc, sc = _dft_mats(gd)
    ch, cl = _hilo(np.concatenate([cc, -sc], axis=0))
    krt = min(32, r)
    const = lambda a: pl.BlockSpec(a.shape, lambda bi, gi, i: (0, 0))
    fo = pl.pallas_call(
        functools.partial(_four_cols_body, krt=krt, scale=1.0 / math.sqrt(t * gd)),
        grid=(b, g, r // krt),
        in_specs=[pl.BlockSpec((1, 1, 2, krt * GRID_W, gd), lambda bi, gi, i: (bi, gi, 0, i, 0)),
                  const(w64h), const(w64l), const(ch), const(cl),
                  pl.BlockSpec((1, gd, gd), lambda bi, gi, i: (gi, 0, 0))],
        out_specs=pl.BlockSpec((1, 1, krt * GRID_W, gd), lambda bi, gi, i: (bi, gi, i, 0)),
        out_shape=jax.ShapeDtypeStruct((b, g, r * GRID_W, gd), ACT),
        scratch_shapes=[pltpu.VMEM((krt * GRID_W, 2 * gd), F32)],
        compiler_params=_cparams(("parallel", "parallel", "parallel")),
        name="fourier_cols",
    )(y2v, w64h, w64l, ch, cl, w_fnet)
    return fo.reshape(b, g, r, GRID_W, gd).transpose(0, 3, 2, 1, 4).reshape(b, t, g * gd)


def _four_dense_body(x_ref, ch_ref, cl_ref, th_ref, tl_ref, sh_ref, sl_ref, wf_ref, o_ref, *, scale):
    x = x_ref[0]
    xh, xl = _split(x)
    d = lambda a, b: jnp.dot(a, b, preferred_element_type=F32)
    z = d(xh, ch_ref[...]) + d(xl, ch_ref[...]) + d(xh, cl_ref[...])
    zc, zs = z[:, :FNET_GROUP_DIM], z[:, FNET_GROUP_DIM:]
    f = (_mm3c(th_ref[...], tl_ref[...], zc) - _mm3c(sh_ref[...], sl_ref[...], zs)) * scale
    o_ref[0] = _mm3(f, wf_ref[0]).astype(o_ref.dtype)


def _fourier_dense(xf, w_fnet):
    b, t, _ = xf.shape
    g, gd = FNET_GROUPS, FNET_GROUP_DIM
    cc, sc = _dft_mats(gd)
    ch, cl = _hilo(np.concatenate([cc, sc], axis=1))
    ct, st = _dft_mats(t)
    cth, ctl = _hilo(ct)
    sth, stl = _hilo(st)
    sq = pl.BlockSpec((t, t), lambda bi, gi: (0, 0))
    cs = pl.BlockSpec((gd, 2 * gd), lambda bi, gi: (0, 0))
    return pl.pallas_call(
        functools.partial(_four_dense_body, scale=1.0 / math.sqrt(t * gd)),
        grid=(b, g),
        in_specs=[pl.BlockSpec((1, t, gd), lambda bi, gi: (bi, 0, gi)), cs, cs, sq, sq, sq, sq,
                  pl.BlockSpec((1, gd, gd), lambda bi, gi: (gi, 0, 0))],
        out_specs=pl.BlockSpec((1, t, gd), lambda bi, gi: (bi, 0, gi)),
        out_shape=jax.ShapeDtypeStruct((b, t, g * gd), ACT),
        compiler_params=_cparams(("parallel", "parallel")),
        name="fourier_dense",
    )(xf, ch, cl, cth, ctl, sth, stl, w_fnet)


def _merge_body(o_ref, f_ref, gate_ref, x_ref, gl_ref, w_ref, out_ref):
    gt = gate_ref[0].astype(F32)
    mix = jnp.concatenate([o_ref[0], f_ref[0]], axis=-1).astype(F32) * (gt * _sigmoid(gt))
    y = jnp.dot(mix.astype(BF16), w_ref[...], preferred_element_type=F32)
    out_ref[0] = x_ref[0] + gl_ref[0] * y


def _merge(o, f, gate, x, gl, w, tm):
    b, t, d = x.shape
    tm = min(tm, t)
    half = o.shape[2]
    tok = lambda n: pl.BlockSpec((1, tm, n), lambda bi, i: (bi, i, 0))
    return pl.pallas_call(
        _merge_body,
        grid=(b, t // tm),
        in_specs=[tok(half), tok(half), tok(d), tok(d),
                  pl.BlockSpec((1, 1, d), lambda bi, i: (bi, 0, 0)),
                  pl.BlockSpec(w.shape, lambda bi, i: (0, 0))],
        out_specs=tok(d),
        out_shape=jax.ShapeDtypeStruct((b, t, d), F32),
        compiler_params=_cparams(("parallel", "parallel")),
        name="merge",
    )(o, f, gate, x, gl, w)


EXP_M05 = math.exp(-0.5)


PAIRS_PER_STEP = 8


def _rwkv_local_body(zk_ref, zv_ref, zr_ref, zwa_ref, w0_ref, w2_ref, a0_ref, a2_ref,
                     kk_ref, ka_ref, rk_ref, m_ref, g_ref, qt_ref, yl_ref, bn_ref):
    c = CHUNK
    zwa = zwa_ref[0].astype(F32)
    lora_w, lora_a = jnp.tanh(zwa[:, :LANE]), zwa[:, LANE:]

    head0 = _iota2((1, LANE), 1) < RWKV_HEAD
    r2 = _iota2((LANE, LANE), 0)
    c2 = _iota2((LANE, LANE), 1)
    same = (r2 // RWKV_HEAD) == (c2 // RWKV_HEAD)
    ones_bd = same.astype(F32)
    eye = r2 == c2
    eye_f = eye.astype(F32)
    rc = _iota2((c, c), 0)
    cc = _iota2((c, c), 1)

    def stack(x):
        z = jnp.zeros_like(x)
        return jnp.concatenate([jnp.where(head0, x, z), jnp.where(head0, z, x)], axis=0)

    stack_b = lambda x: stack(x.astype(BF16))
    fold = lambda x: x[:c] + x[c:]

    pairs = range(PAIRS_PER_STEP)
    chains = [(q, d) for q in pairs for d in range(2)]
    qls = [slice(q * LANE, (q + 1) * LANE) for q in pairs]
    ks = [zk_ref[0, :, ql].astype(F32) for ql in qls]
    vs_ = [zv_ref[0, :, ql].astype(F32) for ql in qls]
    rs = [zr_ref[0, :, ql].astype(F32) for ql in qls]
    wraw = [_mm(lora_w, w2_ref[q]) + w0_ref[q] for q in pairs]
    araw = [_mm(lora_a, a2_ref[q]) + a0_ref[q] for q in pairs]
    logw = [-EXP_M05 * _sigmoid(w) for w in wraw]
    a_all = [_sigmoid(a) for a in araw]
    kk0 = [ks[q] * kk_ref[:, qls[q]] for q in pairs]
    ss = [_mm2r(x * x, ones_bd) for x in kk0]
    kk = [kk0[q] / jnp.maximum(jnp.sqrt(ss[q]), 1e-12) for q in pairs]
    vstk = [stack_b(v) for v in vs_]

    dsl = [slice(d * LANE, (d + 1) * LANE) for d in range(2)]
    lw = [logw[q][:, dsl[d]] for q, d in chains]
    ad = [a_all[q][:, dsl[d]] for q, d in chains]
    kd = [ks[q] * (1.0 + (ad[i] - 1.0) * ka_ref[:, qls[q]]) for i, (q, d) in enumerate(chains)]
    bb = [kk[q] * ad[i] for i, (q, d) in enumerate(chains)]
    bonus = [_mm2r(rs[q] * kd[i] * rk_ref[:, qls[q]], ones_bd) * vs_[q]
             for i, (q, d) in enumerate(chains)]
    r3 = _iota2((c, 3 * c), 0)
    c3 = _iota2((c, 3 * c), 1) & (c - 1)
    tri3 = [(c3 <= r3).astype(BF16), (c3 >= r3).astype(BF16)]
    strict = [same & (c2 < r2), same & (c2 > r2)]
    incl = [same & (c2 <= r2), same & (c2 >= r2)]

    def cumsum(x, tri):
        xh, xl = _split(x)
        xll = (x - xh.astype(F32) - xl.astype(F32)).astype(BF16)
        return jnp.dot(tri, jnp.concatenate([xh, xl, xll], axis=0), preferred_element_type=F32)

    lc = [cumsum(lw[i], tri3[d]) for i, (q, d) in enumerate(chains)]
    ltot = [lc[i][c - 1:c] if d == 0 else lc[i][0:1] for i, (q, d) in enumerate(chains)]
    kkd_s = [stack_b(kk[q] * jnp.exp(lc[i] - lw[i])) for i, (q, d) in enumerate(chains)]
    rd_s = [stack(rs[q] * jnp.exp(lc[i])) for i, (q, d) in enumerate(chains)]
    e_inv = [jnp.exp(-x) for x in lc]
    inv_s = [jnp.concatenate([stack_b(bb[i] * e_inv[i]), stack_b(kd[i] * e_inv[i])], axis=0)
             for i in range(len(chains))]
    amat = [lax.dot_general(jnp.concatenate([kkd_s[i], rd_s[i].astype(BF16)], axis=0), inv_s[i],
                            NT, preferred_element_type=F32).astype(BF16)
            for i in range(len(chains))]
    zero_b = jnp.zeros((LANE, LANE), BF16)
    a_kb_b = [jnp.where(strict[d], amat[i][:LANE, :LANE], zero_b)
              for i, (q, d) in enumerate(chains)]
    a_kb = [x.astype(F32) for x in a_kb_b]
    a_kk = [jnp.where(strict[d], amat[i][:LANE, LANE:], zero_b)
            for i, (q, d) in enumerate(chains)]
    aq_b = [jnp.where(incl[d], amat[i][LANE:, :LANE], zero_b) for i, (q, d) in enumerate(chains)]
    aq_k = [jnp.where(incl[d], amat[i][LANE:, LANE:], zero_b) for i, (q, d) in enumerate(chains)]
    av = [jnp.dot(jnp.concatenate([a_kk[i], aq_k[i]], axis=0), vstk[q],
                  preferred_element_type=F32) for i, (q, d) in enumerate(chains)]
    n = len(chains)
    bdot = lambda a, b: jnp.dot(a, b, preferred_element_type=F32)
    tinv = [eye_f - a for a in a_kb]
    qpow = [bdot(a, a).astype(BF16) for a in a_kb_b]
    for _ in range(4):
        prod = [bdot(qpow[i], jnp.concatenate([qpow[i], tinv[i].astype(BF16)], axis=1))
                for i in range(n)]
        qpow = [x[:, :LANE].astype(BF16) for x in prod]
        tinv = [tinv[i] + prod[i][:, LANE:] for i in range(n)]
    tinv = [tinv[i] + bdot(qpow[i], tinv[i].astype(BF16)) for i in range(n)]
    resid = [eye_f - _mm2l(eye_f + a_kb[i], tinv[i]) for i in range(n)]
    tinv = [tinv[i] + _mm(tinv[i], resid[i]) for i in range(n)]
    x = [bdot(tinv[i].astype(BF16),
              jnp.concatenate([kkd_s[i], av[i][:LANE].astype(BF16)], axis=1)) for i in range(n)]
    xb = [v.astype(BF16) for v in x]
    qy = [jnp.concatenate([rd_s[i], av[i][LANE:]], axis=1) - bdot(aq_b[i], xb[i])
          for i in range(n)]
    e_end = [jnp.exp(ltot[i] - lc[i]) for i in range(n)]
    ends = [jnp.concatenate([stack_b(-bb[i] * e_end[i]), stack_b(kd[i] * e_end[i])], axis=0)
            for i in range(n)]
    wuv = [jnp.concatenate([xb[i], jnp.concatenate([zero_b, vstk[q]], axis=1)], axis=0)
           for i, (q, d) in enumerate(chains)]
    mg = [lax.dot_general(ends[i], wuv[i], TN, preferred_element_type=F32) for i in range(n)]
    for i, (q, d) in enumerate(chains):
        m_ref[0, 0, d, q] = fold(jnp.where(eye, jnp.exp(ltot[i]), 0.0)
                                 + mg[i][:, :LANE]).astype(BF16)
        g_ref[0, 0, d, q] = fold(mg[i][:, LANE:])
        qt_ref[0, d, :, qls[q]] = fold(qy[i][:, :LANE]).astype(BF16)
    for q in pairs:
        yl_ref[0, :, qls[q]] = (fold(qy[2 * q][:, LANE:])
                                + fold(qy[2 * q + 1][:, LANE:])).astype(yl_ref.dtype)
        bn_ref[0, :, qls[q]] = (bonus[2 * q] + bonus[2 * q + 1]).astype(bn_ref.dtype)


def _rwkv_local(z, w0p, w2p, a0p, a2p, k_k, k_a, r_k):
    b, t, _ = z.shape
    w = k_k.shape[1]
    npair = w // LANE
    pp = PAIRS_PER_STEP
    ng = npair // pp
    wl = pp * LANE
    nc = t // CHUNK
    tokc = lambda base: pl.BlockSpec((1, CHUNK, wl), lambda bi, ci, p: (bi, ci, base + p))
    perp3 = lambda n: pl.BlockSpec((pp, n, 2 * LANE), lambda bi, ci, p: (p, 0, 0))
    vecp = pl.BlockSpec((1, wl), lambda bi, ci, p: (0, p))
    mat = pl.BlockSpec((1, 1, 2, pp, CHUNK, LANE), lambda bi, ci, p: (bi, ci, 0, p, 0, 0))
    return pl.pallas_call(
        _rwkv_local_body,
        grid=(b, nc, ng),
        in_specs=[tokc(0), tokc(ng), tokc(2 * ng),
                  pl.BlockSpec((1, CHUNK, 2 * LANE), lambda bi, ci, p: (bi, ci, 3 * npair // 2)),
                  perp3(1), perp3(LANE), perp3(1), perp3(LANE), vecp, vecp, vecp],
        out_specs=[mat, mat,
                   pl.BlockSpec((1, 2, CHUNK, wl), lambda bi, ci, p: (bi, 0, ci, p)),
                   pl.BlockSpec((1, CHUNK, wl), lambda bi, ci, p: (bi, ci, p)),
                   pl.BlockSpec((1, CHUNK, wl), lambda bi, ci, p: (bi, ci, p))],
        out_shape=[jax.ShapeDtypeStruct((b, nc, 2, npair, CHUNK, LANE), BF16),
                   jax.ShapeDtypeStruct((b, nc, 2, npair, CHUNK, LANE), F32),
                   jax.ShapeDtypeStruct((b, 2, t, w), BF16),
                   jax.ShapeDtypeStruct((b, t, w), ACT),
                   jax.ShapeDtypeStruct((b, t, w), ACT)],
        compiler_params=_cparams(("parallel", "parallel", "parallel")),
        name="rwkv_local",
    )(z, z, z, z, w0p, w2p, a0p, a2p, k_k, k_a, r_k)


SCAN_CHUNKS = 4


def _rwkv_scan_body(m0_ref, g0_ref, q0_ref, m1_ref, g1_ref, q1_ref, h0_ref,
                    y0_ref, y1_ref, hfin_ref, h_scr, *, npair, cs):
    ci = pl.program_id(1)

    @pl.when(ci == 0)
    def _():
        h_scr[...] = h0_ref[0]

    head0 = _iota2((1, LANE), 1) < RWKV_HEAD

    def expand(x):
        z = jnp.zeros_like(x)
        return jnp.concatenate([jnp.where(head0, x, z), jnp.where(head0, z, x)], axis=0)

    refs = ((m0_ref, g0_ref, q0_ref, y0_ref), (m1_ref, g1_ref, q1_ref, y1_ref))
    chains = [(d, p) for d in range(2) for p in range(npair)]
    lanes = [slice(p * LANE, (p + 1) * LANE) for p in range(npair)]
    h = [h_scr[d, p] for d, p in chains]
    for step in range(cs):
        ck = (step, cs - 1 - step)
        rows = [slice(c * CHUNK, (c + 1) * CHUNK) for c in ck]
        hb = [x.astype(BF16) for x in h]
        ys = [jnp.dot(refs[d][2][0, 0, rows[d], lanes[p]], hb[i], preferred_element_type=F32)
              for i, (d, p) in enumerate(chains)]
        for i, (d, p) in enumerate(chains):
            refs[d][3][0, rows[d], lanes[p]] = ys[i].astype(refs[d][3].dtype)
        h = [jnp.dot(expand(refs[d][0][0, ck[d], 0, p]), hb[i], preferred_element_type=F32)
             + expand(refs[d][1][0, ck[d], 0, p]) for i, (d, p) in enumerate(chains)]
    for i, (d, p) in enumerate(chains):
        h_scr[d, p] = h[i]

    @pl.when(ci == pl.num_programs(1) - 1)
    def _():
        hfin_ref[0] = h_scr[...]


def _rwkv_scan(mm, gg, qt, h0):
    b, nc, _, npair, _, _ = mm.shape
    t, w = qt.shape[2], qt.shape[3]
    cs = SCAN_CHUNKS if nc % SCAN_CHUNKS == 0 else 1
    nb = nc // cs
    fwd = lambda bi, ci: (bi, ci, 0, 0, 0, 0)
    rev = lambda bi, ci: (bi, nb - 1 - ci, 1, 0, 0, 0)
    mblk = (1, cs, 1, npair, CHUNK, LANE)
    hspec = pl.BlockSpec((1, 2, npair, LANE, LANE), lambda bi, ci: (bi, 0, 0, 0, 0))
    return pl.pallas_call(
        functools.partial(_rwkv_scan_body, npair=npair, cs=cs),
        grid=(b, nb),
        in_specs=[pl.BlockSpec(mblk, fwd), pl.BlockSpec(mblk, fwd),
                  pl.BlockSpec((1, 1, cs * CHUNK, w), lambda bi, ci: (bi, 0, ci, 0)),
                  pl.BlockSpec(mblk, rev), pl.BlockSpec(mblk, rev),
                  pl.BlockSpec((1, 1, cs * CHUNK, w), lambda bi, ci: (bi, 1, nb - 1 - ci, 0)),
                  hspec],
        out_specs=[pl.BlockSpec((1, cs * CHUNK, w), lambda bi, ci: (bi, ci, 0)),
                   pl.BlockSpec((1, cs * CHUNK, w), lambda bi, ci: (bi, nb - 1 - ci, 0)),
                   hspec],
        out_shape=[jax.ShapeDtypeStruct((b, t, w), ACT), jax.ShapeDtypeStruct((b, t, w), ACT),
                   jax.ShapeDtypeStruct(h0.shape, F32)],
        scratch_shapes=[pltpu.VMEM((2, npair, LANE, LANE), F32)],
        compiler_params=_cparams(("parallel", "arbitrary")),
        name="rwkv_scan",
    )(mm, gg, qt, mm, gg, qt, h0)


def _rwkv_out_body(y0_ref, y1_ref, yl_ref, bn_ref, gate_ref, x_ref, gl_ref, gnw_ref, gnb_ref,
                   w_ref, o_ref):
    y = y0_ref[0].astype(F32) + y1_ref[0].astype(F32) + yl_ref[0].astype(F32)
    r2 = _iota2((LANE, LANE), 0)
    c2 = _iota2((LANE, LANE), 1)
    avg = ((r2 // RWKV_HEAD) == (c2 // RWKV_HEAD)).astype(F32) * (1.0 / RWKV_HEAD)
    parts = []
    for p in range(y.shape[1] // LANE):
        yp = y[:, p * LANE:(p + 1) * LANE]
        dl = yp - _mm2r(yp, avg)
        var = _mm2r(dl * dl, avg)
        parts.append(dl * lax.rsqrt(var + GN_EPS))
    yn = jnp.concatenate(parts, axis=1)
    gt = gate_ref[0].astype(F32)
    act = (yn * gnw_ref[...] + gnb_ref[...] + bn_ref[0].astype(F32)) * (gt * _sigmoid(gt))
    out = jnp.dot(act.astype(BF16), w_ref[...], preferred_element_type=F32)
    o_ref[0] = x_ref[0] + gl_ref[0] * out


def _rwkv_out(y0, y1, yl, bn, gate, x, gl, gnw, gnb, w, tm):
    b, t, d = x.shape
    tm = min(tm, t)
    wd = y0.shape[2]
    tok = lambda n: pl.BlockSpec((1, tm, n), lambda bi, i: (bi, i, 0))
    return pl.pallas_call(
        _rwkv_out_body,
        grid=(b, t // tm),
        in_specs=[tok(wd), tok(wd), tok(wd), tok(wd), tok(wd), tok(d),
                  pl.BlockSpec((1, 1, d), lambda bi, i: (bi, 0, 0)),
                  pl.BlockSpec((1, wd), lambda bi, i: (0, 0)),
                  pl.BlockSpec((1, wd), lambda bi, i: (0, 0)),
                  pl.BlockSpec(w.shape, lambda bi, i: (0, 0))],
        out_specs=tok(d),
        out_shape=jax.ShapeDtypeStruct((b, t, d), F32),
        compiler_params=_cparams(("parallel", "parallel")),
        name="rwkv_out",
    )(y0, y1, yl, bn, gate, x, gl, gnw, gnb, w)


def _rope_tables(t):
    rows = t // GRID_W
    row = jnp.repeat(jnp.arange(rows, dtype=F32), GRID_W)
    col = jnp.tile(jnp.arange(GRID_W, dtype=F32), rows)
    inv = 1.0 / (ROPE_BASE ** (jnp.arange(ROPE_FREQS, dtype=F32) / ROPE_FREQS))
    ang = jnp.stack([row[:, None] * inv, col[:, None] * inv], axis=1)
    cos, sin = jnp.cos(ang), jnp.sin(ang)
    zeros = jnp.zeros_like(sin)
    ones_lo = jnp.ones((t, QK_NOPE), F32)
    pad_hi = HEAD_SLOT - QK_HEAD
    cos_t = jnp.concatenate([ones_lo, jnp.concatenate([cos, cos], axis=2).reshape(t, QK_ROPE),
                             jnp.ones((t, pad_hi), F32)], axis=1)
    sa = jnp.concatenate([jnp.zeros((t, QK_NOPE), F32),
                          jnp.concatenate([-sin, zeros], axis=2).reshape(t, QK_ROPE),
                          jnp.zeros((t, pad_hi), F32)], axis=1)
    sb = jnp.concatenate([jnp.zeros((t, QK_NOPE), F32),
                          jnp.concatenate([zeros, sin], axis=2).reshape(t, QK_ROPE),
                          jnp.zeros((t, pad_hi), F32)], axis=1)
    return cos_t, sa, sb


def _even_layer(x, ctx, mod_l, mod_c, need_ctx, g, w_in, kv_norm, q_norm, w_uq, w_ukv,
                q_head_norm, k_head_norm, w_fnet, w_out):
    b, s, d = x.shape
    tc = ctx.shape[1]
    e_q0 = KV_LORA + QK_ROPE
    e_f0 = e_q0 + Q_LORA
    e_g0 = e_f0 + FNET_GROUPS * FNET_GROUP_DIM
    w_p = jnp.concatenate([w_in[:, e_g0:], w_in[:, e_f0:e_g0], w_in[:, :e_q0],
                           jnp.zeros((d, LANE - QK_ROPE), F32), w_in[:, e_q0:e_f0]],
                          axis=1).astype(BF16)
    splits = (d, FNET_GROUPS * FNET_GROUP_DIM, KV_LORA + LANE + Q_LORA)
    kvw = w_ukv.reshape(KV_LORA, MLA_HEADS, QK_NOPE + V_HEAD)
    wk = jnp.pad(kvw[:, :, :QK_NOPE], ((0, 0), (0, 0), (0, HEAD_SLOT - QK_NOPE)))
    wk = wk.reshape(KV_LORA, MLA_HEADS * HEAD_SLOT).astype(BF16)
    wv = jnp.pad(kvw[:, :, QK_NOPE:], ((0, 0), (0, 0), (0, HEAD_SLOT - V_HEAD)))
    wv = wv.reshape(KV_LORA, MLA_HEADS * HEAD_SLOT).astype(BF16)
    wq = jnp.pad(w_uq.reshape(Q_LORA, MLA_HEADS, QK_HEAD), ((0, 0), (0, 0), (0, HEAD_SLOT - QK_HEAD)))
    wq = wq.reshape(Q_LORA, MLA_HEADS * HEAD_SLOT).astype(BF16)
    kg = jnp.pad(k_head_norm, (0, HEAD_SLOT - QK_HEAD)).reshape(1, HEAD_SLOT)
    qg = (jnp.pad(q_head_norm, (0, HEAD_SLOT - QK_HEAD))
          * (QK_HEAD ** -0.5 * math.log2(math.e))).reshape(1, HEAD_SLOT)
    kvn, qn = kv_norm.reshape(1, -1), q_norm.reshape(1, -1)
    g2 = g.reshape(1, d)
    bound = (1.02 * QK_HEAD * jnp.max(jnp.abs(qg)) * jnp.max(jnp.abs(kg))).astype(BF16).astype(F32)
    static_ok = bound <= MAX_STATIC_BOUND
    bias_lane = (jnp.arange(HEAD_SLOT) == BIAS_LANE).astype(F32).reshape(1, HEAD_SLOT)
    kb = bias_lane * jnp.where(static_ok, -bound, 0.0)
    qb = bias_lane

    gate_l, four_l, ua_l = _proj(x, g2, mod_l[1], mod_l[0], w_p, splits, 512)
    gate_c, four_c, ua_c = _proj(ctx, g2, mod_c[1], mod_c[0], w_p, splits, 512)
    sk = s + tc
    q_l, k_all, v_all = _qkv(ua_l, kvn, qn, wk, wv, wq, kg, qg, kb, qb, _rope_tables(s), True,
                             256, sk)
    dummy = jnp.zeros((tc, LANE), F32)
    q_c, k_all, v_all = _qkv(ua_c, kvn, qn, wk, wv, wq, kg, qg, kb, qb, (dummy, dummy, dummy),
                             False, 256, sk, kv_into=(k_all, v_all))
    bk = 768 if sk % 768 == 0 else tc
    o_l = _attention(q_l, k_all, v_all, static_ok, 2048, bk)
    f_l = _fourier_latent(four_l, w_fnet)
    wo = w_out.astype(BF16)
    x_new = _merge(o_l, f_l, gate_l, x, mod_l[2], wo, 512)
    ctx_new = ctx
    if need_ctx:
        o_c = _attention(q_c, k_all, v_all, static_ok, tc, tc, k_start=s, k_rows=tc)
        f_c = _fourier_dense(four_c, w_fnet)
        ctx_new = _merge(o_c, f_c, gate_c, ctx, mod_c[2], wo, 512)
    return x_new, ctx_new


def _odd_layer(x, ctx, mod_l, mod_c, need_ctx, g, w_in, shift_w, w0, w2, a0, a2, k_k, k_a, r_k,
               gn_w, gn_b, w_out):
    b, s, d = x.shape
    w = k_k.shape[0]
    npair = w // LANE
    o_wd0 = 2 * w
    o_r0 = o_wd0 + 2 * DECAY_LORA + 2 * AAA_LORA
    conv_ch = o_r0 + w
    perm = lambda m: jnp.concatenate([m[:, :o_wd0], m[:, o_r0:conv_ch], m[:, o_wd0:o_r0]], axis=1)
    w_p = jnp.concatenate([perm(w_in), w_in[:, conv_ch:]], axis=1).astype(BF16)
    sw = perm(shift_w)
    g2 = g.reshape(1, d)

    def pairs(vec2):
        return vec2.reshape(2, npair, LANE).transpose(1, 0, 2).reshape(npair, 1, 2 * LANE)

    def pair_mats(m):
        rr = m.shape[1]
        mp = m.reshape(2, rr, npair, LANE).transpose(2, 0, 1, 3)
        z = jnp.zeros_like(mp[:, 0])
        top = jnp.concatenate([mp[:, 0], z], axis=2)
        bot = jnp.concatenate([z, mp[:, 1]], axis=2)
        return jnp.concatenate([top, bot], axis=1).astype(BF16)

    w0p, a0p, w2p, a2p = pairs(w0), pairs(a0), pair_mats(w2), pair_mats(a2)
    kk2, ka2, rk2 = k_k.reshape(1, w), k_a.reshape(1, w), r_k.reshape(1, w)
    wo = w_out.astype(BF16)

    def mix(xin, mod, h0):
        z, gate = _proj_shift(xin, g2, mod[1], mod[0], w_p, sw, conv_ch, 512)
        mm, gg, qt, yl, bn = _rwkv_local(z, w0p, w2p, a0p, a2p, kk2, ka2, rk2)
        y0, y1, hfin = _rwkv_scan(mm, gg, qt, h0)
        return (y0, y1, yl, bn, gate), hfin

    h_zero = jnp.zeros((b, 2, npair, LANE, LANE), F32)
    parts_c, h_ctx = mix(ctx, mod_c, h_zero)
    parts_l, _ = mix(x, mod_l, h_ctx)
    x_new = _rwkv_out(*parts_l, x, mod_l[2], gn_w.reshape(1, w), gn_b.reshape(1, w), wo, 256)
    ctx_new = ctx
    if need_ctx:
        ctx_new = _rwkv_out(*parts_c, ctx, mod_c[2], gn_w.reshape(1, w), gn_b.reshape(1, w), wo, 256)
    return x_new, ctx_new


def kernel(x, c, ctx, c_ctx, ada_w, ada_b, norm_g, e_w_in, e_kv_norm, e_q_norm, e_w_uq, e_w_ukv,
           e_q_head_norm, e_k_head_norm, e_w_fnet, e_w_out, o_w_in, o_shift_w, o_w0, o_w2, o_a0,
           o_a2, o_k_k, o_k_a, o_r_k, o_gn_w, o_gn_b, o_w_out):
    b, s, d = x.shape
    depth = ada_w.shape[0]
    assert b + 1 <= 8
    cond8 = jnp.concatenate([c, c_ctx[None, :], jnp.zeros((8 - b - 1, d), F32)], axis=0)
    mod = _ada(cond8, ada_w, ada_b)
    for layer in range(depth):
        need_ctx = layer < depth - 1
        m = mod[layer]
        chunk = lambda rows, i: rows[:, None, i * d:(i + 1) * d]
        lat, cx = m[:b], jnp.broadcast_to(m[b:b + 1], (b, 3 * d))
        mod_l = (chunk(lat, 0), 1.0 + chunk(lat, 1), chunk(lat, 2))
        mod_c = (chunk(cx, 0), 1.0 + chunk(cx, 1), chunk(cx, 2))
        j = layer // 2
        if layer % 2 == 0:
            x, ctx = _even_layer(x, ctx, mod_l, mod_c, need_ctx, norm_g[layer], e_w_in[j],
                                 e_kv_norm[j], e_q_norm[j], e_w_uq[j], e_w_ukv[j],
                                 e_q_head_norm[j], e_k_head_norm[j], e_w_fnet[j], e_w_out[j])
        else:
            x, ctx = _odd_layer(x, ctx, mod_l, mod_c, need_ctx, norm_g[layer], o_w_in[j],
                                o_shift_w[j], o_w0[j], o_w2[j], o_a0[j], o_a2[j], o_k_k[j],
                                o_k_a[j], o_r_k[j].reshape(-1), o_gn_w[j], o_gn_b[j], o_w_out[j])
    return x
```

```python
import functools
import math

import numpy as np
import jax
import jax.numpy as jnp
from jax import lax
from jax.experimental import pallas as pl
from jax.experimental.pallas import tpu as pltpu

F32 = jnp.float32
BF16 = jnp.bfloat16
ACT = BF16

GRID_W = 64
NORM_EPS = 1e-6
MLA_HEADS = 8
QK_NOPE = 64
QK_ROPE = 32
QK_HEAD = QK_NOPE + QK_ROPE
V_HEAD = 64
Q_LORA = 384
KV_LORA = 256
ROPE_FREQS = QK_ROPE // 4
ROPE_BASE = 10000.0
FNET_GROUPS = 4
FNET_GROUP_DIM = 128
RWKV_HEAD = 64
DECAY_LORA = 64
AAA_LORA = 64
GN_EPS = 64e-5

LANE = 128
CHUNK = 64
HEAD_SLOT = 128
VMEM_LIMIT = 56 * 1024 * 1024

NN = (((1,), (0,)), ((), ()))
NT = (((1,), (1,)), ((), ()))
TN = (((0,), (0,)), ((), ()))


def _cparams(sem):
    return pltpu.CompilerParams(dimension_semantics=sem, vmem_limit_bytes=VMEM_LIMIT)


def _mm(a, b, dn=NN):
    return lax.dot_general(a.astype(BF16), b.astype(BF16), dn, preferred_element_type=F32)


def _split(a):
    hi = a.astype(BF16)
    lo = (a - hi.astype(F32)).astype(BF16)
    return hi, lo


def _mm3(a, b, dn=NN):
    ah, al = _split(a)
    bh, bl = _split(b)
    d = lambda x, y: lax.dot_general(x, y, dn, preferred_element_type=F32)
    return d(ah, bh) + d(al, bh) + d(ah, bl)


def _mm2r(a, b_exact):
    ah, al = _split(a)
    bb = b_exact.astype(BF16)
    return jnp.dot(jnp.concatenate([ah, al], axis=1), jnp.concatenate([bb, bb], axis=0),
                   preferred_element_type=F32)


def _mm2l(a_exact, b):
    bh, bl = _split(b)
    n = b.shape[1]
    y = jnp.dot(a_exact.astype(BF16), jnp.concatenate([bh, bl], axis=1),
                preferred_element_type=F32)
    return y[:, :n] + y[:, n:]


def _sigmoid(x):
    return 1.0 / (1.0 + jnp.exp(-x))


def _modnorm(x, g, sc1, sh):
    y = x * lax.rsqrt(jnp.mean(x * x, axis=-1, keepdims=True) + NORM_EPS)
    return (y * g) * sc1 + sh


def _iota2(shape, dim):
    return lax.broadcasted_iota(jnp.int32, shape, dim)


def _ada_body(c_ref, w_ref, b_ref, o_ref):
    c = c_ref[...]
    s = c * _sigmoid(c)
    o_ref[0] = _mm3(s, w_ref[0]) + b_ref[0]


def _ada(cond8, ada_w, ada_b):
    depth, d, n = ada_w.shape
    tn = 512
    return pl.pallas_call(
        _ada_body,
        grid=(depth, n // tn),
        in_specs=[
            pl.BlockSpec((8, d), lambda l, j: (0, 0)),
            pl.BlockSpec((1, d, tn), lambda l, j: (l, 0, j)),
            pl.BlockSpec((1, 1, tn), lambda l, j: (l, 0, j)),
        ],
        out_specs=pl.BlockSpec((1, 8, tn), lambda l, j: (l, 0, j)),
        out_shape=jax.ShapeDtypeStruct((depth, 8, n), F32),
        compiler_params=_cparams(("parallel", "parallel")),
        name="ada",
    )(cond8, ada_w, ada_b.reshape(depth, 1, n))


COL_CHUNK = 512


def _proj_body(x_ref, g_ref, sc_ref, sh_ref, w_ref, *o_refs, splits):
    h = _modnorm(x_ref[0], g_ref[...], sc_ref[0], sh_ref[0]).astype(BF16)
    off = 0
    for o_ref, n in zip(o_refs, splits):
        for c0 in range(0, n, COL_CHUNK):
            c1 = min(n, c0 + COL_CHUNK)
            o_ref[0, :, c0:c1] = jnp.dot(h, w_ref[:, off + c0:off + c1],
                                         preferred_element_type=F32).astype(o_ref.dtype)
        off += n


def _proj(x, g, sc1, sh, w, splits, tm):
    b, t, d = x.shape
    tm = min(tm, t)
    n = w.shape[1]
    vec = pl.BlockSpec((1, 1, d), lambda bi, i: (bi, 0, 0))
    return pl.pallas_call(
        functools.partial(_proj_body, splits=splits),
        grid=(b, t // tm),
        in_specs=[
            pl.BlockSpec((1, tm, d), lambda bi, i: (bi, i, 0)),
            pl.BlockSpec((1, d), lambda bi, i: (0, 0)),
            vec, vec,
            pl.BlockSpec((d, n), lambda bi, i: (0, 0)),
        ],
        out_specs=[pl.BlockSpec((1, tm, s), lambda bi, i: (bi, i, 0)) for s in splits],
        out_shape=[jax.ShapeDtypeStruct((b, t, s), ACT) for s in splits],
        compiler_params=_cparams(("parallel", "parallel")),
        name="proj",
    )(x, g, sc1, sh, w)


HALO = 16


def _proj_shift_body(x_ref, xp_ref, xn_ref, g_ref, sc_ref, sh_ref, w_ref, sw_ref, z_ref, gate_ref,
                     *, tm, n_conv):
    i = pl.program_id(1)
    last = pl.num_programs(1) - 1
    g, sc1, sh = g_ref[...], sc_ref[0], sh_ref[0]
    h = _modnorm(x_ref[0], g, sc1, sh)
    hp = _modnorm(xp_ref[0], g, sc1, sh) * (i > 0).astype(F32)
    hn = _modnorm(xn_ref[0], g, sc1, sh) * (i < last).astype(F32)
    hb = jnp.concatenate([hp, h, hn], axis=0).astype(BF16)
    rows = tm + 2 * HALO
    for c0 in range(0, n_conv, COL_CHUNK):
        c1 = min(n_conv, c0 + COL_CHUNK)
        u = jnp.dot(hb, w_ref[:, c0:c1], preferred_element_type=F32)
        up = pltpu.roll(u, 1, 0)[HALO:HALO + tm]
        un = pltpu.roll(u, rows - 1, 0)[HALO:HALO + tm]
        um = u[HALO:HALO + tm]
        z_ref[0, :, c0:c1] = (sw_ref[0:1, c0:c1] * up + sw_ref[1:2, c0:c1] * um
                              + sw_ref[2:3, c0:c1] * un).astype(z_ref.dtype)
    hc = hb[HALO:HALO + tm]
    n_all = w_ref.shape[1]
    for c0 in range(n_conv, n_all, COL_CHUNK):
        c1 = min(n_all, c0 + COL_CHUNK)
        gate_ref[0, :, c0 - n_conv:c1 - n_conv] = jnp.dot(
            hc, w_ref[:, c0:c1], preferred_element_type=F32).astype(gate_ref.dtype)


def _proj_shift(x, g, sc1, sh, w, sw, n_conv, tm):
    b, t, d = x.shape
    tm = min(tm, t)
    n = w.shape[1]
    hb = tm // HALO
    nhb = t // HALO
    vec = pl.BlockSpec((1, 1, d), lambda bi, i: (bi, 0, 0))
    return pl.pallas_call(
        functools.partial(_proj_shift_body, tm=tm, n_conv=n_conv),
        grid=(b, t // tm),
        in_specs=[
            pl.BlockSpec((1, tm, d), lambda bi, i: (bi, i, 0)),
            pl.BlockSpec((1, HALO, d), lambda bi, i: (bi, jnp.maximum(i * hb - 1, 0), 0)),
            pl.BlockSpec((1, HALO, d), lambda bi, i: (bi, jnp.minimum((i + 1) * hb, nhb - 1), 0)),
            pl.BlockSpec((1, d), lambda bi, i: (0, 0)),
            vec, vec,
            pl.BlockSpec((d, n), lambda bi, i: (0, 0)),
            pl.BlockSpec((3, n_conv), lambda bi, i: (0, 0)),
        ],
        out_specs=[pl.BlockSpec((1, tm, n_conv), lambda bi, i: (bi, i, 0)),
                   pl.BlockSpec((1, tm, n - n_conv), lambda bi, i: (bi, i, 0))],
        out_shape=[jax.ShapeDtypeStruct((b, t, n_conv), ACT),
                   jax.ShapeDtypeStruct((b, t, n - n_conv), ACT)],
        compiler_params=_cparams(("parallel", "parallel")),
        name="proj_shift",
    )(x, x, x, g, sc1, sh, w, sw)


def _rms(x, g):
    return x * lax.rsqrt(jnp.mean(x * x, axis=-1, keepdims=True) + NORM_EPS) * g


def _qkv_body(ua_ref, kvn_ref, qn_ref, wk_ref, wv_ref, wq_ref, kg_ref, qg_ref, kb_ref, qb_ref,
              cos_ref, sa_ref, sb_ref, q_ref, k_ref, v_ref):
    ua = ua_ref[0].astype(F32)
    ckv = _rms(ua[:, :KV_LORA], kvn_ref[...]).astype(BF16)
    kr = ua[:, KV_LORA:KV_LORA + LANE]
    cq = _rms(ua[:, KV_LORA + LANE:], qn_ref[...]).astype(BF16)
    kn = jnp.dot(ckv, wk_ref[...], preferred_element_type=F32)
    vv = jnp.dot(ckv, wv_ref[...], preferred_element_type=F32)
    qq = jnp.dot(cq, wq_ref[...], preferred_element_type=F32)
    ones_hi = (_iota2((1, HEAD_SLOT), 1) >= V_HEAD).astype(F32)
    pe = pltpu.roll(kr, QK_NOPE, 1)
    kg, qg = kg_ref[...], qg_ref[...]
    inv_n = 1.0 / QK_HEAD

    def finish(xh, gain):
        xh = xh * lax.rsqrt(jnp.sum(xh * xh, axis=-1, keepdims=True) * inv_n + NORM_EPS) * gain
        return (xh * cos_ref[...] + pltpu.roll(xh, LANE - ROPE_FREQS, 1) * sa_ref[...]
                + pltpu.roll(xh, ROPE_FREQS, 1) * sb_ref[...])

    for h in range(MLA_HEADS):
        sl = slice(h * HEAD_SLOT, (h + 1) * HEAD_SLOT)
        k_ref[0, h] = (finish(kn[:, sl] + pe, kg) + kb_ref[...]).astype(BF16)
        q_ref[0, h] = (finish(qq[:, sl], qg) + qb_ref[...]).astype(BF16)
        v_ref[0, h] = (vv[:, sl] + ones_hi).astype(BF16)


def _qkv(ua, kvn, qn, wk, wv, wq, kg, qg, kb, qb, tabs, tm):
    b, t, wa = ua.shape
    tm = min(tm, t)
    full = lambda a: pl.BlockSpec(a.shape, lambda bi, i: (0,) * a.ndim)
    tab = pl.BlockSpec((tm, LANE), lambda bi, i: (i, 0))
    head = pl.BlockSpec((1, MLA_HEADS, tm, HEAD_SLOT), lambda bi, i: (bi, 0, i, 0))
    shape = jax.ShapeDtypeStruct((b, MLA_HEADS, t, HEAD_SLOT), BF16)
    return pl.pallas_call(
        _qkv_body,
        grid=(b, t // tm),
        in_specs=[pl.BlockSpec((1, tm, wa), lambda bi, i: (bi, i, 0)),
                  full(kvn), full(qn), full(wk), full(wv), full(wq), full(kg), full(qg),
                  full(kb), full(qb), tab, tab, tab],
        out_specs=[head, head, head],
        out_shape=[shape, shape, shape],
        compiler_params=_cparams(("parallel", "parallel")),
        name="qkv",
    )(ua, kvn, qn, wk, wv, wq, kg, qg, kb, qb, *tabs)


BIAS_LANE = QK_HEAD
MAX_STATIC_BOUND = 50.0


def _attn_finish(acc_ref, o_ref):
    bq = acc_ref.shape[1]
    lane = _iota2((bq, LANE), 1)
    o0 = acc_ref[0] / pltpu.roll(acc_ref[0], V_HEAD, 1)
    o1 = acc_ref[1] / pltpu.roll(acc_ref[1], V_HEAD, 1)
    o_ref[0] = jnp.where(lane < V_HEAD, o0, pltpu.roll(o1, V_HEAD, 1)).astype(o_ref.dtype)


def _attn_static_body(q_ref, k_ref, v_ref, o_ref, acc_ref, *, bk):
    acc_ref[...] = jnp.zeros(acc_ref.shape, F32)

    def step(j, carry):
        rows = pl.ds(pl.multiple_of(j * bk, bk), bk)
        s = [lax.dot_general(q_ref[0, hh], k_ref[0, hh, rows, :], NT, preferred_element_type=F32)
             for hh in range(2)]
        p = [jnp.exp2(x.astype(BF16)) for x in s]
        for hh in range(2):
            acc_ref[hh] += jnp.dot(p[hh], v_ref[0, hh, rows, :], preferred_element_type=F32)
        return carry

    lax.fori_loop(0, k_ref.shape[2] // bk, step, 0)
    _attn_finish(acc_ref, o_ref)


def _attn_online_body(q_ref, k_ref, v_ref, o_ref, acc_ref, m_ref, *, bk):
    acc_ref[...] = jnp.zeros(acc_ref.shape, F32)
    m_ref[...] = jnp.full(m_ref.shape, -jnp.inf, F32)

    def step(j, carry):
        rows = pl.ds(pl.multiple_of(j * bk, bk), bk)
        for hh in range(2):
            s = lax.dot_general(q_ref[0, hh], k_ref[0, hh, rows, :], NT,
                                preferred_element_type=F32)
            m_prev = m_ref[hh]
            m_new = jnp.maximum(m_prev, jnp.max(s, axis=-1, keepdims=True))
            p = jnp.exp2(s - m_new)
            acc_ref[hh] = (jnp.exp2(m_prev - m_new) * acc_ref[hh]
                           + jnp.dot(p.astype(BF16), v_ref[0, hh, rows, :],
                                     preferred_element_type=F32))
            m_ref[hh] = m_new
        return carry

    lax.fori_loop(0, k_ref.shape[2] // bk, step, 0)
    _attn_finish(acc_ref, o_ref)


def _attention(q, k, v, static_ok, q_start, q_rows, k_start, k_rows, bq, bk):
    b, h, _, e = q.shape
    bq, bk = min(bq, q_rows), min(bk, k_rows)
    qi0, kj0 = q_start // bq, k_start // k_rows
    kv_blk = pl.BlockSpec((1, 2, k_rows, e), lambda bi, p, i: (bi, p, kj0, 0))

    def call(online):
        scratch = [pltpu.VMEM((2, bq, LANE), F32)]
        if online:
            scratch.append(pltpu.VMEM((2, bq, 1), F32))
        return pl.pallas_call(
            functools.partial(_attn_online_body if online else _attn_static_body, bk=bk),
            grid=(b, h // 2, q_rows // bq),
            in_specs=[pl.BlockSpec((1, 2, bq, e), lambda bi, p, i: (bi, p, qi0 + i, 0)),
                      kv_blk, kv_blk],
            out_specs=pl.BlockSpec((1, bq, 2 * V_HEAD), lambda bi, p, i: (bi, i, p)),
            out_shape=jax.ShapeDtypeStruct((b, q_rows, h * V_HEAD), ACT),
            scratch_shapes=scratch,
            compiler_params=_cparams(("parallel", "parallel", "arbitrary")),
            name="attention_online" if online else "attention",
        )(q, k, v)

    return lax.cond(static_ok, lambda: call(False), lambda: call(True))


def _dft_mats(n):
    idx = np.arange(n)
    ang = 2.0 * np.pi * ((idx[:, None] * idx[None, :]) % n) / n
    return np.cos(ang), np.sin(ang)


def _hilo(a):
    a = jnp.asarray(a, F32)
    hi = a.astype(BF16)
    return hi, (a - hi.astype(F32)).astype(BF16)


def _mm3c(ah, al, b, dn=NN):
    bh, bl = _split(b)
    d = lambda x, y: lax.dot_general(x, y, dn, preferred_element_type=F32)
    return d(ah, bh) + d(al, bh) + d(ah, bl)


def _four_rows_body(x_ref, w_ref, tc_ref, ts_ref, o_ref):
    r = tc_ref.shape[0]
    y = jnp.dot(w_ref[...], x_ref[0, 0], preferred_element_type=F32)
    yc, ys = y[:r], y[r:]
    tc, ts = tc_ref[...], ts_ref[...]
    o_ref[0, 0, :r] = (yc * tc - ys * ts).astype(o_ref.dtype)
    o_ref[0, 0, r:] = (yc * ts + ys * tc).astype(o_ref.dtype)


def _four_cols_body(y_ref, w_ref, cs_ref, wf_ref, o_ref, y3_scr, *, krt, scale):
    def one(j, carry):
        rows = pl.ds(pl.multiple_of(j * GRID_W, GRID_W), GRID_W)
        ycs = jnp.concatenate([y_ref[0, 0, 0, rows, :], y_ref[0, 0, 1, rows, :]], axis=0)
        y3 = jnp.dot(w_ref[...], ycs, preferred_element_type=F32)
        y3_scr[rows, :] = jnp.concatenate([y3[:GRID_W], y3[GRID_W:]], axis=1).astype(BF16)
        return carry

    lax.fori_loop(0, krt, one, 0, unroll=8)
    f = jnp.dot(y3_scr[...], cs_ref[...], preferred_element_type=F32) * scale
    o_ref[0, 0] = _mm(f, wf_ref[0]).astype(o_ref.dtype)


def _fourier_latent(xf, w_fnet):
    b, t, _ = xf.shape
    g, gd = FNET_GROUPS, FNET_GROUP_DIM
    r = t // GRID_W
    wide = GRID_W * gd
    xv = xf.reshape(b, r, GRID_W, g, gd).transpose(0, 3, 1, 2, 4).reshape(b, g, r, wide)
    cr, sr = _dft_mats(r)
    w_rows = jnp.asarray(np.concatenate([cr, sr], axis=0), BF16)
    kr_i, c_i = np.arange(r)[:, None], np.arange(GRID_W)[None, :]
    ang = 2.0 * np.pi * ((kr_i * c_i) % t) / t
    twc = jnp.repeat(jnp.asarray(np.cos(ang), F32), gd, axis=1)
    tws = jnp.repeat(jnp.asarray(np.sin(ang), F32), gd, axis=1)
    tl = min(2048, wide)
    y2 = pl.pallas_call(
        _four_rows_body,
        grid=(b, g, wide // tl),
        in_specs=[pl.BlockSpec((1, 1, r, tl), lambda bi, gi, l: (bi, gi, 0, l)),
                  pl.BlockSpec((2 * r, r), lambda bi, gi, l: (0, 0)),
                  pl.BlockSpec((r, tl), lambda bi, gi, l: (0, l)),
                  pl.BlockSpec((r, tl), lambda bi, gi, l: (0, l))],
        out_specs=pl.BlockSpec((1, 1, 2 * r, tl), lambda bi, gi, l: (bi, gi, 0, l)),
        out_shape=jax.ShapeDtypeStruct((b, g, 2 * r, wide), ACT),
        compiler_params=_cparams(("parallel", "parallel", "parallel")),
        name="fourier_rows",
    )(xv, w_rows, twc, tws)
    y2v = y2.reshape(b, g, 2, r * GRID_W, gd)
    c64, s64 = _dft_mats(GRID_W)
    w_cols = jnp.asarray(np.block([[c64, -s64], [s64, c64]]), BF16)
    cc, sc = _dft_mats(gd)
    w_chan = jnp.asarray(np.concatenate([cc, -sc], axis=0), BF16)
    krt = min(32, r)
    const = lambda a: pl.BlockSpec(a.shape, lambda bi, gi, i: (0, 0))
    fo = pl.pallas_call(
        functools.partial(_four_cols_body, krt=krt, scale=1.0 / math.sqrt(t * gd)),
        grid=(b, g, r // krt),
        in_specs=[pl.BlockSpec((1, 1, 2, krt * GRID_W, gd), lambda bi, gi, i: (bi, gi, 0, i, 0)),
                  const(w_cols), const(w_chan),
                  pl.BlockSpec((1, gd, gd), lambda bi, gi, i: (gi, 0, 0))],
        out_specs=pl.BlockSpec((1, 1, krt * GRID_W, gd), lambda bi, gi, i: (bi, gi, i, 0)),
        out_shape=jax.ShapeDtypeStruct((b, g, r * GRID_W, gd), ACT),
        scratch_shapes=[pltpu.VMEM((krt * GRID_W, 2 * gd), BF16)],
        compiler_params=_cparams(("parallel", "parallel", "parallel")),
        name="fourier_cols",
    )(y2v, w_cols, w_chan, w_fnet)
    return fo.reshape(b, g, r, GRID_W, gd).transpose(0, 3, 2, 1, 4).reshape(b, t, g * gd)


def _four_dense_body(x_ref, ch_ref, cl_ref, th_ref, tl_ref, sh_ref, sl_ref, wf_ref, o_ref, *, scale):
    x = x_ref[0]
    xh, xl = _split(x)
    d = lambda a, b: jnp.dot(a, b, preferred_element_type=F32)
    z = d(xh, ch_ref[...]) + d(xl, ch_ref[...]) + d(xh, cl_ref[...])
    zc, zs = z[:, :FNET_GROUP_DIM], z[:, FNET_GROUP_DIM:]
    f = (_mm3c(th_ref[...], tl_ref[...], zc) - _mm3c(sh_ref[...], sl_ref[...], zs)) * scale
    o_ref[0] = _mm3(f, wf_ref[0]).astype(o_ref.dtype)


def _fourier_dense(xf, w_fnet):
    b, t, _ = xf.shape
    g, gd = FNET_GROUPS, FNET_GROUP_DIM
    cc, sc = _dft_mats(gd)
    ch, cl = _hilo(np.concatenate([cc, sc], axis=1))
    ct, st = _dft_mats(t)
    cth, ctl = _hilo(ct)
    sth, stl = _hilo(st)
    sq = pl.BlockSpec((t, t), lambda bi, gi: (0, 0))
    cs = pl.BlockSpec((gd, 2 * gd), lambda bi, gi: (0, 0))
    return pl.pallas_call(
        functools.partial(_four_dense_body, scale=1.0 / math.sqrt(t * gd)),
        grid=(b, g),
        in_specs=[pl.BlockSpec((1, t, gd), lambda bi, gi: (bi, 0, gi)), cs, cs, sq, sq, sq, sq,
                  pl.BlockSpec((1, gd, gd), lambda bi, gi: (gi, 0, 0))],
        out_specs=pl.BlockSpec((1, t, gd), lambda bi, gi: (bi, 0, gi)),
        out_shape=jax.ShapeDtypeStruct((b, t, g * gd), ACT),
        compiler_params=_cparams(("parallel", "parallel")),
        name="fourier_dense",
    )(xf, ch, cl, cth, ctl, sth, stl, w_fnet)


def _merge_body(o_ref, f_ref, gate_ref, x_ref, gl_ref, w_ref, out_ref):
    gt = gate_ref[0].astype(F32)
    mix = jnp.concatenate([o_ref[0], f_ref[0]], axis=-1).astype(F32) * (gt * _sigmoid(gt))
    y = jnp.dot(mix.astype(BF16), w_ref[...], preferred_element_type=F32)
    out_ref[0] = x_ref[0] + gl_ref[0] * y


def _merge(o, f, gate, x, gl, w, tm):
    b, t, d = x.shape
    tm = min(tm, t)
    half = o.shape[2]
    tok = lambda n: pl.BlockSpec((1, tm, n), lambda bi, i: (bi, i, 0))
    return pl.pallas_call(
        _merge_body,
        grid=(b, t // tm),
        in_specs=[tok(half), tok(half), tok(d), tok(d),
                  pl.BlockSpec((1, 1, d), lambda bi, i: (bi, 0, 0)),
                  pl.BlockSpec(w.shape, lambda bi, i: (0, 0))],
        out_specs=tok(d),
        out_shape=jax.ShapeDtypeStruct((b, t, d), F32),
        compiler_params=_cparams(("parallel", "parallel")),
        name="merge",
    )(o, f, gate, x, gl, w)


EXP_M05 = math.exp(-0.5)


PAIRS_PER_STEP = 8


def _rwkv_local_body(zk_ref, zv_ref, zr_ref, zwa_ref, w0_ref, w2_ref, a0_ref, a2_ref,
                     kk_ref, ka_ref, rk_ref, m_ref, g_ref, qt_ref, yl_ref, bn_ref):
    c = CHUNK
    zwa = zwa_ref[0].astype(F32)
    lora_w, lora_a = jnp.tanh(zwa[:, :LANE]), zwa[:, LANE:]

    head0 = _iota2((1, LANE), 1) < RWKV_HEAD
    r2 = _iota2((LANE, LANE), 0)
    c2 = _iota2((LANE, LANE), 1)
    same = (r2 // RWKV_HEAD) == (c2 // RWKV_HEAD)
    ones_bd = same.astype(F32)
    eye = r2 == c2
    eye_f = eye.astype(F32)
    rc = _iota2((c, c), 0)
    cc = _iota2((c, c), 1)

    def stack(x):
        z = jnp.zeros_like(x)
        return jnp.concatenate([jnp.where(head0, x, z), jnp.where(head0, z, x)], axis=0)

    stack_b = lambda x: stack(x.astype(BF16))
    fold = lambda x: x[:c] + x[c:]

    pairs = range(PAIRS_PER_STEP)
    chains = [(q, d) for q in pairs for d in range(2)]
    qls = [slice(q * LANE, (q + 1) * LANE) for q in pairs]
    ks = [zk_ref[0, :, ql].astype(F32) for ql in qls]
    vs_ = [zv_ref[0, :, ql].astype(F32) for ql in qls]
    rs = [zr_ref[0, :, ql].astype(F32) for ql in qls]
    wraw = [_mm(lora_w, w2_ref[q]) + w0_ref[q] for q in pairs]
    araw = [_mm(lora_a, a2_ref[q]) + a0_ref[q] for q in pairs]
    logw = [-EXP_M05 * _sigmoid(w) for w in wraw]
    a_all = [_sigmoid(a) for a in araw]
    kk0 = [ks[q] * kk_ref[:, qls[q]] for q in pairs]
    ss = [_mm2r(x * x, ones_bd) for x in kk0]
    kk = [kk0[q] / jnp.maximum(jnp.sqrt(ss[q]), 1e-12) for q in pairs]
    vstk = [stack_b(v) for v in vs_]

    dsl = [slice(d * LANE, (d + 1) * LANE) for d in range(2)]
    lw = [logw[q][:, dsl[d]] for q, d in chains]
    ad = [a_all[q][:, dsl[d]] for q, d in chains]
    kd = [ks[q] * (1.0 + (ad[i] - 1.0) * ka_ref[:, qls[q]]) for i, (q, d) in enumerate(chains)]
    bb = [kk[q] * ad[i] for i, (q, d) in enumerate(chains)]
    bonus = [_mm2r(rs[q] * kd[i] * rk_ref[:, qls[q]], ones_bd) * vs_[q]
             for i, (q, d) in enumerate(chains)]
    r3 = _iota2((c, 3 * c), 0)
    c3 = _iota2((c, 3 * c), 1) & (c - 1)
    tri3 = [(c3 <= r3).astype(BF16), (c3 >= r3).astype(BF16)]
    strict = [same & (c2 < r2), same & (c2 > r2)]
    incl = [same & (c2 <= r2), same & (c2 >= r2)]

    def cumsum(x, tri):
        xh, xl = _split(x)
        xll = (x - xh.astype(F32) - xl.astype(F32)).astype(BF16)
        return jnp.dot(tri, jnp.concatenate([xh, xl, xll], axis=0), preferred_element_type=F32)

    lc = [cumsum(lw[i], tri3[d]) for i, (q, d) in enumerate(chains)]
    ltot = [lc[i][c - 1:c] if d == 0 else lc[i][0:1] for i, (q, d) in enumerate(chains)]
    kkd_s = [stack_b(kk[q] * jnp.exp(lc[i] - lw[i])) for i, (q, d) in enumerate(chains)]
    rd_s = [stack(rs[q] * jnp.exp(lc[i])) for i, (q, d) in enumerate(chains)]
    e_inv = [jnp.exp(-x) for x in lc]
    inv_s = [jnp.concatenate([stack_b(bb[i] * e_inv[i]), stack_b(kd[i] * e_inv[i])], axis=0)
             for i in range(len(chains))]
    amat = [lax.dot_general(jnp.concatenate([kkd_s[i], rd_s[i].astype(BF16)], axis=0), inv_s[i],
                            NT, preferred_element_type=F32).astype(BF16)
            for i in range(len(chains))]
    zero_b = jnp.zeros((LANE, LANE), BF16)
    a_kb_b = [jnp.where(strict[d], amat[i][:LANE, :LANE], zero_b)
              for i, (q, d) in enumerate(chains)]
    a_kb = [x.astype(F32) for x in a_kb_b]
    a_kk = [jnp.where(strict[d], amat[i][:LANE, LANE:], zero_b)
            for i, (q, d) in enumerate(chains)]
    aq_b = [jnp.where(incl[d], amat[i][LANE:, :LANE], zero_b) for i, (q, d) in enumerate(chains)]
    aq_k = [jnp.where(incl[d], amat[i][LANE:, LANE:], zero_b) for i, (q, d) in enumerate(chains)]
    av = [jnp.dot(jnp.concatenate([a_kk[i], aq_k[i]], axis=0), vstk[q],
                  preferred_element_type=F32) for i, (q, d) in enumerate(chains)]
    n = len(chains)
    bdot = lambda a, b: jnp.dot(a, b, preferred_element_type=F32)
    tinv = [eye_f - a for a in a_kb]
    qpow = [bdot(a, a).astype(BF16) for a in a_kb_b]
    for _ in range(4):
        prod = [bdot(qpow[i], jnp.concatenate([qpow[i], tinv[i].astype(BF16)], axis=1))
                for i in range(n)]
        qpow = [x[:, :LANE].astype(BF16) for x in prod]
        tinv = [tinv[i] + prod[i][:, LANE:] for i in range(n)]
    tinv = [tinv[i] + bdot(qpow[i], tinv[i].astype(BF16)) for i in range(n)]
    resid = [eye_f - _mm2l(eye_f + a_kb[i], tinv[i]) for i in range(n)]
    tinv = [tinv[i] + _mm(tinv[i], resid[i]) for i in range(n)]
    x = [bdot(tinv[i].astype(BF16),
              jnp.concatenate([kkd_s[i], av[i][:LANE].astype(BF16)], axis=1)) for i in range(n)]
    xb = [v.astype(BF16) for v in x]
    qy = [jnp.concatenate([rd_s[i], av[i][LANE:]], axis=1) - bdot(aq_b[i], xb[i])
          for i in range(n)]
    e_end = [jnp.exp(ltot[i] - lc[i]) for i in range(n)]
    ends = [jnp.concatenate([stack_b(-bb[i] * e_end[i]), stack_b(kd[i] * e_end[i])], axis=0)
            for i in range(n)]
    wuv = [jnp.concatenate([xb[i], jnp.concatenate([zero_b, vstk[q]], axis=1)], axis=0)
           for i, (q, d) in enumerate(chains)]
    mg = [lax.dot_general(ends[i], wuv[i], TN, preferred_element_type=F32) for i in range(n)]
    for i, (q, d) in enumerate(chains):
        m_ref[0, 0, d, q] = fold(jnp.where(eye, jnp.exp(ltot[i]), 0.0)
                                 + mg[i][:, :LANE]).astype(BF16)
        g_ref[0, 0, d, q] = fold(mg[i][:, LANE:]).astype(g_ref.dtype)
        qt_ref[0, d, :, qls[q]] = fold(qy[i][:, :LANE]).astype(BF16)
    for q in pairs:
        yl_ref[0, :, qls[q]] = (fold(qy[2 * q][:, LANE:])
                                + fold(qy[2 * q + 1][:, LANE:])).astype(yl_ref.dtype)
        bn_ref[0, :, qls[q]] = (bonus[2 * q] + bonus[2 * q + 1]).astype(bn_ref.dtype)


def _rwkv_local(z, w0p, w2p, a0p, a2p, k_k, k_a, r_k):
    b, t, _ = z.shape
    w = k_k.shape[1]
    npair = w // LANE
    pp = PAIRS_PER_STEP
    ng = npair // pp
    wl = pp * LANE
    nc = t // CHUNK
    tokc = lambda base: pl.BlockSpec((1, CHUNK, wl), lambda bi, ci, p: (bi, ci, base + p))
    perp3 = lambda n: pl.BlockSpec((pp, n, 2 * LANE), lambda bi, ci, p: (p, 0, 0))
    vecp = pl.BlockSpec((1, wl), lambda bi, ci, p: (0, p))
    mat = pl.BlockSpec((1, 1, 2, pp, CHUNK, LANE), lambda bi, ci, p: (bi, ci, 0, p, 0, 0))
    return pl.pallas_call(
        _rwkv_local_body,
        grid=(b, nc, ng),
        in_specs=[tokc(0), tokc(ng), tokc(2 * ng),
                  pl.BlockSpec((1, CHUNK, 2 * LANE), lambda bi, ci, p: (bi, ci, 3 * npair // 2)),
                  perp3(1), perp3(LANE), perp3(1), perp3(LANE), vecp, vecp, vecp],
        out_specs=[mat, mat,
                   pl.BlockSpec((1, 2, CHUNK, wl), lambda bi, ci, p: (bi, 0, ci, p)),
                   pl.BlockSpec((1, CHUNK, wl), lambda bi, ci, p: (bi, ci, p)),
                   pl.BlockSpec((1, CHUNK, wl), lambda bi, ci, p: (bi, ci, p))],
        out_shape=[jax.ShapeDtypeStruct((b, nc, 2, npair, CHUNK, LANE), BF16),
                   jax.ShapeDtypeStruct((b, nc, 2, npair, CHUNK, LANE), ACT),
                   jax.ShapeDtypeStruct((b, 2, t, w), BF16),
                   jax.ShapeDtypeStruct((b, t, w), ACT),
                   jax.ShapeDtypeStruct((b, t, w), ACT)],
        compiler_params=_cparams(("parallel", "parallel", "parallel")),
        name="rwkv_local",
    )(z, z, z, z, w0p, w2p, a0p, a2p, k_k, k_a, r_k)


SCAN_CHUNKS = 4


def _rwkv_scan_body(m0_ref, g0_ref, q0_ref, m1_ref, g1_ref, q1_ref, h0_ref,
                    y0_ref, y1_ref, hfin_ref, h_scr, *, npair, cs):
    ci = pl.program_id(1)

    @pl.when(ci == 0)
    def _():
        h_scr[...] = h0_ref[0]

    head0 = _iota2((1, LANE), 1) < RWKV_HEAD

    def expand(x):
        z = jnp.zeros_like(x)
        return jnp.concatenate([jnp.where(head0, x, z), jnp.where(head0, z, x)], axis=0)

    refs = ((m0_ref, g0_ref, q0_ref, y0_ref), (m1_ref, g1_ref, q1_ref, y1_ref))
    chains = [(d, p) for d in range(2) for p in range(npair)]
    lanes = [slice(p * LANE, (p + 1) * LANE) for p in range(npair)]
    h = [h_scr[d, p] for d, p in chains]
    for step in range(cs):
        ck = (step, cs - 1 - step)
        rows = [slice(c * CHUNK, (c + 1) * CHUNK) for c in ck]
        hb = [x.astype(BF16) for x in h]
        ys = [jnp.dot(refs[d][2][0, 0, rows[d], lanes[p]], hb[i], preferred_element_type=F32)
              for i, (d, p) in enumerate(chains)]
        for i, (d, p) in enumerate(chains):
            refs[d][3][0, rows[d], lanes[p]] = ys[i].astype(refs[d][3].dtype)
        h = [jnp.dot(expand(refs[d][0][0, ck[d], 0, p]), hb[i], preferred_element_type=F32)
             + expand(refs[d][1][0, ck[d], 0, p]) for i, (d, p) in enumerate(chains)]
    for i, (d, p) in enumerate(chains):
        h_scr[d, p] = h[i]

    @pl.when(ci == pl.num_programs(1) - 1)
    def _():
        hfin_ref[0] = h_scr[...]


def _rwkv_scan(mm, gg, qt, h0):
    b, nc, _, npair, _, _ = mm.shape
    t, w = qt.shape[2], qt.shape[3]
    cs = SCAN_CHUNKS if nc % SCAN_CHUNKS == 0 else 1
    nb = nc // cs
    fwd = lambda bi, ci: (bi, ci, 0, 0, 0, 0)
    rev = lambda bi, ci: (bi, nb - 1 - ci, 1, 0, 0, 0)
    mblk = (1, cs, 1, npair, CHUNK, LANE)
    hspec = pl.BlockSpec((1, 2, npair, LANE, LANE), lambda bi, ci: (bi, 0, 0, 0, 0))
    return pl.pallas_call(
        functools.partial(_rwkv_scan_body, npair=npair, cs=cs),
        grid=(b, nb),
        in_specs=[pl.BlockSpec(mblk, fwd), pl.BlockSpec(mblk, fwd),
                  pl.BlockSpec((1, 1, cs * CHUNK, w), lambda bi, ci: (bi, 0, ci, 0)),
                  pl.BlockSpec(mblk, rev), pl.BlockSpec(mblk, rev),
                  pl.BlockSpec((1, 1, cs * CHUNK, w), lambda bi, ci: (bi, 1, nb - 1 - ci, 0)),
                  hspec],
        out_specs=[pl.BlockSpec((1, cs * CHUNK, w), lambda bi, ci: (bi, ci, 0)),
                   pl.BlockSpec((1, cs * CHUNK, w), lambda bi, ci: (bi, nb - 1 - ci, 0)),
                   hspec],
        out_shape=[jax.ShapeDtypeStruct((b, t, w), ACT), jax.ShapeDtypeStruct((b, t, w), ACT),
                   jax.ShapeDtypeStruct(h0.shape, F32)],
        scratch_shapes=[pltpu.VMEM((2, npair, LANE, LANE), F32)],
        compiler_params=_cparams(("parallel", "arbitrary")),
        name="rwkv_scan",
    )(mm, gg, qt, mm, gg, qt, h0)


def _rwkv_out_body(y0_ref, y1_ref, yl_ref, bn_ref, gate_ref, x_ref, gl_ref, gnw_ref, gnb_ref,
                   w_ref, o_ref):
    y = y0_ref[0].astype(F32) + y1_ref[0].astype(F32) + yl_ref[0].astype(F32)
    r2 = _iota2((LANE, LANE), 0)
    c2 = _iota2((LANE, LANE), 1)
    avg = ((r2 // RWKV_HEAD) == (c2 // RWKV_HEAD)).astype(F32) * (1.0 / RWKV_HEAD)
    parts = []
    for p in range(y.shape[1] // LANE):
        yp = y[:, p * LANE:(p + 1) * LANE]
        dl = yp - _mm2r(yp, avg)
        var = _mm2r(dl * dl, avg)
        parts.append(dl * lax.rsqrt(var + GN_EPS))
    yn = jnp.concatenate(parts, axis=1)
    gt = gate_ref[0].astype(F32)
    act = (yn * gnw_ref[...] + gnb_ref[...] + bn_ref[0].astype(F32)) * (gt * _sigmoid(gt))
    out = jnp.dot(act.astype(BF16), w_ref[...], preferred_element_type=F32)
    o_ref[0] = x_ref[0] + gl_ref[0] * out


def _rwkv_out(y0, y1, yl, bn, gate, x, gl, gnw, gnb, w, tm):
    b, t, d = x.shape
    tm = min(tm, t)
    wd = y0.shape[2]
    tok = lambda n: pl.BlockSpec((1, tm, n), lambda bi, i: (bi, i, 0))
    return pl.pallas_call(
        _rwkv_out_body,
        grid=(b, t // tm),
        in_specs=[tok(wd), tok(wd), tok(wd), tok(wd), tok(wd), tok(d),
                  pl.BlockSpec((1, 1, d), lambda bi, i: (bi, 0, 0)),
                  pl.BlockSpec((1, wd), lambda bi, i: (0, 0)),
                  pl.BlockSpec((1, wd), lambda bi, i: (0, 0)),
                  pl.BlockSpec(w.shape, lambda bi, i: (0, 0))],
        out_specs=tok(d),
        out_shape=jax.ShapeDtypeStruct((b, t, d), F32),
        compiler_params=_cparams(("parallel", "parallel")),
        name="rwkv_out",
    )(y0, y1, yl, bn, gate, x, gl, gnw, gnb, w)


def _rope_tables(t):
    rows = t // GRID_W
    row = jnp.repeat(jnp.arange(rows, dtype=F32), GRID_W)
    col = jnp.tile(jnp.arange(GRID_W, dtype=F32), rows)
    inv = 1.0 / (ROPE_BASE ** (jnp.arange(ROPE_FREQS, dtype=F32) / ROPE_FREQS))
    ang = jnp.stack([row[:, None] * inv, col[:, None] * inv], axis=1)
    cos, sin = jnp.cos(ang), jnp.sin(ang)
    zeros = jnp.zeros_like(sin)
    ones_lo = jnp.ones((t, QK_NOPE), F32)
    pad_hi = HEAD_SLOT - QK_HEAD
    cos_t = jnp.concatenate([ones_lo, jnp.concatenate([cos, cos], axis=2).reshape(t, QK_ROPE),
                             jnp.ones((t, pad_hi), F32)], axis=1)
    sa = jnp.concatenate([jnp.zeros((t, QK_NOPE), F32),
                          jnp.concatenate([-sin, zeros], axis=2).reshape(t, QK_ROPE),
                          jnp.zeros((t, pad_hi), F32)], axis=1)
    sb = jnp.concatenate([jnp.zeros((t, QK_NOPE), F32),
                          jnp.concatenate([zeros, sin], axis=2).reshape(t, QK_ROPE),
                          jnp.zeros((t, pad_hi), F32)], axis=1)
    return cos_t, sa, sb


def _even_layer(x, ctx, mod_l, mod_c, need_ctx, g, w_in, kv_norm, q_norm, w_uq, w_ukv,
                q_head_norm, k_head_norm, w_fnet, w_out):
    b, s, d = x.shape
    tc = ctx.shape[1]
    e_q0 = KV_LORA + QK_ROPE
    e_f0 = e_q0 + Q_LORA
    e_g0 = e_f0 + FNET_GROUPS * FNET_GROUP_DIM
    w_p = jnp.concatenate([w_in[:, e_g0:], w_in[:, e_f0:e_g0], w_in[:, :e_q0],
                           jnp.zeros((d, LANE - QK_ROPE), F32), w_in[:, e_q0:e_f0]],
                          axis=1).astype(BF16)
    splits = (d, FNET_GROUPS * FNET_GROUP_DIM, KV_LORA + LANE + Q_LORA)
    kvw = w_ukv.reshape(KV_LORA, MLA_HEADS, QK_NOPE + V_HEAD)
    wk = jnp.pad(kvw[:, :, :QK_NOPE], ((0, 0), (0, 0), (0, HEAD_SLOT - QK_NOPE)))
    wk = wk.reshape(KV_LORA, MLA_HEADS * HEAD_SLOT).astype(BF16)
    wv = jnp.pad(kvw[:, :, QK_NOPE:], ((0, 0), (0, 0), (0, HEAD_SLOT - V_HEAD)))
    wv = wv.reshape(KV_LORA, MLA_HEADS * HEAD_SLOT).astype(BF16)
    wq = jnp.pad(w_uq.reshape(Q_LORA, MLA_HEADS, QK_HEAD), ((0, 0), (0, 0), (0, HEAD_SLOT - QK_HEAD)))
    wq = wq.reshape(Q_LORA, MLA_HEADS * HEAD_SLOT).astype(BF16)
    kg = jnp.pad(k_head_norm, (0, HEAD_SLOT - QK_HEAD)).reshape(1, HEAD_SLOT)
    qg = (jnp.pad(q_head_norm, (0, HEAD_SLOT - QK_HEAD))
          * (QK_HEAD ** -0.5 * math.log2(math.e))).reshape(1, HEAD_SLOT)
    kvn, qn = kv_norm.reshape(1, -1), q_norm.reshape(1, -1)
    g2 = g.reshape(1, d)
    bound = (1.02 * QK_HEAD * jnp.max(jnp.abs(qg)) * jnp.max(jnp.abs(kg))).astype(BF16).astype(F32)
    static_ok = bound <= MAX_STATIC_BOUND
    bias_lane = (jnp.arange(HEAD_SLOT) == BIAS_LANE).astype(F32).reshape(1, HEAD_SLOT)
    kb = bias_lane * jnp.where(static_ok, -bound, 0.0)
    qb = bias_lane

    gate_l, four_l, ua_l = _proj(x, g2, mod_l[1], mod_l[0], w_p, splits, 512)
    gate_c, four_c, ua_c = _proj(ctx, g2, mod_c[1], mod_c[0], w_p, splits, 512)
    sk = s + tc
    cos_t, sa, sb = _rope_tables(s)
    tabs = (jnp.concatenate([cos_t, jnp.ones((tc, HEAD_SLOT), F32)], axis=0),
            jnp.concatenate([sa, jnp.zeros((tc, HEAD_SLOT), F32)], axis=0),
            jnp.concatenate([sb, jnp.zeros((tc, HEAD_SLOT), F32)], axis=0))
    q_all, k_all, v_all = _qkv(jnp.concatenate([ua_l, ua_c], axis=1), kvn, qn, wk, wv, wq,
                               kg, qg, kb, qb, tabs, math.gcd(s, tc))
    bk = 768 if sk % 768 == 0 else tc
    o_l = _attention(q_all, k_all, v_all, static_ok, 0, s, 0, sk, 2048, bk)
    f_l = _fourier_latent(four_l, w_fnet)
    wo = w_out.astype(BF16)
    x_new = _merge(o_l, f_l, gate_l, x, mod_l[2], wo, 512)
    ctx_new = ctx
    if need_ctx:
        o_c = _attention(q_all, k_all, v_all, static_ok, s, tc, s, tc, tc, tc)
        f_c = _fourier_dense(four_c, w_fnet)
        ctx_new = _merge(o_c, f_c, gate_c, ctx, mod_c[2], wo, 512)
    return x_new, ctx_new


def _odd_layer(x, ctx, mod_l, mod_c, need_ctx, g, w_in, shift_w, w0, w2, a0, a2, k_k, k_a, r_k,
               gn_w, gn_b, w_out):
    b, s, d = x.shape
    w = k_k.shape[0]
    npair = w // LANE
    o_wd0 = 2 * w
    o_r0 = o_wd0 + 2 * DECAY_LORA + 2 * AAA_LORA
    conv_ch = o_r0 + w
    perm = lambda m: jnp.concatenate([m[:, :o_wd0], m[:, o_r0:conv_ch], m[:, o_wd0:o_r0]], axis=1)
    w_p = jnp.concatenate([perm(w_in), w_in[:, conv_ch:]], axis=1).astype(BF16)
    sw = perm(shift_w)
    g2 = g.reshape(1, d)

    def pairs(vec2):
        return vec2.reshape(2, npair, LANE).transpose(1, 0, 2).reshape(npair, 1, 2 * LANE)

    def pair_mats(m):
        rr = m.shape[1]
        mp = m.reshape(2, rr, npair, LANE).transpose(2, 0, 1, 3)
        z = jnp.zeros_like(mp[:, 0])
        top = jnp.concatenate([mp[:, 0], z], axis=2)
        bot = jnp.concatenate([z, mp[:, 1]], axis=2)
        return jnp.concatenate([top, bot], axis=1).astype(BF16)

    w0p, a0p, w2p, a2p = pairs(w0), pairs(a0), pair_mats(w2), pair_mats(a2)
    kk2, ka2, rk2 = k_k.reshape(1, w), k_a.reshape(1, w), r_k.reshape(1, w)
    wo = w_out.astype(BF16)

    def mix(xin, mod, h0):
        z, gate = _proj_shift(xin, g2, mod[1], mod[0], w_p, sw, conv_ch, 512)
        mm, gg, qt, yl, bn = _rwkv_local(z, w0p, w2p, a0p, a2p, kk2, ka2, rk2)
        y0, y1, hfin = _rwkv_scan(mm, gg, qt, h0)
        return (y0, y1, yl, bn, gate), hfin

    h_zero = jnp.zeros((b, 2, npair, LANE, LANE), F32)
    parts_c, h_ctx = mix(ctx, mod_c, h_zero)
    parts_l, _ = mix(x, mod_l, h_ctx)
    x_new = _rwkv_out(*parts_l, x, mod_l[2], gn_w.reshape(1, w), gn_b.reshape(1, w), wo, 256)
    ctx_new = ctx
    if need_ctx:
        ctx_new = _rwkv_out(*parts_c, ctx, mod_c[2], gn_w.reshape(1, w), gn_b.reshape(1, w), wo, 256)
    return x_new, ctx_new


def kernel(x, c, ctx, c_ctx, ada_w, ada_b, norm_g, e_w_in, e_kv_norm, e_q_norm, e_w_uq, e_w_ukv,
           e_q_head_norm, e_k_head_norm, e_w_fnet, e_w_out, o_w_in, o_shift_w, o_w0, o_w2, o_a0,
           o_a2, o_k_k, o_k_a, o_r_k, o_gn_w, o_gn_b, o_w_out):
    b, s, d = x.shape
    depth = ada_w.shape[0]
    assert b + 1 <= 8
    cond8 = jnp.concatenate([c, c_ctx[None, :], jnp.zeros((8 - b - 1, d), F32)], axis=0)
    mod = _ada(cond8, ada_w, ada_b)
    for layer in range(depth):
        need_ctx = layer < depth - 1
        m = mod[layer]
        chunk = lambda rows, i: rows[:, None, i * d:(i + 1) * d]
        lat, cx = m[:b], jnp.broadcast_to(m[b:b + 1], (b, 3 * d))
        mod_l = (chunk(lat, 0), 1.0 + chunk(lat, 1), chunk(lat, 2))
        mod_c = (chunk(cx, 0), 1.0 + chunk(cx, 1), chunk(cx, 2))
        j = layer // 2
        if layer % 2 == 0:
            x, ctx = _even_layer(x, ctx, mod_l, mod_c, need_ctx, norm_g[layer], e_w_in[j],
                                 e_kv_norm[j], e_q_norm[j], e_w_uq[j], e_w_ukv[j],
                                 e_q_head_norm[j], e_k_head_norm[j], e_w_fnet[j], e_w_out[j])
        else:
            x, ctx = _odd_layer(x, ctx, mod_l, mod_c, need_ctx, norm_g[layer], o_w_in[j],
                                o_shift_w[j], o_w0[j], o_w2[j], o_a0[j], o_a2[j], o_k_k[j],
                                o_k_a[j], o_r_k[j].reshape(-1), o_gn_w[j], o_gn_b[j], o_w_out[j])
    return x
```

```python
import functools
import math

import numpy as np
import jax
import jax.numpy as jnp
from jax import lax
from jax.experimental import pallas as pl
from jax.experimental.pallas import tpu as pltpu

F32 = jnp.float32
BF16 = jnp.bfloat16
ACT = BF16

GRID_W = 64
NORM_EPS = 1e-6
MLA_HEADS = 8
QK_NOPE = 64
QK_ROPE = 32
QK_HEAD = QK_NOPE + QK_ROPE
V_HEAD = 64
Q_LORA = 384
KV_LORA = 256
ROPE_FREQS = QK_ROPE // 4
ROPE_BASE = 10000.0
FNET_GROUPS = 4
FNET_GROUP_DIM = 128
RWKV_HEAD = 64
DECAY_LORA = 64
AAA_LORA = 64
GN_EPS = 64e-5

LANE = 128
CHUNK = 64
HEAD_SLOT = 128
VMEM_LIMIT = 56 * 1024 * 1024

NN = (((1,), (0,)), ((), ()))
NT = (((1,), (1,)), ((), ()))
TN = (((0,), (0,)), ((), ()))


def _cparams(sem):
    return pltpu.CompilerParams(dimension_semantics=sem, vmem_limit_bytes=VMEM_LIMIT)


def _mm(a, b, dn=NN):
    return lax.dot_general(a.astype(BF16), b.astype(BF16), dn, preferred_element_type=F32)


def _split(a):
    hi = a.astype(BF16)
    lo = (a - hi.astype(F32)).astype(BF16)
    return hi, lo


def _mm3(a, b, dn=NN):
    ah, al = _split(a)
    bh, bl = _split(b)
    d = lambda x, y: lax.dot_general(x, y, dn, preferred_element_type=F32)
    return d(ah, bh) + d(al, bh) + d(ah, bl)


def _mm2r(a, b_exact):
    ah, al = _split(a)
    bb = b_exact.astype(BF16)
    return jnp.dot(jnp.concatenate([ah, al], axis=1), jnp.concatenate([bb, bb], axis=0),
                   preferred_element_type=F32)


def _mm2l(a_exact, b):
    bh, bl = _split(b)
    n = b.shape[1]
    y = jnp.dot(a_exact.astype(BF16), jnp.concatenate([bh, bl], axis=1),
                preferred_element_type=F32)
    return y[:, :n] + y[:, n:]


def _sigmoid(x):
    return 1.0 / (1.0 + jnp.exp(-x))


def _modnorm(x, g, sc1, sh):
    y = x * lax.rsqrt(jnp.mean(x * x, axis=-1, keepdims=True) + NORM_EPS)
    return (y * g) * sc1 + sh


def _iota2(shape, dim):
    return lax.broadcasted_iota(jnp.int32, shape, dim)


def _ada_body(c_ref, w_ref, b_ref, o_ref):
    c = c_ref[...]
    s = c * _sigmoid(c)
    o_ref[0] = _mm3(s, w_ref[0]) + b_ref[0]


def _ada(cond8, ada_w, ada_b):
    depth, d, n = ada_w.shape
    tn = 512
    return pl.pallas_call(
        _ada_body,
        grid=(depth, n // tn),
        in_specs=[
            pl.BlockSpec((8, d), lambda l, j: (0, 0)),
            pl.BlockSpec((1, d, tn), lambda l, j: (l, 0, j)),
            pl.BlockSpec((1, 1, tn), lambda l, j: (l, 0, j)),
        ],
        out_specs=pl.BlockSpec((1, 8, tn), lambda l, j: (l, 0, j)),
        out_shape=jax.ShapeDtypeStruct((depth, 8, n), F32),
        compiler_params=_cparams(("parallel", "parallel")),
        name="ada",
    )(cond8, ada_w, ada_b.reshape(depth, 1, n))


COL_CHUNK = 512


def _proj_body(x_ref, g_ref, sc_ref, sh_ref, w_ref, *o_refs, splits):
    h = _modnorm(x_ref[0], g_ref[...], sc_ref[0], sh_ref[0]).astype(BF16)
    off = 0
    for o_ref, n in zip(o_refs, splits):
        for c0 in range(0, n, COL_CHUNK):
            c1 = min(n, c0 + COL_CHUNK)
            o_ref[0, :, c0:c1] = jnp.dot(h, w_ref[:, off + c0:off + c1],
                                         preferred_element_type=F32).astype(o_ref.dtype)
        off += n


def _proj(x, g, sc1, sh, w, splits, tm):
    b, t, d = x.shape
    tm = min(tm, t)
    n = w.shape[1]
    vec = pl.BlockSpec((1, 1, d), lambda bi, i: (bi, 0, 0))
    return pl.pallas_call(
        functools.partial(_proj_body, splits=splits),
        grid=(b, t // tm),
        in_specs=[
            pl.BlockSpec((1, tm, d), lambda bi, i: (bi, i, 0)),
            pl.BlockSpec((1, d), lambda bi, i: (0, 0)),
            vec, vec,
            pl.BlockSpec((d, n), lambda bi, i: (0, 0)),
        ],
        out_specs=[pl.BlockSpec((1, tm, s), lambda bi, i: (bi, i, 0)) for s in splits],
        out_shape=[jax.ShapeDtypeStruct((b, t, s), ACT) for s in splits],
        compiler_params=_cparams(("parallel", "parallel")),
        name="proj",
    )(x, g, sc1, sh, w)


HALO = 16


def _proj_shift_body(x_ref, xp_ref, xn_ref, g_ref, sc_ref, sh_ref, w_ref, sw_ref, z_ref, gate_ref,
                     *, tm, n_conv, segs):
    i = pl.program_id(1)
    last = pl.num_programs(1) - 1
    g, sc1, sh = g_ref[...], sc_ref[0], sh_ref[0]
    h = _modnorm(x_ref[0], g, sc1, sh)
    hp = _modnorm(xp_ref[0], g, sc1, sh) * (i > 0).astype(F32)
    hn = _modnorm(xn_ref[0], g, sc1, sh) * (i < last).astype(F32)
    hb = jnp.concatenate([hp, h, hn], axis=0).astype(BF16)
    rows = tm + 2 * HALO
    dst = 0
    for src, width in segs:
        for c0 in range(0, width, COL_CHUNK):
            cw = min(COL_CHUNK, width - c0)
            cols = slice(src + c0, src + c0 + cw)
            u = jnp.dot(hb, w_ref[:, cols], preferred_element_type=F32)
            up = pltpu.roll(u, 1, 0)[HALO:HALO + tm]
            un = pltpu.roll(u, rows - 1, 0)[HALO:HALO + tm]
            um = u[HALO:HALO + tm]
            z_ref[0, :, dst + c0:dst + c0 + cw] = (
                sw_ref[0:1, cols] * up + sw_ref[1:2, cols] * um
                + sw_ref[2:3, cols] * un).astype(z_ref.dtype)
        dst += width
    hc = hb[HALO:HALO + tm]
    n_all = w_ref.shape[1]
    for c0 in range(n_conv, n_all, COL_CHUNK):
        c1 = min(n_all, c0 + COL_CHUNK)
        gate_ref[0, :, c0 - n_conv:c1 - n_conv] = jnp.dot(
            hc, w_ref[:, c0:c1], preferred_element_type=F32).astype(gate_ref.dtype)


def _proj_shift(x, g, sc1, sh, w, sw, n_conv, segs, tm):
    b, t, d = x.shape
    tm = min(tm, t)
    n = w.shape[1]
    hb = tm // HALO
    nhb = t // HALO
    vec = pl.BlockSpec((1, 1, d), lambda bi, i: (bi, 0, 0))
    return pl.pallas_call(
        functools.partial(_proj_shift_body, tm=tm, n_conv=n_conv, segs=segs),
        grid=(b, t // tm),
        in_specs=[
            pl.BlockSpec((1, tm, d), lambda bi, i: (bi, i, 0)),
            pl.BlockSpec((1, HALO, d), lambda bi, i: (bi, jnp.maximum(i * hb - 1, 0), 0)),
            pl.BlockSpec((1, HALO, d), lambda bi, i: (bi, jnp.minimum((i + 1) * hb, nhb - 1), 0)),
            pl.BlockSpec((1, d), lambda bi, i: (0, 0)),
            vec, vec,
            pl.BlockSpec((d, n), lambda bi, i: (0, 0)),
            pl.BlockSpec((3, n_conv), lambda bi, i: (0, 0)),
        ],
        out_specs=[pl.BlockSpec((1, tm, n_conv), lambda bi, i: (bi, i, 0)),
                   pl.BlockSpec((1, tm, n - n_conv), lambda bi, i: (bi, i, 0))],
        out_shape=[jax.ShapeDtypeStruct((b, t, n_conv), ACT),
                   jax.ShapeDtypeStruct((b, t, n - n_conv), ACT)],
        compiler_params=_cparams(("parallel", "parallel")),
        name="proj_shift",
    )(x, x, x, g, sc1, sh, w, sw)


def _rms(x, g):
    return x * lax.rsqrt(jnp.mean(x * x, axis=-1, keepdims=True) + NORM_EPS) * g


def _qkv_body(ua_ref, kvn_ref, qn_ref, wk_ref, wv_ref, wq_ref, wqr_ref, kg_ref, kb_ref, qb_ref,
              cosk_ref, cosq_ref, sa_ref, sb_ref, sinq_ref, q_ref, k_ref, v_ref):
    ua = ua_ref[0].astype(F32)
    ckv = _rms(ua[:, :KV_LORA], kvn_ref[...]).astype(BF16)
    kr = ua[:, KV_LORA:KV_LORA + LANE]
    cq = _rms(ua[:, KV_LORA + LANE:], qn_ref[...]).astype(BF16)
    kn = jnp.dot(ckv, wk_ref[...], preferred_element_type=F32)
    vv = jnp.dot(ckv, wv_ref[...], preferred_element_type=F32)
    qq = jnp.dot(cq, wq_ref[...], preferred_element_type=F32)
    qr = jnp.dot(cq, wqr_ref[...], preferred_element_type=F32)
    ones_hi = (_iota2((1, HEAD_SLOT), 1) >= V_HEAD).astype(F32)
    pe = pltpu.roll(kr, QK_NOPE, 1)
    gp = pe * kg_ref[...]
    pe_rot = (pltpu.roll(gp, LANE - ROPE_FREQS, 1) * sa_ref[...]
              + pltpu.roll(gp, ROPE_FREQS, 1) * sb_ref[...])
    cosk, cosq, sinq = cosk_ref[...], cosq_ref[...], sinq_ref[...]
    inv_n = 1.0 / QK_HEAD
    scale = lambda x: lax.rsqrt(jnp.sum(x * x, axis=-1, keepdims=True) * inv_n + NORM_EPS)

    for h in range(MLA_HEADS):
        sl = slice(h * HEAD_SLOT, (h + 1) * HEAD_SLOT)
        kh = kn[:, sl] + pe
        k_ref[0, h] = (scale(kh) * (kh * cosk + pe_rot) + kb_ref[...]).astype(BF16)
        qh = qq[:, sl]
        q_ref[0, h] = (scale(qh) * (qh * cosq + qr[:, sl] * sinq) + qb_ref[...]).astype(BF16)
        v_ref[0, h] = (vv[:, sl] + ones_hi).astype(BF16)


def _qkv(ua, kvn, qn, wk, wv, wq, wqr, kg, kb, qb, tabs, tm):
    b, t, wa = ua.shape
    tm = min(tm, t)
    full = lambda a: pl.BlockSpec(a.shape, lambda bi, i: (0,) * a.ndim)
    tab = pl.BlockSpec((tm, LANE), lambda bi, i: (i, 0))
    head = pl.BlockSpec((1, MLA_HEADS, tm, HEAD_SLOT), lambda bi, i: (bi, 0, i, 0))
    shape = jax.ShapeDtypeStruct((b, MLA_HEADS, t, HEAD_SLOT), BF16)
    return pl.pallas_call(
        _qkv_body,
        grid=(b, t // tm),
        in_specs=[pl.BlockSpec((1, tm, wa), lambda bi, i: (bi, i, 0)),
                  full(kvn), full(qn), full(wk), full(wv), full(wq), full(wqr), full(kg),
                  full(kb), full(qb)] + [tab] * len(tabs),
        out_specs=[head, head, head],
        out_shape=[shape, shape, shape],
        compiler_params=_cparams(("parallel", "parallel")),
        name="qkv",
    )(ua, kvn, qn, wk, wv, wq, wqr, kg, kb, qb, *tabs)


BIAS_LANE = QK_HEAD
MAX_STATIC_BOUND = 50.0


def _attn_finish(acc_ref, o_ref):
    bq = acc_ref.shape[1]
    lane = _iota2((bq, LANE), 1)
    o0 = acc_ref[0] / pltpu.roll(acc_ref[0], V_HEAD, 1)
    o1 = acc_ref[1] / pltpu.roll(acc_ref[1], V_HEAD, 1)
    o_ref[0] = jnp.where(lane < V_HEAD, o0, pltpu.roll(o1, V_HEAD, 1)).astype(o_ref.dtype)


def _attn_static_body(q_ref, k_ref, v_ref, o_ref, acc_ref, *, bk):
    acc_ref[...] = jnp.zeros(acc_ref.shape, F32)

    def step(j, carry):
        rows = pl.ds(pl.multiple_of(j * bk, bk), bk)
        s = [lax.dot_general(q_ref[0, hh], k_ref[0, hh, rows, :], NT, preferred_element_type=F32)
             for hh in range(2)]
        p = [jnp.exp2(x.astype(BF16)) for x in s]
        for hh in range(2):
            acc_ref[hh] += jnp.dot(p[hh], v_ref[0, hh, rows, :], preferred_element_type=F32)
        return carry

    lax.fori_loop(0, k_ref.shape[2] // bk, step, 0)
    _attn_finish(acc_ref, o_ref)


def _attn_online_body(q_ref, k_ref, v_ref, o_ref, acc_ref, m_ref, *, bk):
    acc_ref[...] = jnp.zeros(acc_ref.shape, F32)
    m_ref[...] = jnp.full(m_ref.shape, -jnp.inf, F32)

    def step(j, carry):
        rows = pl.ds(pl.multiple_of(j * bk, bk), bk)
        for hh in range(2):
            s = lax.dot_general(q_ref[0, hh], k_ref[0, hh, rows, :], NT,
                                preferred_element_type=F32)
            m_prev = m_ref[hh]
            m_new = jnp.maximum(m_prev, jnp.max(s, axis=-1, keepdims=True))
            p = jnp.exp2(s - m_new)
            acc_ref[hh] = (jnp.exp2(m_prev - m_new) * acc_ref[hh]
                           + jnp.dot(p.astype(BF16), v_ref[0, hh, rows, :],
                                     preferred_element_type=F32))
            m_ref[hh] = m_new
        return carry

    lax.fori_loop(0, k_ref.shape[2] // bk, step, 0)
    _attn_finish(acc_ref, o_ref)


def _attention(q, k, v, static_ok, q_start, q_rows, k_start, k_rows, bq, bk):
    b, h, _, e = q.shape
    bq, bk = min(bq, q_rows), min(bk, k_rows)
    qi0, kj0 = q_start // bq, k_start // k_rows
    kv_blk = pl.BlockSpec((1, 2, k_rows, e), lambda bi, p, i: (bi, p, kj0, 0))

    def call(online):
        scratch = [pltpu.VMEM((2, bq, LANE), F32)]
        if online:
            scratch.append(pltpu.VMEM((2, bq, 1), F32))
        return pl.pallas_call(
            functools.partial(_attn_online_body if online else _attn_static_body, bk=bk),
            grid=(b, h // 2, q_rows // bq),
            in_specs=[pl.BlockSpec((1, 2, bq, e), lambda bi, p, i: (bi, p, qi0 + i, 0)),
                      kv_blk, kv_blk],
            out_specs=pl.BlockSpec((1, bq, 2 * V_HEAD), lambda bi, p, i: (bi, i, p)),
            out_shape=jax.ShapeDtypeStruct((b, q_rows, h * V_HEAD), ACT),
            scratch_shapes=scratch,
            compiler_params=_cparams(("parallel", "parallel", "arbitrary")),
            name="attention_online" if online else "attention",
        )(q, k, v)

    return lax.cond(static_ok, lambda: call(False), lambda: call(True))


def _dft_mats(n):
    idx = np.arange(n)
    ang = 2.0 * np.pi * ((idx[:, None] * idx[None, :]) % n) / n
    return np.cos(ang), np.sin(ang)


def _hilo(a):
    a = jnp.asarray(a, F32)
    hi = a.astype(BF16)
    return hi, (a - hi.astype(F32)).astype(BF16)


def _mm3c(ah, al, b, dn=NN):
    bh, bl = _split(b)
    d = lambda x, y: lax.dot_general(x, y, dn, preferred_element_type=F32)
    return d(ah, bh) + d(al, bh) + d(ah, bl)


def _four_rows_body(x_ref, w_ref, tc_ref, ts_ref, o_ref):
    r = tc_ref.shape[0]
    y = jnp.dot(w_ref[...], x_ref[0, 0], preferred_element_type=F32)
    yc, ys = y[:r], y[r:]
    tc, ts = tc_ref[...], ts_ref[...]
    o_ref[0, 0, :r] = (yc * tc - ys * ts).astype(o_ref.dtype)
    o_ref[0, 0, r:] = (yc * ts + ys * tc).astype(o_ref.dtype)


def _four_cols_body(y_ref, w_ref, cs_ref, wf_ref, o_ref, y3_scr, *, krt, scale):
    def one(j, carry):
        rows = pl.ds(pl.multiple_of(j * GRID_W, GRID_W), GRID_W)
        ycs = jnp.concatenate([y_ref[0, 0, 0, rows, :], y_ref[0, 0, 1, rows, :]], axis=0)
        y3 = jnp.dot(w_ref[...], ycs, preferred_element_type=F32)
        y3_scr[rows, :] = jnp.concatenate([y3[:GRID_W], y3[GRID_W:]], axis=1).astype(BF16)
        return carry

    lax.fori_loop(0, krt, one, 0, unroll=8)
    f = jnp.dot(y3_scr[...], cs_ref[...], preferred_element_type=F32) * scale
    o_ref[0, 0] = _mm(f, wf_ref[0]).astype(o_ref.dtype)


def _fourier_latent(xf, w_fnet):
    b, t, _ = xf.shape
    g, gd = FNET_GROUPS, FNET_GROUP_DIM
    r = t // GRID_W
    wide = GRID_W * gd
    xv = xf.reshape(b, r, GRID_W, g, gd).transpose(0, 3, 1, 2, 4).reshape(b, g, r, wide)
    cr, sr = _dft_mats(r)
    w_rows = jnp.asarray(np.concatenate([cr, sr], axis=0), BF16)
    kr_i, c_i = np.arange(r)[:, None], np.arange(GRID_W)[None, :]
    ang = 2.0 * np.pi * ((kr_i * c_i) % t) / t
    twc = jnp.repeat(jnp.asarray(np.cos(ang), F32), gd, axis=1)
    tws = jnp.repeat(jnp.asarray(np.sin(ang), F32), gd, axis=1)
    tl = min(2048, wide)
    y2 = pl.pallas_call(
        _four_rows_body,
        grid=(b, g, wide // tl),
        in_specs=[pl.BlockSpec((1, 1, r, tl), lambda bi, gi, l: (bi, gi, 0, l)),
                  pl.BlockSpec((2 * r, r), lambda bi, gi, l: (0, 0)),
                  pl.BlockSpec((r, tl), lambda bi, gi, l: (0, l)),
                  pl.BlockSpec((r, tl), lambda bi, gi, l: (0, l))],
        out_specs=pl.BlockSpec((1, 1, 2 * r, tl), lambda bi, gi, l: (bi, gi, 0, l)),
        out_shape=jax.ShapeDtypeStruct((b, g, 2 * r, wide), ACT),
        compiler_params=_cparams(("parallel", "parallel", "parallel")),
        name="fourier_rows",
    )(xv, w_rows, twc, tws)
    y2v = y2.reshape(b, g, 2, r * GRID_W, gd)
    c64, s64 = _dft_mats(GRID_W)
    w_cols = jnp.asarray(np.block([[c64, -s64], [s64, c64]]), BF16)
    cc, sc = _dft_mats(gd)
    w_chan = jnp.asarray(np.concatenate([cc, -sc], axis=0), BF16)
    krt = min(32, r)
    const = lambda a: pl.BlockSpec(a.shape, lambda bi, gi, i: (0, 0))
    fo = pl.pallas_call(
        functools.partial(_four_cols_body, krt=krt, scale=1.0 / math.sqrt(t * gd)),
        grid=(b, g, r // krt),
        in_specs=[pl.BlockSpec((1, 1, 2, krt * GRID_W, gd), lambda bi, gi, i: (bi, gi, 0, i, 0)),
                  const(w_cols), const(w_chan),
                  pl.BlockSpec((1, gd, gd), lambda bi, gi, i: (gi, 0, 0))],
        out_specs=pl.BlockSpec((1, 1, krt * GRID_W, gd), lambda bi, gi, i: (bi, gi, i, 0)),
        out_shape=jax.ShapeDtypeStruct((b, g, r * GRID_W, gd), ACT),
        scratch_shapes=[pltpu.VMEM((krt * GRID_W, 2 * gd), BF16)],
        compiler_params=_cparams(("parallel", "parallel", "parallel")),
        name="fourier_cols",
    )(y2v, w_cols, w_chan, w_fnet)
    return fo.reshape(b, g, r, GRID_W, gd).transpose(0, 3, 2, 1, 4).reshape(b, t, g * gd)


def _four_dense_body(x_ref, ch_ref, cl_ref, th_ref, tl_ref, sh_ref, sl_ref, wf_ref, o_ref, *, scale):
    x = x_ref[0]
    xh, xl = _split(x)
    d = lambda a, b: jnp.dot(a, b, preferred_element_type=F32)
    z = d(xh, ch_ref[...]) + d(xl, ch_ref[...]) + d(xh, cl_ref[...])
    zc, zs = z[:, :FNET_GROUP_DIM], z[:, FNET_GROUP_DIM:]
    f = (_mm3c(th_ref[...], tl_ref[...], zc) - _mm3c(sh_ref[...], sl_ref[...], zs)) * scale
    o_ref[0] = _mm3(f, wf_ref[0]).astype(o_ref.dtype)


def _fourier_dense(xf, w_fnet):
    b, t, _ = xf.shape
    g, gd = FNET_GROUPS, FNET_GROUP_DIM
    cc, sc = _dft_mats(gd)
    ch, cl = _hilo(np.concatenate([cc, sc], axis=1))
    ct, st = _dft_mats(t)
    cth, ctl = _hilo(ct)
    sth, stl = _hilo(st)
    sq = pl.BlockSpec((t, t), lambda bi, gi: (0, 0))
    cs = pl.BlockSpec((gd, 2 * gd), lambda bi, gi: (0, 0))
    return pl.pallas_call(
        functools.partial(_four_dense_body, scale=1.0 / math.sqrt(t * gd)),
        grid=(b, g),
        in_specs=[pl.BlockSpec((1, t, gd), lambda bi, gi: (bi, 0, gi)), cs, cs, sq, sq, sq, sq,
                  pl.BlockSpec((1, gd, gd), lambda bi, gi: (gi, 0, 0))],
        out_specs=pl.BlockSpec((1, t, gd), lambda bi, gi: (bi, 0, gi)),
        out_shape=jax.ShapeDtypeStruct((b, t, g * gd), ACT),
        compiler_params=_cparams(("parallel", "parallel")),
        name="fourier_dense",
    )(xf, ch, cl, cth, ctl, sth, stl, w_fnet)


def _merge_body(o_ref, f_ref, gate_ref, x_ref, gl_ref, w_ref, out_ref):
    gt = gate_ref[0].astype(F32)
    mix = jnp.concatenate([o_ref[0], f_ref[0]], axis=-1).astype(F32) * (gt * _sigmoid(gt))
    y = jnp.dot(mix.astype(BF16), w_ref[...], preferred_element_type=F32)
    out_ref[0] = x_ref[0] + gl_ref[0] * y


def _merge(o, f, gate, x, gl, w, tm):
    b, t, d = x.shape
    tm = min(tm, t)
    half = o.shape[2]
    tok = lambda n: pl.BlockSpec((1, tm, n), lambda bi, i: (bi, i, 0))
    return pl.pallas_call(
        _merge_body,
        grid=(b, t // tm),
        in_specs=[tok(half), tok(half), tok(d), tok(d),
                  pl.BlockSpec((1, 1, d), lambda bi, i: (bi, 0, 0)),
                  pl.BlockSpec(w.shape, lambda bi, i: (0, 0))],
        out_specs=tok(d),
        out_shape=jax.ShapeDtypeStruct((b, t, d), F32),
        compiler_params=_cparams(("parallel", "parallel")),
        name="merge",
    )(o, f, gate, x, gl, w)


EXP_M05 = math.exp(-0.5)


PAIRS_PER_STEP = 8


def _rwkv_local_body(zk_ref, zv_ref, zr_ref, zwa_ref, w0_ref, w2_ref, a0_ref, a2_ref,
                     kk_ref, ka_ref, rk_ref, m_ref, g_ref, qt_ref, yl_ref, bn_ref):
    c = CHUNK
    zwa = zwa_ref[0].astype(F32)
    lora_w, lora_a = jnp.tanh(zwa[:, :LANE]), zwa[:, LANE:]

    head0 = _iota2((1, LANE), 1) < RWKV_HEAD
    r2 = _iota2((LANE, LANE), 0)
    c2 = _iota2((LANE, LANE), 1)
    same = (r2 // RWKV_HEAD) == (c2 // RWKV_HEAD)
    ones_bd = same.astype(F32)
    eye = r2 == c2
    eye_f = eye.astype(F32)
    rc = _iota2((c, c), 0)
    cc = _iota2((c, c), 1)

    def stack(x):
        z = jnp.zeros_like(x)
        return jnp.concatenate([jnp.where(head0, x, z), jnp.where(head0, z, x)], axis=0)

    stack_b = lambda x: stack(x.astype(BF16))
    fold = lambda x: x[:c] + x[c:]

    pairs = range(PAIRS_PER_STEP)
    chains = [(q, d) for q in pairs for d in range(2)]
    qls = [slice(q * LANE, (q + 1) * LANE) for q in pairs]
    ks = [zk_ref[0, :, ql].astype(F32) for ql in qls]
    vs_ = [zv_ref[0, :, ql].astype(F32) for ql in qls]
    rs = [zr_ref[0, :, ql].astype(F32) for ql in qls]
    wraw = [_mm(lora_w, w2_ref[q]) + w0_ref[q] for q in pairs]
    araw = [_mm(lora_a, a2_ref[q]) + a0_ref[q] for q in pairs]
    logw = [-EXP_M05 * _sigmoid(w) for w in wraw]
    a_all = [_sigmoid(a) for a in araw]
    kk0 = [ks[q] * kk_ref[:, qls[q]] for q in pairs]
    ss = [_mm2r(x * x, ones_bd) for x in kk0]
    kk = [kk0[q] / jnp.maximum(jnp.sqrt(ss[q]), 1e-12) for q in pairs]
    vstk = [stack_b(v) for v in vs_]

    dsl = [slice(d * LANE, (d + 1) * LANE) for d in range(2)]
    lw = [logw[q][:, dsl[d]] for q, d in chains]
    ad = [a_all[q][:, dsl[d]] for q, d in chains]
    kd = [ks[q] * (1.0 + (ad[i] - 1.0) * ka_ref[:, qls[q]]) for i, (q, d) in enumerate(chains)]
    bb = [kk[q] * ad[i] for i, (q, d) in enumerate(chains)]
    bonus = [_mm2r(rs[q] * kd[i] * rk_ref[:, qls[q]], ones_bd) * vs_[q]
             for i, (q, d) in enumerate(chains)]
    r3 = _iota2((c, 3 * c), 0)
    c3 = _iota2((c, 3 * c), 1) & (c - 1)
    tri3 = [(c3 <= r3).astype(BF16), (c3 >= r3).astype(BF16)]
    strict = [same & (c2 < r2), same & (c2 > r2)]
    incl = [same & (c2 <= r2), same & (c2 >= r2)]

    def cumsum(x, tri):
        xh, xl = _split(x)
        xll = (x - xh.astype(F32) - xl.astype(F32)).astype(BF16)
        return jnp.dot(tri, jnp.concatenate([xh, xl, xll], axis=0), preferred_element_type=F32)

    lc = [cumsum(lw[i], tri3[d]) for i, (q, d) in enumerate(chains)]
    ltot = [lc[i][c - 1:c] if d == 0 else lc[i][0:1] for i, (q, d) in enumerate(chains)]
    kkd_s = [stack_b(kk[q] * jnp.exp(lc[i] - lw[i])) for i, (q, d) in enumerate(chains)]
    rd_s = [stack(rs[q] * jnp.exp(lc[i])) for i, (q, d) in enumerate(chains)]
    e_inv = [jnp.exp(-x) for x in lc]
    inv_s = [jnp.concatenate([stack_b(bb[i] * e_inv[i]), stack_b(kd[i] * e_inv[i])], axis=0)
             for i in range(len(chains))]
    amat = [lax.dot_general(jnp.concatenate([kkd_s[i], rd_s[i].astype(BF16)], axis=0), inv_s[i],
                            NT, preferred_element_type=F32).astype(BF16)
            for i in range(len(chains))]
    zero_b = jnp.zeros((LANE, LANE), BF16)
    a_kb_b = [jnp.where(strict[d], amat[i][:LANE, :LANE], zero_b)
              for i, (q, d) in enumerate(chains)]
    a_kb = [x.astype(F32) for x in a_kb_b]
    a_kk = [jnp.where(strict[d], amat[i][:LANE, LANE:], zero_b)
            for i, (q, d) in enumerate(chains)]
    aq_b = [jnp.where(incl[d], amat[i][LANE:, :LANE], zero_b) for i, (q, d) in enumerate(chains)]
    aq_k = [jnp.where(incl[d], amat[i][LANE:, LANE:], zero_b) for i, (q, d) in enumerate(chains)]
    av = [jnp.dot(jnp.concatenate([a_kk[i], aq_k[i]], axis=0), vstk[q],
                  preferred_element_type=F32) for i, (q, d) in enumerate(chains)]
    n = len(chains)
    bdot = lambda a, b: jnp.dot(a, b, preferred_element_type=F32)
    tinv = [eye_f - a for a in a_kb]
    qpow = [bdot(a, a).astype(BF16) for a in a_kb_b]
    for _ in range(4):
        prod = [bdot(qpow[i], jnp.concatenate([qpow[i], tinv[i].astype(BF16)], axis=1))
                for i in range(n)]
        qpow = [x[:, :LANE].astype(BF16) for x in prod]
        tinv = [tinv[i] + prod[i][:, LANE:] for i in range(n)]
    tinv = [tinv[i] + bdot(qpow[i], tinv[i].astype(BF16)) for i in range(n)]
    resid = [eye_f - _mm2l(eye_f + a_kb[i], tinv[i]) for i in range(n)]
    tinv = [tinv[i] + _mm(tinv[i], resid[i]) for i in range(n)]
    x = [bdot(tinv[i].astype(BF16),
              jnp.concatenate([kkd_s[i], av[i][:LANE].astype(BF16)], axis=1)) for i in range(n)]
    xb = [v.astype(BF16) for v in x]
    qy = [jnp.concatenate([rd_s[i], av[i][LANE:]], axis=1) - bdot(aq_b[i], xb[i])
          for i in range(n)]
    e_end = [jnp.exp(ltot[i] - lc[i]) for i in range(n)]
    ends = [jnp.concatenate([stack_b(-bb[i] * e_end[i]), stack_b(kd[i] * e_end[i])], axis=0)
            for i in range(n)]
    wuv = [jnp.concatenate([xb[i], jnp.concatenate([zero_b, vstk[q]], axis=1)], axis=0)
           for i, (q, d) in enumerate(chains)]
    mg = [lax.dot_general(ends[i], wuv[i], TN, preferred_element_type=F32) for i in range(n)]
    for i, (q, d) in enumerate(chains):
        m_ref[0, 0, d, q] = fold(jnp.where(eye, jnp.exp(ltot[i]), 0.0)
                                 + mg[i][:, :LANE]).astype(BF16)
        g_ref[0, 0, d, q] = fold(mg[i][:, LANE:]).astype(g_ref.dtype)
        qt_ref[0, d, :, qls[q]] = fold(qy[i][:, :LANE]).astype(BF16)
    for q in pairs:
        yl_ref[0, :, qls[q]] = (fold(qy[2 * q][:, LANE:])
                                + fold(qy[2 * q + 1][:, LANE:])).astype(yl_ref.dtype)
        bn_ref[0, :, qls[q]] = (bonus[2 * q] + bonus[2 * q + 1]).astype(bn_ref.dtype)


def _rwkv_local(z, w0p, w2p, a0p, a2p, k_k, k_a, r_k):
    b, t, _ = z.shape
    w = k_k.shape[1]
    npair = w // LANE
    pp = PAIRS_PER_STEP
    ng = npair // pp
    wl = pp * LANE
    nc = t // CHUNK
    tokc = lambda base: pl.BlockSpec((1, CHUNK, wl), lambda bi, ci, p: (bi, ci, base + p))
    perp3 = lambda n: pl.BlockSpec((pp, n, 2 * LANE), lambda bi, ci, p: (p, 0, 0))
    vecp = pl.BlockSpec((1, wl), lambda bi, ci, p: (0, p))
    mat = pl.BlockSpec((1, 1, 2, pp, CHUNK, LANE), lambda bi, ci, p: (bi, ci, 0, p, 0, 0))
    return pl.pallas_call(
        _rwkv_local_body,
        grid=(b, nc, ng),
        in_specs=[tokc(0), tokc(ng), tokc(2 * ng),
                  pl.BlockSpec((1, CHUNK, 2 * LANE), lambda bi, ci, p: (bi, ci, 3 * npair // 2)),
                  perp3(1), perp3(LANE), perp3(1), perp3(LANE), vecp, vecp, vecp],
        out_specs=[mat, mat,
                   pl.BlockSpec((1, 2, CHUNK, wl), lambda bi, ci, p: (bi, 0, ci, p)),
                   pl.BlockSpec((1, CHUNK, wl), lambda bi, ci, p: (bi, ci, p)),
                   pl.BlockSpec((1, CHUNK, wl), lambda bi, ci, p: (bi, ci, p))],
        out_shape=[jax.ShapeDtypeStruct((b, nc, 2, npair, CHUNK, LANE), BF16),
                   jax.ShapeDtypeStruct((b, nc, 2, npair, CHUNK, LANE), ACT),
                   jax.ShapeDtypeStruct((b, 2, t, w), BF16),
                   jax.ShapeDtypeStruct((b, t, w), ACT),
                   jax.ShapeDtypeStruct((b, t, w), ACT)],
        compiler_params=_cparams(("parallel", "parallel", "parallel")),
        name="rwkv_local",
    )(z, z, z, z, w0p, w2p, a0p, a2p, k_k, k_a, r_k)


SCAN_CHUNKS = 8


def _rwkv_scan_body(m0_ref, g0_ref, q0_ref, m1_ref, g1_ref, q1_ref, h0_ref,
                    y0_ref, y1_ref, hfin_ref, h_scr, *, npair, cs):
    ci = pl.program_id(1)

    @pl.when(ci == 0)
    def _():
        h_scr[...] = h0_ref[0]

    head0 = _iota2((1, LANE), 1) < RWKV_HEAD

    def expand(x):
        z = jnp.zeros_like(x)
        return jnp.concatenate([jnp.where(head0, x, z), jnp.where(head0, z, x)], axis=0)

    refs = ((m0_ref, g0_ref, q0_ref, y0_ref), (m1_ref, g1_ref, q1_ref, y1_ref))
    chains = [(d, p) for d in range(2) for p in range(npair)]
    lanes = [slice(p * LANE, (p + 1) * LANE) for p in range(npair)]
    h = [h_scr[d, p] for d, p in chains]
    for step in range(cs):
        ck = (step, cs - 1 - step)
        rows = [slice(c * CHUNK, (c + 1) * CHUNK) for c in ck]
        hb = [x.astype(BF16) for x in h]
        ys = [jnp.dot(refs[d][2][0, 0, rows[d], lanes[p]], hb[i], preferred_element_type=F32)
              for i, (d, p) in enumerate(chains)]
        for i, (d, p) in enumerate(chains):
            refs[d][3][0, rows[d], lanes[p]] = ys[i].astype(refs[d][3].dtype)
        h = [jnp.dot(expand(refs[d][0][0, ck[d], 0, p]), hb[i], preferred_element_type=F32)
             + expand(refs[d][1][0, ck[d], 0, p]) for i, (d, p) in enumerate(chains)]
    for i, (d, p) in enumerate(chains):
        h_scr[d, p] = h[i]

    @pl.when(ci == pl.num_programs(1) - 1)
    def _():
        hfin_ref[0] = h_scr[...]


def _rwkv_scan(mm, gg, qt, h0):
    b, nc, _, npair, _, _ = mm.shape
    t, w = qt.shape[2], qt.shape[3]
    cs = SCAN_CHUNKS if nc % SCAN_CHUNKS == 0 else 1
    nb = nc // cs
    fwd = lambda bi, ci: (bi, ci, 0, 0, 0, 0)
    rev = lambda bi, ci: (bi, nb - 1 - ci, 1, 0, 0, 0)
    mblk = (1, cs, 1, npair, CHUNK, LANE)
    hspec = pl.BlockSpec((1, 2, npair, LANE, LANE), lambda bi, ci: (bi, 0, 0, 0, 0))
    return pl.pallas_call(
        functools.partial(_rwkv_scan_body, npair=npair, cs=cs),
        grid=(b, nb),
        in_specs=[pl.BlockSpec(mblk, fwd), pl.BlockSpec(mblk, fwd),
                  pl.BlockSpec((1, 1, cs * CHUNK, w), lambda bi, ci: (bi, 0, ci, 0)),
                  pl.BlockSpec(mblk, rev), pl.BlockSpec(mblk, rev),
                  pl.BlockSpec((1, 1, cs * CHUNK, w), lambda bi, ci: (bi, 1, nb - 1 - ci, 0)),
                  hspec],
        out_specs=[pl.BlockSpec((1, cs * CHUNK, w), lambda bi, ci: (bi, ci, 0)),
                   pl.BlockSpec((1, cs * CHUNK, w), lambda bi, ci: (bi, nb - 1 - ci, 0)),
                   hspec],
        out_shape=[jax.ShapeDtypeStruct((b, t, w), ACT), jax.ShapeDtypeStruct((b, t, w), ACT),
                   jax.ShapeDtypeStruct(h0.shape, F32)],
        scratch_shapes=[pltpu.VMEM((2, npair, LANE, LANE), F32)],
        compiler_params=_cparams(("parallel", "arbitrary")),
        name="rwkv_scan",
    )(mm, gg, qt, mm, gg, qt, h0)


def _rwkv_out_body(y0_ref, y1_ref, yl_ref, bn_ref, gate_ref, x_ref, gl_ref, gnw_ref, gnb_ref,
                   w_ref, o_ref):
    y = y0_ref[0].astype(F32) + y1_ref[0].astype(F32) + yl_ref[0].astype(F32)
    r2 = _iota2((LANE, LANE), 0)
    c2 = _iota2((LANE, LANE), 1)
    avg = ((r2 // RWKV_HEAD) == (c2 // RWKV_HEAD)).astype(F32) * (1.0 / RWKV_HEAD)
    parts = []
    for p in range(y.shape[1] // LANE):
        yp = y[:, p * LANE:(p + 1) * LANE]
        dl = yp - _mm2r(yp, avg)
        var = _mm2r(dl * dl, avg)
        parts.append(dl * lax.rsqrt(var + GN_EPS))
    yn = jnp.concatenate(parts, axis=1)
    gt = gate_ref[0].astype(F32)
    act = (yn * gnw_ref[...] + gnb_ref[...] + bn_ref[0].astype(F32)) * (gt * _sigmoid(gt))
    out = jnp.dot(act.astype(BF16), w_ref[...], preferred_element_type=F32)
    o_ref[0] = x_ref[0] + gl_ref[0] * out


def _rwkv_out(y0, y1, yl, bn, gate, x, gl, gnw, gnb, w, tm):
    b, t, d = x.shape
    tm = min(tm, t)
    wd = y0.shape[2]
    tok = lambda n: pl.BlockSpec((1, tm, n), lambda bi, i: (bi, i, 0))
    return pl.pallas_call(
        _rwkv_out_body,
        grid=(b, t // tm),
        in_specs=[tok(wd), tok(wd), tok(wd), tok(wd), tok(wd), tok(d),
                  pl.BlockSpec((1, 1, d), lambda bi, i: (bi, 0, 0)),
                  pl.BlockSpec((1, wd), lambda bi, i: (0, 0)),
                  pl.BlockSpec((1, wd), lambda bi, i: (0, 0)),
                  pl.BlockSpec(w.shape, lambda bi, i: (0, 0))],
        out_specs=tok(d),
        out_shape=jax.ShapeDtypeStruct((b, t, d), F32),
        compiler_params=_cparams(("parallel", "parallel")),
        name="rwkv_out",
    )(y0, y1, yl, bn, gate, x, gl, gnw, gnb, w)


def _rope_tables(t):
    rows = t // GRID_W
    row = jnp.repeat(jnp.arange(rows, dtype=F32), GRID_W)
    col = jnp.tile(jnp.arange(GRID_W, dtype=F32), rows)
    inv = 1.0 / (ROPE_BASE ** (jnp.arange(ROPE_FREQS, dtype=F32) / ROPE_FREQS))
    ang = jnp.stack([row[:, None] * inv, col[:, None] * inv], axis=1)
    cos, sin = jnp.cos(ang), jnp.sin(ang)
    zeros = jnp.zeros_like(sin)
    ones_lo = jnp.ones((t, QK_NOPE), F32)
    pad_hi = HEAD_SLOT - QK_HEAD
    cos_t = jnp.concatenate([ones_lo, jnp.concatenate([cos, cos], axis=2).reshape(t, QK_ROPE),
                             jnp.ones((t, pad_hi), F32)], axis=1)
    sa = jnp.concatenate([jnp.zeros((t, QK_NOPE), F32),
                          jnp.concatenate([-sin, zeros], axis=2).reshape(t, QK_ROPE),
                          jnp.zeros((t, pad_hi), F32)], axis=1)
    sb = jnp.concatenate([jnp.zeros((t, QK_NOPE), F32),
                          jnp.concatenate([zeros, sin], axis=2).reshape(t, QK_ROPE),
                          jnp.zeros((t, pad_hi), F32)], axis=1)
    return cos_t, sa, sb


def _even_layer(x, ctx, mod_l, mod_c, need_ctx, g, w_in, kv_norm, q_norm, w_uq, w_ukv,
                q_head_norm, k_head_norm, w_fnet, w_out):
    b, s, d = x.shape
    tc = ctx.shape[1]
    e_q0 = KV_LORA + QK_ROPE
    e_f0 = e_q0 + Q_LORA
    e_g0 = e_f0 + FNET_GROUPS * FNET_GROUP_DIM
    w_p = jnp.concatenate([w_in[:, e_g0:], w_in[:, e_f0:e_g0], w_in[:, :e_q0],
                           jnp.zeros((d, LANE - QK_ROPE), F32), w_in[:, e_q0:e_f0]],
                          axis=1).astype(BF16)
    splits = (d, FNET_GROUPS * FNET_GROUP_DIM, KV_LORA + LANE + Q_LORA)
    kvw = w_ukv.reshape(KV_LORA, MLA_HEADS, QK_NOPE + V_HEAD)
    wk = jnp.pad(kvw[:, :, :QK_NOPE], ((0, 0), (0, 0), (0, HEAD_SLOT - QK_NOPE)))
    wk = wk.reshape(KV_LORA, MLA_HEADS * HEAD_SLOT).astype(BF16)
    wv = jnp.pad(kvw[:, :, QK_NOPE:], ((0, 0), (0, 0), (0, HEAD_SLOT - V_HEAD)))
    wv = wv.reshape(KV_LORA, MLA_HEADS * HEAD_SLOT).astype(BF16)
    wq3 = jnp.pad(w_uq.reshape(Q_LORA, MLA_HEADS, QK_HEAD), ((0, 0), (0, 0), (0, HEAD_SLOT - QK_HEAD)))
    wq = wq3.reshape(Q_LORA, MLA_HEADS * HEAD_SLOT).astype(BF16)
    kg = jnp.pad(k_head_norm, (0, HEAD_SLOT - QK_HEAD)).reshape(1, HEAD_SLOT)
    qg = (jnp.pad(q_head_norm, (0, HEAD_SLOT - QK_HEAD))
          * (QK_HEAD ** -0.5 * math.log2(math.e))).reshape(1, HEAD_SLOT)
    lane = np.arange(HEAD_SLOT)
    tail = (lane >= QK_NOPE) & (lane < QK_HEAD)
    first = tail & (((lane - QK_NOPE) // ROPE_FREQS) % 2 == 0)
    partner = np.where(first, lane + ROPE_FREQS, np.where(tail, lane - ROPE_FREQS, lane))
    sign = np.where(first, -1.0, np.where(tail, 1.0, 0.0)).astype(np.float32)
    wqr = (wq3[:, :, partner] * (sign * qg[0, partner])).reshape(Q_LORA, MLA_HEADS * HEAD_SLOT)
    wqr = wqr.astype(BF16)
    kvn, qn = kv_norm.reshape(1, -1), q_norm.reshape(1, -1)
    g2 = g.reshape(1, d)
    bound = (1.02 * QK_HEAD * jnp.max(jnp.abs(qg)) * jnp.max(jnp.abs(kg))).astype(BF16).astype(F32)
    static_ok = bound <= MAX_STATIC_BOUND
    bias_lane = (jnp.arange(HEAD_SLOT) == BIAS_LANE).astype(F32).reshape(1, HEAD_SLOT)
    kb = bias_lane * jnp.where(static_ok, -bound, 0.0)
    qb = bias_lane

    gate_l, four_l, ua_l = _proj(x, g2, mod_l[1], mod_l[0], w_p, splits, 512)
    gate_c, four_c, ua_c = _proj(ctx, g2, mod_c[1], mod_c[0], w_p, splits, 512)
    sk = s + tc
    cos_t, sa, sb = _rope_tables(s)
    cos_t = jnp.concatenate([cos_t, jnp.ones((tc, HEAD_SLOT), F32)], axis=0)
    sa = jnp.concatenate([sa, jnp.zeros((tc, HEAD_SLOT), F32)], axis=0)
    sb = jnp.concatenate([sb, jnp.zeros((tc, HEAD_SLOT), F32)], axis=0)
    tabs = (cos_t * kg, cos_t * qg, sa, sb, sb - sa)
    q_all, k_all, v_all = _qkv(jnp.concatenate([ua_l, ua_c], axis=1), kvn, qn, wk, wv, wq, wqr,
                               kg, kb, qb, tabs, math.gcd(s, tc))
    bk = 768 if sk % 768 == 0 else tc
    o_l = _attention(q_all, k_all, v_all, static_ok, 0, s, 0, sk, 2048, bk)
    f_l = _fourier_latent(four_l, w_fnet)
    wo = w_out.astype(BF16)
    x_new = _merge(o_l, f_l, gate_l, x, mod_l[2], wo, 512)
    ctx_new = ctx
    if need_ctx:
        o_c = _attention(q_all, k_all, v_all, static_ok, s, tc, s, tc, tc, tc)
        f_c = _fourier_dense(four_c, w_fnet)
        ctx_new = _merge(o_c, f_c, gate_c, ctx, mod_c[2], wo, 512)
    return x_new, ctx_new


def _odd_layer(x, ctx, mod_l, mod_c, need_ctx, g, w_in, shift_w, w0, w2, a0, a2, k_k, k_a, r_k,
               gn_w, gn_b, w_out):
    b, s, d = x.shape
    w = k_k.shape[0]
    npair = w // LANE
    o_wd0 = 2 * w
    o_r0 = o_wd0 + 2 * DECAY_LORA + 2 * AAA_LORA
    conv_ch = o_r0 + w
    segs = ((0, o_wd0), (o_r0, w), (o_wd0, o_r0 - o_wd0))
    w_p = w_in.astype(BF16)
    sw = shift_w
    g2 = g.reshape(1, d)

    def pairs(vec2):
        return vec2.reshape(2, npair, LANE).transpose(1, 0, 2).reshape(npair, 1, 2 * LANE)

    def pair_mats(m):
        rr = m.shape[1]
        mp = m.reshape(2, rr, npair, LANE).transpose(2, 0, 1, 3)
        z = jnp.zeros_like(mp[:, 0])
        top = jnp.concatenate([mp[:, 0], z], axis=2)
        bot = jnp.concatenate([z, mp[:, 1]], axis=2)
        return jnp.concatenate([top, bot], axis=1).astype(BF16)

    w0p, a0p, w2p, a2p = pairs(w0), pairs(a0), pair_mats(w2), pair_mats(a2)
    kk2, ka2, rk2 = k_k.reshape(1, w), k_a.reshape(1, w), r_k.reshape(1, w)
    wo = w_out.astype(BF16)

    def mix(xin, mod, h0):
        z, gate = _proj_shift(xin, g2, mod[1], mod[0], w_p, sw, conv_ch, segs, 512)
        mm, gg, qt, yl, bn = _rwkv_local(z, w0p, w2p, a0p, a2p, kk2, ka2, rk2)
        y0, y1, hfin = _rwkv_scan(mm, gg, qt, h0)
        return (y0, y1, yl, bn, gate), hfin

    h_zero = jnp.zeros((b, 2, npair, LANE, LANE), F32)
    parts_c, h_ctx = mix(ctx, mod_c, h_zero)
    parts_l, _ = mix(x, mod_l, h_ctx)
    x_new = _rwkv_out(*parts_l, x, mod_l[2], gn_w.reshape(1, w), gn_b.reshape(1, w), wo, 512)
    ctx_new = ctx
    if need_ctx:
        ctx_new = _rwkv_out(*parts_c, ctx, mod_c[2], gn_w.reshape(1, w), gn_b.reshape(1, w), wo, 256)
    return x_new, ctx_new


def kernel(x, c, ctx, c_ctx, ada_w, ada_b, norm_g, e_w_in, e_kv_norm, e_q_norm, e_w_uq, e_w_ukv,
           e_q_head_norm, e_k_head_norm, e_w_fnet, e_w_out, o_w_in, o_shift_w, o_w0, o_w2, o_a0,
           o_a2, o_k_k, o_k_a, o_r_k, o_gn_w, o_gn_b, o_w_out):
    b, s, d = x.shape
    depth = ada_w.shape[0]
    assert b + 1 <= 8
    cond8 = jnp.concatenate([c, c_ctx[None, :], jnp.zeros((8 - b - 1, d), F32)], axis=0)
    mod = _ada(cond8, ada_w, ada_b)
    for layer in range(depth):
        need_ctx = layer < depth - 1
        m = mod[layer]
        chunk = lambda rows, i: rows[:, None, i * d:(i + 1) * d]
        lat, cx = m[:b], jnp.broadcast_to(m[b:b + 1], (b, 3 * d))
        mod_l = (chunk(lat, 0), 1.0 + chunk(lat, 1), chunk(lat, 2))
        mod_c = (chunk(cx, 0), 1.0 + chunk(cx, 1), chunk(cx, 2))
        j = layer // 2
        if layer % 2 == 0:
            x, ctx = _even_layer(x, ctx, mod_l, mod_c, need_ctx, norm_g[layer], e_w_in[j],
                                 e_kv_norm[j], e_q_norm[j], e_w_uq[j], e_w_ukv[j],
                                 e_q_head_norm[j], e_k_head_norm[j], e_w_fnet[j], e_w_out[j])
        else:
            x, ctx = _odd_layer(x, ctx, mod_l, mod_c, need_ctx, norm_g[layer], o_w_in[j],
                                o_shift_w[j], o_w0[j], o_w2[j], o_a0[j], o_a2[j], o_k_k[j],
                                o_k_a[j], o_r_k[j].reshape(-1), o_gn_w[j], o_gn_b[j], o_w_out[j])
    return x
```

```python
import functools
import math

import numpy as np
import jax
import jax.numpy as jnp
from jax import lax
from jax.experimental import pallas as pl
from jax.experimental.pallas import tpu as pltpu

F32 = jnp.float32
BF16 = jnp.bfloat16
ACT = BF16

GRID_W = 64
NORM_EPS = 1e-6
MLA_HEADS = 8
QK_NOPE = 64
QK_ROPE = 32
QK_HEAD = QK_NOPE + QK_ROPE
V_HEAD = 64
Q_LORA = 384
KV_LORA = 256
ROPE_FREQS = QK_ROPE // 4
ROPE_BASE = 10000.0
FNET_GROUPS = 4
FNET_GROUP_DIM = 128
RWKV_HEAD = 64
DECAY_LORA = 64
AAA_LORA = 64
GN_EPS = 64e-5

LANE = 128
CHUNK = 64
HEAD_SLOT = 128
VMEM_LIMIT = 56 * 1024 * 1024

NN = (((1,), (0,)), ((), ()))
NT = (((1,), (1,)), ((), ()))
TN = (((0,), (0,)), ((), ()))


def _cparams(sem):
    return pltpu.CompilerParams(dimension_semantics=sem, vmem_limit_bytes=VMEM_LIMIT)


def _mm(a, b, dn=NN):
    return lax.dot_general(a.astype(BF16), b.astype(BF16), dn, preferred_element_type=F32)


def _split(a):
    hi = a.astype(BF16)
    lo = (a - hi.astype(F32)).astype(BF16)
    return hi, lo


def _mm3(a, b, dn=NN):
    ah, al = _split(a)
    bh, bl = _split(b)
    d = lambda x, y: lax.dot_general(x, y, dn, preferred_element_type=F32)
    return d(ah, bh) + d(al, bh) + d(ah, bl)


def _mm2r(a, b_exact):
    ah, al = _split(a)
    bb = b_exact.astype(BF16)
    return jnp.dot(jnp.concatenate([ah, al], axis=1), jnp.concatenate([bb, bb], axis=0),
                   preferred_element_type=F32)


def _mm2l(a_exact, b):
    bh, bl = _split(b)
    n = b.shape[1]
    y = jnp.dot(a_exact.astype(BF16), jnp.concatenate([bh, bl], axis=1),
                preferred_element_type=F32)
    return y[:, :n] + y[:, n:]


def _sigmoid(x):
    return 1.0 / (1.0 + jnp.exp(-x))


def _modnorm(x, g, sc1, sh):
    y = x * lax.rsqrt(jnp.mean(x * x, axis=-1, keepdims=True) + NORM_EPS)
    return (y * g) * sc1 + sh


def _iota2(shape, dim):
    return lax.broadcasted_iota(jnp.int32, shape, dim)


def _ada_body(c_ref, w_ref, b_ref, o_ref):
    c = c_ref[...]
    s = c * _sigmoid(c)
    o_ref[0] = _mm3(s, w_ref[0]) + b_ref[0]


def _ada(cond8, ada_w, ada_b):
    depth, d, n = ada_w.shape
    tn = 512
    return pl.pallas_call(
        _ada_body,
        grid=(depth, n // tn),
        in_specs=[
            pl.BlockSpec((8, d), lambda l, j: (0, 0)),
            pl.BlockSpec((1, d, tn), lambda l, j: (l, 0, j)),
            pl.BlockSpec((1, 1, tn), lambda l, j: (l, 0, j)),
        ],
        out_specs=pl.BlockSpec((1, 8, tn), lambda l, j: (l, 0, j)),
        out_shape=jax.ShapeDtypeStruct((depth, 8, n), F32),
        compiler_params=_cparams(("parallel", "parallel")),
        name="ada",
    )(cond8, ada_w, ada_b.reshape(depth, 1, n))


COL_CHUNK = 512


def _proj_body(x_ref, g_ref, sc_ref, sh_ref, w_ref, *o_refs, splits):
    h = _modnorm(x_ref[0], g_ref[...], sc_ref[0], sh_ref[0]).astype(BF16)
    mm = lambda c0, c1: jnp.dot(h, w_ref[:, c0:c1], preferred_element_type=F32)
    off = 0
    for o_ref, n in zip(o_refs, splits):
        if isinstance(n, tuple):
            groups, width = n
            for gi in range(groups):
                o_ref[0, gi] = mm(off + gi * width, off + (gi + 1) * width).astype(o_ref.dtype)
            off += groups * width
            continue
        for c0 in range(0, n, COL_CHUNK):
            c1 = min(n, c0 + COL_CHUNK)
            o_ref[0, :, c0:c1] = mm(off + c0, off + c1).astype(o_ref.dtype)
        off += n


def _proj(x, g, sc1, sh, w, splits, tm):
    b, t, d = x.shape
    tm = min(tm, t)
    n = w.shape[1]
    vec = pl.BlockSpec((1, 1, d), lambda bi, i: (bi, 0, 0))
    specs, shapes = [], []
    for s in splits:
        if isinstance(s, tuple):
            specs.append(pl.BlockSpec((1, s[0], tm, s[1]), lambda bi, i: (bi, 0, i, 0)))
            shapes.append(jax.ShapeDtypeStruct((b, s[0], t, s[1]), ACT))
        else:
            specs.append(pl.BlockSpec((1, tm, s), lambda bi, i: (bi, i, 0)))
            shapes.append(jax.ShapeDtypeStruct((b, t, s), ACT))
    return pl.pallas_call(
        functools.partial(_proj_body, splits=splits),
        grid=(b, t // tm),
        in_specs=[
            pl.BlockSpec((1, tm, d), lambda bi, i: (bi, i, 0)),
            pl.BlockSpec((1, d), lambda bi, i: (0, 0)),
            vec, vec,
            pl.BlockSpec((d, n), lambda bi, i: (0, 0)),
        ],
        out_specs=specs,
        out_shape=shapes,
        compiler_params=_cparams(("parallel", "parallel")),
        name="proj",
    )(x, g, sc1, sh, w)


HALO = 16


def _proj_shift_body(x_ref, xp_ref, xn_ref, g_ref, sc_ref, sh_ref, w_ref, sw_ref, z_ref, gate_ref,
                     *, tm, n_conv, segs):
    i = pl.program_id(1)
    last = pl.num_programs(1) - 1
    g, sc1, sh = g_ref[...], sc_ref[0], sh_ref[0]
    h = _modnorm(x_ref[0], g, sc1, sh)
    hp = _modnorm(xp_ref[0], g, sc1, sh) * (i > 0).astype(F32)
    hn = _modnorm(xn_ref[0], g, sc1, sh) * (i < last).astype(F32)
    hb = jnp.concatenate([hp, h, hn], axis=0).astype(BF16)
    rows = tm + 2 * HALO
    dst = 0
    for src, width in segs:
        for c0 in range(0, width, COL_CHUNK):
            cw = min(COL_CHUNK, width - c0)
            cols = slice(src + c0, src + c0 + cw)
            u = jnp.dot(hb, w_ref[:, cols], preferred_element_type=F32)
            up = pltpu.roll(u, 1, 0)[HALO:HALO + tm]
            un = pltpu.roll(u, rows - 1, 0)[HALO:HALO + tm]
            um = u[HALO:HALO + tm]
            z_ref[0, :, dst + c0:dst + c0 + cw] = (
                sw_ref[0:1, cols] * up + sw_ref[1:2, cols] * um
                + sw_ref[2:3, cols] * un).astype(z_ref.dtype)
        dst += width
    hc = hb[HALO:HALO + tm]
    n_all = w_ref.shape[1]
    for c0 in range(n_conv, n_all, COL_CHUNK):
        c1 = min(n_all, c0 + COL_CHUNK)
        gate_ref[0, :, c0 - n_conv:c1 - n_conv] = jnp.dot(
            hc, w_ref[:, c0:c1], preferred_element_type=F32).astype(gate_ref.dtype)


def _proj_shift(x, g, sc1, sh, w, sw, n_conv, segs, tm):
    b, t, d = x.shape
    tm = min(tm, t)
    n = w.shape[1]
    hb = tm // HALO
    nhb = t // HALO
    vec = pl.BlockSpec((1, 1, d), lambda bi, i: (bi, 0, 0))
    return pl.pallas_call(
        functools.partial(_proj_shift_body, tm=tm, n_conv=n_conv, segs=segs),
        grid=(b, t // tm),
        in_specs=[
            pl.BlockSpec((1, tm, d), lambda bi, i: (bi, i, 0)),
            pl.BlockSpec((1, HALO, d), lambda bi, i: (bi, jnp.maximum(i * hb - 1, 0), 0)),
            pl.BlockSpec((1, HALO, d), lambda bi, i: (bi, jnp.minimum((i + 1) * hb, nhb - 1), 0)),
            pl.BlockSpec((1, d), lambda bi, i: (0, 0)),
            vec, vec,
            pl.BlockSpec((d, n), lambda bi, i: (0, 0)),
            pl.BlockSpec((3, n_conv), lambda bi, i: (0, 0)),
        ],
        out_specs=[pl.BlockSpec((1, tm, n_conv), lambda bi, i: (bi, i, 0)),
                   pl.BlockSpec((1, tm, n - n_conv), lambda bi, i: (bi, i, 0))],
        out_shape=[jax.ShapeDtypeStruct((b, t, n_conv), ACT),
                   jax.ShapeDtypeStruct((b, t, n - n_conv), ACT)],
        compiler_params=_cparams(("parallel", "parallel")),
        name="proj_shift",
    )(x, x, x, g, sc1, sh, w, sw)


def _rms(x, g):
    return x * lax.rsqrt(jnp.mean(x * x, axis=-1, keepdims=True) + NORM_EPS) * g


def _qkv_body(ual_ref, uac_ref, kvn_ref, qn_ref, wk_ref, wv_ref, wq_ref, wqr_ref, kg_ref, kb_ref,
              qb_ref, cosk_ref, cosq_ref, sa_ref, sb_ref, sinq_ref, q_ref, k_ref, v_ref, *, n_lat):
    ua = jnp.where(pl.program_id(1) < n_lat, ual_ref[0], uac_ref[0]).astype(F32)
    ckv = _rms(ua[:, :KV_LORA], kvn_ref[...]).astype(BF16)
    kr = ua[:, KV_LORA:KV_LORA + LANE]
    cq = _rms(ua[:, KV_LORA + LANE:], qn_ref[...]).astype(BF16)
    kn = jnp.dot(ckv, wk_ref[...], preferred_element_type=F32)
    vv = jnp.dot(ckv, wv_ref[...], preferred_element_type=F32)
    qq = jnp.dot(cq, wq_ref[...], preferred_element_type=F32)
    qr = jnp.dot(cq, wqr_ref[...], preferred_element_type=F32)
    ones_hi = (_iota2((1, HEAD_SLOT), 1) >= V_HEAD).astype(F32)
    pe = pltpu.roll(kr, QK_NOPE, 1)
    gp = pe * kg_ref[...]
    pe_rot = (pltpu.roll(gp, LANE - ROPE_FREQS, 1) * sa_ref[...]
              + pltpu.roll(gp, ROPE_FREQS, 1) * sb_ref[...])
    cosk, cosq, sinq = cosk_ref[...], cosq_ref[...], sinq_ref[...]
    inv_n = 1.0 / QK_HEAD
    scale = lambda x: lax.rsqrt(jnp.sum(x * x, axis=-1, keepdims=True) * inv_n + NORM_EPS)

    for h in range(MLA_HEADS):
        sl = slice(h * HEAD_SLOT, (h + 1) * HEAD_SLOT)
        kh = kn[:, sl] + pe
        k_ref[0, h] = (scale(kh) * (kh * cosk + pe_rot) + kb_ref[...]).astype(BF16)
        qh = qq[:, sl]
        q_ref[0, h] = (scale(qh) * (qh * cosq + qr[:, sl] * sinq) + qb_ref[...]).astype(BF16)
        v_ref[0, h] = (vv[:, sl] + ones_hi).astype(BF16)


def _qkv(ua_l, ua_c, kvn, qn, wk, wv, wq, wqr, kg, kb, qb, tabs, tm):
    b, s, wa = ua_l.shape
    tc = ua_c.shape[1]
    nl, ncx = s // tm, tc // tm
    full = lambda a: pl.BlockSpec(a.shape, lambda bi, i: (0,) * a.ndim)
    tab = pl.BlockSpec((tm, LANE), lambda bi, i: (i, 0))
    head = pl.BlockSpec((1, MLA_HEADS, tm, HEAD_SLOT), lambda bi, i: (bi, 0, i, 0))
    shape = jax.ShapeDtypeStruct((b, MLA_HEADS, s + tc, HEAD_SLOT), BF16)
    return pl.pallas_call(
        functools.partial(_qkv_body, n_lat=nl),
        grid=(b, nl + ncx),
        in_specs=[pl.BlockSpec((1, tm, wa), lambda bi, i: (bi, jnp.minimum(i, nl - 1), 0)),
                  pl.BlockSpec((1, tm, wa), lambda bi, i: (bi, jnp.maximum(i - nl, 0), 0)),
                  full(kvn), full(qn), full(wk), full(wv), full(wq), full(wqr), full(kg),
                  full(kb), full(qb)] + [tab] * len(tabs),
        out_specs=[head, head, head],
        out_shape=[shape, shape, shape],
        compiler_params=_cparams(("parallel", "parallel")),
        name="qkv",
    )(ua_l, ua_c, kvn, qn, wk, wv, wq, wqr, kg, kb, qb, *tabs)


BIAS_LANE = QK_HEAD
MAX_STATIC_BOUND = 50.0


def _attn_finish(acc_ref, o_ref):
    bq = acc_ref.shape[1]
    lane = _iota2((bq, LANE), 1)
    o0 = acc_ref[0] / pltpu.roll(acc_ref[0], V_HEAD, 1)
    o1 = acc_ref[1] / pltpu.roll(acc_ref[1], V_HEAD, 1)
    o_ref[0] = jnp.where(lane < V_HEAD, o0, pltpu.roll(o1, V_HEAD, 1)).astype(o_ref.dtype)


def _attn_static_body(q_ref, k_ref, v_ref, o_ref, acc_ref, *, bk):
    acc_ref[...] = jnp.zeros(acc_ref.shape, F32)

    def step(j, carry):
        rows = pl.ds(pl.multiple_of(j * bk, bk), bk)
        s = [lax.dot_general(q_ref[0, hh], k_ref[0, hh, rows, :], NT, preferred_element_type=F32)
             for hh in range(2)]
        p = [jnp.exp2(x.astype(BF16)) for x in s]
        for hh in range(2):
            acc_ref[hh] += jnp.dot(p[hh], v_ref[0, hh, rows, :], preferred_element_type=F32)
        return carry

    lax.fori_loop(0, k_ref.shape[2] // bk, step, 0)
    _attn_finish(acc_ref, o_ref)


def _attn_online_body(q_ref, k_ref, v_ref, o_ref, acc_ref, m_ref, *, bk):
    acc_ref[...] = jnp.zeros(acc_ref.shape, F32)
    m_ref[...] = jnp.full(m_ref.shape, -jnp.inf, F32)

    def step(j, carry):
        rows = pl.ds(pl.multiple_of(j * bk, bk), bk)
        for hh in range(2):
            s = lax.dot_general(q_ref[0, hh], k_ref[0, hh, rows, :], NT,
                                preferred_element_type=F32)
            m_prev = m_ref[hh]
            m_new = jnp.maximum(m_prev, jnp.max(s, axis=-1, keepdims=True))
            p = jnp.exp2(s - m_new)
            acc_ref[hh] = (jnp.exp2(m_prev - m_new) * acc_ref[hh]
                           + jnp.dot(p.astype(BF16), v_ref[0, hh, rows, :],
                                     preferred_element_type=F32))
            m_ref[hh] = m_new
        return carry

    lax.fori_loop(0, k_ref.shape[2] // bk, step, 0)
    _attn_finish(acc_ref, o_ref)


def _attention(q, k, v, static_ok, q_start, q_rows, k_start, k_rows, bq, bk):
    b, h, _, e = q.shape
    bq, bk = min(bq, q_rows), min(bk, k_rows)
    qi0, kj0 = q_start // bq, k_start // k_rows
    kv_blk = pl.BlockSpec((1, 2, k_rows, e), lambda bi, p, i: (bi, p, kj0, 0))

    def call(online):
        scratch = [pltpu.VMEM((2, bq, LANE), F32)]
        if online:
            scratch.append(pltpu.VMEM((2, bq, 1), F32))
        return pl.pallas_call(
            functools.partial(_attn_online_body if online else _attn_static_body, bk=bk),
            grid=(b, h // 2, q_rows // bq),
            in_specs=[pl.BlockSpec((1, 2, bq, e), lambda bi, p, i: (bi, p, qi0 + i, 0)),
                      kv_blk, kv_blk],
            out_specs=pl.BlockSpec((1, bq, 2 * V_HEAD), lambda bi, p, i: (bi, i, p)),
            out_shape=jax.ShapeDtypeStruct((b, q_rows, h * V_HEAD), ACT),
            scratch_shapes=scratch,
            compiler_params=_cparams(("parallel", "parallel", "arbitrary")),
            name="attention_online" if online else "attention",
        )(q, k, v)

    return lax.cond(static_ok, lambda: call(False), lambda: call(True))


def _dft_mats(n):
    idx = np.arange(n)
    ang = 2.0 * np.pi * ((idx[:, None] * idx[None, :]) % n) / n
    return np.cos(ang), np.sin(ang)


def _hilo(a):
    a = jnp.asarray(a, F32)
    hi = a.astype(BF16)
    return hi, (a - hi.astype(F32)).astype(BF16)


def _mm3c(ah, al, b, dn=NN):
    bh, bl = _split(b)
    d = lambda x, y: lax.dot_general(x, y, dn, preferred_element_type=F32)
    return d(ah, bh) + d(al, bh) + d(ah, bl)


def _four_rows_body(x_ref, w_ref, tc_ref, ts_ref, o_ref):
    r = tc_ref.shape[0]
    y = jnp.dot(w_ref[...], x_ref[0, 0], preferred_element_type=F32)
    yc, ys = y[:r], y[r:]
    tc, ts = tc_ref[...], ts_ref[...]
    o_ref[0, 0, :r] = (yc * tc - ys * ts).astype(o_ref.dtype)
    o_ref[0, 0, r:] = (yc * ts + ys * tc).astype(o_ref.dtype)


def _four_cols_body(y_ref, w_ref, cs_ref, wf_ref, o_ref, y3_scr, *, krt, scale):
    def one(j, carry):
        rows = pl.ds(pl.multiple_of(j * GRID_W, GRID_W), GRID_W)
        ycs = jnp.concatenate([y_ref[0, 0, 0, rows, :], y_ref[0, 0, 1, rows, :]], axis=0)
        y3 = jnp.dot(w_ref[...], ycs, preferred_element_type=F32)
        y3_scr[rows, :] = jnp.concatenate([y3[:GRID_W], y3[GRID_W:]], axis=1).astype(BF16)
        return carry

    lax.fori_loop(0, krt, one, 0, unroll=8)
    f = jnp.dot(y3_scr[...], cs_ref[...], preferred_element_type=F32) * scale
    o_ref[0, 0] = _mm(f, wf_ref[0]).astype(o_ref.dtype)


def _fourier_latent(xf, w_fnet):
    b, g, t, gd = xf.shape
    r = t // GRID_W
    wide = GRID_W * gd
    xv = xf.reshape(b, g, r, wide)
    cr, sr = _dft_mats(r)
    w_rows = jnp.asarray(np.concatenate([cr, sr], axis=0), BF16)
    kr_i, c_i = np.arange(r)[:, None], np.arange(GRID_W)[None, :]
    ang = 2.0 * np.pi * ((kr_i * c_i) % t) / t
    twc = jnp.repeat(jnp.asarray(np.cos(ang), F32), gd, axis=1)
    tws = jnp.repeat(jnp.asarray(np.sin(ang), F32), gd, axis=1)
    tl = min(2048, wide)
    y2 = pl.pallas_call(
        _four_rows_body,
        grid=(b, g, wide // tl),
        in_specs=[pl.BlockSpec((1, 1, r, tl), lambda bi, gi, l: (bi, gi, 0, l)),
                  pl.BlockSpec((2 * r, r), lambda bi, gi, l: (0, 0)),
                  pl.BlockSpec((r, tl), lambda bi, gi, l: (0, l)),
                  pl.BlockSpec((r, tl), lambda bi, gi, l: (0, l))],
        out_specs=pl.BlockSpec((1, 1, 2 * r, tl), lambda bi, gi, l: (bi, gi, 0, l)),
        out_shape=jax.ShapeDtypeStruct((b, g, 2 * r, wide), ACT),
        compiler_params=_cparams(("parallel", "parallel", "parallel")),
        name="fourier_rows",
    )(xv, w_rows, twc, tws)
    y2v = y2.reshape(b, g, 2, r * GRID_W, gd)
    c64, s64 = _dft_mats(GRID_W)
    w_cols = jnp.asarray(np.block([[c64, -s64], [s64, c64]]), BF16)
    cc, sc = _dft_mats(gd)
    w_chan = jnp.asarray(np.concatenate([cc, -sc], axis=0), BF16)
    krt = min(32, r)
    const = lambda a: pl.BlockSpec(a.shape, lambda bi, gi, i: (0, 0))
    fo = pl.pallas_call(
        functools.partial(_four_cols_body, krt=krt, scale=1.0 / math.sqrt(t * gd)),
        grid=(b, g, r // krt),
        in_specs=[pl.BlockSpec((1, 1, 2, krt * GRID_W, gd), lambda bi, gi, i: (bi, gi, 0, i, 0)),
                  const(w_cols), const(w_chan),
                  pl.BlockSpec((1, gd, gd), lambda bi, gi, i: (gi, 0, 0))],
        out_specs=pl.BlockSpec((1, 1, krt * GRID_W, gd), lambda bi, gi, i: (bi, gi, i, 0)),
        out_shape=jax.ShapeDtypeStruct((b, g, r * GRID_W, gd), ACT),
        scratch_shapes=[pltpu.VMEM((krt * GRID_W, 2 * gd), BF16)],
        compiler_params=_cparams(("parallel", "parallel", "parallel")),
        name="fourier_cols",
    )(y2v, w_cols, w_chan, w_fnet)
    return fo.reshape(b, g, r, GRID_W, gd).transpose(0, 3, 2, 1, 4).reshape(b, t, g * gd)


def _four_dense_body(x_ref, ch_ref, cl_ref, th_ref, tl_ref, sh_ref, sl_ref, wf_ref, o_ref, *, scale):
    x = x_ref[0, 0]
    xh, xl = _split(x)
    d = lambda a, b: jnp.dot(a, b, preferred_element_type=F32)
    z = d(xh, ch_ref[...]) + d(xl, ch_ref[...]) + d(xh, cl_ref[...])
    zc, zs = z[:, :FNET_GROUP_DIM], z[:, FNET_GROUP_DIM:]
    f = (_mm3c(th_ref[...], tl_ref[...], zc) - _mm3c(sh_ref[...], sl_ref[...], zs)) * scale
    o_ref[0] = _mm3(f, wf_ref[0]).astype(o_ref.dtype)


def _fourier_dense(xf, w_fnet):
    b, g, t, gd = xf.shape
    cc, sc = _dft_mats(gd)
    ch, cl = _hilo(np.concatenate([cc, sc], axis=1))
    ct, st = _dft_mats(t)
    cth, ctl = _hilo(ct)
    sth, stl = _hilo(st)
    sq = pl.BlockSpec((t, t), lambda bi, gi: (0, 0))
    cs = pl.BlockSpec((gd, 2 * gd), lambda bi, gi: (0, 0))
    return pl.pallas_call(
        functools.partial(_four_dense_body, scale=1.0 / math.sqrt(t * gd)),
        grid=(b, g),
        in_specs=[pl.BlockSpec((1, 1, t, gd), lambda bi, gi: (bi, gi, 0, 0)), cs, cs, sq, sq, sq, sq,
                  pl.BlockSpec((1, gd, gd), lambda bi, gi: (gi, 0, 0))],
        out_specs=pl.BlockSpec((1, t, gd), lambda bi, gi: (bi, 0, gi)),
        out_shape=jax.ShapeDtypeStruct((b, t, g * gd), ACT),
        compiler_params=_cparams(("parallel", "parallel")),
        name="fourier_dense",
    )(xf, ch, cl, cth, ctl, sth, stl, w_fnet)


def _merge_body(o_ref, f_ref, gate_ref, x_ref, gl_ref, w_ref, out_ref):
    gt = gate_ref[0].astype(F32)
    mix = jnp.concatenate([o_ref[0], f_ref[0]], axis=-1).astype(F32) * (gt * _sigmoid(gt))
    y = jnp.dot(mix.astype(BF16), w_ref[...], preferred_element_type=F32)
    out_ref[0] = x_ref[0] + gl_ref[0] * y


def _merge(o, f, gate, x, gl, w, tm):
    b, t, d = x.shape
    tm = min(tm, t)
    half = o.shape[2]
    tok = lambda n: pl.BlockSpec((1, tm, n), lambda bi, i: (bi, i, 0))
    return pl.pallas_call(
        _merge_body,
        grid=(b, t // tm),
        in_specs=[tok(half), tok(half), tok(d), tok(d),
                  pl.BlockSpec((1, 1, d), lambda bi, i: (bi, 0, 0)),
                  pl.BlockSpec(w.shape, lambda bi, i: (0, 0))],
        out_specs=tok(d),
        out_shape=jax.ShapeDtypeStruct((b, t, d), F32),
        compiler_params=_cparams(("parallel", "parallel")),
        name="merge",
    )(o, f, gate, x, gl, w)


EXP_M05 = math.exp(-0.5)


PAIRS_PER_STEP = 8


def _rwkv_local_body(zk_ref, zv_ref, zr_ref, zwa_ref, w0_ref, w2_ref, a0_ref, a2_ref,
                     kk_ref, ka_ref, rk_ref, m_ref, g_ref, qt_ref, yl_ref, bn_ref):
    c = CHUNK
    zwa = zwa_ref[0].astype(F32)
    lora_w, lora_a = jnp.tanh(zwa[:, :LANE]), zwa[:, LANE:]

    head0 = _iota2((1, LANE), 1) < RWKV_HEAD
    r2 = _iota2((LANE, LANE), 0)
    c2 = _iota2((LANE, LANE), 1)
    same = (r2 // RWKV_HEAD) == (c2 // RWKV_HEAD)
    ones_bd = same.astype(F32)
    eye = r2 == c2
    eye_f = eye.astype(F32)
    rc = _iota2((c, c), 0)
    cc = _iota2((c, c), 1)

    def stack(x):
        z = jnp.zeros_like(x)
        return jnp.concatenate([jnp.where(head0, x, z), jnp.where(head0, z, x)], axis=0)

    stack_b = lambda x: stack(x.astype(BF16))
    fold = lambda x: x[:c] + x[c:]

    pairs = range(PAIRS_PER_STEP)
    chains = [(q, d) for q in pairs for d in range(2)]
    qls = [slice(q * LANE, (q + 1) * LANE) for q in pairs]
    ks = [zk_ref[0, :, ql].astype(F32) for ql in qls]
    vs_ = [zv_ref[0, :, ql].astype(F32) for ql in qls]
    rs = [zr_ref[0, :, ql].astype(F32) for ql in qls]
    wraw = [_mm(lora_w, w2_ref[q]) + w0_ref[q] for q in pairs]
    araw = [_mm(lora_a, a2_ref[q]) + a0_ref[q] for q in pairs]
    logw = [-EXP_M05 * _sigmoid(w) for w in wraw]
    a_all = [_sigmoid(a) for a in araw]
    kk0 = [ks[q] * kk_ref[:, qls[q]] for q in pairs]
    ss = [_mm2r(x * x, ones_bd) for x in kk0]
    kk = [kk0[q] / jnp.maximum(jnp.sqrt(ss[q]), 1e-12) for q in pairs]
    vstk = [stack_b(v) for v in vs_]

    dsl = [slice(d * LANE, (d + 1) * LANE) for d in range(2)]
    lw = [logw[q][:, dsl[d]] for q, d in chains]
    ad = [a_all[q][:, dsl[d]] for q, d in chains]
    kd = [ks[q] * (1.0 + (ad[i] - 1.0) * ka_ref[:, qls[q]]) for i, (q, d) in enumerate(chains)]
    bb = [kk[q] * ad[i] for i, (q, d) in enumerate(chains)]
    bonus = [_mm2r(rs[q] * kd[i] * rk_ref[:, qls[q]], ones_bd) * vs_[q]
             for i, (q, d) in enumerate(chains)]
    r3 = _iota2((c, 3 * c), 0)
    c3 = _iota2((c, 3 * c), 1) & (c - 1)
    tri3 = [(c3 <= r3).astype(BF16), (c3 >= r3).astype(BF16)]
    strict = [same & (c2 < r2), same & (c2 > r2)]
    incl = [same & (c2 <= r2), same & (c2 >= r2)]

    def cumsum(x, tri):
        xh, xl = _split(x)
        xll = (x - xh.astype(F32) - xl.astype(F32)).astype(BF16)
        return jnp.dot(tri, jnp.concatenate([xh, xl, xll], axis=0), preferred_element_type=F32)

    lc = [cumsum(lw[i], tri3[d]) for i, (q, d) in enumerate(chains)]
    ltot = [lc[i][c - 1:c] if d == 0 else lc[i][0:1] for i, (q, d) in enumerate(chains)]
    kkd_s = [stack_b(kk[q] * jnp.exp(lc[i] - lw[i])) for i, (q, d) in enumerate(chains)]
    rd_s = [stack(rs[q] * jnp.exp(lc[i])) for i, (q, d) in enumerate(chains)]
    e_inv = [jnp.exp(-x) for x in lc]
    inv_s = [jnp.concatenate([stack_b(bb[i] * e_inv[i]), stack_b(kd[i] * e_inv[i])], axis=0)
             for i in range(len(chains))]
    amat = [lax.dot_general(jnp.concatenate([kkd_s[i], rd_s[i].astype(BF16)], axis=0), inv_s[i],
                            NT, preferred_element_type=F32).astype(BF16)
            for i in range(len(chains))]
    zero_b = jnp.zeros((LANE, LANE), BF16)
    a_kb_b = [jnp.where(strict[d], amat[i][:LANE, :LANE], zero_b)
              for i, (q, d) in enumerate(chains)]
    a_kb = [x.astype(F32) for x in a_kb_b]
    a_kk = [jnp.where(strict[d], amat[i][:LANE, LANE:], zero_b)
            for i, (q, d) in enumerate(chains)]
    aq_b = [jnp.where(incl[d], amat[i][LANE:, :LANE], zero_b) for i, (q, d) in enumerate(chains)]
    aq_k = [jnp.where(incl[d], amat[i][LANE:, LANE:], zero_b) for i, (q, d) in enumerate(chains)]
    av = [jnp.dot(jnp.concatenate([a_kk[i], aq_k[i]], axis=0), vstk[q],
                  preferred_element_type=F32) for i, (q, d) in enumerate(chains)]
    n = len(chains)
    bdot = lambda a, b: jnp.dot(a, b, preferred_element_type=F32)
    tinv = [eye_f - a for a in a_kb]
    qpow = [bdot(a, a).astype(BF16) for a in a_kb_b]
    for _ in range(4):
        prod = [bdot(qpow[i], jnp.concatenate([qpow[i], tinv[i].astype(BF16)], axis=1))
                for i in range(n)]
        qpow = [x[:, :LANE].astype(BF16) for x in prod]
        tinv = [tinv[i] + prod[i][:, LANE:] for i in range(n)]
    tinv = [tinv[i] + bdot(qpow[i], tinv[i].astype(BF16)) for i in range(n)]
    resid = [eye_f - _mm2l(eye_f + a_kb[i], tinv[i]) for i in range(n)]
    tinv = [tinv[i] + _mm(tinv[i], resid[i]) for i in range(n)]
    x = [bdot(tinv[i].astype(BF16),
              jnp.concatenate([kkd_s[i], av[i][:LANE].astype(BF16)], axis=1)) for i in range(n)]
    xb = [v.astype(BF16) for v in x]
    qy = [jnp.concatenate([rd_s[i], av[i][LANE:]], axis=1) - bdot(aq_b[i], xb[i])
          for i in range(n)]
    e_end = [jnp.exp(ltot[i] - lc[i]) for i in range(n)]
    ends = [jnp.concatenate([stack_b(-bb[i] * e_end[i]), stack_b(kd[i] * e_end[i])], axis=0)
            for i in range(n)]
    wuv = [jnp.concatenate([xb[i], jnp.concatenate([zero_b, vstk[q]], axis=1)], axis=0)
           for i, (q, d) in enumerate(chains)]
    mg = [lax.dot_general(ends[i], wuv[i], TN, preferred_element_type=F32) for i in range(n)]
    for i, (q, d) in enumerate(chains):
        m_ref[0, 0, d, q] = fold(jnp.where(eye, jnp.exp(ltot[i]), 0.0)
                                 + mg[i][:, :LANE]).astype(BF16)
        g_ref[0, 0, d, q] = fold(mg[i][:, LANE:]).astype(g_ref.dtype)
        qt_ref[0, d, :, qls[q]] = fold(qy[i][:, :LANE]).astype(BF16)
    for q in pairs:
        yl_ref[0, :, qls[q]] = (fold(qy[2 * q][:, LANE:])
                                + fold(qy[2 * q + 1][:, LANE:])).astype(yl_ref.dtype)
        bn_ref[0, :, qls[q]] = (bonus[2 * q] + bonus[2 * q + 1]).astype(bn_ref.dtype)


def _rwkv_local(z, w0p, w2p, a0p, a2p, k_k, k_a, r_k):
    b, t, _ = z.shape
    w = k_k.shape[1]
    npair = w // LANE
    pp = PAIRS_PER_STEP
    ng = npair // pp
    wl = pp * LANE
    nc = t // CHUNK
    tokc = lambda base: pl.BlockSpec((1, CHUNK, wl), lambda bi, ci, p: (bi, ci, base + p))
    perp3 = lambda n: pl.BlockSpec((pp, n, 2 * LANE), lambda bi, ci, p: (p, 0, 0))
    vecp = pl.BlockSpec((1, wl), lambda bi, ci, p: (0, p))
    mat = pl.BlockSpec((1, 1, 2, pp, CHUNK, LANE), lambda bi, ci, p: (bi, ci, 0, p, 0, 0))
    return pl.pallas_call(
        _rwkv_local_body,
        grid=(b, nc, ng),
        in_specs=[tokc(0), tokc(ng), tokc(2 * ng),
                  pl.BlockSpec((1, CHUNK, 2 * LANE), lambda bi, ci, p: (bi, ci, 3 * npair // 2)),
                  perp3(1), perp3(LANE), perp3(1), perp3(LANE), vecp, vecp, vecp],
        out_specs=[mat, mat,
                   pl.BlockSpec((1, 2, CHUNK, wl), lambda bi, ci, p: (bi, 0, ci, p)),
                   pl.BlockSpec((1, CHUNK, wl), lambda bi, ci, p: (bi, ci, p)),
                   pl.BlockSpec((1, CHUNK, wl), lambda bi, ci, p: (bi, ci, p))],
        out_shape=[jax.ShapeDtypeStruct((b, nc, 2, npair, CHUNK, LANE), BF16),
                   jax.ShapeDtypeStruct((b, nc, 2, npair, CHUNK, LANE), ACT),
                   jax.ShapeDtypeStruct((b, 2, t, w), BF16),
                   jax.ShapeDtypeStruct((b, t, w), ACT),
                   jax.ShapeDtypeStruct((b, t, w), ACT)],
        compiler_params=_cparams(("parallel", "parallel", "parallel")),
        name="rwkv_local",
    )(z, z, z, z, w0p, w2p, a0p, a2p, k_k, k_a, r_k)


SCAN_CHUNKS = 8


def _rwkv_scan_body(m0_ref, g0_ref, q0_ref, m1_ref, g1_ref, q1_ref, h0_ref,
                    y0_ref, y1_ref, hfin_ref, h_scr, *, npair, cs):
    ci = pl.program_id(1)

    @pl.when(ci == 0)
    def _():
        h_scr[...] = h0_ref[0]

    head0 = _iota2((1, LANE), 1) < RWKV_HEAD

    def expand(x):
        z = jnp.zeros_like(x)
        return jnp.concatenate([jnp.where(head0, x, z), jnp.where(head0, z, x)], axis=0)

    refs = ((m0_ref, g0_ref, q0_ref, y0_ref), (m1_ref, g1_ref, q1_ref, y1_ref))
    chains = [(d, p) for d in range(2) for p in range(npair)]
    lanes = [slice(p * LANE, (p + 1) * LANE) for p in range(npair)]
    h = [h_scr[d, p] for d, p in chains]
    for step in range(cs):
        ck = (step, cs - 1 - step)
        rows = [slice(c * CHUNK, (c + 1) * CHUNK) for c in ck]
        hb = [x.astype(BF16) for x in h]
        ys = [jnp.dot(refs[d][2][0, 0, rows[d], lanes[p]], hb[i], preferred_element_type=F32)
              for i, (d, p) in enumerate(chains)]
        for i, (d, p) in enumerate(chains):
            refs[d][3][0, rows[d], lanes[p]] = ys[i].astype(refs[d][3].dtype)
        h = [jnp.dot(expand(refs[d][0][0, ck[d], 0, p]), hb[i], preferred_element_type=F32)
             + expand(refs[d][1][0, ck[d], 0, p]) for i, (d, p) in enumerate(chains)]
    for i, (d, p) in enumerate(chains):
        h_scr[d, p] = h[i]

    @pl.when(ci == pl.num_programs(1) - 1)
    def _():
        hfin_ref[0] = h_scr[...]


def _rwkv_scan(mm, gg, qt, h0):
    b, nc, _, npair, _, _ = mm.shape
    t, w = qt.shape[2], qt.shape[3]
    cs = SCAN_CHUNKS if nc % SCAN_CHUNKS == 0 else 1
    nb = nc // cs
    fwd = lambda bi, ci: (bi, ci, 0, 0, 0, 0)
    rev = lambda bi, ci: (bi, nb - 1 - ci, 1, 0, 0, 0)
    mblk = (1, cs, 1, npair, CHUNK, LANE)
    hspec = pl.BlockSpec((1, 2, npair, LANE, LANE), lambda bi, ci: (bi, 0, 0, 0, 0))
    return pl.pallas_call(
        functools.partial(_rwkv_scan_body, npair=npair, cs=cs),
        grid=(b, nb),
        in_specs=[pl.BlockSpec(mblk, fwd), pl.BlockSpec(mblk, fwd),
                  pl.BlockSpec((1, 1, cs * CHUNK, w), lambda bi, ci: (bi, 0, ci, 0)),
                  pl.BlockSpec(mblk, rev), pl.BlockSpec(mblk, rev),
                  pl.BlockSpec((1, 1, cs * CHUNK, w), lambda bi, ci: (bi, 1, nb - 1 - ci, 0)),
                  hspec],
        out_specs=[pl.BlockSpec((1, cs * CHUNK, w), lambda bi, ci: (bi, ci, 0)),
                   pl.BlockSpec((1, cs * CHUNK, w), lambda bi, ci: (bi, nb - 1 - ci, 0)),
                   hspec],
        out_shape=[jax.ShapeDtypeStruct((b, t, w), ACT), jax.ShapeDtypeStruct((b, t, w), ACT),
                   jax.ShapeDtypeStruct(h0.shape, F32)],
        scratch_shapes=[pltpu.VMEM((2, npair, LANE, LANE), F32)],
        compiler_params=_cparams(("parallel", "arbitrary")),
        name="rwkv_scan",
    )(mm, gg, qt, mm, gg, qt, h0)


def _rwkv_out_body(y0_ref, y1_ref, yl_ref, bn_ref, gate_ref, x_ref, gl_ref, gnw_ref, gnb_ref,
                   w_ref, o_ref):
    y = y0_ref[0].astype(F32) + y1_ref[0].astype(F32) + yl_ref[0].astype(F32)
    r2 = _iota2((LANE, LANE), 0)
    c2 = _iota2((LANE, LANE), 1)
    avg = ((r2 // RWKV_HEAD) == (c2 // RWKV_HEAD)).astype(F32) * (1.0 / RWKV_HEAD)
    parts = []
    for p in range(y.shape[1] // LANE):
        yp = y[:, p * LANE:(p + 1) * LANE]
        dl = yp - _mm2r(yp, avg)
        var = _mm2r(dl * dl, avg)
        parts.append(dl * lax.rsqrt(var + GN_EPS))
    yn = jnp.concatenate(parts, axis=1)
    gt = gate_ref[0].astype(F32)
    act = (yn * gnw_ref[...] + gnb_ref[...] + bn_ref[0].astype(F32)) * (gt * _sigmoid(gt))
    out = jnp.dot(act.astype(BF16), w_ref[...], preferred_element_type=F32)
    o_ref[0] = x_ref[0] + gl_ref[0] * out


def _rwkv_out(y0, y1, yl, bn, gate, x, gl, gnw, gnb, w, tm):
    b, t, d = x.shape
    tm = min(tm, t)
    wd = y0.shape[2]
    tok = lambda n: pl.BlockSpec((1, tm, n), lambda bi, i: (bi, i, 0))
    return pl.pallas_call(
        _rwkv_out_body,
        grid=(b, t // tm),
        in_specs=[tok(wd), tok(wd), tok(wd), tok(wd), tok(wd), tok(d),
                  pl.BlockSpec((1, 1, d), lambda bi, i: (bi, 0, 0)),
                  pl.BlockSpec((1, wd), lambda bi, i: (0, 0)),
                  pl.BlockSpec((1, wd), lambda bi, i: (0, 0)),
                  pl.BlockSpec(w.shape, lambda bi, i: (0, 0))],
        out_specs=tok(d),
        out_shape=jax.ShapeDtypeStruct((b, t, d), F32),
        compiler_params=_cparams(("parallel", "parallel")),
        name="rwkv_out",
    )(y0, y1, yl, bn, gate, x, gl, gnw, gnb, w)


def _rope_tables(t):
    rows = t // GRID_W
    row = jnp.repeat(jnp.arange(rows, dtype=F32), GRID_W)
    col = jnp.tile(jnp.arange(GRID_W, dtype=F32), rows)
    inv = 1.0 / (ROPE_BASE ** (jnp.arange(ROPE_FREQS, dtype=F32) / ROPE_FREQS))
    ang = jnp.stack([row[:, None] * inv, col[:, None] * inv], axis=1)
    cos, sin = jnp.cos(ang), jnp.sin(ang)
    zeros = jnp.zeros_like(sin)
    ones_lo = jnp.ones((t, QK_NOPE), F32)
    pad_hi = HEAD_SLOT - QK_HEAD
    cos_t = jnp.concatenate([ones_lo, jnp.concatenate([cos, cos], axis=2).reshape(t, QK_ROPE),
                             jnp.ones((t, pad_hi), F32)], axis=1)
    sa = jnp.concatenate([jnp.zeros((t, QK_NOPE), F32),
                          jnp.concatenate([-sin, zeros], axis=2).reshape(t, QK_ROPE),
                          jnp.zeros((t, pad_hi), F32)], axis=1)
    sb = jnp.concatenate([jnp.zeros((t, QK_NOPE), F32),
                          jnp.concatenate([zeros, sin], axis=2).reshape(t, QK_ROPE),
                          jnp.zeros((t, pad_hi), F32)], axis=1)
    return cos_t, sa, sb


def _even_layer(x, ctx, mod_l, mod_c, need_ctx, g, w_in, kv_norm, q_norm, w_uq, w_ukv,
                q_head_norm, k_head_norm, w_fnet, w_out):
    b, s, d = x.shape
    tc = ctx.shape[1]
    e_q0 = KV_LORA + QK_ROPE
    e_f0 = e_q0 + Q_LORA
    e_g0 = e_f0 + FNET_GROUPS * FNET_GROUP_DIM
    w_p = jnp.concatenate([w_in[:, e_g0:], w_in[:, e_f0:e_g0], w_in[:, :e_q0],
                           jnp.zeros((d, LANE - QK_ROPE), F32), w_in[:, e_q0:e_f0]],
                          axis=1).astype(BF16)
    splits = (d, (FNET_GROUPS, FNET_GROUP_DIM), KV_LORA + LANE + Q_LORA)
    kvw = w_ukv.reshape(KV_LORA, MLA_HEADS, QK_NOPE + V_HEAD)
    wk = jnp.pad(kvw[:, :, :QK_NOPE], ((0, 0), (0, 0), (0, HEAD_SLOT - QK_NOPE)))
    wk = wk.reshape(KV_LORA, MLA_HEADS * HEAD_SLOT).astype(BF16)
    wv = jnp.pad(kvw[:, :, QK_NOPE:], ((0, 0), (0, 0), (0, HEAD_SLOT - V_HEAD)))
    wv = wv.reshape(KV_LORA, MLA_HEADS * HEAD_SLOT).astype(BF16)
    wq3 = jnp.pad(w_uq.reshape(Q_LORA, MLA_HEADS, QK_HEAD), ((0, 0), (0, 0), (0, HEAD_SLOT - QK_HEAD)))
    wq = wq3.reshape(Q_LORA, MLA_HEADS * HEAD_SLOT).astype(BF16)
    kg = jnp.pad(k_head_norm, (0, HEAD_SLOT - QK_HEAD)).reshape(1, HEAD_SLOT)
    qg = (jnp.pad(q_head_norm, (0, HEAD_SLOT - QK_HEAD))
          * (QK_HEAD ** -0.5 * math.log2(math.e))).reshape(1, HEAD_SLOT)
    lane = np.arange(HEAD_SLOT)
    tail = (lane >= QK_NOPE) & (lane < QK_HEAD)
    first = tail & (((lane - QK_NOPE) // ROPE_FREQS) % 2 == 0)
    partner = np.where(first, lane + ROPE_FREQS, np.where(tail, lane - ROPE_FREQS, lane))
    sign = np.where(first, -1.0, np.where(tail, 1.0, 0.0)).astype(np.float32)
    wqr = (wq3[:, :, partner] * (sign * qg[0, partner])).reshape(Q_LORA, MLA_HEADS * HEAD_SLOT)
    wqr = wqr.astype(BF16)
    kvn, qn = kv_norm.reshape(1, -1), q_norm.reshape(1, -1)
    g2 = g.reshape(1, d)
    bound = (1.02 * QK_HEAD * jnp.max(jnp.abs(qg)) * jnp.max(jnp.abs(kg))).astype(BF16).astype(F32)
    static_ok = bound <= MAX_STATIC_BOUND
    bias_lane = (jnp.arange(HEAD_SLOT) == BIAS_LANE).astype(F32).reshape(1, HEAD_SLOT)
    kb = bias_lane * jnp.where(static_ok, -bound, 0.0)
    qb = bias_lane

    gate_l, four_l, ua_l = _proj(x, g2, mod_l[1], mod_l[0], w_p, splits, 512)
    gate_c, four_c, ua_c = _proj(ctx, g2, mod_c[1], mod_c[0], w_p, splits, 512)
    sk = s + tc
    cos_t, sa, sb = _rope_tables(s)
    cos_t = jnp.concatenate([cos_t, jnp.ones((tc, HEAD_SLOT), F32)], axis=0)
    sa = jnp.concatenate([sa, jnp.zeros((tc, HEAD_SLOT), F32)], axis=0)
    sb = jnp.concatenate([sb, jnp.zeros((tc, HEAD_SLOT), F32)], axis=0)
    tabs = (cos_t * kg, cos_t * qg, sa, sb, sb - sa)
    q_all, k_all, v_all = _qkv(ua_l, ua_c, kvn, qn, wk, wv, wq, wqr, kg, kb, qb, tabs,
                               math.gcd(s, tc))
    bk = 768 if sk % 768 == 0 else tc
    o_l = _attention(q_all, k_all, v_all, static_ok, 0, s, 0, sk, 2048, bk)
    f_l = _fourier_latent(four_l, w_fnet)
    wo = w_out.astype(BF16)
    x_new = _merge(o_l, f_l, gate_l, x, mod_l[2], wo, 512)
    ctx_new = ctx
    if need_ctx:
        o_c = _attention(q_all, k_all, v_all, static_ok, s, tc, s, tc, tc, tc)
        f_c = _fourier_dense(four_c, w_fnet)
        ctx_new = _merge(o_c, f_c, gate_c, ctx, mod_c[2], wo, 512)
    return x_new, ctx_new


def _odd_layer(x, ctx, mod_l, mod_c, need_ctx, g, w_in, shift_w, w0, w2, a0, a2, k_k, k_a, r_k,
               gn_w, gn_b, w_out):
    b, s, d = x.shape
    w = k_k.shape[0]
    npair = w // LANE
    o_wd0 = 2 * w
    o_r0 = o_wd0 + 2 * DECAY_LORA + 2 * AAA_LORA
    conv_ch = o_r0 + w
    segs = ((0, o_wd0), (o_r0, w), (o_wd0, o_r0 - o_wd0))
    w_p = w_in.astype(BF16)
    sw = shift_w
    g2 = g.reshape(1, d)

    def pairs(vec2):
        return vec2.reshape(2, npair, LANE).transpose(1, 0, 2).reshape(npair, 1, 2 * LANE)

    def pair_mats(m):
        rr = m.shape[1]
        mp = m.reshape(2, rr, npair, LANE).transpose(2, 0, 1, 3)
        z = jnp.zeros_like(mp[:, 0])
        top = jnp.concatenate([mp[:, 0], z], axis=2)
        bot = jnp.concatenate([z, mp[:, 1]], axis=2)
        return jnp.concatenate([top, bot], axis=1).astype(BF16)

    w0p, a0p, w2p, a2p = pairs(w0), pairs(a0), pair_mats(w2), pair_mats(a2)
    kk2, ka2, rk2 = k_k.reshape(1, w), k_a.reshape(1, w), r_k.reshape(1, w)
    wo = w_out.astype(BF16)

    def mix(xin, mod, h0):
        z, gate = _proj_shift(xin, g2, mod[1], mod[0], w_p, sw, conv_ch, segs, 512)
        mm, gg, qt, yl, bn = _rwkv_local(z, w0p, w2p, a0p, a2p, kk2, ka2, rk2)
        y0, y1, hfin = _rwkv_scan(mm, gg, qt, h0)
        return (y0, y1, yl, bn, gate), hfin

    h_zero = jnp.zeros((b, 2, npair, LANE, LANE), F32)
    parts_c, h_ctx = mix(ctx, mod_c, h_zero)
    parts_l, _ = mix(x, mod_l, h_ctx)
    x_new = _rwkv_out(*parts_l, x, mod_l[2], gn_w.reshape(1, w), gn_b.reshape(1, w), wo, 512)
    ctx_new = ctx
    if need_ctx:
        ctx_new = _rwkv_out(*parts_c, ctx, mod_c[2], gn_w.reshape(1, w), gn_b.reshape(1, w), wo, 256)
    return x_new, ctx_new


def kernel(x, c, ctx, c_ctx, ada_w, ada_b, norm_g, e_w_in, e_kv_norm, e_q_norm, e_w_uq, e_w_ukv,
           e_q_head_norm, e_k_head_norm, e_w_fnet, e_w_out, o_w_in, o_shift_w, o_w0, o_w2, o_a0,
           o_a2, o_k_k, o_k_a, o_r_k, o_gn_w, o_gn_b, o_w_out):
    b, s, d = x.shape
    depth = ada_w.shape[0]
    assert b + 1 <= 8
    cond8 = jnp.concatenate([c, c_ctx[None, :], jnp.zeros((8 - b - 1, d), F32)], axis=0)
    mod = _ada(cond8, ada_w, ada_b)
    for layer in range(depth):
        need_ctx = layer < depth - 1
        m = mod[layer]
        chunk = lambda rows, i: rows[:, None, i * d:(i + 1) * d]
        lat, cx = m[:b], jnp.broadcast_to(m[b:b + 1], (b, 3 * d))
        mod_l = (chunk(lat, 0), 1.0 + chunk(lat, 1), chunk(lat, 2))
        mod_c = (chunk(cx, 0), 1.0 + chunk(cx, 1), chunk(cx, 2))
        j = layer // 2
        if layer % 2 == 0:
            x, ctx = _even_layer(x, ctx, mod_l, mod_c, need_ctx, norm_g[layer], e_w_in[j],
                                 e_kv_norm[j], e_q_norm[j], e_w_uq[j], e_w_ukv[j],
                                 e_q_head_norm[j], e_k_head_norm[j], e_w_fnet[j], e_w_out[j])
        else:
            x, ctx = _odd_layer(x, ctx, mod_l, mod_c, need_ctx, norm_g[layer], o_w_in[j],
                                o_shift_w[j], o_w0[j], o_w2[j], o_a0[j], o_a2[j], o_k_k[j],
                                o_k_a[j], o_r_k[j].reshape(-1), o_gn_w[j], o_gn_b[j], o_w_out[j])
    return x
```

```python
import functools
import math

import numpy as np
import jax
import jax.numpy as jnp
from jax import lax
from jax.experimental import pallas as pl
from jax.experimental.pallas import tpu as pltpu

F32 = jnp.float32
BF16 = jnp.bfloat16
ACT = BF16

GRID_W = 64
NORM_EPS = 1e-6
MLA_HEADS = 8
QK_NOPE = 64
QK_ROPE = 32
QK_HEAD = QK_NOPE + QK_ROPE
V_HEAD = 64
Q_LORA = 384
KV_LORA = 256
ROPE_FREQS = QK_ROPE // 4
ROPE_BASE = 10000.0
FNET_GROUPS = 4
FNET_GROUP_DIM = 128
RWKV_HEAD = 64
DECAY_LORA = 64
AAA_LORA = 64
GN_EPS = 64e-5

LANE = 128
CHUNK = 64
HEAD_SLOT = 128
VMEM_LIMIT = 56 * 1024 * 1024

TOKEN_TILE = 512
ATTN_Q_BLOCK = 2048
ATTN_K_BLOCK = 768
FOUR_LANE_TILE = 2048
FOUR_ROWS_TILE = 32

NN = (((1,), (0,)), ((), ()))
NT = (((1,), (1,)), ((), ()))
TN = (((0,), (0,)), ((), ()))


def _cparams(sem):
    return pltpu.CompilerParams(dimension_semantics=sem, vmem_limit_bytes=VMEM_LIMIT)


def _mm(a, b, dn=NN):
    return lax.dot_general(a.astype(BF16), b.astype(BF16), dn, preferred_element_type=F32)


def _split(a):
    hi = a.astype(BF16)
    lo = (a - hi.astype(F32)).astype(BF16)
    return hi, lo


def _mm3(a, b, dn=NN):
    ah, al = _split(a)
    bh, bl = _split(b)
    d = lambda x, y: lax.dot_general(x, y, dn, preferred_element_type=F32)
    return d(ah, bh) + d(al, bh) + d(ah, bl)


def _mm2r(a, b_exact):
    ah, al = _split(a)
    bb = b_exact.astype(BF16)
    return jnp.dot(jnp.concatenate([ah, al], axis=1), jnp.concatenate([bb, bb], axis=0),
                   preferred_element_type=F32)


def _mm2l(a_exact, b):
    bh, bl = _split(b)
    n = b.shape[1]
    y = jnp.dot(a_exact.astype(BF16), jnp.concatenate([bh, bl], axis=1),
                preferred_element_type=F32)
    return y[:, :n] + y[:, n:]


def _sigmoid(x):
    return 1.0 / (1.0 + jnp.exp(-x))


def _modnorm(x, g, sc1, sh):
    y = x * lax.rsqrt(jnp.mean(x * x, axis=-1, keepdims=True) + NORM_EPS)
    return (y * g) * sc1 + sh


def _iota2(shape, dim):
    return lax.broadcasted_iota(jnp.int32, shape, dim)


def _ada_body(c_ref, w_ref, b_ref, o_ref):
    c = c_ref[...]
    s = c * _sigmoid(c)
    o_ref[0] = _mm3(s, w_ref[0]) + b_ref[0]


def _ada(cond8, ada_w, ada_b):
    depth, d, n = ada_w.shape
    tn = 512
    return pl.pallas_call(
        _ada_body,
        grid=(depth, n // tn),
        in_specs=[
            pl.BlockSpec((8, d), lambda l, j: (0, 0)),
            pl.BlockSpec((1, d, tn), lambda l, j: (l, 0, j)),
            pl.BlockSpec((1, 1, tn), lambda l, j: (l, 0, j)),
        ],
        out_specs=pl.BlockSpec((1, 8, tn), lambda l, j: (l, 0, j)),
        out_shape=jax.ShapeDtypeStruct((depth, 8, n), F32),
        compiler_params=_cparams(("parallel", "parallel")),
        name="ada",
    )(cond8, ada_w, ada_b.reshape(depth, 1, n))


COL_CHUNK = 512


def _proj_body(x_ref, g_ref, sc_ref, sh_ref, w_ref, *o_refs, splits):
    h = _modnorm(x_ref[0], g_ref[...], sc_ref[0], sh_ref[0]).astype(BF16)
    mm = lambda c0, c1: jnp.dot(h, w_ref[:, c0:c1], preferred_element_type=F32)
    off = 0
    for o_ref, n in zip(o_refs, splits):
        if isinstance(n, tuple):
            groups, width = n
            y = mm(off, off + groups * width).astype(o_ref.dtype)
            for gi in range(groups):
                o_ref[0, gi] = y[:, gi * width:(gi + 1) * width]
            off += groups * width
            continue
        for c0 in range(0, n, COL_CHUNK):
            c1 = min(n, c0 + COL_CHUNK)
            o_ref[0, :, c0:c1] = mm(off + c0, off + c1).astype(o_ref.dtype)
        off += n


def _proj(x, g, sc1, sh, w, splits, tm):
    b, t, d = x.shape
    tm = min(tm, t)
    n = w.shape[1]
    vec = pl.BlockSpec((1, 1, d), lambda bi, i: (bi, 0, 0))
    specs, shapes = [], []
    for s in splits:
        if isinstance(s, tuple):
            specs.append(pl.BlockSpec((1, s[0], tm, s[1]), lambda bi, i: (bi, 0, i, 0)))
            shapes.append(jax.ShapeDtypeStruct((b, s[0], t, s[1]), ACT))
        else:
            specs.append(pl.BlockSpec((1, tm, s), lambda bi, i: (bi, i, 0)))
            shapes.append(jax.ShapeDtypeStruct((b, t, s), ACT))
    return pl.pallas_call(
        functools.partial(_proj_body, splits=splits),
        grid=(b, t // tm),
        in_specs=[
            pl.BlockSpec((1, tm, d), lambda bi, i: (bi, i, 0)),
            pl.BlockSpec((1, d), lambda bi, i: (0, 0)),
            vec, vec,
            pl.BlockSpec((d, n), lambda bi, i: (0, 0)),
        ],
        out_specs=specs,
        out_shape=shapes,
        compiler_params=_cparams(("parallel", "parallel")),
        name="proj",
    )(x, g, sc1, sh, w)


HALO = 16


def _proj_shift_body(x_ref, xp_ref, xn_ref, g_ref, sc_ref, sh_ref, w_ref, sw_ref, z_ref, gate_ref,
                     *, tm, n_conv, segs):
    i = pl.program_id(1)
    last = pl.num_programs(1) - 1
    g, sc1, sh = g_ref[...], sc_ref[0], sh_ref[0]
    h = _modnorm(x_ref[0], g, sc1, sh)
    hp = _modnorm(xp_ref[0], g, sc1, sh) * (i > 0).astype(F32)
    hn = _modnorm(xn_ref[0], g, sc1, sh) * (i < last).astype(F32)
    hb = jnp.concatenate([hp, h, hn], axis=0).astype(BF16)
    rows = tm + 2 * HALO
    dst = 0
    for src, width in segs:
        for c0 in range(0, width, COL_CHUNK):
            cw = min(COL_CHUNK, width - c0)
            cols = slice(src + c0, src + c0 + cw)
            u = jnp.dot(hb, w_ref[:, cols], preferred_element_type=F32)
            up = pltpu.roll(u, 1, 0)[HALO:HALO + tm]
            un = pltpu.roll(u, rows - 1, 0)[HALO:HALO + tm]
            um = u[HALO:HALO + tm]
            z_ref[0, :, dst + c0:dst + c0 + cw] = (
                sw_ref[0:1, cols] * up + sw_ref[1:2, cols] * um
                + sw_ref[2:3, cols] * un).astype(z_ref.dtype)
        dst += width
    hc = hb[HALO:HALO + tm]
    n_all = w_ref.shape[1]
    for c0 in range(n_conv, n_all, COL_CHUNK):
        c1 = min(n_all, c0 + COL_CHUNK)
        gate_ref[0, :, c0 - n_conv:c1 - n_conv] = jnp.dot(
            hc, w_ref[:, c0:c1], preferred_element_type=F32).astype(gate_ref.dtype)


def _proj_shift(x, g, sc1, sh, w, sw, n_conv, segs, tm):
    b, t, d = x.shape
    tm = min(tm, t)
    n = w.shape[1]
    hb = tm // HALO
    nhb = t // HALO
    vec = pl.BlockSpec((1, 1, d), lambda bi, i: (bi, 0, 0))
    return pl.pallas_call(
        functools.partial(_proj_shift_body, tm=tm, n_conv=n_conv, segs=segs),
        grid=(b, t // tm),
        in_specs=[
            pl.BlockSpec((1, tm, d), lambda bi, i: (bi, i, 0)),
            pl.BlockSpec((1, HALO, d), lambda bi, i: (bi, jnp.maximum(i * hb - 1, 0), 0)),
            pl.BlockSpec((1, HALO, d), lambda bi, i: (bi, jnp.minimum((i + 1) * hb, nhb - 1), 0)),
            pl.BlockSpec((1, d), lambda bi, i: (0, 0)),
            vec, vec,
            pl.BlockSpec((d, n), lambda bi, i: (0, 0)),
            pl.BlockSpec((3, n_conv), lambda bi, i: (0, 0)),
        ],
        out_specs=[pl.BlockSpec((1, tm, n_conv), lambda bi, i: (bi, i, 0)),
                   pl.BlockSpec((1, tm, n - n_conv), lambda bi, i: (bi, i, 0))],
        out_shape=[jax.ShapeDtypeStruct((b, t, n_conv), ACT),
                   jax.ShapeDtypeStruct((b, t, n - n_conv), ACT)],
        compiler_params=_cparams(("parallel", "parallel")),
        name="proj_shift",
    )(x, x, x, g, sc1, sh, w, sw)


def _rms(x, g):
    return x * lax.rsqrt(jnp.mean(x * x, axis=-1, keepdims=True) + NORM_EPS) * g


def _qkv_body(ual_ref, uac_ref, kvn_ref, qn_ref, wk_ref, wv_ref, wq_ref, wqr_ref, kg_ref, kb_ref,
              qb_ref, cosk_ref, cosq_ref, sa_ref, sb_ref, sinq_ref, q_ref, k_ref, v_ref, *, n_lat):
    ua = jnp.where(pl.program_id(1) < n_lat, ual_ref[0], uac_ref[0]).astype(F32)
    ckv = _rms(ua[:, :KV_LORA], kvn_ref[...]).astype(BF16)
    kr = ua[:, KV_LORA:KV_LORA + LANE]
    cq = _rms(ua[:, KV_LORA + LANE:], qn_ref[...]).astype(BF16)
    kn = jnp.dot(ckv, wk_ref[...], preferred_element_type=F32)
    vv = jnp.dot(ckv, wv_ref[...], preferred_element_type=F32)
    qq = jnp.dot(cq, wq_ref[...], preferred_element_type=F32)
    qr = jnp.dot(cq, wqr_ref[...], preferred_element_type=F32)
    ones_hi = (_iota2((1, HEAD_SLOT), 1) >= V_HEAD).astype(F32)
    pe = pltpu.roll(kr, QK_NOPE, 1)
    gp = pe * kg_ref[...]
    pe_rot = (pltpu.roll(gp, LANE - ROPE_FREQS, 1) * sa_ref[...]
              + pltpu.roll(gp, ROPE_FREQS, 1) * sb_ref[...])
    cosk, cosq, sinq = cosk_ref[...], cosq_ref[...], sinq_ref[...]
    inv_n = 1.0 / QK_HEAD
    scale = lambda x: lax.rsqrt(jnp.sum(x * x, axis=-1, keepdims=True) * inv_n + NORM_EPS)

    for h in range(MLA_HEADS):
        sl = slice(h * HEAD_SLOT, (h + 1) * HEAD_SLOT)
        kh = kn[:, sl] + pe
        k_ref[0, h] = (scale(kh) * (kh * cosk + pe_rot) + kb_ref[...]).astype(BF16)
        qh = qq[:, sl]
        q_ref[0, h] = (scale(qh) * (qh * cosq + qr[:, sl] * sinq) + qb_ref[...]).astype(BF16)
        v_ref[0, h] = (vv[:, sl] + ones_hi).astype(BF16)


def _qkv(ua_l, ua_c, kvn, qn, wk, wv, wq, wqr, kg, kb, qb, tabs, tm):
    b, s, wa = ua_l.shape
    tc = ua_c.shape[1]
    nl, ncx = s // tm, tc // tm
    full = lambda a: pl.BlockSpec(a.shape, lambda bi, i: (0,) * a.ndim)
    tab = pl.BlockSpec((tm, LANE), lambda bi, i: (i, 0))
    head = pl.BlockSpec((1, MLA_HEADS, tm, HEAD_SLOT), lambda bi, i: (bi, 0, i, 0))
    shape = jax.ShapeDtypeStruct((b, MLA_HEADS, s + tc, HEAD_SLOT), BF16)
    return pl.pallas_call(
        functools.partial(_qkv_body, n_lat=nl),
        grid=(b, nl + ncx),
        in_specs=[pl.BlockSpec((1, tm, wa), lambda bi, i: (bi, jnp.minimum(i, nl - 1), 0)),
                  pl.BlockSpec((1, tm, wa), lambda bi, i: (bi, jnp.maximum(i - nl, 0), 0)),
                  full(kvn), full(qn), full(wk), full(wv), full(wq), full(wqr), full(kg),
                  full(kb), full(qb)] + [tab] * len(tabs),
        out_specs=[head, head, head],
        out_shape=[shape, shape, shape],
        compiler_params=_cparams(("parallel", "parallel")),
        name="qkv",
    )(ua_l, ua_c, kvn, qn, wk, wv, wq, wqr, kg, kb, qb, *tabs)


BIAS_LANE = QK_HEAD
MAX_STATIC_BOUND = 50.0


def _attn_finish(acc_ref, o_ref):
    bq = acc_ref.shape[1]
    lane = _iota2((bq, LANE), 1)
    o0 = acc_ref[0] / pltpu.roll(acc_ref[0], V_HEAD, 1)
    o1 = acc_ref[1] / pltpu.roll(acc_ref[1], V_HEAD, 1)
    o_ref[0] = jnp.where(lane < V_HEAD, o0, pltpu.roll(o1, V_HEAD, 1)).astype(o_ref.dtype)


def _attn_static_body(q_ref, k_ref, v_ref, o_ref, acc_ref, *, bk):
    acc_ref[...] = jnp.zeros(acc_ref.shape, F32)

    def step(j, carry):
        rows = pl.ds(pl.multiple_of(j * bk, bk), bk)
        s = [lax.dot_general(q_ref[0, hh], k_ref[0, hh, rows, :], NT, preferred_element_type=F32)
             for hh in range(2)]
        p = [jnp.exp2(x.astype(BF16)) for x in s]
        for hh in range(2):
            acc_ref[hh] += jnp.dot(p[hh], v_ref[0, hh, rows, :], preferred_element_type=F32)
        return carry

    lax.fori_loop(0, k_ref.shape[2] // bk, step, 0)
    _attn_finish(acc_ref, o_ref)


def _attn_online_body(q_ref, k_ref, v_ref, o_ref, acc_ref, m_ref, *, bk):
    acc_ref[...] = jnp.zeros(acc_ref.shape, F32)
    m_ref[...] = jnp.full(m_ref.shape, -jnp.inf, F32)

    def step(j, carry):
        rows = pl.ds(pl.multiple_of(j * bk, bk), bk)
        for hh in range(2):
            s = lax.dot_general(q_ref[0, hh], k_ref[0, hh, rows, :], NT,
                                preferred_element_type=F32)
            m_prev = m_ref[hh]
            m_new = jnp.maximum(m_prev, jnp.max(s, axis=-1, keepdims=True))
            p = jnp.exp2(s - m_new)
            acc_ref[hh] = (jnp.exp2(m_prev - m_new) * acc_ref[hh]
                           + jnp.dot(p.astype(BF16), v_ref[0, hh, rows, :],
                                     preferred_element_type=F32))
            m_ref[hh] = m_new
        return carry

    lax.fori_loop(0, k_ref.shape[2] // bk, step, 0)
    _attn_finish(acc_ref, o_ref)


def _attention(q, k, v, static_ok, q_start, q_rows, k_start, k_rows, bq, bk):
    b, h, _, e = q.shape
    bq, bk = min(bq, q_rows), min(bk, k_rows)
    qi0, kj0 = q_start // bq, k_start // k_rows
    kv_blk = pl.BlockSpec((1, 2, k_rows, e), lambda bi, p, i: (bi, p, kj0, 0))

    def call(online):
        scratch = [pltpu.VMEM((2, bq, LANE), F32)]
        if online:
            scratch.append(pltpu.VMEM((2, bq, 1), F32))
        return pl.pallas_call(
            functools.partial(_attn_online_body if online else _attn_static_body, bk=bk),
            grid=(b, h // 2, q_rows // bq),
            in_specs=[pl.BlockSpec((1, 2, bq, e), lambda bi, p, i: (bi, p, qi0 + i, 0)),
                      kv_blk, kv_blk],
            out_specs=pl.BlockSpec((1, bq, 2 * V_HEAD), lambda bi, p, i: (bi, i, p)),
            out_shape=jax.ShapeDtypeStruct((b, q_rows, h * V_HEAD), ACT),
            scratch_shapes=scratch,
            compiler_params=_cparams(("parallel", "parallel", "arbitrary")),
            name="attention_online" if online else "attention",
        )(q, k, v)

    return lax.cond(static_ok, lambda: call(False), lambda: call(True))


def _dft_mats(n):
    idx = np.arange(n)
    ang = 2.0 * np.pi * ((idx[:, None] * idx[None, :]) % n) / n
    return np.cos(ang), np.sin(ang)


def _hilo(a):
    a = jnp.asarray(a, F32)
    hi = a.astype(BF16)
    return hi, (a - hi.astype(F32)).astype(BF16)


def _mm3c(ah, al, b, dn=NN):
    bh, bl = _split(b)
    d = lambda x, y: lax.dot_general(x, y, dn, preferred_element_type=F32)
    return d(ah, bh) + d(al, bh) + d(ah, bl)


def _four_rows_body(x_ref, w_ref, tc_ref, ts_ref, o_ref):
    r = tc_ref.shape[0]
    y = jnp.dot(w_ref[...], x_ref[0, 0], preferred_element_type=F32)
    yc, ys = y[:r], y[r:]
    tc, ts = tc_ref[...], ts_ref[...]
    o_ref[0, 0, :r] = (yc * tc - ys * ts).astype(o_ref.dtype)
    o_ref[0, 0, r:] = (yc * ts + ys * tc).astype(o_ref.dtype)


def _four_cols_body(y_ref, w_ref, cs_ref, wf_ref, o_ref, y3_scr, *, krt, scale):
    def one(j, carry):
        rows = pl.ds(pl.multiple_of(j * GRID_W, GRID_W), GRID_W)
        ycs = jnp.concatenate([y_ref[0, 0, 0, rows, :], y_ref[0, 0, 1, rows, :]], axis=0)
        y3 = jnp.dot(w_ref[...], ycs, preferred_element_type=F32)
        y3_scr[rows, :] = jnp.concatenate([y3[:GRID_W], y3[GRID_W:]], axis=1).astype(BF16)
        return carry

    lax.fori_loop(0, krt, one, 0, unroll=8)
    f = jnp.dot(y3_scr[...], cs_ref[...], preferred_element_type=F32) * scale
    o_ref[0, 0] = _mm(f, wf_ref[0]).astype(o_ref.dtype)


def _fourier_latent(xf, w_fnet):
    b, g, t, gd = xf.shape
    r = t // GRID_W
    wide = GRID_W * gd
    xv = xf.reshape(b, g, r, wide)
    cr, sr = _dft_mats(r)
    w_rows = jnp.asarray(np.concatenate([cr, sr], axis=0), BF16)
    kr_i, c_i = np.arange(r)[:, None], np.arange(GRID_W)[None, :]
    ang = 2.0 * np.pi * ((kr_i * c_i) % t) / t
    twc = jnp.repeat(jnp.asarray(np.cos(ang), F32), gd, axis=1)
    tws = jnp.repeat(jnp.asarray(np.sin(ang), F32), gd, axis=1)
    tl = min(FOUR_LANE_TILE, wide)
    y2 = pl.pallas_call(
        _four_rows_body,
        grid=(b, g, wide // tl),
        in_specs=[pl.BlockSpec((1, 1, r, tl), lambda bi, gi, l: (bi, gi, 0, l)),
                  pl.BlockSpec((2 * r, r), lambda bi, gi, l: (0, 0)),
                  pl.BlockSpec((r, tl), lambda bi, gi, l: (0, l)),
                  pl.BlockSpec((r, tl), lambda bi, gi, l: (0, l))],
        out_specs=pl.BlockSpec((1, 1, 2 * r, tl), lambda bi, gi, l: (bi, gi, 0, l)),
        out_shape=jax.ShapeDtypeStruct((b, g, 2 * r, wide), ACT),
        compiler_params=_cparams(("parallel", "parallel", "parallel")),
        name="fourier_rows",
    )(xv, w_rows, twc, tws)
    y2v = y2.reshape(b, g, 2, r * GRID_W, gd)
    c64, s64 = _dft_mats(GRID_W)
    w_cols = jnp.asarray(np.block([[c64, -s64], [s64, c64]]), BF16)
    cc, sc = _dft_mats(gd)
    w_chan = jnp.asarray(np.concatenate([cc, -sc], axis=0), BF16)
    krt = min(FOUR_ROWS_TILE, r)
    const = lambda a: pl.BlockSpec(a.shape, lambda bi, gi, i: (0, 0))
    fo = pl.pallas_call(
        functools.partial(_four_cols_body, krt=krt, scale=1.0 / math.sqrt(t * gd)),
        grid=(b, g, r // krt),
        in_specs=[pl.BlockSpec((1, 1, 2, krt * GRID_W, gd), lambda bi, gi, i: (bi, gi, 0, i, 0)),
                  const(w_cols), const(w_chan),
                  pl.BlockSpec((1, gd, gd), lambda bi, gi, i: (gi, 0, 0))],
        out_specs=pl.BlockSpec((1, 1, krt * GRID_W, gd), lambda bi, gi, i: (bi, gi, i, 0)),
        out_shape=jax.ShapeDtypeStruct((b, g, r * GRID_W, gd), ACT),
        scratch_shapes=[pltpu.VMEM((krt * GRID_W, 2 * gd), BF16)],
        compiler_params=_cparams(("parallel", "parallel", "parallel")),
        name="fourier_cols",
    )(y2v, w_cols, w_chan, w_fnet)
    return fo.reshape(b, g, r, GRID_W, gd).transpose(0, 3, 2, 1, 4).reshape(b, t, g * gd)


def _four_dense_body(x_ref, ch_ref, cl_ref, th_ref, tl_ref, sh_ref, sl_ref, wf_ref, o_ref, *, scale):
    x = x_ref[0, 0]
    xh, xl = _split(x)
    d = lambda a, b: jnp.dot(a, b, preferred_element_type=F32)
    z = d(xh, ch_ref[...]) + d(xl, ch_ref[...]) + d(xh, cl_ref[...])
    zc, zs = z[:, :FNET_GROUP_DIM], z[:, FNET_GROUP_DIM:]
    f = (_mm3c(th_ref[...], tl_ref[...], zc) - _mm3c(sh_ref[...], sl_ref[...], zs)) * scale
    o_ref[0] = _mm3(f, wf_ref[0]).astype(o_ref.dtype)


def _fourier_dense(xf, w_fnet):
    b, g, t, gd = xf.shape
    cc, sc = _dft_mats(gd)
    ch, cl = _hilo(np.concatenate([cc, sc], axis=1))
    ct, st = _dft_mats(t)
    cth, ctl = _hilo(ct)
    sth, stl = _hilo(st)
    sq = pl.BlockSpec((t, t), lambda bi, gi: (0, 0))
    cs = pl.BlockSpec((gd, 2 * gd), lambda bi, gi: (0, 0))
    return pl.pallas_call(
        functools.partial(_four_dense_body, scale=1.0 / math.sqrt(t * gd)),
        grid=(b, g),
        in_specs=[pl.BlockSpec((1, 1, t, gd), lambda bi, gi: (bi, gi, 0, 0)), cs, cs, sq, sq, sq, sq,
                  pl.BlockSpec((1, gd, gd), lambda bi, gi: (gi, 0, 0))],
        out_specs=pl.BlockSpec((1, t, gd), lambda bi, gi: (bi, 0, gi)),
        out_shape=jax.ShapeDtypeStruct((b, t, g * gd), ACT),
        compiler_params=_cparams(("parallel", "parallel")),
        name="fourier_dense",
    )(xf, ch, cl, cth, ctl, sth, stl, w_fnet)


def _merge_body(o_ref, f_ref, gate_ref, x_ref, gl_ref, w_ref, out_ref):
    gt = gate_ref[0].astype(F32)
    mix = jnp.concatenate([o_ref[0], f_ref[0]], axis=-1).astype(F32) * (gt * _sigmoid(gt))
    y = jnp.dot(mix.astype(BF16), w_ref[...], preferred_element_type=F32)
    out_ref[0] = x_ref[0] + gl_ref[0] * y


def _merge(o, f, gate, x, gl, w, tm):
    b, t, d = x.shape
    tm = min(tm, t)
    half = o.shape[2]
    tok = lambda n: pl.BlockSpec((1, tm, n), lambda bi, i: (bi, i, 0))
    return pl.pallas_call(
        _merge_body,
        grid=(b, t // tm),
        in_specs=[tok(half), tok(half), tok(d), tok(d),
                  pl.BlockSpec((1, 1, d), lambda bi, i: (bi, 0, 0)),
                  pl.BlockSpec(w.shape, lambda bi, i: (0, 0))],
        out_specs=tok(d),
        out_shape=jax.ShapeDtypeStruct((b, t, d), F32),
        compiler_params=_cparams(("parallel", "parallel")),
        name="merge",
    )(o, f, gate, x, gl, w)


EXP_M05 = math.exp(-0.5)


PAIRS_PER_STEP = 8


def _rwkv_local_body(zk_ref, zv_ref, zr_ref, zwa_ref, w0_ref, w2_ref, a0_ref, a2_ref,
                     kk_ref, ka_ref, rk_ref, m_ref, g_ref, qt_ref, yl_ref, bn_ref):
    c = CHUNK
    zwa = zwa_ref[0].astype(F32)
    lora_w, lora_a = jnp.tanh(zwa[:, :LANE]), zwa[:, LANE:]

    head0 = _iota2((1, LANE), 1) < RWKV_HEAD
    r2 = _iota2((LANE, LANE), 0)
    c2 = _iota2((LANE, LANE), 1)
    same = (r2 // RWKV_HEAD) == (c2 // RWKV_HEAD)
    ones_bd = same.astype(F32)
    eye = r2 == c2
    eye_f = eye.astype(F32)
    rc = _iota2((c, c), 0)
    cc = _iota2((c, c), 1)

    def stack(x):
        z = jnp.zeros_like(x)
        return jnp.concatenate([jnp.where(head0, x, z), jnp.where(head0, z, x)], axis=0)

    stack_b = lambda x: stack(x.astype(BF16))
    fold = lambda x: x[:c] + x[c:]

    pairs = range(PAIRS_PER_STEP)
    chains = [(q, d) for q in pairs for d in range(2)]
    qls = [slice(q * LANE, (q + 1) * LANE) for q in pairs]
    ks = [zk_ref[0, :, ql].astype(F32) for ql in qls]
    vs_ = [zv_ref[0, :, ql].astype(F32) for ql in qls]
    rs = [zr_ref[0, :, ql].astype(F32) for ql in qls]
    wraw = [_mm(lora_w, w2_ref[q]) + w0_ref[q] for q in pairs]
    araw = [_mm(lora_a, a2_ref[q]) + a0_ref[q] for q in pairs]
    logw = [-EXP_M05 * _sigmoid(w) for w in wraw]
    a_all = [_sigmoid(a) for a in araw]
    kk0 = [ks[q] * kk_ref[:, qls[q]] for q in pairs]
    ss = [_mm2r(x * x, ones_bd) for x in kk0]
    kk = [kk0[q] / jnp.maximum(jnp.sqrt(ss[q]), 1e-12) for q in pairs]
    vstk = [stack_b(v) for v in vs_]

    dsl = [slice(d * LANE, (d + 1) * LANE) for d in range(2)]
    lw = [logw[q][:, dsl[d]] for q, d in chains]
    ad = [a_all[q][:, dsl[d]] for q, d in chains]
    kd = [ks[q] * (1.0 + (ad[i] - 1.0) * ka_ref[:, qls[q]]) for i, (q, d) in enumerate(chains)]
    bb = [kk[q] * ad[i] for i, (q, d) in enumerate(chains)]
    bonus = [_mm2r(rs[q] * kd[i] * rk_ref[:, qls[q]], ones_bd) * vs_[q]
             for i, (q, d) in enumerate(chains)]
    r3 = _iota2((c, 3 * c), 0)
    c3 = _iota2((c, 3 * c), 1) & (c - 1)
    tri3 = [(c3 <= r3).astype(BF16), (c3 >= r3).astype(BF16)]
    strict = [same & (c2 < r2), same & (c2 > r2)]
    incl = [same & (c2 <= r2), same & (c2 >= r2)]

    def cumsum(x, tri):
        xh, xl = _split(x)
        xll = (x - xh.astype(F32) - xl.astype(F32)).astype(BF16)
        return jnp.dot(tri, jnp.concatenate([xh, xl, xll], axis=0), preferred_element_type=F32)

    lc = [cumsum(lw[i], tri3[d]) for i, (q, d) in enumerate(chains)]
    ltot = [lc[i][c - 1:c] if d == 0 else lc[i][0:1] for i, (q, d) in enumerate(chains)]
    kkd_s = [stack_b(kk[q] * jnp.exp(lc[i] - lw[i])) for i, (q, d) in enumerate(chains)]
    rd_s = [stack(rs[q] * jnp.exp(lc[i])) for i, (q, d) in enumerate(chains)]
    e_inv = [jnp.exp(-x) for x in lc]
    inv_s = [jnp.concatenate([stack_b(bb[i] * e_inv[i]), stack_b(kd[i] * e_inv[i])], axis=0)
             for i in range(len(chains))]
    amat = [lax.dot_general(jnp.concatenate([kkd_s[i], rd_s[i].astype(BF16)], axis=0), inv_s[i],
                            NT, preferred_element_type=F32).astype(BF16)
            for i in range(len(chains))]
    zero_b = jnp.zeros((LANE, LANE), BF16)
    a_kb_b = [jnp.where(strict[d], amat[i][:LANE, :LANE], zero_b)
              for i, (q, d) in enumerate(chains)]
    a_kb = [x.astype(F32) for x in a_kb_b]
    a_kk = [jnp.where(strict[d], amat[i][:LANE, LANE:], zero_b)
            for i, (q, d) in enumerate(chains)]
    aq_b = [jnp.where(incl[d], amat[i][LANE:, :LANE], zero_b) for i, (q, d) in enumerate(chains)]
    aq_k = [jnp.where(incl[d], amat[i][LANE:, LANE:], zero_b) for i, (q, d) in enumerate(chains)]
    av = [jnp.dot(jnp.concatenate([a_kk[i], aq_k[i]], axis=0), vstk[q],
                  preferred_element_type=F32) for i, (q, d) in enumerate(chains)]
    n = len(chains)
    bdot = lambda a, b: jnp.dot(a, b, preferred_element_type=F32)
    tinv = [eye_f - a for a in a_kb]
    qpow = [bdot(a, a).astype(BF16) for a in a_kb_b]
    for _ in range(4):
        prod = [bdot(qpow[i], jnp.concatenate([qpow[i], tinv[i].astype(BF16)], axis=1))
                for i in range(n)]
        qpow = [x[:, :LANE].astype(BF16) for x in prod]
        tinv = [tinv[i] + prod[i][:, LANE:] for i in range(n)]
    tinv = [tinv[i] + bdot(qpow[i], tinv[i].astype(BF16)) for i in range(n)]
    resid = [eye_f - _mm2l(eye_f + a_kb[i], tinv[i]) for i in range(n)]
    tinv = [tinv[i] + _mm(tinv[i], resid[i]) for i in range(n)]
    x = [bdot(tinv[i].astype(BF16),
              jnp.concatenate([kkd_s[i], av[i][:LANE].astype(BF16)], axis=1)) for i in range(n)]
    xb = [v.astype(BF16) for v in x]
    qy = [jnp.concatenate([rd_s[i], av[i][LANE:]], axis=1) - bdot(aq_b[i], xb[i])
          for i in range(n)]
    e_end = [jnp.exp(ltot[i] - lc[i]) for i in range(n)]
    ends = [jnp.concatenate([stack_b(-bb[i] * e_end[i]), stack_b(kd[i] * e_end[i])], axis=0)
            for i in range(n)]
    wuv = [jnp.concatenate([xb[i], jnp.concatenate([zero_b, vstk[q]], axis=1)], axis=0)
           for i, (q, d) in enumerate(chains)]
    mg = [lax.dot_general(ends[i], wuv[i], TN, preferred_element_type=F32) for i in range(n)]
    for i, (q, d) in enumerate(chains):
        m_ref[0, 0, d, q] = fold(jnp.where(eye, jnp.exp(ltot[i]), 0.0)
                                 + mg[i][:, :LANE]).astype(BF16)
        g_ref[0, 0, d, q] = fold(mg[i][:, LANE:]).astype(g_ref.dtype)
        qt_ref[0, d, :, qls[q]] = fold(qy[i][:, :LANE]).astype(BF16)
    for q in pairs:
        yl_ref[0, :, qls[q]] = (fold(qy[2 * q][:, LANE:])
                                + fold(qy[2 * q + 1][:, LANE:])).astype(yl_ref.dtype)
        bn_ref[0, :, qls[q]] = (bonus[2 * q] + bonus[2 * q + 1]).astype(bn_ref.dtype)


def _rwkv_local(z, w0p, w2p, a0p, a2p, k_k, k_a, r_k):
    b, t, _ = z.shape
    w = k_k.shape[1]
    npair = w // LANE
    pp = PAIRS_PER_STEP
    ng = npair // pp
    wl = pp * LANE
    nc = t // CHUNK
    tokc = lambda base: pl.BlockSpec((1, CHUNK, wl), lambda bi, ci, p: (bi, ci, base + p))
    perp3 = lambda n: pl.BlockSpec((pp, n, 2 * LANE), lambda bi, ci, p: (p, 0, 0))
    vecp = pl.BlockSpec((1, wl), lambda bi, ci, p: (0, p))
    mat = pl.BlockSpec((1, 1, 2, pp, CHUNK, LANE), lambda bi, ci, p: (bi, ci, 0, p, 0, 0))
    return pl.pallas_call(
        _rwkv_local_body,
        grid=(b, nc, ng),
        in_specs=[tokc(0), tokc(ng), tokc(2 * ng),
                  pl.BlockSpec((1, CHUNK, 2 * LANE), lambda bi, ci, p: (bi, ci, 3 * npair // 2)),
                  perp3(1), perp3(LANE), perp3(1), perp3(LANE), vecp, vecp, vecp],
        out_specs=[mat, mat,
                   pl.BlockSpec((1, 2, CHUNK, wl), lambda bi, ci, p: (bi, 0, ci, p)),
                   pl.BlockSpec((1, CHUNK, wl), lambda bi, ci, p: (bi, ci, p)),
                   pl.BlockSpec((1, CHUNK, wl), lambda bi, ci, p: (bi, ci, p))],
        out_shape=[jax.ShapeDtypeStruct((b, nc, 2, npair, CHUNK, LANE), BF16),
                   jax.ShapeDtypeStruct((b, nc, 2, npair, CHUNK, LANE), ACT),
                   jax.ShapeDtypeStruct((b, 2, t, w), BF16),
                   jax.ShapeDtypeStruct((b, t, w), ACT),
                   jax.ShapeDtypeStruct((b, t, w), ACT)],
        compiler_params=_cparams(("parallel", "parallel", "parallel")),
        name="rwkv_local",
    )(z, z, z, z, w0p, w2p, a0p, a2p, k_k, k_a, r_k)


SCAN_CHUNKS = 8


def _rwkv_scan_body(m0_ref, g0_ref, q0_ref, m1_ref, g1_ref, q1_ref, h0_ref,
                    y0_ref, y1_ref, hfin_ref, h_scr, *, npair, cs):
    ci = pl.program_id(1)

    @pl.when(ci == 0)
    def _():
        h_scr[...] = h0_ref[0]

    head0 = _iota2((1, LANE), 1) < RWKV_HEAD

    def expand(x):
        z = jnp.zeros_like(x)
        return jnp.concatenate([jnp.where(head0, x, z), jnp.where(head0, z, x)], axis=0)

    refs = ((m0_ref, g0_ref, q0_ref, y0_ref), (m1_ref, g1_ref, q1_ref, y1_ref))
    chains = [(d, p) for d in range(2) for p in range(npair)]
    lanes = [slice(p * LANE, (p + 1) * LANE) for p in range(npair)]
    h = [h_scr[d, p] for d, p in chains]
    for step in range(cs):
        ck = (step, cs - 1 - step)
        rows = [slice(c * CHUNK, (c + 1) * CHUNK) for c in ck]
        hb = [x.astype(BF16) for x in h]
        ys = [jnp.dot(refs[d][2][0, 0, rows[d], lanes[p]], hb[i], preferred_element_type=F32)
              for i, (d, p) in enumerate(chains)]
        for i, (d, p) in enumerate(chains):
            refs[d][3][0, rows[d], lanes[p]] = ys[i].astype(refs[d][3].dtype)
        h = [jnp.dot(expand(refs[d][0][0, ck[d], 0, p]), hb[i], preferred_element_type=F32)
             + expand(refs[d][1][0, ck[d], 0, p]) for i, (d, p) in enumerate(chains)]
    for i, (d, p) in enumerate(chains):
        h_scr[d, p] = h[i]

    @pl.when(ci == pl.num_programs(1) - 1)
    def _():
        hfin_ref[0] = h_scr[...]


def _rwkv_scan(mm, gg, qt, h0):
    b, nc, _, npair, _, _ = mm.shape
    t, w = qt.shape[2], qt.shape[3]
    cs = SCAN_CHUNKS if nc % SCAN_CHUNKS == 0 else 1
    nb = nc // cs
    fwd = lambda bi, ci: (bi, ci, 0, 0, 0, 0)
    rev = lambda bi, ci: (bi, nb - 1 - ci, 1, 0, 0, 0)
    mblk = (1, cs, 1, npair, CHUNK, LANE)
    hspec = pl.BlockSpec((1, 2, npair, LANE, LANE), lambda bi, ci: (bi, 0, 0, 0, 0))
    return pl.pallas_call(
        functools.partial(_rwkv_scan_body, npair=npair, cs=cs),
        grid=(b, nb),
        in_specs=[pl.BlockSpec(mblk, fwd), pl.BlockSpec(mblk, fwd),
                  pl.BlockSpec((1, 1, cs * CHUNK, w), lambda bi, ci: (bi, 0, ci, 0)),
                  pl.BlockSpec(mblk, rev), pl.BlockSpec(mblk, rev),
                  pl.BlockSpec((1, 1, cs * CHUNK, w), lambda bi, ci: (bi, 1, nb - 1 - ci, 0)),
                  hspec],
        out_specs=[pl.BlockSpec((1, cs * CHUNK, w), lambda bi, ci: (bi, ci, 0)),
                   pl.BlockSpec((1, cs * CHUNK, w), lambda bi, ci: (bi, nb - 1 - ci, 0)),
                   hspec],
        out_shape=[jax.ShapeDtypeStruct((b, t, w), ACT), jax.ShapeDtypeStruct((b, t, w), ACT),
                   jax.ShapeDtypeStruct(h0.shape, F32)],
        scratch_shapes=[pltpu.VMEM((2, npair, LANE, LANE), F32)],
        compiler_params=_cparams(("parallel", "arbitrary")),
        name="rwkv_scan",
    )(mm, gg, qt, mm, gg, qt, h0)


def _rwkv_out_body(y0_ref, y1_ref, yl_ref, bn_ref, gate_ref, x_ref, gl_ref, gnw_ref, gnb_ref,
                   w_ref, o_ref):
    y = y0_ref[0].astype(F32) + y1_ref[0].astype(F32) + yl_ref[0].astype(F32)
    r2 = _iota2((LANE, LANE), 0)
    c2 = _iota2((LANE, LANE), 1)
    avg = ((r2 // RWKV_HEAD) == (c2 // RWKV_HEAD)).astype(F32) * (1.0 / RWKV_HEAD)
    parts = []
    for p in range(y.shape[1] // LANE):
        yp = y[:, p * LANE:(p + 1) * LANE]
        dl = yp - _mm2r(yp, avg)
        var = _mm2r(dl * dl, avg)
        parts.append(dl * lax.rsqrt(var + GN_EPS))
    yn = jnp.concatenate(parts, axis=1)
    gt = gate_ref[0].astype(F32)
    act = (yn * gnw_ref[...] + gnb_ref[...] + bn_ref[0].astype(F32)) * (gt * _sigmoid(gt))
    out = jnp.dot(act.astype(BF16), w_ref[...], preferred_element_type=F32)
    o_ref[0] = x_ref[0] + gl_ref[0] * out


def _rwkv_out(y0, y1, yl, bn, gate, x, gl, gnw, gnb, w, tm):
    b, t, d = x.shape
    tm = min(tm, t)
    wd = y0.shape[2]
    tok = lambda n: pl.BlockSpec((1, tm, n), lambda bi, i: (bi, i, 0))
    return pl.pallas_call(
        _rwkv_out_body,
        grid=(b, t // tm),
        in_specs=[tok(wd), tok(wd), tok(wd), tok(wd), tok(wd), tok(d),
                  pl.BlockSpec((1, 1, d), lambda bi, i: (bi, 0, 0)),
                  pl.BlockSpec((1, wd), lambda bi, i: (0, 0)),
                  pl.BlockSpec((1, wd), lambda bi, i: (0, 0)),
                  pl.BlockSpec(w.shape, lambda bi, i: (0, 0))],
        out_specs=tok(d),
        out_shape=jax.ShapeDtypeStruct((b, t, d), F32),
        compiler_params=_cparams(("parallel", "parallel")),
        name="rwkv_out",
    )(y0, y1, yl, bn, gate, x, gl, gnw, gnb, w)


def _rope_tables(t):
    rows = t // GRID_W
    row = jnp.repeat(jnp.arange(rows, dtype=F32), GRID_W)
    col = jnp.tile(jnp.arange(GRID_W, dtype=F32), rows)
    inv = 1.0 / (ROPE_BASE ** (jnp.arange(ROPE_FREQS, dtype=F32) / ROPE_FREQS))
    ang = jnp.stack([row[:, None] * inv, col[:, None] * inv], axis=1)
    cos, sin = jnp.cos(ang), jnp.sin(ang)
    zeros = jnp.zeros_like(sin)
    ones_lo = jnp.ones((t, QK_NOPE), F32)
    pad_hi = HEAD_SLOT - QK_HEAD
    cos_t = jnp.concatenate([ones_lo, jnp.concatenate([cos, cos], axis=2).reshape(t, QK_ROPE),
                             jnp.ones((t, pad_hi), F32)], axis=1)
    sa = jnp.concatenate([jnp.zeros((t, QK_NOPE), F32),
                          jnp.concatenate([-sin, zeros], axis=2).reshape(t, QK_ROPE),
                          jnp.zeros((t, pad_hi), F32)], axis=1)
    sb = jnp.concatenate([jnp.zeros((t, QK_NOPE), F32),
                          jnp.concatenate([zeros, sin], axis=2).reshape(t, QK_ROPE),
                          jnp.zeros((t, pad_hi), F32)], axis=1)
    return cos_t, sa, sb


def _even_layer(x, ctx, mod_l, mod_c, need_ctx, g, w_in, kv_norm, q_norm, w_uq, w_ukv,
                q_head_norm, k_head_norm, w_fnet, w_out):
    b, s, d = x.shape
    tc = ctx.shape[1]
    e_q0 = KV_LORA + QK_ROPE
    e_f0 = e_q0 + Q_LORA
    e_g0 = e_f0 + FNET_GROUPS * FNET_GROUP_DIM
    w_p = jnp.concatenate([w_in[:, e_g0:], w_in[:, e_f0:e_g0], w_in[:, :e_q0],
                           jnp.zeros((d, LANE - QK_ROPE), F32), w_in[:, e_q0:e_f0]],
                          axis=1).astype(BF16)
    splits = (d, (FNET_GROUPS, FNET_GROUP_DIM), KV_LORA + LANE + Q_LORA)
    kvw = w_ukv.reshape(KV_LORA, MLA_HEADS, QK_NOPE + V_HEAD)
    wk = jnp.pad(kvw[:, :, :QK_NOPE], ((0, 0), (0, 0), (0, HEAD_SLOT - QK_NOPE)))
    wk = wk.reshape(KV_LORA, MLA_HEADS * HEAD_SLOT).astype(BF16)
    wv = jnp.pad(kvw[:, :, QK_NOPE:], ((0, 0), (0, 0), (0, HEAD_SLOT - V_HEAD)))
    wv = wv.reshape(KV_LORA, MLA_HEADS * HEAD_SLOT).astype(BF16)
    wq3 = jnp.pad(w_uq.reshape(Q_LORA, MLA_HEADS, QK_HEAD), ((0, 0), (0, 0), (0, HEAD_SLOT - QK_HEAD)))
    wq = wq3.reshape(Q_LORA, MLA_HEADS * HEAD_SLOT).astype(BF16)
    kg = jnp.pad(k_head_norm, (0, HEAD_SLOT - QK_HEAD)).reshape(1, HEAD_SLOT)
    qg = (jnp.pad(q_head_norm, (0, HEAD_SLOT - QK_HEAD))
          * (QK_HEAD ** -0.5 * math.log2(math.e))).reshape(1, HEAD_SLOT)
    lane = np.arange(HEAD_SLOT)
    tail = (lane >= QK_NOPE) & (lane < QK_HEAD)
    first = tail & (((lane - QK_NOPE) // ROPE_FREQS) % 2 == 0)
    partner = np.where(first, lane + ROPE_FREQS, np.where(tail, lane - ROPE_FREQS, lane))
    sign = np.where(first, -1.0, np.where(tail, 1.0, 0.0)).astype(np.float32)
    wqr = (wq3[:, :, partner] * (sign * qg[0, partner])).reshape(Q_LORA, MLA_HEADS * HEAD_SLOT)
    wqr = wqr.astype(BF16)
    kvn, qn = kv_norm.reshape(1, -1), q_norm.reshape(1, -1)
    g2 = g.reshape(1, d)
    bound = (1.02 * QK_HEAD * jnp.max(jnp.abs(qg)) * jnp.max(jnp.abs(kg))).astype(BF16).astype(F32)
    static_ok = bound <= MAX_STATIC_BOUND
    bias_lane = (jnp.arange(HEAD_SLOT) == BIAS_LANE).astype(F32).reshape(1, HEAD_SLOT)
    kb = bias_lane * jnp.where(static_ok, -bound, 0.0)
    qb = bias_lane

    gate_l, four_l, ua_l = _proj(x, g2, mod_l[1], mod_l[0], w_p, splits, TOKEN_TILE)
    gate_c, four_c, ua_c = _proj(ctx, g2, mod_c[1], mod_c[0], w_p, splits, TOKEN_TILE)
    sk = s + tc
    cos_t, sa, sb = _rope_tables(s)
    cos_t = jnp.concatenate([cos_t, jnp.ones((tc, HEAD_SLOT), F32)], axis=0)
    sa = jnp.concatenate([sa, jnp.zeros((tc, HEAD_SLOT), F32)], axis=0)
    sb = jnp.concatenate([sb, jnp.zeros((tc, HEAD_SLOT), F32)], axis=0)
    tabs = (cos_t * kg, cos_t * qg, sa, sb, sb - sa)
    q_all, k_all, v_all = _qkv(ua_l, ua_c, kvn, qn, wk, wv, wq, wqr, kg, kb, qb, tabs,
                               math.gcd(s, tc))
    bk = ATTN_K_BLOCK if sk % ATTN_K_BLOCK == 0 else tc
    o_l = _attention(q_all, k_all, v_all, static_ok, 0, s, 0, sk, ATTN_Q_BLOCK, bk)
    f_l = _fourier_latent(four_l, w_fnet)
    wo = w_out.astype(BF16)
    x_new = _merge(o_l, f_l, gate_l, x, mod_l[2], wo, TOKEN_TILE)
    ctx_new = ctx
    if need_ctx:
        o_c = _attention(q_all, k_all, v_all, static_ok, s, tc, s, tc, tc, tc)
        f_c = _fourier_dense(four_c, w_fnet)
        ctx_new = _merge(o_c, f_c, gate_c, ctx, mod_c[2], wo, TOKEN_TILE)
    return x_new, ctx_new


def _odd_layer(x, ctx, mod_l, mod_c, need_ctx, g, w_in, shift_w, w0, w2, a0, a2, k_k, k_a, r_k,
               gn_w, gn_b, w_out):
    b, s, d = x.shape
    w = k_k.shape[0]
    npair = w // LANE
    o_wd0 = 2 * w
    o_r0 = o_wd0 + 2 * DECAY_LORA + 2 * AAA_LORA
    conv_ch = o_r0 + w
    segs = ((0, o_wd0), (o_r0, w), (o_wd0, o_r0 - o_wd0))
    w_p = w_in.astype(BF16)
    sw = shift_w
    g2 = g.reshape(1, d)

    def pairs(vec2):
        return vec2.reshape(2, npair, LANE).transpose(1, 0, 2).reshape(npair, 1, 2 * LANE)

    def pair_mats(m):
        rr = m.shape[1]
        mp = m.reshape(2, rr, npair, LANE).transpose(2, 0, 1, 3)
        z = jnp.zeros_like(mp[:, 0])
        top = jnp.concatenate([mp[:, 0], z], axis=2)
        bot = jnp.concatenate([z, mp[:, 1]], axis=2)
        return jnp.concatenate([top, bot], axis=1).astype(BF16)

    w0p, a0p, w2p, a2p = pairs(w0), pairs(a0), pair_mats(w2), pair_mats(a2)
    kk2, ka2, rk2 = k_k.reshape(1, w), k_a.reshape(1, w), r_k.reshape(1, w)
    wo = w_out.astype(BF16)

    def mix(xin, mod, h0):
        z, gate = _proj_shift(xin, g2, mod[1], mod[0], w_p, sw, conv_ch, segs, TOKEN_TILE)
        mm, gg, qt, yl, bn = _rwkv_local(z, w0p, w2p, a0p, a2p, kk2, ka2, rk2)
        y0, y1, hfin = _rwkv_scan(mm, gg, qt, h0)
        return (y0, y1, yl, bn, gate), hfin

    h_zero = jnp.zeros((b, 2, npair, LANE, LANE), F32)
    parts_c, h_ctx = mix(ctx, mod_c, h_zero)
    parts_l, _ = mix(x, mod_l, h_ctx)
    gnw, gnb = gn_w.reshape(1, w), gn_b.reshape(1, w)
    x_new = _rwkv_out(*parts_l, x, mod_l[2], gnw, gnb, wo, TOKEN_TILE)
    ctx_new = ctx
    if need_ctx:
        ctx_new = _rwkv_out(*parts_c, ctx, mod_c[2], gnw, gnb, wo, TOKEN_TILE)
    return x_new, ctx_new


def kernel(x, c, ctx, c_ctx, ada_w, ada_b, norm_g, e_w_in, e_kv_norm, e_q_norm, e_w_uq, e_w_ukv,
           e_q_head_norm, e_k_head_norm, e_w_fnet, e_w_out, o_w_in, o_shift_w, o_w0, o_w2, o_a0,
           o_a2, o_k_k, o_k_a, o_r_k, o_gn_w, o_gn_b, o_w_out):
    b, s, d = x.shape
    depth = ada_w.shape[0]
    assert b + 1 <= 8
    cond8 = jnp.concatenate([c, c_ctx[None, :], jnp.zeros((8 - b - 1, d), F32)], axis=0)
    mod = _ada(cond8, ada_w, ada_b)
    for layer in range(depth):
        need_ctx = layer < depth - 1
        m = mod[layer]
        chunk = lambda rows, i: rows[:, None, i * d:(i + 1) * d]
        lat, cx = m[:b], jnp.broadcast_to(m[b:b + 1], (b, 3 * d))
        mod_l = (chunk(lat, 0), 1.0 + chunk(lat, 1), chunk(lat, 2))
        mod_c = (chunk(cx, 0), 1.0 + chunk(cx, 1), chunk(cx, 2))
        j = layer // 2
        if layer % 2 == 0:
            x, ctx = _even_layer(x, ctx, mod_l, mod_c, need_ctx, norm_g[layer], e_w_in[j],
                                 e_kv_norm[j], e_q_norm[j], e_w_uq[j], e_w_ukv[j],
                                 e_q_head_norm[j], e_k_head_norm[j], e_w_fnet[j], e_w_out[j])
        else:
            x, ctx = _odd_layer(x, ctx, mod_l, mod_c, need_ctx, norm_g[layer], o_w_in[j],
                                o_shift_w[j], o_w0[j], o_w2[j], o_a0[j], o_a2[j], o_k_k[j],
                                o_k_a[j], o_r_k[j].reshape(-1), o_gn_w[j], o_gn_b[j], o_w_out[j])
    return x
```

```python
import functools
import math

import numpy as np
import jax
import jax.numpy as jnp
from jax import lax
from jax.experimental import pallas as pl
from jax.experimental.pallas import tpu as pltpu

F32 = jnp.float32
BF16 = jnp.bfloat16
ACT = BF16

GRID_W = 64
NORM_EPS = 1e-6
MLA_HEADS = 8
QK_NOPE = 64
QK_ROPE = 32
QK_HEAD = QK_NOPE + QK_ROPE
V_HEAD = 64
Q_LORA = 384
KV_LORA = 256
ROPE_FREQS = QK_ROPE // 4
ROPE_BASE = 10000.0
FNET_GROUPS = 4
FNET_GROUP_DIM = 128
RWKV_HEAD = 64
DECAY_LORA = 64
AAA_LORA = 64
GN_EPS = 64e-5

LANE = 128
CHUNK = 64
HEAD_SLOT = 128
VMEM_LIMIT = 56 * 1024 * 1024

TOKEN_TILE = 512
ATTN_Q_BLOCK = 2048
ATTN_K_BLOCK = 768
FOUR_LANE_TILE = 2048
FOUR_ROWS_TILE = 32

NN = (((1,), (0,)), ((), ()))
NT = (((1,), (1,)), ((), ()))
TN = (((0,), (0,)), ((), ()))


def _cparams(sem):
    return pltpu.CompilerParams(dimension_semantics=sem, vmem_limit_bytes=VMEM_LIMIT)


def _mm(a, b, dn=NN):
    return lax.dot_general(a.astype(BF16), b.astype(BF16), dn, preferred_element_type=F32)


def _split(a):
    hi = a.astype(BF16)
    lo = (a - hi.astype(F32)).astype(BF16)
    return hi, lo


def _mm3(a, b, dn=NN):
    ah, al = _split(a)
    bh, bl = _split(b)
    d = lambda x, y: lax.dot_general(x, y, dn, preferred_element_type=F32)
    return d(ah, bh) + d(al, bh) + d(ah, bl)


def _mm2r(a, b_exact):
    ah, al = _split(a)
    bb = b_exact.astype(BF16)
    return jnp.dot(jnp.concatenate([ah, al], axis=1), jnp.concatenate([bb, bb], axis=0),
                   preferred_element_type=F32)


def _mm2l(a_exact, b):
    bh, bl = _split(b)
    n = b.shape[1]
    y = jnp.dot(a_exact.astype(BF16), jnp.concatenate([bh, bl], axis=1),
                preferred_element_type=F32)
    return y[:, :n] + y[:, n:]


def _sigmoid(x):
    return 1.0 / (1.0 + jnp.exp(-x))


def _modnorm(x, g, sc1, sh):
    y = x * lax.rsqrt(jnp.mean(x * x, axis=-1, keepdims=True) + NORM_EPS)
    return (y * g) * sc1 + sh


def _iota2(shape, dim):
    return lax.broadcasted_iota(jnp.int32, shape, dim)


def _ada_body(c_ref, w_ref, b_ref, o_ref):
    c = c_ref[...]
    s = c * _sigmoid(c)
    o_ref[0] = _mm3(s, w_ref[0]) + b_ref[0]


def _ada(cond8, ada_w, ada_b):
    depth, d, n = ada_w.shape
    tn = 512
    return pl.pallas_call(
        _ada_body,
        grid=(depth, n // tn),
        in_specs=[
            pl.BlockSpec((8, d), lambda l, j: (0, 0)),
            pl.BlockSpec((1, d, tn), lambda l, j: (l, 0, j)),
            pl.BlockSpec((1, 1, tn), lambda l, j: (l, 0, j)),
        ],
        out_specs=pl.BlockSpec((1, 8, tn), lambda l, j: (l, 0, j)),
        out_shape=jax.ShapeDtypeStruct((depth, 8, n), F32),
        compiler_params=_cparams(("parallel", "parallel")),
        name="ada",
    )(cond8, ada_w, ada_b.reshape(depth, 1, n))


COL_CHUNK = 512


def _proj_body(x_ref, g_ref, sc_ref, sh_ref, w_ref, *o_refs, splits):
    h = _modnorm(x_ref[0], g_ref[...], sc_ref[0], sh_ref[0]).astype(BF16)
    mm = lambda c0, c1: jnp.dot(h, w_ref[:, c0:c1], preferred_element_type=F32)
    off = 0
    for o_ref, n in zip(o_refs, splits):
        if isinstance(n, tuple):
            groups, width = n
            y = mm(off, off + groups * width).astype(o_ref.dtype)
            for gi in range(groups):
                o_ref[0, gi] = y[:, gi * width:(gi + 1) * width]
            off += groups * width
            continue
        for c0 in range(0, n, COL_CHUNK):
            c1 = min(n, c0 + COL_CHUNK)
            o_ref[0, :, c0:c1] = mm(off + c0, off + c1).astype(o_ref.dtype)
        off += n


def _proj(x, g, sc1, sh, w, splits, tm):
    b, t, d = x.shape
    tm = min(tm, t)
    n = w.shape[1]
    vec = pl.BlockSpec((1, 1, d), lambda bi, i: (bi, 0, 0))
    specs, shapes = [], []
    for s in splits:
        if isinstance(s, tuple):
            specs.append(pl.BlockSpec((1, s[0], tm, s[1]), lambda bi, i: (bi, 0, i, 0)))
            shapes.append(jax.ShapeDtypeStruct((b, s[0], t, s[1]), ACT))
        else:
            specs.append(pl.BlockSpec((1, tm, s), lambda bi, i: (bi, i, 0)))
            shapes.append(jax.ShapeDtypeStruct((b, t, s), ACT))
    return pl.pallas_call(
        functools.partial(_proj_body, splits=splits),
        grid=(b, t // tm),
        in_specs=[
            pl.BlockSpec((1, tm, d), lambda bi, i: (bi, i, 0)),
            pl.BlockSpec((1, d), lambda bi, i: (0, 0)),
            vec, vec,
            pl.BlockSpec((d, n), lambda bi, i: (0, 0)),
        ],
        out_specs=specs,
        out_shape=shapes,
        compiler_params=_cparams(("parallel", "parallel")),
        name="proj",
    )(x, g, sc1, sh, w)


HALO = 16


def _proj_shift_body(x_ref, xp_ref, xn_ref, g_ref, sc_ref, sh_ref, w_ref, sw_ref, z_ref, gate_ref,
                     *, tm, n_conv, segs):
    i = pl.program_id(1)
    last = pl.num_programs(1) - 1
    g, sc1, sh = g_ref[...], sc_ref[0], sh_ref[0]
    h = _modnorm(x_ref[0], g, sc1, sh)
    hp = _modnorm(xp_ref[0], g, sc1, sh) * (i > 0).astype(F32)
    hn = _modnorm(xn_ref[0], g, sc1, sh) * (i < last).astype(F32)
    hb = jnp.concatenate([hp, h, hn], axis=0).astype(BF16)
    rows = tm + 2 * HALO
    dst = 0
    for src, width in segs:
        for c0 in range(0, width, COL_CHUNK):
            cw = min(COL_CHUNK, width - c0)
            cols = slice(src + c0, src + c0 + cw)
            u = jnp.dot(hb, w_ref[:, cols], preferred_element_type=F32)
            up = pltpu.roll(u, 1, 0)[HALO:HALO + tm]
            un = pltpu.roll(u, rows - 1, 0)[HALO:HALO + tm]
            um = u[HALO:HALO + tm]
            z_ref[0, :, dst + c0:dst + c0 + cw] = (
                sw_ref[0:1, cols] * up + sw_ref[1:2, cols] * um
                + sw_ref[2:3, cols] * un).astype(z_ref.dtype)
        dst += width
    hc = hb[HALO:HALO + tm]
    n_all = w_ref.shape[1]
    for c0 in range(n_conv, n_all, COL_CHUNK):
        c1 = min(n_all, c0 + COL_CHUNK)
        gate_ref[0, :, c0 - n_conv:c1 - n_conv] = jnp.dot(
            hc, w_ref[:, c0:c1], preferred_element_type=F32).astype(gate_ref.dtype)


def _proj_shift(x, g, sc1, sh, w, sw, n_conv, segs, tm):
    b, t, d = x.shape
    tm = min(tm, t)
    n = w.shape[1]
    hb = tm // HALO
    nhb = t // HALO
    vec = pl.BlockSpec((1, 1, d), lambda bi, i: (bi, 0, 0))
    return pl.pallas_call(
        functools.partial(_proj_shift_body, tm=tm, n_conv=n_conv, segs=segs),
        grid=(b, t // tm),
        in_specs=[
            pl.BlockSpec((1, tm, d), lambda bi, i: (bi, i, 0)),
            pl.BlockSpec((1, HALO, d), lambda bi, i: (bi, jnp.maximum(i * hb - 1, 0), 0)),
            pl.BlockSpec((1, HALO, d), lambda bi, i: (bi, jnp.minimum((i + 1) * hb, nhb - 1), 0)),
            pl.BlockSpec((1, d), lambda bi, i: (0, 0)),
            vec, vec,
            pl.BlockSpec((d, n), lambda bi, i: (0, 0)),
            pl.BlockSpec((3, n_conv), lambda bi, i: (0, 0)),
        ],
        out_specs=[pl.BlockSpec((1, tm, n_conv), lambda bi, i: (bi, i, 0)),
                   pl.BlockSpec((1, tm, n - n_conv), lambda bi, i: (bi, i, 0))],
        out_shape=[jax.ShapeDtypeStruct((b, t, n_conv), ACT),
                   jax.ShapeDtypeStruct((b, t, n - n_conv), ACT)],
        compiler_params=_cparams(("parallel", "parallel")),
        name="proj_shift",
    )(x, x, x, g, sc1, sh, w, sw)


def _rms(x, g):
    return x * lax.rsqrt(jnp.mean(x * x, axis=-1, keepdims=True) + NORM_EPS) * g


def _qkv_body(ual_ref, uac_ref, kvn_ref, qn_ref, wk_ref, wv_ref, wq_ref, wqr_ref, kg_ref, kb_ref,
              qb_ref, cosk_ref, cosq_ref, sa_ref, sb_ref, sinq_ref, q_ref, k_ref, v_ref, *, n_lat):
    ua = jnp.where(pl.program_id(1) < n_lat, ual_ref[0], uac_ref[0]).astype(F32)
    ckv = _rms(ua[:, :KV_LORA], kvn_ref[...]).astype(BF16)
    kr = ua[:, KV_LORA:KV_LORA + LANE]
    cq = _rms(ua[:, KV_LORA + LANE:], qn_ref[...]).astype(BF16)
    kn = jnp.dot(ckv, wk_ref[...], preferred_element_type=F32)
    vv = jnp.dot(ckv, wv_ref[...], preferred_element_type=F32)
    qq = jnp.dot(cq, wq_ref[...], preferred_element_type=F32)
    qr = jnp.dot(cq, wqr_ref[...], preferred_element_type=F32)
    ones_hi = (_iota2((1, HEAD_SLOT), 1) >= V_HEAD).astype(F32)
    pe = pltpu.roll(kr, QK_NOPE, 1)
    gp = pe * kg_ref[...]
    pe_rot = (pltpu.roll(gp, LANE - ROPE_FREQS, 1) * sa_ref[...]
              + pltpu.roll(gp, ROPE_FREQS, 1) * sb_ref[...])
    cosk, cosq, sinq = cosk_ref[...], cosq_ref[...], sinq_ref[...]
    inv_n = 1.0 / QK_HEAD
    scale = lambda x: lax.rsqrt(jnp.sum(x * x, axis=-1, keepdims=True) * inv_n + NORM_EPS)

    for h in range(MLA_HEADS):
        sl = slice(h * HEAD_SLOT, (h + 1) * HEAD_SLOT)
        kh = kn[:, sl] + pe
        k_ref[0, h] = (scale(kh) * (kh * cosk + pe_rot) + kb_ref[...]).astype(BF16)
        qh = qq[:, sl]
        q_ref[0, h] = (scale(qh) * (qh * cosq + qr[:, sl] * sinq) + qb_ref[...]).astype(BF16)
        v_ref[0, h] = (vv[:, sl] + ones_hi).astype(BF16)


def _qkv(ua_l, ua_c, kvn, qn, wk, wv, wq, wqr, kg, kb, qb, tabs, tm):
    b, s, wa = ua_l.shape
    tc = ua_c.shape[1]
    nl, ncx = s // tm, tc // tm
    full = lambda a: pl.BlockSpec(a.shape, lambda bi, i: (0,) * a.ndim)
    tab = pl.BlockSpec((tm, LANE), lambda bi, i: (i, 0))
    head = pl.BlockSpec((1, MLA_HEADS, tm, HEAD_SLOT), lambda bi, i: (bi, 0, i, 0))
    shape = jax.ShapeDtypeStruct((b, MLA_HEADS, s + tc, HEAD_SLOT), BF16)
    return pl.pallas_call(
        functools.partial(_qkv_body, n_lat=nl),
        grid=(b, nl + ncx),
        in_specs=[pl.BlockSpec((1, tm, wa), lambda bi, i: (bi, jnp.minimum(i, nl - 1), 0)),
                  pl.BlockSpec((1, tm, wa), lambda bi, i: (bi, jnp.maximum(i - nl, 0), 0)),
                  full(kvn), full(qn), full(wk), full(wv), full(wq), full(wqr), full(kg),
                  full(kb), full(qb)] + [tab] * len(tabs),
        out_specs=[head, head, head],
        out_shape=[shape, shape, shape],
        compiler_params=_cparams(("parallel", "parallel")),
        name="qkv",
    )(ua_l, ua_c, kvn, qn, wk, wv, wq, wqr, kg, kb, qb, *tabs)


BIAS_LANE = QK_HEAD
MAX_STATIC_BOUND = 50.0


def _attn_finish(acc_ref, o_ref):
    bq = acc_ref.shape[1]
    lane = _iota2((bq, LANE), 1)
    o0 = acc_ref[0] / pltpu.roll(acc_ref[0], V_HEAD, 1)
    o1 = acc_ref[1] / pltpu.roll(acc_ref[1], V_HEAD, 1)
    o_ref[0] = jnp.where(lane < V_HEAD, o0, pltpu.roll(o1, V_HEAD, 1)).astype(o_ref.dtype)


def _attn_static_body(q_ref, k_ref, v_ref, o_ref, acc_ref, *, bk):
    acc_ref[...] = jnp.zeros(acc_ref.shape, F32)

    def step(j, carry):
        rows = pl.ds(pl.multiple_of(j * bk, bk), bk)
        s = [lax.dot_general(q_ref[0, hh], k_ref[0, hh, rows, :], NT, preferred_element_type=F32)
             for hh in range(2)]
        p = [jnp.exp2(x.astype(BF16)) for x in s]
        for hh in range(2):
            acc_ref[hh] += jnp.dot(p[hh], v_ref[0, hh, rows, :], preferred_element_type=F32)
        return carry

    lax.fori_loop(0, k_ref.shape[2] // bk, step, 0)
    _attn_finish(acc_ref, o_ref)


def _attn_online_body(q_ref, k_ref, v_ref, o_ref, acc_ref, m_ref, *, bk):
    acc_ref[...] = jnp.zeros(acc_ref.shape, F32)
    m_ref[...] = jnp.full(m_ref.shape, -jnp.inf, F32)

    def step(j, carry):
        rows = pl.ds(pl.multiple_of(j * bk, bk), bk)
        for hh in range(2):
            s = lax.dot_general(q_ref[0, hh], k_ref[0, hh, rows, :], NT,
                                preferred_element_type=F32)
            m_prev = m_ref[hh]
            m_new = jnp.maximum(m_prev, jnp.max(s, axis=-1, keepdims=True))
            p = jnp.exp2(s - m_new)
            acc_ref[hh] = (jnp.exp2(m_prev - m_new) * acc_ref[hh]
                           + jnp.dot(p.astype(BF16), v_ref[0, hh, rows, :],
                                     preferred_element_type=F32))
            m_ref[hh] = m_new
        return carry

    lax.fori_loop(0, k_ref.shape[2] // bk, step, 0)
    _attn_finish(acc_ref, o_ref)


def _attention(q, k, v, static_ok, q_start, q_rows, k_start, k_rows, bq, bk):
    b, h, _, e = q.shape
    bq, bk = min(bq, q_rows), min(bk, k_rows)
    qi0, kj0 = q_start // bq, k_start // k_rows
    kv_blk = pl.BlockSpec((1, 2, k_rows, e), lambda bi, p, i: (bi, p, kj0, 0))

    def call(online):
        scratch = [pltpu.VMEM((2, bq, LANE), F32)]
        if online:
            scratch.append(pltpu.VMEM((2, bq, 1), F32))
        return pl.pallas_call(
            functools.partial(_attn_online_body if online else _attn_static_body, bk=bk),
            grid=(b, h // 2, q_rows // bq),
            in_specs=[pl.BlockSpec((1, 2, bq, e), lambda bi, p, i: (bi, p, qi0 + i, 0)),
                      kv_blk, kv_blk],
            out_specs=pl.BlockSpec((1, bq, 2 * V_HEAD), lambda bi, p, i: (bi, i, p)),
            out_shape=jax.ShapeDtypeStruct((b, q_rows, h * V_HEAD), ACT),
            scratch_shapes=scratch,
            compiler_params=_cparams(("parallel", "parallel", "arbitrary")),
            name="attention_online" if online else "attention",
        )(q, k, v)

    return lax.cond(static_ok, lambda: call(False), lambda: call(True))


def _dft_mats(n):
    idx = np.arange(n)
    ang = 2.0 * np.pi * ((idx[:, None] * idx[None, :]) % n) / n
    return np.cos(ang), np.sin(ang)


def _hilo(a):
    a = jnp.asarray(a, F32)
    hi = a.astype(BF16)
    return hi, (a - hi.astype(F32)).astype(BF16)


def _mm3c(ah, al, b, dn=NN):
    bh, bl = _split(b)
    d = lambda x, y: lax.dot_general(x, y, dn, preferred_element_type=F32)
    return d(ah, bh) + d(al, bh) + d(ah, bl)


def _four_rows_body(x_ref, w_ref, tc_ref, ts_ref, o_ref):
    r = tc_ref.shape[0]
    y = jnp.dot(w_ref[...], x_ref[0, 0], preferred_element_type=F32)
    yc, ys = y[:r], y[r:]
    tc, ts = tc_ref[...], ts_ref[...]
    o_ref[0, 0, :r] = (yc * tc - ys * ts).astype(o_ref.dtype)
    o_ref[0, 0, r:] = (yc * ts + ys * tc).astype(o_ref.dtype)


def _four_cols_body(y_ref, w_ref, cs_ref, wf_ref, o_ref, y3_scr, *, krt, scale):
    def one(j, carry):
        rows = pl.ds(pl.multiple_of(j * GRID_W, GRID_W), GRID_W)
        ycs = jnp.concatenate([y_ref[0, 0, 0, rows, :], y_ref[0, 0, 1, rows, :]], axis=0)
        y3 = jnp.dot(w_ref[...], ycs, preferred_element_type=F32)
        y3_scr[rows, :] = jnp.concatenate([y3[:GRID_W], y3[GRID_W:]], axis=1).astype(BF16)
        return carry

    lax.fori_loop(0, krt, one, 0, unroll=8)
    f = jnp.dot(y3_scr[...], cs_ref[...], preferred_element_type=F32) * scale
    o_ref[0, 0] = _mm(f, wf_ref[0]).astype(o_ref.dtype)


def _fourier_latent(xf, w_fnet):
    b, g, t, gd = xf.shape
    r = t // GRID_W
    wide = GRID_W * gd
    xv = xf.reshape(b, g, r, wide)
    cr, sr = _dft_mats(r)
    w_rows = jnp.asarray(np.concatenate([cr, sr], axis=0), BF16)
    kr_i, c_i = np.arange(r)[:, None], np.arange(GRID_W)[None, :]
    ang = 2.0 * np.pi * ((kr_i * c_i) % t) / t
    twc = jnp.repeat(jnp.asarray(np.cos(ang), F32), gd, axis=1)
    tws = jnp.repeat(jnp.asarray(np.sin(ang), F32), gd, axis=1)
    tl = min(FOUR_LANE_TILE, wide)
    y2 = pl.pallas_call(
        _four_rows_body,
        grid=(b, g, wide // tl),
        in_specs=[pl.BlockSpec((1, 1, r, tl), lambda bi, gi, l: (bi, gi, 0, l)),
                  pl.BlockSpec((2 * r, r), lambda bi, gi, l: (0, 0)),
                  pl.BlockSpec((r, tl), lambda bi, gi, l: (0, l)),
                  pl.BlockSpec((r, tl), lambda bi, gi, l: (0, l))],
        out_specs=pl.BlockSpec((1, 1, 2 * r, tl), lambda bi, gi, l: (bi, gi, 0, l)),
        out_shape=jax.ShapeDtypeStruct((b, g, 2 * r, wide), ACT),
        compiler_params=_cparams(("parallel", "parallel", "parallel")),
        name="fourier_rows",
    )(xv, w_rows, twc, tws)
    y2v = y2.reshape(b, g, 2, r * GRID_W, gd)
    c64, s64 = _dft_mats(GRID_W)
    w_cols = jnp.asarray(np.block([[c64, -s64], [s64, c64]]), BF16)
    cc, sc = _dft_mats(gd)
    w_chan = jnp.asarray(np.concatenate([cc, -sc], axis=0), BF16)
    krt = min(FOUR_ROWS_TILE, r)
    const = lambda a: pl.BlockSpec(a.shape, lambda bi, gi, i: (0, 0))
    fo = pl.pallas_call(
        functools.partial(_four_cols_body, krt=krt, scale=1.0 / math.sqrt(t * gd)),
        grid=(b, g, r // krt),
        in_specs=[pl.BlockSpec((1, 1, 2, krt * GRID_W, gd), lambda bi, gi, i: (bi, gi, 0, i, 0)),
                  const(w_cols), const(w_chan),
                  pl.BlockSpec((1, gd, gd), lambda bi, gi, i: (gi, 0, 0))],
        out_specs=pl.BlockSpec((1, 1, krt * GRID_W, gd), lambda bi, gi, i: (bi, gi, i, 0)),
        out_shape=jax.ShapeDtypeStruct((b, g, r * GRID_W, gd), ACT),
        scratch_shapes=[pltpu.VMEM((krt * GRID_W, 2 * gd), BF16)],
        compiler_params=_cparams(("parallel", "parallel", "parallel")),
        name="fourier_cols",
    )(y2v, w_cols, w_chan, w_fnet)
    return fo.reshape(b, g, r, GRID_W, gd).transpose(0, 3, 2, 1, 4).reshape(b, t, g * gd)


def _four_dense_body(x_ref, ch_ref, cl_ref, th_ref, tl_ref, sh_ref, sl_ref, wf_ref, o_ref, *, scale):
    x = x_ref[0, 0]
    xh, xl = _split(x)
    d = lambda a, b: jnp.dot(a, b, preferred_element_type=F32)
    z = d(xh, ch_ref[...]) + d(xl, ch_ref[...]) + d(xh, cl_ref[...])
    zc, zs = z[:, :FNET_GROUP_DIM], z[:, FNET_GROUP_DIM:]
    f = (_mm3c(th_ref[...], tl_ref[...], zc) - _mm3c(sh_ref[...], sl_ref[...], zs)) * scale
    o_ref[0] = _mm3(f, wf_ref[0]).astype(o_ref.dtype)


def _fourier_dense(xf, w_fnet):
    b, g, t, gd = xf.shape
    cc, sc = _dft_mats(gd)
    ch, cl = _hilo(np.concatenate([cc, sc], axis=1))
    ct, st = _dft_mats(t)
    cth, ctl = _hilo(ct)
    sth, stl = _hilo(st)
    sq = pl.BlockSpec((t, t), lambda bi, gi: (0, 0))
    cs = pl.BlockSpec((gd, 2 * gd), lambda bi, gi: (0, 0))
    return pl.pallas_call(
        functools.partial(_four_dense_body, scale=1.0 / math.sqrt(t * gd)),
        grid=(b, g),
        in_specs=[pl.BlockSpec((1, 1, t, gd), lambda bi, gi: (bi, gi, 0, 0)), cs, cs, sq, sq, sq, sq,
                  pl.BlockSpec((1, gd, gd), lambda bi, gi: (gi, 0, 0))],
        out_specs=pl.BlockSpec((1, t, gd), lambda bi, gi: (bi, 0, gi)),
        out_shape=jax.ShapeDtypeStruct((b, t, g * gd), ACT),
        compiler_params=_cparams(("parallel", "parallel")),
        name="fourier_dense",
    )(xf, ch, cl, cth, ctl, sth, stl, w_fnet)


def _merge_body(o_ref, f_ref, gate_ref, x_ref, gl_ref, w_ref, out_ref):
    gt = gate_ref[0].astype(F32)
    mix = jnp.concatenate([o_ref[0], f_ref[0]], axis=-1).astype(F32) * (gt * _sigmoid(gt))
    y = jnp.dot(mix.astype(BF16), w_ref[...], preferred_element_type=F32)
    out_ref[0] = x_ref[0] + gl_ref[0] * y


def _merge(o, f, gate, x, gl, w, tm):
    b, t, d = x.shape
    tm = min(tm, t)
    half = o.shape[2]
    tok = lambda n: pl.BlockSpec((1, tm, n), lambda bi, i: (bi, i, 0))
    return pl.pallas_call(
        _merge_body,
        grid=(b, t // tm),
        in_specs=[tok(half), tok(half), tok(d), tok(d),
                  pl.BlockSpec((1, 1, d), lambda bi, i: (bi, 0, 0)),
                  pl.BlockSpec(w.shape, lambda bi, i: (0, 0))],
        out_specs=tok(d),
        out_shape=jax.ShapeDtypeStruct((b, t, d), F32),
        compiler_params=_cparams(("parallel", "parallel")),
        name="merge",
    )(o, f, gate, x, gl, w)


EXP_M05 = math.exp(-0.5)


PAIRS_PER_STEP = 8


def _rwkv_local_body(zk_ref, zv_ref, zr_ref, zwa_ref, w0_ref, w2_ref, a0_ref, a2_ref,
                     kk_ref, ka_ref, rk_ref, m_ref, g_ref, qt_ref, yl_ref, bn_ref):
    c = CHUNK
    zwa = zwa_ref[0].astype(F32)
    lora_w, lora_a = jnp.tanh(zwa[:, :LANE]), zwa[:, LANE:]

    head0 = _iota2((1, LANE), 1) < RWKV_HEAD
    r2 = _iota2((LANE, LANE), 0)
    c2 = _iota2((LANE, LANE), 1)
    same = (r2 // RWKV_HEAD) == (c2 // RWKV_HEAD)
    ones_bd = same.astype(F32)
    eye = r2 == c2

    def stack(x):
        z = jnp.zeros_like(x)
        return jnp.concatenate([jnp.where(head0, x, z), jnp.where(head0, z, x)], axis=0)

    stack_b = lambda x: stack(x.astype(BF16))
    fold = lambda x: x[:c] + x[c:]

    pairs = range(PAIRS_PER_STEP)
    chains = [(q, d) for q in pairs for d in range(2)]
    qls = [slice(q * LANE, (q + 1) * LANE) for q in pairs]
    ks = [zk_ref[0, :, ql].astype(F32) for ql in qls]
    vs_ = [zv_ref[0, :, ql].astype(F32) for ql in qls]
    rs = [zr_ref[0, :, ql].astype(F32) for ql in qls]
    wraw = [_mm(lora_w, w2_ref[q]) + w0_ref[q] for q in pairs]
    araw = [_mm(lora_a, a2_ref[q]) + a0_ref[q] for q in pairs]
    logw = [-EXP_M05 * _sigmoid(w) for w in wraw]
    a_all = [_sigmoid(a) for a in araw]
    kk0 = [ks[q] * kk_ref[:, qls[q]] for q in pairs]
    ss = [_mm2r(x * x, ones_bd) for x in kk0]
    kk = [kk0[q] / jnp.maximum(jnp.sqrt(ss[q]), 1e-12) for q in pairs]
    vstk = [stack_b(v) for v in vs_]

    dsl = [slice(d * LANE, (d + 1) * LANE) for d in range(2)]
    lw = [logw[q][:, dsl[d]] for q, d in chains]
    ad = [a_all[q][:, dsl[d]] for q, d in chains]
    kd = [ks[q] * (1.0 + (ad[i] - 1.0) * ka_ref[:, qls[q]]) for i, (q, d) in enumerate(chains)]
    bb = [kk[q] * ad[i] for i, (q, d) in enumerate(chains)]
    bonus = [_mm2r(rs[q] * kd[i] * rk_ref[:, qls[q]], ones_bd) * vs_[q]
             for i, (q, d) in enumerate(chains)]
    r3 = _iota2((c, 3 * c), 0)
    c3 = _iota2((c, 3 * c), 1) & (c - 1)
    tri3 = [(c3 <= r3).astype(BF16), (c3 >= r3).astype(BF16)]
    tt = _iota2((c, LANE), 0)
    ts = _iota2((c, LANE), 1) & (c - 1)
    strict = [ts < tt, ts > tt]
    incl = [ts <= tt, ts >= tt]
    eye_c = (ts == tt).astype(F32)

    def cumsum(x, tri):
        xh, xl = _split(x)
        xll = (x - xh.astype(F32) - xl.astype(F32)).astype(BF16)
        return jnp.dot(tri, jnp.concatenate([xh, xl, xll], axis=0), preferred_element_type=F32)

    lc = [cumsum(lw[i], tri3[d]) for i, (q, d) in enumerate(chains)]
    ltot = [lc[i][c - 1:c] if d == 0 else lc[i][0:1] for i, (q, d) in enumerate(chains)]
    n = len(chains)
    bdot = lambda a, b: jnp.dot(a, b, preferred_element_type=F32)
    kkd = [(kk[q] * jnp.exp(lc[i] - lw[i])).astype(BF16) for i, (q, d) in enumerate(chains)]
    rd = [rs[q] * jnp.exp(lc[i]) for i, (q, d) in enumerate(chains)]
    e_inv = [jnp.exp(-x) for x in lc]
    inv_s = [jnp.concatenate([stack_b(bb[i] * e_inv[i]), stack_b(kd[i] * e_inv[i])], axis=0)
             for i in range(n)]
    amat = [lax.dot_general(jnp.concatenate([kkd[i], rd[i].astype(BF16)], axis=0), inv_s[i],
                            NT, preferred_element_type=F32).astype(BF16) for i in range(n)]
    zero_c = jnp.zeros((c, LANE), BF16)
    a_kb = [jnp.where(strict[d], amat[i][:c, :LANE], zero_c) for i, (q, d) in enumerate(chains)]
    a_kk = [jnp.where(strict[d], amat[i][:c, LANE:], zero_c) for i, (q, d) in enumerate(chains)]
    aq_b = [jnp.where(incl[d], amat[i][c:, :LANE], zero_c) for i, (q, d) in enumerate(chains)]
    aq_k = [jnp.where(incl[d], amat[i][c:, LANE:], zero_c) for i, (q, d) in enumerate(chains)]
    av = [bdot(jnp.concatenate([a_kk[i], aq_k[i]], axis=0), vstk[q])
          for i, (q, d) in enumerate(chains)]
    tinv = [eye_c - a.astype(F32) for a in a_kb]
    qpow = [bdot(a, stack(a)).astype(BF16) for a in a_kb]
    for _ in range(4):
        prod = [bdot(qpow[i], jnp.concatenate([stack(qpow[i]), stack_b(tinv[i])], axis=1))
                for i in range(n)]
        qpow = [x[:, :LANE].astype(BF16) for x in prod]
        tinv = [tinv[i] + prod[i][:, LANE:] for i in range(n)]
    tinv = [tinv[i] + bdot(qpow[i], stack_b(tinv[i])) for i in range(n)]
    tsplit = [_split(t) for t in tinv]
    ia_t = [bdot((eye_c + a_kb[i].astype(F32)).astype(BF16),
                 jnp.concatenate([stack(tsplit[i][0]), stack(tsplit[i][1])], axis=1))
            for i in range(n)]
    resid = [eye_c - ia_t[i][:, :LANE] - ia_t[i][:, LANE:] for i in range(n)]
    tinv = [tinv[i] + bdot(tinv[i].astype(BF16), stack_b(resid[i])) for i in range(n)]
    x = [bdot(tinv[i].astype(BF16),
              jnp.concatenate([stack(kkd[i]), stack_b(av[i][:c])], axis=1))
         for i in range(n)]
    xb = [v.astype(BF16) for v in x]
    qy = [jnp.concatenate([rd[i], av[i][c:]], axis=1)
          - bdot(aq_b[i], jnp.concatenate([stack(xb[i][:, :LANE]), stack(xb[i][:, LANE:])],
                                          axis=1)) for i in range(n)]
    e_end = [jnp.exp(ltot[i] - lc[i]) for i in range(n)]
    ends = [jnp.concatenate([(-bb[i] * e_end[i]).astype(BF16), (kd[i] * e_end[i]).astype(BF16)],
                            axis=0) for i in range(n)]
    wuv = [jnp.concatenate([xb[i], jnp.concatenate([zero_c, vs_[q].astype(BF16)], axis=1)],
                           axis=0) for i, (q, d) in enumerate(chains)]
    mg = [lax.dot_general(ends[i], wuv[i], TN, preferred_element_type=F32) for i in range(n)]
    for i, (q, d) in enumerate(chains):
        m_ref[0, 0, d, q] = fold(jnp.where(eye, jnp.exp(ltot[i]), 0.0)
                                 + jnp.where(same, mg[i][:, :LANE], 0.0)).astype(BF16)
        g_ref[0, 0, d, q] = fold(jnp.where(same, mg[i][:, LANE:], 0.0)).astype(g_ref.dtype)
        qt_ref[0, d, :, qls[q]] = qy[i][:, :LANE].astype(BF16)
    for q in pairs:
        yl_ref[0, :, qls[q]] = (qy[2 * q][:, LANE:] + qy[2 * q + 1][:, LANE:]).astype(yl_ref.dtype)
        bn_ref[0, :, qls[q]] = (bonus[2 * q] + bonus[2 * q + 1]).astype(bn_ref.dtype)


def _rwkv_local(z, w0p, w2p, a0p, a2p, k_k, k_a, r_k):
    b, t, _ = z.shape
    w = k_k.shape[1]
    npair = w // LANE
    pp = PAIRS_PER_STEP
    ng = npair // pp
    wl = pp * LANE
    nc = t // CHUNK
    tokc = lambda base: pl.BlockSpec((1, CHUNK, wl), lambda bi, ci, p: (bi, ci, base + p))
    perp3 = lambda n: pl.BlockSpec((pp, n, 2 * LANE), lambda bi, ci, p: (p, 0, 0))
    vecp = pl.BlockSpec((1, wl), lambda bi, ci, p: (0, p))
    mat = pl.BlockSpec((1, 1, 2, pp, CHUNK, LANE), lambda bi, ci, p: (bi, ci, 0, p, 0, 0))
    return pl.pallas_call(
        _rwkv_local_body,
        grid=(b, nc, ng),
        in_specs=[tokc(0), tokc(ng), tokc(2 * ng),
                  pl.BlockSpec((1, CHUNK, 2 * LANE), lambda bi, ci, p: (bi, ci, 3 * npair // 2)),
                  perp3(1), perp3(LANE), perp3(1), perp3(LANE), vecp, vecp, vecp],
        out_specs=[mat, mat,
                   pl.BlockSpec((1, 2, CHUNK, wl), lambda bi, ci, p: (bi, 0, ci, p)),
                   pl.BlockSpec((1, CHUNK, wl), lambda bi, ci, p: (bi, ci, p)),
                   pl.BlockSpec((1, CHUNK, wl), lambda bi, ci, p: (bi, ci, p))],
        out_shape=[jax.ShapeDtypeStruct((b, nc, 2, npair, CHUNK, LANE), BF16),
                   jax.ShapeDtypeStruct((b, nc, 2, npair, CHUNK, LANE), ACT),
                   jax.ShapeDtypeStruct((b, 2, t, w), BF16),
                   jax.ShapeDtypeStruct((b, t, w), ACT),
                   jax.ShapeDtypeStruct((b, t, w), ACT)],
        compiler_params=_cparams(("parallel", "parallel", "parallel")),
        name="rwkv_local",
    )(z, z, z, z, w0p, w2p, a0p, a2p, k_k, k_a, r_k)


SCAN_CHUNKS = 8


def _rwkv_scan_body(m0_ref, g0_ref, q0_ref, m1_ref, g1_ref, q1_ref, h0_ref,
                    y0_ref, y1_ref, hfin_ref, h_scr, *, npair, cs):
    ci = pl.program_id(1)

    @pl.when(ci == 0)
    def _():
        h_scr[...] = h0_ref[0]

    head0 = _iota2((1, LANE), 1) < RWKV_HEAD

    def expand(x):
        z = jnp.zeros_like(x)
        return jnp.concatenate([jnp.where(head0, x, z), jnp.where(head0, z, x)], axis=0)

    refs = ((m0_ref, g0_ref, q0_ref, y0_ref), (m1_ref, g1_ref, q1_ref, y1_ref))
    chains = [(d, p) for d in range(2) for p in range(npair)]
    lanes = [slice(p * LANE, (p + 1) * LANE) for p in range(npair)]
    h = [h_scr[d, p] for d, p in chains]
    for step in range(cs):
        ck = (step, cs - 1 - step)
        rows = [slice(c * CHUNK, (c + 1) * CHUNK) for c in ck]
        hb = [x.astype(BF16) for x in h]
        res = [jnp.dot(jnp.concatenate([refs[d][2][0, 0, rows[d], lanes[p]],
                                        refs[d][0][0, ck[d], 0, p]], axis=0), hb[i],
                       preferred_element_type=F32) for i, (d, p) in enumerate(chains)]
        for i, (d, p) in enumerate(chains):
            refs[d][3][0, rows[d], lanes[p]] = res[i][:CHUNK].astype(refs[d][3].dtype)
        h = [expand(res[i][CHUNK:] + refs[d][1][0, ck[d], 0, p].astype(F32))
             for i, (d, p) in enumerate(chains)]
    for i, (d, p) in enumerate(chains):
        h_scr[d, p] = h[i]

    @pl.when(ci == pl.num_programs(1) - 1)
    def _():
        hfin_ref[0] = h_scr[...]


def _rwkv_scan(mm, gg, qt, h0):
    b, nc, _, npair, _, _ = mm.shape
    t, w = qt.shape[2], qt.shape[3]
    cs = SCAN_CHUNKS if nc % SCAN_CHUNKS == 0 else 1
    nb = nc // cs
    fwd = lambda bi, ci: (bi, ci, 0, 0, 0, 0)
    rev = lambda bi, ci: (bi, nb - 1 - ci, 1, 0, 0, 0)
    mblk = (1, cs, 1, npair, CHUNK, LANE)
    hspec = pl.BlockSpec((1, 2, npair, LANE, LANE), lambda bi, ci: (bi, 0, 0, 0, 0))
    return pl.pallas_call(
        functools.partial(_rwkv_scan_body, npair=npair, cs=cs),
        grid=(b, nb),
        in_specs=[pl.BlockSpec(mblk, fwd), pl.BlockSpec(mblk, fwd),
                  pl.BlockSpec((1, 1, cs * CHUNK, w), lambda bi, ci: (bi, 0, ci, 0)),
                  pl.BlockSpec(mblk, rev), pl.BlockSpec(mblk, rev),
                  pl.BlockSpec((1, 1, cs * CHUNK, w), lambda bi, ci: (bi, 1, nb - 1 - ci, 0)),
                  hspec],
        out_specs=[pl.BlockSpec((1, cs * CHUNK, w), lambda bi, ci: (bi, ci, 0)),
                   pl.BlockSpec((1, cs * CHUNK, w), lambda bi, ci: (bi, nb - 1 - ci, 0)),
                   hspec],
        out_shape=[jax.ShapeDtypeStruct((b, t, w), ACT), jax.ShapeDtypeStruct((b, t, w), ACT),
                   jax.ShapeDtypeStruct(h0.shape, F32)],
        scratch_shapes=[pltpu.VMEM((2, npair, LANE, LANE), F32)],
        compiler_params=_cparams(("parallel", "arbitrary")),
        name="rwkv_scan",
    )(mm, gg, qt, mm, gg, qt, h0)


def _rwkv_out_body(y0_ref, y1_ref, yl_ref, bn_ref, gate_ref, x_ref, gl_ref, gnw_ref, gnb_ref,
                   w_ref, o_ref):
    y = y0_ref[0].astype(F32) + y1_ref[0].astype(F32) + yl_ref[0].astype(F32)
    r2 = _iota2((LANE, LANE), 0)
    c2 = _iota2((LANE, LANE), 1)
    avg = ((r2 // RWKV_HEAD) == (c2 // RWKV_HEAD)).astype(F32) * (1.0 / RWKV_HEAD)
    parts = []
    for p in range(y.shape[1] // LANE):
        yp = y[:, p * LANE:(p + 1) * LANE]
        dl = yp - _mm2r(yp, avg)
        var = _mm2r(dl * dl, avg)
        parts.append(dl * lax.rsqrt(var + GN_EPS))
    yn = jnp.concatenate(parts, axis=1)
    gt = gate_ref[0].astype(F32)
    act = (yn * gnw_ref[...] + gnb_ref[...] + bn_ref[0].astype(F32)) * (gt * _sigmoid(gt))
    out = jnp.dot(act.astype(BF16), w_ref[...], preferred_element_type=F32)
    o_ref[0] = x_ref[0] + gl_ref[0] * out


def _rwkv_out(y0, y1, yl, bn, gate, x, gl, gnw, gnb, w, tm):
    b, t, d = x.shape
    tm = min(tm, t)
    wd = y0.shape[2]
    tok = lambda n: pl.BlockSpec((1, tm, n), lambda bi, i: (bi, i, 0))
    return pl.pallas_call(
        _rwkv_out_body,
        grid=(b, t // tm),
        in_specs=[tok(wd), tok(wd), tok(wd), tok(wd), tok(wd), tok(d),
                  pl.BlockSpec((1, 1, d), lambda bi, i: (bi, 0, 0)),
                  pl.BlockSpec((1, wd), lambda bi, i: (0, 0)),
                  pl.BlockSpec((1, wd), lambda bi, i: (0, 0)),
                  pl.BlockSpec(w.shape, lambda bi, i: (0, 0))],
        out_specs=tok(d),
        out_shape=jax.ShapeDtypeStruct((b, t, d), F32),
        compiler_params=_cparams(("parallel", "parallel")),
        name="rwkv_out",
    )(y0, y1, yl, bn, gate, x, gl, gnw, gnb, w)


def _rope_tables(t):
    rows = t // GRID_W
    row = jnp.repeat(jnp.arange(rows, dtype=F32), GRID_W)
    col = jnp.tile(jnp.arange(GRID_W, dtype=F32), rows)
    inv = 1.0 / (ROPE_BASE ** (jnp.arange(ROPE_FREQS, dtype=F32) / ROPE_FREQS))
    ang = jnp.stack([row[:, None] * inv, col[:, None] * inv], axis=1)
    cos, sin = jnp.cos(ang), jnp.sin(ang)
    zeros = jnp.zeros_like(sin)
    ones_lo = jnp.ones((t, QK_NOPE), F32)
    pad_hi = HEAD_SLOT - QK_HEAD
    cos_t = jnp.concatenate([ones_lo, jnp.concatenate([cos, cos], axis=2).reshape(t, QK_ROPE),
                             jnp.ones((t, pad_hi), F32)], axis=1)
    sa = jnp.concatenate([jnp.zeros((t, QK_NOPE), F32),
                          jnp.concatenate([-sin, zeros], axis=2).reshape(t, QK_ROPE),
                          jnp.zeros((t, pad_hi), F32)], axis=1)
    sb = jnp.concatenate([jnp.zeros((t, QK_NOPE), F32),
                          jnp.concatenate([zeros, sin], axis=2).reshape(t, QK_ROPE),
                          jnp.zeros((t, pad_hi), F32)], axis=1)
    return cos_t, sa, sb


def _even_layer(x, ctx, mod_l, mod_c, need_ctx, g, w_in, kv_norm, q_norm, w_uq, w_ukv,
                q_head_norm, k_head_norm, w_fnet, w_out):
    b, s, d = x.shape
    tc = ctx.shape[1]
    e_q0 = KV_LORA + QK_ROPE
    e_f0 = e_q0 + Q_LORA
    e_g0 = e_f0 + FNET_GROUPS * FNET_GROUP_DIM
    w_p = jnp.concatenate([w_in[:, e_g0:], w_in[:, e_f0:e_g0], w_in[:, :e_q0],
                           jnp.zeros((d, LANE - QK_ROPE), F32), w_in[:, e_q0:e_f0]],
                          axis=1).astype(BF16)
    splits = (d, (FNET_GROUPS, FNET_GROUP_DIM), KV_LORA + LANE + Q_LORA)
    kvw = w_ukv.reshape(KV_LORA, MLA_HEADS, QK_NOPE + V_HEAD)
    wk = jnp.pad(kvw[:, :, :QK_NOPE], ((0, 0), (0, 0), (0, HEAD_SLOT - QK_NOPE)))
    wk = wk.reshape(KV_LORA, MLA_HEADS * HEAD_SLOT).astype(BF16)
    wv = jnp.pad(kvw[:, :, QK_NOPE:], ((0, 0), (0, 0), (0, HEAD_SLOT - V_HEAD)))
    wv = wv.reshape(KV_LORA, MLA_HEADS * HEAD_SLOT).astype(BF16)
    wq3 = jnp.pad(w_uq.reshape(Q_LORA, MLA_HEADS, QK_HEAD), ((0, 0), (0, 0), (0, HEAD_SLOT - QK_HEAD)))
    wq = wq3.reshape(Q_LORA, MLA_HEADS * HEAD_SLOT).astype(BF16)
    kg = jnp.pad(k_head_norm, (0, HEAD_SLOT - QK_HEAD)).reshape(1, HEAD_SLOT)
    qg = (jnp.pad(q_head_norm, (0, HEAD_SLOT - QK_HEAD))
          * (QK_HEAD ** -0.5 * math.log2(math.e))).reshape(1, HEAD_SLOT)
    lane = np.arange(HEAD_SLOT)
    tail = (lane >= QK_NOPE) & (lane < QK_HEAD)
    first = tail & (((lane - QK_NOPE) // ROPE_FREQS) % 2 == 0)
    partner = np.where(first, lane + ROPE_FREQS, np.where(tail, lane - ROPE_FREQS, lane))
    sign = np.where(first, -1.0, np.where(tail, 1.0, 0.0)).astype(np.float32)
    wqr = (wq3[:, :, partner] * (sign * qg[0, partner])).reshape(Q_LORA, MLA_HEADS * HEAD_SLOT)
    wqr = wqr.astype(BF16)
    kvn, qn = kv_norm.reshape(1, -1), q_norm.reshape(1, -1)
    g2 = g.reshape(1, d)
    bound = (1.02 * QK_HEAD * jnp.max(jnp.abs(qg)) * jnp.max(jnp.abs(kg))).astype(BF16).astype(F32)
    static_ok = bound <= MAX_STATIC_BOUND
    bias_lane = (jnp.arange(HEAD_SLOT) == BIAS_LANE).astype(F32).reshape(1, HEAD_SLOT)
    kb = bias_lane * jnp.where(static_ok, -bound, 0.0)
    qb = bias_lane

    gate_l, four_l, ua_l = _proj(x, g2, mod_l[1], mod_l[0], w_p, splits, TOKEN_TILE)
    gate_c, four_c, ua_c = _proj(ctx, g2, mod_c[1], mod_c[0], w_p, splits, TOKEN_TILE)
    sk = s + tc
    cos_t, sa, sb = _rope_tables(s)
    cos_t = jnp.concatenate([cos_t, jnp.ones((tc, HEAD_SLOT), F32)], axis=0)
    sa = jnp.concatenate([sa, jnp.zeros((tc, HEAD_SLOT), F32)], axis=0)
    sb = jnp.concatenate([sb, jnp.zeros((tc, HEAD_SLOT), F32)], axis=0)
    tabs = (cos_t * kg, cos_t * qg, sa, sb, sb - sa)
    q_all, k_all, v_all = _qkv(ua_l, ua_c, kvn, qn, wk, wv, wq, wqr, kg, kb, qb, tabs,
                               math.gcd(s, tc))
    bk = ATTN_K_BLOCK if sk % ATTN_K_BLOCK == 0 else tc
    o_l = _attention(q_all, k_all, v_all, static_ok, 0, s, 0, sk, ATTN_Q_BLOCK, bk)
    f_l = _fourier_latent(four_l, w_fnet)
    wo = w_out.astype(BF16)
    x_new = _merge(o_l, f_l, gate_l, x, mod_l[2], wo, TOKEN_TILE)
    ctx_new = ctx
    if need_ctx:
        o_c = _attention(q_all, k_all, v_all, static_ok, s, tc, s, tc, tc, tc)
        f_c = _fourier_dense(four_c, w_fnet)
        ctx_new = _merge(o_c, f_c, gate_c, ctx, mod_c[2], wo, TOKEN_TILE)
    return x_new, ctx_new


def _odd_layer(x, ctx, mod_l, mod_c, need_ctx, g, w_in, shift_w, w0, w2, a0, a2, k_k, k_a, r_k,
               gn_w, gn_b, w_out):
    b, s, d = x.shape
    w = k_k.shape[0]
    npair = w // LANE
    o_wd0 = 2 * w
    o_r0 = o_wd0 + 2 * DECAY_LORA + 2 * AAA_LORA
    conv_ch = o_r0 + w
    segs = ((0, o_wd0), (o_r0, w), (o_wd0, o_r0 - o_wd0))
    w_p = w_in.astype(BF16)
    sw = shift_w
    g2 = g.reshape(1, d)

    def pairs(vec2):
        return vec2.reshape(2, npair, LANE).transpose(1, 0, 2).reshape(npair, 1, 2 * LANE)

    def pair_mats(m):
        rr = m.shape[1]
        mp = m.reshape(2, rr, npair, LANE).transpose(2, 0, 1, 3)
        z = jnp.zeros_like(mp[:, 0])
        top = jnp.concatenate([mp[:, 0], z], axis=2)
        bot = jnp.concatenate([z, mp[:, 1]], axis=2)
        return jnp.concatenate([top, bot], axis=1).astype(BF16)

    w0p, a0p, w2p, a2p = pairs(w0), pairs(a0), pair_mats(w2), pair_mats(a2)
    kk2, ka2, rk2 = k_k.reshape(1, w), k_a.reshape(1, w), r_k.reshape(1, w)
    wo = w_out.astype(BF16)

    def mix(xin, mod, h0):
        z, gate = _proj_shift(xin, g2, mod[1], mod[0], w_p, sw, conv_ch, segs, TOKEN_TILE)
        mm, gg, qt, yl, bn = _rwkv_local(z, w0p, w2p, a0p, a2p, kk2, ka2, rk2)
        y0, y1, hfin = _rwkv_scan(mm, gg, qt, h0)
        return (y0, y1, yl, bn, gate), hfin

    h_zero = jnp.zeros((b, 2, npair, LANE, LANE), F32)
    parts_c, h_ctx = mix(ctx, mod_c, h_zero)
    parts_l, _ = mix(x, mod_l, h_ctx)
    gnw, gnb = gn_w.reshape(1, w), gn_b.reshape(1, w)
    x_new = _rwkv_out(*parts_l, x, mod_l[2], gnw, gnb, wo, TOKEN_TILE)
    ctx_new = ctx
    if need_ctx:
        ctx_new = _rwkv_out(*parts_c, ctx, mod_c[2], gnw, gnb, wo, TOKEN_TILE)
    return x_new, ctx_new


def kernel(x, c, ctx, c_ctx, ada_w, ada_b, norm_g, e_w_in, e_kv_norm, e_q_norm, e_w_uq, e_w_ukv,
           e_q_head_norm, e_k_head_norm, e_w_fnet, e_w_out, o_w_in, o_shift_w, o_w0, o_w2, o_a0,
           o_a2, o_k_k, o_k_a, o_r_k, o_gn_w, o_gn_b, o_w_out):
    b, s, d = x.shape
    depth = ada_w.shape[0]
    assert b + 1 <= 8
    cond8 = jnp.concatenate([c, c_ctx[None, :], jnp.zeros((8 - b - 1, d), F32)], axis=0)
    mod = _ada(cond8, ada_w, ada_b)
    for layer in range(depth):
        need_ctx = layer < depth - 1
        m = mod[layer]
        chunk = lambda rows, i: rows[:, None, i * d:(i + 1) * d]
        lat, cx = m[:b], jnp.broadcast_to(m[b:b + 1], (b, 3 * d))
        mod_l = (chunk(lat, 0), 1.0 + chunk(lat, 1), chunk(lat, 2))
        mod_c = (chunk(cx, 0), 1.0 + chunk(cx, 1), chunk(cx, 2))
        j = layer // 2
        if layer % 2 == 0:
            x, ctx = _even_layer(x, ctx, mod_l, mod_c, need_ctx, norm_g[layer], e_w_in[j],
                                 e_kv_norm[j], e_q_norm[j], e_w_uq[j], e_w_ukv[j],
                                 e_q_head_norm[j], e_k_head_norm[j], e_w_fnet[j], e_w_out[j])
        else:
            x, ctx = _odd_layer(x, ctx, mod_l, mod_c, need_ctx, norm_g[layer], o_w_in[j],
                                o_shift_w[j], o_w0[j], o_w2[j], o_a0[j], o_a2[j], o_k_k[j],
                                o_k_a[j], o_r_k[j].reshape(-1), o_gn_w[j], o_gn_b[j], o_w_out[j])
    return x
```

```python
import functools
import math

import numpy as np
import jax
import jax.numpy as jnp
from jax import lax
from jax.experimental import pallas as pl
from jax.experimental.pallas import tpu as pltpu

F32 = jnp.float32
BF16 = jnp.bfloat16
ACT = BF16

GRID_W = 64
NORM_EPS = 1e-6
MLA_HEADS = 8
QK_NOPE = 64
QK_ROPE = 32
QK_HEAD = QK_NOPE + QK_ROPE
V_HEAD = 64
Q_LORA = 384
KV_LORA = 256
ROPE_FREQS = QK_ROPE // 4
ROPE_BASE = 10000.0
FNET_GROUPS = 4
FNET_GROUP_DIM = 128
RWKV_HEAD = 64
DECAY_LORA = 64
AAA_LORA = 64
GN_EPS = 64e-5

LANE = 128
CHUNK = 64
HEAD_SLOT = 128
VMEM_LIMIT = 56 * 1024 * 1024

TOKEN_TILE = 512
ATTN_Q_BLOCK = 2048
ATTN_K_BLOCK = 768
FOUR_LANE_TILE = 2048
FOUR_ROWS_TILE = 32

NN = (((1,), (0,)), ((), ()))
NT = (((1,), (1,)), ((), ()))
TN = (((0,), (0,)), ((), ()))


def _cparams(sem):
    return pltpu.CompilerParams(dimension_semantics=sem, vmem_limit_bytes=VMEM_LIMIT)


def _mm(a, b, dn=NN):
    return lax.dot_general(a.astype(BF16), b.astype(BF16), dn, preferred_element_type=F32)


def _split(a):
    hi = a.astype(BF16)
    lo = (a - hi.astype(F32)).astype(BF16)
    return hi, lo


def _mm3(a, b, dn=NN):
    ah, al = _split(a)
    bh, bl = _split(b)
    d = lambda x, y: lax.dot_general(x, y, dn, preferred_element_type=F32)
    return d(ah, bh) + d(al, bh) + d(ah, bl)


def _mm2r(a, b_exact):
    ah, al = _split(a)
    bb = b_exact.astype(BF16)
    return jnp.dot(jnp.concatenate([ah, al], axis=1), jnp.concatenate([bb, bb], axis=0),
                   preferred_element_type=F32)


def _mm2l(a_exact, b):
    bh, bl = _split(b)
    n = b.shape[1]
    y = jnp.dot(a_exact.astype(BF16), jnp.concatenate([bh, bl], axis=1),
                preferred_element_type=F32)
    return y[:, :n] + y[:, n:]


def _sigmoid(x):
    return 1.0 / (1.0 + jnp.exp(-x))


def _modnorm(x, g, sc1, sh):
    y = x * lax.rsqrt(jnp.mean(x * x, axis=-1, keepdims=True) + NORM_EPS)
    return (y * g) * sc1 + sh


def _iota2(shape, dim):
    return lax.broadcasted_iota(jnp.int32, shape, dim)


def _ada_body(c_ref, w_ref, b_ref, o_ref):
    c = c_ref[...]
    s = c * _sigmoid(c)
    o_ref[0] = _mm3(s, w_ref[0]) + b_ref[0]


def _ada(cond8, ada_w, ada_b):
    depth, d, n = ada_w.shape
    tn = 512
    return pl.pallas_call(
        _ada_body,
        grid=(depth, n // tn),
        in_specs=[
            pl.BlockSpec((8, d), lambda l, j: (0, 0)),
            pl.BlockSpec((1, d, tn), lambda l, j: (l, 0, j)),
            pl.BlockSpec((1, 1, tn), lambda l, j: (l, 0, j)),
        ],
        out_specs=pl.BlockSpec((1, 8, tn), lambda l, j: (l, 0, j)),
        out_shape=jax.ShapeDtypeStruct((depth, 8, n), F32),
        compiler_params=_cparams(("parallel", "parallel")),
        name="ada",
    )(cond8, ada_w, ada_b.reshape(depth, 1, n))


COL_CHUNK = 512


def _proj_body(x_ref, g_ref, sc_ref, sh_ref, w_ref, *o_refs, splits):
    h = _modnorm(x_ref[0], g_ref[...], sc_ref[0], sh_ref[0]).astype(BF16)
    mm = lambda c0, c1: jnp.dot(h, w_ref[:, c0:c1], preferred_element_type=F32)
    off = 0
    for o_ref, n in zip(o_refs, splits):
        if isinstance(n, tuple):
            groups, width = n
            y = mm(off, off + groups * width).astype(o_ref.dtype)
            for gi in range(groups):
                o_ref[0, gi] = y[:, gi * width:(gi + 1) * width]
            off += groups * width
            continue
        for c0 in range(0, n, COL_CHUNK):
            c1 = min(n, c0 + COL_CHUNK)
            o_ref[0, :, c0:c1] = mm(off + c0, off + c1).astype(o_ref.dtype)
        off += n


def _proj(x, g, sc1, sh, w, splits, tm):
    b, t, d = x.shape
    tm = min(tm, t)
    n = w.shape[1]
    vec = pl.BlockSpec((1, 1, d), lambda bi, i: (bi, 0, 0))
    specs, shapes = [], []
    for s in splits:
        if isinstance(s, tuple):
            specs.append(pl.BlockSpec((1, s[0], tm, s[1]), lambda bi, i: (bi, 0, i, 0)))
            shapes.append(jax.ShapeDtypeStruct((b, s[0], t, s[1]), ACT))
        else:
            specs.append(pl.BlockSpec((1, tm, s), lambda bi, i: (bi, i, 0)))
            shapes.append(jax.ShapeDtypeStruct((b, t, s), ACT))
    return pl.pallas_call(
        functools.partial(_proj_body, splits=splits),
        grid=(b, t // tm),
        in_specs=[
            pl.BlockSpec((1, tm, d), lambda bi, i: (bi, i, 0)),
            pl.BlockSpec((1, d), lambda bi, i: (0, 0)),
            vec, vec,
            pl.BlockSpec((d, n), lambda bi, i: (0, 0)),
        ],
        out_specs=specs,
        out_shape=shapes,
        compiler_params=_cparams(("parallel", "parallel")),
        name="proj",
    )(x, g, sc1, sh, w)


HALO = 16


def _proj_shift_body(x_ref, xp_ref, xn_ref, g_ref, sc_ref, sh_ref, w_ref, sw_ref, z_ref, gate_ref,
                     *, tm, n_conv, segs):
    i = pl.program_id(1)
    last = pl.num_programs(1) - 1
    g, sc1, sh = g_ref[...], sc_ref[0], sh_ref[0]
    h = _modnorm(x_ref[0], g, sc1, sh)
    hp = _modnorm(xp_ref[0], g, sc1, sh) * (i > 0).astype(F32)
    hn = _modnorm(xn_ref[0], g, sc1, sh) * (i < last).astype(F32)
    hb = jnp.concatenate([hp, h, hn], axis=0).astype(BF16)
    rows = tm + 2 * HALO
    dst = 0
    for src, width in segs:
        for c0 in range(0, width, COL_CHUNK):
            cw = min(COL_CHUNK, width - c0)
            cols = slice(src + c0, src + c0 + cw)
            u = jnp.dot(hb, w_ref[:, cols], preferred_element_type=F32)
            up = pltpu.roll(u, 1, 0)[HALO:HALO + tm]
            un = pltpu.roll(u, rows - 1, 0)[HALO:HALO + tm]
            um = u[HALO:HALO + tm]
            z_ref[0, :, dst + c0:dst + c0 + cw] = (
                sw_ref[0:1, cols] * up + sw_ref[1:2, cols] * um
                + sw_ref[2:3, cols] * un).astype(z_ref.dtype)
        dst += width
    hc = hb[HALO:HALO + tm]
    n_all = w_ref.shape[1]
    for c0 in range(n_conv, n_all, COL_CHUNK):
        c1 = min(n_all, c0 + COL_CHUNK)
        gate_ref[0, :, c0 - n_conv:c1 - n_conv] = jnp.dot(
            hc, w_ref[:, c0:c1], preferred_element_type=F32).astype(gate_ref.dtype)


def _proj_shift(x, g, sc1, sh, w, sw, n_conv, segs, tm):
    b, t, d = x.shape
    tm = min(tm, t)
    n = w.shape[1]
    hb = tm // HALO
    nhb = t // HALO
    vec = pl.BlockSpec((1, 1, d), lambda bi, i: (bi, 0, 0))
    return pl.pallas_call(
        functools.partial(_proj_shift_body, tm=tm, n_conv=n_conv, segs=segs),
        grid=(b, t // tm),
        in_specs=[
            pl.BlockSpec((1, tm, d), lambda bi, i: (bi, i, 0)),
            pl.BlockSpec((1, HALO, d), lambda bi, i: (bi, jnp.maximum(i * hb - 1, 0), 0)),
            pl.BlockSpec((1, HALO, d), lambda bi, i: (bi, jnp.minimum((i + 1) * hb, nhb - 1), 0)),
            pl.BlockSpec((1, d), lambda bi, i: (0, 0)),
            vec, vec,
            pl.BlockSpec((d, n), lambda bi, i: (0, 0)),
            pl.BlockSpec((3, n_conv), lambda bi, i: (0, 0)),
        ],
        out_specs=[pl.BlockSpec((1, tm, n_conv), lambda bi, i: (bi, i, 0)),
                   pl.BlockSpec((1, tm, n - n_conv), lambda bi, i: (bi, i, 0))],
        out_shape=[jax.ShapeDtypeStruct((b, t, n_conv), ACT),
                   jax.ShapeDtypeStruct((b, t, n - n_conv), ACT)],
        compiler_params=_cparams(("parallel", "parallel")),
        name="proj_shift",
    )(x, x, x, g, sc1, sh, w, sw)


def _rms(x, g):
    return x * lax.rsqrt(jnp.mean(x * x, axis=-1, keepdims=True) + NORM_EPS) * g


def _qkv_body(ual_ref, uac_ref, kvn_ref, qn_ref, wk_ref, wv_ref, wq_ref, wqr_ref, kg_ref, kb_ref,
              qb_ref, cosk_ref, cosq_ref, sa_ref, sb_ref, sinq_ref, q_ref, k_ref, v_ref, *, n_lat):
    ua = jnp.where(pl.program_id(1) < n_lat, ual_ref[0], uac_ref[0]).astype(F32)
    ckv = _rms(ua[:, :KV_LORA], kvn_ref[...]).astype(BF16)
    kr = ua[:, KV_LORA:KV_LORA + LANE]
    cq = _rms(ua[:, KV_LORA + LANE:], qn_ref[...]).astype(BF16)
    kn = jnp.dot(ckv, wk_ref[...], preferred_element_type=F32)
    vv = jnp.dot(ckv, wv_ref[...], preferred_element_type=F32)
    qq = jnp.dot(cq, wq_ref[...], preferred_element_type=F32)
    qr = jnp.dot(cq, wqr_ref[...], preferred_element_type=F32)
    ones_hi = (_iota2((1, HEAD_SLOT), 1) >= V_HEAD).astype(F32)
    pe = pltpu.roll(kr, QK_NOPE, 1)
    gp = pe * kg_ref[...]
    pe_rot = (pltpu.roll(gp, LANE - ROPE_FREQS, 1) * sa_ref[...]
              + pltpu.roll(gp, ROPE_FREQS, 1) * sb_ref[...])
    cosk, cosq, sinq = cosk_ref[...], cosq_ref[...], sinq_ref[...]
    inv_n = 1.0 / QK_HEAD
    scale = lambda x: lax.rsqrt(jnp.sum(x * x, axis=-1, keepdims=True) * inv_n + NORM_EPS)

    for h in range(MLA_HEADS):
        sl = slice(h * HEAD_SLOT, (h + 1) * HEAD_SLOT)
        kh = kn[:, sl] + pe
        k_ref[0, h] = (scale(kh) * (kh * cosk + pe_rot) + kb_ref[...]).astype(BF16)
        qh = qq[:, sl]
        q_ref[0, h] = (scale(qh) * (qh * cosq + qr[:, sl] * sinq) + qb_ref[...]).astype(BF16)
        v_ref[0, h] = (vv[:, sl] + ones_hi).astype(BF16)


def _qkv(ua_l, ua_c, kvn, qn, wk, wv, wq, wqr, kg, kb, qb, tabs, tm):
    b, s, wa = ua_l.shape
    tc = ua_c.shape[1]
    nl, ncx = s // tm, tc // tm
    full = lambda a: pl.BlockSpec(a.shape, lambda bi, i: (0,) * a.ndim)
    tab = pl.BlockSpec((tm, LANE), lambda bi, i: (i, 0))
    head = pl.BlockSpec((1, MLA_HEADS, tm, HEAD_SLOT), lambda bi, i: (bi, 0, i, 0))
    shape = jax.ShapeDtypeStruct((b, MLA_HEADS, s + tc, HEAD_SLOT), BF16)
    return pl.pallas_call(
        functools.partial(_qkv_body, n_lat=nl),
        grid=(b, nl + ncx),
        in_specs=[pl.BlockSpec((1, tm, wa), lambda bi, i: (bi, jnp.minimum(i, nl - 1), 0)),
                  pl.BlockSpec((1, tm, wa), lambda bi, i: (bi, jnp.maximum(i - nl, 0), 0)),
                  full(kvn), full(qn), full(wk), full(wv), full(wq), full(wqr), full(kg),
                  full(kb), full(qb)] + [tab] * len(tabs),
        out_specs=[head, head, head],
        out_shape=[shape, shape, shape],
        compiler_params=_cparams(("parallel", "parallel")),
        name="qkv",
    )(ua_l, ua_c, kvn, qn, wk, wv, wq, wqr, kg, kb, qb, *tabs)


BIAS_LANE = QK_HEAD
MAX_STATIC_BOUND = 50.0


def _attn_finish(acc_ref, o_ref):
    bq = acc_ref.shape[1]
    lane = _iota2((bq, LANE), 1)
    o0 = acc_ref[0] / pltpu.roll(acc_ref[0], V_HEAD, 1)
    o1 = acc_ref[1] / pltpu.roll(acc_ref[1], V_HEAD, 1)
    o_ref[0] = jnp.where(lane < V_HEAD, o0, pltpu.roll(o1, V_HEAD, 1)).astype(o_ref.dtype)


def _attn_static_body(q_ref, k_ref, v_ref, o_ref, acc_ref, *, bk):
    acc_ref[...] = jnp.zeros(acc_ref.shape, F32)

    def step(j, carry):
        rows = pl.ds(pl.multiple_of(j * bk, bk), bk)
        s = [lax.dot_general(q_ref[0, hh], k_ref[0, hh, rows, :], NT, preferred_element_type=F32)
             for hh in range(2)]
        p = [jnp.exp2(x.astype(BF16)) for x in s]
        for hh in range(2):
            acc_ref[hh] += jnp.dot(p[hh], v_ref[0, hh, rows, :], preferred_element_type=F32)
        return carry

    lax.fori_loop(0, k_ref.shape[2] // bk, step, 0)
    _attn_finish(acc_ref, o_ref)


def _attn_online_body(q_ref, k_ref, v_ref, o_ref, acc_ref, m_ref, *, bk):
    acc_ref[...] = jnp.zeros(acc_ref.shape, F32)
    m_ref[...] = jnp.full(m_ref.shape, -jnp.inf, F32)

    def step(j, carry):
        rows = pl.ds(pl.multiple_of(j * bk, bk), bk)
        for hh in range(2):
            s = lax.dot_general(q_ref[0, hh], k_ref[0, hh, rows, :], NT,
                                preferred_element_type=F32)
            m_prev = m_ref[hh]
            m_new = jnp.maximum(m_prev, jnp.max(s, axis=-1, keepdims=True))
            p = jnp.exp2(s - m_new)
            acc_ref[hh] = (jnp.exp2(m_prev - m_new) * acc_ref[hh]
                           + jnp.dot(p.astype(BF16), v_ref[0, hh, rows, :],
                                     preferred_element_type=F32))
            m_ref[hh] = m_new
        return carry

    lax.fori_loop(0, k_ref.shape[2] // bk, step, 0)
    _attn_finish(acc_ref, o_ref)


def _attention(q, k, v, static_ok, q_start, q_rows, k_start, k_rows, bq, bk):
    b, h, _, e = q.shape
    bq, bk = min(bq, q_rows), min(bk, k_rows)
    qi0, kj0 = q_start // bq, k_start // k_rows
    kv_blk = pl.BlockSpec((1, 2, k_rows, e), lambda bi, p, i: (bi, p, kj0, 0))

    def call(online):
        scratch = [pltpu.VMEM((2, bq, LANE), F32)]
        if online:
            scratch.append(pltpu.VMEM((2, bq, 1), F32))
        return pl.pallas_call(
            functools.partial(_attn_online_body if online else _attn_static_body, bk=bk),
            grid=(b, h // 2, q_rows // bq),
            in_specs=[pl.BlockSpec((1, 2, bq, e), lambda bi, p, i: (bi, p, qi0 + i, 0)),
                      kv_blk, kv_blk],
            out_specs=pl.BlockSpec((1, bq, 2 * V_HEAD), lambda bi, p, i: (bi, i, p)),
            out_shape=jax.ShapeDtypeStruct((b, q_rows, h * V_HEAD), ACT),
            scratch_shapes=scratch,
            compiler_params=_cparams(("parallel", "parallel", "arbitrary")),
            name="attention_online" if online else "attention",
        )(q, k, v)

    return lax.cond(static_ok, lambda: call(False), lambda: call(True))


def _dft_mats(n):
    idx = np.arange(n)
    ang = 2.0 * np.pi * ((idx[:, None] * idx[None, :]) % n) / n
    return np.cos(ang), np.sin(ang)


def _hilo(a):
    a = jnp.asarray(a, F32)
    hi = a.astype(BF16)
    return hi, (a - hi.astype(F32)).astype(BF16)


def _mm3c(ah, al, b, dn=NN):
    bh, bl = _split(b)
    d = lambda x, y: lax.dot_general(x, y, dn, preferred_element_type=F32)
    return d(ah, bh) + d(al, bh) + d(ah, bl)


def _four_rows_body(x_ref, w_ref, tc_ref, ts_ref, o_ref):
    r = tc_ref.shape[0]
    y = jnp.dot(w_ref[...], x_ref[0, 0], preferred_element_type=F32)
    yc, ys = y[:r], y[r:]
    tc, ts = tc_ref[...], ts_ref[...]
    o_ref[0, 0, :r] = (yc * tc - ys * ts).astype(o_ref.dtype)
    o_ref[0, 0, r:] = (yc * ts + ys * tc).astype(o_ref.dtype)


def _four_cols_body(y_ref, w_ref, cs_ref, wf_ref, o_ref, y3_scr, *, krt, scale):
    def one(j, carry):
        rows = pl.ds(pl.multiple_of(j * GRID_W, GRID_W), GRID_W)
        ycs = jnp.concatenate([y_ref[0, 0, 0, rows, :], y_ref[0, 0, 1, rows, :]], axis=0)
        y3 = jnp.dot(w_ref[...], ycs, preferred_element_type=F32)
        y3_scr[rows, :] = jnp.concatenate([y3[:GRID_W], y3[GRID_W:]], axis=1).astype(BF16)
        return carry

    lax.fori_loop(0, krt, one, 0, unroll=8)
    f = jnp.dot(y3_scr[...], cs_ref[...], preferred_element_type=F32) * scale
    o_ref[0, 0] = _mm(f, wf_ref[0]).astype(o_ref.dtype)


def _fourier_latent(xf, w_fnet):
    b, g, t, gd = xf.shape
    r = t // GRID_W
    wide = GRID_W * gd
    xv = xf.reshape(b, g, r, wide)
    cr, sr = _dft_mats(r)
    w_rows = jnp.asarray(np.concatenate([cr, sr], axis=0), BF16)
    kr_i, c_i = np.arange(r)[:, None], np.arange(GRID_W)[None, :]
    ang = 2.0 * np.pi * ((kr_i * c_i) % t) / t
    twc = jnp.repeat(jnp.asarray(np.cos(ang), F32), gd, axis=1)
    tws = jnp.repeat(jnp.asarray(np.sin(ang), F32), gd, axis=1)
    tl = min(FOUR_LANE_TILE, wide)
    y2 = pl.pallas_call(
        _four_rows_body,
        grid=(b, g, wide // tl),
        in_specs=[pl.BlockSpec((1, 1, r, tl), lambda bi, gi, l: (bi, gi, 0, l)),
                  pl.BlockSpec((2 * r, r), lambda bi, gi, l: (0, 0)),
                  pl.BlockSpec((r, tl), lambda bi, gi, l: (0, l)),
                  pl.BlockSpec((r, tl), lambda bi, gi, l: (0, l))],
        out_specs=pl.BlockSpec((1, 1, 2 * r, tl), lambda bi, gi, l: (bi, gi, 0, l)),
        out_shape=jax.ShapeDtypeStruct((b, g, 2 * r, wide), ACT),
        compiler_params=_cparams(("parallel", "parallel", "parallel")),
        name="fourier_rows",
    )(xv, w_rows, twc, tws)
    y2v = y2.reshape(b, g, 2, r * GRID_W, gd)
    c64, s64 = _dft_mats(GRID_W)
    w_cols = jnp.asarray(np.block([[c64, -s64], [s64, c64]]), BF16)
    cc, sc = _dft_mats(gd)
    w_chan = jnp.asarray(np.concatenate([cc, -sc], axis=0), BF16)
    krt = min(FOUR_ROWS_TILE, r)
    const = lambda a: pl.BlockSpec(a.shape, lambda bi, gi, i: (0, 0))
    fo = pl.pallas_call(
        functools.partial(_four_cols_body, krt=krt, scale=1.0 / math.sqrt(t * gd)),
        grid=(b, g, r // krt),
        in_specs=[pl.BlockSpec((1, 1, 2, krt * GRID_W, gd), lambda bi, gi, i: (bi, gi, 0, i, 0)),
                  const(w_cols), const(w_chan),
                  pl.BlockSpec((1, gd, gd), lambda bi, gi, i: (gi, 0, 0))],
        out_specs=pl.BlockSpec((1, 1, krt * GRID_W, gd), lambda bi, gi, i: (bi, gi, i, 0)),
        out_shape=jax.ShapeDtypeStruct((b, g, r * GRID_W, gd), ACT),
        scratch_shapes=[pltpu.VMEM((krt * GRID_W, 2 * gd), BF16)],
        compiler_params=_cparams(("parallel", "parallel", "parallel")),
        name="fourier_cols",
    )(y2v, w_cols, w_chan, w_fnet)
    return fo.reshape(b, g, r, GRID_W, gd).transpose(0, 3, 2, 1, 4).reshape(b, t, g * gd)


def _four_dense_body(x_ref, ch_ref, cl_ref, th_ref, tl_ref, sh_ref, sl_ref, wf_ref, o_ref, *, scale):
    x = x_ref[0, 0]
    xh, xl = _split(x)
    d = lambda a, b: jnp.dot(a, b, preferred_element_type=F32)
    z = d(xh, ch_ref[...]) + d(xl, ch_ref[...]) + d(xh, cl_ref[...])
    zc, zs = z[:, :FNET_GROUP_DIM], z[:, FNET_GROUP_DIM:]
    f = (_mm3c(th_ref[...], tl_ref[...], zc) - _mm3c(sh_ref[...], sl_ref[...], zs)) * scale
    o_ref[0] = _mm3(f, wf_ref[0]).astype(o_ref.dtype)


def _fourier_dense(xf, w_fnet):
    b, g, t, gd = xf.shape
    cc, sc = _dft_mats(gd)
    ch, cl = _hilo(np.concatenate([cc, sc], axis=1))
    ct, st = _dft_mats(t)
    cth, ctl = _hilo(ct)
    sth, stl = _hilo(st)
    sq = pl.BlockSpec((t, t), lambda bi, gi: (0, 0))
    cs = pl.BlockSpec((gd, 2 * gd), lambda bi, gi: (0, 0))
    return pl.pallas_call(
        functools.partial(_four_dense_body, scale=1.0 / math.sqrt(t * gd)),
        grid=(b, g),
        in_specs=[pl.BlockSpec((1, 1, t, gd), lambda bi, gi: (bi, gi, 0, 0)), cs, cs, sq, sq, sq, sq,
                  pl.BlockSpec((1, gd, gd), lambda bi, gi: (gi, 0, 0))],
        out_specs=pl.BlockSpec((1, t, gd), lambda bi, gi: (bi, 0, gi)),
        out_shape=jax.ShapeDtypeStruct((b, t, g * gd), ACT),
        compiler_params=_cparams(("parallel", "parallel")),
        name="fourier_dense",
    )(xf, ch, cl, cth, ctl, sth, stl, w_fnet)


def _merge_body(o_ref, f_ref, gate_ref, x_ref, gl_ref, w_ref, out_ref):
    gt = gate_ref[0].astype(F32)
    mix = jnp.concatenate([o_ref[0], f_ref[0]], axis=-1).astype(F32) * (gt * _sigmoid(gt))
    y = jnp.dot(mix.astype(BF16), w_ref[...], preferred_element_type=F32)
    out_ref[0] = x_ref[0] + gl_ref[0] * y


def _merge(o, f, gate, x, gl, w, tm):
    b, t, d = x.shape
    tm = min(tm, t)
    half = o.shape[2]
    tok = lambda n: pl.BlockSpec((1, tm, n), lambda bi, i: (bi, i, 0))
    return pl.pallas_call(
        _merge_body,
        grid=(b, t // tm),
        in_specs=[tok(half), tok(half), tok(d), tok(d),
                  pl.BlockSpec((1, 1, d), lambda bi, i: (bi, 0, 0)),
                  pl.BlockSpec(w.shape, lambda bi, i: (0, 0))],
        out_specs=tok(d),
        out_shape=jax.ShapeDtypeStruct((b, t, d), F32),
        compiler_params=_cparams(("parallel", "parallel")),
        name="merge",
    )(o, f, gate, x, gl, w)


EXP_M05 = math.exp(-0.5)


PAIRS_PER_STEP = 8
PAIRS_PER_GROUP = 8
LOCAL_CHUNKS = 4
GROUP_LAG = 5


def _rwkv_local_body(*refs, cs):
    zwa_ref, w0_ref, w2_ref, a0_ref, a2_ref = refs[3:8]
    zwa = zwa_ref[0].astype(F32)
    lora = (_mm(jnp.tanh(zwa[:, :LANE]), w2_ref[...]) + w0_ref[...],
            _mm(zwa[:, LANE:], a2_ref[...]) + a0_ref[...])
    groups = [_rwkv_local_group(*refs, lora=lora, base=base, ck=ck) for ck in range(cs)
              for base in range(0, PAIRS_PER_STEP, PAIRS_PER_GROUP)]
    tick = 0
    while groups:
        live = groups[:tick // GROUP_LAG + 1]
        for g in live:
            if next(g, StopIteration) is StopIteration:
                groups.remove(g)
        tick += 1


def _rwkv_local_group(zk_ref, zv_ref, zr_ref, zwa_ref, w0_ref, w2_ref, a0_ref, a2_ref,
                      kk_ref, ka_ref, rk_ref, m_ref, g_ref, qt_ref, yl_ref, bn_ref,
                      *, lora, base, ck):
    c = CHUNK
    rows = slice(ck * CHUNK, (ck + 1) * CHUNK)

    head0 = _iota2((1, LANE), 1) < RWKV_HEAD
    r2 = _iota2((LANE, LANE), 0)
    c2 = _iota2((LANE, LANE), 1)
    same = (r2 // RWKV_HEAD) == (c2 // RWKV_HEAD)
    ones_bd = same.astype(F32)
    eye = r2 == c2

    def stack(x):
        z = jnp.zeros_like(x)
        return jnp.concatenate([jnp.where(head0, x, z), jnp.where(head0, z, x)], axis=0)

    stack_b = lambda x: stack(x.astype(BF16))
    fold = lambda x: x[:c] + x[c:]

    pairs = range(PAIRS_PER_GROUP)
    chains = [(q, d) for q in pairs for d in range(2)]
    qls = [slice((base + q) * LANE, (base + q + 1) * LANE) for q in pairs]
    ks = [zk_ref[0, rows, ql].astype(F32) for ql in qls]
    vs_ = [zv_ref[0, rows, ql].astype(F32) for ql in qls]
    rs = [zr_ref[0, rows, ql].astype(F32) for ql in qls]
    pcols = [slice((base + q) * 2 * LANE, (base + q + 1) * 2 * LANE) for q in pairs]
    wraw = [lora[0][rows, pc] for pc in pcols]
    araw = [lora[1][rows, pc] for pc in pcols]
    yield
    logw = [-EXP_M05 * _sigmoid(w) for w in wraw]
    a_all = [_sigmoid(a) for a in araw]
    kk0 = [ks[q] * kk_ref[:, qls[q]] for q in pairs]
    ss = [_mm2r(x * x, ones_bd) for x in kk0]
    yield
    kk = [kk0[q] / jnp.maximum(jnp.sqrt(ss[q]), 1e-12) for q in pairs]
    vstk = [stack_b(v) for v in vs_]

    dsl = [slice(d * LANE, (d + 1) * LANE) for d in range(2)]
    lw = [logw[q][:, dsl[d]] for q, d in chains]
    ad = [a_all[q][:, dsl[d]] for q, d in chains]
    kd = [ks[q] * (1.0 + (ad[i] - 1.0) * ka_ref[:, qls[q]]) for i, (q, d) in enumerate(chains)]
    bb = [kk[q] * ad[i] for i, (q, d) in enumerate(chains)]
    bonus = [_mm2r(rs[q] * kd[i] * rk_ref[:, qls[q]], ones_bd) * vs_[q]
             for i, (q, d) in enumerate(chains)]
    yield
    r3 = _iota2((c, 3 * c), 0)
    c3 = _iota2((c, 3 * c), 1) & (c - 1)
    tri3 = [(c3 <= r3).astype(BF16), (c3 >= r3).astype(BF16)]
    tt = _iota2((c, LANE), 0)
    ts = _iota2((c, LANE), 1) & (c - 1)
    strict = [ts < tt, ts > tt]
    incl = [ts <= tt, ts >= tt]
    eye_c = (ts == tt).astype(F32)

    def cumsum(x, tri):
        xh, xl = _split(x)
        xll = (x - xh.astype(F32) - xl.astype(F32)).astype(BF16)
        return jnp.dot(tri, jnp.concatenate([xh, xl, xll], axis=0), preferred_element_type=F32)

    lc = [cumsum(lw[i], tri3[d]) for i, (q, d) in enumerate(chains)]
    ltot = [lc[i][c - 1:c] if d == 0 else lc[i][0:1] for i, (q, d) in enumerate(chains)]
    yield
    n = len(chains)
    bdot = lambda a, b: jnp.dot(a, b, preferred_element_type=F32)
    kkd = [(kk[q] * jnp.exp(lc[i] - lw[i])).astype(BF16) for i, (q, d) in enumerate(chains)]
    rd = [rs[q] * jnp.exp(lc[i]) for i, (q, d) in enumerate(chains)]
    e_inv = [jnp.exp(-x) for x in lc]
    inv_s = [jnp.concatenate([stack_b(bb[i] * e_inv[i]), stack_b(kd[i] * e_inv[i])], axis=0)
             for i in range(n)]
    yield
    amat = [lax.dot_general(jnp.concatenate([kkd[i], rd[i].astype(BF16)], axis=0), inv_s[i],
                            NT, preferred_element_type=F32).astype(BF16) for i in range(n)]
    zero_c = jnp.zeros((c, LANE), BF16)
    a_kb = [jnp.where(strict[d], amat[i][:c, :LANE], zero_c) for i, (q, d) in enumerate(chains)]
    a_kk = [jnp.where(strict[d], amat[i][:c, LANE:], zero_c) for i, (q, d) in enumerate(chains)]
    aq_b = [jnp.where(incl[d], amat[i][c:, :LANE], zero_c) for i, (q, d) in enumerate(chains)]
    aq_k = [jnp.where(incl[d], amat[i][c:, LANE:], zero_c) for i, (q, d) in enumerate(chains)]
    yield
    av = [bdot(jnp.concatenate([a_kk[i], aq_k[i]], axis=0), vstk[q])
          for i, (q, d) in enumerate(chains)]
    yield
    tinv = [eye_c - a.astype(F32) for a in a_kb]
    qpow = [bdot(a, stack(a)).astype(BF16) for a in a_kb]
    yield
    for _ in range(4):
        prod = [bdot(qpow[i], jnp.concatenate([stack(qpow[i]), stack_b(tinv[i])], axis=1))
                for i in range(n)]
        qpow = [x[:, :LANE].astype(BF16) for x in prod]
        tinv = [tinv[i] + prod[i][:, LANE:] for i in range(n)]
        yield
    tinv = [tinv[i] + bdot(qpow[i], stack_b(tinv[i])) for i in range(n)]
    yield
    tsplit = [_split(t) for t in tinv]
    ia_t = [bdot((eye_c + a_kb[i].astype(F32)).astype(BF16),
                 jnp.concatenate([stack(tsplit[i][0]), stack(tsplit[i][1])], axis=1))
            for i in range(n)]
    resid = [eye_c - ia_t[i][:, :LANE] - ia_t[i][:, LANE:] for i in range(n)]
    yield
    tinv = [tinv[i] + bdot(tinv[i].astype(BF16), stack_b(resid[i])) for i in range(n)]
    yield
    x = [bdot(tinv[i].astype(BF16),
              jnp.concatenate([stack(kkd[i]), stack_b(av[i][:c])], axis=1))
         for i in range(n)]
    xb = [v.astype(BF16) for v in x]
    yield
    qy = [jnp.concatenate([rd[i], av[i][c:]], axis=1)
          - bdot(aq_b[i], jnp.concatenate([stack(xb[i][:, :LANE]), stack(xb[i][:, LANE:])],
                                          axis=1)) for i in range(n)]
    yield
    e_end = [jnp.exp(ltot[i] - lc[i]) for i in range(n)]
    ends = [jnp.concatenate([(-bb[i] * e_end[i]).astype(BF16), (kd[i] * e_end[i]).astype(BF16)],
                            axis=0) for i in range(n)]
    wuv = [jnp.concatenate([xb[i], jnp.concatenate([zero_c, vs_[q].astype(BF16)], axis=1)],
                           axis=0) for i, (q, d) in enumerate(chains)]
    mg = [lax.dot_general(ends[i], wuv[i], TN, preferred_element_type=F32) for i in range(n)]
    yield
    for i, (q, d) in enumerate(chains):
        m_ref[0, ck, d, base + q] = fold(jnp.where(eye, jnp.exp(ltot[i]), 0.0)
                                         + jnp.where(same, mg[i][:, :LANE], 0.0)).astype(BF16)
        g_ref[0, ck, d, base + q] = fold(jnp.where(same, mg[i][:, LANE:], 0.0)
                                         ).astype(g_ref.dtype)
        qt_ref[0, d, rows, qls[q]] = qy[i][:, :LANE].astype(BF16)
    for q in pairs:
        yl_ref[0, rows, qls[q]] = (qy[2 * q][:, LANE:]
                                   + qy[2 * q + 1][:, LANE:]).astype(yl_ref.dtype)
        bn_ref[0, rows, qls[q]] = (bonus[2 * q] + bonus[2 * q + 1]).astype(bn_ref.dtype)


def _rwkv_local(z, w0p, w2p, a0p, a2p, k_k, k_a, r_k):
    b, t, _ = z.shape
    w = k_k.shape[1]
    npair = w // LANE
    pp = PAIRS_PER_STEP
    ng = npair // pp
    wl = pp * LANE
    nc = t // CHUNK
    cs = LOCAL_CHUNKS if nc % LOCAL_CHUNKS == 0 else 1
    rows = cs * CHUNK
    tokc = lambda base: pl.BlockSpec((1, rows, wl), lambda bi, ci, p: (bi, ci, base + p))
    perp3 = lambda n: pl.BlockSpec((n, pp * 2 * LANE), lambda bi, ci, p: (0, p))
    vecp = pl.BlockSpec((1, wl), lambda bi, ci, p: (0, p))
    mat = pl.BlockSpec((1, cs, 2, pp, CHUNK, LANE), lambda bi, ci, p: (bi, ci, 0, p, 0, 0))
    return pl.pallas_call(
        functools.partial(_rwkv_local_body, cs=cs),
        grid=(b, nc // cs, ng),
        in_specs=[tokc(0), tokc(ng), tokc(2 * ng),
                  pl.BlockSpec((1, rows, 2 * LANE), lambda bi, ci, p: (bi, ci, 3 * npair // 2)),
                  perp3(1), perp3(LANE), perp3(1), perp3(LANE), vecp, vecp, vecp],
        out_specs=[mat, mat,
                   pl.BlockSpec((1, 2, rows, wl), lambda bi, ci, p: (bi, 0, ci, p)),
                   pl.BlockSpec((1, rows, wl), lambda bi, ci, p: (bi, ci, p)),
                   pl.BlockSpec((1, rows, wl), lambda bi, ci, p: (bi, ci, p))],
        out_shape=[jax.ShapeDtypeStruct((b, nc, 2, npair, CHUNK, LANE), BF16),
                   jax.ShapeDtypeStruct((b, nc, 2, npair, CHUNK, LANE), ACT),
                   jax.ShapeDtypeStruct((b, 2, t, w), BF16),
                   jax.ShapeDtypeStruct((b, t, w), ACT),
                   jax.ShapeDtypeStruct((b, t, w), ACT)],
        compiler_params=_cparams(("parallel", "parallel", "parallel")),
        name="rwkv_local",
    )(z, z, z, z, w0p, w2p, a0p, a2p, k_k, k_a, r_k)


SCAN_CHUNKS = 8


def _rwkv_scan_body(m0_ref, g0_ref, q0_ref, m1_ref, g1_ref, q1_ref, h0_ref,
                    y0_ref, y1_ref, hfin_ref, h_scr, *, npair, cs):
    ci = pl.program_id(1)

    @pl.when(ci == 0)
    def _():
        h_scr[...] = h0_ref[0]

    head0 = _iota2((1, LANE), 1) < RWKV_HEAD

    def expand(x):
        z = jnp.zeros_like(x)
        return jnp.concatenate([jnp.where(head0, x, z), jnp.where(head0, z, x)], axis=0)

    refs = ((m0_ref, g0_ref, q0_ref, y0_ref), (m1_ref, g1_ref, q1_ref, y1_ref))
    chains = [(d, p) for d in range(2) for p in range(npair)]
    lanes = [slice(p * LANE, (p + 1) * LANE) for p in range(npair)]
    h = [h_scr[d, p] for d, p in chains]
    for step in range(cs):
        ck = (step, cs - 1 - step)
        rows = [slice(c * CHUNK, (c + 1) * CHUNK) for c in ck]
        hb = [x.astype(BF16) for x in h]
        res = [jnp.dot(jnp.concatenate([refs[d][2][0, 0, rows[d], lanes[p]],
                                        refs[d][0][0, ck[d], 0, p]], axis=0), hb[i],
                       preferred_element_type=F32) for i, (d, p) in enumerate(chains)]
        for i, (d, p) in enumerate(chains):
            refs[d][3][0, rows[d], lanes[p]] = res[i][:CHUNK].astype(refs[d][3].dtype)
        h = [expand(res[i][CHUNK:] + refs[d][1][0, ck[d], 0, p].astype(F32))
             for i, (d, p) in enumerate(chains)]
    for i, (d, p) in enumerate(chains):
        h_scr[d, p] = h[i]

    @pl.when(ci == pl.num_programs(1) - 1)
    def _():
        hfin_ref[0] = h_scr[...]


def _rwkv_scan(mm, gg, qt, h0):
    b, nc, _, npair, _, _ = mm.shape
    t, w = qt.shape[2], qt.shape[3]
    cs = SCAN_CHUNKS if nc % SCAN_CHUNKS == 0 else 1
    nb = nc // cs
    fwd = lambda bi, ci: (bi, ci, 0, 0, 0, 0)
    rev = lambda bi, ci: (bi, nb - 1 - ci, 1, 0, 0, 0)
    mblk = (1, cs, 1, npair, CHUNK, LANE)
    hspec = pl.BlockSpec((1, 2, npair, LANE, LANE), lambda bi, ci: (bi, 0, 0, 0, 0))
    return pl.pallas_call(
        functools.partial(_rwkv_scan_body, npair=npair, cs=cs),
        grid=(b, nb),
        in_specs=[pl.BlockSpec(mblk, fwd), pl.BlockSpec(mblk, fwd),
                  pl.BlockSpec((1, 1, cs * CHUNK, w), lambda bi, ci: (bi, 0, ci, 0)),
                  pl.BlockSpec(mblk, rev), pl.BlockSpec(mblk, rev),
                  pl.BlockSpec((1, 1, cs * CHUNK, w), lambda bi, ci: (bi, 1, nb - 1 - ci, 0)),
                  hspec],
        out_specs=[pl.BlockSpec((1, cs * CHUNK, w), lambda bi, ci: (bi, ci, 0)),
                   pl.BlockSpec((1, cs * CHUNK, w), lambda bi, ci: (bi, nb - 1 - ci, 0)),
                   hspec],
        out_shape=[jax.ShapeDtypeStruct((b, t, w), ACT), jax.ShapeDtypeStruct((b, t, w), ACT),
                   jax.ShapeDtypeStruct(h0.shape, F32)],
        scratch_shapes=[pltpu.VMEM((2, npair, LANE, LANE), F32)],
        compiler_params=_cparams(("parallel", "arbitrary")),
        name="rwkv_scan",
    )(mm, gg, qt, mm, gg, qt, h0)


def _rwkv_out_body(y0_ref, y1_ref, yl_ref, bn_ref, gate_ref, x_ref, gl_ref, gnw_ref, gnb_ref,
                   w_ref, o_ref):
    y = y0_ref[0].astype(F32) + y1_ref[0].astype(F32) + yl_ref[0].astype(F32)
    r2 = _iota2((LANE, LANE), 0)
    c2 = _iota2((LANE, LANE), 1)
    avg = ((r2 // RWKV_HEAD) == (c2 // RWKV_HEAD)).astype(F32) * (1.0 / RWKV_HEAD)
    parts = []
    for p in range(y.shape[1] // LANE):
        yp = y[:, p * LANE:(p + 1) * LANE]
        dl = yp - _mm2r(yp, avg)
        var = _mm2r(dl * dl, avg)
        parts.append(dl * lax.rsqrt(var + GN_EPS))
    yn = jnp.concatenate(parts, axis=1)
    gt = gate_ref[0].astype(F32)
    act = (yn * gnw_ref[...] + gnb_ref[...] + bn_ref[0].astype(F32)) * (gt * _sigmoid(gt))
    out = jnp.dot(act.astype(BF16), w_ref[...], preferred_element_type=F32)
    o_ref[0] = x_ref[0] + gl_ref[0] * out


def _rwkv_out(y0, y1, yl, bn, gate, x, gl, gnw, gnb, w, tm):
    b, t, d = x.shape
    tm = min(tm, t)
    wd = y0.shape[2]
    tok = lambda n: pl.BlockSpec((1, tm, n), lambda bi, i: (bi, i, 0))
    return pl.pallas_call(
        _rwkv_out_body,
        grid=(b, t // tm),
        in_specs=[tok(wd), tok(wd), tok(wd), tok(wd), tok(wd), tok(d),
                  pl.BlockSpec((1, 1, d), lambda bi, i: (bi, 0, 0)),
                  pl.BlockSpec((1, wd), lambda bi, i: (0, 0)),
                  pl.BlockSpec((1, wd), lambda bi, i: (0, 0)),
                  pl.BlockSpec(w.shape, lambda bi, i: (0, 0))],
        out_specs=tok(d),
        out_shape=jax.ShapeDtypeStruct((b, t, d), F32),
        compiler_params=_cparams(("parallel", "parallel")),
        name="rwkv_out",
    )(y0, y1, yl, bn, gate, x, gl, gnw, gnb, w)


def _rope_tables(t):
    rows = t // GRID_W
    row = jnp.repeat(jnp.arange(rows, dtype=F32), GRID_W)
    col = jnp.tile(jnp.arange(GRID_W, dtype=F32), rows)
    inv = 1.0 / (ROPE_BASE ** (jnp.arange(ROPE_FREQS, dtype=F32) / ROPE_FREQS))
    ang = jnp.stack([row[:, None] * inv, col[:, None] * inv], axis=1)
    cos, sin = jnp.cos(ang), jnp.sin(ang)
    zeros = jnp.zeros_like(sin)
    ones_lo = jnp.ones((t, QK_NOPE), F32)
    pad_hi = HEAD_SLOT - QK_HEAD
    cos_t = jnp.concatenate([ones_lo, jnp.concatenate([cos, cos], axis=2).reshape(t, QK_ROPE),
                             jnp.ones((t, pad_hi), F32)], axis=1)
    sa = jnp.concatenate([jnp.zeros((t, QK_NOPE), F32),
                          jnp.concatenate([-sin, zeros], axis=2).reshape(t, QK_ROPE),
                          jnp.zeros((t, pad_hi), F32)], axis=1)
    sb = jnp.concatenate([jnp.zeros((t, QK_NOPE), F32),
                          jnp.concatenate([zeros, sin], axis=2).reshape(t, QK_ROPE),
                          jnp.zeros((t, pad_hi), F32)], axis=1)
    return cos_t, sa, sb


def _even_layer(x, ctx, mod_l, mod_c, need_ctx, g, w_in, kv_norm, q_norm, w_uq, w_ukv,
                q_head_norm, k_head_norm, w_fnet, w_out):
    b, s, d = x.shape
    tc = ctx.shape[1]
    e_q0 = KV_LORA + QK_ROPE
    e_f0 = e_q0 + Q_LORA
    e_g0 = e_f0 + FNET_GROUPS * FNET_GROUP_DIM
    w_p = jnp.concatenate([w_in[:, e_g0:], w_in[:, e_f0:e_g0], w_in[:, :e_q0],
                           jnp.zeros((d, LANE - QK_ROPE), F32), w_in[:, e_q0:e_f0]],
                          axis=1).astype(BF16)
    splits = (d, (FNET_GROUPS, FNET_GROUP_DIM), KV_LORA + LANE + Q_LORA)
    kvw = w_ukv.reshape(KV_LORA, MLA_HEADS, QK_NOPE + V_HEAD)
    wk = jnp.pad(kvw[:, :, :QK_NOPE], ((0, 0), (0, 0), (0, HEAD_SLOT - QK_NOPE)))
    wk = wk.reshape(KV_LORA, MLA_HEADS * HEAD_SLOT).astype(BF16)
    wv = jnp.pad(kvw[:, :, QK_NOPE:], ((0, 0), (0, 0), (0, HEAD_SLOT - V_HEAD)))
    wv = wv.reshape(KV_LORA, MLA_HEADS * HEAD_SLOT).astype(BF16)
    wq3 = jnp.pad(w_uq.reshape(Q_LORA, MLA_HEADS, QK_HEAD), ((0, 0), (0, 0), (0, HEAD_SLOT - QK_HEAD)))
    wq = wq3.reshape(Q_LORA, MLA_HEADS * HEAD_SLOT).astype(BF16)
    kg = jnp.pad(k_head_norm, (0, HEAD_SLOT - QK_HEAD)).reshape(1, HEAD_SLOT)
    qg = (jnp.pad(q_head_norm, (0, HEAD_SLOT - QK_HEAD))
          * (QK_HEAD ** -0.5 * math.log2(math.e))).reshape(1, HEAD_SLOT)
    lane = np.arange(HEAD_SLOT)
    tail = (lane >= QK_NOPE) & (lane < QK_HEAD)
    first = tail & (((lane - QK_NOPE) // ROPE_FREQS) % 2 == 0)
    partner = np.where(first, lane + ROPE_FREQS, np.where(tail, lane - ROPE_FREQS, lane))
    sign = np.where(first, -1.0, np.where(tail, 1.0, 0.0)).astype(np.float32)
    wqr = (wq3[:, :, partner] * (sign * qg[0, partner])).reshape(Q_LORA, MLA_HEADS * HEAD_SLOT)
    wqr = wqr.astype(BF16)
    kvn, qn = kv_norm.reshape(1, -1), q_norm.reshape(1, -1)
    g2 = g.reshape(1, d)
    bound = (1.02 * QK_HEAD * jnp.max(jnp.abs(qg)) * jnp.max(jnp.abs(kg))).astype(BF16).astype(F32)
    static_ok = bound <= MAX_STATIC_BOUND
    bias_lane = (jnp.arange(HEAD_SLOT) == BIAS_LANE).astype(F32).reshape(1, HEAD_SLOT)
    kb = bias_lane * jnp.where(static_ok, -bound, 0.0)
    qb = bias_lane

    gate_l, four_l, ua_l = _proj(x, g2, mod_l[1], mod_l[0], w_p, splits, TOKEN_TILE)
    gate_c, four_c, ua_c = _proj(ctx, g2, mod_c[1], mod_c[0], w_p, splits, TOKEN_TILE)
    sk = s + tc
    cos_t, sa, sb = _rope_tables(s)
    cos_t = jnp.concatenate([cos_t, jnp.ones((tc, HEAD_SLOT), F32)], axis=0)
    sa = jnp.concatenate([sa, jnp.zeros((tc, HEAD_SLOT), F32)], axis=0)
    sb = jnp.concatenate([sb, jnp.zeros((tc, HEAD_SLOT), F32)], axis=0)
    tabs = (cos_t * kg, cos_t * qg, sa, sb, sb - sa)
    q_all, k_all, v_all = _qkv(ua_l, ua_c, kvn, qn, wk, wv, wq, wqr, kg, kb, qb, tabs,
                               math.gcd(s, tc))
    bk = ATTN_K_BLOCK if sk % ATTN_K_BLOCK == 0 else tc
    o_l = _attention(q_all, k_all, v_all, static_ok, 0, s, 0, sk, ATTN_Q_BLOCK, bk)
    f_l = _fourier_latent(four_l, w_fnet)
    wo = w_out.astype(BF16)
    x_new = _merge(o_l, f_l, gate_l, x, mod_l[2], wo, TOKEN_TILE)
    ctx_new = ctx
    if need_ctx:
        o_c = _attention(q_all, k_all, v_all, static_ok, s, tc, s, tc, tc, tc)
        f_c = _fourier_dense(four_c, w_fnet)
        ctx_new = _merge(o_c, f_c, gate_c, ctx, mod_c[2], wo, TOKEN_TILE)
    return x_new, ctx_new


def _odd_layer(x, ctx, mod_l, mod_c, need_ctx, g, w_in, shift_w, w0, w2, a0, a2, k_k, k_a, r_k,
               gn_w, gn_b, w_out):
    b, s, d = x.shape
    w = k_k.shape[0]
    npair = w // LANE
    o_wd0 = 2 * w
    o_r0 = o_wd0 + 2 * DECAY_LORA + 2 * AAA_LORA
    conv_ch = o_r0 + w
    segs = ((0, o_wd0), (o_r0, w), (o_wd0, o_r0 - o_wd0))
    w_p = w_in.astype(BF16)
    sw = shift_w
    g2 = g.reshape(1, d)

    def pairs(vec2):
        return vec2.reshape(2, npair, LANE).transpose(1, 0, 2).reshape(1, npair * 2 * LANE)

    def pair_mats(m):
        rr = m.shape[1]
        mp = m.reshape(2, rr, npair, LANE).transpose(2, 0, 1, 3)
        z = jnp.zeros_like(mp[:, 0])
        top = jnp.concatenate([mp[:, 0], z], axis=2)
        bot = jnp.concatenate([z, mp[:, 1]], axis=2)
        full = jnp.concatenate([top, bot], axis=1)
        return full.transpose(1, 0, 2).reshape(2 * rr, npair * 2 * LANE).astype(BF16)

    w0p, a0p, w2p, a2p = pairs(w0), pairs(a0), pair_mats(w2), pair_mats(a2)
    kk2, ka2, rk2 = k_k.reshape(1, w), k_a.reshape(1, w), r_k.reshape(1, w)
    wo = w_out.astype(BF16)

    def mix(xin, mod, h0):
        z, gate = _proj_shift(xin, g2, mod[1], mod[0], w_p, sw, conv_ch, segs, TOKEN_TILE)
        mm, gg, qt, yl, bn = _rwkv_local(z, w0p, w2p, a0p, a2p, kk2, ka2, rk2)
        y0, y1, hfin = _rwkv_scan(mm, gg, qt, h0)
        return (y0, y1, yl, bn, gate), hfin

    h_zero = jnp.zeros((b, 2, npair, LANE, LANE), F32)
    parts_c, h_ctx = mix(ctx, mod_c, h_zero)
    parts_l, _ = mix(x, mod_l, h_ctx)
    gnw, gnb = gn_w.reshape(1, w), gn_b.reshape(1, w)
    x_new = _rwkv_out(*parts_l, x, mod_l[2], gnw, gnb, wo, TOKEN_TILE)
    ctx_new = ctx
    if need_ctx:
        ctx_new = _rwkv_out(*parts_c, ctx, mod_c[2], gnw, gnb, wo, TOKEN_TILE)
    return x_new, ctx_new


def kernel(x, c, ctx, c_ctx, ada_w, ada_b, norm_g, e_w_in, e_kv_norm, e_q_norm, e_w_uq, e_w_ukv,
           e_q_head_norm, e_k_head_norm, e_w_fnet, e_w_out, o_w_in, o_shift_w, o_w0, o_w2, o_a0,
           o_a2, o_k_k, o_k_a, o_r_k, o_gn_w, o_gn_b, o_w_out):
    b, s, d = x.shape
    depth = ada_w.shape[0]
    assert b + 1 <= 8
    cond8 = jnp.concatenate([c, c_ctx[None, :], jnp.zeros((8 - b - 1, d), F32)], axis=0)
    mod = _ada(cond8, ada_w, ada_b)
    for layer in range(depth):
        need_ctx = layer < depth - 1
        m = mod[layer]
        chunk = lambda rows, i: rows[:, None, i * d:(i + 1) * d]
        lat, cx = m[:b], jnp.broadcast_to(m[b:b + 1], (b, 3 * d))
        mod_l = (chunk(lat, 0), 1.0 + chunk(lat, 1), chunk(lat, 2))
        mod_c = (chunk(cx, 0), 1.0 + chunk(cx, 1), chunk(cx, 2))
        j = layer // 2
        if layer % 2 == 0:
            x, ctx = _even_layer(x, ctx, mod_l, mod_c, need_ctx, norm_g[layer], e_w_in[j],
                                 e_kv_norm[j], e_q_norm[j], e_w_uq[j], e_w_ukv[j],
                                 e_q_head_norm[j], e_k_head_norm[j], e_w_fnet[j], e_w_out[j])
        else:
            x, ctx = _odd_layer(x, ctx, mod_l, mod_c, need_ctx, norm_g[layer], o_w_in[j],
                                o_shift_w[j], o_w0[j], o_w2[j], o_a0[j], o_a2[j], o_k_k[j],
                                o_k_a[j], o_r_k[j].reshape(-1), o_gn_w[j], o_gn_b[j], o_w_out[j])
    return x
```

```python
import functools
import math

import numpy as np
import jax
import jax.numpy as jnp
from jax import lax
from jax.experimental import pallas as pl
from jax.experimental.pallas import tpu as pltpu

F32 = jnp.float32
BF16 = jnp.bfloat16
ACT = BF16

GRID_W = 64
NORM_EPS = 1e-6
MLA_HEADS = 8
QK_NOPE = 64
QK_ROPE = 32
QK_HEAD = QK_NOPE + QK_ROPE
V_HEAD = 64
Q_LORA = 384
KV_LORA = 256
ROPE_FREQS = QK_ROPE // 4
ROPE_BASE = 10000.0
FNET_GROUPS = 4
FNET_GROUP_DIM = 128
RWKV_HEAD = 64
DECAY_LORA = 64
AAA_LORA = 64
GN_EPS = 64e-5

LANE = 128
CHUNK = 64
HEAD_SLOT = 128
VMEM_LIMIT = 56 * 1024 * 1024

TOKEN_TILE = 512
ATTN_Q_BLOCK = 2048
ATTN_K_BLOCK = 768
FOUR_LANE_TILE = 2048
FOUR_ROWS_TILE = 32

NN = (((1,), (0,)), ((), ()))
NT = (((1,), (1,)), ((), ()))
TN = (((0,), (0,)), ((), ()))


def _cparams(sem):
    return pltpu.CompilerParams(dimension_semantics=sem, vmem_limit_bytes=VMEM_LIMIT)


def _mm(a, b, dn=NN):
    return lax.dot_general(a.astype(BF16), b.astype(BF16), dn, preferred_element_type=F32)


def _split(a):
    hi = a.astype(BF16)
    lo = (a - hi.astype(F32)).astype(BF16)
    return hi, lo


def _mm3(a, b, dn=NN):
    ah, al = _split(a)
    bh, bl = _split(b)
    d = lambda x, y: lax.dot_general(x, y, dn, preferred_element_type=F32)
    return d(ah, bh) + d(al, bh) + d(ah, bl)


def _mm2r(a, b_exact):
    ah, al = _split(a)
    bb = b_exact.astype(BF16)
    return jnp.dot(jnp.concatenate([ah, al], axis=1), jnp.concatenate([bb, bb], axis=0),
                   preferred_element_type=F32)


def _mm2l(a_exact, b):
    bh, bl = _split(b)
    n = b.shape[1]
    y = jnp.dot(a_exact.astype(BF16), jnp.concatenate([bh, bl], axis=1),
                preferred_element_type=F32)
    return y[:, :n] + y[:, n:]


def _sigmoid(x):
    return 1.0 / (1.0 + jnp.exp(-x))


def _modnorm(x, g, sc1, sh):
    y = x * lax.rsqrt(jnp.mean(x * x, axis=-1, keepdims=True) + NORM_EPS)
    return (y * g) * sc1 + sh


def _iota2(shape, dim):
    return lax.broadcasted_iota(jnp.int32, shape, dim)


def _ada_body(c_ref, w_ref, b_ref, o_ref):
    c = c_ref[...]
    s = c * _sigmoid(c)
    o_ref[0] = _mm3(s, w_ref[0]) + b_ref[0]


def _ada(cond8, ada_w, ada_b):
    depth, d, n = ada_w.shape
    tn = 512
    return pl.pallas_call(
        _ada_body,
        grid=(depth, n // tn),
        in_specs=[
            pl.BlockSpec((8, d), lambda l, j: (0, 0)),
            pl.BlockSpec((1, d, tn), lambda l, j: (l, 0, j)),
            pl.BlockSpec((1, 1, tn), lambda l, j: (l, 0, j)),
        ],
        out_specs=pl.BlockSpec((1, 8, tn), lambda l, j: (l, 0, j)),
        out_shape=jax.ShapeDtypeStruct((depth, 8, n), F32),
        compiler_params=_cparams(("parallel", "parallel")),
        name="ada",
    )(cond8, ada_w, ada_b.reshape(depth, 1, n))


COL_CHUNK = 512


def _proj_body(x_ref, g_ref, sc_ref, sh_ref, w_ref, *o_refs, splits):
    h = _modnorm(x_ref[0], g_ref[...], sc_ref[0], sh_ref[0]).astype(BF16)
    mm = lambda c0, c1: jnp.dot(h, w_ref[:, c0:c1], preferred_element_type=F32)
    off = 0
    for o_ref, n in zip(o_refs, splits):
        if isinstance(n, tuple):
            groups, width = n
            y = mm(off, off + groups * width).astype(o_ref.dtype)
            for gi in range(groups):
                o_ref[0, gi] = y[:, gi * width:(gi + 1) * width]
            off += groups * width
            continue
        for c0 in range(0, n, COL_CHUNK):
            c1 = min(n, c0 + COL_CHUNK)
            o_ref[0, :, c0:c1] = mm(off + c0, off + c1).astype(o_ref.dtype)
        off += n


def _proj(x, g, sc1, sh, w, splits, tm):
    b, t, d = x.shape
    tm = min(tm, t)
    n = w.shape[1]
    vec = pl.BlockSpec((1, 1, d), lambda bi, i: (bi, 0, 0))
    specs, shapes = [], []
    for s in splits:
        if isinstance(s, tuple):
            specs.append(pl.BlockSpec((1, s[0], tm, s[1]), lambda bi, i: (bi, 0, i, 0)))
            shapes.append(jax.ShapeDtypeStruct((b, s[0], t, s[1]), ACT))
        else:
            specs.append(pl.BlockSpec((1, tm, s), lambda bi, i: (bi, i, 0)))
            shapes.append(jax.ShapeDtypeStruct((b, t, s), ACT))
    return pl.pallas_call(
        functools.partial(_proj_body, splits=splits),
        grid=(b, t // tm),
        in_specs=[
            pl.BlockSpec((1, tm, d), lambda bi, i: (bi, i, 0)),
            pl.BlockSpec((1, d), lambda bi, i: (0, 0)),
            vec, vec,
            pl.BlockSpec((d, n), lambda bi, i: (0, 0)),
        ],
        out_specs=specs,
        out_shape=shapes,
        compiler_params=_cparams(("parallel", "parallel")),
        name="proj",
    )(x, g, sc1, sh, w)


HALO = 16


def _proj_shift_body(x_ref, xp_ref, xn_ref, g_ref, sc_ref, sh_ref, w_ref, sw_ref, z_ref, gate_ref,
                     *, tm, n_conv, segs):
    i = pl.program_id(1)
    last = pl.num_programs(1) - 1
    g, sc1, sh = g_ref[...], sc_ref[0], sh_ref[0]
    h = _modnorm(x_ref[0], g, sc1, sh)
    hp = _modnorm(xp_ref[0], g, sc1, sh) * (i > 0).astype(F32)
    hn = _modnorm(xn_ref[0], g, sc1, sh) * (i < last).astype(F32)
    hb = jnp.concatenate([hp, h, hn], axis=0).astype(BF16)
    rows = tm + 2 * HALO
    dst = 0
    for src, width in segs:
        for c0 in range(0, width, COL_CHUNK):
            cw = min(COL_CHUNK, width - c0)
            cols = slice(src + c0, src + c0 + cw)
            u = jnp.dot(hb, w_ref[:, cols], preferred_element_type=F32)
            up = pltpu.roll(u, 1, 0)[HALO:HALO + tm]
            un = pltpu.roll(u, rows - 1, 0)[HALO:HALO + tm]
            um = u[HALO:HALO + tm]
            z_ref[0, :, dst + c0:dst + c0 + cw] = (
                sw_ref[0:1, cols] * up + sw_ref[1:2, cols] * um
                + sw_ref[2:3, cols] * un).astype(z_ref.dtype)
        dst += width
    hc = hb[HALO:HALO + tm]
    n_all = w_ref.shape[1]
    for c0 in range(n_conv, n_all, COL_CHUNK):
        c1 = min(n_all, c0 + COL_CHUNK)
        gate_ref[0, :, c0 - n_conv:c1 - n_conv] = jnp.dot(
            hc, w_ref[:, c0:c1], preferred_element_type=F32).astype(gate_ref.dtype)


def _proj_shift(x, g, sc1, sh, w, sw, n_conv, segs, tm):
    b, t, d = x.shape
    tm = min(tm, t)
    n = w.shape[1]
    hb = tm // HALO
    nhb = t // HALO
    vec = pl.BlockSpec((1, 1, d), lambda bi, i: (bi, 0, 0))
    return pl.pallas_call(
        functools.partial(_proj_shift_body, tm=tm, n_conv=n_conv, segs=segs),
        grid=(b, t // tm),
        in_specs=[
            pl.BlockSpec((1, tm, d), lambda bi, i: (bi, i, 0)),
            pl.BlockSpec((1, HALO, d), lambda bi, i: (bi, jnp.maximum(i * hb - 1, 0), 0)),
            pl.BlockSpec((1, HALO, d), lambda bi, i: (bi, jnp.minimum((i + 1) * hb, nhb - 1), 0)),
            pl.BlockSpec((1, d), lambda bi, i: (0, 0)),
            vec, vec,
            pl.BlockSpec((d, n), lambda bi, i: (0, 0)),
            pl.BlockSpec((3, n_conv), lambda bi, i: (0, 0)),
        ],
        out_specs=[pl.BlockSpec((1, tm, n_conv), lambda bi, i: (bi, i, 0)),
                   pl.BlockSpec((1, tm, n - n_conv), lambda bi, i: (bi, i, 0))],
        out_shape=[jax.ShapeDtypeStruct((b, t, n_conv), ACT),
                   jax.ShapeDtypeStruct((b, t, n - n_conv), ACT)],
        compiler_params=_cparams(("parallel", "parallel")),
        name="proj_shift",
    )(x, x, x, g, sc1, sh, w, sw)


def _rms(x, g):
    return x * lax.rsqrt(jnp.mean(x * x, axis=-1, keepdims=True) + NORM_EPS) * g


def _qkv_body(ual_ref, uac_ref, kvn_ref, qn_ref, wk_ref, wv_ref, wq_ref, wqr_ref, kg_ref, kb_ref,
              qb_ref, cosk_ref, cosq_ref, sa_ref, sb_ref, sinq_ref, q_ref, k_ref, v_ref, *, n_lat):
    ua = jnp.where(pl.program_id(1) < n_lat, ual_ref[0], uac_ref[0]).astype(F32)
    ckv = _rms(ua[:, :KV_LORA], kvn_ref[...]).astype(BF16)
    kr = ua[:, KV_LORA:KV_LORA + LANE]
    cq = _rms(ua[:, KV_LORA + LANE:], qn_ref[...]).astype(BF16)
    kn = jnp.dot(ckv, wk_ref[...], preferred_element_type=F32)
    vv = jnp.dot(ckv, wv_ref[...], preferred_element_type=F32)
    qq = jnp.dot(cq, wq_ref[...], preferred_element_type=F32)
    qr = jnp.dot(cq, wqr_ref[...], preferred_element_type=F32)
    ones_hi = (_iota2((1, HEAD_SLOT), 1) >= V_HEAD).astype(F32)
    pe = pltpu.roll(kr, QK_NOPE, 1)
    gp = pe * kg_ref[...]
    pe_rot = (pltpu.roll(gp, LANE - ROPE_FREQS, 1) * sa_ref[...]
              + pltpu.roll(gp, ROPE_FREQS, 1) * sb_ref[...])
    cosk, cosq, sinq = cosk_ref[...], cosq_ref[...], sinq_ref[...]
    inv_n = 1.0 / QK_HEAD
    scale = lambda x: lax.rsqrt(jnp.sum(x * x, axis=-1, keepdims=True) * inv_n + NORM_EPS)

    for h in range(MLA_HEADS):
        sl = slice(h * HEAD_SLOT, (h + 1) * HEAD_SLOT)
        kh = kn[:, sl] + pe
        k_ref[0, h] = (scale(kh) * (kh * cosk + pe_rot) + kb_ref[...]).astype(BF16)
        qh = qq[:, sl]
        q_ref[0, h] = (scale(qh) * (qh * cosq + qr[:, sl] * sinq) + qb_ref[...]).astype(BF16)
        v_ref[0, h] = (vv[:, sl] + ones_hi).astype(BF16)


def _qkv(ua_l, ua_c, kvn, qn, wk, wv, wq, wqr, kg, kb, qb, tabs, tm):
    b, s, wa = ua_l.shape
    tc = ua_c.shape[1]
    nl, ncx = s // tm, tc // tm
    full = lambda a: pl.BlockSpec(a.shape, lambda bi, i: (0,) * a.ndim)
    tab = pl.BlockSpec((tm, LANE), lambda bi, i: (i, 0))
    head = pl.BlockSpec((1, MLA_HEADS, tm, HEAD_SLOT), lambda bi, i: (bi, 0, i, 0))
    shape = jax.ShapeDtypeStruct((b, MLA_HEADS, s + tc, HEAD_SLOT), BF16)
    return pl.pallas_call(
        functools.partial(_qkv_body, n_lat=nl),
        grid=(b, nl + ncx),
        in_specs=[pl.BlockSpec((1, tm, wa), lambda bi, i: (bi, jnp.minimum(i, nl - 1), 0)),
                  pl.BlockSpec((1, tm, wa), lambda bi, i: (bi, jnp.maximum(i - nl, 0), 0)),
                  full(kvn), full(qn), full(wk), full(wv), full(wq), full(wqr), full(kg),
                  full(kb), full(qb)] + [tab] * len(tabs),
        out_specs=[head, head, head],
        out_shape=[shape, shape, shape],
        compiler_params=_cparams(("parallel", "parallel")),
        name="qkv",
    )(ua_l, ua_c, kvn, qn, wk, wv, wq, wqr, kg, kb, qb, *tabs)


BIAS_LANE = QK_HEAD
MAX_STATIC_BOUND = 50.0


def _attn_finish(acc_ref, o_ref):
    bq = acc_ref.shape[1]
    lane = _iota2((bq, LANE), 1)
    o0 = acc_ref[0] / pltpu.roll(acc_ref[0], V_HEAD, 1)
    o1 = acc_ref[1] / pltpu.roll(acc_ref[1], V_HEAD, 1)
    o_ref[0] = jnp.where(lane < V_HEAD, o0, pltpu.roll(o1, V_HEAD, 1)).astype(o_ref.dtype)


def _attn_static_body(q_ref, k_ref, v_ref, o_ref, acc_ref, *, bk):
    acc_ref[...] = jnp.zeros(acc_ref.shape, F32)

    def step(j, carry):
        rows = pl.ds(pl.multiple_of(j * bk, bk), bk)
        s = [lax.dot_general(q_ref[0, hh], k_ref[0, hh, rows, :], NT, preferred_element_type=F32)
             for hh in range(2)]
        p = [jnp.exp2(x.astype(BF16)) for x in s]
        for hh in range(2):
            acc_ref[hh] += jnp.dot(p[hh], v_ref[0, hh, rows, :], preferred_element_type=F32)
        return carry

    lax.fori_loop(0, k_ref.shape[2] // bk, step, 0)
    _attn_finish(acc_ref, o_ref)


def _attn_online_body(q_ref, k_ref, v_ref, o_ref, acc_ref, m_ref, *, bk):
    acc_ref[...] = jnp.zeros(acc_ref.shape, F32)
    m_ref[...] = jnp.full(m_ref.shape, -jnp.inf, F32)

    def step(j, carry):
        rows = pl.ds(pl.multiple_of(j * bk, bk), bk)
        for hh in range(2):
            s = lax.dot_general(q_ref[0, hh], k_ref[0, hh, rows, :], NT,
                                preferred_element_type=F32)
            m_prev = m_ref[hh]
            m_new = jnp.maximum(m_prev, jnp.max(s, axis=-1, keepdims=True))
            p = jnp.exp2(s - m_new)
            acc_ref[hh] = (jnp.exp2(m_prev - m_new) * acc_ref[hh]
                           + jnp.dot(p.astype(BF16), v_ref[0, hh, rows, :],
                                     preferred_element_type=F32))
            m_ref[hh] = m_new
        return carry

    lax.fori_loop(0, k_ref.shape[2] // bk, step, 0)
    _attn_finish(acc_ref, o_ref)


def _attention(q, k, v, static_ok, q_start, q_rows, k_start, k_rows, bq, bk):
    b, h, _, e = q.shape
    bq, bk = min(bq, q_rows), min(bk, k_rows)
    qi0, kj0 = q_start // bq, k_start // k_rows
    kv_blk = pl.BlockSpec((1, 2, k_rows, e), lambda bi, p, i: (bi, p, kj0, 0))

    def call(online):
        scratch = [pltpu.VMEM((2, bq, LANE), F32)]
        if online:
            scratch.append(pltpu.VMEM((2, bq, 1), F32))
        return pl.pallas_call(
            functools.partial(_attn_online_body if online else _attn_static_body, bk=bk),
            grid=(b, h // 2, q_rows // bq),
            in_specs=[pl.BlockSpec((1, 2, bq, e), lambda bi, p, i: (bi, p, qi0 + i, 0)),
                      kv_blk, kv_blk],
            out_specs=pl.BlockSpec((1, bq, 2 * V_HEAD), lambda bi, p, i: (bi, i, p)),
            out_shape=jax.ShapeDtypeStruct((b, q_rows, h * V_HEAD), ACT),
            scratch_shapes=scratch,
            compiler_params=_cparams(("parallel", "parallel", "arbitrary")),
            name="attention_online" if online else "attention",
        )(q, k, v)

    return lax.cond(static_ok, lambda: call(False), lambda: call(True))


def _dft_mats(n):
    idx = np.arange(n)
    ang = 2.0 * np.pi * ((idx[:, None] * idx[None, :]) % n) / n
    return np.cos(ang), np.sin(ang)


def _hilo(a):
    a = jnp.asarray(a, F32)
    hi = a.astype(BF16)
    return hi, (a - hi.astype(F32)).astype(BF16)


def _mm3c(ah, al, b, dn=NN):
    bh, bl = _split(b)
    d = lambda x, y: lax.dot_general(x, y, dn, preferred_element_type=F32)
    return d(ah, bh) + d(al, bh) + d(ah, bl)


def _four_rows_body(x_ref, w_ref, tc_ref, ts_ref, o_ref):
    r = tc_ref.shape[0]
    y = jnp.dot(w_ref[...], x_ref[0, 0], preferred_element_type=F32)
    yc, ys = y[:r], y[r:]
    tc, ts = tc_ref[...], ts_ref[...]
    o_ref[0, 0, :r] = (yc * tc - ys * ts).astype(o_ref.dtype)
    o_ref[0, 0, r:] = (yc * ts + ys * tc).astype(o_ref.dtype)


def _four_cols_body(y_ref, w_ref, cs_ref, wf_ref, o_ref, y3_scr, *, krt, scale):
    def one(j, carry):
        rows = pl.ds(pl.multiple_of(j * GRID_W, GRID_W), GRID_W)
        ycs = jnp.concatenate([y_ref[0, 0, 0, rows, :], y_ref[0, 0, 1, rows, :]], axis=0)
        y3 = jnp.dot(w_ref[...], ycs, preferred_element_type=F32)
        y3_scr[rows, :] = jnp.concatenate([y3[:GRID_W], y3[GRID_W:]], axis=1).astype(BF16)
        return carry

    lax.fori_loop(0, krt, one, 0, unroll=8)
    f = jnp.dot(y3_scr[...], cs_ref[...], preferred_element_type=F32) * scale
    o_ref[0, 0] = _mm(f, wf_ref[0]).astype(o_ref.dtype)


def _fourier_latent(xf, w_fnet):
    b, g, t, gd = xf.shape
    r = t // GRID_W
    wide = GRID_W * gd
    xv = xf.reshape(b, g, r, wide)
    cr, sr = _dft_mats(r)
    w_rows = jnp.asarray(np.concatenate([cr, sr], axis=0), BF16)
    kr_i, c_i = np.arange(r)[:, None], np.arange(GRID_W)[None, :]
    ang = 2.0 * np.pi * ((kr_i * c_i) % t) / t
    twc = jnp.repeat(jnp.asarray(np.cos(ang), F32), gd, axis=1)
    tws = jnp.repeat(jnp.asarray(np.sin(ang), F32), gd, axis=1)
    tl = min(FOUR_LANE_TILE, wide)
    y2 = pl.pallas_call(
        _four_rows_body,
        grid=(b, g, wide // tl),
        in_specs=[pl.BlockSpec((1, 1, r, tl), lambda bi, gi, l: (bi, gi, 0, l)),
                  pl.BlockSpec((2 * r, r), lambda bi, gi, l: (0, 0)),
                  pl.BlockSpec((r, tl), lambda bi, gi, l: (0, l)),
                  pl.BlockSpec((r, tl), lambda bi, gi, l: (0, l))],
        out_specs=pl.BlockSpec((1, 1, 2 * r, tl), lambda bi, gi, l: (bi, gi, 0, l)),
        out_shape=jax.ShapeDtypeStruct((b, g, 2 * r, wide), ACT),
        compiler_params=_cparams(("parallel", "parallel", "parallel")),
        name="fourier_rows",
    )(xv, w_rows, twc, tws)
    y2v = y2.reshape(b, g, 2, r * GRID_W, gd)
    c64, s64 = _dft_mats(GRID_W)
    w_cols = jnp.asarray(np.block([[c64, -s64], [s64, c64]]), BF16)
    cc, sc = _dft_mats(gd)
    w_chan = jnp.asarray(np.concatenate([cc, -sc], axis=0), BF16)
    krt = min(FOUR_ROWS_TILE, r)
    const = lambda a: pl.BlockSpec(a.shape, lambda bi, gi, i: (0, 0))
    fo = pl.pallas_call(
        functools.partial(_four_cols_body, krt=krt, scale=1.0 / math.sqrt(t * gd)),
        grid=(b, g, r // krt),
        in_specs=[pl.BlockSpec((1, 1, 2, krt * GRID_W, gd), lambda bi, gi, i: (bi, gi, 0, i, 0)),
                  const(w_cols), const(w_chan),
                  pl.BlockSpec((1, gd, gd), lambda bi, gi, i: (gi, 0, 0))],
        out_specs=pl.BlockSpec((1, 1, krt * GRID_W, gd), lambda bi, gi, i: (bi, gi, i, 0)),
        out_shape=jax.ShapeDtypeStruct((b, g, r * GRID_W, gd), ACT),
        scratch_shapes=[pltpu.VMEM((krt * GRID_W, 2 * gd), BF16)],
        compiler_params=_cparams(("parallel", "parallel", "parallel")),
        name="fourier_cols",
    )(y2v, w_cols, w_chan, w_fnet)
    return fo.reshape(b, g, r, GRID_W, gd).transpose(0, 3, 2, 1, 4).reshape(b, t, g * gd)


def _four_dense_body(x_ref, ch_ref, cl_ref, th_ref, tl_ref, sh_ref, sl_ref, wf_ref, o_ref, *, scale):
    x = x_ref[0, 0]
    xh, xl = _split(x)
    d = lambda a, b: jnp.dot(a, b, preferred_element_type=F32)
    z = d(xh, ch_ref[...]) + d(xl, ch_ref[...]) + d(xh, cl_ref[...])
    zc, zs = z[:, :FNET_GROUP_DIM], z[:, FNET_GROUP_DIM:]
    f = (_mm3c(th_ref[...], tl_ref[...], zc) - _mm3c(sh_ref[...], sl_ref[...], zs)) * scale
    o_ref[0] = _mm3(f, wf_ref[0]).astype(o_ref.dtype)


def _fourier_dense(xf, w_fnet):
    b, g, t, gd = xf.shape
    cc, sc = _dft_mats(gd)
    ch, cl = _hilo(np.concatenate([cc, sc], axis=1))
    ct, st = _dft_mats(t)
    cth, ctl = _hilo(ct)
    sth, stl = _hilo(st)
    sq = pl.BlockSpec((t, t), lambda bi, gi: (0, 0))
    cs = pl.BlockSpec((gd, 2 * gd), lambda bi, gi: (0, 0))
    return pl.pallas_call(
        functools.partial(_four_dense_body, scale=1.0 / math.sqrt(t * gd)),
        grid=(b, g),
        in_specs=[pl.BlockSpec((1, 1, t, gd), lambda bi, gi: (bi, gi, 0, 0)), cs, cs, sq, sq, sq, sq,
                  pl.BlockSpec((1, gd, gd), lambda bi, gi: (gi, 0, 0))],
        out_specs=pl.BlockSpec((1, t, gd), lambda bi, gi: (bi, 0, gi)),
        out_shape=jax.ShapeDtypeStruct((b, t, g * gd), ACT),
        compiler_params=_cparams(("parallel", "parallel")),
        name="fourier_dense",
    )(xf, ch, cl, cth, ctl, sth, stl, w_fnet)


def _merge_body(o_ref, f_ref, gate_ref, x_ref, gl_ref, w_ref, out_ref):
    gt = gate_ref[0].astype(F32)
    mix = jnp.concatenate([o_ref[0], f_ref[0]], axis=-1).astype(F32) * (gt * _sigmoid(gt))
    y = jnp.dot(mix.astype(BF16), w_ref[...], preferred_element_type=F32)
    out_ref[0] = x_ref[0] + gl_ref[0] * y


def _merge(o, f, gate, x, gl, w, tm):
    b, t, d = x.shape
    tm = min(tm, t)
    half = o.shape[2]
    tok = lambda n: pl.BlockSpec((1, tm, n), lambda bi, i: (bi, i, 0))
    return pl.pallas_call(
        _merge_body,
        grid=(b, t // tm),
        in_specs=[tok(half), tok(half), tok(d), tok(d),
                  pl.BlockSpec((1, 1, d), lambda bi, i: (bi, 0, 0)),
                  pl.BlockSpec(w.shape, lambda bi, i: (0, 0))],
        out_specs=tok(d),
        out_shape=jax.ShapeDtypeStruct((b, t, d), F32),
        compiler_params=_cparams(("parallel", "parallel")),
        name="merge",
    )(o, f, gate, x, gl, w)


EXP_M05 = math.exp(-0.5)


PAIRS_PER_STEP = 8
PAIRS_PER_GROUP = 8
LOCAL_CHUNKS = 4
GROUP_LAG = 5


def _rwkv_local_body(*refs, cs):
    zwa_ref, w0_ref, w2_ref, a0_ref, a2_ref = refs[3:8]
    zwa = zwa_ref[0].astype(F32)
    lora = (_mm(jnp.tanh(zwa[:, :LANE]), w2_ref[...]) + w0_ref[...],
            _mm(zwa[:, LANE:], a2_ref[...]) + a0_ref[...])
    groups = [_rwkv_local_group(*refs, lora=lora, base=base, ck=ck) for ck in range(cs)
              for base in range(0, PAIRS_PER_STEP, PAIRS_PER_GROUP)]
    tick = 0
    while groups:
        live = groups[:tick // GROUP_LAG + 1]
        for g in live:
            if next(g, StopIteration) is StopIteration:
                groups.remove(g)
        tick += 1


def _rwkv_local_group(zk_ref, zv_ref, zr_ref, zwa_ref, w0_ref, w2_ref, a0_ref, a2_ref,
                      kk_ref, ka_ref, rk_ref, m_ref, g_ref, qt_ref, yl_ref, bn_ref,
                      *, lora, base, ck):
    c = CHUNK
    rows = slice(ck * CHUNK, (ck + 1) * CHUNK)

    head0 = _iota2((1, LANE), 1) < RWKV_HEAD
    r2 = _iota2((LANE, LANE), 0)
    c2 = _iota2((LANE, LANE), 1)
    same = (r2 // RWKV_HEAD) == (c2 // RWKV_HEAD)
    ones_bd = same.astype(F32)
    eye = r2 == c2

    def stack(x):
        z = jnp.zeros_like(x)
        return jnp.concatenate([jnp.where(head0, x, z), jnp.where(head0, z, x)], axis=0)

    stack_b = lambda x: stack(x.astype(BF16))
    fold = lambda x: x[:c] + x[c:]

    pairs = range(PAIRS_PER_GROUP)
    chains = [(q, d) for q in pairs for d in range(2)]
    qls = [slice((base + q) * LANE, (base + q + 1) * LANE) for q in pairs]
    ks = [zk_ref[0, rows, ql].astype(F32) for ql in qls]
    vs_ = [zv_ref[0, rows, ql].astype(F32) for ql in qls]
    rs = [zr_ref[0, rows, ql].astype(F32) for ql in qls]
    pcols = [slice((base + q) * 2 * LANE, (base + q + 1) * 2 * LANE) for q in pairs]
    wraw = [lora[0][rows, pc] for pc in pcols]
    araw = [lora[1][rows, pc] for pc in pcols]
    yield
    logw = [-EXP_M05 * _sigmoid(w) for w in wraw]
    a_all = [_sigmoid(a) for a in araw]
    kk0 = [ks[q] * kk_ref[:, qls[q]] for q in pairs]
    ss = [_mm2r(x * x, ones_bd) for x in kk0]
    yield
    kk = [kk0[q] / jnp.maximum(jnp.sqrt(ss[q]), 1e-12) for q in pairs]
    vstk = [stack_b(v) for v in vs_]

    dsl = [slice(d * LANE, (d + 1) * LANE) for d in range(2)]
    lw = [logw[q][:, dsl[d]] for q, d in chains]
    ad = [a_all[q][:, dsl[d]] for q, d in chains]
    kd = [ks[q] * (1.0 + (ad[i] - 1.0) * ka_ref[:, qls[q]]) for i, (q, d) in enumerate(chains)]
    bb = [kk[q] * ad[i] for i, (q, d) in enumerate(chains)]
    bonus = [_mm2r(rs[q] * (kd[2 * q] + kd[2 * q + 1]) * rk_ref[:, qls[q]], ones_bd) * vs_[q]
             for q in pairs]
    yield
    r3 = _iota2((c, 3 * c), 0)
    c3 = _iota2((c, 3 * c), 1) & (c - 1)
    tri3 = (c3 <= r3).astype(BF16)
    tt = _iota2((c, LANE), 0)
    ts = _iota2((c, LANE), 1) & (c - 1)
    strict = [ts < tt, ts > tt]
    incl = [ts <= tt, ts >= tt]
    eye_c = (ts == tt).astype(F32)

    def prefix(x):
        xh, xl = _split(x)
        xll = (x - xh.astype(F32) - xl.astype(F32)).astype(BF16)
        return jnp.dot(tri3, jnp.concatenate([xh, xl, xll], axis=0), preferred_element_type=F32)

    pre = [prefix(x) for x in logw]
    ltot = [pre[q][c - 1:c, dsl[d]] for q, d in chains]
    lc = [pre[q][:, dsl[0]] if d == 0 else ltot[i] - pre[q][:, dsl[1]] + lw[i]
          for i, (q, d) in enumerate(chains)]
    yield
    n = len(chains)
    bdot = lambda a, b: jnp.dot(a, b, preferred_element_type=F32)
    kkd = [(kk[q] * jnp.exp(lc[i] - lw[i])).astype(BF16) for i, (q, d) in enumerate(chains)]
    rd = [rs[q] * jnp.exp(lc[i]) for i, (q, d) in enumerate(chains)]
    e_inv = [jnp.exp(-x) for x in lc]
    inv_s = [jnp.concatenate([stack_b(bb[i] * e_inv[i]), stack_b(kd[i] * e_inv[i])], axis=0)
             for i in range(n)]
    yield
    amat = [lax.dot_general(jnp.concatenate([kkd[i], rd[i].astype(BF16)], axis=0), inv_s[i],
                            NT, preferred_element_type=F32).astype(BF16) for i in range(n)]
    zero_c = jnp.zeros((c, LANE), BF16)
    a_kb = [jnp.where(strict[d], amat[i][:c, :LANE], zero_c) for i, (q, d) in enumerate(chains)]
    a_kk = [jnp.where(strict[d], amat[i][:c, LANE:], zero_c) for i, (q, d) in enumerate(chains)]
    aq_b = [jnp.where(incl[d], amat[i][c:, :LANE], zero_c) for i, (q, d) in enumerate(chains)]
    aq_k = [jnp.where(incl[d], amat[i][c:, LANE:], zero_c) for i, (q, d) in enumerate(chains)]
    yield
    av = [bdot(jnp.concatenate([a_kk[i], aq_k[i]], axis=0), vstk[q])
          for i, (q, d) in enumerate(chains)]
    yield
    tinv = [eye_c - a.astype(F32) for a in a_kb]
    qpow = [bdot(a, stack(a)).astype(BF16) for a in a_kb]
    yield
    for _ in range(4):
        prod = [bdot(qpow[i], jnp.concatenate([stack(qpow[i]), stack_b(tinv[i])], axis=1))
                for i in range(n)]
        qpow = [x[:, :LANE].astype(BF16) for x in prod]
        tinv = [tinv[i] + prod[i][:, LANE:] for i in range(n)]
        yield
    tinv = [tinv[i] + bdot(qpow[i], stack_b(tinv[i])) for i in range(n)]
    yield
    tsplit = [_split(t) for t in tinv]
    ia_t = [bdot((eye_c + a_kb[i].astype(F32)).astype(BF16),
                 jnp.concatenate([stack(tsplit[i][0]), stack(tsplit[i][1])], axis=1))
            for i in range(n)]
    resid = [eye_c - ia_t[i][:, :LANE] - ia_t[i][:, LANE:] for i in range(n)]
    yield
    tinv = [tinv[i] + bdot(tinv[i].astype(BF16), stack_b(resid[i])) for i in range(n)]
    yield
    x = [bdot(tinv[i].astype(BF16),
              jnp.concatenate([stack(kkd[i]), stack_b(av[i][:c])], axis=1))
         for i in range(n)]
    xb = [v.astype(BF16) for v in x]
    yield
    qy = [jnp.concatenate([rd[i], av[i][c:]], axis=1)
          - bdot(aq_b[i], jnp.concatenate([stack(xb[i][:, :LANE]), stack(xb[i][:, LANE:])],
                                          axis=1)) for i in range(n)]
    yield
    e_end = [jnp.exp(ltot[i] - lc[i]) for i in range(n)]
    ends = [jnp.concatenate([(-bb[i] * e_end[i]).astype(BF16), (kd[i] * e_end[i]).astype(BF16)],
                            axis=0) for i in range(n)]
    wuv = [jnp.concatenate([xb[i], jnp.concatenate([zero_c, vs_[q].astype(BF16)], axis=1)],
                           axis=0) for i, (q, d) in enumerate(chains)]
    mg = [lax.dot_general(ends[i], wuv[i], TN, preferred_element_type=F32) for i in range(n)]
    yield
    for i, (q, d) in enumerate(chains):
        m_ref[0, ck, d, base + q] = fold(jnp.where(eye, jnp.exp(ltot[i]), 0.0)
                                         + jnp.where(same, mg[i][:, :LANE], 0.0)).astype(BF16)
        g_ref[0, ck, d, base + q] = fold(jnp.where(same, mg[i][:, LANE:], 0.0)
                                         ).astype(g_ref.dtype)
        qt_ref[0, d, rows, qls[q]] = qy[i][:, :LANE].astype(BF16)
    for q in pairs:
        yl_ref[0, rows, qls[q]] = (qy[2 * q][:, LANE:]
                                   + qy[2 * q + 1][:, LANE:]).astype(yl_ref.dtype)
        bn_ref[0, rows, qls[q]] = bonus[q].astype(bn_ref.dtype)


def _rwkv_local(z, w0p, w2p, a0p, a2p, k_k, k_a, r_k):
    b, t, _ = z.shape
    w = k_k.shape[1]
    npair = w // LANE
    pp = PAIRS_PER_STEP
    ng = npair // pp
    wl = pp * LANE
    nc = t // CHUNK
    cs = LOCAL_CHUNKS if nc % LOCAL_CHUNKS == 0 else 1
    rows = cs * CHUNK
    tokc = lambda base: pl.BlockSpec((1, rows, wl), lambda bi, ci, p: (bi, ci, base + p))
    perp3 = lambda n: pl.BlockSpec((n, pp * 2 * LANE), lambda bi, ci, p: (0, p))
    vecp = pl.BlockSpec((1, wl), lambda bi, ci, p: (0, p))
    mat = pl.BlockSpec((1, cs, 2, pp, CHUNK, LANE), lambda bi, ci, p: (bi, ci, 0, p, 0, 0))
    return pl.pallas_call(
        functools.partial(_rwkv_local_body, cs=cs),
        grid=(b, nc // cs, ng),
        in_specs=[tokc(0), tokc(ng), tokc(2 * ng),
                  pl.BlockSpec((1, rows, 2 * LANE), lambda bi, ci, p: (bi, ci, 3 * npair // 2)),
                  perp3(1), perp3(LANE), perp3(1), perp3(LANE), vecp, vecp, vecp],
        out_specs=[mat, mat,
                   pl.BlockSpec((1, 2, rows, wl), lambda bi, ci, p: (bi, 0, ci, p)),
                   pl.BlockSpec((1, rows, wl), lambda bi, ci, p: (bi, ci, p)),
                   pl.BlockSpec((1, rows, wl), lambda bi, ci, p: (bi, ci, p))],
        out_shape=[jax.ShapeDtypeStruct((b, nc, 2, npair, CHUNK, LANE), BF16),
                   jax.ShapeDtypeStruct((b, nc, 2, npair, CHUNK, LANE), ACT),
                   jax.ShapeDtypeStruct((b, 2, t, w), BF16),
                   jax.ShapeDtypeStruct((b, t, w), ACT),
                   jax.ShapeDtypeStruct((b, t, w), ACT)],
        compiler_params=_cparams(("parallel", "parallel", "parallel")),
        name="rwkv_local",
    )(z, z, z, z, w0p, w2p, a0p, a2p, k_k, k_a, r_k)


SCAN_CHUNKS = 8


def _rwkv_scan_body(m0_ref, g0_ref, q0_ref, m1_ref, g1_ref, q1_ref, h0_ref,
                    y0_ref, y1_ref, hfin_ref, h_scr, *, npair, cs):
    ci = pl.program_id(1)

    @pl.when(ci == 0)
    def _():
        h_scr[...] = h0_ref[0]

    head0 = _iota2((1, LANE), 1) < RWKV_HEAD

    def expand(x):
        z = jnp.zeros_like(x)
        return jnp.concatenate([jnp.where(head0, x, z), jnp.where(head0, z, x)], axis=0)

    refs = ((m0_ref, g0_ref, q0_ref, y0_ref), (m1_ref, g1_ref, q1_ref, y1_ref))
    chains = [(d, p) for d in range(2) for p in range(npair)]
    lanes = [slice(p * LANE, (p + 1) * LANE) for p in range(npair)]
    h = [h_scr[d, p] for d, p in chains]
    for step in range(cs):
        ck = (step, cs - 1 - step)
        rows = [slice(c * CHUNK, (c + 1) * CHUNK) for c in ck]
        hb = [x.astype(BF16) for x in h]
        res = [jnp.dot(jnp.concatenate([refs[d][2][0, 0, rows[d], lanes[p]],
                                        refs[d][0][0, ck[d], 0, p]], axis=0), hb[i],
                       preferred_element_type=F32) for i, (d, p) in enumerate(chains)]
        for i, (d, p) in enumerate(chains):
            refs[d][3][0, rows[d], lanes[p]] = res[i][:CHUNK].astype(refs[d][3].dtype)
        h = [expand(res[i][CHUNK:] + refs[d][1][0, ck[d], 0, p].astype(F32))
             for i, (d, p) in enumerate(chains)]
    for i, (d, p) in enumerate(chains):
        h_scr[d, p] = h[i]

    @pl.when(ci == pl.num_programs(1) - 1)
    def _():
        hfin_ref[0] = h_scr[...]


def _rwkv_scan(mm, gg, qt, h0):
    b, nc, _, npair, _, _ = mm.shape
    t, w = qt.shape[2], qt.shape[3]
    cs = SCAN_CHUNKS if nc % SCAN_CHUNKS == 0 else 1
    nb = nc // cs
    fwd = lambda bi, ci: (bi, ci, 0, 0, 0, 0)
    rev = lambda bi, ci: (bi, nb - 1 - ci, 1, 0, 0, 0)
    mblk = (1, cs, 1, npair, CHUNK, LANE)
    hspec = pl.BlockSpec((1, 2, npair, LANE, LANE), lambda bi, ci: (bi, 0, 0, 0, 0))
    return pl.pallas_call(
        functools.partial(_rwkv_scan_body, npair=npair, cs=cs),
        grid=(b, nb),
        in_specs=[pl.BlockSpec(mblk, fwd), pl.BlockSpec(mblk, fwd),
                  pl.BlockSpec((1, 1, cs * CHUNK, w), lambda bi, ci: (bi, 0, ci, 0)),
                  pl.BlockSpec(mblk, rev), pl.BlockSpec(mblk, rev),
                  pl.BlockSpec((1, 1, cs * CHUNK, w), lambda bi, ci: (bi, 1, nb - 1 - ci, 0)),
                  hspec],
        out_specs=[pl.BlockSpec((1, cs * CHUNK, w), lambda bi, ci: (bi, ci, 0)),
                   pl.BlockSpec((1, cs * CHUNK, w), lambda bi, ci: (bi, nb - 1 - ci, 0)),
                   hspec],
        out_shape=[jax.ShapeDtypeStruct((b, t, w), ACT), jax.ShapeDtypeStruct((b, t, w), ACT),
                   jax.ShapeDtypeStruct(h0.shape, F32)],
        scratch_shapes=[pltpu.VMEM((2, npair, LANE, LANE), F32)],
        compiler_params=_cparams(("parallel", "arbitrary")),
        name="rwkv_scan",
    )(mm, gg, qt, mm, gg, qt, h0)


def _rwkv_out_body(y0_ref, y1_ref, yl_ref, bn_ref, gate_ref, x_ref, gl_ref, gnw_ref, gnb_ref,
                   w_ref, o_ref):
    y = y0_ref[0].astype(F32) + y1_ref[0].astype(F32) + yl_ref[0].astype(F32)
    r2 = _iota2((LANE, LANE), 0)
    c2 = _iota2((LANE, LANE), 1)
    avg = ((r2 // RWKV_HEAD) == (c2 // RWKV_HEAD)).astype(F32) * (1.0 / RWKV_HEAD)
    parts = []
    for p in range(y.shape[1] // LANE):
        yp = y[:, p * LANE:(p + 1) * LANE]
        dl = yp - _mm2r(yp, avg)
        var = _mm2r(dl * dl, avg)
        parts.append(dl * lax.rsqrt(var + GN_EPS))
    yn = jnp.concatenate(parts, axis=1)
    gt = gate_ref[0].astype(F32)
    act = (yn * gnw_ref[...] + gnb_ref[...] + bn_ref[0].astype(F32)) * (gt * _sigmoid(gt))
    out = jnp.dot(act.astype(BF16), w_ref[...], preferred_element_type=F32)
    o_ref[0] = x_ref[0] + gl_ref[0] * out


def _rwkv_out(y0, y1, yl, bn, gate, x, gl, gnw, gnb, w, tm):
    b, t, d = x.shape
    tm = min(tm, t)
    wd = y0.shape[2]
    tok = lambda n: pl.BlockSpec((1, tm, n), lambda bi, i: (bi, i, 0))
    return pl.pallas_call(
        _rwkv_out_body,
        grid=(b, t // tm),
        in_specs=[tok(wd), tok(wd), tok(wd), tok(wd), tok(wd), tok(d),
                  pl.BlockSpec((1, 1, d), lambda bi, i: (bi, 0, 0)),
                  pl.BlockSpec((1, wd), lambda bi, i: (0, 0)),
                  pl.BlockSpec((1, wd), lambda bi, i: (0, 0)),
                  pl.BlockSpec(w.shape, lambda bi, i: (0, 0))],
        out_specs=tok(d),
        out_shape=jax.ShapeDtypeStruct((b, t, d), F32),
        compiler_params=_cparams(("parallel", "parallel")),
        name="rwkv_out",
    )(y0, y1, yl, bn, gate, x, gl, gnw, gnb, w)


def _rope_tables(t):
    rows = t // GRID_W
    row = jnp.repeat(jnp.arange(rows, dtype=F32), GRID_W)
    col = jnp.tile(jnp.arange(GRID_W, dtype=F32), rows)
    inv = 1.0 / (ROPE_BASE ** (jnp.arange(ROPE_FREQS, dtype=F32) / ROPE_FREQS))
    ang = jnp.stack([row[:, None] * inv, col[:, None] * inv], axis=1)
    cos, sin = jnp.cos(ang), jnp.sin(ang)
    zeros = jnp.zeros_like(sin)
    ones_lo = jnp.ones((t, QK_NOPE), F32)
    pad_hi = HEAD_SLOT - QK_HEAD
    cos_t = jnp.concatenate([ones_lo, jnp.concatenate([cos, cos], axis=2).reshape(t, QK_ROPE),
                             jnp.ones((t, pad_hi), F32)], axis=1)
    sa = jnp.concatenate([jnp.zeros((t, QK_NOPE), F32),
                          jnp.concatenate([-sin, zeros], axis=2).reshape(t, QK_ROPE),
                          jnp.zeros((t, pad_hi), F32)], axis=1)
    sb = jnp.concatenate([jnp.zeros((t, QK_NOPE), F32),
                          jnp.concatenate([zeros, sin], axis=2).reshape(t, QK_ROPE),
                          jnp.zeros((t, pad_hi), F32)], axis=1)
    return cos_t, sa, sb


def _even_layer(x, ctx, mod_l, mod_c, need_ctx, g, w_in, kv_norm, q_norm, w_uq, w_ukv,
                q_head_norm, k_head_norm, w_fnet, w_out):
    b, s, d = x.shape
    tc = ctx.shape[1]
    e_q0 = KV_LORA + QK_ROPE
    e_f0 = e_q0 + Q_LORA
    e_g0 = e_f0 + FNET_GROUPS * FNET_GROUP_DIM
    w_p = jnp.concatenate([w_in[:, e_g0:], w_in[:, e_f0:e_g0], w_in[:, :e_q0],
                           jnp.zeros((d, LANE - QK_ROPE), F32), w_in[:, e_q0:e_f0]],
                          axis=1).astype(BF16)
    splits = (d, (FNET_GROUPS, FNET_GROUP_DIM), KV_LORA + LANE + Q_LORA)
    kvw = w_ukv.reshape(KV_LORA, MLA_HEADS, QK_NOPE + V_HEAD)
    wk = jnp.pad(kvw[:, :, :QK_NOPE], ((0, 0), (0, 0), (0, HEAD_SLOT - QK_NOPE)))
    wk = wk.reshape(KV_LORA, MLA_HEADS * HEAD_SLOT).astype(BF16)
    wv = jnp.pad(kvw[:, :, QK_NOPE:], ((0, 0), (0, 0), (0, HEAD_SLOT - V_HEAD)))
    wv = wv.reshape(KV_LORA, MLA_HEADS * HEAD_SLOT).astype(BF16)
    wq3 = jnp.pad(w_uq.reshape(Q_LORA, MLA_HEADS, QK_HEAD), ((0, 0), (0, 0), (0, HEAD_SLOT - QK_HEAD)))
    wq = wq3.reshape(Q_LORA, MLA_HEADS * HEAD_SLOT).astype(BF16)
    kg = jnp.pad(k_head_norm, (0, HEAD_SLOT - QK_HEAD)).reshape(1, HEAD_SLOT)
    qg = (jnp.pad(q_head_norm, (0, HEAD_SLOT - QK_HEAD))
          * (QK_HEAD ** -0.5 * math.log2(math.e))).reshape(1, HEAD_SLOT)
    lane = np.arange(HEAD_SLOT)
    tail = (lane >= QK_NOPE) & (lane < QK_HEAD)
    first = tail & (((lane - QK_NOPE) // ROPE_FREQS) % 2 == 0)
    partner = np.where(first, lane + ROPE_FREQS, np.where(tail, lane - ROPE_FREQS, lane))
    sign = np.where(first, -1.0, np.where(tail, 1.0, 0.0)).astype(np.float32)
    wqr = (wq3[:, :, partner] * (sign * qg[0, partner])).reshape(Q_LORA, MLA_HEADS * HEAD_SLOT)
    wqr = wqr.astype(BF16)
    kvn, qn = kv_norm.reshape(1, -1), q_norm.reshape(1, -1)
    g2 = g.reshape(1, d)
    bound = (1.02 * QK_HEAD * jnp.max(jnp.abs(qg)) * jnp.max(jnp.abs(kg))).astype(BF16).astype(F32)
    static_ok = bound <= MAX_STATIC_BOUND
    bias_lane = (jnp.arange(HEAD_SLOT) == BIAS_LANE).astype(F32).reshape(1, HEAD_SLOT)
    kb = bias_lane * jnp.where(static_ok, -bound, 0.0)
    qb = bias_lane

    gate_l, four_l, ua_l = _proj(x, g2, mod_l[1], mod_l[0], w_p, splits, TOKEN_TILE)
    gate_c, four_c, ua_c = _proj(ctx, g2, mod_c[1], mod_c[0], w_p, splits, TOKEN_TILE)
    sk = s + tc
    cos_t, sa, sb = _rope_tables(s)
    cos_t = jnp.concatenate([cos_t, jnp.ones((tc, HEAD_SLOT), F32)], axis=0)
    sa = jnp.concatenate([sa, jnp.zeros((tc, HEAD_SLOT), F32)], axis=0)
    sb = jnp.concatenate([sb, jnp.zeros((tc, HEAD_SLOT), F32)], axis=0)
    tabs = (cos_t * kg, cos_t * qg, sa, sb, sb - sa)
    q_all, k_all, v_all = _qkv(ua_l, ua_c, kvn, qn, wk, wv, wq, wqr, kg, kb, qb, tabs,
                               math.gcd(s, tc))
    bk = ATTN_K_BLOCK if sk % ATTN_K_BLOCK == 0 else tc
    o_l = _attention(q_all, k_all, v_all, static_ok, 0, s, 0, sk, ATTN_Q_BLOCK, bk)
    f_l = _fourier_latent(four_l, w_fnet)
    wo = w_out.astype(BF16)
    x_new = _merge(o_l, f_l, gate_l, x, mod_l[2], wo, TOKEN_TILE)
    ctx_new = ctx
    if need_ctx:
        o_c = _attention(q_all, k_all, v_all, static_ok, s, tc, s, tc, tc, tc)
        f_c = _fourier_dense(four_c, w_fnet)
        ctx_new = _merge(o_c, f_c, gate_c, ctx, mod_c[2], wo, TOKEN_TILE)
    return x_new, ctx_new


def _odd_layer(x, ctx, mod_l, mod_c, need_ctx, g, w_in, shift_w, w0, w2, a0, a2, k_k, k_a, r_k,
               gn_w, gn_b, w_out):
    b, s, d = x.shape
    w = k_k.shape[0]
    npair = w // LANE
    o_wd0 = 2 * w
    o_r0 = o_wd0 + 2 * DECAY_LORA + 2 * AAA_LORA
    conv_ch = o_r0 + w
    segs = ((0, o_wd0), (o_r0, w), (o_wd0, o_r0 - o_wd0))
    w_p = w_in.astype(BF16)
    sw = shift_w
    g2 = g.reshape(1, d)

    def pairs(vec2):
        return vec2.reshape(2, npair, LANE).transpose(1, 0, 2).reshape(1, npair * 2 * LANE)

    def pair_mats(m):
        rr = m.shape[1]
        mp = m.reshape(2, rr, npair, LANE).transpose(2, 0, 1, 3)
        z = jnp.zeros_like(mp[:, 0])
        top = jnp.concatenate([mp[:, 0], z], axis=2)
        bot = jnp.concatenate([z, mp[:, 1]], axis=2)
        full = jnp.concatenate([top, bot], axis=1)
        return full.transpose(1, 0, 2).reshape(2 * rr, npair * 2 * LANE).astype(BF16)

    w0p, a0p, w2p, a2p = pairs(w0), pairs(a0), pair_mats(w2), pair_mats(a2)
    kk2, ka2, rk2 = k_k.reshape(1, w), k_a.reshape(1, w), r_k.reshape(1, w)
    wo = w_out.astype(BF16)

    def mix(xin, mod, h0):
        z, gate = _proj_shift(xin, g2, mod[1], mod[0], w_p, sw, conv_ch, segs, TOKEN_TILE)
        mm, gg, qt, yl, bn = _rwkv_local(z, w0p, w2p, a0p, a2p, kk2, ka2, rk2)
        y0, y1, hfin = _rwkv_scan(mm, gg, qt, h0)
        return (y0, y1, yl, bn, gate), hfin

    h_zero = jnp.zeros((b, 2, npair, LANE, LANE), F32)
    parts_c, h_ctx = mix(ctx, mod_c, h_zero)
    parts_l, _ = mix(x, mod_l, h_ctx)
    gnw, gnb = gn_w.reshape(1, w), gn_b.reshape(1, w)
    x_new = _rwkv_out(*parts_l, x, mod_l[2], gnw, gnb, wo, TOKEN_TILE)
    ctx_new = ctx
    if need_ctx:
        ctx_new = _rwkv_out(*parts_c, ctx, mod_c[2], gnw, gnb, wo, TOKEN_TILE)
    return x_new, ctx_new


def kernel(x, c, ctx, c_ctx, ada_w, ada_b, norm_g, e_w_in, e_kv_norm, e_q_norm, e_w_uq, e_w_ukv,
           e_q_head_norm, e_k_head_norm, e_w_fnet, e_w_out, o_w_in, o_shift_w, o_w0, o_w2, o_a0,
           o_a2, o_k_k, o_k_a, o_r_k, o_gn_w, o_gn_b, o_w_out):
    b, s, d = x.shape
    depth = ada_w.shape[0]
    assert b + 1 <= 8
    cond8 = jnp.concatenate([c, c_ctx[None, :], jnp.zeros((8 - b - 1, d), F32)], axis=0)
    mod = _ada(cond8, ada_w, ada_b)
    for layer in range(depth):
        need_ctx = layer < depth - 1
        m = mod[layer]
        chunk = lambda rows, i: rows[:, None, i * d:(i + 1) * d]
        lat, cx = m[:b], jnp.broadcast_to(m[b:b + 1], (b, 3 * d))
        mod_l = (chunk(lat, 0), 1.0 + chunk(lat, 1), chunk(lat, 2))
        mod_c = (chunk(cx, 0), 1.0 + chunk(cx, 1), chunk(cx, 2))
        j = layer // 2
        if layer % 2 == 0:
            x, ctx = _even_layer(x, ctx, mod_l, mod_c, need_ctx, norm_g[layer], e_w_in[j],
                                 e_kv_norm[j], e_q_norm[j], e_w_uq[j], e_w_ukv[j],
                                 e_q_head_norm[j], e_k_head_norm[j], e_w_fnet[j], e_w_out[j])
        else:
            x, ctx = _odd_layer(x, ctx, mod_l, mod_c, need_ctx, norm_g[layer], o_w_in[j],
                                o_shift_w[j], o_w0[j], o_w2[j], o_a0[j], o_a2[j], o_k_k[j],
                                o_k_a[j], o_r_k[j].reshape(-1), o_gn_w[j], o_gn_b[j], o_w_out[j])
    return x
```

```python
import functools
import math

import numpy as np
import jax
import jax.numpy as jnp
from jax import lax
from jax.experimental import pallas as pl
from jax.experimental.pallas import tpu as pltpu

F32 = jnp.float32
BF16 = jnp.bfloat16
ACT = BF16

GRID_W = 64
NORM_EPS = 1e-6
MLA_HEADS = 8
QK_NOPE = 64
QK_ROPE = 32
QK_HEAD = QK_NOPE + QK_ROPE
V_HEAD = 64
Q_LORA = 384
KV_LORA = 256
ROPE_FREQS = QK_ROPE // 4
ROPE_BASE = 10000.0
FNET_GROUPS = 4
FNET_GROUP_DIM = 128
RWKV_HEAD = 64
DECAY_LORA = 64
AAA_LORA = 64
GN_EPS = 64e-5

LANE = 128
CHUNK = 64
HEAD_SLOT = 128
VMEM_LIMIT = 56 * 1024 * 1024

TOKEN_TILE = 512
ATTN_Q_BLOCK = 2048
ATTN_K_BLOCK = 768
FOUR_LANE_TILE = 2048
FOUR_ROWS_TILE = 32

NN = (((1,), (0,)), ((), ()))
NT = (((1,), (1,)), ((), ()))
TN = (((0,), (0,)), ((), ()))


def _cparams(sem):
    return pltpu.CompilerParams(dimension_semantics=sem, vmem_limit_bytes=VMEM_LIMIT)


def _mm(a, b, dn=NN):
    return lax.dot_general(a.astype(BF16), b.astype(BF16), dn, preferred_element_type=F32)


def _split(a):
    hi = a.astype(BF16)
    lo = (a - hi.astype(F32)).astype(BF16)
    return hi, lo


def _mm3(a, b, dn=NN):
    ah, al = _split(a)
    bh, bl = _split(b)
    d = lambda x, y: lax.dot_general(x, y, dn, preferred_element_type=F32)
    return d(ah, bh) + d(al, bh) + d(ah, bl)


def _mm2r(a, b_exact):
    ah, al = _split(a)
    bb = b_exact.astype(BF16)
    return jnp.dot(jnp.concatenate([ah, al], axis=1), jnp.concatenate([bb, bb], axis=0),
                   preferred_element_type=F32)


def _mm2l(a_exact, b):
    bh, bl = _split(b)
    n = b.shape[1]
    y = jnp.dot(a_exact.astype(BF16), jnp.concatenate([bh, bl], axis=1),
                preferred_element_type=F32)
    return y[:, :n] + y[:, n:]


def _sigmoid(x):
    return 1.0 / (1.0 + jnp.exp(-x))


def _modnorm(x, g, sc1, sh):
    y = x * lax.rsqrt(jnp.mean(x * x, axis=-1, keepdims=True) + NORM_EPS)
    return (y * g) * sc1 + sh


def _iota2(shape, dim):
    return lax.broadcasted_iota(jnp.int32, shape, dim)


def _ada_body(c_ref, w_ref, b_ref, o_ref):
    c = c_ref[...]
    s = c * _sigmoid(c)
    o_ref[0] = _mm3(s, w_ref[0]) + b_ref[0]


def _ada(cond8, ada_w, ada_b):
    depth, d, n = ada_w.shape
    tn = 512
    return pl.pallas_call(
        _ada_body,
        grid=(depth, n // tn),
        in_specs=[
            pl.BlockSpec((8, d), lambda l, j: (0, 0)),
            pl.BlockSpec((1, d, tn), lambda l, j: (l, 0, j)),
            pl.BlockSpec((1, 1, tn), lambda l, j: (l, 0, j)),
        ],
        out_specs=pl.BlockSpec((1, 8, tn), lambda l, j: (l, 0, j)),
        out_shape=jax.ShapeDtypeStruct((depth, 8, n), F32),
        compiler_params=_cparams(("parallel", "parallel")),
        name="ada",
    )(cond8, ada_w, ada_b.reshape(depth, 1, n))


COL_CHUNK = 512


def _proj_body(x_ref, g_ref, sc_ref, sh_ref, w_ref, *o_refs, splits):
    h = _modnorm(x_ref[0], g_ref[...], sc_ref[0], sh_ref[0]).astype(BF16)
    mm = lambda c0, c1: jnp.dot(h, w_ref[:, c0:c1], preferred_element_type=F32)
    off = 0
    for o_ref, n in zip(o_refs, splits):
        if isinstance(n, tuple):
            groups, width = n
            y = mm(off, off + groups * width).astype(o_ref.dtype)
            for gi in range(groups):
                o_ref[0, gi] = y[:, gi * width:(gi + 1) * width]
            off += groups * width
            continue
        for c0 in range(0, n, COL_CHUNK):
            c1 = min(n, c0 + COL_CHUNK)
            o_ref[0, :, c0:c1] = mm(off + c0, off + c1).astype(o_ref.dtype)
        off += n


def _proj(x, g, sc1, sh, w, splits, tm):
    b, t, d = x.shape
    tm = min(tm, t)
    n = w.shape[1]
    vec = pl.BlockSpec((1, 1, d), lambda bi, i: (bi, 0, 0))
    specs, shapes = [], []
    for s in splits:
        if isinstance(s, tuple):
            specs.append(pl.BlockSpec((1, s[0], tm, s[1]), lambda bi, i: (bi, 0, i, 0)))
            shapes.append(jax.ShapeDtypeStruct((b, s[0], t, s[1]), ACT))
        else:
            specs.append(pl.BlockSpec((1, tm, s), lambda bi, i: (bi, i, 0)))
            shapes.append(jax.ShapeDtypeStruct((b, t, s), ACT))
    return pl.pallas_call(
        functools.partial(_proj_body, splits=splits),
        grid=(b, t // tm),
        in_specs=[
            pl.BlockSpec((1, tm, d), lambda bi, i: (bi, i, 0)),
            pl.BlockSpec((1, d), lambda bi, i: (0, 0)),
            vec, vec,
            pl.BlockSpec((d, n), lambda bi, i: (0, 0)),
        ],
        out_specs=specs,
        out_shape=shapes,
        compiler_params=_cparams(("parallel", "parallel")),
        name="proj",
    )(x, g, sc1, sh, w)


HALO = 16


def _proj_shift_body(x_ref, xp_ref, xn_ref, g_ref, sc_ref, sh_ref, w_ref, sw_ref, z_ref, gate_ref,
                     *, tm, n_conv, segs):
    i = pl.program_id(1)
    last = pl.num_programs(1) - 1
    g, sc1, sh = g_ref[...], sc_ref[0], sh_ref[0]
    h = _modnorm(x_ref[0], g, sc1, sh)
    hp = _modnorm(xp_ref[0], g, sc1, sh) * (i > 0).astype(F32)
    hn = _modnorm(xn_ref[0], g, sc1, sh) * (i < last).astype(F32)
    hb = jnp.concatenate([hp, h, hn], axis=0).astype(BF16)
    rows = tm + 2 * HALO
    dst = 0
    for src, width in segs:
        for c0 in range(0, width, COL_CHUNK):
            cw = min(COL_CHUNK, width - c0)
            cols = slice(src + c0, src + c0 + cw)
            u = jnp.dot(hb, w_ref[:, cols], preferred_element_type=F32)
            up = pltpu.roll(u, 1, 0)[HALO:HALO + tm]
            un = pltpu.roll(u, rows - 1, 0)[HALO:HALO + tm]
            um = u[HALO:HALO + tm]
            z_ref[0, :, dst + c0:dst + c0 + cw] = (
                sw_ref[0:1, cols] * up + sw_ref[1:2, cols] * um
                + sw_ref[2:3, cols] * un).astype(z_ref.dtype)
        dst += width
    hc = hb[HALO:HALO + tm]
    n_all = w_ref.shape[1]
    for c0 in range(n_conv, n_all, COL_CHUNK):
        c1 = min(n_all, c0 + COL_CHUNK)
        gate_ref[0, :, c0 - n_conv:c1 - n_conv] = jnp.dot(
            hc, w_ref[:, c0:c1], preferred_element_type=F32).astype(gate_ref.dtype)


def _proj_shift(x, g, sc1, sh, w, sw, n_conv, segs, tm):
    b, t, d = x.shape
    tm = min(tm, t)
    n = w.shape[1]
    hb = tm // HALO
    nhb = t // HALO
    vec = pl.BlockSpec((1, 1, d), lambda bi, i: (bi, 0, 0))
    return pl.pallas_call(
        functools.partial(_proj_shift_body, tm=tm, n_conv=n_conv, segs=segs),
        grid=(b, t // tm),
        in_specs=[
            pl.BlockSpec((1, tm, d), lambda bi, i: (bi, i, 0)),
            pl.BlockSpec((1, HALO, d), lambda bi, i: (bi, jnp.maximum(i * hb - 1, 0), 0)),
            pl.BlockSpec((1, HALO, d), lambda bi, i: (bi, jnp.minimum((i + 1) * hb, nhb - 1), 0)),
            pl.BlockSpec((1, d), lambda bi, i: (0, 0)),
            vec, vec,
            pl.BlockSpec((d, n), lambda bi, i: (0, 0)),
            pl.BlockSpec((3, n_conv), lambda bi, i: (0, 0)),
        ],
        out_specs=[pl.BlockSpec((1, tm, n_conv), lambda bi, i: (bi, i, 0)),
                   pl.BlockSpec((1, tm, n - n_conv), lambda bi, i: (bi, i, 0))],
        out_shape=[jax.ShapeDtypeStruct((b, t, n_conv), ACT),
                   jax.ShapeDtypeStruct((b, t, n - n_conv), ACT)],
        compiler_params=_cparams(("parallel", "parallel")),
        name="proj_shift",
    )(x, x, x, g, sc1, sh, w, sw)


def _rms(x, g):
    return x * lax.rsqrt(jnp.mean(x * x, axis=-1, keepdims=True) + NORM_EPS) * g


def _qkv_body(ual_ref, uac_ref, kvn_ref, qn_ref, wk_ref, wv_ref, wq_ref, wqr_ref, kg_ref, kb_ref,
              qb_ref, cosk_ref, cosq_ref, sa_ref, sb_ref, sinq_ref, q_ref, k_ref, v_ref, *, n_lat):
    ua = jnp.where(pl.program_id(1) < n_lat, ual_ref[0], uac_ref[0]).astype(F32)
    ckv = _rms(ua[:, :KV_LORA], kvn_ref[...]).astype(BF16)
    kr = ua[:, KV_LORA:KV_LORA + LANE]
    cq = _rms(ua[:, KV_LORA + LANE:], qn_ref[...]).astype(BF16)
    kn = jnp.dot(ckv, wk_ref[...], preferred_element_type=F32)
    vv = jnp.dot(ckv, wv_ref[...], preferred_element_type=F32)
    qq = jnp.dot(cq, wq_ref[...], preferred_element_type=F32)
    qr = jnp.dot(cq, wqr_ref[...], preferred_element_type=F32)
    ones_hi = (_iota2((1, HEAD_SLOT), 1) >= V_HEAD).astype(F32)
    pe = pltpu.roll(kr, QK_NOPE, 1)
    gp = pe * kg_ref[...]
    pe_rot = (pltpu.roll(gp, LANE - ROPE_FREQS, 1) * sa_ref[...]
              + pltpu.roll(gp, ROPE_FREQS, 1) * sb_ref[...])
    cosk, cosq, sinq = cosk_ref[...], cosq_ref[...], sinq_ref[...]
    inv_n = 1.0 / QK_HEAD
    scale = lambda x: lax.rsqrt(jnp.sum(x * x, axis=-1, keepdims=True) * inv_n + NORM_EPS)

    for h in range(MLA_HEADS):
        sl = slice(h * HEAD_SLOT, (h + 1) * HEAD_SLOT)
        kh = kn[:, sl] + pe
        k_ref[0, h] = (scale(kh) * (kh * cosk + pe_rot) + kb_ref[...]).astype(BF16)
        qh = qq[:, sl]
        q_ref[0, h] = (scale(qh) * (qh * cosq + qr[:, sl] * sinq) + qb_ref[...]).astype(BF16)
        v_ref[0, h] = (vv[:, sl] + ones_hi).astype(BF16)


def _qkv(ua_l, ua_c, kvn, qn, wk, wv, wq, wqr, kg, kb, qb, tabs, tm):
    b, s, wa = ua_l.shape
    tc = ua_c.shape[1]
    nl, ncx = s // tm, tc // tm
    full = lambda a: pl.BlockSpec(a.shape, lambda bi, i: (0,) * a.ndim)
    tab = pl.BlockSpec((tm, LANE), lambda bi, i: (i, 0))
    head = pl.BlockSpec((1, MLA_HEADS, tm, HEAD_SLOT), lambda bi, i: (bi, 0, i, 0))
    shape = jax.ShapeDtypeStruct((b, MLA_HEADS, s + tc, HEAD_SLOT), BF16)
    return pl.pallas_call(
        functools.partial(_qkv_body, n_lat=nl),
        grid=(b, nl + ncx),
        in_specs=[pl.BlockSpec((1, tm, wa), lambda bi, i: (bi, jnp.minimum(i, nl - 1), 0)),
                  pl.BlockSpec((1, tm, wa), lambda bi, i: (bi, jnp.maximum(i - nl, 0), 0)),
                  full(kvn), full(qn), full(wk), full(wv), full(wq), full(wqr), full(kg),
                  full(kb), full(qb)] + [tab] * len(tabs),
        out_specs=[head, head, head],
        out_shape=[shape, shape, shape],
        compiler_params=_cparams(("parallel", "parallel")),
        name="qkv",
    )(ua_l, ua_c, kvn, qn, wk, wv, wq, wqr, kg, kb, qb, *tabs)


BIAS_LANE = QK_HEAD
MAX_STATIC_BOUND = 50.0


def _attn_finish(acc_ref, o_ref):
    bq = acc_ref.shape[1]
    lane = _iota2((bq, LANE), 1)
    o0 = acc_ref[0] / pltpu.roll(acc_ref[0], V_HEAD, 1)
    o1 = acc_ref[1] / pltpu.roll(acc_ref[1], V_HEAD, 1)
    o_ref[0] = jnp.where(lane < V_HEAD, o0, pltpu.roll(o1, V_HEAD, 1)).astype(o_ref.dtype)


def _attn_static_body(q_ref, k_ref, v_ref, o_ref, acc_ref, p0_ref, p1_ref, *, bk):
    nk = k_ref.shape[2] // bk
    krows = lambda j: pl.ds(pl.multiple_of(j * bk, bk), bk)
    bufs = (p0_ref, p1_ref)

    def weights(j, slot):
        for hh in range(2):
            s = lax.dot_general(q_ref[0, hh], k_ref[0, hh, krows(j), :], NT,
                                preferred_element_type=F32)
            bufs[slot][hh] = jnp.exp2(s.astype(BF16))

    def values(j, slot):
        for hh in range(2):
            acc_ref[hh] += jnp.dot(bufs[slot][hh], v_ref[0, hh, krows(j), :],
                                   preferred_element_type=F32)

    acc_ref[...] = jnp.zeros(acc_ref.shape, F32)
    weights(0, 0)

    def two_blocks(jj, carry):
        j = 2 * jj
        weights(j + 1, 1)
        values(j, 0)
        weights(j + 2, 0)
        values(j + 1, 1)
        return carry

    pairs_done = (nk - 1) // 2
    lax.fori_loop(0, pairs_done, two_blocks, 0)
    j = 2 * pairs_done
    if (nk - 1) % 2:
        weights(j + 1, 1)
        values(j, 0)
        values(j + 1, 1)
    else:
        values(j, 0)
    _attn_finish(acc_ref, o_ref)


def _attn_online_body(q_ref, k_ref, v_ref, o_ref, acc_ref, m_ref, *, bk):
    acc_ref[...] = jnp.zeros(acc_ref.shape, F32)
    m_ref[...] = jnp.full(m_ref.shape, -jnp.inf, F32)

    def step(j, carry):
        rows = pl.ds(pl.multiple_of(j * bk, bk), bk)
        for hh in range(2):
            s = lax.dot_general(q_ref[0, hh], k_ref[0, hh, rows, :], NT,
                                preferred_element_type=F32)
            m_prev = m_ref[hh]
            m_new = jnp.maximum(m_prev, jnp.max(s, axis=-1, keepdims=True))
            p = jnp.exp2(s - m_new)
            acc_ref[hh] = (jnp.exp2(m_prev - m_new) * acc_ref[hh]
                           + jnp.dot(p.astype(BF16), v_ref[0, hh, rows, :],
                                     preferred_element_type=F32))
            m_ref[hh] = m_new
        return carry

    lax.fori_loop(0, k_ref.shape[2] // bk, step, 0)
    _attn_finish(acc_ref, o_ref)


def _attention(q, k, v, static_ok, q_start, q_rows, k_start, k_rows, bq, bk):
    b, h, _, e = q.shape
    bq, bk = min(bq, q_rows), min(bk, k_rows)
    qi0, kj0 = q_start // bq, k_start // k_rows
    kv_blk = pl.BlockSpec((1, 2, k_rows, e), lambda bi, p, i: (bi, p, kj0, 0))

    def call(online):
        scratch = [pltpu.VMEM((2, bq, LANE), F32)]
        if online:
            scratch.append(pltpu.VMEM((2, bq, 1), F32))
        else:
            scratch += [pltpu.VMEM((2, bq, bk), BF16)] * 2
        return pl.pallas_call(
            functools.partial(_attn_online_body if online else _attn_static_body, bk=bk),
            grid=(b, h // 2, q_rows // bq),
            in_specs=[pl.BlockSpec((1, 2, bq, e), lambda bi, p, i: (bi, p, qi0 + i, 0)),
                      kv_blk, kv_blk],
            out_specs=pl.BlockSpec((1, bq, 2 * V_HEAD), lambda bi, p, i: (bi, i, p)),
            out_shape=jax.ShapeDtypeStruct((b, q_rows, h * V_HEAD), ACT),
            scratch_shapes=scratch,
            compiler_params=_cparams(("parallel", "parallel", "arbitrary")),
            name="attention_online" if online else "attention",
        )(q, k, v)

    return lax.cond(static_ok, lambda: call(False), lambda: call(True))


def _dft_mats(n):
    idx = np.arange(n)
    ang = 2.0 * np.pi * ((idx[:, None] * idx[None, :]) % n) / n
    return np.cos(ang), np.sin(ang)


def _hilo(a):
    a = jnp.asarray(a, F32)
    hi = a.astype(BF16)
    return hi, (a - hi.astype(F32)).astype(BF16)


def _mm3c(ah, al, b, dn=NN):
    bh, bl = _split(b)
    d = lambda x, y: lax.dot_general(x, y, dn, preferred_element_type=F32)
    return d(ah, bh) + d(al, bh) + d(ah, bl)


def _four_rows_body(x_ref, w_ref, tc_ref, ts_ref, o_ref):
    r = tc_ref.shape[0]
    y = jnp.dot(w_ref[...], x_ref[0, 0], preferred_element_type=F32)
    yc, ys = y[:r], y[r:]
    tc, ts = tc_ref[...], ts_ref[...]
    o_ref[0, 0, :r] = (yc * tc - ys * ts).astype(o_ref.dtype)
    o_ref[0, 0, r:] = (yc * ts + ys * tc).astype(o_ref.dtype)


def _four_cols_body(y_ref, w_ref, cs_ref, wf_ref, o_ref, y3_scr, *, krt, scale):
    def one(j, carry):
        rows = pl.ds(pl.multiple_of(j * GRID_W, GRID_W), GRID_W)
        ycs = jnp.concatenate([y_ref[0, 0, 0, rows, :], y_ref[0, 0, 1, rows, :]], axis=0)
        y3 = jnp.dot(w_ref[...], ycs, preferred_element_type=F32)
        y3_scr[rows, :] = jnp.concatenate([y3[:GRID_W], y3[GRID_W:]], axis=1).astype(BF16)
        return carry

    lax.fori_loop(0, krt, one, 0, unroll=8)
    f = jnp.dot(y3_scr[...], cs_ref[...], preferred_element_type=F32) * scale
    o_ref[0, 0] = _mm(f, wf_ref[0]).astype(o_ref.dtype)


def _fourier_latent(xf, w_fnet):
    b, g, t, gd = xf.shape
    r = t // GRID_W
    wide = GRID_W * gd
    xv = xf.reshape(b, g, r, wide)
    cr, sr = _dft_mats(r)
    w_rows = jnp.asarray(np.concatenate([cr, sr], axis=0), BF16)
    kr_i, c_i = np.arange(r)[:, None], np.arange(GRID_W)[None, :]
    ang = 2.0 * np.pi * ((kr_i * c_i) % t) / t
    twc = jnp.repeat(jnp.asarray(np.cos(ang), F32), gd, axis=1)
    tws = jnp.repeat(jnp.asarray(np.sin(ang), F32), gd, axis=1)
    tl = min(FOUR_LANE_TILE, wide)
    y2 = pl.pallas_call(
        _four_rows_body,
        grid=(b, g, wide // tl),
        in_specs=[pl.BlockSpec((1, 1, r, tl), lambda bi, gi, l: (bi, gi, 0, l)),
                  pl.BlockSpec((2 * r, r), lambda bi, gi, l: (0, 0)),
                  pl.BlockSpec((r, tl), lambda bi, gi, l: (0, l)),
                  pl.BlockSpec((r, tl), lambda bi, gi, l: (0, l))],
        out_specs=pl.BlockSpec((1, 1, 2 * r, tl), lambda bi, gi, l: (bi, gi, 0, l)),
        out_shape=jax.ShapeDtypeStruct((b, g, 2 * r, wide), ACT),
        compiler_params=_cparams(("parallel", "parallel", "parallel")),
        name="fourier_rows",
    )(xv, w_rows, twc, tws)
    y2v = y2.reshape(b, g, 2, r * GRID_W, gd)
    c64, s64 = _dft_mats(GRID_W)
    w_cols = jnp.asarray(np.block([[c64, -s64], [s64, c64]]), BF16)
    cc, sc = _dft_mats(gd)
    w_chan = jnp.asarray(np.concatenate([cc, -sc], axis=0), BF16)
    krt = min(FOUR_ROWS_TILE, r)
    const = lambda a: pl.BlockSpec(a.shape, lambda bi, gi, i: (0, 0))
    fo = pl.pallas_call(
        functools.partial(_four_cols_body, krt=krt, scale=1.0 / math.sqrt(t * gd)),
        grid=(b, g, r // krt),
        in_specs=[pl.BlockSpec((1, 1, 2, krt * GRID_W, gd), lambda bi, gi, i: (bi, gi, 0, i, 0)),
                  const(w_cols), const(w_chan),
                  pl.BlockSpec((1, gd, gd), lambda bi, gi, i: (gi, 0, 0))],
        out_specs=pl.BlockSpec((1, 1, krt * GRID_W, gd), lambda bi, gi, i: (bi, gi, i, 0)),
        out_shape=jax.ShapeDtypeStruct((b, g, r * GRID_W, gd), ACT),
        scratch_shapes=[pltpu.VMEM((krt * GRID_W, 2 * gd), BF16)],
        compiler_params=_cparams(("parallel", "parallel", "parallel")),
        name="fourier_cols",
    )(y2v, w_cols, w_chan, w_fnet)
    return fo.reshape(b, g, r, GRID_W, gd).transpose(0, 3, 2, 1, 4).reshape(b, t, g * gd)


def _four_dense_body(x_ref, ch_ref, cl_ref, th_ref, tl_ref, sh_ref, sl_ref, wf_ref, o_ref, *, scale):
    x = x_ref[0, 0]
    xh, xl = _split(x)
    d = lambda a, b: jnp.dot(a, b, preferred_element_type=F32)
    z = d(xh, ch_ref[...]) + d(xl, ch_ref[...]) + d(xh, cl_ref[...])
    zc, zs = z[:, :FNET_GROUP_DIM], z[:, FNET_GROUP_DIM:]
    f = (_mm3c(th_ref[...], tl_ref[...], zc) - _mm3c(sh_ref[...], sl_ref[...], zs)) * scale
    o_ref[0] = _mm3(f, wf_ref[0]).astype(o_ref.dtype)


def _fourier_dense(xf, w_fnet):
    b, g, t, gd = xf.shape
    cc, sc = _dft_mats(gd)
    ch, cl = _hilo(np.concatenate([cc, sc], axis=1))
    ct, st = _dft_mats(t)
    cth, ctl = _hilo(ct)
    sth, stl = _hilo(st)
    sq = pl.BlockSpec((t, t), lambda bi, gi: (0, 0))
    cs = pl.BlockSpec((gd, 2 * gd), lambda bi, gi: (0, 0))
    return pl.pallas_call(
        functools.partial(_four_dense_body, scale=1.0 / math.sqrt(t * gd)),
        grid=(b, g),
        in_specs=[pl.BlockSpec((1, 1, t, gd), lambda bi, gi: (bi, gi, 0, 0)), cs, cs, sq, sq, sq, sq,
                  pl.BlockSpec((1, gd, gd), lambda bi, gi: (gi, 0, 0))],
        out_specs=pl.BlockSpec((1, t, gd), lambda bi, gi: (bi, 0, gi)),
        out_shape=jax.ShapeDtypeStruct((b, t, g * gd), ACT),
        compiler_params=_cparams(("parallel", "parallel")),
        name="fourier_dense",
    )(xf, ch, cl, cth, ctl, sth, stl, w_fnet)


def _merge_body(o_ref, f_ref, gate_ref, x_ref, gl_ref, w_ref, out_ref):
    gt = gate_ref[0].astype(F32)
    mix = jnp.concatenate([o_ref[0], f_ref[0]], axis=-1).astype(F32) * (gt * _sigmoid(gt))
    y = jnp.dot(mix.astype(BF16), w_ref[...], preferred_element_type=F32)
    out_ref[0] = x_ref[0] + gl_ref[0] * y


def _merge(o, f, gate, x, gl, w, tm):
    b, t, d = x.shape
    tm = min(tm, t)
    half = o.shape[2]
    tok = lambda n: pl.BlockSpec((1, tm, n), lambda bi, i: (bi, i, 0))
    return pl.pallas_call(
        _merge_body,
        grid=(b, t // tm),
        in_specs=[tok(half), tok(half), tok(d), tok(d),
                  pl.BlockSpec((1, 1, d), lambda bi, i: (bi, 0, 0)),
                  pl.BlockSpec(w.shape, lambda bi, i: (0, 0))],
        out_specs=tok(d),
        out_shape=jax.ShapeDtypeStruct((b, t, d), F32),
        compiler_params=_cparams(("parallel", "parallel")),
        name="merge",
    )(o, f, gate, x, gl, w)


EXP_M05 = math.exp(-0.5)


PAIRS_PER_STEP = 8
PAIRS_PER_GROUP = 8
LOCAL_CHUNKS = 4
GROUP_LAG = 5


def _rwkv_local_body(*refs, cs):
    zwa_ref, w0_ref, w2_ref, a0_ref, a2_ref = refs[3:8]
    zwa = zwa_ref[0].astype(F32)
    lora = (_mm(jnp.tanh(zwa[:, :LANE]), w2_ref[...]) + w0_ref[...],
            _mm(zwa[:, LANE:], a2_ref[...]) + a0_ref[...])
    groups = [_rwkv_local_group(*refs, lora=lora, base=base, ck=ck) for ck in range(cs)
              for base in range(0, PAIRS_PER_STEP, PAIRS_PER_GROUP)]
    tick = 0
    while groups:
        live = groups[:tick // GROUP_LAG + 1]
        for g in live:
            if next(g, StopIteration) is StopIteration:
                groups.remove(g)
        tick += 1


def _rwkv_local_group(zk_ref, zv_ref, zr_ref, zwa_ref, w0_ref, w2_ref, a0_ref, a2_ref,
                      kk_ref, ka_ref, rk_ref, m_ref, g_ref, qt_ref, yl_ref, bn_ref,
                      *, lora, base, ck):
    c = CHUNK
    rows = slice(ck * CHUNK, (ck + 1) * CHUNK)

    head0 = _iota2((1, LANE), 1) < RWKV_HEAD
    r2 = _iota2((LANE, LANE), 0)
    c2 = _iota2((LANE, LANE), 1)
    same = (r2 // RWKV_HEAD) == (c2 // RWKV_HEAD)
    ones_bd = same.astype(F32)
    eye = r2 == c2

    def stack(x):
        z = jnp.zeros_like(x)
        return jnp.concatenate([jnp.where(head0, x, z), jnp.where(head0, z, x)], axis=0)

    stack_b = lambda x: stack(x.astype(BF16))
    fold = lambda x: x[:c] + x[c:]

    pairs = range(PAIRS_PER_GROUP)
    chains = [(q, d) for q in pairs for d in range(2)]
    qls = [slice((base + q) * LANE, (base + q + 1) * LANE) for q in pairs]
    ks = [zk_ref[0, rows, ql].astype(F32) for ql in qls]
    vs_ = [zv_ref[0, rows, ql].astype(F32) for ql in qls]
    rs = [zr_ref[0, rows, ql].astype(F32) for ql in qls]
    pcols = [slice((base + q) * 2 * LANE, (base + q + 1) * 2 * LANE) for q in pairs]
    wraw = [lora[0][rows, pc] for pc in pcols]
    araw = [lora[1][rows, pc] for pc in pcols]
    yield
    logw = [-EXP_M05 * _sigmoid(w) for w in wraw]
    a_all = [_sigmoid(a) for a in araw]
    kk0 = [ks[q] * kk_ref[:, qls[q]] for q in pairs]
    ss = [_mm2r(x * x, ones_bd) for x in kk0]
    yield
    kk = [kk0[q] / jnp.maximum(jnp.sqrt(ss[q]), 1e-12) for q in pairs]
    vstk = [stack_b(v) for v in vs_]

    dsl = [slice(d * LANE, (d + 1) * LANE) for d in range(2)]
    lw = [logw[q][:, dsl[d]] for q, d in chains]
    ad = [a_all[q][:, dsl[d]] for q, d in chains]
    kd = [ks[q] * (1.0 + (ad[i] - 1.0) * ka_ref[:, qls[q]]) for i, (q, d) in enumerate(chains)]
    bb = [kk[q] * ad[i] for i, (q, d) in enumerate(chains)]
    bonus = [_mm2r(rs[q] * (kd[2 * q] + kd[2 * q + 1]) * rk_ref[:, qls[q]], ones_bd) * vs_[q]
             for q in pairs]
    yield
    r3 = _iota2((c, 3 * c), 0)
    c3 = _iota2((c, 3 * c), 1) & (c - 1)
    tri3 = (c3 <= r3).astype(BF16)
    tt = _iota2((c, LANE), 0)
    ts = _iota2((c, LANE), 1) & (c - 1)
    strict = [ts < tt, ts > tt]
    incl = [ts <= tt, ts >= tt]
    eye_c = (ts == tt).astype(F32)

    def prefix(x):
        xh, xl = _split(x)
        xll = (x - xh.astype(F32) - xl.astype(F32)).astype(BF16)
        return jnp.dot(tri3, jnp.concatenate([xh, xl, xll], axis=0), preferred_element_type=F32)

    pre = [prefix(x) for x in logw]
    ltot = [pre[q][c - 1:c, dsl[d]] for q, d in chains]
    lc = [pre[q][:, dsl[0]] if d == 0 else ltot[i] - pre[q][:, dsl[1]] + lw[i]
          for i, (q, d) in enumerate(chains)]
    yield
    n = len(chains)
    bdot = lambda a, b: jnp.dot(a, b, preferred_element_type=F32)
    kkd = [(kk[q] * jnp.exp(lc[i] - lw[i])).astype(BF16) for i, (q, d) in enumerate(chains)]
    rd = [rs[q] * jnp.exp(lc[i]) for i, (q, d) in enumerate(chains)]
    e_inv = [jnp.exp(-x) for x in lc]
    inv_s = [jnp.concatenate([stack_b(bb[i] * e_inv[i]), stack_b(kd[i] * e_inv[i])], axis=0)
             for i in range(n)]
    yield
    amat = [lax.dot_general(jnp.concatenate([kkd[i], rd[i].astype(BF16)], axis=0), inv_s[i],
                            NT, preferred_element_type=F32).astype(BF16) for i in range(n)]
    zero_c = jnp.zeros((c, LANE), BF16)
    a_kb = [jnp.where(strict[d], amat[i][:c, :LANE], zero_c) for i, (q, d) in enumerate(chains)]
    a_kk = [jnp.where(strict[d], amat[i][:c, LANE:], zero_c) for i, (q, d) in enumerate(chains)]
    aq_b = [jnp.where(incl[d], amat[i][c:, :LANE], zero_c) for i, (q, d) in enumerate(chains)]
    aq_k = [jnp.where(incl[d], amat[i][c:, LANE:], zero_c) for i, (q, d) in enumerate(chains)]
    yield
    av = [bdot(jnp.concatenate([a_kk[i], aq_k[i]], axis=0), vstk[q])
          for i, (q, d) in enumerate(chains)]
    yield
    tinv = [eye_c - a.astype(F32) for a in a_kb]
    qpow = [bdot(a, stack(a)).astype(BF16) for a in a_kb]
    yield
    for _ in range(4):
        prod = [bdot(qpow[i], jnp.concatenate([stack(qpow[i]), stack_b(tinv[i])], axis=1))
                for i in range(n)]
        qpow = [x[:, :LANE].astype(BF16) for x in prod]
        tinv = [tinv[i] + prod[i][:, LANE:] for i in range(n)]
        yield
    tinv = [tinv[i] + bdot(qpow[i], stack_b(tinv[i])) for i in range(n)]
    yield
    tsplit = [_split(t) for t in tinv]
    ia_t = [bdot((eye_c + a_kb[i].astype(F32)).astype(BF16),
                 jnp.concatenate([stack(tsplit[i][0]), stack(tsplit[i][1])], axis=1))
            for i in range(n)]
    resid = [eye_c - ia_t[i][:, :LANE] - ia_t[i][:, LANE:] for i in range(n)]
    yield
    tinv = [tinv[i] + bdot(tinv[i].astype(BF16), stack_b(resid[i])) for i in range(n)]
    yield
    x = [bdot(tinv[i].astype(BF16),
              jnp.concatenate([stack(kkd[i]), stack_b(av[i][:c])], axis=1))
         for i in range(n)]
    xb = [v.astype(BF16) for v in x]
    yield
    qy = [jnp.concatenate([rd[i], av[i][c:]], axis=1)
          - bdot(aq_b[i], jnp.concatenate([stack(xb[i][:, :LANE]), stack(xb[i][:, LANE:])],
                                          axis=1)) for i in range(n)]
    yield
    e_end = [jnp.exp(ltot[i] - lc[i]) for i in range(n)]
    ends = [jnp.concatenate([(-bb[i] * e_end[i]).astype(BF16), (kd[i] * e_end[i]).astype(BF16)],
                            axis=0) for i in range(n)]
    wuv = [jnp.concatenate([xb[i], jnp.concatenate([zero_c, vs_[q].astype(BF16)], axis=1)],
                           axis=0) for i, (q, d) in enumerate(chains)]
    mg = [lax.dot_general(ends[i], wuv[i], TN, preferred_element_type=F32) for i in range(n)]
    yield
    for i, (q, d) in enumerate(chains):
        m_ref[0, ck, d, base + q] = fold(jnp.where(eye, jnp.exp(ltot[i]), 0.0)
                                         + jnp.where(same, mg[i][:, :LANE], 0.0)).astype(BF16)
        g_ref[0, ck, d, base + q] = fold(jnp.where(same, mg[i][:, LANE:], 0.0)
                                         ).astype(g_ref.dtype)
        qt_ref[0, d, rows, qls[q]] = qy[i][:, :LANE].astype(BF16)
    for q in pairs:
        yl_ref[0, rows, qls[q]] = (qy[2 * q][:, LANE:]
                                   + qy[2 * q + 1][:, LANE:]).astype(yl_ref.dtype)
        bn_ref[0, rows, qls[q]] = bonus[q].astype(bn_ref.dtype)


def _rwkv_local(z, w0p, w2p, a0p, a2p, k_k, k_a, r_k):
    b, t, _ = z.shape
    w = k_k.shape[1]
    npair = w // LANE
    pp = PAIRS_PER_STEP
    ng = npair // pp
    wl = pp * LANE
    nc = t // CHUNK
    cs = LOCAL_CHUNKS if nc % LOCAL_CHUNKS == 0 else 1
    rows = cs * CHUNK
    tokc = lambda base: pl.BlockSpec((1, rows, wl), lambda bi, ci, p: (bi, ci, base + p))
    perp3 = lambda n: pl.BlockSpec((n, pp * 2 * LANE), lambda bi, ci, p: (0, p))
    vecp = pl.BlockSpec((1, wl), lambda bi, ci, p: (0, p))
    mat = pl.BlockSpec((1, cs, 2, pp, CHUNK, LANE), lambda bi, ci, p: (bi, ci, 0, p, 0, 0))
    return pl.pallas_call(
        functools.partial(_rwkv_local_body, cs=cs),
        grid=(b, nc // cs, ng),
        in_specs=[tokc(0), tokc(ng), tokc(2 * ng),
                  pl.BlockSpec((1, rows, 2 * LANE), lambda bi, ci, p: (bi, ci, 3 * npair // 2)),
                  perp3(1), perp3(LANE), perp3(1), perp3(LANE), vecp, vecp, vecp],
        out_specs=[mat, mat,
                   pl.BlockSpec((1, 2, rows, wl), lambda bi, ci, p: (bi, 0, ci, p)),
                   pl.BlockSpec((1, rows, wl), lambda bi, ci, p: (bi, ci, p)),
                   pl.BlockSpec((1, rows, wl), lambda bi, ci, p: (bi, ci, p))],
        out_shape=[jax.ShapeDtypeStruct((b, nc, 2, npair, CHUNK, LANE), BF16),
                   jax.ShapeDtypeStruct((b, nc, 2, npair, CHUNK, LANE), ACT),
                   jax.ShapeDtypeStruct((b, 2, t, w), BF16),
                   jax.ShapeDtypeStruct((b, t, w), ACT),
                   jax.ShapeDtypeStruct((b, t, w), ACT)],
        compiler_params=_cparams(("parallel", "parallel", "parallel")),
        name="rwkv_local",
    )(z, z, z, z, w0p, w2p, a0p, a2p, k_k, k_a, r_k)


SCAN_CHUNKS = 8


def _rwkv_scan_body(m0_ref, g0_ref, q0_ref, m1_ref, g1_ref, q1_ref, h0_ref,
                    y0_ref, y1_ref, hfin_ref, h_scr, *, npair, cs):
    ci = pl.program_id(1)

    @pl.when(ci == 0)
    def _():
        h_scr[...] = h0_ref[0]

    head0 = _iota2((1, LANE), 1) < RWKV_HEAD

    def expand(x):
        z = jnp.zeros_like(x)
        return jnp.concatenate([jnp.where(head0, x, z), jnp.where(head0, z, x)], axis=0)

    refs = ((m0_ref, g0_ref, q0_ref, y0_ref), (m1_ref, g1_ref, q1_ref, y1_ref))
    chains = [(d, p) for d in range(2) for p in range(npair)]
    lanes = [slice(p * LANE, (p + 1) * LANE) for p in range(npair)]
    h = [h_scr[d, p] for d, p in chains]
    for step in range(cs):
        ck = (step, cs - 1 - step)
        rows = [slice(c * CHUNK, (c + 1) * CHUNK) for c in ck]
        hb = [x.astype(BF16) for x in h]
        res = [jnp.dot(jnp.concatenate([refs[d][2][0, 0, rows[d], lanes[p]],
                                        refs[d][0][0, ck[d], 0, p]], axis=0), hb[i],
                       preferred_element_type=F32) for i, (d, p) in enumerate(chains)]
        for i, (d, p) in enumerate(chains):
            refs[d][3][0, rows[d], lanes[p]] = res[i][:CHUNK].astype(refs[d][3].dtype)
        h = [expand(res[i][CHUNK:] + refs[d][1][0, ck[d], 0, p].astype(F32))
             for i, (d, p) in enumerate(chains)]
    for i, (d, p) in enumerate(chains):
        h_scr[d, p] = h[i]

    @pl.when(ci == pl.num_programs(1) - 1)
    def _():
        hfin_ref[0] = h_scr[...]


def _rwkv_scan(mm, gg, qt, h0):
    b, nc, _, npair, _, _ = mm.shape
    t, w = qt.shape[2], qt.shape[3]
    cs = SCAN_CHUNKS if nc % SCAN_CHUNKS == 0 else 1
    nb = nc // cs
    fwd = lambda bi, ci: (bi, ci, 0, 0, 0, 0)
    rev = lambda bi, ci: (bi, nb - 1 - ci, 1, 0, 0, 0)
    mblk = (1, cs, 1, npair, CHUNK, LANE)
    hspec = pl.BlockSpec((1, 2, npair, LANE, LANE), lambda bi, ci: (bi, 0, 0, 0, 0))
    return pl.pallas_call(
        functools.partial(_rwkv_scan_body, npair=npair, cs=cs),
        grid=(b, nb),
        in_specs=[pl.BlockSpec(mblk, fwd), pl.BlockSpec(mblk, fwd),
                  pl.BlockSpec((1, 1, cs * CHUNK, w), lambda bi, ci: (bi, 0, ci, 0)),
                  pl.BlockSpec(mblk, rev), pl.BlockSpec(mblk, rev),
                  pl.BlockSpec((1, 1, cs * CHUNK, w), lambda bi, ci: (bi, 1, nb - 1 - ci, 0)),
                  hspec],
        out_specs=[pl.BlockSpec((1, cs * CHUNK, w), lambda bi, ci: (bi, ci, 0)),
                   pl.BlockSpec((1, cs * CHUNK, w), lambda bi, ci: (bi, nb - 1 - ci, 0)),
                   hspec],
        out_shape=[jax.ShapeDtypeStruct((b, t, w), ACT), jax.ShapeDtypeStruct((b, t, w), ACT),
                   jax.ShapeDtypeStruct(h0.shape, F32)],
        scratch_shapes=[pltpu.VMEM((2, npair, LANE, LANE), F32)],
        compiler_params=_cparams(("parallel", "arbitrary")),
        name="rwkv_scan",
    )(mm, gg, qt, mm, gg, qt, h0)


def _rwkv_out_body(y0_ref, y1_ref, yl_ref, bn_ref, gate_ref, x_ref, gl_ref, gnw_ref, gnb_ref,
                   w_ref, o_ref):
    y = y0_ref[0].astype(F32) + y1_ref[0].astype(F32) + yl_ref[0].astype(F32)
    r2 = _iota2((LANE, LANE), 0)
    c2 = _iota2((LANE, LANE), 1)
    avg = ((r2 // RWKV_HEAD) == (c2 // RWKV_HEAD)).astype(F32) * (1.0 / RWKV_HEAD)
    parts = []
    for p in range(y.shape[1] // LANE):
        yp = y[:, p * LANE:(p + 1) * LANE]
        dl = yp - _mm2r(yp, avg)
        var = _mm2r(dl * dl, avg)
        parts.append(dl * lax.rsqrt(var + GN_EPS))
    yn = jnp.concatenate(parts, axis=1)
    gt = gate_ref[0].astype(F32)
    act = (yn * gnw_ref[...] + gnb_ref[...] + bn_ref[0].astype(F32)) * (gt * _sigmoid(gt))
    out = jnp.dot(act.astype(BF16), w_ref[...], preferred_element_type=F32)
    o_ref[0] = x_ref[0] + gl_ref[0] * out


def _rwkv_out(y0, y1, yl, bn, gate, x, gl, gnw, gnb, w, tm):
    b, t, d = x.shape
    tm = min(tm, t)
    wd = y0.shape[2]
    tok = lambda n: pl.BlockSpec((1, tm, n), lambda bi, i: (bi, i, 0))
    return pl.pallas_call(
        _rwkv_out_body,
        grid=(b, t // tm),
        in_specs=[tok(wd), tok(wd), tok(wd), tok(wd), tok(wd), tok(d),
                  pl.BlockSpec((1, 1, d), lambda bi, i: (bi, 0, 0)),
                  pl.BlockSpec((1, wd), lambda bi, i: (0, 0)),
                  pl.BlockSpec((1, wd), lambda bi, i: (0, 0)),
                  pl.BlockSpec(w.shape, lambda bi, i: (0, 0))],
        out_specs=tok(d),
        out_shape=jax.ShapeDtypeStruct((b, t, d), F32),
        compiler_params=_cparams(("parallel", "parallel")),
        name="rwkv_out",
    )(y0, y1, yl, bn, gate, x, gl, gnw, gnb, w)


def _rope_tables(t):
    rows = t // GRID_W
    row = jnp.repeat(jnp.arange(rows, dtype=F32), GRID_W)
    col = jnp.tile(jnp.arange(GRID_W, dtype=F32), rows)
    inv = 1.0 / (ROPE_BASE ** (jnp.arange(ROPE_FREQS, dtype=F32) / ROPE_FREQS))
    ang = jnp.stack([row[:, None] * inv, col[:, None] * inv], axis=1)
    cos, sin = jnp.cos(ang), jnp.sin(ang)
    zeros = jnp.zeros_like(sin)
    ones_lo = jnp.ones((t, QK_NOPE), F32)
    pad_hi = HEAD_SLOT - QK_HEAD
    cos_t = jnp.concatenate([ones_lo, jnp.concatenate([cos, cos], axis=2).reshape(t, QK_ROPE),
                             jnp.ones((t, pad_hi), F32)], axis=1)
    sa = jnp.concatenate([jnp.zeros((t, QK_NOPE), F32),
                          jnp.concatenate([-sin, zeros], axis=2).reshape(t, QK_ROPE),
                          jnp.zeros((t, pad_hi), F32)], axis=1)
    sb = jnp.concatenate([jnp.zeros((t, QK_NOPE), F32),
                          jnp.concatenate([zeros, sin], axis=2).reshape(t, QK_ROPE),
                          jnp.zeros((t, pad_hi), F32)], axis=1)
    return cos_t, sa, sb


def _even_layer(x, ctx, mod_l, mod_c, need_ctx, g, w_in, kv_norm, q_norm, w_uq, w_ukv,
                q_head_norm, k_head_norm, w_fnet, w_out):
    b, s, d = x.shape
    tc = ctx.shape[1]
    e_q0 = KV_LORA + QK_ROPE
    e_f0 = e_q0 + Q_LORA
    e_g0 = e_f0 + FNET_GROUPS * FNET_GROUP_DIM
    w_p = jnp.concatenate([w_in[:, e_g0:], w_in[:, e_f0:e_g0], w_in[:, :e_q0],
                           jnp.zeros((d, LANE - QK_ROPE), F32), w_in[:, e_q0:e_f0]],
                          axis=1).astype(BF16)
    splits = (d, (FNET_GROUPS, FNET_GROUP_DIM), KV_LORA + LANE + Q_LORA)
    kvw = w_ukv.reshape(KV_LORA, MLA_HEADS, QK_NOPE + V_HEAD)
    wk = jnp.pad(kvw[:, :, :QK_NOPE], ((0, 0), (0, 0), (0, HEAD_SLOT - QK_NOPE)))
    wk = wk.reshape(KV_LORA, MLA_HEADS * HEAD_SLOT).astype(BF16)
    wv = jnp.pad(kvw[:, :, QK_NOPE:], ((0, 0), (0, 0), (0, HEAD_SLOT - V_HEAD)))
    wv = wv.reshape(KV_LORA, MLA_HEADS * HEAD_SLOT).astype(BF16)
    wq3 = jnp.pad(w_uq.reshape(Q_LORA, MLA_HEADS, QK_HEAD), ((0, 0), (0, 0), (0, HEAD_SLOT - QK_HEAD)))
    wq = wq3.reshape(Q_LORA, MLA_HEADS * HEAD_SLOT).astype(BF16)
    kg = jnp.pad(k_head_norm, (0, HEAD_SLOT - QK_HEAD)).reshape(1, HEAD_SLOT)
    qg = (jnp.pad(q_head_norm, (0, HEAD_SLOT - QK_HEAD))
          * (QK_HEAD ** -0.5 * math.log2(math.e))).reshape(1, HEAD_SLOT)
    lane = np.arange(HEAD_SLOT)
    tail = (lane >= QK_NOPE) & (lane < QK_HEAD)
    first = tail & (((lane - QK_NOPE) // ROPE_FREQS) % 2 == 0)
    partner = np.where(first, lane + ROPE_FREQS, np.where(tail, lane - ROPE_FREQS, lane))
    sign = np.where(first, -1.0, np.where(tail, 1.0, 0.0)).astype(np.float32)
    wqr = (wq3[:, :, partner] * (sign * qg[0, partner])).reshape(Q_LORA, MLA_HEADS * HEAD_SLOT)
    wqr = wqr.astype(BF16)
    kvn, qn = kv_norm.reshape(1, -1), q_norm.reshape(1, -1)
    g2 = g.reshape(1, d)
    bound = (1.02 * QK_HEAD * jnp.max(jnp.abs(qg)) * jnp.max(jnp.abs(kg))).astype(BF16).astype(F32)
    static_ok = bound <= MAX_STATIC_BOUND
    bias_lane = (jnp.arange(HEAD_SLOT) == BIAS_LANE).astype(F32).reshape(1, HEAD_SLOT)
    kb = bias_lane * jnp.where(static_ok, -bound, 0.0)
    qb = bias_lane

    gate_l, four_l, ua_l = _proj(x, g2, mod_l[1], mod_l[0], w_p, splits, TOKEN_TILE)
    gate_c, four_c, ua_c = _proj(ctx, g2, mod_c[1], mod_c[0], w_p, splits, TOKEN_TILE)
    sk = s + tc
    cos_t, sa, sb = _rope_tables(s)
    cos_t = jnp.concatenate([cos_t, jnp.ones((tc, HEAD_SLOT), F32)], axis=0)
    sa = jnp.concatenate([sa, jnp.zeros((tc, HEAD_SLOT), F32)], axis=0)
    sb = jnp.concatenate([sb, jnp.zeros((tc, HEAD_SLOT), F32)], axis=0)
    tabs = (cos_t * kg, cos_t * qg, sa, sb, sb - sa)
    q_all, k_all, v_all = _qkv(ua_l, ua_c, kvn, qn, wk, wv, wq, wqr, kg, kb, qb, tabs,
                               math.gcd(s, tc))
    bk = ATTN_K_BLOCK if sk % ATTN_K_BLOCK == 0 else tc
    o_l = _attention(q_all, k_all, v_all, static_ok, 0, s, 0, sk, ATTN_Q_BLOCK, bk)
    f_l = _fourier_latent(four_l, w_fnet)
    wo = w_out.astype(BF16)
    x_new = _merge(o_l, f_l, gate_l, x, mod_l[2], wo, TOKEN_TILE)
    ctx_new = ctx
    if need_ctx:
        o_c = _attention(q_all, k_all, v_all, static_ok, s, tc, s, tc, tc, tc)
        f_c = _fourier_dense(four_c, w_fnet)
        ctx_new = _merge(o_c, f_c, gate_c, ctx, mod_c[2], wo, TOKEN_TILE)
    return x_new, ctx_new


def _odd_layer(x, ctx, mod_l, mod_c, need_ctx, g, w_in, shift_w, w0, w2, a0, a2, k_k, k_a, r_k,
               gn_w, gn_b, w_out):
    b, s, d = x.shape
    w = k_k.shape[0]
    npair = w // LANE
    o_wd0 = 2 * w
    o_r0 = o_wd0 + 2 * DECAY_LORA + 2 * AAA_LORA
    conv_ch = o_r0 + w
    segs = ((0, o_wd0), (o_r0, w), (o_wd0, o_r0 - o_wd0))
    w_p = w_in.astype(BF16)
    sw = shift_w
    g2 = g.reshape(1, d)

    def pairs(vec2):
        return vec2.reshape(2, npair, LANE).transpose(1, 0, 2).reshape(1, npair * 2 * LANE)

    def pair_mats(m):
        rr = m.shape[1]
        mp = m.reshape(2, rr, npair, LANE).transpose(2, 0, 1, 3)
        z = jnp.zeros_like(mp[:, 0])
        top = jnp.concatenate([mp[:, 0], z], axis=2)
        bot = jnp.concatenate([z, mp[:, 1]], axis=2)
        full = jnp.concatenate([top, bot], axis=1)
        return full.transpose(1, 0, 2).reshape(2 * rr, npair * 2 * LANE).astype(BF16)

    w0p, a0p, w2p, a2p = pairs(w0), pairs(a0), pair_mats(w2), pair_mats(a2)
    kk2, ka2, rk2 = k_k.reshape(1, w), k_a.reshape(1, w), r_k.reshape(1, w)
    wo = w_out.astype(BF16)

    def mix(xin, mod, h0):
        z, gate = _proj_shift(xin, g2, mod[1], mod[0], w_p, sw, conv_ch, segs, TOKEN_TILE)
        mm, gg, qt, yl, bn = _rwkv_local(z, w0p, w2p, a0p, a2p, kk2, ka2, rk2)
        y0, y1, hfin = _rwkv_scan(mm, gg, qt, h0)
        return (y0, y1, yl, bn, gate), hfin

    h_zero = jnp.zeros((b, 2, npair, LANE, LANE), F32)
    parts_c, h_ctx = mix(ctx, mod_c, h_zero)
    parts_l, _ = mix(x, mod_l, h_ctx)
    gnw, gnb = gn_w.reshape(1, w), gn_b.reshape(1, w)
    x_new = _rwkv_out(*parts_l, x, mod_l[2], gnw, gnb, wo, TOKEN_TILE)
    ctx_new = ctx
    if need_ctx:
        ctx_new = _rwkv_out(*parts_c, ctx, mod_c[2], gnw, gnb, wo, TOKEN_TILE)
    return x_new, ctx_new


def kernel(x, c, ctx, c_ctx, ada_w, ada_b, norm_g, e_w_in, e_kv_norm, e_q_norm, e_w_uq, e_w_ukv,
           e_q_head_norm, e_k_head_norm, e_w_fnet, e_w_out, o_w_in, o_shift_w, o_w0, o_w2, o_a0,
           o_a2, o_k_k, o_k_a, o_r_k, o_gn_w, o_gn_b, o_w_out):
    b, s, d = x.shape
    depth = ada_w.shape[0]
    assert b + 1 <= 8
    cond8 = jnp.concatenate([c, c_ctx[None, :], jnp.zeros((8 - b - 1, d), F32)], axis=0)
    mod = _ada(cond8, ada_w, ada_b)
    for layer in range(depth):
        need_ctx = layer < depth - 1
        m = mod[layer]
        chunk = lambda rows, i: rows[:, None, i * d:(i + 1) * d]
        lat, cx = m[:b], jnp.broadcast_to(m[b:b + 1], (b, 3 * d))
        mod_l = (chunk(lat, 0), 1.0 + chunk(lat, 1), chunk(lat, 2))
        mod_c = (chunk(cx, 0), 1.0 + chunk(cx, 1), chunk(cx, 2))
        j = layer // 2
        if layer % 2 == 0:
            x, ctx = _even_layer(x, ctx, mod_l, mod_c, need_ctx, norm_g[layer], e_w_in[j],
                                 e_kv_norm[j], e_q_norm[j], e_w_uq[j], e_w_ukv[j],
                                 e_q_head_norm[j], e_k_head_norm[j], e_w_fnet[j], e_w_out[j])
        else:
            x, ctx = _odd_layer(x, ctx, mod_l, mod_c, need_ctx, norm_g[layer], o_w_in[j],
                                o_shift_w[j], o_w0[j], o_w2[j], o_a0[j], o_a2[j], o_k_k[j],
                                o_k_a[j], o_r_k[j].reshape(-1), o_gn_w[j], o_gn_b[j], o_w_out[j])
    return x
```

```python
import functools
import math

import numpy as np
import jax
import jax.numpy as jnp
from jax import lax
from jax.experimental import pallas as pl
from jax.experimental.pallas import tpu as pltpu

F32 = jnp.float32
BF16 = jnp.bfloat16
ACT = BF16

GRID_W = 64
NORM_EPS = 1e-6
MLA_HEADS = 8
QK_NOPE = 64
QK_ROPE = 32
QK_HEAD = QK_NOPE + QK_ROPE
V_HEAD = 64
Q_LORA = 384
KV_LORA = 256
ROPE_FREQS = QK_ROPE // 4
ROPE_BASE = 10000.0
FNET_GROUPS = 4
FNET_GROUP_DIM = 128
RWKV_HEAD = 64
DECAY_LORA = 64
AAA_LORA = 64
GN_EPS = 64e-5

LANE = 128
CHUNK = 64
HEAD_SLOT = 128
VMEM_LIMIT = 56 * 1024 * 1024

TOKEN_TILE = 512
ATTN_Q_BLOCK = 2048
ATTN_K_BLOCK = 768
FOUR_LANE_TILE = 2048
FOUR_ROWS_TILE = 32

NN = (((1,), (0,)), ((), ()))
NT = (((1,), (1,)), ((), ()))
TN = (((0,), (0,)), ((), ()))


def _cparams(sem):
    return pltpu.CompilerParams(dimension_semantics=sem, vmem_limit_bytes=VMEM_LIMIT)


def _mm(a, b, dn=NN):
    return lax.dot_general(a.astype(BF16), b.astype(BF16), dn, preferred_element_type=F32)


def _split(a):
    hi = a.astype(BF16)
    lo = (a - hi.astype(F32)).astype(BF16)
    return hi, lo


def _mm3(a, b, dn=NN):
    ah, al = _split(a)
    bh, bl = _split(b)
    d = lambda x, y: lax.dot_general(x, y, dn, preferred_element_type=F32)
    return d(ah, bh) + d(al, bh) + d(ah, bl)


def _mm2r(a, b_exact):
    ah, al = _split(a)
    bb = b_exact.astype(BF16)
    return jnp.dot(jnp.concatenate([ah, al], axis=1), jnp.concatenate([bb, bb], axis=0),
                   preferred_element_type=F32)


def _mm2l(a_exact, b):
    bh, bl = _split(b)
    n = b.shape[1]
    y = jnp.dot(a_exact.astype(BF16), jnp.concatenate([bh, bl], axis=1),
                preferred_element_type=F32)
    return y[:, :n] + y[:, n:]


def _sigmoid(x):
    return 1.0 / (1.0 + jnp.exp(-x))


def _modnorm(x, g, sc1, sh):
    y = x * lax.rsqrt(jnp.mean(x * x, axis=-1, keepdims=True) + NORM_EPS)
    return (y * g) * sc1 + sh


def _iota2(shape, dim):
    return lax.broadcasted_iota(jnp.int32, shape, dim)


def _ada_body(c_ref, w_ref, b_ref, o_ref):
    c = c_ref[...]
    s = c * _sigmoid(c)
    o_ref[0] = _mm3(s, w_ref[0]) + b_ref[0]


def _ada(cond8, ada_w, ada_b):
    depth, d, n = ada_w.shape
    tn = 512
    return pl.pallas_call(
        _ada_body,
        grid=(depth, n // tn),
        in_specs=[
            pl.BlockSpec((8, d), lambda l, j: (0, 0)),
            pl.BlockSpec((1, d, tn), lambda l, j: (l, 0, j)),
            pl.BlockSpec((1, 1, tn), lambda l, j: (l, 0, j)),
        ],
        out_specs=pl.BlockSpec((1, 8, tn), lambda l, j: (l, 0, j)),
        out_shape=jax.ShapeDtypeStruct((depth, 8, n), F32),
        compiler_params=_cparams(("parallel", "parallel")),
        name="ada",
    )(cond8, ada_w, ada_b.reshape(depth, 1, n))


COL_CHUNK = 512


def _proj_body(x_ref, g_ref, sc_ref, sh_ref, w_ref, *o_refs, splits):
    h = _modnorm(x_ref[0], g_ref[...], sc_ref[0], sh_ref[0]).astype(BF16)
    mm = lambda c0, c1: jnp.dot(h, w_ref[:, c0:c1], preferred_element_type=F32)
    off = 0
    for o_ref, n in zip(o_refs, splits):
        if isinstance(n, tuple):
            groups, width = n
            y = mm(off, off + groups * width).astype(o_ref.dtype)
            for gi in range(groups):
                o_ref[0, gi] = y[:, gi * width:(gi + 1) * width]
            off += groups * width
            continue
        for c0 in range(0, n, COL_CHUNK):
            c1 = min(n, c0 + COL_CHUNK)
            o_ref[0, :, c0:c1] = mm(off + c0, off + c1).astype(o_ref.dtype)
        off += n


def _proj(x, g, sc1, sh, w, splits, tm):
    b, t, d = x.shape
    tm = min(tm, t)
    n = w.shape[1]
    vec = pl.BlockSpec((1, 1, d), lambda bi, i: (bi, 0, 0))
    specs, shapes = [], []
    for s in splits:
        if isinstance(s, tuple):
            specs.append(pl.BlockSpec((1, s[0], tm, s[1]), lambda bi, i: (bi, 0, i, 0)))
            shapes.append(jax.ShapeDtypeStruct((b, s[0], t, s[1]), ACT))
        else:
            specs.append(pl.BlockSpec((1, tm, s), lambda bi, i: (bi, i, 0)))
            shapes.append(jax.ShapeDtypeStruct((b, t, s), ACT))
    return pl.pallas_call(
        functools.partial(_proj_body, splits=splits),
        grid=(b, t // tm),
        in_specs=[
            pl.BlockSpec((1, tm, d), lambda bi, i: (bi, i, 0)),
            pl.BlockSpec((1, d), lambda bi, i: (0, 0)),
            vec, vec,
            pl.BlockSpec((d, n), lambda bi, i: (0, 0)),
        ],
        out_specs=specs,
        out_shape=shapes,
        compiler_params=_cparams(("parallel", "parallel")),
        name="proj",
    )(x, g, sc1, sh, w)


HALO = 16


def _proj_shift_body(x_ref, xp_ref, xn_ref, g_ref, sc_ref, sh_ref, w_ref, sw_ref, z_ref, gate_ref,
                     *, tm, n_conv, segs):
    i = pl.program_id(1)
    last = pl.num_programs(1) - 1
    g, sc1, sh = g_ref[...], sc_ref[0], sh_ref[0]
    h = _modnorm(x_ref[0], g, sc1, sh)
    hp = _modnorm(xp_ref[0], g, sc1, sh) * (i > 0).astype(F32)
    hn = _modnorm(xn_ref[0], g, sc1, sh) * (i < last).astype(F32)
    hb = jnp.concatenate([hp, h, hn], axis=0).astype(BF16)
    rows = tm + 2 * HALO
    dst = 0
    for src, width in segs:
        for c0 in range(0, width, COL_CHUNK):
            cw = min(COL_CHUNK, width - c0)
            cols = slice(src + c0, src + c0 + cw)
            u = jnp.dot(hb, w_ref[:, cols], preferred_element_type=F32)
            up = pltpu.roll(u, 1, 0)[HALO:HALO + tm]
            un = pltpu.roll(u, rows - 1, 0)[HALO:HALO + tm]
            um = u[HALO:HALO + tm]
            z_ref[0, :, dst + c0:dst + c0 + cw] = (
                sw_ref[0:1, cols] * up + sw_ref[1:2, cols] * um
                + sw_ref[2:3, cols] * un).astype(z_ref.dtype)
        dst += width
    hc = hb[HALO:HALO + tm]
    n_all = w_ref.shape[1]
    for c0 in range(n_conv, n_all, COL_CHUNK):
        c1 = min(n_all, c0 + COL_CHUNK)
        gate_ref[0, :, c0 - n_conv:c1 - n_conv] = jnp.dot(
            hc, w_ref[:, c0:c1], preferred_element_type=F32).astype(gate_ref.dtype)


def _proj_shift(x, g, sc1, sh, w, sw, n_conv, segs, tm):
    b, t, d = x.shape
    tm = min(tm, t)
    n = w.shape[1]
    hb = tm // HALO
    nhb = t // HALO
    vec = pl.BlockSpec((1, 1, d), lambda bi, i: (bi, 0, 0))
    return pl.pallas_call(
        functools.partial(_proj_shift_body, tm=tm, n_conv=n_conv, segs=segs),
        grid=(b, t // tm),
        in_specs=[
            pl.BlockSpec((1, tm, d), lambda bi, i: (bi, i, 0)),
            pl.BlockSpec((1, HALO, d), lambda bi, i: (bi, jnp.maximum(i * hb - 1, 0), 0)),
            pl.BlockSpec((1, HALO, d), lambda bi, i: (bi, jnp.minimum((i + 1) * hb, nhb - 1), 0)),
            pl.BlockSpec((1, d), lambda bi, i: (0, 0)),
            vec, vec,
            pl.BlockSpec((d, n), lambda bi, i: (0, 0)),
            pl.BlockSpec((3, n_conv), lambda bi, i: (0, 0)),
        ],
        out_specs=[pl.BlockSpec((1, tm, n_conv), lambda bi, i: (bi, i, 0)),
                   pl.BlockSpec((1, tm, n - n_conv), lambda bi, i: (bi, i, 0))],
        out_shape=[jax.ShapeDtypeStruct((b, t, n_conv), ACT),
                   jax.ShapeDtypeStruct((b, t, n - n_conv), ACT)],
        compiler_params=_cparams(("parallel", "parallel")),
        name="proj_shift",
    )(x, x, x, g, sc1, sh, w, sw)


def _rms(x, g):
    return x * lax.rsqrt(jnp.mean(x * x, axis=-1, keepdims=True) + NORM_EPS) * g


def _qkv_body(ual_ref, uac_ref, kvn_ref, qn_ref, wk_ref, wv_ref, wq_ref, wqr_ref, kg_ref, kb_ref,
              qb_ref, cosk_ref, cosq_ref, sa_ref, sb_ref, sinq_ref, q_ref, k_ref, v_ref, *, n_lat):
    ua = jnp.where(pl.program_id(1) < n_lat, ual_ref[0], uac_ref[0]).astype(F32)
    ckv = _rms(ua[:, :KV_LORA], kvn_ref[...]).astype(BF16)
    kr = ua[:, KV_LORA:KV_LORA + LANE]
    cq = _rms(ua[:, KV_LORA + LANE:], qn_ref[...]).astype(BF16)
    kn = jnp.dot(ckv, wk_ref[...], preferred_element_type=F32)
    vv = jnp.dot(ckv, wv_ref[...], preferred_element_type=F32)
    qq = jnp.dot(cq, wq_ref[...], preferred_element_type=F32)
    qr = jnp.dot(cq, wqr_ref[...], preferred_element_type=F32)
    ones_hi = (_iota2((1, HEAD_SLOT), 1) >= V_HEAD).astype(F32)
    pe = pltpu.roll(kr, QK_NOPE, 1)
    gp = pe * kg_ref[...]
    pe_rot = (pltpu.roll(gp, LANE - ROPE_FREQS, 1) * sa_ref[...]
              + pltpu.roll(gp, ROPE_FREQS, 1) * sb_ref[...])
    cosk, cosq, sinq = cosk_ref[...], cosq_ref[...], sinq_ref[...]
    inv_n = 1.0 / QK_HEAD
    scale = lambda x: lax.rsqrt(jnp.sum(x * x, axis=-1, keepdims=True) * inv_n + NORM_EPS)

    for h in range(MLA_HEADS):
        sl = slice(h * HEAD_SLOT, (h + 1) * HEAD_SLOT)
        kh = kn[:, sl] + pe
        k_ref[0, h] = (scale(kh) * (kh * cosk + pe_rot) + kb_ref[...]).astype(BF16)
        qh = qq[:, sl]
        q_ref[0, h] = (scale(qh) * (qh * cosq + qr[:, sl] * sinq) + qb_ref[...]).astype(BF16)
        v_ref[0, h] = (vv[:, sl] + ones_hi).astype(BF16)


def _qkv(ua_l, ua_c, kvn, qn, wk, wv, wq, wqr, kg, kb, qb, tabs, tm):
    b, s, wa = ua_l.shape
    tc = ua_c.shape[1]
    nl, ncx = s // tm, tc // tm
    full = lambda a: pl.BlockSpec(a.shape, lambda bi, i: (0,) * a.ndim)
    tab = pl.BlockSpec((tm, LANE), lambda bi, i: (i, 0))
    head = pl.BlockSpec((1, MLA_HEADS, tm, HEAD_SLOT), lambda bi, i: (bi, 0, i, 0))
    shape = jax.ShapeDtypeStruct((b, MLA_HEADS, s + tc, HEAD_SLOT), BF16)
    return pl.pallas_call(
        functools.partial(_qkv_body, n_lat=nl),
        grid=(b, nl + ncx),
        in_specs=[pl.BlockSpec((1, tm, wa), lambda bi, i: (bi, jnp.minimum(i, nl - 1), 0)),
                  pl.BlockSpec((1, tm, wa), lambda bi, i: (bi, jnp.maximum(i - nl, 0), 0)),
                  full(kvn), full(qn), full(wk), full(wv), full(wq), full(wqr), full(kg),
                  full(kb), full(qb)] + [tab] * len(tabs),
        out_specs=[head, head, head],
        out_shape=[shape, shape, shape],
        compiler_params=_cparams(("parallel", "parallel")),
        name="qkv",
    )(ua_l, ua_c, kvn, qn, wk, wv, wq, wqr, kg, kb, qb, *tabs)


BIAS_LANE = QK_HEAD
MAX_STATIC_BOUND = 50.0


def _attn_finish(acc_ref, o_ref):
    bq = acc_ref.shape[1]
    lane = _iota2((bq, LANE), 1)
    o0 = acc_ref[0] / pltpu.roll(acc_ref[0], V_HEAD, 1)
    o1 = acc_ref[1] / pltpu.roll(acc_ref[1], V_HEAD, 1)
    o_ref[0] = jnp.where(lane < V_HEAD, o0, pltpu.roll(o1, V_HEAD, 1)).astype(o_ref.dtype)


def _attn_static_body(q_ref, k_ref, v_ref, o_ref, acc_ref, p0_ref, p1_ref, *, bk):
    nk = k_ref.shape[2] // bk
    krows = lambda j: pl.ds(pl.multiple_of(j * bk, bk), bk)
    bufs = (p0_ref, p1_ref)

    def weights(j, slot):
        for hh in range(2):
            s = lax.dot_general(q_ref[0, hh], k_ref[0, hh, krows(j), :], NT,
                                preferred_element_type=F32)
            bufs[slot][hh] = jnp.exp2(s.astype(BF16))

    def values(j, slot):
        for hh in range(2):
            acc_ref[hh] += jnp.dot(bufs[slot][hh], v_ref[0, hh, krows(j), :],
                                   preferred_element_type=F32)

    acc_ref[...] = jnp.zeros(acc_ref.shape, F32)
    weights(0, 0)

    def two_blocks(jj, carry):
        j = 2 * jj
        weights(j + 1, 1)
        values(j, 0)
        weights(j + 2, 0)
        values(j + 1, 1)
        return carry

    pairs_done = (nk - 1) // 2
    lax.fori_loop(0, pairs_done, two_blocks, 0)
    j = 2 * pairs_done
    if (nk - 1) % 2:
        weights(j + 1, 1)
        values(j, 0)
        values(j + 1, 1)
    else:
        values(j, 0)
    _attn_finish(acc_ref, o_ref)


def _attn_online_body(q_ref, k_ref, v_ref, o_ref, acc_ref, m_ref, *, bk):
    acc_ref[...] = jnp.zeros(acc_ref.shape, F32)
    m_ref[...] = jnp.full(m_ref.shape, -jnp.inf, F32)

    def step(j, carry):
        rows = pl.ds(pl.multiple_of(j * bk, bk), bk)
        for hh in range(2):
            s = lax.dot_general(q_ref[0, hh], k_ref[0, hh, rows, :], NT,
                                preferred_element_type=F32)
            m_prev = m_ref[hh]
            m_new = jnp.maximum(m_prev, jnp.max(s, axis=-1, keepdims=True))
            p = jnp.exp2(s - m_new)
            acc_ref[hh] = (jnp.exp2(m_prev - m_new) * acc_ref[hh]
                           + jnp.dot(p.astype(BF16), v_ref[0, hh, rows, :],
                                     preferred_element_type=F32))
            m_ref[hh] = m_new
        return carry

    lax.fori_loop(0, k_ref.shape[2] // bk, step, 0)
    _attn_finish(acc_ref, o_ref)


def _attention(q, k, v, static_ok, q_start, q_rows, k_start, k_rows, bq, bk):
    b, h, _, e = q.shape
    bq, bk = min(bq, q_rows), min(bk, k_rows)
    qi0, kj0 = q_start // bq, k_start // k_rows
    kv_blk = pl.BlockSpec((1, 2, k_rows, e), lambda bi, p, i: (bi, p, kj0, 0))

    def call(online):
        scratch = [pltpu.VMEM((2, bq, LANE), F32)]
        if online:
            scratch.append(pltpu.VMEM((2, bq, 1), F32))
        else:
            scratch += [pltpu.VMEM((2, bq, bk), BF16)] * 2
        return pl.pallas_call(
            functools.partial(_attn_online_body if online else _attn_static_body, bk=bk),
            grid=(b, h // 2, q_rows // bq),
            in_specs=[pl.BlockSpec((1, 2, bq, e), lambda bi, p, i: (bi, p, qi0 + i, 0)),
                      kv_blk, kv_blk],
            out_specs=pl.BlockSpec((1, bq, 2 * V_HEAD), lambda bi, p, i: (bi, i, p)),
            out_shape=jax.ShapeDtypeStruct((b, q_rows, h * V_HEAD), ACT),
            scratch_shapes=scratch,
            compiler_params=_cparams(("parallel", "parallel", "arbitrary")),
            name="attention_online" if online else "attention",
        )(q, k, v)

    return lax.cond(static_ok, lambda: call(False), lambda: call(True))


def _dft_mats(n):
    idx = np.arange(n)
    ang = 2.0 * np.pi * ((idx[:, None] * idx[None, :]) % n) / n
    return np.cos(ang), np.sin(ang)


def _hilo(a):
    a = jnp.asarray(a, F32)
    hi = a.astype(BF16)
    return hi, (a - hi.astype(F32)).astype(BF16)


def _mm3c(ah, al, b, dn=NN):
    bh, bl = _split(b)
    d = lambda x, y: lax.dot_general(x, y, dn, preferred_element_type=F32)
    return d(ah, bh) + d(al, bh) + d(ah, bl)


def _four_rows_body(x_ref, w_ref, tc_ref, ts_ref, o_ref):
    r = tc_ref.shape[0]
    y = jnp.dot(w_ref[...], x_ref[0, 0], preferred_element_type=F32)
    yc, ys = y[:r], y[r:]
    tc, ts = tc_ref[...], ts_ref[...]
    o_ref[0, 0, :r] = (yc * tc - ys * ts).astype(o_ref.dtype)
    o_ref[0, 0, r:] = (yc * ts + ys * tc).astype(o_ref.dtype)


def _four_cols_body(y_ref, w_ref, cs_ref, wf_ref, o_ref, y3_scr, *, krt, scale):
    def one(j, carry):
        rows = pl.ds(pl.multiple_of(j * GRID_W, GRID_W), GRID_W)
        ycs = jnp.concatenate([y_ref[0, 0, 0, rows, :], y_ref[0, 0, 1, rows, :]], axis=0)
        y3 = jnp.dot(w_ref[...], ycs, preferred_element_type=F32)
        y3_scr[rows, :] = jnp.concatenate([y3[:GRID_W], y3[GRID_W:]], axis=1).astype(BF16)
        return carry

    lax.fori_loop(0, krt, one, 0, unroll=8)
    f = jnp.dot(y3_scr[...], cs_ref[...], preferred_element_type=F32) * scale
    o_ref[0, 0] = _mm(f, wf_ref[0]).astype(o_ref.dtype)


def _fourier_latent(xf, w_fnet):
    b, g, t, gd = xf.shape
    r = t // GRID_W
    wide = GRID_W * gd
    xv = xf.reshape(b, g, r, wide)
    cr, sr = _dft_mats(r)
    w_rows = jnp.asarray(np.concatenate([cr, sr], axis=0), BF16)
    kr_i, c_i = np.arange(r)[:, None], np.arange(GRID_W)[None, :]
    ang = 2.0 * np.pi * ((kr_i * c_i) % t) / t
    twc = jnp.repeat(jnp.asarray(np.cos(ang), F32), gd, axis=1)
    tws = jnp.repeat(jnp.asarray(np.sin(ang), F32), gd, axis=1)
    tl = min(FOUR_LANE_TILE, wide)
    y2 = pl.pallas_call(
        _four_rows_body,
        grid=(b, g, wide // tl),
        in_specs=[pl.BlockSpec((1, 1, r, tl), lambda bi, gi, l: (bi, gi, 0, l)),
                  pl.BlockSpec((2 * r, r), lambda bi, gi, l: (0, 0)),
                  pl.BlockSpec((r, tl), lambda bi, gi, l: (0, l)),
                  pl.BlockSpec((r, tl), lambda bi, gi, l: (0, l))],
        out_specs=pl.BlockSpec((1, 1, 2 * r, tl), lambda bi, gi, l: (bi, gi, 0, l)),
        out_shape=jax.ShapeDtypeStruct((b, g, 2 * r, wide), ACT),
        compiler_params=_cparams(("parallel", "parallel", "parallel")),
        name="fourier_rows",
    )(xv, w_rows, twc, tws)
    y2v = y2.reshape(b, g, 2, r * GRID_W, gd)
    c64, s64 = _dft_mats(GRID_W)
    w_cols = jnp.asarray(np.block([[c64, -s64], [s64, c64]]), BF16)
    cc, sc = _dft_mats(gd)
    w_chan = jnp.asarray(np.concatenate([cc, -sc], axis=0), BF16)
    krt = min(FOUR_ROWS_TILE, r)
    const = lambda a: pl.BlockSpec(a.shape, lambda bi, gi, i: (0, 0))
    fo = pl.pallas_call(
        functools.partial(_four_cols_body, krt=krt, scale=1.0 / math.sqrt(t * gd)),
        grid=(b, g, r // krt),
        in_specs=[pl.BlockSpec((1, 1, 2, krt * GRID_W, gd), lambda bi, gi, i: (bi, gi, 0, i, 0)),
                  const(w_cols), const(w_chan),
                  pl.BlockSpec((1, gd, gd), lambda bi, gi, i: (gi, 0, 0))],
        out_specs=pl.BlockSpec((1, 1, krt * GRID_W, gd), lambda bi, gi, i: (bi, gi, i, 0)),
        out_shape=jax.ShapeDtypeStruct((b, g, r * GRID_W, gd), ACT),
        scratch_shapes=[pltpu.VMEM((krt * GRID_W, 2 * gd), BF16)],
        compiler_params=_cparams(("parallel", "parallel", "parallel")),
        name="fourier_cols",
    )(y2v, w_cols, w_chan, w_fnet)
    return fo.reshape(b, g, r, GRID_W, gd).transpose(0, 1, 3, 2, 4).reshape(b, g, t, gd)


def _four_dense_body(x_ref, ch_ref, cl_ref, th_ref, tl_ref, sh_ref, sl_ref, wf_ref, o_ref, *, scale):
    x = x_ref[0, 0]
    xh, xl = _split(x)
    d = lambda a, b: jnp.dot(a, b, preferred_element_type=F32)
    z = d(xh, ch_ref[...]) + d(xl, ch_ref[...]) + d(xh, cl_ref[...])
    zc, zs = z[:, :FNET_GROUP_DIM], z[:, FNET_GROUP_DIM:]
    f = (_mm3c(th_ref[...], tl_ref[...], zc) - _mm3c(sh_ref[...], sl_ref[...], zs)) * scale
    o_ref[0, 0] = _mm3(f, wf_ref[0]).astype(o_ref.dtype)


def _fourier_dense(xf, w_fnet):
    b, g, t, gd = xf.shape
    cc, sc = _dft_mats(gd)
    ch, cl = _hilo(np.concatenate([cc, sc], axis=1))
    ct, st = _dft_mats(t)
    cth, ctl = _hilo(ct)
    sth, stl = _hilo(st)
    sq = pl.BlockSpec((t, t), lambda bi, gi: (0, 0))
    cs = pl.BlockSpec((gd, 2 * gd), lambda bi, gi: (0, 0))
    return pl.pallas_call(
        functools.partial(_four_dense_body, scale=1.0 / math.sqrt(t * gd)),
        grid=(b, g),
        in_specs=[pl.BlockSpec((1, 1, t, gd), lambda bi, gi: (bi, gi, 0, 0)), cs, cs, sq, sq, sq, sq,
                  pl.BlockSpec((1, gd, gd), lambda bi, gi: (gi, 0, 0))],
        out_specs=pl.BlockSpec((1, 1, t, gd), lambda bi, gi: (bi, gi, 0, 0)),
        out_shape=jax.ShapeDtypeStruct((b, g, t, gd), ACT),
        compiler_params=_cparams(("parallel", "parallel")),
        name="fourier_dense",
    )(xf, ch, cl, cth, ctl, sth, stl, w_fnet)


def _merge_body(o_ref, f_ref, gate_ref, x_ref, gl_ref, w_ref, out_ref):
    gt = gate_ref[0].astype(F32)
    parts = [o_ref[0]] + [f_ref[0, gi] for gi in range(f_ref.shape[1])]
    mix = jnp.concatenate(parts, axis=-1).astype(F32) * (gt * _sigmoid(gt))
    y = jnp.dot(mix.astype(BF16), w_ref[...], preferred_element_type=F32)
    out_ref[0] = x_ref[0] + gl_ref[0] * y


def _merge(o, f, gate, x, gl, w, tm):
    b, t, d = x.shape
    tm = min(tm, t)
    half = o.shape[2]
    tok = lambda n: pl.BlockSpec((1, tm, n), lambda bi, i: (bi, i, 0))
    return pl.pallas_call(
        _merge_body,
        grid=(b, t // tm),
        in_specs=[tok(half),
                  pl.BlockSpec((1, f.shape[1], tm, f.shape[3]), lambda bi, i: (bi, 0, i, 0)),
                  tok(d), tok(d),
                  pl.BlockSpec((1, 1, d), lambda bi, i: (bi, 0, 0)),
                  pl.BlockSpec(w.shape, lambda bi, i: (0, 0))],
        out_specs=tok(d),
        out_shape=jax.ShapeDtypeStruct((b, t, d), F32),
        compiler_params=_cparams(("parallel", "parallel")),
        name="merge",
    )(o, f, gate, x, gl, w)


EXP_M05 = math.exp(-0.5)


PAIRS_PER_STEP = 8
PAIRS_PER_GROUP = 8
LOCAL_CHUNKS = 4
GROUP_LAG = 5


def _rwkv_local_body(*refs, cs):
    zwa_ref, w0_ref, w2_ref, a0_ref, a2_ref = refs[3:8]
    zwa = zwa_ref[0].astype(F32)
    lora = (_mm(jnp.tanh(zwa[:, :LANE]), w2_ref[...]) + w0_ref[...],
            _mm(zwa[:, LANE:], a2_ref[...]) + a0_ref[...])
    groups = [_rwkv_local_group(*refs, lora=lora, base=base, ck=ck) for ck in range(cs)
              for base in range(0, PAIRS_PER_STEP, PAIRS_PER_GROUP)]
    tick = 0
    while groups:
        live = groups[:tick // GROUP_LAG + 1]
        for g in live:
            if next(g, StopIteration) is StopIteration:
                groups.remove(g)
        tick += 1


def _rwkv_local_group(zk_ref, zv_ref, zr_ref, zwa_ref, w0_ref, w2_ref, a0_ref, a2_ref,
                      kk_ref, ka_ref, rk_ref, m_ref, g_ref, qt_ref, yl_ref, bn_ref,
                      *, lora, base, ck):
    c = CHUNK
    rows = slice(ck * CHUNK, (ck + 1) * CHUNK)

    head0 = _iota2((1, LANE), 1) < RWKV_HEAD
    r2 = _iota2((LANE, LANE), 0)
    c2 = _iota2((LANE, LANE), 1)
    same = (r2 // RWKV_HEAD) == (c2 // RWKV_HEAD)
    ones_bd = same.astype(F32)
    eye = r2 == c2

    def stack(x):
        z = jnp.zeros_like(x)
        return jnp.concatenate([jnp.where(head0, x, z), jnp.where(head0, z, x)], axis=0)

    stack_b = lambda x: stack(x.astype(BF16))
    fold = lambda x: x[:c] + x[c:]

    pairs = range(PAIRS_PER_GROUP)
    chains = [(q, d) for q in pairs for d in range(2)]
    qls = [slice((base + q) * LANE, (base + q + 1) * LANE) for q in pairs]
    ks = [zk_ref[0, rows, ql].astype(F32) for ql in qls]
    vs_ = [zv_ref[0, rows, ql].astype(F32) for ql in qls]
    rs = [zr_ref[0, rows, ql].astype(F32) for ql in qls]
    pcols = [slice((base + q) * 2 * LANE, (base + q + 1) * 2 * LANE) for q in pairs]
    wraw = [lora[0][rows, pc] for pc in pcols]
    araw = [lora[1][rows, pc] for pc in pcols]
    yield
    logw = [-EXP_M05 * _sigmoid(w) for w in wraw]
    a_all = [_sigmoid(a) for a in araw]
    kk0 = [ks[q] * kk_ref[:, qls[q]] for q in pairs]
    ss = [_mm2r(x * x, ones_bd) for x in kk0]
    yield
    kk = [kk0[q] / jnp.maximum(jnp.sqrt(ss[q]), 1e-12) for q in pairs]
    vstk = [stack_b(v) for v in vs_]

    dsl = [slice(d * LANE, (d + 1) * LANE) for d in range(2)]
    lw = [logw[q][:, dsl[d]] for q, d in chains]
    ad = [a_all[q][:, dsl[d]] for q, d in chains]
    kd = [ks[q] * (1.0 + (ad[i] - 1.0) * ka_ref[:, qls[q]]) for i, (q, d) in enumerate(chains)]
    bb = [kk[q] * ad[i] for i, (q, d) in enumerate(chains)]
    bonus = [_mm2r(rs[q] * (kd[2 * q] + kd[2 * q + 1]) * rk_ref[:, qls[q]], ones_bd) * vs_[q]
             for q in pairs]
    yield
    r3 = _iota2((c, 3 * c), 0)
    c3 = _iota2((c, 3 * c), 1) & (c - 1)
    tri3 = (c3 <= r3).astype(BF16)
    tt = _iota2((c, LANE), 0)
    ts = _iota2((c, LANE), 1) & (c - 1)
    strict = [ts < tt, ts > tt]
    incl = [ts <= tt, ts >= tt]
    eye_c = (ts == tt).astype(F32)

    def prefix(x):
        xh, xl = _split(x)
        xll = (x - xh.astype(F32) - xl.astype(F32)).astype(BF16)
        return jnp.dot(tri3, jnp.concatenate([xh, xl, xll], axis=0), preferred_element_type=F32)

    pre = [prefix(x) for x in logw]
    ltot = [pre[q][c - 1:c, dsl[d]] for q, d in chains]
    lc = [pre[q][:, dsl[0]] if d == 0 else ltot[i] - pre[q][:, dsl[1]] + lw[i]
          for i, (q, d) in enumerate(chains)]
    yield
    n = len(chains)
    bdot = lambda a, b: jnp.dot(a, b, preferred_element_type=F32)
    kkd = [(kk[q] * jnp.exp(lc[i] - lw[i])).astype(BF16) for i, (q, d) in enumerate(chains)]
    rd = [rs[q] * jnp.exp(lc[i]) for i, (q, d) in enumerate(chains)]
    e_inv = [jnp.exp(-x) for x in lc]
    inv_s = [jnp.concatenate([stack_b(bb[i] * e_inv[i]), stack_b(kd[i] * e_inv[i])], axis=0)
             for i in range(n)]
    yield
    amat = [lax.dot_general(jnp.concatenate([kkd[i], rd[i].astype(BF16)], axis=0), inv_s[i],
                            NT, preferred_element_type=F32).astype(BF16) for i in range(n)]
    zero_c = jnp.zeros((c, LANE), BF16)
    a_kb = [jnp.where(strict[d], amat[i][:c, :LANE], zero_c) for i, (q, d) in enumerate(chains)]
    a_kk = [jnp.where(strict[d], amat[i][:c, LANE:], zero_c) for i, (q, d) in enumerate(chains)]
    aq_b = [jnp.where(incl[d], amat[i][c:, :LANE], zero_c) for i, (q, d) in enumerate(chains)]
    aq_k = [jnp.where(incl[d], amat[i][c:, LANE:], zero_c) for i, (q, d) in enumerate(chains)]
    yield
    av = [bdot(jnp.concatenate([a_kk[i], aq_k[i]], axis=0), vstk[q])
          for i, (q, d) in enumerate(chains)]
    yield
    tinv = [eye_c - a.astype(F32) for a in a_kb]
    qpow = [bdot(a, stack(a)).astype(BF16) for a in a_kb]
    yield
    for _ in range(4):
        prod = [bdot(qpow[i], jnp.concatenate([stack(qpow[i]), stack_b(tinv[i])], axis=1))
                for i in range(n)]
        qpow = [x[:, :LANE].astype(BF16) for x in prod]
        tinv = [tinv[i] + prod[i][:, LANE:] for i in range(n)]
        yield
    tinv = [tinv[i] + bdot(qpow[i], stack_b(tinv[i])) for i in range(n)]
    yield
    tsplit = [_split(t) for t in tinv]
    ia_t = [bdot((eye_c + a_kb[i].astype(F32)).astype(BF16),
                 jnp.concatenate([stack(tsplit[i][0]), stack(tsplit[i][1])], axis=1))
            for i in range(n)]
    resid = [eye_c - ia_t[i][:, :LANE] - ia_t[i][:, LANE:] for i in range(n)]
    yield
    tinv = [tinv[i] + bdot(tinv[i].astype(BF16), stack_b(resid[i])) for i in range(n)]
    yield
    x = [bdot(tinv[i].astype(BF16),
              jnp.concatenate([stack(kkd[i]), stack_b(av[i][:c])], axis=1))
         for i in range(n)]
    xb = [v.astype(BF16) for v in x]
    yield
    qy = [jnp.concatenate([rd[i], av[i][c:]], axis=1)
          - bdot(aq_b[i], jnp.concatenate([stack(xb[i][:, :LANE]), stack(xb[i][:, LANE:])],
                                          axis=1)) for i in range(n)]
    yield
    e_end = [jnp.exp(ltot[i] - lc[i]) for i in range(n)]
    ends = [jnp.concatenate([(-bb[i] * e_end[i]).astype(BF16), (kd[i] * e_end[i]).astype(BF16)],
                            axis=0) for i in range(n)]
    wuv = [jnp.concatenate([xb[i], jnp.concatenate([zero_c, vs_[q].astype(BF16)], axis=1)],
                           axis=0) for i, (q, d) in enumerate(chains)]
    mg = [lax.dot_general(ends[i], wuv[i], TN, preferred_element_type=F32) for i in range(n)]
    yield
    for i, (q, d) in enumerate(chains):
        m_ref[0, ck, d, base + q] = fold(jnp.where(eye, jnp.exp(ltot[i]), 0.0)
                                         + jnp.where(same, mg[i][:, :LANE], 0.0)).astype(BF16)
        g_ref[0, ck, d, base + q] = fold(jnp.where(same, mg[i][:, LANE:], 0.0)
                                         ).astype(g_ref.dtype)
        qt_ref[0, d, rows, qls[q]] = qy[i][:, :LANE].astype(BF16)
    for q in pairs:
        yl_ref[0, rows, qls[q]] = (qy[2 * q][:, LANE:]
                                   + qy[2 * q + 1][:, LANE:]).astype(yl_ref.dtype)
        bn_ref[0, rows, qls[q]] = bonus[q].astype(bn_ref.dtype)


def _rwkv_local(z, w0p, w2p, a0p, a2p, k_k, k_a, r_k):
    b, t, _ = z.shape
    w = k_k.shape[1]
    npair = w // LANE
    pp = PAIRS_PER_STEP
    ng = npair // pp
    wl = pp * LANE
    nc = t // CHUNK
    cs = LOCAL_CHUNKS if nc % LOCAL_CHUNKS == 0 else 1
    rows = cs * CHUNK
    tokc = lambda base: pl.BlockSpec((1, rows, wl), lambda bi, ci, p: (bi, ci, base + p))
    perp3 = lambda n: pl.BlockSpec((n, pp * 2 * LANE), lambda bi, ci, p: (0, p))
    vecp = pl.BlockSpec((1, wl), lambda bi, ci, p: (0, p))
    mat = pl.BlockSpec((1, cs, 2, pp, CHUNK, LANE), lambda bi, ci, p: (bi, ci, 0, p, 0, 0))
    return pl.pallas_call(
        functools.partial(_rwkv_local_body, cs=cs),
        grid=(b, nc // cs, ng),
        in_specs=[tokc(0), tokc(ng), tokc(2 * ng),
                  pl.BlockSpec((1, rows, 2 * LANE), lambda bi, ci, p: (bi, ci, 3 * npair // 2)),
                  perp3(1), perp3(LANE), perp3(1), perp3(LANE), vecp, vecp, vecp],
        out_specs=[mat, mat,
                   pl.BlockSpec((1, 2, rows, wl), lambda bi, ci, p: (bi, 0, ci, p)),
                   pl.BlockSpec((1, rows, wl), lambda bi, ci, p: (bi, ci, p)),
                   pl.BlockSpec((1, rows, wl), lambda bi, ci, p: (bi, ci, p))],
        out_shape=[jax.ShapeDtypeStruct((b, nc, 2, npair, CHUNK, LANE), BF16),
                   jax.ShapeDtypeStruct((b, nc, 2, npair, CHUNK, LANE), ACT),
                   jax.ShapeDtypeStruct((b, 2, t, w), BF16),
                   jax.ShapeDtypeStruct((b, t, w), ACT),
                   jax.ShapeDtypeStruct((b, t, w), ACT)],
        compiler_params=_cparams(("parallel", "parallel", "parallel")),
        name="rwkv_local",
    )(z, z, z, z, w0p, w2p, a0p, a2p, k_k, k_a, r_k)


SCAN_CHUNKS = 8


def _rwkv_scan_body(m0_ref, g0_ref, q0_ref, m1_ref, g1_ref, q1_ref, h0_ref,
                    y0_ref, y1_ref, hfin_ref, h_scr, *, npair, cs):
    ci = pl.program_id(1)

    @pl.when(ci == 0)
    def _():
        h_scr[...] = h0_ref[0]

    head0 = _iota2((1, LANE), 1) < RWKV_HEAD

    def expand(x):
        z = jnp.zeros_like(x)
        return jnp.concatenate([jnp.where(head0, x, z), jnp.where(head0, z, x)], axis=0)

    refs = ((m0_ref, g0_ref, q0_ref, y0_ref), (m1_ref, g1_ref, q1_ref, y1_ref))
    chains = [(d, p) for d in range(2) for p in range(npair)]
    lanes = [slice(p * LANE, (p + 1) * LANE) for p in range(npair)]
    h = [h_scr[d, p] for d, p in chains]
    for step in range(cs):
        ck = (step, cs - 1 - step)
        rows = [slice(c * CHUNK, (c + 1) * CHUNK) for c in ck]
        hb = [x.astype(BF16) for x in h]
        res = [jnp.dot(jnp.concatenate([refs[d][2][0, 0, rows[d], lanes[p]],
                                        refs[d][0][0, ck[d], 0, p]], axis=0), hb[i],
                       preferred_element_type=F32) for i, (d, p) in enumerate(chains)]
        for i, (d, p) in enumerate(chains):
            refs[d][3][0, rows[d], lanes[p]] = res[i][:CHUNK].astype(refs[d][3].dtype)
        h = [expand(res[i][CHUNK:] + refs[d][1][0, ck[d], 0, p].astype(F32))
             for i, (d, p) in enumerate(chains)]
    for i, (d, p) in enumerate(chains):
        h_scr[d, p] = h[i]

    @pl.when(ci == pl.num_programs(1) - 1)
    def _():
        hfin_ref[0] = h_scr[...]


def _rwkv_scan(mm, gg, qt, h0):
    b, nc, _, npair, _, _ = mm.shape
    t, w = qt.shape[2], qt.shape[3]
    cs = SCAN_CHUNKS if nc % SCAN_CHUNKS == 0 else 1
    nb = nc // cs
    fwd = lambda bi, ci: (bi, ci, 0, 0, 0, 0)
    rev = lambda bi, ci: (bi, nb - 1 - ci, 1, 0, 0, 0)
    mblk = (1, cs, 1, npair, CHUNK, LANE)
    hspec = pl.BlockSpec((1, 2, npair, LANE, LANE), lambda bi, ci: (bi, 0, 0, 0, 0))
    return pl.pallas_call(
        functools.partial(_rwkv_scan_body, npair=npair, cs=cs),
        grid=(b, nb),
        in_specs=[pl.BlockSpec(mblk, fwd), pl.BlockSpec(mblk, fwd),
                  pl.BlockSpec((1, 1, cs * CHUNK, w), lambda bi, ci: (bi, 0, ci, 0)),
                  pl.BlockSpec(mblk, rev), pl.BlockSpec(mblk, rev),
                  pl.BlockSpec((1, 1, cs * CHUNK, w), lambda bi, ci: (bi, 1, nb - 1 - ci, 0)),
                  hspec],
        out_specs=[pl.BlockSpec((1, cs * CHUNK, w), lambda bi, ci: (bi, ci, 0)),
                   pl.BlockSpec((1, cs * CHUNK, w), lambda bi, ci: (bi, nb - 1 - ci, 0)),
                   hspec],
        out_shape=[jax.ShapeDtypeStruct((b, t, w), ACT), jax.ShapeDtypeStruct((b, t, w), ACT),
                   jax.ShapeDtypeStruct(h0.shape, F32)],
        scratch_shapes=[pltpu.VMEM((2, npair, LANE, LANE), F32)],
        compiler_params=_cparams(("parallel", "arbitrary")),
        name="rwkv_scan",
    )(mm, gg, qt, mm, gg, qt, h0)


def _rwkv_out_body(y0_ref, y1_ref, yl_ref, bn_ref, gate_ref, x_ref, gl_ref, gnw_ref, gnb_ref,
                   w_ref, o_ref):
    y = y0_ref[0].astype(F32) + y1_ref[0].astype(F32) + yl_ref[0].astype(F32)
    r2 = _iota2((LANE, LANE), 0)
    c2 = _iota2((LANE, LANE), 1)
    avg = ((r2 // RWKV_HEAD) == (c2 // RWKV_HEAD)).astype(F32) * (1.0 / RWKV_HEAD)
    parts = []
    for p in range(y.shape[1] // LANE):
        yp = y[:, p * LANE:(p + 1) * LANE]
        dl = yp - _mm2r(yp, avg)
        var = _mm2r(dl * dl, avg)
        parts.append(dl * lax.rsqrt(var + GN_EPS))
    yn = jnp.concatenate(parts, axis=1)
    gt = gate_ref[0].astype(F32)
    act = (yn * gnw_ref[...] + gnb_ref[...] + bn_ref[0].astype(F32)) * (gt * _sigmoid(gt))
    out = jnp.dot(act.astype(BF16), w_ref[...], preferred_element_type=F32)
    o_ref[0] = x_ref[0] + gl_ref[0] * out


def _rwkv_out(y0, y1, yl, bn, gate, x, gl, gnw, gnb, w, tm):
    b, t, d = x.shape
    tm = min(tm, t)
    wd = y0.shape[2]
    tok = lambda n: pl.BlockSpec((1, tm, n), lambda bi, i: (bi, i, 0))
    return pl.pallas_call(
        _rwkv_out_body,
        grid=(b, t // tm),
        in_specs=[tok(wd), tok(wd), tok(wd), tok(wd), tok(wd), tok(d),
                  pl.BlockSpec((1, 1, d), lambda bi, i: (bi, 0, 0)),
                  pl.BlockSpec((1, wd), lambda bi, i: (0, 0)),
                  pl.BlockSpec((1, wd), lambda bi, i: (0, 0)),
                  pl.BlockSpec(w.shape, lambda bi, i: (0, 0))],
        out_specs=tok(d),
        out_shape=jax.ShapeDtypeStruct((b, t, d), F32),
        compiler_params=_cparams(("parallel", "parallel")),
        name="rwkv_out",
    )(y0, y1, yl, bn, gate, x, gl, gnw, gnb, w)


def _rope_tables(t):
    rows = t // GRID_W
    row = jnp.repeat(jnp.arange(rows, dtype=F32), GRID_W)
    col = jnp.tile(jnp.arange(GRID_W, dtype=F32), rows)
    inv = 1.0 / (ROPE_BASE ** (jnp.arange(ROPE_FREQS, dtype=F32) / ROPE_FREQS))
    ang = jnp.stack([row[:, None] * inv, col[:, None] * inv], axis=1)
    cos, sin = jnp.cos(ang), jnp.sin(ang)
    zeros = jnp.zeros_like(sin)
    ones_lo = jnp.ones((t, QK_NOPE), F32)
    pad_hi = HEAD_SLOT - QK_HEAD
    cos_t = jnp.concatenate([ones_lo, jnp.concatenate([cos, cos], axis=2).reshape(t, QK_ROPE),
                             jnp.ones((t, pad_hi), F32)], axis=1)
    sa = jnp.concatenate([jnp.zeros((t, QK_NOPE), F32),
                          jnp.concatenate([-sin, zeros], axis=2).reshape(t, QK_ROPE),
                          jnp.zeros((t, pad_hi), F32)], axis=1)
    sb = jnp.concatenate([jnp.zeros((t, QK_NOPE), F32),
                          jnp.concatenate([zeros, sin], axis=2).reshape(t, QK_ROPE),
                          jnp.zeros((t, pad_hi), F32)], axis=1)
    return cos_t, sa, sb


def _even_layer(x, ctx, mod_l, mod_c, need_ctx, g, w_in, kv_norm, q_norm, w_uq, w_ukv,
                q_head_norm, k_head_norm, w_fnet, w_out):
    b, s, d = x.shape
    tc = ctx.shape[1]
    e_q0 = KV_LORA + QK_ROPE
    e_f0 = e_q0 + Q_LORA
    e_g0 = e_f0 + FNET_GROUPS * FNET_GROUP_DIM
    w_p = jnp.concatenate([w_in[:, e_g0:], w_in[:, e_f0:e_g0], w_in[:, :e_q0],
                           jnp.zeros((d, LANE - QK_ROPE), F32), w_in[:, e_q0:e_f0]],
                          axis=1).astype(BF16)
    splits = (d, (FNET_GROUPS, FNET_GROUP_DIM), KV_LORA + LANE + Q_LORA)
    kvw = w_ukv.reshape(KV_LORA, MLA_HEADS, QK_NOPE + V_HEAD)
    wk = jnp.pad(kvw[:, :, :QK_NOPE], ((0, 0), (0, 0), (0, HEAD_SLOT - QK_NOPE)))
    wk = wk.reshape(KV_LORA, MLA_HEADS * HEAD_SLOT).astype(BF16)
    wv = jnp.pad(kvw[:, :, QK_NOPE:], ((0, 0), (0, 0), (0, HEAD_SLOT - V_HEAD)))
    wv = wv.reshape(KV_LORA, MLA_HEADS * HEAD_SLOT).astype(BF16)
    wq3 = jnp.pad(w_uq.reshape(Q_LORA, MLA_HEADS, QK_HEAD), ((0, 0), (0, 0), (0, HEAD_SLOT - QK_HEAD)))
    wq = wq3.reshape(Q_LORA, MLA_HEADS * HEAD_SLOT).astype(BF16)
    kg = jnp.pad(k_head_norm, (0, HEAD_SLOT - QK_HEAD)).reshape(1, HEAD_SLOT)
    qg = (jnp.pad(q_head_norm, (0, HEAD_SLOT - QK_HEAD))
          * (QK_HEAD ** -0.5 * math.log2(math.e))).reshape(1, HEAD_SLOT)
    lane = np.arange(HEAD_SLOT)
    tail = (lane >= QK_NOPE) & (lane < QK_HEAD)
    first = tail & (((lane - QK_NOPE) // ROPE_FREQS) % 2 == 0)
    partner = np.where(first, lane + ROPE_FREQS, np.where(tail, lane - ROPE_FREQS, lane))
    sign = np.where(first, -1.0, np.where(tail, 1.0, 0.0)).astype(np.float32)
    wqr = (wq3[:, :, partner] * (sign * qg[0, partner])).reshape(Q_LORA, MLA_HEADS * HEAD_SLOT)
    wqr = wqr.astype(BF16)
    kvn, qn = kv_norm.reshape(1, -1), q_norm.reshape(1, -1)
    g2 = g.reshape(1, d)
    bound = (1.02 * QK_HEAD * jnp.max(jnp.abs(qg)) * jnp.max(jnp.abs(kg))).astype(BF16).astype(F32)
    static_ok = bound <= MAX_STATIC_BOUND
    bias_lane = (jnp.arange(HEAD_SLOT) == BIAS_LANE).astype(F32).reshape(1, HEAD_SLOT)
    kb = bias_lane * jnp.where(static_ok, -bound, 0.0)
    qb = bias_lane

    gate_l, four_l, ua_l = _proj(x, g2, mod_l[1], mod_l[0], w_p, splits, TOKEN_TILE)
    gate_c, four_c, ua_c = _proj(ctx, g2, mod_c[1], mod_c[0], w_p, splits, TOKEN_TILE)
    sk = s + tc
    cos_t, sa, sb = _rope_tables(s)
    cos_t = jnp.concatenate([cos_t, jnp.ones((tc, HEAD_SLOT), F32)], axis=0)
    sa = jnp.concatenate([sa, jnp.zeros((tc, HEAD_SLOT), F32)], axis=0)
    sb = jnp.concatenate([sb, jnp.zeros((tc, HEAD_SLOT), F32)], axis=0)
    tabs = (cos_t * kg, cos_t * qg, sa, sb, sb - sa)
    q_all, k_all, v_all = _qkv(ua_l, ua_c, kvn, qn, wk, wv, wq, wqr, kg, kb, qb, tabs,
                               math.gcd(s, tc))
    bk = ATTN_K_BLOCK if sk % ATTN_K_BLOCK == 0 else tc
    o_l = _attention(q_all, k_all, v_all, static_ok, 0, s, 0, sk, ATTN_Q_BLOCK, bk)
    f_l = _fourier_latent(four_l, w_fnet)
    wo = w_out.astype(BF16)
    x_new = _merge(o_l, f_l, gate_l, x, mod_l[2], wo, TOKEN_TILE)
    ctx_new = ctx
    if need_ctx:
        o_c = _attention(q_all, k_all, v_all, static_ok, s, tc, s, tc, tc, tc)
        f_c = _fourier_dense(four_c, w_fnet)
        ctx_new = _merge(o_c, f_c, gate_c, ctx, mod_c[2], wo, TOKEN_TILE)
    return x_new, ctx_new


def _odd_layer(x, ctx, mod_l, mod_c, need_ctx, g, w_in, shift_w, w0, w2, a0, a2, k_k, k_a, r_k,
               gn_w, gn_b, w_out):
    b, s, d = x.shape
    w = k_k.shape[0]
    npair = w // LANE
    o_wd0 = 2 * w
    o_r0 = o_wd0 + 2 * DECAY_LORA + 2 * AAA_LORA
    conv_ch = o_r0 + w
    segs = ((0, o_wd0), (o_r0, w), (o_wd0, o_r0 - o_wd0))
    w_p = w_in.astype(BF16)
    sw = shift_w
    g2 = g.reshape(1, d)

    def pairs(vec2):
        return vec2.reshape(2, npair, LANE).transpose(1, 0, 2).reshape(1, npair * 2 * LANE)

    def pair_mats(m):
        rr = m.shape[1]
        mp = m.reshape(2, rr, npair, LANE).transpose(2, 0, 1, 3)
        z = jnp.zeros_like(mp[:, 0])
        top = jnp.concatenate([mp[:, 0], z], axis=2)
        bot = jnp.concatenate([z, mp[:, 1]], axis=2)
        full = jnp.concatenate([top, bot], axis=1)
        return full.transpose(1, 0, 2).reshape(2 * rr, npair * 2 * LANE).astype(BF16)

    w0p, a0p, w2p, a2p = pairs(w0), pairs(a0), pair_mats(w2), pair_mats(a2)
    kk2, ka2, rk2 = k_k.reshape(1, w), k_a.reshape(1, w), r_k.reshape(1, w)
    wo = w_out.astype(BF16)

    def mix(xin, mod, h0):
        z, gate = _proj_shift(xin, g2, mod[1], mod[0], w_p, sw, conv_ch, segs, TOKEN_TILE)
        mm, gg, qt, yl, bn = _rwkv_local(z, w0p, w2p, a0p, a2p, kk2, ka2, rk2)
        y0, y1, hfin = _rwkv_scan(mm, gg, qt, h0)
        return (y0, y1, yl, bn, gate), hfin

    h_zero = jnp.zeros((b, 2, npair, LANE, LANE), F32)
    parts_c, h_ctx = mix(ctx, mod_c, h_zero)
    parts_l, _ = mix(x, mod_l, h_ctx)
    gnw, gnb = gn_w.reshape(1, w), gn_b.reshape(1, w)
    x_new = _rwkv_out(*parts_l, x, mod_l[2], gnw, gnb, wo, TOKEN_TILE)
    ctx_new = ctx
    if need_ctx:
        ctx_new = _rwkv_out(*parts_c, ctx, mod_c[2], gnw, gnb, wo, TOKEN_TILE)
    return x_new, ctx_new


def kernel(x, c, ctx, c_ctx, ada_w, ada_b, norm_g, e_w_in, e_kv_norm, e_q_norm, e_w_uq, e_w_ukv,
           e_q_head_norm, e_k_head_norm, e_w_fnet, e_w_out, o_w_in, o_shift_w, o_w0, o_w2, o_a0,
           o_a2, o_k_k, o_k_a, o_r_k, o_gn_w, o_gn_b, o_w_out):
    b, s, d = x.shape
    depth = ada_w.shape[0]
    assert b + 1 <= 8
    cond8 = jnp.concatenate([c, c_ctx[None, :], jnp.zeros((8 - b - 1, d), F32)], axis=0)
    mod = _ada(cond8, ada_w, ada_b)
    for layer in range(depth):
        need_ctx = layer < depth - 1
        m = mod[layer]
        chunk = lambda rows, i: rows[:, None, i * d:(i + 1) * d]
        lat, cx = m[:b], jnp.broadcast_to(m[b:b + 1], (b, 3 * d))
        mod_l = (chunk(lat, 0), 1.0 + chunk(lat, 1), chunk(lat, 2))
        mod_c = (chunk(cx, 0), 1.0 + chunk(cx, 1), chunk(cx, 2))
        j = layer // 2
        if layer % 2 == 0:
            x, ctx = _even_layer(x, ctx, mod_l, mod_c, need_ctx, norm_g[layer], e_w_in[j],
                                 e_kv_norm[j], e_q_norm[j], e_w_uq[j], e_w_ukv[j],
                                 e_q_head_norm[j], e_k_head_norm[j], e_w_fnet[j], e_w_out[j])
        else:
            x, ctx = _odd_layer(x, ctx, mod_l, mod_c, need_ctx, norm_g[layer], o_w_in[j],
                                o_shift_w[j], o_w0[j], o_w2[j], o_a0[j], o_a2[j], o_k_k[j],
                                o_k_a[j], o_r_k[j].reshape(-1), o_gn_w[j], o_gn_b[j], o_w_out[j])
    return x
```

```python
import functools
import math

import numpy as np
import jax
import jax.numpy as jnp
from jax import lax
from jax.experimental import pallas as pl
from jax.experimental.pallas import tpu as pltpu

F32 = jnp.float32
BF16 = jnp.bfloat16
ACT = BF16

GRID_W = 64
NORM_EPS = 1e-6
MLA_HEADS = 8
QK_NOPE = 64
QK_ROPE = 32
QK_HEAD = QK_NOPE + QK_ROPE
V_HEAD = 64
Q_LORA = 384
KV_LORA = 256
ROPE_FREQS = QK_ROPE // 4
ROPE_BASE = 10000.0
FNET_GROUPS = 4
FNET_GROUP_DIM = 128
RWKV_HEAD = 64
DECAY_LORA = 64
AAA_LORA = 64
GN_EPS = 64e-5

LANE = 128
CHUNK = 64
HEAD_SLOT = 128
VMEM_LIMIT = 56 * 1024 * 1024

TOKEN_TILE = 512
ATTN_Q_BLOCK = 2048
ATTN_K_BLOCK = 768
FOUR_LANE_TILE = 2048
FOUR_ROWS_TILE = 32

NN = (((1,), (0,)), ((), ()))
NT = (((1,), (1,)), ((), ()))
TN = (((0,), (0,)), ((), ()))


def _cparams(sem):
    return pltpu.CompilerParams(dimension_semantics=sem, vmem_limit_bytes=VMEM_LIMIT)


def _mm(a, b, dn=NN):
    return lax.dot_general(a.astype(BF16), b.astype(BF16), dn, preferred_element_type=F32)


def _split(a):
    hi = a.astype(BF16)
    lo = (a - hi.astype(F32)).astype(BF16)
    return hi, lo


def _mm3(a, b, dn=NN):
    ah, al = _split(a)
    bh, bl = _split(b)
    d = lambda x, y: lax.dot_general(x, y, dn, preferred_element_type=F32)
    return d(ah, bh) + d(al, bh) + d(ah, bl)


def _mm2r(a, b_exact):
    ah, al = _split(a)
    bb = b_exact.astype(BF16)
    return jnp.dot(jnp.concatenate([ah, al], axis=1), jnp.concatenate([bb, bb], axis=0),
                   preferred_element_type=F32)


def _mm2l(a_exact, b):
    bh, bl = _split(b)
    n = b.shape[1]
    y = jnp.dot(a_exact.astype(BF16), jnp.concatenate([bh, bl], axis=1),
                preferred_element_type=F32)
    return y[:, :n] + y[:, n:]


def _sigmoid(x):
    return 1.0 / (1.0 + jnp.exp(-x))


def _modnorm(x, g, sc1, sh):
    y = x * lax.rsqrt(jnp.mean(x * x, axis=-1, keepdims=True) + NORM_EPS)
    return (y * g) * sc1 + sh


def _iota2(shape, dim):
    return lax.broadcasted_iota(jnp.int32, shape, dim)


def _ada_body(c_ref, w_ref, b_ref, o_ref):
    c = c_ref[...]
    s = c * _sigmoid(c)
    o_ref[0] = _mm3(s, w_ref[0]) + b_ref[0]


def _ada(cond8, ada_w, ada_b):
    depth, d, n = ada_w.shape
    tn = 512
    return pl.pallas_call(
        _ada_body,
        grid=(depth, n // tn),
        in_specs=[
            pl.BlockSpec((8, d), lambda l, j: (0, 0)),
            pl.BlockSpec((1, d, tn), lambda l, j: (l, 0, j)),
            pl.BlockSpec((1, 1, tn), lambda l, j: (l, 0, j)),
        ],
        out_specs=pl.BlockSpec((1, 8, tn), lambda l, j: (l, 0, j)),
        out_shape=jax.ShapeDtypeStruct((depth, 8, n), F32),
        compiler_params=_cparams(("parallel", "parallel")),
        name="ada",
    )(cond8, ada_w, ada_b.reshape(depth, 1, n))


COL_CHUNK = 512


def _proj_body(x_ref, g_ref, sc_ref, sh_ref, w_ref, *o_refs, splits):
    h = _modnorm(x_ref[0], g_ref[...], sc_ref[0], sh_ref[0]).astype(BF16)
    mm = lambda c0, c1: jnp.dot(h, w_ref[:, c0:c1], preferred_element_type=F32)
    off = 0
    for o_ref, n in zip(o_refs, splits):
        if isinstance(n, tuple):
            groups, width = n
            y = mm(off, off + groups * width).astype(o_ref.dtype)
            for gi in range(groups):
                o_ref[0, gi] = y[:, gi * width:(gi + 1) * width]
            off += groups * width
            continue
        for c0 in range(0, n, COL_CHUNK):
            c1 = min(n, c0 + COL_CHUNK)
            o_ref[0, :, c0:c1] = mm(off + c0, off + c1).astype(o_ref.dtype)
        off += n


def _proj(x, g, sc1, sh, w, splits, tm):
    b, t, d = x.shape
    tm = min(tm, t)
    n = w.shape[1]
    vec = pl.BlockSpec((1, 1, d), lambda bi, i: (bi, 0, 0))
    specs, shapes = [], []
    for s in splits:
        if isinstance(s, tuple):
            specs.append(pl.BlockSpec((1, s[0], tm, s[1]), lambda bi, i: (bi, 0, i, 0)))
            shapes.append(jax.ShapeDtypeStruct((b, s[0], t, s[1]), ACT))
        else:
            specs.append(pl.BlockSpec((1, tm, s), lambda bi, i: (bi, i, 0)))
            shapes.append(jax.ShapeDtypeStruct((b, t, s), ACT))
    return pl.pallas_call(
        functools.partial(_proj_body, splits=splits),
        grid=(b, t // tm),
        in_specs=[
            pl.BlockSpec((1, tm, d), lambda bi, i: (bi, i, 0)),
            pl.BlockSpec((1, d), lambda bi, i: (0, 0)),
            vec, vec,
            pl.BlockSpec((d, n), lambda bi, i: (0, 0)),
        ],
        out_specs=specs,
        out_shape=shapes,
        compiler_params=_cparams(("parallel", "parallel")),
        name="proj",
    )(x, g, sc1, sh, w)


HALO = 16


def _proj_shift_body(x_ref, xp_ref, xn_ref, g_ref, sc_ref, sh_ref, w_ref, sw_ref, z_ref, gate_ref,
                     *, tm, n_conv, segs):
    i = pl.program_id(1)
    last = pl.num_programs(1) - 1
    g, sc1, sh = g_ref[...], sc_ref[0], sh_ref[0]
    h = _modnorm(x_ref[0], g, sc1, sh)
    hp = _modnorm(xp_ref[0], g, sc1, sh) * (i > 0).astype(F32)
    hn = _modnorm(xn_ref[0], g, sc1, sh) * (i < last).astype(F32)
    hb = jnp.concatenate([hp, h, hn], axis=0).astype(BF16)
    rows = tm + 2 * HALO
    dst = 0
    for src, width in segs:
        for c0 in range(0, width, COL_CHUNK):
            cw = min(COL_CHUNK, width - c0)
            cols = slice(src + c0, src + c0 + cw)
            u = jnp.dot(hb, w_ref[:, cols], preferred_element_type=F32)
            up = pltpu.roll(u, 1, 0)[HALO:HALO + tm]
            un = pltpu.roll(u, rows - 1, 0)[HALO:HALO + tm]
            um = u[HALO:HALO + tm]
            z_ref[0, :, dst + c0:dst + c0 + cw] = (
                sw_ref[0:1, cols] * up + sw_ref[1:2, cols] * um
                + sw_ref[2:3, cols] * un).astype(z_ref.dtype)
        dst += width
    hc = hb[HALO:HALO + tm]
    n_all = w_ref.shape[1]
    for c0 in range(n_conv, n_all, COL_CHUNK):
        c1 = min(n_all, c0 + COL_CHUNK)
        gate_ref[0, :, c0 - n_conv:c1 - n_conv] = jnp.dot(
            hc, w_ref[:, c0:c1], preferred_element_type=F32).astype(gate_ref.dtype)


def _proj_shift(x, g, sc1, sh, w, sw, n_conv, segs, tm):
    b, t, d = x.shape
    tm = min(tm, t)
    n = w.shape[1]
    hb = tm // HALO
    nhb = t // HALO
    vec = pl.BlockSpec((1, 1, d), lambda bi, i: (bi, 0, 0))
    return pl.pallas_call(
        functools.partial(_proj_shift_body, tm=tm, n_conv=n_conv, segs=segs),
        grid=(b, t // tm),
        in_specs=[
            pl.BlockSpec((1, tm, d), lambda bi, i: (bi, i, 0)),
            pl.BlockSpec((1, HALO, d), lambda bi, i: (bi, jnp.maximum(i * hb - 1, 0), 0)),
            pl.BlockSpec((1, HALO, d), lambda bi, i: (bi, jnp.minimum((i + 1) * hb, nhb - 1), 0)),
            pl.BlockSpec((1, d), lambda bi, i: (0, 0)),
            vec, vec,
            pl.BlockSpec((d, n), lambda bi, i: (0, 0)),
            pl.BlockSpec((3, n_conv), lambda bi, i: (0, 0)),
        ],
        out_specs=[pl.BlockSpec((1, tm, n_conv), lambda bi, i: (bi, i, 0)),
                   pl.BlockSpec((1, tm, n - n_conv), lambda bi, i: (bi, i, 0))],
        out_shape=[jax.ShapeDtypeStruct((b, t, n_conv), ACT),
                   jax.ShapeDtypeStruct((b, t, n - n_conv), ACT)],
        compiler_params=_cparams(("parallel", "parallel")),
        name="proj_shift",
    )(x, x, x, g, sc1, sh, w, sw)


def _rms(x, g):
    return x * lax.rsqrt(jnp.mean(x * x, axis=-1, keepdims=True) + NORM_EPS) * g


def _qkv_body(ual_ref, uac_ref, kvn_ref, qn_ref, wk_ref, wv_ref, wq_ref, wqr_ref, kg_ref, kb_ref,
              qb_ref, cosk_ref, cosq_ref, sa_ref, sb_ref, sinq_ref, q_ref, k_ref, v_ref, *, n_lat):
    ua = jnp.where(pl.program_id(1) < n_lat, ual_ref[0], uac_ref[0]).astype(F32)
    ckv = _rms(ua[:, :KV_LORA], kvn_ref[...]).astype(BF16)
    kr = ua[:, KV_LORA:KV_LORA + LANE]
    cq = _rms(ua[:, KV_LORA + LANE:], qn_ref[...]).astype(BF16)
    kn = jnp.dot(ckv, wk_ref[...], preferred_element_type=F32)
    vv = jnp.dot(ckv, wv_ref[...], preferred_element_type=F32)
    qq = jnp.dot(cq, wq_ref[...], preferred_element_type=F32)
    qr = jnp.dot(cq, wqr_ref[...], preferred_element_type=F32)
    ones_hi = (_iota2((1, HEAD_SLOT), 1) >= V_HEAD).astype(F32)
    pe = pltpu.roll(kr, QK_NOPE, 1)
    gp = pe * kg_ref[...]
    pe_rot = (pltpu.roll(gp, LANE - ROPE_FREQS, 1) * sa_ref[...]
              + pltpu.roll(gp, ROPE_FREQS, 1) * sb_ref[...])
    cosk, cosq, sinq = cosk_ref[...], cosq_ref[...], sinq_ref[...]
    inv_n = 1.0 / QK_HEAD
    scale = lambda x: lax.rsqrt(jnp.sum(x * x, axis=-1, keepdims=True) * inv_n + NORM_EPS)

    for h in range(MLA_HEADS):
        sl = slice(h * HEAD_SLOT, (h + 1) * HEAD_SLOT)
        kh = kn[:, sl] + pe
        k_ref[0, h] = (scale(kh) * (kh * cosk + pe_rot) + kb_ref[...]).astype(BF16)
        qh = qq[:, sl]
        q_ref[0, h] = (scale(qh) * (qh * cosq + qr[:, sl] * sinq) + qb_ref[...]).astype(BF16)
        v_ref[0, h] = (vv[:, sl] + ones_hi).astype(BF16)


def _qkv(ua_l, ua_c, kvn, qn, wk, wv, wq, wqr, kg, kb, qb, tabs, tm):
    b, s, wa = ua_l.shape
    tc = ua_c.shape[1]
    nl, ncx = s // tm, tc // tm
    full = lambda a: pl.BlockSpec(a.shape, lambda bi, i: (0,) * a.ndim)
    tab = pl.BlockSpec((tm, LANE), lambda bi, i: (i, 0))
    head = pl.BlockSpec((1, MLA_HEADS, tm, HEAD_SLOT), lambda bi, i: (bi, 0, i, 0))
    shape = jax.ShapeDtypeStruct((b, MLA_HEADS, s + tc, HEAD_SLOT), BF16)
    return pl.pallas_call(
        functools.partial(_qkv_body, n_lat=nl),
        grid=(b, nl + ncx),
        in_specs=[pl.BlockSpec((1, tm, wa), lambda bi, i: (bi, jnp.minimum(i, nl - 1), 0)),
                  pl.BlockSpec((1, tm, wa), lambda bi, i: (bi, jnp.maximum(i - nl, 0), 0)),
                  full(kvn), full(qn), full(wk), full(wv), full(wq), full(wqr), full(kg),
                  full(kb), full(qb)] + [tab] * len(tabs),
        out_specs=[head, head, head],
        out_shape=[shape, shape, shape],
        compiler_params=_cparams(("parallel", "parallel")),
        name="qkv",
    )(ua_l, ua_c, kvn, qn, wk, wv, wq, wqr, kg, kb, qb, *tabs)


BIAS_LANE = QK_HEAD
MAX_STATIC_BOUND = 50.0


def _attn_finish(acc_ref, o_ref):
    bq = acc_ref.shape[1]
    lane = _iota2((bq, LANE), 1)
    o0 = acc_ref[0] / pltpu.roll(acc_ref[0], V_HEAD, 1)
    o1 = acc_ref[1] / pltpu.roll(acc_ref[1], V_HEAD, 1)
    o_ref[0] = jnp.where(lane < V_HEAD, o0, pltpu.roll(o1, V_HEAD, 1)).astype(o_ref.dtype)


def _attn_static_body(q_ref, k_ref, v_ref, o_ref, acc_ref, p0_ref, p1_ref, *, bk):
    nk = k_ref.shape[2] // bk
    krows = lambda j: pl.ds(pl.multiple_of(j * bk, bk), bk)
    bufs = (p0_ref, p1_ref)

    def weights(j, slot):
        for hh in range(2):
            s = lax.dot_general(q_ref[0, hh], k_ref[0, hh, krows(j), :], NT,
                                preferred_element_type=F32)
            bufs[slot][hh] = jnp.exp2(s.astype(BF16))

    def values(j, slot):
        for hh in range(2):
            acc_ref[hh] += jnp.dot(bufs[slot][hh], v_ref[0, hh, krows(j), :],
                                   preferred_element_type=F32)

    acc_ref[...] = jnp.zeros(acc_ref.shape, F32)
    weights(0, 0)

    def two_blocks(jj, carry):
        j = 2 * jj
        weights(j + 1, 1)
        values(j, 0)
        weights(j + 2, 0)
        values(j + 1, 1)
        return carry

    pairs_done = (nk - 1) // 2
    lax.fori_loop(0, pairs_done, two_blocks, 0)
    j = 2 * pairs_done
    if (nk - 1) % 2:
        weights(j + 1, 1)
        values(j, 0)
        values(j + 1, 1)
    else:
        values(j, 0)
    _attn_finish(acc_ref, o_ref)


def _attn_online_body(q_ref, k_ref, v_ref, o_ref, acc_ref, m_ref, *, bk):
    acc_ref[...] = jnp.zeros(acc_ref.shape, F32)
    m_ref[...] = jnp.full(m_ref.shape, -jnp.inf, F32)

    def step(j, carry):
        rows = pl.ds(pl.multiple_of(j * bk, bk), bk)
        for hh in range(2):
            s = lax.dot_general(q_ref[0, hh], k_ref[0, hh, rows, :], NT,
                                preferred_element_type=F32)
            m_prev = m_ref[hh]
            m_new = jnp.maximum(m_prev, jnp.max(s, axis=-1, keepdims=True))
            p = jnp.exp2(s - m_new)
            acc_ref[hh] = (jnp.exp2(m_prev - m_new) * acc_ref[hh]
                           + jnp.dot(p.astype(BF16), v_ref[0, hh, rows, :],
                                     preferred_element_type=F32))
            m_ref[hh] = m_new
        return carry

    lax.fori_loop(0, k_ref.shape[2] // bk, step, 0)
    _attn_finish(acc_ref, o_ref)


def _attention(q, k, v, static_ok, q_start, q_rows, k_start, k_rows, bq, bk):
    b, h, _, e = q.shape
    bq, bk = min(bq, q_rows), min(bk, k_rows)
    qi0, kj0 = q_start // bq, k_start // k_rows
    kv_blk = pl.BlockSpec((1, 2, k_rows, e), lambda bi, p, i: (bi, p, kj0, 0))

    def call(online):
        scratch = [pltpu.VMEM((2, bq, LANE), F32)]
        if online:
            scratch.append(pltpu.VMEM((2, bq, 1), F32))
        else:
            scratch += [pltpu.VMEM((2, bq, bk), BF16)] * 2
        return pl.pallas_call(
            functools.partial(_attn_online_body if online else _attn_static_body, bk=bk),
            grid=(b, h // 2, q_rows // bq),
            in_specs=[pl.BlockSpec((1, 2, bq, e), lambda bi, p, i: (bi, p, qi0 + i, 0)),
                      kv_blk, kv_blk],
            out_specs=pl.BlockSpec((1, bq, 2 * V_HEAD), lambda bi, p, i: (bi, i, p)),
            out_shape=jax.ShapeDtypeStruct((b, q_rows, h * V_HEAD), ACT),
            scratch_shapes=scratch,
            compiler_params=_cparams(("parallel", "parallel", "arbitrary")),
            name="attention_online" if online else "attention",
        )(q, k, v)

    return lax.cond(static_ok, lambda: call(False), lambda: call(True))


def _dft_mats(n):
    idx = np.arange(n)
    ang = 2.0 * np.pi * ((idx[:, None] * idx[None, :]) % n) / n
    return np.cos(ang), np.sin(ang)


def _hilo(a):
    a = jnp.asarray(a, F32)
    hi = a.astype(BF16)
    return hi, (a - hi.astype(F32)).astype(BF16)


def _mm3c(ah, al, b, dn=NN):
    bh, bl = _split(b)
    d = lambda x, y: lax.dot_general(x, y, dn, preferred_element_type=F32)
    return d(ah, bh) + d(al, bh) + d(ah, bl)


def _four_rows_body(x_ref, w_ref, tc_ref, ts_ref, o_ref):
    r = tc_ref.shape[0]
    y = jnp.dot(w_ref[...], x_ref[0, 0], preferred_element_type=F32)
    yc, ys = y[:r], y[r:]
    tc, ts = tc_ref[...], ts_ref[...]
    zc = yc * tc - ys * ts
    zs = yc * ts + ys * tc
    gd = FNET_GROUP_DIM
    for j in range(zc.shape[1] // gd):
        o_ref[0, 0, 0, :, j, :] = zc[:, j * gd:(j + 1) * gd]
        o_ref[0, 0, 1, :, j, :] = zs[:, j * gd:(j + 1) * gd]


def _four_cols_body(y_ref, w_ref, cs_ref, wf_ref, o_ref, y3_scr, *, krt, scale):
    def one(j, carry):
        rows = pl.ds(pl.multiple_of(j * GRID_W, GRID_W), GRID_W)
        ycs = jnp.concatenate([y_ref[0, 0, 0, rows, :], y_ref[0, 0, 1, rows, :]], axis=0)
        y3 = jnp.dot(w_ref[...], ycs.astype(BF16),
                     preferred_element_type=F32)
        y3_scr[rows, :] = jnp.concatenate([y3[:GRID_W], y3[GRID_W:]], axis=1).astype(BF16)
        return carry

    lax.fori_loop(0, krt, one, 0, unroll=8)
    f = jnp.dot(y3_scr[...], cs_ref[...], preferred_element_type=F32) * scale
    o_ref[0, 0] = _mm(f, wf_ref[0]).astype(o_ref.dtype)


def _fourier_latent(xf, w_fnet):
    b, g, t, gd = xf.shape
    r = t // GRID_W
    wide = GRID_W * gd
    xv = xf.reshape(b, g, r, wide)
    cr, sr = _dft_mats(r)
    w_rows = jnp.asarray(np.concatenate([cr, sr], axis=0), BF16)
    kr_i, c_i = np.arange(r)[:, None], np.arange(GRID_W)[None, :]
    ang = 2.0 * np.pi * ((kr_i * c_i) % t) / t
    twc = jnp.repeat(jnp.asarray(np.cos(ang), F32), gd, axis=1)
    tws = jnp.repeat(jnp.asarray(np.sin(ang), F32), gd, axis=1)
    tl = min(FOUR_LANE_TILE, wide)
    y2 = pl.pallas_call(
        _four_rows_body,
        grid=(b, g, wide // tl),
        in_specs=[pl.BlockSpec((1, 1, r, tl), lambda bi, gi, l: (bi, gi, 0, l)),
                  pl.BlockSpec((2 * r, r), lambda bi, gi, l: (0, 0)),
                  pl.BlockSpec((r, tl), lambda bi, gi, l: (0, l)),
                  pl.BlockSpec((r, tl), lambda bi, gi, l: (0, l))],
        out_specs=pl.BlockSpec((1, 1, 2, r, tl // gd, gd), lambda bi, gi, l: (bi, gi, 0, 0, l, 0)),
        out_shape=jax.ShapeDtypeStruct((b, g, 2, r, GRID_W, gd), F32),
        compiler_params=_cparams(("parallel", "parallel", "parallel")),
        name="fourier_rows",
    )(xv, w_rows, twc, tws)
    y2v = y2.reshape(b, g, 2, r * GRID_W, gd)
    c64, s64 = _dft_mats(GRID_W)
    w_cols = jnp.asarray(np.block([[c64, -s64], [s64, c64]]), BF16)
    cc, sc = _dft_mats(gd)
    w_chan = jnp.asarray(np.concatenate([cc, -sc], axis=0), BF16)
    krt = min(FOUR_ROWS_TILE, r)
    const = lambda a: pl.BlockSpec(a.shape, lambda bi, gi, i: (0, 0))
    fo = pl.pallas_call(
        functools.partial(_four_cols_body, krt=krt, scale=1.0 / math.sqrt(t * gd)),
        grid=(b, g, r // krt),
        in_specs=[pl.BlockSpec((1, 1, 2, krt * GRID_W, gd), lambda bi, gi, i: (bi, gi, 0, i, 0)),
                  const(w_cols), const(w_chan),
                  pl.BlockSpec((1, gd, gd), lambda bi, gi, i: (gi, 0, 0))],
        out_specs=pl.BlockSpec((1, 1, krt * GRID_W, gd), lambda bi, gi, i: (bi, gi, i, 0)),
        out_shape=jax.ShapeDtypeStruct((b, g, r * GRID_W, gd), ACT),
        scratch_shapes=[pltpu.VMEM((krt * GRID_W, 2 * gd), BF16)],
        compiler_params=_cparams(("parallel", "parallel", "parallel")),
        name="fourier_cols",
    )(y2v, w_cols, w_chan, w_fnet)
    return fo.reshape(b, g, r, GRID_W, gd).transpose(0, 1, 3, 2, 4).reshape(b, g, t, gd)


def _four_dense_body(x_ref, ch_ref, cl_ref, th_ref, tl_ref, sh_ref, sl_ref, wf_ref, o_ref, *, scale):
    x = x_ref[0, 0]
    xh, xl = _split(x)
    d = lambda a, b: jnp.dot(a, b, preferred_element_type=F32)
    z = d(xh, ch_ref[...]) + d(xl, ch_ref[...]) + d(xh, cl_ref[...])
    zc, zs = z[:, :FNET_GROUP_DIM], z[:, FNET_GROUP_DIM:]
    f = (_mm3c(th_ref[...], tl_ref[...], zc) - _mm3c(sh_ref[...], sl_ref[...], zs)) * scale
    o_ref[0, 0] = _mm3(f, wf_ref[0]).astype(o_ref.dtype)


def _fourier_dense(xf, w_fnet):
    b, g, t, gd = xf.shape
    cc, sc = _dft_mats(gd)
    ch, cl = _hilo(np.concatenate([cc, sc], axis=1))
    ct, st = _dft_mats(t)
    cth, ctl = _hilo(ct)
    sth, stl = _hilo(st)
    sq = pl.BlockSpec((t, t), lambda bi, gi: (0, 0))
    cs = pl.BlockSpec((gd, 2 * gd), lambda bi, gi: (0, 0))
    return pl.pallas_call(
        functools.partial(_four_dense_body, scale=1.0 / math.sqrt(t * gd)),
        grid=(b, g),
        in_specs=[pl.BlockSpec((1, 1, t, gd), lambda bi, gi: (bi, gi, 0, 0)), cs, cs, sq, sq, sq, sq,
                  pl.BlockSpec((1, gd, gd), lambda bi, gi: (gi, 0, 0))],
        out_specs=pl.BlockSpec((1, 1, t, gd), lambda bi, gi: (bi, gi, 0, 0)),
        out_shape=jax.ShapeDtypeStruct((b, g, t, gd), ACT),
        compiler_params=_cparams(("parallel", "parallel")),
        name="fourier_dense",
    )(xf, ch, cl, cth, ctl, sth, stl, w_fnet)


def _merge_body(o_ref, f_ref, gate_ref, x_ref, gl_ref, w_ref, out_ref):
    gt = gate_ref[0].astype(F32)
    parts = [o_ref[0]] + [f_ref[0, gi] for gi in range(f_ref.shape[1])]
    mix = jnp.concatenate(parts, axis=-1).astype(F32) * (gt * _sigmoid(gt))
    y = jnp.dot(mix.astype(BF16), w_ref[...], preferred_element_type=F32)
    out_ref[0] = x_ref[0] + gl_ref[0] * y


def _merge(o, f, gate, x, gl, w, tm):
    b, t, d = x.shape
    tm = min(tm, t)
    half = o.shape[2]
    tok = lambda n: pl.BlockSpec((1, tm, n), lambda bi, i: (bi, i, 0))
    return pl.pallas_call(
        _merge_body,
        grid=(b, t // tm),
        in_specs=[tok(half),
                  pl.BlockSpec((1, f.shape[1], tm, f.shape[3]), lambda bi, i: (bi, 0, i, 0)),
                  tok(d), tok(d),
                  pl.BlockSpec((1, 1, d), lambda bi, i: (bi, 0, 0)),
                  pl.BlockSpec(w.shape, lambda bi, i: (0, 0))],
        out_specs=tok(d),
        out_shape=jax.ShapeDtypeStruct((b, t, d), F32),
        compiler_params=_cparams(("parallel", "parallel")),
        name="merge",
    )(o, f, gate, x, gl, w)


EXP_M05 = math.exp(-0.5)


PAIRS_PER_STEP = 8
PAIRS_PER_GROUP = 8
LOCAL_CHUNKS = 4
GROUP_LAG = 5


def _rwkv_local_body(*refs, cs):
    zwa_ref, w0_ref, w2_ref, a0_ref, a2_ref = refs[3:8]
    zwa = zwa_ref[0].astype(F32)
    lora = (_mm(jnp.tanh(zwa[:, :LANE]), w2_ref[...]) + w0_ref[...],
            _mm(zwa[:, LANE:], a2_ref[...]) + a0_ref[...])
    groups = [_rwkv_local_group(*refs, lora=lora, base=base, ck=ck) for ck in range(cs)
              for base in range(0, PAIRS_PER_STEP, PAIRS_PER_GROUP)]
    tick = 0
    while groups:
        live = groups[:tick // GROUP_LAG + 1]
        for g in live:
            if next(g, StopIteration) is StopIteration:
                groups.remove(g)
        tick += 1


def _rwkv_local_group(zk_ref, zv_ref, zr_ref, zwa_ref, w0_ref, w2_ref, a0_ref, a2_ref,
                      kk_ref, ka_ref, rk_ref, m_ref, g_ref, qt_ref, yl_ref, bn_ref,
                      *, lora, base, ck):
    c = CHUNK
    rows = slice(ck * CHUNK, (ck + 1) * CHUNK)

    head0 = _iota2((1, LANE), 1) < RWKV_HEAD
    r2 = _iota2((LANE, LANE), 0)
    c2 = _iota2((LANE, LANE), 1)
    same = (r2 // RWKV_HEAD) == (c2 // RWKV_HEAD)
    ones_bd = same.astype(F32)
    eye = r2 == c2

    def stack(x):
        z = jnp.zeros_like(x)
        return jnp.concatenate([jnp.where(head0, x, z), jnp.where(head0, z, x)], axis=0)

    stack_b = lambda x: stack(x.astype(BF16))
    fold = lambda x: x[:c] + x[c:]

    pairs = range(PAIRS_PER_GROUP)
    chains = [(q, d) for q in pairs for d in range(2)]
    qls = [slice((base + q) * LANE, (base + q + 1) * LANE) for q in pairs]
    ks = [zk_ref[0, rows, ql].astype(F32) for ql in qls]
    vs_ = [zv_ref[0, rows, ql].astype(F32) for ql in qls]
    rs = [zr_ref[0, rows, ql].astype(F32) for ql in qls]
    pcols = [slice((base + q) * 2 * LANE, (base + q + 1) * 2 * LANE) for q in pairs]
    wraw = [lora[0][rows, pc] for pc in pcols]
    araw = [lora[1][rows, pc] for pc in pcols]
    yield
    logw = [-EXP_M05 * _sigmoid(w) for w in wraw]
    a_all = [_sigmoid(a) for a in araw]
    kk0 = [ks[q] * kk_ref[:, qls[q]] for q in pairs]
    ss = [_mm2r(x * x, ones_bd) for x in kk0]
    yield
    kk = [kk0[q] / jnp.maximum(jnp.sqrt(ss[q]), 1e-12) for q in pairs]
    vstk = [stack_b(v) for v in vs_]

    dsl = [slice(d * LANE, (d + 1) * LANE) for d in range(2)]
    lw = [logw[q][:, dsl[d]] for q, d in chains]
    ad = [a_all[q][:, dsl[d]] for q, d in chains]
    kd = [ks[q] * (1.0 + (ad[i] - 1.0) * ka_ref[:, qls[q]]) for i, (q, d) in enumerate(chains)]
    bb = [kk[q] * ad[i] for i, (q, d) in enumerate(chains)]
    bonus = [_mm2r(rs[q] * (kd[2 * q] + kd[2 * q + 1]) * rk_ref[:, qls[q]], ones_bd) * vs_[q]
             for q in pairs]
    yield
    r3 = _iota2((c, 3 * c), 0)
    c3 = _iota2((c, 3 * c), 1) & (c - 1)
    tri3 = (c3 <= r3).astype(BF16)
    tt = _iota2((c, LANE), 0)
    ts = _iota2((c, LANE), 1) & (c - 1)
    strict = [ts < tt, ts > tt]
    incl = [ts <= tt, ts >= tt]
    eye_c = (ts == tt).astype(F32)

    def prefix(x):
        xh, xl = _split(x)
        xll = (x - xh.astype(F32) - xl.astype(F32)).astype(BF16)
        return jnp.dot(tri3, jnp.concatenate([xh, xl, xll], axis=0), preferred_element_type=F32)

    pre = [prefix(x) for x in logw]
    ltot = [pre[q][c - 1:c, dsl[d]] for q, d in chains]
    lc = [pre[q][:, dsl[0]] if d == 0 else ltot[i] - pre[q][:, dsl[1]] + lw[i]
          for i, (q, d) in enumerate(chains)]
    yield
    n = len(chains)
    bdot = lambda a, b: jnp.dot(a, b, preferred_element_type=F32)
    kkd = [(kk[q] * jnp.exp(lc[i] - lw[i])).astype(BF16) for i, (q, d) in enumerate(chains)]
    rd = [rs[q] * jnp.exp(lc[i]) for i, (q, d) in enumerate(chains)]
    e_inv = [jnp.exp(-x) for x in lc]
    inv_s = [jnp.concatenate([stack_b(bb[i] * e_inv[i]), stack_b(kd[i] * e_inv[i])], axis=0)
             for i in range(n)]
    yield
    amat = [lax.dot_general(jnp.concatenate([kkd[i], rd[i].astype(BF16)], axis=0), inv_s[i],
                            NT, preferred_element_type=F32).astype(BF16) for i in range(n)]
    zero_c = jnp.zeros((c, LANE), BF16)
    a_kb = [jnp.where(strict[d], amat[i][:c, :LANE], zero_c) for i, (q, d) in enumerate(chains)]
    a_kk = [jnp.where(strict[d], amat[i][:c, LANE:], zero_c) for i, (q, d) in enumerate(chains)]
    aq_b = [jnp.where(incl[d], amat[i][c:, :LANE], zero_c) for i, (q, d) in enumerate(chains)]
    aq_k = [jnp.where(incl[d], amat[i][c:, LANE:], zero_c) for i, (q, d) in enumerate(chains)]
    yield
    av = [bdot(jnp.concatenate([a_kk[i], aq_k[i]], axis=0), vstk[q])
          for i, (q, d) in enumerate(chains)]
    yield
    tinv = [eye_c - a.astype(F32) for a in a_kb]
    qpow = [bdot(a, stack(a)).astype(BF16) for a in a_kb]
    yield
    for _ in range(4):
        prod = [bdot(qpow[i], jnp.concatenate([stack(qpow[i]), stack_b(tinv[i])], axis=1))
                for i in range(n)]
        qpow = [x[:, :LANE].astype(BF16) for x in prod]
        tinv = [tinv[i] + prod[i][:, LANE:] for i in range(n)]
        yield
    tinv = [tinv[i] + bdot(qpow[i], stack_b(tinv[i])) for i in range(n)]
    yield
    tsplit = [_split(t) for t in tinv]
    ia_t = [bdot((eye_c + a_kb[i].astype(F32)).astype(BF16),
                 jnp.concatenate([stack(tsplit[i][0]), stack(tsplit[i][1])], axis=1))
            for i in range(n)]
    resid = [eye_c - ia_t[i][:, :LANE] - ia_t[i][:, LANE:] for i in range(n)]
    yield
    tinv = [tinv[i] + bdot(tinv[i].astype(BF16), stack_b(resid[i])) for i in range(n)]
    yield
    x = [bdot(tinv[i].astype(BF16),
              jnp.concatenate([stack(kkd[i]), stack_b(av[i][:c])], axis=1))
         for i in range(n)]
    xb = [v.astype(BF16) for v in x]
    yield
    qy = [jnp.concatenate([rd[i], av[i][c:]], axis=1)
          - bdot(aq_b[i], jnp.concatenate([stack(xb[i][:, :LANE]), stack(xb[i][:, LANE:])],
                                          axis=1)) for i in range(n)]
    yield
    e_end = [jnp.exp(ltot[i] - lc[i]) for i in range(n)]
    ends = [jnp.concatenate([(-bb[i] * e_end[i]).astype(BF16), (kd[i] * e_end[i]).astype(BF16)],
                            axis=0) for i in range(n)]
    wuv = [jnp.concatenate([xb[i], jnp.concatenate([zero_c, vs_[q].astype(BF16)], axis=1)],
                           axis=0) for i, (q, d) in enumerate(chains)]
    mg = [lax.dot_general(ends[i], wuv[i], TN, preferred_element_type=F32) for i in range(n)]
    yield
    for i, (q, d) in enumerate(chains):
        m_ref[0, ck, d, base + q] = fold(jnp.where(eye, jnp.exp(ltot[i]), 0.0)
                                         + jnp.where(same, mg[i][:, :LANE], 0.0)).astype(BF16)
        g_ref[0, ck, d, base + q] = fold(jnp.where(same, mg[i][:, LANE:], 0.0)
                                         ).astype(g_ref.dtype)
        qt_ref[0, d, rows, qls[q]] = qy[i][:, :LANE].astype(BF16)
    for q in pairs:
        yl_ref[0, rows, qls[q]] = (qy[2 * q][:, LANE:]
                                   + qy[2 * q + 1][:, LANE:]).astype(yl_ref.dtype)
        bn_ref[0, rows, qls[q]] = bonus[q].astype(bn_ref.dtype)


def _rwkv_local(z, w0p, w2p, a0p, a2p, k_k, k_a, r_k):
    b, t, _ = z.shape
    w = k_k.shape[1]
    npair = w // LANE
    pp = PAIRS_PER_STEP
    ng = npair // pp
    wl = pp * LANE
    nc = t // CHUNK
    cs = LOCAL_CHUNKS if nc % LOCAL_CHUNKS == 0 else 1
    rows = cs * CHUNK
    tokc = lambda base: pl.BlockSpec((1, rows, wl), lambda bi, ci, p: (bi, ci, base + p))
    perp3 = lambda n: pl.BlockSpec((n, pp * 2 * LANE), lambda bi, ci, p: (0, p))
    vecp = pl.BlockSpec((1, wl), lambda bi, ci, p: (0, p))
    mat = pl.BlockSpec((1, cs, 2, pp, CHUNK, LANE), lambda bi, ci, p: (bi, ci, 0, p, 0, 0))
    return pl.pallas_call(
        functools.partial(_rwkv_local_body, cs=cs),
        grid=(b, nc // cs, ng),
        in_specs=[tokc(0), tokc(ng), tokc(2 * ng),
                  pl.BlockSpec((1, rows, 2 * LANE), lambda bi, ci, p: (bi, ci, 3 * npair // 2)),
                  perp3(1), perp3(LANE), perp3(1), perp3(LANE), vecp, vecp, vecp],
        out_specs=[mat, mat,
                   pl.BlockSpec((1, 2, rows, wl), lambda bi, ci, p: (bi, 0, ci, p)),
                   pl.BlockSpec((1, rows, wl), lambda bi, ci, p: (bi, ci, p)),
                   pl.BlockSpec((1, rows, wl), lambda bi, ci, p: (bi, ci, p))],
        out_shape=[jax.ShapeDtypeStruct((b, nc, 2, npair, CHUNK, LANE), BF16),
                   jax.ShapeDtypeStruct((b, nc, 2, npair, CHUNK, LANE), ACT),
                   jax.ShapeDtypeStruct((b, 2, t, w), BF16),
                   jax.ShapeDtypeStruct((b, t, w), ACT),
                   jax.ShapeDtypeStruct((b, t, w), ACT)],
        compiler_params=_cparams(("parallel", "parallel", "parallel")),
        name="rwkv_local",
    )(z, z, z, z, w0p, w2p, a0p, a2p, k_k, k_a, r_k)


SCAN_CHUNKS = 8


def _rwkv_scan_body(m0_ref, g0_ref, q0_ref, m1_ref, g1_ref, q1_ref, h0_ref,
                    y0_ref, y1_ref, hfin_ref, h_scr, *, npair, cs):
    ci = pl.program_id(1)

    @pl.when(ci == 0)
    def _():
        h_scr[...] = h0_ref[0]

    head0 = _iota2((1, LANE), 1) < RWKV_HEAD

    def expand(x):
        z = jnp.zeros_like(x)
        return jnp.concatenate([jnp.where(head0, x, z), jnp.where(head0, z, x)], axis=0)

    refs = ((m0_ref, g0_ref, q0_ref, y0_ref), (m1_ref, g1_ref, q1_ref, y1_ref))
    chains = [(d, p) for d in range(2) for p in range(npair)]
    lanes = [slice(p * LANE, (p + 1) * LANE) for p in range(npair)]
    h = [h_scr[d, p] for d, p in chains]
    for step in range(cs):
        ck = (step, cs - 1 - step)
        rows = [slice(c * CHUNK, (c + 1) * CHUNK) for c in ck]
        hb = [x.astype(BF16) for x in h]
        res = [jnp.dot(jnp.concatenate([refs[d][2][0, 0, rows[d], lanes[p]],
                                        refs[d][0][0, ck[d], 0, p]], axis=0), hb[i],
                       preferred_element_type=F32) for i, (d, p) in enumerate(chains)]
        for i, (d, p) in enumerate(chains):
            refs[d][3][0, rows[d], lanes[p]] = res[i][:CHUNK].astype(refs[d][3].dtype)
        h = [expand(res[i][CHUNK:] + refs[d][1][0, ck[d], 0, p].astype(F32))
             for i, (d, p) in enumerate(chains)]
    for i, (d, p) in enumerate(chains):
        h_scr[d, p] = h[i]

    @pl.when(ci == pl.num_programs(1) - 1)
    def _():
        hfin_ref[0] = h_scr[...]


def _rwkv_scan(mm, gg, qt, h0):
    b, nc, _, npair, _, _ = mm.shape
    t, w = qt.shape[2], qt.shape[3]
    cs = SCAN_CHUNKS if nc % SCAN_CHUNKS == 0 else 1
    nb = nc // cs
    fwd = lambda bi, ci: (bi, ci, 0, 0, 0, 0)
    rev = lambda bi, ci: (bi, nb - 1 - ci, 1, 0, 0, 0)
    mblk = (1, cs, 1, npair, CHUNK, LANE)
    hspec = pl.BlockSpec((1, 2, npair, LANE, LANE), lambda bi, ci: (bi, 0, 0, 0, 0))
    return pl.pallas_call(
        functools.partial(_rwkv_scan_body, npair=npair, cs=cs),
        grid=(b, nb),
        in_specs=[pl.BlockSpec(mblk, fwd), pl.BlockSpec(mblk, fwd),
                  pl.BlockSpec((1, 1, cs * CHUNK, w), lambda bi, ci: (bi, 0, ci, 0)),
                  pl.BlockSpec(mblk, rev), pl.BlockSpec(mblk, rev),
                  pl.BlockSpec((1, 1, cs * CHUNK, w), lambda bi, ci: (bi, 1, nb - 1 - ci, 0)),
                  hspec],
        out_specs=[pl.BlockSpec((1, cs * CHUNK, w), lambda bi, ci: (bi, ci, 0)),
                   pl.BlockSpec((1, cs * CHUNK, w), lambda bi, ci: (bi, nb - 1 - ci, 0)),
                   hspec],
        out_shape=[jax.ShapeDtypeStruct((b, t, w), ACT), jax.ShapeDtypeStruct((b, t, w), ACT),
                   jax.ShapeDtypeStruct(h0.shape, F32)],
        scratch_shapes=[pltpu.VMEM((2, npair, LANE, LANE), F32)],
        compiler_params=_cparams(("parallel", "arbitrary")),
        name="rwkv_scan",
    )(mm, gg, qt, mm, gg, qt, h0)


def _rwkv_out_body(y0_ref, y1_ref, yl_ref, bn_ref, gate_ref, x_ref, gl_ref, gnw_ref, gnb_ref,
                   w_ref, o_ref):
    y = y0_ref[0].astype(F32) + y1_ref[0].astype(F32) + yl_ref[0].astype(F32)
    r2 = _iota2((LANE, LANE), 0)
    c2 = _iota2((LANE, LANE), 1)
    avg = ((r2 // RWKV_HEAD) == (c2 // RWKV_HEAD)).astype(F32) * (1.0 / RWKV_HEAD)
    parts = []
    for p in range(y.shape[1] // LANE):
        yp = y[:, p * LANE:(p + 1) * LANE]
        dl = yp - _mm2r(yp, avg)
        var = _mm2r(dl * dl, avg)
        parts.append(dl * lax.rsqrt(var + GN_EPS))
    yn = jnp.concatenate(parts, axis=1)
    gt = gate_ref[0].astype(F32)
    act = (yn * gnw_ref[...] + gnb_ref[...] + bn_ref[0].astype(F32)) * (gt * _sigmoid(gt))
    out = jnp.dot(act.astype(BF16), w_ref[...], preferred_element_type=F32)
    o_ref[0] = x_ref[0] + gl_ref[0] * out


def _rwkv_out(y0, y1, yl, bn, gate, x, gl, gnw, gnb, w, tm):
    b, t, d = x.shape
    tm = min(tm, t)
    wd = y0.shape[2]
    tok = lambda n: pl.BlockSpec((1, tm, n), lambda bi, i: (bi, i, 0))
    return pl.pallas_call(
        _rwkv_out_body,
        grid=(b, t // tm),
        in_specs=[tok(wd), tok(wd), tok(wd), tok(wd), tok(wd), tok(d),
                  pl.BlockSpec((1, 1, d), lambda bi, i: (bi, 0, 0)),
                  pl.BlockSpec((1, wd), lambda bi, i: (0, 0)),
                  pl.BlockSpec((1, wd), lambda bi, i: (0, 0)),
                  pl.BlockSpec(w.shape, lambda bi, i: (0, 0))],
        out_specs=tok(d),
        out_shape=jax.ShapeDtypeStruct((b, t, d), F32),
        compiler_params=_cparams(("parallel", "parallel")),
        name="rwkv_out",
    )(y0, y1, yl, bn, gate, x, gl, gnw, gnb, w)


def _rope_tables(t):
    rows = t // GRID_W
    row = jnp.repeat(jnp.arange(rows, dtype=F32), GRID_W)
    col = jnp.tile(jnp.arange(GRID_W, dtype=F32), rows)
    inv = 1.0 / (ROPE_BASE ** (jnp.arange(ROPE_FREQS, dtype=F32) / ROPE_FREQS))
    ang = jnp.stack([row[:, None] * inv, col[:, None] * inv], axis=1)
    cos, sin = jnp.cos(ang), jnp.sin(ang)
    zeros = jnp.zeros_like(sin)
    ones_lo = jnp.ones((t, QK_NOPE), F32)
    pad_hi = HEAD_SLOT - QK_HEAD
    cos_t = jnp.concatenate([ones_lo, jnp.concatenate([cos, cos], axis=2).reshape(t, QK_ROPE),
                             jnp.ones((t, pad_hi), F32)], axis=1)
    sa = jnp.concatenate([jnp.zeros((t, QK_NOPE), F32),
                          jnp.concatenate([-sin, zeros], axis=2).reshape(t, QK_ROPE),
                          jnp.zeros((t, pad_hi), F32)], axis=1)
    sb = jnp.concatenate([jnp.zeros((t, QK_NOPE), F32),
                          jnp.concatenate([zeros, sin], axis=2).reshape(t, QK_ROPE),
                          jnp.zeros((t, pad_hi), F32)], axis=1)
    return cos_t, sa, sb


def _even_layer(x, ctx, mod_l, mod_c, need_ctx, g, w_in, kv_norm, q_norm, w_uq, w_ukv,
                q_head_norm, k_head_norm, w_fnet, w_out):
    b, s, d = x.shape
    tc = ctx.shape[1]
    e_q0 = KV_LORA + QK_ROPE
    e_f0 = e_q0 + Q_LORA
    e_g0 = e_f0 + FNET_GROUPS * FNET_GROUP_DIM
    w_p = jnp.concatenate([w_in[:, e_g0:], w_in[:, e_f0:e_g0], w_in[:, :e_q0],
                           jnp.zeros((d, LANE - QK_ROPE), F32), w_in[:, e_q0:e_f0]],
                          axis=1).astype(BF16)
    splits = (d, (FNET_GROUPS, FNET_GROUP_DIM), KV_LORA + LANE + Q_LORA)
    kvw = w_ukv.reshape(KV_LORA, MLA_HEADS, QK_NOPE + V_HEAD)
    wk = jnp.pad(kvw[:, :, :QK_NOPE], ((0, 0), (0, 0), (0, HEAD_SLOT - QK_NOPE)))
    wk = wk.reshape(KV_LORA, MLA_HEADS * HEAD_SLOT).astype(BF16)
    wv = jnp.pad(kvw[:, :, QK_NOPE:], ((0, 0), (0, 0), (0, HEAD_SLOT - V_HEAD)))
    wv = wv.reshape(KV_LORA, MLA_HEADS * HEAD_SLOT).astype(BF16)
    wq3 = jnp.pad(w_uq.reshape(Q_LORA, MLA_HEADS, QK_HEAD), ((0, 0), (0, 0), (0, HEAD_SLOT - QK_HEAD)))
    wq = wq3.reshape(Q_LORA, MLA_HEADS * HEAD_SLOT).astype(BF16)
    kg = jnp.pad(k_head_norm, (0, HEAD_SLOT - QK_HEAD)).reshape(1, HEAD_SLOT)
    qg = (jnp.pad(q_head_norm, (0, HEAD_SLOT - QK_HEAD))
          * (QK_HEAD ** -0.5 * math.log2(math.e))).reshape(1, HEAD_SLOT)
    lane = np.arange(HEAD_SLOT)
    tail = (lane >= QK_NOPE) & (lane < QK_HEAD)
    first = tail & (((lane - QK_NOPE) // ROPE_FREQS) % 2 == 0)
    partner = np.where(first, lane + ROPE_FREQS, np.where(tail, lane - ROPE_FREQS, lane))
    sign = np.where(first, -1.0, np.where(tail, 1.0, 0.0)).astype(np.float32)
    wqr = (wq3[:, :, partner] * (sign * qg[0, partner])).reshape(Q_LORA, MLA_HEADS * HEAD_SLOT)
    wqr = wqr.astype(BF16)
    kvn, qn = kv_norm.reshape(1, -1), q_norm.reshape(1, -1)
    g2 = g.reshape(1, d)
    bound = (1.02 * QK_HEAD * jnp.max(jnp.abs(qg)) * jnp.max(jnp.abs(kg))).astype(BF16).astype(F32)
    static_ok = bound <= MAX_STATIC_BOUND
    bias_lane = (jnp.arange(HEAD_SLOT) == BIAS_LANE).astype(F32).reshape(1, HEAD_SLOT)
    kb = bias_lane * jnp.where(static_ok, -bound, 0.0)
    qb = bias_lane

    gate_l, four_l, ua_l = _proj(x, g2, mod_l[1], mod_l[0], w_p, splits, TOKEN_TILE)
    gate_c, four_c, ua_c = _proj(ctx, g2, mod_c[1], mod_c[0], w_p, splits, TOKEN_TILE)
    sk = s + tc
    cos_t, sa, sb = _rope_tables(s)
    cos_t = jnp.concatenate([cos_t, jnp.ones((tc, HEAD_SLOT), F32)], axis=0)
    sa = jnp.concatenate([sa, jnp.zeros((tc, HEAD_SLOT), F32)], axis=0)
    sb = jnp.concatenate([sb, jnp.zeros((tc, HEAD_SLOT), F32)], axis=0)
    tabs = (cos_t * kg, cos_t * qg, sa, sb, sb - sa)
    q_all, k_all, v_all = _qkv(ua_l, ua_c, kvn, qn, wk, wv, wq, wqr, kg, kb, qb, tabs,
                               math.gcd(s, tc))
    bk = ATTN_K_BLOCK if sk % ATTN_K_BLOCK == 0 else tc
    o_l = _attention(q_all, k_all, v_all, static_ok, 0, s, 0, sk, ATTN_Q_BLOCK, bk)
    f_l = _fourier_latent(four_l, w_fnet)
    wo = w_out.astype(BF16)
    x_new = _merge(o_l, f_l, gate_l, x, mod_l[2], wo, TOKEN_TILE)
    ctx_new = ctx
    if need_ctx:
        o_c = _attention(q_all, k_all, v_all, static_ok, s, tc, s, tc, tc, tc)
        f_c = _fourier_dense(four_c, w_fnet)
        ctx_new = _merge(o_c, f_c, gate_c, ctx, mod_c[2], wo, TOKEN_TILE)
    return x_new, ctx_new


def _odd_layer(x, ctx, mod_l, mod_c, need_ctx, g, w_in, shift_w, w0, w2, a0, a2, k_k, k_a, r_k,
               gn_w, gn_b, w_out):
    b, s, d = x.shape
    w = k_k.shape[0]
    npair = w // LANE
    o_wd0 = 2 * w
    o_r0 = o_wd0 + 2 * DECAY_LORA + 2 * AAA_LORA
    conv_ch = o_r0 + w
    segs = ((0, o_wd0), (o_r0, w), (o_wd0, o_r0 - o_wd0))
    w_p = w_in.astype(BF16)
    sw = shift_w
    g2 = g.reshape(1, d)

    def pairs(vec2):
        return vec2.reshape(2, npair, LANE).transpose(1, 0, 2).reshape(1, npair * 2 * LANE)

    def pair_mats(m):
        rr = m.shape[1]
        mp = m.reshape(2, rr, npair, LANE).transpose(2, 0, 1, 3)
        z = jnp.zeros_like(mp[:, 0])
        top = jnp.concatenate([mp[:, 0], z], axis=2)
        bot = jnp.concatenate([z, mp[:, 1]], axis=2)
        full = jnp.concatenate([top, bot], axis=1)
        return full.transpose(1, 0, 2).reshape(2 * rr, npair * 2 * LANE).astype(BF16)

    w0p, a0p, w2p, a2p = pairs(w0), pairs(a0), pair_mats(w2), pair_mats(a2)
    kk2, ka2, rk2 = k_k.reshape(1, w), k_a.reshape(1, w), r_k.reshape(1, w)
    wo = w_out.astype(BF16)

    def mix(xin, mod, h0):
        z, gate = _proj_shift(xin, g2, mod[1], mod[0], w_p, sw, conv_ch, segs, TOKEN_TILE)
        mm, gg, qt, yl, bn = _rwkv_local(z, w0p, w2p, a0p, a2p, kk2, ka2, rk2)
        y0, y1, hfin = _rwkv_scan(mm, gg, qt, h0)
        return (y0, y1, yl, bn, gate), hfin

    h_zero = jnp.zeros((b, 2, npair, LANE, LANE), F32)
    parts_c, h_ctx = mix(ctx, mod_c, h_zero)
    parts_l, _ = mix(x, mod_l, h_ctx)
    gnw, gnb = gn_w.reshape(1, w), gn_b.reshape(1, w)
    x_new = _rwkv_out(*parts_l, x, mod_l[2], gnw, gnb, wo, TOKEN_TILE)
    ctx_new = ctx
    if need_ctx:
        ctx_new = _rwkv_out(*parts_c, ctx, mod_c[2], gnw, gnb, wo, TOKEN_TILE)
    return x_new, ctx_new


def kernel(x, c, ctx, c_ctx, ada_w, ada_b, norm_g, e_w_in, e_kv_norm, e_q_norm, e_w_uq, e_w_ukv,
           e_q_head_norm, e_k_head_norm, e_w_fnet, e_w_out, o_w_in, o_shift_w, o_w0, o_w2, o_a0,
           o_a2, o_k_k, o_k_a, o_r_k, o_gn_w, o_gn_b, o_w_out):
    b, s, d = x.shape
    depth = ada_w.shape[0]
    assert b + 1 <= 8
    cond8 = jnp.concatenate([c, c_ctx[None, :], jnp.zeros((8 - b - 1, d), F32)], axis=0)
    mod = _ada(cond8, ada_w, ada_b)
    for layer in range(depth):
        need_ctx = layer < depth - 1
        m = mod[layer]
        chunk = lambda rows, i: rows[:, None, i * d:(i + 1) * d]
        lat, cx = m[:b], jnp.broadcast_to(m[b:b + 1], (b, 3 * d))
        mod_l = (chunk(lat, 0), 1.0 + chunk(lat, 1), chunk(lat, 2))
        mod_c = (chunk(cx, 0), 1.0 + chunk(cx, 1), chunk(cx, 2))
        j = layer // 2
        if layer % 2 == 0:
            x, ctx = _even_layer(x, ctx, mod_l, mod_c, need_ctx, norm_g[layer], e_w_in[j],
                                 e_kv_norm[j], e_q_norm[j], e_w_uq[j], e_w_ukv[j],
                                 e_q_head_norm[j], e_k_head_norm[j], e_w_fnet[j], e_w_out[j])
        else:
            x, ctx = _odd_layer(x, ctx, mod_l, mod_c, need_ctx, norm_g[layer], o_w_in[j],
                                o_shift_w[j], o_w0[j], o_w2[j], o_a0[j], o_a2[j], o_k_k[j],
                                o_k_a[j], o_r_k[j].reshape(-1), o_gn_w[j], o_gn_b[j], o_w_out[j])
    return x
```

```python
import functools
import math

import numpy as np
import jax
import jax.numpy as jnp
from jax import lax
from jax.experimental import pallas as pl
from jax.experimental.pallas import tpu as pltpu

F32 = jnp.float32
BF16 = jnp.bfloat16
ACT = BF16

GRID_W = 64
NORM_EPS = 1e-6
MLA_HEADS = 8
QK_NOPE = 64
QK_ROPE = 32
QK_HEAD = QK_NOPE + QK_ROPE
V_HEAD = 64
Q_LORA = 384
KV_LORA = 256
ROPE_FREQS = QK_ROPE // 4
ROPE_BASE = 10000.0
FNET_GROUPS = 4
FNET_GROUP_DIM = 128
RWKV_HEAD = 64
DECAY_LORA = 64
AAA_LORA = 64
GN_EPS = 64e-5

LANE = 128
CHUNK = 64
HEAD_SLOT = 128
VMEM_LIMIT = 56 * 1024 * 1024

TOKEN_TILE = 512
ATTN_Q_BLOCK = 2048
ATTN_K_BLOCK = 768
FOUR_LANE_TILE = 2048
FOUR_ROWS_TILE = 32

NN = (((1,), (0,)), ((), ()))
NT = (((1,), (1,)), ((), ()))
TN = (((0,), (0,)), ((), ()))


def _cparams(sem):
    return pltpu.CompilerParams(dimension_semantics=sem, vmem_limit_bytes=VMEM_LIMIT)


def _mm(a, b, dn=NN):
    return lax.dot_general(a.astype(BF16), b.astype(BF16), dn, preferred_element_type=F32)


def _split(a):
    hi = a.astype(BF16)
    lo = (a - hi.astype(F32)).astype(BF16)
    return hi, lo


def _mm3(a, b, dn=NN):
    ah, al = _split(a)
    bh, bl = _split(b)
    d = lambda x, y: lax.dot_general(x, y, dn, preferred_element_type=F32)
    return d(ah, bh) + d(al, bh) + d(ah, bl)


def _mm2r(a, b_exact):
    ah, al = _split(a)
    bb = b_exact.astype(BF16)
    return jnp.dot(jnp.concatenate([ah, al], axis=1), jnp.concatenate([bb, bb], axis=0),
                   preferred_element_type=F32)


def _sigmoid(x):
    return 1.0 / (1.0 + jnp.exp(-x))


def _modnorm(x, g, sc1, sh):
    y = x * lax.rsqrt(jnp.mean(x * x, axis=-1, keepdims=True) + NORM_EPS)
    return (y * g) * sc1 + sh


def _iota2(shape, dim):
    return lax.broadcasted_iota(jnp.int32, shape, dim)


def _ada_body(c_ref, w_ref, b_ref, o_ref):
    c = c_ref[...]
    s = c * _sigmoid(c)
    o_ref[0] = _mm3(s, w_ref[0]) + b_ref[0]


def _ada(cond8, ada_w, ada_b):
    depth, d, n = ada_w.shape
    tn = 512
    return pl.pallas_call(
        _ada_body,
        grid=(depth, n // tn),
        in_specs=[
            pl.BlockSpec((8, d), lambda l, j: (0, 0)),
            pl.BlockSpec((1, d, tn), lambda l, j: (l, 0, j)),
            pl.BlockSpec((1, 1, tn), lambda l, j: (l, 0, j)),
        ],
        out_specs=pl.BlockSpec((1, 8, tn), lambda l, j: (l, 0, j)),
        out_shape=jax.ShapeDtypeStruct((depth, 8, n), F32),
        compiler_params=_cparams(("parallel", "parallel")),
        name="ada",
    )(cond8, ada_w, ada_b.reshape(depth, 1, n))


COL_CHUNK = 512


def _proj_body(x_ref, g_ref, sc_ref, sh_ref, w_ref, *o_refs, splits):
    h = _modnorm(x_ref[0], g_ref[...], sc_ref[0], sh_ref[0]).astype(BF16)
    mm = lambda c0, c1: jnp.dot(h, w_ref[:, c0:c1], preferred_element_type=F32)
    off = 0
    for o_ref, n in zip(o_refs, splits):
        if isinstance(n, tuple):
            groups, width = n
            y = mm(off, off + groups * width).astype(o_ref.dtype)
            for gi in range(groups):
                o_ref[0, gi] = y[:, gi * width:(gi + 1) * width]
            off += groups * width
            continue
        for c0 in range(0, n, COL_CHUNK):
            c1 = min(n, c0 + COL_CHUNK)
            o_ref[0, :, c0:c1] = mm(off + c0, off + c1).astype(o_ref.dtype)
        off += n


def _proj(x, g, sc1, sh, w, splits, tm):
    b, t, d = x.shape
    tm = min(tm, t)
    n = w.shape[1]
    vec = pl.BlockSpec((1, 1, d), lambda bi, i: (bi, 0, 0))
    specs, shapes = [], []
    for s in splits:
        if isinstance(s, tuple):
            specs.append(pl.BlockSpec((1, s[0], tm, s[1]), lambda bi, i: (bi, 0, i, 0)))
            shapes.append(jax.ShapeDtypeStruct((b, s[0], t, s[1]), ACT))
        else:
            specs.append(pl.BlockSpec((1, tm, s), lambda bi, i: (bi, i, 0)))
            shapes.append(jax.ShapeDtypeStruct((b, t, s), ACT))
    return pl.pallas_call(
        functools.partial(_proj_body, splits=splits),
        grid=(b, t // tm),
        in_specs=[
            pl.BlockSpec((1, tm, d), lambda bi, i: (bi, i, 0)),
            pl.BlockSpec((1, d), lambda bi, i: (0, 0)),
            vec, vec,
            pl.BlockSpec((d, n), lambda bi, i: (0, 0)),
        ],
        out_specs=specs,
        out_shape=shapes,
        compiler_params=_cparams(("parallel", "parallel")),
        name="proj",
    )(x, g, sc1, sh, w)


HALO = 16


def _proj_shift_body(x_ref, xp_ref, xn_ref, g_ref, sc_ref, sh_ref, w_ref, sw_ref, z_ref, gate_ref,
                     *, tm, n_conv, segs):
    i = pl.program_id(1)
    last = pl.num_programs(1) - 1
    g, sc1, sh = g_ref[...], sc_ref[0], sh_ref[0]
    h = _modnorm(x_ref[0], g, sc1, sh)
    hp = _modnorm(xp_ref[0], g, sc1, sh) * (i > 0).astype(F32)
    hn = _modnorm(xn_ref[0], g, sc1, sh) * (i < last).astype(F32)
    hb = jnp.concatenate([hp, h, hn], axis=0).astype(BF16)
    rows = tm + 2 * HALO
    dst = 0
    for src, width in segs:
        for c0 in range(0, width, COL_CHUNK):
            cw = min(COL_CHUNK, width - c0)
            cols = slice(src + c0, src + c0 + cw)
            u = jnp.dot(hb, w_ref[:, cols], preferred_element_type=F32)
            up = pltpu.roll(u, 1, 0)[HALO:HALO + tm]
            un = pltpu.roll(u, rows - 1, 0)[HALO:HALO + tm]
            um = u[HALO:HALO + tm]
            z_ref[0, :, dst + c0:dst + c0 + cw] = (
                sw_ref[0:1, cols] * up + sw_ref[1:2, cols] * um
                + sw_ref[2:3, cols] * un).astype(z_ref.dtype)
        dst += width
    hc = hb[HALO:HALO + tm]
    n_all = w_ref.shape[1]
    for c0 in range(n_conv, n_all, COL_CHUNK):
        c1 = min(n_all, c0 + COL_CHUNK)
        gate_ref[0, :, c0 - n_conv:c1 - n_conv] = jnp.dot(
            hc, w_ref[:, c0:c1], preferred_element_type=F32).astype(gate_ref.dtype)


def _proj_shift(x, g, sc1, sh, w, sw, n_conv, segs, tm):
    b, t, d = x.shape
    tm = min(tm, t)
    n = w.shape[1]
    hb = tm // HALO
    nhb = t // HALO
    vec = pl.BlockSpec((1, 1, d), lambda bi, i: (bi, 0, 0))
    return pl.pallas_call(
        functools.partial(_proj_shift_body, tm=tm, n_conv=n_conv, segs=segs),
        grid=(b, t // tm),
        in_specs=[
            pl.BlockSpec((1, tm, d), lambda bi, i: (bi, i, 0)),
            pl.BlockSpec((1, HALO, d), lambda bi, i: (bi, jnp.maximum(i * hb - 1, 0), 0)),
            pl.BlockSpec((1, HALO, d), lambda bi, i: (bi, jnp.minimum((i + 1) * hb, nhb - 1), 0)),
            pl.BlockSpec((1, d), lambda bi, i: (0, 0)),
            vec, vec,
            pl.BlockSpec((d, n), lambda bi, i: (0, 0)),
            pl.BlockSpec((3, n_conv), lambda bi, i: (0, 0)),
        ],
        out_specs=[pl.BlockSpec((1, tm, n_conv), lambda bi, i: (bi, i, 0)),
                   pl.BlockSpec((1, tm, n - n_conv), lambda bi, i: (bi, i, 0))],
        out_shape=[jax.ShapeDtypeStruct((b, t, n_conv), ACT),
                   jax.ShapeDtypeStruct((b, t, n - n_conv), ACT)],
        compiler_params=_cparams(("parallel", "parallel")),
        name="proj_shift",
    )(x, x, x, g, sc1, sh, w, sw)


def _rms(x, g):
    return x * lax.rsqrt(jnp.mean(x * x, axis=-1, keepdims=True) + NORM_EPS) * g


def _qkv_body(ual_ref, uac_ref, kvn_ref, qn_ref, wk_ref, wv_ref, wq_ref, wqr_ref, kg_ref, kb_ref,
              qb_ref, cosk_ref, cosq_ref, sa_ref, sb_ref, sinq_ref, q_ref, k_ref, v_ref, *, n_lat):
    ua = jnp.where(pl.program_id(1) < n_lat, ual_ref[0], uac_ref[0]).astype(F32)
    ckv = _rms(ua[:, :KV_LORA], kvn_ref[...]).astype(BF16)
    kr = ua[:, KV_LORA:KV_LORA + LANE]
    cq = _rms(ua[:, KV_LORA + LANE:], qn_ref[...]).astype(BF16)
    kn = jnp.dot(ckv, wk_ref[...], preferred_element_type=F32)
    vv = jnp.dot(ckv, wv_ref[...], preferred_element_type=F32)
    qq = jnp.dot(cq, wq_ref[...], preferred_element_type=F32)
    qr = jnp.dot(cq, wqr_ref[...], preferred_element_type=F32)
    ones_hi = (_iota2((1, HEAD_SLOT), 1) >= V_HEAD).astype(F32)
    pe = pltpu.roll(kr, QK_NOPE, 1)
    gp = pe * kg_ref[...]
    pe_rot = (pltpu.roll(gp, LANE - ROPE_FREQS, 1) * sa_ref[...]
              + pltpu.roll(gp, ROPE_FREQS, 1) * sb_ref[...])
    cosk, cosq, sinq = cosk_ref[...], cosq_ref[...], sinq_ref[...]
    inv_n = 1.0 / QK_HEAD
    scale = lambda x: lax.rsqrt(jnp.sum(x * x, axis=-1, keepdims=True) * inv_n + NORM_EPS)

    for h in range(MLA_HEADS):
        sl = slice(h * HEAD_SLOT, (h + 1) * HEAD_SLOT)
        kh = kn[:, sl] + pe
        k_ref[0, h] = (scale(kh) * (kh * cosk + pe_rot) + kb_ref[...]).astype(BF16)
        qh = qq[:, sl]
        q_ref[0, h] = (scale(qh) * (qh * cosq + qr[:, sl] * sinq) + qb_ref[...]).astype(BF16)
        v_ref[0, h] = (vv[:, sl] + ones_hi).astype(BF16)


def _qkv(ua_l, ua_c, kvn, qn, wk, wv, wq, wqr, kg, kb, qb, tabs, tm):
    b, s, wa = ua_l.shape
    tc = ua_c.shape[1]
    nl, ncx = s // tm, tc // tm
    full = lambda a: pl.BlockSpec(a.shape, lambda bi, i: (0,) * a.ndim)
    tab = pl.BlockSpec((tm, LANE), lambda bi, i: (i, 0))
    head = pl.BlockSpec((1, MLA_HEADS, tm, HEAD_SLOT), lambda bi, i: (bi, 0, i, 0))
    shape = jax.ShapeDtypeStruct((b, MLA_HEADS, s + tc, HEAD_SLOT), BF16)
    return pl.pallas_call(
        functools.partial(_qkv_body, n_lat=nl),
        grid=(b, nl + ncx),
        in_specs=[pl.BlockSpec((1, tm, wa), lambda bi, i: (bi, jnp.minimum(i, nl - 1), 0)),
                  pl.BlockSpec((1, tm, wa), lambda bi, i: (bi, jnp.maximum(i - nl, 0), 0)),
                  full(kvn), full(qn), full(wk), full(wv), full(wq), full(wqr), full(kg),
                  full(kb), full(qb)] + [tab] * len(tabs),
        out_specs=[head, head, head],
        out_shape=[shape, shape, shape],
        compiler_params=_cparams(("parallel", "parallel")),
        name="qkv",
    )(ua_l, ua_c, kvn, qn, wk, wv, wq, wqr, kg, kb, qb, *tabs)


BIAS_LANE = QK_HEAD
MAX_STATIC_BOUND = 50.0


def _attn_finish(acc_ref, o_ref):
    bq = acc_ref.shape[1]
    lane = _iota2((bq, LANE), 1)
    o0 = acc_ref[0] / pltpu.roll(acc_ref[0], V_HEAD, 1)
    o1 = acc_ref[1] / pltpu.roll(acc_ref[1], V_HEAD, 1)
    o_ref[0] = jnp.where(lane < V_HEAD, o0, pltpu.roll(o1, V_HEAD, 1)).astype(o_ref.dtype)


def _attn_static_body(q_ref, k_ref, v_ref, o_ref, acc_ref, p0_ref, p1_ref, *, bk):
    nk = k_ref.shape[2] // bk
    krows = lambda j: pl.ds(pl.multiple_of(j * bk, bk), bk)
    bufs = (p0_ref, p1_ref)

    def weights(j, slot):
        for hh in range(2):
            s = lax.dot_general(q_ref[0, hh], k_ref[0, hh, krows(j), :], NT,
                                preferred_element_type=F32)
            bufs[slot][hh] = jnp.exp2(s.astype(BF16))

    def values(j, slot):
        for hh in range(2):
            acc_ref[hh] += jnp.dot(bufs[slot][hh], v_ref[0, hh, krows(j), :],
                                   preferred_element_type=F32)

    acc_ref[...] = jnp.zeros(acc_ref.shape, F32)
    weights(0, 0)

    def two_blocks(jj, carry):
        j = 2 * jj
        weights(j + 1, 1)
        values(j, 0)
        weights(j + 2, 0)
        values(j + 1, 1)
        return carry

    pairs_done = (nk - 1) // 2
    lax.fori_loop(0, pairs_done, two_blocks, 0)
    j = 2 * pairs_done
    if (nk - 1) % 2:
        weights(j + 1, 1)
        values(j, 0)
        values(j + 1, 1)
    else:
        values(j, 0)
    _attn_finish(acc_ref, o_ref)


def _attn_online_body(q_ref, k_ref, v_ref, o_ref, acc_ref, m_ref, *, bk):
    acc_ref[...] = jnp.zeros(acc_ref.shape, F32)
    m_ref[...] = jnp.full(m_ref.shape, -jnp.inf, F32)

    def step(j, carry):
        rows = pl.ds(pl.multiple_of(j * bk, bk), bk)
        for hh in range(2):
            s = lax.dot_general(q_ref[0, hh], k_ref[0, hh, rows, :], NT,
                                preferred_element_type=F32)
            m_prev = m_ref[hh]
            m_new = jnp.maximum(m_prev, jnp.max(s, axis=-1, keepdims=True))
            p = jnp.exp2(s - m_new)
            acc_ref[hh] = (jnp.exp2(m_prev - m_new) * acc_ref[hh]
                           + jnp.dot(p.astype(BF16), v_ref[0, hh, rows, :],
                                     preferred_element_type=F32))
            m_ref[hh] = m_new
        return carry

    lax.fori_loop(0, k_ref.shape[2] // bk, step, 0)
    _attn_finish(acc_ref, o_ref)


def _attention(q, k, v, static_ok, q_start, q_rows, k_start, k_rows, bq, bk):
    b, h, _, e = q.shape
    bq, bk = min(bq, q_rows), min(bk, k_rows)
    qi0, kj0 = q_start // bq, k_start // k_rows
    kv_blk = pl.BlockSpec((1, 2, k_rows, e), lambda bi, p, i: (bi, p, kj0, 0))

    def call(online):
        scratch = [pltpu.VMEM((2, bq, LANE), F32)]
        if online:
            scratch.append(pltpu.VMEM((2, bq, 1), F32))
        else:
            scratch += [pltpu.VMEM((2, bq, bk), BF16)] * 2
        return pl.pallas_call(
            functools.partial(_attn_online_body if online else _attn_static_body, bk=bk),
            grid=(b, h // 2, q_rows // bq),
            in_specs=[pl.BlockSpec((1, 2, bq, e), lambda bi, p, i: (bi, p, qi0 + i, 0)),
                      kv_blk, kv_blk],
            out_specs=pl.BlockSpec((1, bq, 2 * V_HEAD), lambda bi, p, i: (bi, i, p)),
            out_shape=jax.ShapeDtypeStruct((b, q_rows, h * V_HEAD), ACT),
            scratch_shapes=scratch,
            compiler_params=_cparams(("parallel", "parallel", "arbitrary")),
            name="attention_online" if online else "attention",
        )(q, k, v)

    return lax.cond(static_ok, lambda: call(False), lambda: call(True))


def _dft_mats(n):
    idx = np.arange(n)
    ang = 2.0 * np.pi * ((idx[:, None] * idx[None, :]) % n) / n
    return np.cos(ang), np.sin(ang)


def _hilo(a):
    a = jnp.asarray(a, F32)
    hi = a.astype(BF16)
    return hi, (a - hi.astype(F32)).astype(BF16)


def _mm3c(ah, al, b, dn=NN):
    bh, bl = _split(b)
    d = lambda x, y: lax.dot_general(x, y, dn, preferred_element_type=F32)
    return d(ah, bh) + d(al, bh) + d(ah, bl)


def _four_rows_body(x_ref, w_ref, tc_ref, ts_ref, o_ref):
    r = tc_ref.shape[0]
    y = jnp.dot(w_ref[...], x_ref[0, 0], preferred_element_type=F32)
    yc, ys = y[:r], y[r:]
    tc, ts = tc_ref[...], ts_ref[...]
    zc = yc * tc - ys * ts
    zs = yc * ts + ys * tc
    gd = FNET_GROUP_DIM
    for j in range(zc.shape[1] // gd):
        o_ref[0, 0, 0, :, j, :] = zc[:, j * gd:(j + 1) * gd]
        o_ref[0, 0, 1, :, j, :] = zs[:, j * gd:(j + 1) * gd]


def _four_cols_body(y_ref, w_ref, cs_ref, wf_ref, o_ref, y3_scr, *, krt, scale):
    def one(j, carry):
        rows = pl.ds(pl.multiple_of(j * GRID_W, GRID_W), GRID_W)
        ycs = jnp.concatenate([y_ref[0, 0, 0, rows, :], y_ref[0, 0, 1, rows, :]], axis=0)
        y3 = jnp.dot(w_ref[...], ycs.astype(BF16),
                     preferred_element_type=F32)
        y3_scr[rows, :] = jnp.concatenate([y3[:GRID_W], y3[GRID_W:]], axis=1).astype(BF16)
        return carry

    lax.fori_loop(0, krt, one, 0, unroll=8)
    f = jnp.dot(y3_scr[...], cs_ref[...], preferred_element_type=F32) * scale
    o_ref[0, 0] = _mm(f, wf_ref[0]).astype(o_ref.dtype)


def _fourier_latent(xf, w_fnet):
    b, g, t, gd = xf.shape
    r = t // GRID_W
    wide = GRID_W * gd
    xv = xf.reshape(b, g, r, wide)
    cr, sr = _dft_mats(r)
    w_rows = jnp.asarray(np.concatenate([cr, sr], axis=0), BF16)
    kr_i, c_i = np.arange(r)[:, None], np.arange(GRID_W)[None, :]
    ang = 2.0 * np.pi * ((kr_i * c_i) % t) / t
    twc = jnp.repeat(jnp.asarray(np.cos(ang), F32), gd, axis=1)
    tws = jnp.repeat(jnp.asarray(np.sin(ang), F32), gd, axis=1)
    tl = min(FOUR_LANE_TILE, wide)
    y2 = pl.pallas_call(
        _four_rows_body,
        grid=(b, g, wide // tl),
        in_specs=[pl.BlockSpec((1, 1, r, tl), lambda bi, gi, l: (bi, gi, 0, l)),
                  pl.BlockSpec((2 * r, r), lambda bi, gi, l: (0, 0)),
                  pl.BlockSpec((r, tl), lambda bi, gi, l: (0, l)),
                  pl.BlockSpec((r, tl), lambda bi, gi, l: (0, l))],
        out_specs=pl.BlockSpec((1, 1, 2, r, tl // gd, gd), lambda bi, gi, l: (bi, gi, 0, 0, l, 0)),
        out_shape=jax.ShapeDtypeStruct((b, g, 2, r, GRID_W, gd), F32),
        compiler_params=_cparams(("parallel", "parallel", "parallel")),
        name="fourier_rows",
    )(xv, w_rows, twc, tws)
    y2v = y2.reshape(b, g, 2, r * GRID_W, gd)
    c64, s64 = _dft_mats(GRID_W)
    w_cols = jnp.asarray(np.block([[c64, -s64], [s64, c64]]), BF16)
    cc, sc = _dft_mats(gd)
    w_chan = jnp.asarray(np.concatenate([cc, -sc], axis=0), BF16)
    krt = min(FOUR_ROWS_TILE, r)
    const = lambda a: pl.BlockSpec(a.shape, lambda bi, gi, i: (0, 0))
    fo = pl.pallas_call(
        functools.partial(_four_cols_body, krt=krt, scale=1.0 / math.sqrt(t * gd)),
        grid=(b, g, r // krt),
        in_specs=[pl.BlockSpec((1, 1, 2, krt * GRID_W, gd), lambda bi, gi, i: (bi, gi, 0, i, 0)),
                  const(w_cols), const(w_chan),
                  pl.BlockSpec((1, gd, gd), lambda bi, gi, i: (gi, 0, 0))],
        out_specs=pl.BlockSpec((1, 1, krt * GRID_W, gd), lambda bi, gi, i: (bi, gi, i, 0)),
        out_shape=jax.ShapeDtypeStruct((b, g, r * GRID_W, gd), ACT),
        scratch_shapes=[pltpu.VMEM((krt * GRID_W, 2 * gd), BF16)],
        compiler_params=_cparams(("parallel", "parallel", "parallel")),
        name="fourier_cols",
    )(y2v, w_cols, w_chan, w_fnet)
    return fo.reshape(b, g, r, GRID_W, gd).transpose(0, 1, 3, 2, 4).reshape(b, g, t, gd)


def _four_dense_body(x_ref, ch_ref, cl_ref, th_ref, tl_ref, sh_ref, sl_ref, wf_ref, o_ref, *, scale):
    x = x_ref[0, 0]
    xh, xl = _split(x)
    d = lambda a, b: jnp.dot(a, b, preferred_element_type=F32)
    z = d(xh, ch_ref[...]) + d(xl, ch_ref[...]) + d(xh, cl_ref[...])
    zc, zs = z[:, :FNET_GROUP_DIM], z[:, FNET_GROUP_DIM:]
    f = (_mm3c(th_ref[...], tl_ref[...], zc) - _mm3c(sh_ref[...], sl_ref[...], zs)) * scale
    o_ref[0, 0] = _mm3(f, wf_ref[0]).astype(o_ref.dtype)


def _fourier_dense(xf, w_fnet):
    b, g, t, gd = xf.shape
    cc, sc = _dft_mats(gd)
    ch, cl = _hilo(np.concatenate([cc, sc], axis=1))
    ct, st = _dft_mats(t)
    cth, ctl = _hilo(ct)
    sth, stl = _hilo(st)
    sq = pl.BlockSpec((t, t), lambda bi, gi: (0, 0))
    cs = pl.BlockSpec((gd, 2 * gd), lambda bi, gi: (0, 0))
    return pl.pallas_call(
        functools.partial(_four_dense_body, scale=1.0 / math.sqrt(t * gd)),
        grid=(b, g),
        in_specs=[pl.BlockSpec((1, 1, t, gd), lambda bi, gi: (bi, gi, 0, 0)), cs, cs, sq, sq, sq, sq,
                  pl.BlockSpec((1, gd, gd), lambda bi, gi: (gi, 0, 0))],
        out_specs=pl.BlockSpec((1, 1, t, gd), lambda bi, gi: (bi, gi, 0, 0)),
        out_shape=jax.ShapeDtypeStruct((b, g, t, gd), ACT),
        compiler_params=_cparams(("parallel", "parallel")),
        name="fourier_dense",
    )(xf, ch, cl, cth, ctl, sth, stl, w_fnet)


def _merge_body(o_ref, f_ref, gate_ref, x_ref, gl_ref, w_ref, out_ref):
    gt = gate_ref[0].astype(F32)
    parts = [o_ref[0]] + [f_ref[0, gi] for gi in range(f_ref.shape[1])]
    mix = jnp.concatenate(parts, axis=-1).astype(F32) * (gt * _sigmoid(gt))
    y = jnp.dot(mix.astype(BF16), w_ref[...], preferred_element_type=F32)
    out_ref[0] = x_ref[0] + gl_ref[0] * y


def _merge(o, f, gate, x, gl, w, tm):
    b, t, d = x.shape
    tm = min(tm, t)
    half = o.shape[2]
    tok = lambda n: pl.BlockSpec((1, tm, n), lambda bi, i: (bi, i, 0))
    return pl.pallas_call(
        _merge_body,
        grid=(b, t // tm),
        in_specs=[tok(half),
                  pl.BlockSpec((1, f.shape[1], tm, f.shape[3]), lambda bi, i: (bi, 0, i, 0)),
                  tok(d), tok(d),
                  pl.BlockSpec((1, 1, d), lambda bi, i: (bi, 0, 0)),
                  pl.BlockSpec(w.shape, lambda bi, i: (0, 0))],
        out_specs=tok(d),
        out_shape=jax.ShapeDtypeStruct((b, t, d), F32),
        compiler_params=_cparams(("parallel", "parallel")),
        name="merge",
    )(o, f, gate, x, gl, w)


EXP_M05 = math.exp(-0.5)


PAIRS_PER_STEP = 8
PAIRS_PER_GROUP = 8
LOCAL_CHUNKS = 4
GROUP_LAG = 5


def _rwkv_local_body(*refs, cs):
    zwa_ref, w0_ref, w2_ref, a0_ref, a2_ref = refs[3:8]
    zwa = zwa_ref[0].astype(F32)
    lora = (_mm(jnp.tanh(zwa[:, :LANE]), w2_ref[...]) + w0_ref[...],
            _mm(zwa[:, LANE:], a2_ref[...]) + a0_ref[...])
    groups = [_rwkv_local_group(*refs, lora=lora, base=base, ck=ck) for ck in range(cs)
              for base in range(0, PAIRS_PER_STEP, PAIRS_PER_GROUP)]
    tick = 0
    while groups:
        live = groups[:tick // GROUP_LAG + 1]
        for g in live:
            if next(g, StopIteration) is StopIteration:
                groups.remove(g)
        tick += 1


def _rwkv_local_group(zk_ref, zv_ref, zr_ref, zwa_ref, w0_ref, w2_ref, a0_ref, a2_ref,
                      kk_ref, ka_ref, rk_ref, m_ref, g_ref, qt_ref, yl_ref, bn_ref,
                      *, lora, base, ck):
    c = CHUNK
    rows = slice(ck * CHUNK, (ck + 1) * CHUNK)

    head0 = _iota2((1, LANE), 1) < RWKV_HEAD
    r2 = _iota2((LANE, LANE), 0)
    c2 = _iota2((LANE, LANE), 1)
    same = (r2 // RWKV_HEAD) == (c2 // RWKV_HEAD)
    ones_bd = same.astype(F32)
    eye = r2 == c2

    def stack(x):
        z = jnp.zeros_like(x)
        return jnp.concatenate([jnp.where(head0, x, z), jnp.where(head0, z, x)], axis=0)

    stack_b = lambda x: stack(x.astype(BF16))
    fold = lambda x: x[:c] + x[c:]

    pairs = range(PAIRS_PER_GROUP)
    chains = [(q, d) for q in pairs for d in range(2)]
    qls = [slice((base + q) * LANE, (base + q + 1) * LANE) for q in pairs]
    ks = [zk_ref[0, rows, ql].astype(F32) for ql in qls]
    vs_ = [zv_ref[0, rows, ql].astype(F32) for ql in qls]
    rs = [zr_ref[0, rows, ql].astype(F32) for ql in qls]
    pcols = [slice((base + q) * 2 * LANE, (base + q + 1) * 2 * LANE) for q in pairs]
    wraw = [lora[0][rows, pc] for pc in pcols]
    araw = [lora[1][rows, pc] for pc in pcols]
    yield
    logw = [-EXP_M05 * _sigmoid(w) for w in wraw]
    a_all = [_sigmoid(a) for a in araw]
    kk0 = [ks[q] * kk_ref[:, qls[q]] for q in pairs]
    ss = [_mm2r(x * x, ones_bd) for x in kk0]
    yield
    kk = [kk0[q] / jnp.maximum(jnp.sqrt(ss[q]), 1e-12) for q in pairs]
    vstk = [stack_b(v) for v in vs_]

    dsl = [slice(d * LANE, (d + 1) * LANE) for d in range(2)]
    lw = [logw[q][:, dsl[d]] for q, d in chains]
    ad = [a_all[q][:, dsl[d]] for q, d in chains]
    kd = [ks[q] * (1.0 + (ad[i] - 1.0) * ka_ref[:, qls[q]]) for i, (q, d) in enumerate(chains)]
    bb = [kk[q] * ad[i] for i, (q, d) in enumerate(chains)]
    bonus = [_mm2r(rs[q] * (kd[2 * q] + kd[2 * q + 1]) * rk_ref[:, qls[q]], ones_bd) * vs_[q]
             for q in pairs]
    yield
    r3 = _iota2((c, 3 * c), 0)
    c3 = _iota2((c, 3 * c), 1) & (c - 1)
    tri3 = (c3 <= r3).astype(BF16)
    tt = _iota2((c, LANE), 0)
    ts = _iota2((c, LANE), 1) & (c - 1)
    strict = [ts < tt, ts > tt]
    incl = [ts <= tt, ts >= tt]
    eye_c = (ts == tt).astype(F32)

    def prefix(x):
        xh, xl = _split(x)
        xll = (x - xh.astype(F32) - xl.astype(F32)).astype(BF16)
        return jnp.dot(tri3, jnp.concatenate([xh, xl, xll], axis=0), preferred_element_type=F32)

    pre = [prefix(x) for x in logw]
    ltot = [pre[q][c - 1:c, dsl[d]] for q, d in chains]
    lc = [pre[q][:, dsl[0]] if d == 0 else ltot[i] - pre[q][:, dsl[1]] + lw[i]
          for i, (q, d) in enumerate(chains)]
    yield
    n = len(chains)
    bdot = lambda a, b: jnp.dot(a, b, preferred_element_type=F32)
    kkd = [(kk[q] * jnp.exp(lc[i] - lw[i])).astype(BF16) for i, (q, d) in enumerate(chains)]
    rd = [rs[q] * jnp.exp(lc[i]) for i, (q, d) in enumerate(chains)]
    e_inv = [jnp.exp(-x) for x in lc]
    inv_s = [jnp.concatenate([stack_b(bb[i] * e_inv[i]), stack_b(kd[i] * e_inv[i])], axis=0)
             for i in range(n)]
    yield
    amat = [lax.dot_general(jnp.concatenate([kkd[i], rd[i].astype(BF16)], axis=0), inv_s[i],
                            NT, preferred_element_type=F32).astype(BF16) for i in range(n)]
    zero_c = jnp.zeros((c, LANE), BF16)
    a_kb = [jnp.where(strict[d], amat[i][:c, :LANE], zero_c) for i, (q, d) in enumerate(chains)]
    a_kk = [jnp.where(strict[d], amat[i][:c, LANE:], zero_c) for i, (q, d) in enumerate(chains)]
    aq_b = [jnp.where(incl[d], amat[i][c:, :LANE], zero_c) for i, (q, d) in enumerate(chains)]
    aq_k = [jnp.where(incl[d], amat[i][c:, LANE:], zero_c) for i, (q, d) in enumerate(chains)]
    yield
    av = [bdot(jnp.concatenate([a_kk[i], aq_k[i]], axis=0), vstk[q])
          for i, (q, d) in enumerate(chains)]
    yield
    tinv = [eye_c - a.astype(F32) for a in a_kb]
    qpow = [bdot(a, stack(a)).astype(BF16) for a in a_kb]
    yield
    for _ in range(4):
        prod = [bdot(qpow[i], jnp.concatenate([stack(qpow[i]), stack_b(tinv[i])], axis=1))
                for i in range(n)]
        qpow = [x[:, :LANE].astype(BF16) for x in prod]
        tinv = [tinv[i] + prod[i][:, LANE:] for i in range(n)]
        yield
    tinv = [tinv[i] + bdot(qpow[i], stack_b(tinv[i])) for i in range(n)]
    yield
    tsplit = [_split(t) for t in tinv]
    ia_t = [bdot((eye_c + a_kb[i].astype(F32)).astype(BF16),
                 jnp.concatenate([stack(tsplit[i][0]), stack(tsplit[i][1])], axis=1))
            for i in range(n)]
    resid = [eye_c - ia_t[i][:, :LANE] - ia_t[i][:, LANE:] for i in range(n)]
    yield
    tinv = [tinv[i] + bdot(tinv[i].astype(BF16), stack_b(resid[i])) for i in range(n)]
    yield
    x = [bdot(tinv[i].astype(BF16),
              jnp.concatenate([stack(kkd[i]), stack_b(av[i][:c])], axis=1))
         for i in range(n)]
    xb = [v.astype(BF16) for v in x]
    yield
    qy = [jnp.concatenate([rd[i], av[i][c:]], axis=1)
          - bdot(aq_b[i], jnp.concatenate([stack(xb[i][:, :LANE]), stack(xb[i][:, LANE:])],
                                          axis=1)) for i in range(n)]
    yield
    e_end = [jnp.exp(ltot[i] - lc[i]) for i in range(n)]
    ends = [jnp.concatenate([(-bb[i] * e_end[i]).astype(BF16), (kd[i] * e_end[i]).astype(BF16)],
                            axis=0) for i in range(n)]
    wuv = [jnp.concatenate([xb[i], jnp.concatenate([zero_c, vs_[q].astype(BF16)], axis=1)],
                           axis=0) for i, (q, d) in enumerate(chains)]
    mg = [lax.dot_general(ends[i], wuv[i], TN, preferred_element_type=F32) for i in range(n)]
    yield
    for i, (q, d) in enumerate(chains):
        m_ref[0, ck, d, base + q] = fold(jnp.where(eye, jnp.exp(ltot[i]), 0.0)
                                         + jnp.where(same, mg[i][:, :LANE], 0.0)).astype(BF16)
        g_ref[0, ck, d, base + q] = fold(jnp.where(same, mg[i][:, LANE:], 0.0)
                                         ).astype(g_ref.dtype)
        qt_ref[0, d, rows, qls[q]] = qy[i][:, :LANE].astype(BF16)
    for q in pairs:
        yl_ref[0, rows, qls[q]] = (qy[2 * q][:, LANE:]
                                   + qy[2 * q + 1][:, LANE:]).astype(yl_ref.dtype)
        bn_ref[0, rows, qls[q]] = bonus[q].astype(bn_ref.dtype)


def _rwkv_local(z, w0p, w2p, a0p, a2p, k_k, k_a, r_k):
    b, t, _ = z.shape
    w = k_k.shape[1]
    npair = w // LANE
    pp = PAIRS_PER_STEP
    ng = npair // pp
    wl = pp * LANE
    nc = t // CHUNK
    cs = LOCAL_CHUNKS if nc % LOCAL_CHUNKS == 0 else 1
    rows = cs * CHUNK
    tokc = lambda base: pl.BlockSpec((1, rows, wl), lambda bi, ci, p: (bi, ci, base + p))
    perp3 = lambda n: pl.BlockSpec((n, pp * 2 * LANE), lambda bi, ci, p: (0, p))
    vecp = pl.BlockSpec((1, wl), lambda bi, ci, p: (0, p))
    mat = pl.BlockSpec((1, cs, 2, pp, CHUNK, LANE), lambda bi, ci, p: (bi, ci, 0, p, 0, 0))
    return pl.pallas_call(
        functools.partial(_rwkv_local_body, cs=cs),
        grid=(b, nc // cs, ng),
        in_specs=[tokc(0), tokc(ng), tokc(2 * ng),
                  pl.BlockSpec((1, rows, 2 * LANE), lambda bi, ci, p: (bi, ci, 3 * npair // 2)),
                  perp3(1), perp3(LANE), perp3(1), perp3(LANE), vecp, vecp, vecp],
        out_specs=[mat, mat,
                   pl.BlockSpec((1, 2, rows, wl), lambda bi, ci, p: (bi, 0, ci, p)),
                   pl.BlockSpec((1, rows, wl), lambda bi, ci, p: (bi, ci, p)),
                   pl.BlockSpec((1, rows, wl), lambda bi, ci, p: (bi, ci, p))],
        out_shape=[jax.ShapeDtypeStruct((b, nc, 2, npair, CHUNK, LANE), BF16),
                   jax.ShapeDtypeStruct((b, nc, 2, npair, CHUNK, LANE), ACT),
                   jax.ShapeDtypeStruct((b, 2, t, w), BF16),
                   jax.ShapeDtypeStruct((b, t, w), ACT),
                   jax.ShapeDtypeStruct((b, t, w), ACT)],
        compiler_params=_cparams(("parallel", "parallel", "parallel")),
        name="rwkv_local",
    )(z, z, z, z, w0p, w2p, a0p, a2p, k_k, k_a, r_k)


SCAN_CHUNKS = 8


def _rwkv_scan_body(m0_ref, g0_ref, q0_ref, m1_ref, g1_ref, q1_ref, h0_ref,
                    y0_ref, y1_ref, hfin_ref, h_scr, *, npair, cs):
    ci = pl.program_id(1)

    @pl.when(ci == 0)
    def _():
        h_scr[...] = h0_ref[0]

    head0 = _iota2((1, LANE), 1) < RWKV_HEAD

    def expand(x):
        z = jnp.zeros_like(x)
        return jnp.concatenate([jnp.where(head0, x, z), jnp.where(head0, z, x)], axis=0)

    refs = ((m0_ref, g0_ref, q0_ref, y0_ref), (m1_ref, g1_ref, q1_ref, y1_ref))
    chains = [(d, p) for d in range(2) for p in range(npair)]
    lanes = [slice(p * LANE, (p + 1) * LANE) for p in range(npair)]
    h = [h_scr[d, p] for d, p in chains]
    for step in range(cs):
        ck = (step, cs - 1 - step)
        rows = [slice(c * CHUNK, (c + 1) * CHUNK) for c in ck]
        hb = [x.astype(BF16) for x in h]
        res = [jnp.dot(jnp.concatenate([refs[d][2][0, 0, rows[d], lanes[p]],
                                        refs[d][0][0, ck[d], 0, p]], axis=0), hb[i],
                       preferred_element_type=F32) for i, (d, p) in enumerate(chains)]
        for i, (d, p) in enumerate(chains):
            refs[d][3][0, rows[d], lanes[p]] = res[i][:CHUNK].astype(refs[d][3].dtype)
        h = [expand(res[i][CHUNK:] + refs[d][1][0, ck[d], 0, p].astype(F32))
             for i, (d, p) in enumerate(chains)]
    for i, (d, p) in enumerate(chains):
        h_scr[d, p] = h[i]

    @pl.when(ci == pl.num_programs(1) - 1)
    def _():
        hfin_ref[0] = h_scr[...]


def _rwkv_scan(mm, gg, qt, h0):
    b, nc, _, npair, _, _ = mm.shape
    t, w = qt.shape[2], qt.shape[3]
    cs = SCAN_CHUNKS if nc % SCAN_CHUNKS == 0 else 1
    nb = nc // cs
    fwd = lambda bi, ci: (bi, ci, 0, 0, 0, 0)
    rev = lambda bi, ci: (bi, nb - 1 - ci, 1, 0, 0, 0)
    mblk = (1, cs, 1, npair, CHUNK, LANE)
    hspec = pl.BlockSpec((1, 2, npair, LANE, LANE), lambda bi, ci: (bi, 0, 0, 0, 0))
    return pl.pallas_call(
        functools.partial(_rwkv_scan_body, npair=npair, cs=cs),
        grid=(b, nb),
        in_specs=[pl.BlockSpec(mblk, fwd), pl.BlockSpec(mblk, fwd),
                  pl.BlockSpec((1, 1, cs * CHUNK, w), lambda bi, ci: (bi, 0, ci, 0)),
                  pl.BlockSpec(mblk, rev), pl.BlockSpec(mblk, rev),
                  pl.BlockSpec((1, 1, cs * CHUNK, w), lambda bi, ci: (bi, 1, nb - 1 - ci, 0)),
                  hspec],
        out_specs=[pl.BlockSpec((1, cs * CHUNK, w), lambda bi, ci: (bi, ci, 0)),
                   pl.BlockSpec((1, cs * CHUNK, w), lambda bi, ci: (bi, nb - 1 - ci, 0)),
                   hspec],
        out_shape=[jax.ShapeDtypeStruct((b, t, w), ACT), jax.ShapeDtypeStruct((b, t, w), ACT),
                   jax.ShapeDtypeStruct(h0.shape, F32)],
        scratch_shapes=[pltpu.VMEM((2, npair, LANE, LANE), F32)],
        compiler_params=_cparams(("parallel", "arbitrary")),
        name="rwkv_scan",
    )(mm, gg, qt, mm, gg, qt, h0)


def _rwkv_out_body(y0_ref, y1_ref, yl_ref, bn_ref, gate_ref, x_ref, gl_ref, gnw_ref, gnb_ref,
                   w_ref, o_ref):
    y = y0_ref[0].astype(F32) + y1_ref[0].astype(F32) + yl_ref[0].astype(F32)
    r2 = _iota2((LANE, LANE), 0)
    c2 = _iota2((LANE, LANE), 1)
    avg = ((r2 // RWKV_HEAD) == (c2 // RWKV_HEAD)).astype(F32) * (1.0 / RWKV_HEAD)
    parts = []
    for p in range(y.shape[1] // LANE):
        yp = y[:, p * LANE:(p + 1) * LANE]
        dl = yp - _mm2r(yp, avg)
        var = _mm2r(dl * dl, avg)
        parts.append(dl * lax.rsqrt(var + GN_EPS))
    yn = jnp.concatenate(parts, axis=1)
    gt = gate_ref[0].astype(F32)
    act = (yn * gnw_ref[...] + gnb_ref[...] + bn_ref[0].astype(F32)) * (gt * _sigmoid(gt))
    out = jnp.dot(act.astype(BF16), w_ref[...], preferred_element_type=F32)
    o_ref[0] = x_ref[0] + gl_ref[0] * out


def _rwkv_out(y0, y1, yl, bn, gate, x, gl, gnw, gnb, w, tm):
    b, t, d = x.shape
    tm = min(tm, t)
    wd = y0.shape[2]
    tok = lambda n: pl.BlockSpec((1, tm, n), lambda bi, i: (bi, i, 0))
    return pl.pallas_call(
        _rwkv_out_body,
        grid=(b, t // tm),
        in_specs=[tok(wd), tok(wd), tok(wd), tok(wd), tok(wd), tok(d),
                  pl.BlockSpec((1, 1, d), lambda bi, i: (bi, 0, 0)),
                  pl.BlockSpec((1, wd), lambda bi, i: (0, 0)),
                  pl.BlockSpec((1, wd), lambda bi, i: (0, 0)),
                  pl.BlockSpec(w.shape, lambda bi, i: (0, 0))],
        out_specs=tok(d),
        out_shape=jax.ShapeDtypeStruct((b, t, d), F32),
        compiler_params=_cparams(("parallel", "parallel")),
        name="rwkv_out",
    )(y0, y1, yl, bn, gate, x, gl, gnw, gnb, w)


def _rope_tables(t):
    rows = t // GRID_W
    row = jnp.repeat(jnp.arange(rows, dtype=F32), GRID_W)
    col = jnp.tile(jnp.arange(GRID_W, dtype=F32), rows)
    inv = 1.0 / (ROPE_BASE ** (jnp.arange(ROPE_FREQS, dtype=F32) / ROPE_FREQS))
    ang = jnp.stack([row[:, None] * inv, col[:, None] * inv], axis=1)
    cos, sin = jnp.cos(ang), jnp.sin(ang)
    zeros = jnp.zeros_like(sin)
    ones_lo = jnp.ones((t, QK_NOPE), F32)
    pad_hi = HEAD_SLOT - QK_HEAD
    cos_t = jnp.concatenate([ones_lo, jnp.concatenate([cos, cos], axis=2).reshape(t, QK_ROPE),
                             jnp.ones((t, pad_hi), F32)], axis=1)
    sa = jnp.concatenate([jnp.zeros((t, QK_NOPE), F32),
                          jnp.concatenate([-sin, zeros], axis=2).reshape(t, QK_ROPE),
                          jnp.zeros((t, pad_hi), F32)], axis=1)
    sb = jnp.concatenate([jnp.zeros((t, QK_NOPE), F32),
                          jnp.concatenate([zeros, sin], axis=2).reshape(t, QK_ROPE),
                          jnp.zeros((t, pad_hi), F32)], axis=1)
    return cos_t, sa, sb


def _even_layer(x, ctx, mod_l, mod_c, need_ctx, g, w_in, kv_norm, q_norm, w_uq, w_ukv,
                q_head_norm, k_head_norm, w_fnet, w_out):
    b, s, d = x.shape
    tc = ctx.shape[1]
    e_q0 = KV_LORA + QK_ROPE
    e_f0 = e_q0 + Q_LORA
    e_g0 = e_f0 + FNET_GROUPS * FNET_GROUP_DIM
    w_p = jnp.concatenate([w_in[:, e_g0:], w_in[:, e_f0:e_g0], w_in[:, :e_q0],
                           jnp.zeros((d, LANE - QK_ROPE), F32), w_in[:, e_q0:e_f0]],
                          axis=1).astype(BF16)
    splits = (d, (FNET_GROUPS, FNET_GROUP_DIM), KV_LORA + LANE + Q_LORA)
    kvw = w_ukv.reshape(KV_LORA, MLA_HEADS, QK_NOPE + V_HEAD)
    wk = jnp.pad(kvw[:, :, :QK_NOPE], ((0, 0), (0, 0), (0, HEAD_SLOT - QK_NOPE)))
    wk = wk.reshape(KV_LORA, MLA_HEADS * HEAD_SLOT).astype(BF16)
    wv = jnp.pad(kvw[:, :, QK_NOPE:], ((0, 0), (0, 0), (0, HEAD_SLOT - V_HEAD)))
    wv = wv.reshape(KV_LORA, MLA_HEADS * HEAD_SLOT).astype(BF16)
    wq3 = jnp.pad(w_uq.reshape(Q_LORA, MLA_HEADS, QK_HEAD), ((0, 0), (0, 0), (0, HEAD_SLOT - QK_HEAD)))
    wq = wq3.reshape(Q_LORA, MLA_HEADS * HEAD_SLOT).astype(BF16)
    kg = jnp.pad(k_head_norm, (0, HEAD_SLOT - QK_HEAD)).reshape(1, HEAD_SLOT)
    qg = (jnp.pad(q_head_norm, (0, HEAD_SLOT - QK_HEAD))
          * (QK_HEAD ** -0.5 * math.log2(math.e))).reshape(1, HEAD_SLOT)
    lane = np.arange(HEAD_SLOT)
    tail = (lane >= QK_NOPE) & (lane < QK_HEAD)
    first = tail & (((lane - QK_NOPE) // ROPE_FREQS) % 2 == 0)
    partner = np.where(first, lane + ROPE_FREQS, np.where(tail, lane - ROPE_FREQS, lane))
    sign = np.where(first, -1.0, np.where(tail, 1.0, 0.0)).astype(np.float32)
    wqr = (wq3[:, :, partner] * (sign * qg[0, partner])).reshape(Q_LORA, MLA_HEADS * HEAD_SLOT)
    wqr = wqr.astype(BF16)
    kvn, qn = kv_norm.reshape(1, -1), q_norm.reshape(1, -1)
    g2 = g.reshape(1, d)
    bound = (1.02 * QK_HEAD * jnp.max(jnp.abs(qg)) * jnp.max(jnp.abs(kg))).astype(BF16).astype(F32)
    static_ok = bound <= MAX_STATIC_BOUND
    bias_lane = (jnp.arange(HEAD_SLOT) == BIAS_LANE).astype(F32).reshape(1, HEAD_SLOT)
    kb = bias_lane * jnp.where(static_ok, -bound, 0.0)
    qb = bias_lane

    gate_l, four_l, ua_l = _proj(x, g2, mod_l[1], mod_l[0], w_p, splits, TOKEN_TILE)
    gate_c, four_c, ua_c = _proj(ctx, g2, mod_c[1], mod_c[0], w_p, splits, TOKEN_TILE)
    sk = s + tc
    cos_t, sa, sb = _rope_tables(s)
    cos_t = jnp.concatenate([cos_t, jnp.ones((tc, HEAD_SLOT), F32)], axis=0)
    sa = jnp.concatenate([sa, jnp.zeros((tc, HEAD_SLOT), F32)], axis=0)
    sb = jnp.concatenate([sb, jnp.zeros((tc, HEAD_SLOT), F32)], axis=0)
    tabs = (cos_t * kg, cos_t * qg, sa, sb, sb - sa)
    q_all, k_all, v_all = _qkv(ua_l, ua_c, kvn, qn, wk, wv, wq, wqr, kg, kb, qb, tabs,
                               math.gcd(s, tc))
    bk = ATTN_K_BLOCK if sk % ATTN_K_BLOCK == 0 else tc
    o_l = _attention(q_all, k_all, v_all, static_ok, 0, s, 0, sk, ATTN_Q_BLOCK, bk)
    f_l = _fourier_latent(four_l, w_fnet)
    wo = w_out.astype(BF16)
    x_new = _merge(o_l, f_l, gate_l, x, mod_l[2], wo, TOKEN_TILE)
    ctx_new = ctx
    if need_ctx:
        o_c = _attention(q_all, k_all, v_all, static_ok, s, tc, s, tc, tc, tc)
        f_c = _fourier_dense(four_c, w_fnet)
        ctx_new = _merge(o_c, f_c, gate_c, ctx, mod_c[2], wo, TOKEN_TILE)
    return x_new, ctx_new


def _odd_layer(x, ctx, mod_l, mod_c, need_ctx, g, w_in, shift_w, w0, w2, a0, a2, k_k, k_a, r_k,
               gn_w, gn_b, w_out):
    b, s, d = x.shape
    w = k_k.shape[0]
    npair = w // LANE
    o_wd0 = 2 * w
    o_r0 = o_wd0 + 2 * DECAY_LORA + 2 * AAA_LORA
    conv_ch = o_r0 + w
    segs = ((0, o_wd0), (o_r0, w), (o_wd0, o_r0 - o_wd0))
    w_p = w_in.astype(BF16)
    sw = shift_w
    g2 = g.reshape(1, d)

    def pairs(vec2):
        return vec2.reshape(2, npair, LANE).transpose(1, 0, 2).reshape(1, npair * 2 * LANE)

    def pair_mats(m):
        rr = m.shape[1]
        mp = m.reshape(2, rr, npair, LANE).transpose(2, 0, 1, 3)
        z = jnp.zeros_like(mp[:, 0])
        top = jnp.concatenate([mp[:, 0], z], axis=2)
        bot = jnp.concatenate([z, mp[:, 1]], axis=2)
        full = jnp.concatenate([top, bot], axis=1)
        return full.transpose(1, 0, 2).reshape(2 * rr, npair * 2 * LANE).astype(BF16)

    w0p, a0p, w2p, a2p = pairs(w0), pairs(a0), pair_mats(w2), pair_mats(a2)
    kk2, ka2, rk2 = k_k.reshape(1, w), k_a.reshape(1, w), r_k.reshape(1, w)
    wo = w_out.astype(BF16)

    def mix(xin, mod, h0):
        z, gate = _proj_shift(xin, g2, mod[1], mod[0], w_p, sw, conv_ch, segs, TOKEN_TILE)
        mm, gg, qt, yl, bn = _rwkv_local(z, w0p, w2p, a0p, a2p, kk2, ka2, rk2)
        y0, y1, hfin = _rwkv_scan(mm, gg, qt, h0)
        return (y0, y1, yl, bn, gate), hfin

    h_zero = jnp.zeros((b, 2, npair, LANE, LANE), F32)
    parts_c, h_ctx = mix(ctx, mod_c, h_zero)
    parts_l, _ = mix(x, mod_l, h_ctx)
    gnw, gnb = gn_w.reshape(1, w), gn_b.reshape(1, w)
    x_new = _rwkv_out(*parts_l, x, mod_l[2], gnw, gnb, wo, TOKEN_TILE)
    ctx_new = ctx
    if need_ctx:
        ctx_new = _rwkv_out(*parts_c, ctx, mod_c[2], gnw, gnb, wo, TOKEN_TILE)
    return x_new, ctx_new


def kernel(x, c, ctx, c_ctx, ada_w, ada_b, norm_g, e_w_in, e_kv_norm, e_q_norm, e_w_uq, e_w_ukv,
           e_q_head_norm, e_k_head_norm, e_w_fnet, e_w_out, o_w_in, o_shift_w, o_w0, o_w2, o_a0,
           o_a2, o_k_k, o_k_a, o_r_k, o_gn_w, o_gn_b, o_w_out):
    b, s, d = x.shape
    depth = ada_w.shape[0]
    assert b + 1 <= 8
    cond8 = jnp.concatenate([c, c_ctx[None, :], jnp.zeros((8 - b - 1, d), F32)], axis=0)
    mod = _ada(cond8, ada_w, ada_b)
    for layer in range(depth):
        need_ctx = layer < depth - 1
        m = mod[layer]
        chunk = lambda rows, i: rows[:, None, i * d:(i + 1) * d]
        lat, cx = m[:b], jnp.broadcast_to(m[b:b + 1], (b, 3 * d))
        mod_l = (chunk(lat, 0), 1.0 + chunk(lat, 1), chunk(lat, 2))
        mod_c = (chunk(cx, 0), 1.0 + chunk(cx, 1), chunk(cx, 2))
        j = layer // 2
        if layer % 2 == 0:
            x, ctx = _even_layer(x, ctx, mod_l, mod_c, need_ctx, norm_g[layer], e_w_in[j],
                                 e_kv_norm[j], e_q_norm[j], e_w_uq[j], e_w_ukv[j],
                                 e_q_head_norm[j], e_k_head_norm[j], e_w_fnet[j], e_w_out[j])
        else:
            x, ctx = _odd_layer(x, ctx, mod_l, mod_c, need_ctx, norm_g[layer], o_w_in[j],
                                o_shift_w[j], o_w0[j], o_w2[j], o_a0[j], o_a2[j], o_k_k[j],
                                o_k_a[j], o_r_k[j].reshape(-1), o_gn_w[j], o_gn_b[j], o_w_out[j])
    return x
```

```python
import functools
import math

import numpy as np
import jax
import jax.numpy as jnp
from jax import lax
from jax.experimental import pallas as pl
from jax.experimental.pallas import tpu as pltpu

F32 = jnp.float32
BF16 = jnp.bfloat16
ACT = BF16

GRID_W = 64
NORM_EPS = 1e-6
MLA_HEADS = 8
QK_NOPE = 64
QK_ROPE = 32
QK_HEAD = QK_NOPE + QK_ROPE
V_HEAD = 64
Q_LORA = 384
KV_LORA = 256
ROPE_FREQS = QK_ROPE // 4
ROPE_BASE = 10000.0
FNET_GROUPS = 4
FNET_GROUP_DIM = 128
RWKV_HEAD = 64
DECAY_LORA = 64
AAA_LORA = 64
GN_EPS = 64e-5

LANE = 128
CHUNK = 64
HEAD_SLOT = 128
VMEM_LIMIT = 56 * 1024 * 1024

TOKEN_TILE = 512
OUT_TILE = 1024
ATTN_Q_BLOCK = 2048
ATTN_K_BLOCK = 768
FOUR_LANE_TILE = 2048
FOUR_ROWS_TILE = 32

NN = (((1,), (0,)), ((), ()))
NT = (((1,), (1,)), ((), ()))
TN = (((0,), (0,)), ((), ()))


def _cparams(sem):
    return pltpu.CompilerParams(dimension_semantics=sem, vmem_limit_bytes=VMEM_LIMIT)


def _mm(a, b, dn=NN):
    return lax.dot_general(a.astype(BF16), b.astype(BF16), dn, preferred_element_type=F32)


def _split(a):
    hi = a.astype(BF16)
    lo = (a - hi.astype(F32)).astype(BF16)
    return hi, lo


def _mm3(a, b, dn=NN):
    ah, al = _split(a)
    bh, bl = _split(b)
    d = lambda x, y: lax.dot_general(x, y, dn, preferred_element_type=F32)
    return d(ah, bh) + d(al, bh) + d(ah, bl)


def _mm2r(a, b_exact):
    ah, al = _split(a)
    bb = b_exact.astype(BF16)
    return jnp.dot(jnp.concatenate([ah, al], axis=1), jnp.concatenate([bb, bb], axis=0),
                   preferred_element_type=F32)


def _sigmoid(x):
    return 1.0 / (1.0 + jnp.exp(-x))


def _modnorm(x, g, sc1, sh):
    y = x * lax.rsqrt(jnp.mean(x * x, axis=-1, keepdims=True) + NORM_EPS)
    return (y * g) * sc1 + sh


def _iota2(shape, dim):
    return lax.broadcasted_iota(jnp.int32, shape, dim)


def _ada_body(c_ref, w_ref, b_ref, o_ref):
    c = c_ref[...]
    s = c * _sigmoid(c)
    o_ref[0] = _mm3(s, w_ref[0]) + b_ref[0]


def _ada(cond8, ada_w, ada_b):
    depth, d, n = ada_w.shape
    tn = 512
    return pl.pallas_call(
        _ada_body,
        grid=(depth, n // tn),
        in_specs=[
            pl.BlockSpec((8, d), lambda l, j: (0, 0)),
            pl.BlockSpec((1, d, tn), lambda l, j: (l, 0, j)),
            pl.BlockSpec((1, 1, tn), lambda l, j: (l, 0, j)),
        ],
        out_specs=pl.BlockSpec((1, 8, tn), lambda l, j: (l, 0, j)),
        out_shape=jax.ShapeDtypeStruct((depth, 8, n), F32),
        compiler_params=_cparams(("parallel", "parallel")),
        name="ada",
    )(cond8, ada_w, ada_b.reshape(depth, 1, n))


COL_CHUNK = 512


def _proj_body(x_ref, g_ref, sc_ref, sh_ref, w_ref, *o_refs, splits):
    h = _modnorm(x_ref[0], g_ref[...], sc_ref[0], sh_ref[0]).astype(BF16)
    mm = lambda c0, c1: jnp.dot(h, w_ref[:, c0:c1], preferred_element_type=F32)
    off = 0
    for o_ref, n in zip(o_refs, splits):
        if isinstance(n, tuple):
            groups, width = n
            y = mm(off, off + groups * width).astype(o_ref.dtype)
            for gi in range(groups):
                o_ref[0, gi] = y[:, gi * width:(gi + 1) * width]
            off += groups * width
            continue
        for c0 in range(0, n, COL_CHUNK):
            c1 = min(n, c0 + COL_CHUNK)
            o_ref[0, :, c0:c1] = mm(off + c0, off + c1).astype(o_ref.dtype)
        off += n


def _proj(x, g, sc1, sh, w, splits, tm):
    b, t, d = x.shape
    tm = min(tm, t)
    n = w.shape[1]
    vec = pl.BlockSpec((1, 1, d), lambda bi, i: (bi, 0, 0))
    specs, shapes = [], []
    for s in splits:
        if isinstance(s, tuple):
            specs.append(pl.BlockSpec((1, s[0], tm, s[1]), lambda bi, i: (bi, 0, i, 0)))
            shapes.append(jax.ShapeDtypeStruct((b, s[0], t, s[1]), ACT))
        else:
            specs.append(pl.BlockSpec((1, tm, s), lambda bi, i: (bi, i, 0)))
            shapes.append(jax.ShapeDtypeStruct((b, t, s), ACT))
    return pl.pallas_call(
        functools.partial(_proj_body, splits=splits),
        grid=(b, t // tm),
        in_specs=[
            pl.BlockSpec((1, tm, d), lambda bi, i: (bi, i, 0)),
            pl.BlockSpec((1, d), lambda bi, i: (0, 0)),
            vec, vec,
            pl.BlockSpec((d, n), lambda bi, i: (0, 0)),
        ],
        out_specs=specs,
        out_shape=shapes,
        compiler_params=_cparams(("parallel", "parallel")),
        name="proj",
    )(x, g, sc1, sh, w)


HALO = 16


def _proj_shift_body(x_ref, xp_ref, xn_ref, g_ref, sc_ref, sh_ref, w_ref, sw_ref, z_ref, gate_ref,
                     *, tm, n_conv, segs):
    i = pl.program_id(1)
    last = pl.num_programs(1) - 1
    g, sc1, sh = g_ref[...], sc_ref[0], sh_ref[0]
    h = _modnorm(x_ref[0], g, sc1, sh)
    hp = _modnorm(xp_ref[0], g, sc1, sh) * (i > 0).astype(F32)
    hn = _modnorm(xn_ref[0], g, sc1, sh) * (i < last).astype(F32)
    hb = jnp.concatenate([hp, h, hn], axis=0).astype(BF16)
    rows = tm + 2 * HALO
    dst = 0
    for src, width in segs:
        for c0 in range(0, width, COL_CHUNK):
            cw = min(COL_CHUNK, width - c0)
            cols = slice(src + c0, src + c0 + cw)
            u = jnp.dot(hb, w_ref[:, cols], preferred_element_type=F32)
            up = pltpu.roll(u, 1, 0)[HALO:HALO + tm]
            un = pltpu.roll(u, rows - 1, 0)[HALO:HALO + tm]
            um = u[HALO:HALO + tm]
            z_ref[0, :, dst + c0:dst + c0 + cw] = (
                sw_ref[0:1, cols] * up + sw_ref[1:2, cols] * um
                + sw_ref[2:3, cols] * un).astype(z_ref.dtype)
        dst += width
    hc = hb[HALO:HALO + tm]
    n_all = w_ref.shape[1]
    for c0 in range(n_conv, n_all, COL_CHUNK):
        c1 = min(n_all, c0 + COL_CHUNK)
        gate_ref[0, :, c0 - n_conv:c1 - n_conv] = jnp.dot(
            hc, w_ref[:, c0:c1], preferred_element_type=F32).astype(gate_ref.dtype)


def _proj_shift(x, g, sc1, sh, w, sw, n_conv, segs, tm):
    b, t, d = x.shape
    tm = min(tm, t)
    n = w.shape[1]
    hb = tm // HALO
    nhb = t // HALO
    vec = pl.BlockSpec((1, 1, d), lambda bi, i: (bi, 0, 0))
    return pl.pallas_call(
        functools.partial(_proj_shift_body, tm=tm, n_conv=n_conv, segs=segs),
        grid=(b, t // tm),
        in_specs=[
            pl.BlockSpec((1, tm, d), lambda bi, i: (bi, i, 0)),
            pl.BlockSpec((1, HALO, d), lambda bi, i: (bi, jnp.maximum(i * hb - 1, 0), 0)),
            pl.BlockSpec((1, HALO, d), lambda bi, i: (bi, jnp.minimum((i + 1) * hb, nhb - 1), 0)),
            pl.BlockSpec((1, d), lambda bi, i: (0, 0)),
            vec, vec,
            pl.BlockSpec((d, n), lambda bi, i: (0, 0)),
            pl.BlockSpec((3, n_conv), lambda bi, i: (0, 0)),
        ],
        out_specs=[pl.BlockSpec((1, tm, n_conv), lambda bi, i: (bi, i, 0)),
                   pl.BlockSpec((1, tm, n - n_conv), lambda bi, i: (bi, i, 0))],
        out_shape=[jax.ShapeDtypeStruct((b, t, n_conv), ACT),
                   jax.ShapeDtypeStruct((b, t, n - n_conv), ACT)],
        compiler_params=_cparams(("parallel", "parallel")),
        name="proj_shift",
    )(x, x, x, g, sc1, sh, w, sw)


def _rms(x, g):
    return x * lax.rsqrt(jnp.mean(x * x, axis=-1, keepdims=True) + NORM_EPS) * g


def _qkv_body(ual_ref, uac_ref, kvn_ref, qn_ref, wk_ref, wv_ref, wq_ref, wqr_ref, kg_ref, kb_ref,
              qb_ref, cosk_ref, cosq_ref, sa_ref, sb_ref, sinq_ref, q_ref, k_ref, v_ref, *, n_lat):
    ua = jnp.where(pl.program_id(1) < n_lat, ual_ref[0], uac_ref[0]).astype(F32)
    ckv = _rms(ua[:, :KV_LORA], kvn_ref[...]).astype(BF16)
    kr = ua[:, KV_LORA:KV_LORA + LANE]
    cq = _rms(ua[:, KV_LORA + LANE:], qn_ref[...]).astype(BF16)
    kn = jnp.dot(ckv, wk_ref[...], preferred_element_type=F32)
    vv = jnp.dot(ckv, wv_ref[...], preferred_element_type=F32)
    qq = jnp.dot(cq, wq_ref[...], preferred_element_type=F32)
    qr = jnp.dot(cq, wqr_ref[...], preferred_element_type=F32)
    ones_hi = (_iota2((1, HEAD_SLOT), 1) >= V_HEAD).astype(F32)
    pe = pltpu.roll(kr, QK_NOPE, 1)
    gp = pe * kg_ref[...]
    pe_rot = (pltpu.roll(gp, LANE - ROPE_FREQS, 1) * sa_ref[...]
              + pltpu.roll(gp, ROPE_FREQS, 1) * sb_ref[...])
    cosk, cosq, sinq = cosk_ref[...], cosq_ref[...], sinq_ref[...]
    inv_n = 1.0 / QK_HEAD
    scale = lambda x: lax.rsqrt(jnp.sum(x * x, axis=-1, keepdims=True) * inv_n + NORM_EPS)

    for h in range(MLA_HEADS):
        sl = slice(h * HEAD_SLOT, (h + 1) * HEAD_SLOT)
        kh = kn[:, sl] + pe
        k_ref[0, h] = (scale(kh) * (kh * cosk + pe_rot) + kb_ref[...]).astype(BF16)
        qh = qq[:, sl]
        q_ref[0, h] = (scale(qh) * (qh * cosq + qr[:, sl] * sinq) + qb_ref[...]).astype(BF16)
        v_ref[0, h] = (vv[:, sl] + ones_hi).astype(BF16)


def _qkv(ua_l, ua_c, kvn, qn, wk, wv, wq, wqr, kg, kb, qb, tabs, tm):
    b, s, wa = ua_l.shape
    tc = ua_c.shape[1]
    nl, ncx = s // tm, tc // tm
    full = lambda a: pl.BlockSpec(a.shape, lambda bi, i: (0,) * a.ndim)
    tab = pl.BlockSpec((tm, LANE), lambda bi, i: (i, 0))
    head = pl.BlockSpec((1, MLA_HEADS, tm, HEAD_SLOT), lambda bi, i: (bi, 0, i, 0))
    shape = jax.ShapeDtypeStruct((b, MLA_HEADS, s + tc, HEAD_SLOT), BF16)
    return pl.pallas_call(
        functools.partial(_qkv_body, n_lat=nl),
        grid=(b, nl + ncx),
        in_specs=[pl.BlockSpec((1, tm, wa), lambda bi, i: (bi, jnp.minimum(i, nl - 1), 0)),
                  pl.BlockSpec((1, tm, wa), lambda bi, i: (bi, jnp.maximum(i - nl, 0), 0)),
                  full(kvn), full(qn), full(wk), full(wv), full(wq), full(wqr), full(kg),
                  full(kb), full(qb)] + [tab] * len(tabs),
        out_specs=[head, head, head],
        out_shape=[shape, shape, shape],
        compiler_params=_cparams(("parallel", "parallel")),
        name="qkv",
    )(ua_l, ua_c, kvn, qn, wk, wv, wq, wqr, kg, kb, qb, *tabs)


BIAS_LANE = QK_HEAD
MAX_STATIC_BOUND = 50.0


def _attn_finish(acc_ref, o_ref):
    bq = acc_ref.shape[1]
    lane = _iota2((bq, LANE), 1)
    o0 = acc_ref[0] / pltpu.roll(acc_ref[0], V_HEAD, 1)
    o1 = acc_ref[1] / pltpu.roll(acc_ref[1], V_HEAD, 1)
    o_ref[0] = jnp.where(lane < V_HEAD, o0, pltpu.roll(o1, V_HEAD, 1)).astype(o_ref.dtype)


def _attn_static_body(q_ref, k_ref, v_ref, o_ref, acc_ref, p0_ref, p1_ref, *, bk):
    nk = k_ref.shape[2] // bk
    krows = lambda j: pl.ds(pl.multiple_of(j * bk, bk), bk)
    bufs = (p0_ref, p1_ref)

    def weights(j, slot):
        for hh in range(2):
            s = lax.dot_general(q_ref[0, hh], k_ref[0, hh, krows(j), :], NT,
                                preferred_element_type=F32)
            bufs[slot][hh] = jnp.exp2(s.astype(BF16))

    def values(j, slot):
        for hh in range(2):
            acc_ref[hh] += jnp.dot(bufs[slot][hh], v_ref[0, hh, krows(j), :],
                                   preferred_element_type=F32)

    acc_ref[...] = jnp.zeros(acc_ref.shape, F32)
    weights(0, 0)

    def two_blocks(jj, carry):
        j = 2 * jj
        weights(j + 1, 1)
        values(j, 0)
        weights(j + 2, 0)
        values(j + 1, 1)
        return carry

    pairs_done = (nk - 1) // 2
    lax.fori_loop(0, pairs_done, two_blocks, 0)
    j = 2 * pairs_done
    if (nk - 1) % 2:
        weights(j + 1, 1)
        values(j, 0)
        values(j + 1, 1)
    else:
        values(j, 0)
    _attn_finish(acc_ref, o_ref)


def _attn_online_body(q_ref, k_ref, v_ref, o_ref, acc_ref, m_ref, *, bk):
    acc_ref[...] = jnp.zeros(acc_ref.shape, F32)
    m_ref[...] = jnp.full(m_ref.shape, -jnp.inf, F32)

    def step(j, carry):
        rows = pl.ds(pl.multiple_of(j * bk, bk), bk)
        for hh in range(2):
            s = lax.dot_general(q_ref[0, hh], k_ref[0, hh, rows, :], NT,
                                preferred_element_type=F32)
            m_prev = m_ref[hh]
            m_new = jnp.maximum(m_prev, jnp.max(s, axis=-1, keepdims=True))
            p = jnp.exp2(s - m_new)
            acc_ref[hh] = (jnp.exp2(m_prev - m_new) * acc_ref[hh]
                           + jnp.dot(p.astype(BF16), v_ref[0, hh, rows, :],
                                     preferred_element_type=F32))
            m_ref[hh] = m_new
        return carry

    lax.fori_loop(0, k_ref.shape[2] // bk, step, 0)
    _attn_finish(acc_ref, o_ref)


def _attention(q, k, v, static_ok, q_start, q_rows, k_start, k_rows, bq, bk):
    b, h, _, e = q.shape
    bq, bk = min(bq, q_rows), min(bk, k_rows)
    qi0, kj0 = q_start // bq, k_start // k_rows
    kv_blk = pl.BlockSpec((1, 2, k_rows, e), lambda bi, p, i: (bi, p, kj0, 0))

    def call(online):
        scratch = [pltpu.VMEM((2, bq, LANE), F32)]
        if online:
            scratch.append(pltpu.VMEM((2, bq, 1), F32))
        else:
            scratch += [pltpu.VMEM((2, bq, bk), BF16)] * 2
        return pl.pallas_call(
            functools.partial(_attn_online_body if online else _attn_static_body, bk=bk),
            grid=(b, h // 2, q_rows // bq),
            in_specs=[pl.BlockSpec((1, 2, bq, e), lambda bi, p, i: (bi, p, qi0 + i, 0)),
                      kv_blk, kv_blk],
            out_specs=pl.BlockSpec((1, bq, 2 * V_HEAD), lambda bi, p, i: (bi, i, p)),
            out_shape=jax.ShapeDtypeStruct((b, q_rows, h * V_HEAD), ACT),
            scratch_shapes=scratch,
            compiler_params=_cparams(("parallel", "parallel", "arbitrary")),
            name="attention_online" if online else "attention",
        )(q, k, v)

    return lax.cond(static_ok, lambda: call(False), lambda: call(True))


def _dft_mats(n):
    idx = np.arange(n)
    ang = 2.0 * np.pi * ((idx[:, None] * idx[None, :]) % n) / n
    return np.cos(ang), np.sin(ang)


def _hilo(a):
    a = jnp.asarray(a, F32)
    hi = a.astype(BF16)
    return hi, (a - hi.astype(F32)).astype(BF16)


def _mm3c(ah, al, b, dn=NN):
    bh, bl = _split(b)
    d = lambda x, y: lax.dot_general(x, y, dn, preferred_element_type=F32)
    return d(ah, bh) + d(al, bh) + d(ah, bl)


def _four_rows_body(x_ref, w_ref, tc_ref, ts_ref, o_ref):
    r = tc_ref.shape[0]
    y = jnp.dot(w_ref[...], x_ref[0, 0], preferred_element_type=F32)
    yc, ys = y[:r], y[r:]
    tc, ts = tc_ref[...], ts_ref[...]
    zc = yc * tc - ys * ts
    zs = yc * ts + ys * tc
    gd = FNET_GROUP_DIM
    for j in range(zc.shape[1] // gd):
        o_ref[0, 0, 0, :, j, :] = zc[:, j * gd:(j + 1) * gd]
        o_ref[0, 0, 1, :, j, :] = zs[:, j * gd:(j + 1) * gd]


def _four_cols_body(y_ref, w_ref, cs_ref, wf_ref, o_ref, y3_scr, *, krt, scale):
    def one(j, carry):
        rows = pl.ds(pl.multiple_of(j * GRID_W, GRID_W), GRID_W)
        ycs = jnp.concatenate([y_ref[0, 0, 0, rows, :], y_ref[0, 0, 1, rows, :]], axis=0)
        y3 = jnp.dot(w_ref[...], ycs.astype(BF16),
                     preferred_element_type=F32)
        y3_scr[rows, :] = jnp.concatenate([y3[:GRID_W], y3[GRID_W:]], axis=1).astype(BF16)
        return carry

    lax.fori_loop(0, krt, one, 0, unroll=8)
    f = jnp.dot(y3_scr[...], cs_ref[...], preferred_element_type=F32) * scale
    o_ref[0, 0] = _mm(f, wf_ref[0]).astype(o_ref.dtype)


def _fourier_latent(xf, w_fnet):
    b, g, t, gd = xf.shape
    r = t // GRID_W
    wide = GRID_W * gd
    xv = xf.reshape(b, g, r, wide)
    cr, sr = _dft_mats(r)
    w_rows = jnp.asarray(np.concatenate([cr, sr], axis=0), BF16)
    kr_i, c_i = np.arange(r)[:, None], np.arange(GRID_W)[None, :]
    ang = 2.0 * np.pi * ((kr_i * c_i) % t) / t
    twc = jnp.repeat(jnp.asarray(np.cos(ang), F32), gd, axis=1)
    tws = jnp.repeat(jnp.asarray(np.sin(ang), F32), gd, axis=1)
    tl = min(FOUR_LANE_TILE, wide)
    y2 = pl.pallas_call(
        _four_rows_body,
        grid=(b, g, wide // tl),
        in_specs=[pl.BlockSpec((1, 1, r, tl), lambda bi, gi, l: (bi, gi, 0, l)),
                  pl.BlockSpec((2 * r, r), lambda bi, gi, l: (0, 0)),
                  pl.BlockSpec((r, tl), lambda bi, gi, l: (0, l)),
                  pl.BlockSpec((r, tl), lambda bi, gi, l: (0, l))],
        out_specs=pl.BlockSpec((1, 1, 2, r, tl // gd, gd), lambda bi, gi, l: (bi, gi, 0, 0, l, 0)),
        out_shape=jax.ShapeDtypeStruct((b, g, 2, r, GRID_W, gd), F32),
        compiler_params=_cparams(("parallel", "parallel", "parallel")),
        name="fourier_rows",
    )(xv, w_rows, twc, tws)
    y2v = y2.reshape(b, g, 2, r * GRID_W, gd)
    c64, s64 = _dft_mats(GRID_W)
    w_cols = jnp.asarray(np.block([[c64, -s64], [s64, c64]]), BF16)
    cc, sc = _dft_mats(gd)
    w_chan = jnp.asarray(np.concatenate([cc, -sc], axis=0), BF16)
    krt = min(FOUR_ROWS_TILE, r)
    const = lambda a: pl.BlockSpec(a.shape, lambda bi, gi, i: (0, 0))
    fo = pl.pallas_call(
        functools.partial(_four_cols_body, krt=krt, scale=1.0 / math.sqrt(t * gd)),
        grid=(b, g, r // krt),
        in_specs=[pl.BlockSpec((1, 1, 2, krt * GRID_W, gd), lambda bi, gi, i: (bi, gi, 0, i, 0)),
                  const(w_cols), const(w_chan),
                  pl.BlockSpec((1, gd, gd), lambda bi, gi, i: (gi, 0, 0))],
        out_specs=pl.BlockSpec((1, 1, krt * GRID_W, gd), lambda bi, gi, i: (bi, gi, i, 0)),
        out_shape=jax.ShapeDtypeStruct((b, g, r * GRID_W, gd), ACT),
        scratch_shapes=[pltpu.VMEM((krt * GRID_W, 2 * gd), BF16)],
        compiler_params=_cparams(("parallel", "parallel", "parallel")),
        name="fourier_cols",
    )(y2v, w_cols, w_chan, w_fnet)
    return fo.reshape(b, g, r, GRID_W, gd).transpose(0, 1, 3, 2, 4).reshape(b, g, t, gd)


def _four_dense_body(x_ref, ch_ref, cl_ref, th_ref, tl_ref, sh_ref, sl_ref, wf_ref, o_ref, *, scale):
    x = x_ref[0, 0]
    xh, xl = _split(x)
    d = lambda a, b: jnp.dot(a, b, preferred_element_type=F32)
    z = d(xh, ch_ref[...]) + d(xl, ch_ref[...]) + d(xh, cl_ref[...])
    zc, zs = z[:, :FNET_GROUP_DIM], z[:, FNET_GROUP_DIM:]
    f = (_mm3c(th_ref[...], tl_ref[...], zc) - _mm3c(sh_ref[...], sl_ref[...], zs)) * scale
    o_ref[0, 0] = _mm3(f, wf_ref[0]).astype(o_ref.dtype)


def _fourier_dense(xf, w_fnet):
    b, g, t, gd = xf.shape
    cc, sc = _dft_mats(gd)
    ch, cl = _hilo(np.concatenate([cc, sc], axis=1))
    ct, st = _dft_mats(t)
    cth, ctl = _hilo(ct)
    sth, stl = _hilo(st)
    sq = pl.BlockSpec((t, t), lambda bi, gi: (0, 0))
    cs = pl.BlockSpec((gd, 2 * gd), lambda bi, gi: (0, 0))
    return pl.pallas_call(
        functools.partial(_four_dense_body, scale=1.0 / math.sqrt(t * gd)),
        grid=(b, g),
        in_specs=[pl.BlockSpec((1, 1, t, gd), lambda bi, gi: (bi, gi, 0, 0)), cs, cs, sq, sq, sq, sq,
                  pl.BlockSpec((1, gd, gd), lambda bi, gi: (gi, 0, 0))],
        out_specs=pl.BlockSpec((1, 1, t, gd), lambda bi, gi: (bi, gi, 0, 0)),
        out_shape=jax.ShapeDtypeStruct((b, g, t, gd), ACT),
        compiler_params=_cparams(("parallel", "parallel")),
        name="fourier_dense",
    )(xf, ch, cl, cth, ctl, sth, stl, w_fnet)


def _merge_body(o_ref, f_ref, gate_ref, x_ref, gl_ref, w_ref, out_ref):
    gt = gate_ref[0].astype(F32)
    parts = [o_ref[0]] + [f_ref[0, gi] for gi in range(f_ref.shape[1])]
    mix = jnp.concatenate(parts, axis=-1).astype(F32) * (gt * _sigmoid(gt))
    y = jnp.dot(mix.astype(BF16), w_ref[...], preferred_element_type=F32)
    out_ref[0] = x_ref[0] + gl_ref[0] * y


def _merge(o, f, gate, x, gl, w, tm):
    b, t, d = x.shape
    tm = min(tm, t)
    half = o.shape[2]
    tok = lambda n: pl.BlockSpec((1, tm, n), lambda bi, i: (bi, i, 0))
    return pl.pallas_call(
        _merge_body,
        grid=(b, t // tm),
        in_specs=[tok(half),
                  pl.BlockSpec((1, f.shape[1], tm, f.shape[3]), lambda bi, i: (bi, 0, i, 0)),
                  tok(d), tok(d),
                  pl.BlockSpec((1, 1, d), lambda bi, i: (bi, 0, 0)),
                  pl.BlockSpec(w.shape, lambda bi, i: (0, 0))],
        out_specs=tok(d),
        out_shape=jax.ShapeDtypeStruct((b, t, d), F32),
        compiler_params=_cparams(("parallel", "parallel")),
        name="merge",
    )(o, f, gate, x, gl, w)


EXP_M05 = math.exp(-0.5)


PAIRS_PER_STEP = 8
PAIRS_PER_GROUP = 8
LOCAL_CHUNKS = 4
GROUP_LAG = 5


def _rwkv_local_body(*refs, cs):
    zwa_ref, w0_ref, w2_ref, a0_ref, a2_ref = refs[3:8]
    zwa = zwa_ref[0].astype(F32)
    lora = (_mm(jnp.tanh(zwa[:, :LANE]), w2_ref[...]) + w0_ref[...],
            _mm(zwa[:, LANE:], a2_ref[...]) + a0_ref[...])
    groups = [_rwkv_local_group(*refs, lora=lora, base=base, ck=ck) for ck in range(cs)
              for base in range(0, PAIRS_PER_STEP, PAIRS_PER_GROUP)]
    tick = 0
    while groups:
        live = groups[:tick // GROUP_LAG + 1]
        for g in live:
            if next(g, StopIteration) is StopIteration:
                groups.remove(g)
        tick += 1


def _rwkv_local_group(zk_ref, zv_ref, zr_ref, zwa_ref, w0_ref, w2_ref, a0_ref, a2_ref,
                      kk_ref, ka_ref, rk_ref, m_ref, g_ref, qt_ref, yl_ref, bn_ref,
                      *, lora, base, ck):
    c = CHUNK
    rows = slice(ck * CHUNK, (ck + 1) * CHUNK)

    head0 = _iota2((1, LANE), 1) < RWKV_HEAD
    r2 = _iota2((LANE, LANE), 0)
    c2 = _iota2((LANE, LANE), 1)
    same = (r2 // RWKV_HEAD) == (c2 // RWKV_HEAD)
    ones_bd = same.astype(F32)
    eye = r2 == c2

    def stack(x):
        z = jnp.zeros_like(x)
        return jnp.concatenate([jnp.where(head0, x, z), jnp.where(head0, z, x)], axis=0)

    stack_b = lambda x: stack(x.astype(BF16))
    fold = lambda x: x[:c] + x[c:]

    pairs = range(PAIRS_PER_GROUP)
    chains = [(q, d) for q in pairs for d in range(2)]
    qls = [slice((base + q) * LANE, (base + q + 1) * LANE) for q in pairs]
    ks = [zk_ref[0, rows, ql].astype(F32) for ql in qls]
    vs_ = [zv_ref[0, rows, ql].astype(F32) for ql in qls]
    rs = [zr_ref[0, rows, ql].astype(F32) for ql in qls]
    pcols = [slice((base + q) * 2 * LANE, (base + q + 1) * 2 * LANE) for q in pairs]
    wraw = [lora[0][rows, pc] for pc in pcols]
    araw = [lora[1][rows, pc] for pc in pcols]
    yield
    logw = [-EXP_M05 * _sigmoid(w) for w in wraw]
    a_all = [_sigmoid(a) for a in araw]
    kk0 = [ks[q] * kk_ref[:, qls[q]] for q in pairs]
    ss = [_mm2r(x * x, ones_bd) for x in kk0]
    yield
    kk = [kk0[q] / jnp.maximum(jnp.sqrt(ss[q]), 1e-12) for q in pairs]
    vstk = [stack_b(v) for v in vs_]

    dsl = [slice(d * LANE, (d + 1) * LANE) for d in range(2)]
    lw = [logw[q][:, dsl[d]] for q, d in chains]
    ad = [a_all[q][:, dsl[d]] for q, d in chains]
    kd = [ks[q] * (1.0 + (ad[i] - 1.0) * ka_ref[:, qls[q]]) for i, (q, d) in enumerate(chains)]
    bb = [kk[q] * ad[i] for i, (q, d) in enumerate(chains)]
    bonus = [_mm2r(rs[q] * (kd[2 * q] + kd[2 * q + 1]) * rk_ref[:, qls[q]], ones_bd) * vs_[q]
             for q in pairs]
    yield
    r3 = _iota2((c, 3 * c), 0)
    c3 = _iota2((c, 3 * c), 1) & (c - 1)
    tri3 = (c3 <= r3).astype(BF16)
    tt = _iota2((c, LANE), 0)
    ts = _iota2((c, LANE), 1) & (c - 1)
    strict = [ts < tt, ts > tt]
    incl = [ts <= tt, ts >= tt]
    eye_c = (ts == tt).astype(F32)

    def prefix(x):
        xh, xl = _split(x)
        xll = (x - xh.astype(F32) - xl.astype(F32)).astype(BF16)
        return jnp.dot(tri3, jnp.concatenate([xh, xl, xll], axis=0), preferred_element_type=F32)

    pre = [prefix(x) for x in logw]
    ltot = [pre[q][c - 1:c, dsl[d]] for q, d in chains]
    lc = [pre[q][:, dsl[0]] if d == 0 else ltot[i] - pre[q][:, dsl[1]] + lw[i]
          for i, (q, d) in enumerate(chains)]
    yield
    n = len(chains)
    bdot = lambda a, b: jnp.dot(a, b, preferred_element_type=F32)
    kkd = [(kk[q] * jnp.exp(lc[i] - lw[i])).astype(BF16) for i, (q, d) in enumerate(chains)]
    rd = [rs[q] * jnp.exp(lc[i]) for i, (q, d) in enumerate(chains)]
    e_inv = [jnp.exp(-x) for x in lc]
    inv_s = [jnp.concatenate([stack_b(bb[i] * e_inv[i]), stack_b(kd[i] * e_inv[i])], axis=0)
             for i in range(n)]
    yield
    amat = [lax.dot_general(jnp.concatenate([kkd[i], rd[i].astype(BF16)], axis=0), inv_s[i],
                            NT, preferred_element_type=F32).astype(BF16) for i in range(n)]
    zero_c = jnp.zeros((c, LANE), BF16)
    a_kb = [jnp.where(strict[d], amat[i][:c, :LANE], zero_c) for i, (q, d) in enumerate(chains)]
    a_kk = [jnp.where(strict[d], amat[i][:c, LANE:], zero_c) for i, (q, d) in enumerate(chains)]
    aq_b = [jnp.where(incl[d], amat[i][c:, :LANE], zero_c) for i, (q, d) in enumerate(chains)]
    aq_k = [jnp.where(incl[d], amat[i][c:, LANE:], zero_c) for i, (q, d) in enumerate(chains)]
    yield
    av = [bdot(jnp.concatenate([a_kk[i], aq_k[i]], axis=0), vstk[q])
          for i, (q, d) in enumerate(chains)]
    yield
    tinv = [eye_c - a.astype(F32) for a in a_kb]
    qpow = [bdot(a, stack(a)).astype(BF16) for a in a_kb]
    yield
    for _ in range(4):
        prod = [bdot(qpow[i], jnp.concatenate([stack(qpow[i]), stack_b(tinv[i])], axis=1))
                for i in range(n)]
        qpow = [x[:, :LANE].astype(BF16) for x in prod]
        tinv = [tinv[i] + prod[i][:, LANE:] for i in range(n)]
        yield
    tinv = [tinv[i] + bdot(qpow[i], stack_b(tinv[i])) for i in range(n)]
    yield
    tsplit = [_split(t) for t in tinv]
    ia_t = [bdot((eye_c + a_kb[i].astype(F32)).astype(BF16),
                 jnp.concatenate([stack(tsplit[i][0]), stack(tsplit[i][1])], axis=1))
            for i in range(n)]
    resid = [eye_c - ia_t[i][:, :LANE] - ia_t[i][:, LANE:] for i in range(n)]
    yield
    tinv = [tinv[i] + bdot(tinv[i].astype(BF16), stack_b(resid[i])) for i in range(n)]
    yield
    x = [bdot(tinv[i].astype(BF16),
              jnp.concatenate([stack(kkd[i]), stack_b(av[i][:c])], axis=1))
         for i in range(n)]
    xb = [v.astype(BF16) for v in x]
    yield
    qy = [jnp.concatenate([rd[i], av[i][c:]], axis=1)
          - bdot(aq_b[i], jnp.concatenate([stack(xb[i][:, :LANE]), stack(xb[i][:, LANE:])],
                                          axis=1)) for i in range(n)]
    yield
    e_end = [jnp.exp(ltot[i] - lc[i]) for i in range(n)]
    ends = [jnp.concatenate([(-bb[i] * e_end[i]).astype(BF16), (kd[i] * e_end[i]).astype(BF16)],
                            axis=0) for i in range(n)]
    wuv = [jnp.concatenate([xb[i], jnp.concatenate([zero_c, vs_[q].astype(BF16)], axis=1)],
                           axis=0) for i, (q, d) in enumerate(chains)]
    mg = [lax.dot_general(ends[i], wuv[i], TN, preferred_element_type=F32) for i in range(n)]
    yield
    for i, (q, d) in enumerate(chains):
        m_ref[0, ck, d, base + q] = fold(jnp.where(eye, jnp.exp(ltot[i]), 0.0)
                                         + jnp.where(same, mg[i][:, :LANE], 0.0)).astype(BF16)
        g_ref[0, ck, d, base + q] = fold(jnp.where(same, mg[i][:, LANE:], 0.0)
                                         ).astype(g_ref.dtype)
        qt_ref[0, d, rows, qls[q]] = qy[i][:, :LANE].astype(BF16)
    for q in pairs:
        yl_ref[0, rows, qls[q]] = (qy[2 * q][:, LANE:]
                                   + qy[2 * q + 1][:, LANE:]).astype(yl_ref.dtype)
        bn_ref[0, rows, qls[q]] = bonus[q].astype(bn_ref.dtype)


def _rwkv_local(z, w0p, w2p, a0p, a2p, k_k, k_a, r_k):
    b, t, _ = z.shape
    w = k_k.shape[1]
    npair = w // LANE
    pp = PAIRS_PER_STEP
    ng = npair // pp
    wl = pp * LANE
    nc = t // CHUNK
    cs = LOCAL_CHUNKS if nc % LOCAL_CHUNKS == 0 else 1
    rows = cs * CHUNK
    tokc = lambda base: pl.BlockSpec((1, rows, wl), lambda bi, ci, p: (bi, ci, base + p))
    perp3 = lambda n: pl.BlockSpec((n, pp * 2 * LANE), lambda bi, ci, p: (0, p))
    vecp = pl.BlockSpec((1, wl), lambda bi, ci, p: (0, p))
    mat = pl.BlockSpec((1, cs, 2, pp, CHUNK, LANE), lambda bi, ci, p: (bi, ci, 0, p, 0, 0))
    return pl.pallas_call(
        functools.partial(_rwkv_local_body, cs=cs),
        grid=(b, nc // cs, ng),
        in_specs=[tokc(0), tokc(ng), tokc(2 * ng),
                  pl.BlockSpec((1, rows, 2 * LANE), lambda bi, ci, p: (bi, ci, 3 * npair // 2)),
                  perp3(1), perp3(LANE), perp3(1), perp3(LANE), vecp, vecp, vecp],
        out_specs=[mat, mat,
                   pl.BlockSpec((1, 2, rows, wl), lambda bi, ci, p: (bi, 0, ci, p)),
                   pl.BlockSpec((1, rows, wl), lambda bi, ci, p: (bi, ci, p)),
                   pl.BlockSpec((1, rows, wl), lambda bi, ci, p: (bi, ci, p))],
        out_shape=[jax.ShapeDtypeStruct((b, nc, 2, npair, CHUNK, LANE), BF16),
                   jax.ShapeDtypeStruct((b, nc, 2, npair, CHUNK, LANE), ACT),
                   jax.ShapeDtypeStruct((b, 2, t, w), BF16),
                   jax.ShapeDtypeStruct((b, t, w), ACT),
                   jax.ShapeDtypeStruct((b, t, w), ACT)],
        compiler_params=_cparams(("parallel", "parallel", "parallel")),
        name="rwkv_local",
    )(z, z, z, z, w0p, w2p, a0p, a2p, k_k, k_a, r_k)


SCAN_CHUNKS = 8


def _rwkv_scan_body(m0_ref, g0_ref, q0_ref, m1_ref, g1_ref, q1_ref, h0_ref,
                    y0_ref, y1_ref, hfin_ref, h_scr, *, npair, cs):
    ci = pl.program_id(1)

    @pl.when(ci == 0)
    def _():
        h_scr[...] = h0_ref[0]

    head0 = _iota2((1, LANE), 1) < RWKV_HEAD

    def expand(x):
        z = jnp.zeros_like(x)
        return jnp.concatenate([jnp.where(head0, x, z), jnp.where(head0, z, x)], axis=0)

    refs = ((m0_ref, g0_ref, q0_ref, y0_ref), (m1_ref, g1_ref, q1_ref, y1_ref))
    chains = [(d, p) for d in range(2) for p in range(npair)]
    lanes = [slice(p * LANE, (p + 1) * LANE) for p in range(npair)]
    h = [h_scr[d, p] for d, p in chains]
    for step in range(cs):
        ck = (step, cs - 1 - step)
        rows = [slice(c * CHUNK, (c + 1) * CHUNK) for c in ck]
        hb = [x.astype(BF16) for x in h]
        res = [jnp.dot(jnp.concatenate([refs[d][2][0, 0, rows[d], lanes[p]],
                                        refs[d][0][0, ck[d], 0, p]], axis=0), hb[i],
                       preferred_element_type=F32) for i, (d, p) in enumerate(chains)]
        for i, (d, p) in enumerate(chains):
            refs[d][3][0, rows[d], lanes[p]] = res[i][:CHUNK].astype(refs[d][3].dtype)
        h = [expand(res[i][CHUNK:] + refs[d][1][0, ck[d], 0, p].astype(F32))
             for i, (d, p) in enumerate(chains)]
    for i, (d, p) in enumerate(chains):
        h_scr[d, p] = h[i]

    @pl.when(ci == pl.num_programs(1) - 1)
    def _():
        hfin_ref[0] = h_scr[...]


def _rwkv_scan(mm, gg, qt, h0):
    b, nc, _, npair, _, _ = mm.shape
    t, w = qt.shape[2], qt.shape[3]
    cs = SCAN_CHUNKS if nc % SCAN_CHUNKS == 0 else 1
    nb = nc // cs
    fwd = lambda bi, ci: (bi, ci, 0, 0, 0, 0)
    rev = lambda bi, ci: (bi, nb - 1 - ci, 1, 0, 0, 0)
    mblk = (1, cs, 1, npair, CHUNK, LANE)
    hspec = pl.BlockSpec((1, 2, npair, LANE, LANE), lambda bi, ci: (bi, 0, 0, 0, 0))
    return pl.pallas_call(
        functools.partial(_rwkv_scan_body, npair=npair, cs=cs),
        grid=(b, nb),
        in_specs=[pl.BlockSpec(mblk, fwd), pl.BlockSpec(mblk, fwd),
                  pl.BlockSpec((1, 1, cs * CHUNK, w), lambda bi, ci: (bi, 0, ci, 0)),
                  pl.BlockSpec(mblk, rev), pl.BlockSpec(mblk, rev),
                  pl.BlockSpec((1, 1, cs * CHUNK, w), lambda bi, ci: (bi, 1, nb - 1 - ci, 0)),
                  hspec],
        out_specs=[pl.BlockSpec((1, cs * CHUNK, w), lambda bi, ci: (bi, ci, 0)),
                   pl.BlockSpec((1, cs * CHUNK, w), lambda bi, ci: (bi, nb - 1 - ci, 0)),
                   hspec],
        out_shape=[jax.ShapeDtypeStruct((b, t, w), ACT), jax.ShapeDtypeStruct((b, t, w), ACT),
                   jax.ShapeDtypeStruct(h0.shape, F32)],
        scratch_shapes=[pltpu.VMEM((2, npair, LANE, LANE), F32)],
        compiler_params=_cparams(("parallel", "arbitrary")),
        name="rwkv_scan",
    )(mm, gg, qt, mm, gg, qt, h0)


def _rwkv_out_body(y0_ref, y1_ref, yl_ref, bn_ref, gate_ref, x_ref, gl_ref, gnw_ref, gnb_ref,
                   w_ref, o_ref):
    y = y0_ref[0].astype(F32) + y1_ref[0].astype(F32) + yl_ref[0].astype(F32)
    r2 = _iota2((LANE, LANE), 0)
    c2 = _iota2((LANE, LANE), 1)
    avg = ((r2 // RWKV_HEAD) == (c2 // RWKV_HEAD)).astype(F32) * (1.0 / RWKV_HEAD)
    parts = []
    for p in range(y.shape[1] // LANE):
        yp = y[:, p * LANE:(p + 1) * LANE]
        dl = yp - _mm2r(yp, avg)
        var = _mm2r(dl * dl, avg)
        parts.append(dl * lax.rsqrt(var + GN_EPS))
    yn = jnp.concatenate(parts, axis=1)
    gt = gate_ref[0].astype(F32)
    act = (yn * gnw_ref[...] + gnb_ref[...] + bn_ref[0].astype(F32)) * (gt * _sigmoid(gt))
    out = jnp.dot(act.astype(BF16), w_ref[...], preferred_element_type=F32)
    o_ref[0] = x_ref[0] + gl_ref[0] * out


def _rwkv_out(y0, y1, yl, bn, gate, x, gl, gnw, gnb, w, tm):
    b, t, d = x.shape
    tm = min(tm, t)
    wd = y0.shape[2]
    tok = lambda n: pl.BlockSpec((1, tm, n), lambda bi, i: (bi, i, 0))
    return pl.pallas_call(
        _rwkv_out_body,
        grid=(b, t // tm),
        in_specs=[tok(wd), tok(wd), tok(wd), tok(wd), tok(wd), tok(d),
                  pl.BlockSpec((1, 1, d), lambda bi, i: (bi, 0, 0)),
                  pl.BlockSpec((1, wd), lambda bi, i: (0, 0)),
                  pl.BlockSpec((1, wd), lambda bi, i: (0, 0)),
                  pl.BlockSpec(w.shape, lambda bi, i: (0, 0))],
        out_specs=tok(d),
        out_shape=jax.ShapeDtypeStruct((b, t, d), F32),
        compiler_params=_cparams(("parallel", "parallel")),
        name="rwkv_out",
    )(y0, y1, yl, bn, gate, x, gl, gnw, gnb, w)


def _rope_tables(t):
    rows = t // GRID_W
    row = jnp.repeat(jnp.arange(rows, dtype=F32), GRID_W)
    col = jnp.tile(jnp.arange(GRID_W, dtype=F32), rows)
    inv = 1.0 / (ROPE_BASE ** (jnp.arange(ROPE_FREQS, dtype=F32) / ROPE_FREQS))
    ang = jnp.stack([row[:, None] * inv, col[:, None] * inv], axis=1)
    cos, sin = jnp.cos(ang), jnp.sin(ang)
    zeros = jnp.zeros_like(sin)
    ones_lo = jnp.ones((t, QK_NOPE), F32)
    pad_hi = HEAD_SLOT - QK_HEAD
    cos_t = jnp.concatenate([ones_lo, jnp.concatenate([cos, cos], axis=2).reshape(t, QK_ROPE),
                             jnp.ones((t, pad_hi), F32)], axis=1)
    sa = jnp.concatenate([jnp.zeros((t, QK_NOPE), F32),
                          jnp.concatenate([-sin, zeros], axis=2).reshape(t, QK_ROPE),
                          jnp.zeros((t, pad_hi), F32)], axis=1)
    sb = jnp.concatenate([jnp.zeros((t, QK_NOPE), F32),
                          jnp.concatenate([zeros, sin], axis=2).reshape(t, QK_ROPE),
                          jnp.zeros((t, pad_hi), F32)], axis=1)
    return cos_t, sa, sb


def _even_layer(x, ctx, mod_l, mod_c, need_ctx, g, w_in, kv_norm, q_norm, w_uq, w_ukv,
                q_head_norm, k_head_norm, w_fnet, w_out):
    b, s, d = x.shape
    tc = ctx.shape[1]
    e_q0 = KV_LORA + QK_ROPE
    e_f0 = e_q0 + Q_LORA
    e_g0 = e_f0 + FNET_GROUPS * FNET_GROUP_DIM
    w_p = jnp.concatenate([w_in[:, e_g0:], w_in[:, e_f0:e_g0], w_in[:, :e_q0],
                           jnp.zeros((d, LANE - QK_ROPE), F32), w_in[:, e_q0:e_f0]],
                          axis=1).astype(BF16)
    splits = (d, (FNET_GROUPS, FNET_GROUP_DIM), KV_LORA + LANE + Q_LORA)
    kvw = w_ukv.reshape(KV_LORA, MLA_HEADS, QK_NOPE + V_HEAD)
    wk = jnp.pad(kvw[:, :, :QK_NOPE], ((0, 0), (0, 0), (0, HEAD_SLOT - QK_NOPE)))
    wk = wk.reshape(KV_LORA, MLA_HEADS * HEAD_SLOT).astype(BF16)
    wv = jnp.pad(kvw[:, :, QK_NOPE:], ((0, 0), (0, 0), (0, HEAD_SLOT - V_HEAD)))
    wv = wv.reshape(KV_LORA, MLA_HEADS * HEAD_SLOT).astype(BF16)
    wq3 = jnp.pad(w_uq.reshape(Q_LORA, MLA_HEADS, QK_HEAD), ((0, 0), (0, 0), (0, HEAD_SLOT - QK_HEAD)))
    wq = wq3.reshape(Q_LORA, MLA_HEADS * HEAD_SLOT).astype(BF16)
    kg = jnp.pad(k_head_norm, (0, HEAD_SLOT - QK_HEAD)).reshape(1, HEAD_SLOT)
    qg = (jnp.pad(q_head_norm, (0, HEAD_SLOT - QK_HEAD))
          * (QK_HEAD ** -0.5 * math.log2(math.e))).reshape(1, HEAD_SLOT)
    lane = np.arange(HEAD_SLOT)
    tail = (lane >= QK_NOPE) & (lane < QK_HEAD)
    first = tail & (((lane - QK_NOPE) // ROPE_FREQS) % 2 == 0)
    partner = np.where(first, lane + ROPE_FREQS, np.where(tail, lane - ROPE_FREQS, lane))
    sign = np.where(first, -1.0, np.where(tail, 1.0, 0.0)).astype(np.float32)
    wqr = (wq3[:, :, partner] * (sign * qg[0, partner])).reshape(Q_LORA, MLA_HEADS * HEAD_SLOT)
    wqr = wqr.astype(BF16)
    kvn, qn = kv_norm.reshape(1, -1), q_norm.reshape(1, -1)
    g2 = g.reshape(1, d)
    bound = (1.02 * QK_HEAD * jnp.max(jnp.abs(qg)) * jnp.max(jnp.abs(kg))).astype(BF16).astype(F32)
    static_ok = bound <= MAX_STATIC_BOUND
    bias_lane = (jnp.arange(HEAD_SLOT) == BIAS_LANE).astype(F32).reshape(1, HEAD_SLOT)
    kb = bias_lane * jnp.where(static_ok, -bound, 0.0)
    qb = bias_lane

    gate_l, four_l, ua_l = _proj(x, g2, mod_l[1], mod_l[0], w_p, splits, TOKEN_TILE)
    gate_c, four_c, ua_c = _proj(ctx, g2, mod_c[1], mod_c[0], w_p, splits, TOKEN_TILE)
    sk = s + tc
    cos_t, sa, sb = _rope_tables(s)
    cos_t = jnp.concatenate([cos_t, jnp.ones((tc, HEAD_SLOT), F32)], axis=0)
    sa = jnp.concatenate([sa, jnp.zeros((tc, HEAD_SLOT), F32)], axis=0)
    sb = jnp.concatenate([sb, jnp.zeros((tc, HEAD_SLOT), F32)], axis=0)
    tabs = (cos_t * kg, cos_t * qg, sa, sb, sb - sa)
    q_all, k_all, v_all = _qkv(ua_l, ua_c, kvn, qn, wk, wv, wq, wqr, kg, kb, qb, tabs,
                               math.gcd(s, tc))
    bk = ATTN_K_BLOCK if sk % ATTN_K_BLOCK == 0 else tc
    o_l = _attention(q_all, k_all, v_all, static_ok, 0, s, 0, sk, ATTN_Q_BLOCK, bk)
    f_l = _fourier_latent(four_l, w_fnet)
    wo = w_out.astype(BF16)
    x_new = _merge(o_l, f_l, gate_l, x, mod_l[2], wo, OUT_TILE)
    ctx_new = ctx
    if need_ctx:
        o_c = _attention(q_all, k_all, v_all, static_ok, s, tc, s, tc, tc, tc)
        f_c = _fourier_dense(four_c, w_fnet)
        ctx_new = _merge(o_c, f_c, gate_c, ctx, mod_c[2], wo, OUT_TILE)
    return x_new, ctx_new


def _odd_layer(x, ctx, mod_l, mod_c, need_ctx, g, w_in, shift_w, w0, w2, a0, a2, k_k, k_a, r_k,
               gn_w, gn_b, w_out):
    b, s, d = x.shape
    w = k_k.shape[0]
    npair = w // LANE
    o_wd0 = 2 * w
    o_r0 = o_wd0 + 2 * DECAY_LORA + 2 * AAA_LORA
    conv_ch = o_r0 + w
    segs = ((0, o_wd0), (o_r0, w), (o_wd0, o_r0 - o_wd0))
    w_p = w_in.astype(BF16)
    sw = shift_w
    g2 = g.reshape(1, d)

    def pairs(vec2):
        return vec2.reshape(2, npair, LANE).transpose(1, 0, 2).reshape(1, npair * 2 * LANE)

    def pair_mats(m):
        rr = m.shape[1]
        mp = m.reshape(2, rr, npair, LANE).transpose(2, 0, 1, 3)
        z = jnp.zeros_like(mp[:, 0])
        top = jnp.concatenate([mp[:, 0], z], axis=2)
        bot = jnp.concatenate([z, mp[:, 1]], axis=2)
        full = jnp.concatenate([top, bot], axis=1)
        return full.transpose(1, 0, 2).reshape(2 * rr, npair * 2 * LANE).astype(BF16)

    w0p, a0p, w2p, a2p = pairs(w0), pairs(a0), pair_mats(w2), pair_mats(a2)
    kk2, ka2, rk2 = k_k.reshape(1, w), k_a.reshape(1, w), r_k.reshape(1, w)
    wo = w_out.astype(BF16)

    def mix(xin, mod, h0):
        z, gate = _proj_shift(xin, g2, mod[1], mod[0], w_p, sw, conv_ch, segs, TOKEN_TILE)
        mm, gg, qt, yl, bn = _rwkv_local(z, w0p, w2p, a0p, a2p, kk2, ka2, rk2)
        y0, y1, hfin = _rwkv_scan(mm, gg, qt, h0)
        return (y0, y1, yl, bn, gate), hfin

    h_zero = jnp.zeros((b, 2, npair, LANE, LANE), F32)
    parts_c, h_ctx = mix(ctx, mod_c, h_zero)
    parts_l, _ = mix(x, mod_l, h_ctx)
    gnw, gnb = gn_w.reshape(1, w), gn_b.reshape(1, w)
    x_new = _rwkv_out(*parts_l, x, mod_l[2], gnw, gnb, wo, OUT_TILE)
    ctx_new = ctx
    if need_ctx:
        ctx_new = _rwkv_out(*parts_c, ctx, mod_c[2], gnw, gnb, wo, OUT_TILE)
    return x_new, ctx_new


def kernel(x, c, ctx, c_ctx, ada_w, ada_b, norm_g, e_w_in, e_kv_norm, e_q_norm, e_w_uq, e_w_ukv,
           e_q_head_norm, e_k_head_norm, e_w_fnet, e_w_out, o_w_in, o_shift_w, o_w0, o_w2, o_a0,
           o_a2, o_k_k, o_k_a, o_r_k, o_gn_w, o_gn_b, o_w_out):
    b, s, d = x.shape
    depth = ada_w.shape[0]
    assert b + 1 <= 8
    cond8 = jnp.concatenate([c, c_ctx[None, :], jnp.zeros((8 - b - 1, d), F32)], axis=0)
    mod = _ada(cond8, ada_w, ada_b)
    for layer in range(depth):
        need_ctx = layer < depth - 1
        m = mod[layer]
        chunk = lambda rows, i: rows[:, None, i * d:(i + 1) * d]
        lat, cx = m[:b], jnp.broadcast_to(m[b:b + 1], (b, 3 * d))
        mod_l = (chunk(lat, 0), 1.0 + chunk(lat, 1), chunk(lat, 2))
        mod_c = (chunk(cx, 0), 1.0 + chunk(cx, 1), chunk(cx, 2))
        j = layer // 2
        if layer % 2 == 0:
            x, ctx = _even_layer(x, ctx, mod_l, mod_c, need_ctx, norm_g[layer], e_w_in[j],
                                 e_kv_norm[j], e_q_norm[j], e_w_uq[j], e_w_ukv[j],
                                 e_q_head_norm[j], e_k_head_norm[j], e_w_fnet[j], e_w_out[j])
        else:
            x, ctx = _odd_layer(x, ctx, mod_l, mod_c, need_ctx, norm_g[layer], o_w_in[j],
                                o_shift_w[j], o_w0[j], o_w2[j], o_a0[j], o_a2[j], o_k_k[j],
                                o_k_a[j], o_r_k[j].reshape(-1), o_gn_w[j], o_gn_b[j], o_w_out[j])
    return x
```

```python
import functools
import math

import numpy as np
import jax
import jax.numpy as jnp
from jax import lax
from jax.experimental import pallas as pl
from jax.experimental.pallas import tpu as pltpu

F32 = jnp.float32
BF16 = jnp.bfloat16
ACT = BF16

GRID_W = 64
NORM_EPS = 1e-6
MLA_HEADS = 8
QK_NOPE = 64
QK_ROPE = 32
QK_HEAD = QK_NOPE + QK_ROPE
V_HEAD = 64
Q_LORA = 384
KV_LORA = 256
ROPE_FREQS = QK_ROPE // 4
ROPE_BASE = 10000.0
FNET_GROUPS = 4
FNET_GROUP_DIM = 128
RWKV_HEAD = 64
DECAY_LORA = 64
AAA_LORA = 64
GN_EPS = 64e-5

LANE = 128
CHUNK = 64
HEAD_SLOT = 128
VMEM_LIMIT = 56 * 1024 * 1024

TOKEN_TILE = 512
OUT_TILE = 1024
ATTN_Q_BLOCK = 2048
ATTN_K_BLOCK = 768
FOUR_LANE_TILE = 4096
FOUR_ROWS_TILE = 32

NN = (((1,), (0,)), ((), ()))
NT = (((1,), (1,)), ((), ()))
TN = (((0,), (0,)), ((), ()))


def _cparams(sem):
    return pltpu.CompilerParams(dimension_semantics=sem, vmem_limit_bytes=VMEM_LIMIT)


def _mm(a, b, dn=NN):
    return lax.dot_general(a.astype(BF16), b.astype(BF16), dn, preferred_element_type=F32)


def _split(a):
    hi = a.astype(BF16)
    lo = (a - hi.astype(F32)).astype(BF16)
    return hi, lo


def _mm3(a, b, dn=NN):
    ah, al = _split(a)
    bh, bl = _split(b)
    d = lambda x, y: lax.dot_general(x, y, dn, preferred_element_type=F32)
    return d(ah, bh) + d(al, bh) + d(ah, bl)


def _mm2r(a, b_exact):
    ah, al = _split(a)
    bb = b_exact.astype(BF16)
    return jnp.dot(jnp.concatenate([ah, al], axis=1), jnp.concatenate([bb, bb], axis=0),
                   preferred_element_type=F32)


def _sigmoid(x):
    return 1.0 / (1.0 + jnp.exp(-x))


def _modnorm(x, g, sc1, sh):
    y = x * lax.rsqrt(jnp.mean(x * x, axis=-1, keepdims=True) + NORM_EPS)
    return (y * g) * sc1 + sh


def _iota2(shape, dim):
    return lax.broadcasted_iota(jnp.int32, shape, dim)


def _ada_body(c_ref, w_ref, b_ref, o_ref):
    c = c_ref[...]
    s = c * _sigmoid(c)
    o_ref[0] = _mm3(s, w_ref[0]) + b_ref[0]


def _ada(cond8, ada_w, ada_b):
    depth, d, n = ada_w.shape
    tn = 512
    return pl.pallas_call(
        _ada_body,
        grid=(depth, n // tn),
        in_specs=[
            pl.BlockSpec((8, d), lambda l, j: (0, 0)),
            pl.BlockSpec((1, d, tn), lambda l, j: (l, 0, j)),
            pl.BlockSpec((1, 1, tn), lambda l, j: (l, 0, j)),
        ],
        out_specs=pl.BlockSpec((1, 8, tn), lambda l, j: (l, 0, j)),
        out_shape=jax.ShapeDtypeStruct((depth, 8, n), F32),
        compiler_params=_cparams(("parallel", "parallel")),
        name="ada",
    )(cond8, ada_w, ada_b.reshape(depth, 1, n))


COL_CHUNK = 512


def _proj_body(x_ref, g_ref, sc_ref, sh_ref, w_ref, *o_refs, splits):
    h = _modnorm(x_ref[0], g_ref[...], sc_ref[0], sh_ref[0]).astype(BF16)
    mm = lambda c0, c1: jnp.dot(h, w_ref[:, c0:c1], preferred_element_type=F32)
    off = 0
    for o_ref, n in zip(o_refs, splits):
        if isinstance(n, tuple):
            groups, width = n
            y = mm(off, off + groups * width).astype(o_ref.dtype)
            for gi in range(groups):
                o_ref[0, gi] = y[:, gi * width:(gi + 1) * width]
            off += groups * width
            continue
        for c0 in range(0, n, COL_CHUNK):
            c1 = min(n, c0 + COL_CHUNK)
            o_ref[0, :, c0:c1] = mm(off + c0, off + c1).astype(o_ref.dtype)
        off += n


def _proj(x, g, sc1, sh, w, splits, tm):
    b, t, d = x.shape
    tm = min(tm, t)
    n = w.shape[1]
    vec = pl.BlockSpec((1, 1, d), lambda bi, i: (bi, 0, 0))
    specs, shapes = [], []
    for s in splits:
        if isinstance(s, tuple):
            specs.append(pl.BlockSpec((1, s[0], tm, s[1]), lambda bi, i: (bi, 0, i, 0)))
            shapes.append(jax.ShapeDtypeStruct((b, s[0], t, s[1]), ACT))
        else:
            specs.append(pl.BlockSpec((1, tm, s), lambda bi, i: (bi, i, 0)))
            shapes.append(jax.ShapeDtypeStruct((b, t, s), ACT))
    return pl.pallas_call(
        functools.partial(_proj_body, splits=splits),
        grid=(b, t // tm),
        in_specs=[
            pl.BlockSpec((1, tm, d), lambda bi, i: (bi, i, 0)),
            pl.BlockSpec((1, d), lambda bi, i: (0, 0)),
            vec, vec,
            pl.BlockSpec((d, n), lambda bi, i: (0, 0)),
        ],
        out_specs=specs,
        out_shape=shapes,
        compiler_params=_cparams(("parallel", "parallel")),
        name="proj",
    )(x, g, sc1, sh, w)


HALO = 16


def _proj_shift_body(x_ref, xp_ref, xn_ref, g_ref, sc_ref, sh_ref, w_ref, sw_ref, z_ref, gate_ref,
                     *, tm, n_conv, segs):
    i = pl.program_id(1)
    last = pl.num_programs(1) - 1
    g, sc1, sh = g_ref[...], sc_ref[0], sh_ref[0]
    h = _modnorm(x_ref[0], g, sc1, sh)
    hp = _modnorm(xp_ref[0], g, sc1, sh) * (i > 0).astype(F32)
    hn = _modnorm(xn_ref[0], g, sc1, sh) * (i < last).astype(F32)
    hb = jnp.concatenate([hp, h, hn], axis=0).astype(BF16)
    rows = tm + 2 * HALO
    dst = 0
    for src, width in segs:
        for c0 in range(0, width, COL_CHUNK):
            cw = min(COL_CHUNK, width - c0)
            cols = slice(src + c0, src + c0 + cw)
            u = jnp.dot(hb, w_ref[:, cols], preferred_element_type=F32)
            up = pltpu.roll(u, 1, 0)[HALO:HALO + tm]
            un = pltpu.roll(u, rows - 1, 0)[HALO:HALO + tm]
            um = u[HALO:HALO + tm]
            z_ref[0, :, dst + c0:dst + c0 + cw] = (
                sw_ref[0:1, cols] * up + sw_ref[1:2, cols] * um
                + sw_ref[2:3, cols] * un).astype(z_ref.dtype)
        dst += width
    hc = hb[HALO:HALO + tm]
    n_all = w_ref.shape[1]
    for c0 in range(n_conv, n_all, COL_CHUNK):
        c1 = min(n_all, c0 + COL_CHUNK)
        gate_ref[0, :, c0 - n_conv:c1 - n_conv] = jnp.dot(
            hc, w_ref[:, c0:c1], preferred_element_type=F32).astype(gate_ref.dtype)


def _proj_shift(x, g, sc1, sh, w, sw, n_conv, segs, tm):
    b, t, d = x.shape
    tm = min(tm, t)
    n = w.shape[1]
    hb = tm // HALO
    nhb = t // HALO
    vec = pl.BlockSpec((1, 1, d), lambda bi, i: (bi, 0, 0))
    return pl.pallas_call(
        functools.partial(_proj_shift_body, tm=tm, n_conv=n_conv, segs=segs),
        grid=(b, t // tm),
        in_specs=[
            pl.BlockSpec((1, tm, d), lambda bi, i: (bi, i, 0)),
            pl.BlockSpec((1, HALO, d), lambda bi, i: (bi, jnp.maximum(i * hb - 1, 0), 0)),
            pl.BlockSpec((1, HALO, d), lambda bi, i: (bi, jnp.minimum((i + 1) * hb, nhb - 1), 0)),
            pl.BlockSpec((1, d), lambda bi, i: (0, 0)),
            vec, vec,
            pl.BlockSpec((d, n), lambda bi, i: (0, 0)),
            pl.BlockSpec((3, n_conv), lambda bi, i: (0, 0)),
        ],
        out_specs=[pl.BlockSpec((1, tm, n_conv), lambda bi, i: (bi, i, 0)),
                   pl.BlockSpec((1, tm, n - n_conv), lambda bi, i: (bi, i, 0))],
        out_shape=[jax.ShapeDtypeStruct((b, t, n_conv), ACT),
                   jax.ShapeDtypeStruct((b, t, n - n_conv), ACT)],
        compiler_params=_cparams(("parallel", "parallel")),
        name="proj_shift",
    )(x, x, x, g, sc1, sh, w, sw)


def _rms(x, g):
    return x * lax.rsqrt(jnp.mean(x * x, axis=-1, keepdims=True) + NORM_EPS) * g


def _qkv_body(ual_ref, uac_ref, kvn_ref, qn_ref, wk_ref, wv_ref, wq_ref, wqr_ref, kg_ref, kb_ref,
              qb_ref, cosk_ref, cosq_ref, sa_ref, sb_ref, sinq_ref, q_ref, k_ref, v_ref, *, n_lat):
    ua = jnp.where(pl.program_id(1) < n_lat, ual_ref[0], uac_ref[0]).astype(F32)
    ckv = _rms(ua[:, :KV_LORA], kvn_ref[...]).astype(BF16)
    kr = ua[:, KV_LORA:KV_LORA + LANE]
    cq = _rms(ua[:, KV_LORA + LANE:], qn_ref[...]).astype(BF16)
    kn = jnp.dot(ckv, wk_ref[...], preferred_element_type=F32)
    vv = jnp.dot(ckv, wv_ref[...], preferred_element_type=F32)
    qq = jnp.dot(cq, wq_ref[...], preferred_element_type=F32)
    qr = jnp.dot(cq, wqr_ref[...], preferred_element_type=F32)
    ones_hi = (_iota2((1, HEAD_SLOT), 1) >= V_HEAD).astype(F32)
    pe = pltpu.roll(kr, QK_NOPE, 1)
    gp = pe * kg_ref[...]
    pe_rot = (pltpu.roll(gp, LANE - ROPE_FREQS, 1) * sa_ref[...]
              + pltpu.roll(gp, ROPE_FREQS, 1) * sb_ref[...])
    cosk, cosq, sinq = cosk_ref[...], cosq_ref[...], sinq_ref[...]
    inv_n = 1.0 / QK_HEAD
    scale = lambda x: lax.rsqrt(jnp.sum(x * x, axis=-1, keepdims=True) * inv_n + NORM_EPS)

    for h in range(MLA_HEADS):
        sl = slice(h * HEAD_SLOT, (h + 1) * HEAD_SLOT)
        kh = kn[:, sl] + pe
        k_ref[0, h] = (scale(kh) * (kh * cosk + pe_rot) + kb_ref[...]).astype(BF16)
        qh = qq[:, sl]
        q_ref[0, h] = (scale(qh) * (qh * cosq + qr[:, sl] * sinq) + qb_ref[...]).astype(BF16)
        v_ref[0, h] = (vv[:, sl] + ones_hi).astype(BF16)


def _qkv(ua_l, ua_c, kvn, qn, wk, wv, wq, wqr, kg, kb, qb, tabs, tm):
    b, s, wa = ua_l.shape
    tc = ua_c.shape[1]
    nl, ncx = s // tm, tc // tm
    full = lambda a: pl.BlockSpec(a.shape, lambda bi, i: (0,) * a.ndim)
    tab = pl.BlockSpec((tm, LANE), lambda bi, i: (i, 0))
    head = pl.BlockSpec((1, MLA_HEADS, tm, HEAD_SLOT), lambda bi, i: (bi, 0, i, 0))
    shape = jax.ShapeDtypeStruct((b, MLA_HEADS, s + tc, HEAD_SLOT), BF16)
    return pl.pallas_call(
        functools.partial(_qkv_body, n_lat=nl),
        grid=(b, nl + ncx),
        in_specs=[pl.BlockSpec((1, tm, wa), lambda bi, i: (bi, jnp.minimum(i, nl - 1), 0)),
                  pl.BlockSpec((1, tm, wa), lambda bi, i: (bi, jnp.maximum(i - nl, 0), 0)),
                  full(kvn), full(qn), full(wk), full(wv), full(wq), full(wqr), full(kg),
                  full(kb), full(qb)] + [tab] * len(tabs),
        out_specs=[head, head, head],
        out_shape=[shape, shape, shape],
        compiler_params=_cparams(("parallel", "parallel")),
        name="qkv",
    )(ua_l, ua_c, kvn, qn, wk, wv, wq, wqr, kg, kb, qb, *tabs)


BIAS_LANE = QK_HEAD
MAX_STATIC_BOUND = 50.0


def _attn_finish(acc_ref, o_ref):
    bq = acc_ref.shape[1]
    lane = _iota2((bq, LANE), 1)
    o0 = acc_ref[0] / pltpu.roll(acc_ref[0], V_HEAD, 1)
    o1 = acc_ref[1] / pltpu.roll(acc_ref[1], V_HEAD, 1)
    o_ref[0] = jnp.where(lane < V_HEAD, o0, pltpu.roll(o1, V_HEAD, 1)).astype(o_ref.dtype)


def _attn_static_body(q_ref, k_ref, v_ref, o_ref, acc_ref, p0_ref, p1_ref, *, bk):
    nk = k_ref.shape[2] // bk
    krows = lambda j: pl.ds(pl.multiple_of(j * bk, bk), bk)
    bufs = (p0_ref, p1_ref)

    def weights(j, slot):
        for hh in range(2):
            s = lax.dot_general(q_ref[0, hh], k_ref[0, hh, krows(j), :], NT,
                                preferred_element_type=F32)
            bufs[slot][hh] = jnp.exp2(s.astype(BF16))

    def values(j, slot):
        for hh in range(2):
            acc_ref[hh] += jnp.dot(bufs[slot][hh], v_ref[0, hh, krows(j), :],
                                   preferred_element_type=F32)

    acc_ref[...] = jnp.zeros(acc_ref.shape, F32)
    weights(0, 0)

    def two_blocks(jj, carry):
        j = 2 * jj
        weights(j + 1, 1)
        values(j, 0)
        weights(j + 2, 0)
        values(j + 1, 1)
        return carry

    pairs_done = (nk - 1) // 2
    lax.fori_loop(0, pairs_done, two_blocks, 0)
    j = 2 * pairs_done
    if (nk - 1) % 2:
        weights(j + 1, 1)
        values(j, 0)
        values(j + 1, 1)
    else:
        values(j, 0)
    _attn_finish(acc_ref, o_ref)


def _attn_online_body(q_ref, k_ref, v_ref, o_ref, acc_ref, m_ref, *, bk):
    acc_ref[...] = jnp.zeros(acc_ref.shape, F32)
    m_ref[...] = jnp.full(m_ref.shape, -jnp.inf, F32)

    def step(j, carry):
        rows = pl.ds(pl.multiple_of(j * bk, bk), bk)
        for hh in range(2):
            s = lax.dot_general(q_ref[0, hh], k_ref[0, hh, rows, :], NT,
                                preferred_element_type=F32)
            m_prev = m_ref[hh]
            m_new = jnp.maximum(m_prev, jnp.max(s, axis=-1, keepdims=True))
            p = jnp.exp2(s - m_new)
            acc_ref[hh] = (jnp.exp2(m_prev - m_new) * acc_ref[hh]
                           + jnp.dot(p.astype(BF16), v_ref[0, hh, rows, :],
                                     preferred_element_type=F32))
            m_ref[hh] = m_new
        return carry

    lax.fori_loop(0, k_ref.shape[2] // bk, step, 0)
    _attn_finish(acc_ref, o_ref)


def _attention(q, k, v, static_ok, q_start, q_rows, k_start, k_rows, bq, bk):
    b, h, _, e = q.shape
    bq, bk = min(bq, q_rows), min(bk, k_rows)
    qi0, kj0 = q_start // bq, k_start // k_rows
    kv_blk = pl.BlockSpec((1, 2, k_rows, e), lambda bi, p, i: (bi, p, kj0, 0))

    def call(online):
        scratch = [pltpu.VMEM((2, bq, LANE), F32)]
        if online:
            scratch.append(pltpu.VMEM((2, bq, 1), F32))
        else:
            scratch += [pltpu.VMEM((2, bq, bk), BF16)] * 2
        return pl.pallas_call(
            functools.partial(_attn_online_body if online else _attn_static_body, bk=bk),
            grid=(b, h // 2, q_rows // bq),
            in_specs=[pl.BlockSpec((1, 2, bq, e), lambda bi, p, i: (bi, p, qi0 + i, 0)),
                      kv_blk, kv_blk],
            out_specs=pl.BlockSpec((1, bq, 2 * V_HEAD), lambda bi, p, i: (bi, i, p)),
            out_shape=jax.ShapeDtypeStruct((b, q_rows, h * V_HEAD), ACT),
            scratch_shapes=scratch,
            compiler_params=_cparams(("parallel", "parallel", "arbitrary")),
            name="attention_online" if online else "attention",
        )(q, k, v)

    return lax.cond(static_ok, lambda: call(False), lambda: call(True))


def _dft_mats(n):
    idx = np.arange(n)
    ang = 2.0 * np.pi * ((idx[:, None] * idx[None, :]) % n) / n
    return np.cos(ang), np.sin(ang)


def _hilo(a):
    a = jnp.asarray(a, F32)
    hi = a.astype(BF16)
    return hi, (a - hi.astype(F32)).astype(BF16)


def _mm3c(ah, al, b, dn=NN):
    bh, bl = _split(b)
    d = lambda x, y: lax.dot_general(x, y, dn, preferred_element_type=F32)
    return d(ah, bh) + d(al, bh) + d(ah, bl)


def _four_rows_body(x_ref, w_ref, tc_ref, ts_ref, o_ref):
    r = tc_ref.shape[0]
    y = jnp.dot(w_ref[...], x_ref[0, 0], preferred_element_type=F32)
    yc, ys = y[:r], y[r:]
    tc, ts = tc_ref[...], ts_ref[...]
    zc = yc * tc - ys * ts
    zs = yc * ts + ys * tc
    gd = FNET_GROUP_DIM
    for j in range(zc.shape[1] // gd):
        o_ref[0, 0, 0, :, j, :] = zc[:, j * gd:(j + 1) * gd]
        o_ref[0, 0, 1, :, j, :] = zs[:, j * gd:(j + 1) * gd]


def _four_cols_body(y_ref, w_ref, cs_ref, wf_ref, o_ref, y3_scr, *, krt, scale):
    def one(j, carry):
        rows = pl.ds(pl.multiple_of(j * GRID_W, GRID_W), GRID_W)
        ycs = jnp.concatenate([y_ref[0, 0, 0, rows, :], y_ref[0, 0, 1, rows, :]], axis=0)
        y3 = jnp.dot(w_ref[...], ycs.astype(BF16),
                     preferred_element_type=F32)
        y3_scr[rows, :] = jnp.concatenate([y3[:GRID_W], y3[GRID_W:]], axis=1).astype(BF16)
        return carry

    lax.fori_loop(0, krt, one, 0, unroll=8)
    f = jnp.dot(y3_scr[...], cs_ref[...], preferred_element_type=F32) * scale
    o_ref[0, 0] = _mm(f, wf_ref[0]).astype(o_ref.dtype)


def _fourier_latent(xf, w_fnet):
    b, g, t, gd = xf.shape
    r = t // GRID_W
    wide = GRID_W * gd
    xv = xf.reshape(b, g, r, wide)
    cr, sr = _dft_mats(r)
    w_rows = jnp.asarray(np.concatenate([cr, sr], axis=0), BF16)
    kr_i, c_i = np.arange(r)[:, None], np.arange(GRID_W)[None, :]
    ang = 2.0 * np.pi * ((kr_i * c_i) % t) / t
    twc = jnp.repeat(jnp.asarray(np.cos(ang), F32), gd, axis=1)
    tws = jnp.repeat(jnp.asarray(np.sin(ang), F32), gd, axis=1)
    tl = min(FOUR_LANE_TILE, wide)
    y2 = pl.pallas_call(
        _four_rows_body,
        grid=(b, g, wide // tl),
        in_specs=[pl.BlockSpec((1, 1, r, tl), lambda bi, gi, l: (bi, gi, 0, l)),
                  pl.BlockSpec((2 * r, r), lambda bi, gi, l: (0, 0)),
                  pl.BlockSpec((r, tl), lambda bi, gi, l: (0, l)),
                  pl.BlockSpec((r, tl), lambda bi, gi, l: (0, l))],
        out_specs=pl.BlockSpec((1, 1, 2, r, tl // gd, gd), lambda bi, gi, l: (bi, gi, 0, 0, l, 0)),
        out_shape=jax.ShapeDtypeStruct((b, g, 2, r, GRID_W, gd), F32),
        compiler_params=_cparams(("parallel", "parallel", "parallel")),
        name="fourier_rows",
    )(xv, w_rows, twc, tws)
    y2v = y2.reshape(b, g, 2, r * GRID_W, gd)
    c64, s64 = _dft_mats(GRID_W)
    w_cols = jnp.asarray(np.block([[c64, -s64], [s64, c64]]), BF16)
    cc, sc = _dft_mats(gd)
    w_chan = jnp.asarray(np.concatenate([cc, -sc], axis=0), BF16)
    krt = min(FOUR_ROWS_TILE, r)
    const = lambda a: pl.BlockSpec(a.shape, lambda bi, gi, i: (0, 0))
    fo = pl.pallas_call(
        functools.partial(_four_cols_body, krt=krt, scale=1.0 / math.sqrt(t * gd)),
        grid=(b, g, r // krt),
        in_specs=[pl.BlockSpec((1, 1, 2, krt * GRID_W, gd), lambda bi, gi, i: (bi, gi, 0, i, 0)),
                  const(w_cols), const(w_chan),
                  pl.BlockSpec((1, gd, gd), lambda bi, gi, i: (gi, 0, 0))],
        out_specs=pl.BlockSpec((1, 1, krt * GRID_W, gd), lambda bi, gi, i: (bi, gi, i, 0)),
        out_shape=jax.ShapeDtypeStruct((b, g, r * GRID_W, gd), ACT),
        scratch_shapes=[pltpu.VMEM((krt * GRID_W, 2 * gd), BF16)],
        compiler_params=_cparams(("parallel", "parallel", "parallel")),
        name="fourier_cols",
    )(y2v, w_cols, w_chan, w_fnet)
    return fo.reshape(b, g, r, GRID_W, gd).transpose(0, 1, 3, 2, 4).reshape(b, g, t, gd)


def _four_dense_body(x_ref, ch_ref, cl_ref, th_ref, tl_ref, sh_ref, sl_ref, wf_ref, o_ref, *, scale):
    x = x_ref[0, 0]
    xh, xl = _split(x)
    d = lambda a, b: jnp.dot(a, b, preferred_element_type=F32)
    z = d(xh, ch_ref[...]) + d(xl, ch_ref[...]) + d(xh, cl_ref[...])
    zc, zs = z[:, :FNET_GROUP_DIM], z[:, FNET_GROUP_DIM:]
    f = (_mm3c(th_ref[...], tl_ref[...], zc) - _mm3c(sh_ref[...], sl_ref[...], zs)) * scale
    o_ref[0, 0] = _mm3(f, wf_ref[0]).astype(o_ref.dtype)


def _fourier_dense(xf, w_fnet):
    b, g, t, gd = xf.shape
    cc, sc = _dft_mats(gd)
    ch, cl = _hilo(np.concatenate([cc, sc], axis=1))
    ct, st = _dft_mats(t)
    cth, ctl = _hilo(ct)
    sth, stl = _hilo(st)
    sq = pl.BlockSpec((t, t), lambda bi, gi: (0, 0))
    cs = pl.BlockSpec((gd, 2 * gd), lambda bi, gi: (0, 0))
    return pl.pallas_call(
        functools.partial(_four_dense_body, scale=1.0 / math.sqrt(t * gd)),
        grid=(b, g),
        in_specs=[pl.BlockSpec((1, 1, t, gd), lambda bi, gi: (bi, gi, 0, 0)), cs, cs, sq, sq, sq, sq,
                  pl.BlockSpec((1, gd, gd), lambda bi, gi: (gi, 0, 0))],
        out_specs=pl.BlockSpec((1, 1, t, gd), lambda bi, gi: (bi, gi, 0, 0)),
        out_shape=jax.ShapeDtypeStruct((b, g, t, gd), ACT),
        compiler_params=_cparams(("parallel", "parallel")),
        name="fourier_dense",
    )(xf, ch, cl, cth, ctl, sth, stl, w_fnet)


def _merge_body(o_ref, f_ref, gate_ref, x_ref, gl_ref, w_ref, out_ref):
    gt = gate_ref[0].astype(F32)
    parts = [o_ref[0]] + [f_ref[0, gi] for gi in range(f_ref.shape[1])]
    mix = jnp.concatenate(parts, axis=-1).astype(F32) * (gt * _sigmoid(gt))
    y = jnp.dot(mix.astype(BF16), w_ref[...], preferred_element_type=F32)
    out_ref[0] = x_ref[0] + gl_ref[0] * y


def _merge(o, f, gate, x, gl, w, tm):
    b, t, d = x.shape
    tm = min(tm, t)
    half = o.shape[2]
    tok = lambda n: pl.BlockSpec((1, tm, n), lambda bi, i: (bi, i, 0))
    return pl.pallas_call(
        _merge_body,
        grid=(b, t // tm),
        in_specs=[tok(half),
                  pl.BlockSpec((1, f.shape[1], tm, f.shape[3]), lambda bi, i: (bi, 0, i, 0)),
                  tok(d), tok(d),
                  pl.BlockSpec((1, 1, d), lambda bi, i: (bi, 0, 0)),
                  pl.BlockSpec(w.shape, lambda bi, i: (0, 0))],
        out_specs=tok(d),
        out_shape=jax.ShapeDtypeStruct((b, t, d), F32),
        compiler_params=_cparams(("parallel", "parallel")),
        name="merge",
    )(o, f, gate, x, gl, w)


EXP_M05 = math.exp(-0.5)


PAIRS_PER_STEP = 8
PAIRS_PER_GROUP = 8
LOCAL_CHUNKS = 4
GROUP_LAG = 5


def _rwkv_local_body(*refs, cs):
    zwa_ref, w0_ref, w2_ref, a0_ref, a2_ref = refs[3:8]
    zwa = zwa_ref[0].astype(F32)
    lora = (_mm(jnp.tanh(zwa[:, :LANE]), w2_ref[...]) + w0_ref[...],
            _mm(zwa[:, LANE:], a2_ref[...]) + a0_ref[...])
    groups = [_rwkv_local_group(*refs, lora=lora, base=base, ck=ck) for ck in range(cs)
              for base in range(0, PAIRS_PER_STEP, PAIRS_PER_GROUP)]
    tick = 0
    while groups:
        live = groups[:tick // GROUP_LAG + 1]
        for g in live:
            if next(g, StopIteration) is StopIteration:
                groups.remove(g)
        tick += 1


def _rwkv_local_group(zk_ref, zv_ref, zr_ref, zwa_ref, w0_ref, w2_ref, a0_ref, a2_ref,
                      kk_ref, ka_ref, rk_ref, m_ref, g_ref, qt_ref, yl_ref, bn_ref,
                      *, lora, base, ck):
    c = CHUNK
    rows = slice(ck * CHUNK, (ck + 1) * CHUNK)

    head0 = _iota2((1, LANE), 1) < RWKV_HEAD
    r2 = _iota2((LANE, LANE), 0)
    c2 = _iota2((LANE, LANE), 1)
    same = (r2 // RWKV_HEAD) == (c2 // RWKV_HEAD)
    ones_bd = same.astype(F32)
    eye = r2 == c2

    def stack(x):
        z = jnp.zeros_like(x)
        return jnp.concatenate([jnp.where(head0, x, z), jnp.where(head0, z, x)], axis=0)

    stack_b = lambda x: stack(x.astype(BF16))
    fold = lambda x: x[:c] + x[c:]

    pairs = range(PAIRS_PER_GROUP)
    chains = [(q, d) for q in pairs for d in range(2)]
    qls = [slice((base + q) * LANE, (base + q + 1) * LANE) for q in pairs]
    ks = [zk_ref[0, rows, ql].astype(F32) for ql in qls]
    vs_ = [zv_ref[0, rows, ql].astype(F32) for ql in qls]
    rs = [zr_ref[0, rows, ql].astype(F32) for ql in qls]
    pcols = [slice((base + q) * 2 * LANE, (base + q + 1) * 2 * LANE) for q in pairs]
    wraw = [lora[0][rows, pc] for pc in pcols]
    araw = [lora[1][rows, pc] for pc in pcols]
    yield
    logw = [-EXP_M05 * _sigmoid(w) for w in wraw]
    a_all = [_sigmoid(a) for a in araw]
    kk0 = [ks[q] * kk_ref[:, qls[q]] for q in pairs]
    ss = [_mm2r(x * x, ones_bd) for x in kk0]
    yield
    kk = [kk0[q] / jnp.maximum(jnp.sqrt(ss[q]), 1e-12) for q in pairs]
    vstk = [stack_b(v) for v in vs_]

    dsl = [slice(d * LANE, (d + 1) * LANE) for d in range(2)]
    lw = [logw[q][:, dsl[d]] for q, d in chains]
    ad = [a_all[q][:, dsl[d]] for q, d in chains]
    kd = [ks[q] * (1.0 + (ad[i] - 1.0) * ka_ref[:, qls[q]]) for i, (q, d) in enumerate(chains)]
    bb = [kk[q] * ad[i] for i, (q, d) in enumerate(chains)]
    bonus = [_mm2r(rs[q] * (kd[2 * q] + kd[2 * q + 1]) * rk_ref[:, qls[q]], ones_bd) * vs_[q]
             for q in pairs]
    yield
    r3 = _iota2((c, 3 * c), 0)
    c3 = _iota2((c, 3 * c), 1) & (c - 1)
    tri3 = (c3 <= r3).astype(BF16)
    tt = _iota2((c, LANE), 0)
    ts = _iota2((c, LANE), 1) & (c - 1)
    strict = [ts < tt, ts > tt]
    incl = [ts <= tt, ts >= tt]
    eye_c = (ts == tt).astype(F32)

    def prefix(x):
        xh, xl = _split(x)
        xll = (x - xh.astype(F32) - xl.astype(F32)).astype(BF16)
        return jnp.dot(tri3, jnp.concatenate([xh, xl, xll], axis=0), preferred_element_type=F32)

    pre = [prefix(x) for x in logw]
    ltot = [pre[q][c - 1:c, dsl[d]] for q, d in chains]
    lc = [pre[q][:, dsl[0]] if d == 0 else ltot[i] - pre[q][:, dsl[1]] + lw[i]
          for i, (q, d) in enumerate(chains)]
    yield
    n = len(chains)
    bdot = lambda a, b: jnp.dot(a, b, preferred_element_type=F32)
    kkd = [(kk[q] * jnp.exp(lc[i] - lw[i])).astype(BF16) for i, (q, d) in enumerate(chains)]
    rd = [rs[q] * jnp.exp(lc[i]) for i, (q, d) in enumerate(chains)]
    e_inv = [jnp.exp(-x) for x in lc]
    inv_s = [jnp.concatenate([stack_b(bb[i] * e_inv[i]), stack_b(kd[i] * e_inv[i])], axis=0)
             for i in range(n)]
    yield
    amat = [lax.dot_general(jnp.concatenate([kkd[i], rd[i].astype(BF16)], axis=0), inv_s[i],
                            NT, preferred_element_type=F32).astype(BF16) for i in range(n)]
    zero_c = jnp.zeros((c, LANE), BF16)
    a_kb = [jnp.where(strict[d], amat[i][:c, :LANE], zero_c) for i, (q, d) in enumerate(chains)]
    a_kk = [jnp.where(strict[d], amat[i][:c, LANE:], zero_c) for i, (q, d) in enumerate(chains)]
    aq_b = [jnp.where(incl[d], amat[i][c:, :LANE], zero_c) for i, (q, d) in enumerate(chains)]
    aq_k = [jnp.where(incl[d], amat[i][c:, LANE:], zero_c) for i, (q, d) in enumerate(chains)]
    yield
    av = [bdot(jnp.concatenate([a_kk[i], aq_k[i]], axis=0), vstk[q])
          for i, (q, d) in enumerate(chains)]
    yield
    tinv = [eye_c - a.astype(F32) for a in a_kb]
    qpow = [bdot(a, stack(a)).astype(BF16) for a in a_kb]
    yield
    for _ in range(4):
        prod = [bdot(qpow[i], jnp.concatenate([stack(qpow[i]), stack_b(tinv[i])], axis=1))
                for i in range(n)]
        qpow = [x[:, :LANE].astype(BF16) for x in prod]
        tinv = [tinv[i] + prod[i][:, LANE:] for i in range(n)]
        yield
    tinv = [tinv[i] + bdot(qpow[i], stack_b(tinv[i])) for i in range(n)]
    yield
    tsplit = [_split(t) for t in tinv]
    ia_t = [bdot((eye_c + a_kb[i].astype(F32)).astype(BF16),
                 jnp.concatenate([stack(tsplit[i][0]), stack(tsplit[i][1])], axis=1))
            for i in range(n)]
    resid = [eye_c - ia_t[i][:, :LANE] - ia_t[i][:, LANE:] for i in range(n)]
    yield
    tinv = [tinv[i] + bdot(tinv[i].astype(BF16), stack_b(resid[i])) for i in range(n)]
    yield
    x = [bdot(tinv[i].astype(BF16),
              jnp.concatenate([stack(kkd[i]), stack_b(av[i][:c])], axis=1))
         for i in range(n)]
    xb = [v.astype(BF16) for v in x]
    yield
    qy = [jnp.concatenate([rd[i], av[i][c:]], axis=1)
          - bdot(aq_b[i], jnp.concatenate([stack(xb[i][:, :LANE]), stack(xb[i][:, LANE:])],
                                          axis=1)) for i in range(n)]
    yield
    e_end = [jnp.exp(ltot[i] - lc[i]) for i in range(n)]
    ends = [jnp.concatenate([(-bb[i] * e_end[i]).astype(BF16), (kd[i] * e_end[i]).astype(BF16)],
                            axis=0) for i in range(n)]
    wuv = [jnp.concatenate([xb[i], jnp.concatenate([zero_c, vs_[q].astype(BF16)], axis=1)],
                           axis=0) for i, (q, d) in enumerate(chains)]
    mg = [lax.dot_general(ends[i], wuv[i], TN, preferred_element_type=F32) for i in range(n)]
    yield
    for i, (q, d) in enumerate(chains):
        m_ref[0, ck, d, base + q] = fold(jnp.where(eye, jnp.exp(ltot[i]), 0.0)
                                         + jnp.where(same, mg[i][:, :LANE], 0.0)).astype(BF16)
        g_ref[0, ck, d, base + q] = fold(jnp.where(same, mg[i][:, LANE:], 0.0)
                                         ).astype(g_ref.dtype)
        qt_ref[0, d, rows, qls[q]] = qy[i][:, :LANE].astype(BF16)
    for q in pairs:
        yl_ref[0, rows, qls[q]] = (qy[2 * q][:, LANE:]
                                   + qy[2 * q + 1][:, LANE:]).astype(yl_ref.dtype)
        bn_ref[0, rows, qls[q]] = bonus[q].astype(bn_ref.dtype)


def _rwkv_local(z, w0p, w2p, a0p, a2p, k_k, k_a, r_k):
    b, t, _ = z.shape
    w = k_k.shape[1]
    npair = w // LANE
    pp = PAIRS_PER_STEP
    ng = npair // pp
    wl = pp * LANE
    nc = t // CHUNK
    cs = LOCAL_CHUNKS if nc % LOCAL_CHUNKS == 0 else 1
    rows = cs * CHUNK
    tokc = lambda base: pl.BlockSpec((1, rows, wl), lambda bi, ci, p: (bi, ci, base + p))
    perp3 = lambda n: pl.BlockSpec((n, pp * 2 * LANE), lambda bi, ci, p: (0, p))
    vecp = pl.BlockSpec((1, wl), lambda bi, ci, p: (0, p))
    mat = pl.BlockSpec((1, cs, 2, pp, CHUNK, LANE), lambda bi, ci, p: (bi, ci, 0, p, 0, 0))
    return pl.pallas_call(
        functools.partial(_rwkv_local_body, cs=cs),
        grid=(b, nc // cs, ng),
        in_specs=[tokc(0), tokc(ng), tokc(2 * ng),
                  pl.BlockSpec((1, rows, 2 * LANE), lambda bi, ci, p: (bi, ci, 3 * npair // 2)),
                  perp3(1), perp3(LANE), perp3(1), perp3(LANE), vecp, vecp, vecp],
        out_specs=[mat, mat,
                   pl.BlockSpec((1, 2, rows, wl), lambda bi, ci, p: (bi, 0, ci, p)),
                   pl.BlockSpec((1, rows, wl), lambda bi, ci, p: (bi, ci, p)),
                   pl.BlockSpec((1, rows, wl), lambda bi, ci, p: (bi, ci, p))],
        out_shape=[jax.ShapeDtypeStruct((b, nc, 2, npair, CHUNK, LANE), BF16),
                   jax.ShapeDtypeStruct((b, nc, 2, npair, CHUNK, LANE), ACT),
                   jax.ShapeDtypeStruct((b, 2, t, w), BF16),
                   jax.ShapeDtypeStruct((b, t, w), ACT),
                   jax.ShapeDtypeStruct((b, t, w), ACT)],
        compiler_params=_cparams(("parallel", "parallel", "parallel")),
        name="rwkv_local",
    )(z, z, z, z, w0p, w2p, a0p, a2p, k_k, k_a, r_k)


SCAN_CHUNKS = 8


def _rwkv_scan_body(m0_ref, g0_ref, q0_ref, m1_ref, g1_ref, q1_ref, h0_ref,
                    y0_ref, y1_ref, hfin_ref, h_scr, *, npair, cs):
    ci = pl.program_id(1)

    @pl.when(ci == 0)
    def _():
        h_scr[...] = h0_ref[0]

    head0 = _iota2((1, LANE), 1) < RWKV_HEAD

    def expand(x):
        z = jnp.zeros_like(x)
        return jnp.concatenate([jnp.where(head0, x, z), jnp.where(head0, z, x)], axis=0)

    refs = ((m0_ref, g0_ref, q0_ref, y0_ref), (m1_ref, g1_ref, q1_ref, y1_ref))
    chains = [(d, p) for d in range(2) for p in range(npair)]
    lanes = [slice(p * LANE, (p + 1) * LANE) for p in range(npair)]
    h = [h_scr[d, p] for d, p in chains]
    for step in range(cs):
        ck = (step, cs - 1 - step)
        rows = [slice(c * CHUNK, (c + 1) * CHUNK) for c in ck]
        hb = [x.astype(BF16) for x in h]
        res = [jnp.dot(jnp.concatenate([refs[d][2][0, 0, rows[d], lanes[p]],
                                        refs[d][0][0, ck[d], 0, p]], axis=0), hb[i],
                       preferred_element_type=F32) for i, (d, p) in enumerate(chains)]
        for i, (d, p) in enumerate(chains):
            refs[d][3][0, rows[d], lanes[p]] = res[i][:CHUNK].astype(refs[d][3].dtype)
        h = [expand(res[i][CHUNK:] + refs[d][1][0, ck[d], 0, p].astype(F32))
             for i, (d, p) in enumerate(chains)]
    for i, (d, p) in enumerate(chains):
        h_scr[d, p] = h[i]

    @pl.when(ci == pl.num_programs(1) - 1)
    def _():
        hfin_ref[0] = h_scr[...]


def _rwkv_scan(mm, gg, qt, h0):
    b, nc, _, npair, _, _ = mm.shape
    t, w = qt.shape[2], qt.shape[3]
    cs = SCAN_CHUNKS if nc % SCAN_CHUNKS == 0 else 1
    nb = nc // cs
    fwd = lambda bi, ci: (bi, ci, 0, 0, 0, 0)
    rev = lambda bi, ci: (bi, nb - 1 - ci, 1, 0, 0, 0)
    mblk = (1, cs, 1, npair, CHUNK, LANE)
    hspec = pl.BlockSpec((1, 2, npair, LANE, LANE), lambda bi, ci: (bi, 0, 0, 0, 0))
    return pl.pallas_call(
        functools.partial(_rwkv_scan_body, npair=npair, cs=cs),
        grid=(b, nb),
        in_specs=[pl.BlockSpec(mblk, fwd), pl.BlockSpec(mblk, fwd),
                  pl.BlockSpec((1, 1, cs * CHUNK, w), lambda bi, ci: (bi, 0, ci, 0)),
                  pl.BlockSpec(mblk, rev), pl.BlockSpec(mblk, rev),
                  pl.BlockSpec((1, 1, cs * CHUNK, w), lambda bi, ci: (bi, 1, nb - 1 - ci, 0)),
                  hspec],
        out_specs=[pl.BlockSpec((1, cs * CHUNK, w), lambda bi, ci: (bi, ci, 0)),
                   pl.BlockSpec((1, cs * CHUNK, w), lambda bi, ci: (bi, nb - 1 - ci, 0)),
                   hspec],
        out_shape=[jax.ShapeDtypeStruct((b, t, w), ACT), jax.ShapeDtypeStruct((b, t, w), ACT),
                   jax.ShapeDtypeStruct(h0.shape, F32)],
        scratch_shapes=[pltpu.VMEM((2, npair, LANE, LANE), F32)],
        compiler_params=_cparams(("parallel", "arbitrary")),
        name="rwkv_scan",
    )(mm, gg, qt, mm, gg, qt, h0)


def _rwkv_out_body(y0_ref, y1_ref, yl_ref, bn_ref, gate_ref, x_ref, gl_ref, gnw_ref, gnb_ref,
                   w_ref, o_ref):
    y = y0_ref[0].astype(F32) + y1_ref[0].astype(F32) + yl_ref[0].astype(F32)
    r2 = _iota2((LANE, LANE), 0)
    c2 = _iota2((LANE, LANE), 1)
    avg = ((r2 // RWKV_HEAD) == (c2 // RWKV_HEAD)).astype(F32) * (1.0 / RWKV_HEAD)
    parts = []
    for p in range(y.shape[1] // LANE):
        yp = y[:, p * LANE:(p + 1) * LANE]
        dl = yp - _mm2r(yp, avg)
        var = _mm2r(dl * dl, avg)
        parts.append(dl * lax.rsqrt(var + GN_EPS))
    yn = jnp.concatenate(parts, axis=1)
    gt = gate_ref[0].astype(F32)
    act = (yn * gnw_ref[...] + gnb_ref[...] + bn_ref[0].astype(F32)) * (gt * _sigmoid(gt))
    out = jnp.dot(act.astype(BF16), w_ref[...], preferred_element_type=F32)
    o_ref[0] = x_ref[0] + gl_ref[0] * out


def _rwkv_out(y0, y1, yl, bn, gate, x, gl, gnw, gnb, w, tm):
    b, t, d = x.shape
    tm = min(tm, t)
    wd = y0.shape[2]
    tok = lambda n: pl.BlockSpec((1, tm, n), lambda bi, i: (bi, i, 0))
    return pl.pallas_call(
        _rwkv_out_body,
        grid=(b, t // tm),
        in_specs=[tok(wd), tok(wd), tok(wd), tok(wd), tok(wd), tok(d),
                  pl.BlockSpec((1, 1, d), lambda bi, i: (bi, 0, 0)),
                  pl.BlockSpec((1, wd), lambda bi, i: (0, 0)),
                  pl.BlockSpec((1, wd), lambda bi, i: (0, 0)),
                  pl.BlockSpec(w.shape, lambda bi, i: (0, 0))],
        out_specs=tok(d),
        out_shape=jax.ShapeDtypeStruct((b, t, d), F32),
        compiler_params=_cparams(("parallel", "parallel")),
        name="rwkv_out",
    )(y0, y1, yl, bn, gate, x, gl, gnw, gnb, w)


def _rope_tables(t):
    rows = t // GRID_W
    row = jnp.repeat(jnp.arange(rows, dtype=F32), GRID_W)
    col = jnp.tile(jnp.arange(GRID_W, dtype=F32), rows)
    inv = 1.0 / (ROPE_BASE ** (jnp.arange(ROPE_FREQS, dtype=F32) / ROPE_FREQS))
    ang = jnp.stack([row[:, None] * inv, col[:, None] * inv], axis=1)
    cos, sin = jnp.cos(ang), jnp.sin(ang)
    zeros = jnp.zeros_like(sin)
    ones_lo = jnp.ones((t, QK_NOPE), F32)
    pad_hi = HEAD_SLOT - QK_HEAD
    cos_t = jnp.concatenate([ones_lo, jnp.concatenate([cos, cos], axis=2).reshape(t, QK_ROPE),
                             jnp.ones((t, pad_hi), F32)], axis=1)
    sa = jnp.concatenate([jnp.zeros((t, QK_NOPE), F32),
                          jnp.concatenate([-sin, zeros], axis=2).reshape(t, QK_ROPE),
                          jnp.zeros((t, pad_hi), F32)], axis=1)
    sb = jnp.concatenate([jnp.zeros((t, QK_NOPE), F32),
                          jnp.concatenate([zeros, sin], axis=2).reshape(t, QK_ROPE),
                          jnp.zeros((t, pad_hi), F32)], axis=1)
    return cos_t, sa, sb


def _even_layer(x, ctx, mod_l, mod_c, need_ctx, g, w_in, kv_norm, q_norm, w_uq, w_ukv,
                q_head_norm, k_head_norm, w_fnet, w_out):
    b, s, d = x.shape
    tc = ctx.shape[1]
    e_q0 = KV_LORA + QK_ROPE
    e_f0 = e_q0 + Q_LORA
    e_g0 = e_f0 + FNET_GROUPS * FNET_GROUP_DIM
    w_p = jnp.concatenate([w_in[:, e_g0:], w_in[:, e_f0:e_g0], w_in[:, :e_q0],
                           jnp.zeros((d, LANE - QK_ROPE), F32), w_in[:, e_q0:e_f0]],
                          axis=1).astype(BF16)
    splits = (d, (FNET_GROUPS, FNET_GROUP_DIM), KV_LORA + LANE + Q_LORA)
    kvw = w_ukv.reshape(KV_LORA, MLA_HEADS, QK_NOPE + V_HEAD)
    wk = jnp.pad(kvw[:, :, :QK_NOPE], ((0, 0), (0, 0), (0, HEAD_SLOT - QK_NOPE)))
    wk = wk.reshape(KV_LORA, MLA_HEADS * HEAD_SLOT).astype(BF16)
    wv = jnp.pad(kvw[:, :, QK_NOPE:], ((0, 0), (0, 0), (0, HEAD_SLOT - V_HEAD)))
    wv = wv.reshape(KV_LORA, MLA_HEADS * HEAD_SLOT).astype(BF16)
    wq3 = jnp.pad(w_uq.reshape(Q_LORA, MLA_HEADS, QK_HEAD), ((0, 0), (0, 0), (0, HEAD_SLOT - QK_HEAD)))
    wq = wq3.reshape(Q_LORA, MLA_HEADS * HEAD_SLOT).astype(BF16)
    kg = jnp.pad(k_head_norm, (0, HEAD_SLOT - QK_HEAD)).reshape(1, HEAD_SLOT)
    qg = (jnp.pad(q_head_norm, (0, HEAD_SLOT - QK_HEAD))
          * (QK_HEAD ** -0.5 * math.log2(math.e))).reshape(1, HEAD_SLOT)
    lane = np.arange(HEAD_SLOT)
    tail = (lane >= QK_NOPE) & (lane < QK_HEAD)
    first = tail & (((lane - QK_NOPE) // ROPE_FREQS) % 2 == 0)
    partner = np.where(first, lane + ROPE_FREQS, np.where(tail, lane - ROPE_FREQS, lane))
    sign = np.where(first, -1.0, np.where(tail, 1.0, 0.0)).astype(np.float32)
    wqr = (wq3[:, :, partner] * (sign * qg[0, partner])).reshape(Q_LORA, MLA_HEADS * HEAD_SLOT)
    wqr = wqr.astype(BF16)
    kvn, qn = kv_norm.reshape(1, -1), q_norm.reshape(1, -1)
    g2 = g.reshape(1, d)
    bound = (1.02 * QK_HEAD * jnp.max(jnp.abs(qg)) * jnp.max(jnp.abs(kg))).astype(BF16).astype(F32)
    static_ok = bound <= MAX_STATIC_BOUND
    bias_lane = (jnp.arange(HEAD_SLOT) == BIAS_LANE).astype(F32).reshape(1, HEAD_SLOT)
    kb = bias_lane * jnp.where(static_ok, -bound, 0.0)
    qb = bias_lane

    gate_l, four_l, ua_l = _proj(x, g2, mod_l[1], mod_l[0], w_p, splits, TOKEN_TILE)
    gate_c, four_c, ua_c = _proj(ctx, g2, mod_c[1], mod_c[0], w_p, splits, TOKEN_TILE)
    sk = s + tc
    cos_t, sa, sb = _rope_tables(s)
    cos_t = jnp.concatenate([cos_t, jnp.ones((tc, HEAD_SLOT), F32)], axis=0)
    sa = jnp.concatenate([sa, jnp.zeros((tc, HEAD_SLOT), F32)], axis=0)
    sb = jnp.concatenate([sb, jnp.zeros((tc, HEAD_SLOT), F32)], axis=0)
    tabs = (cos_t * kg, cos_t * qg, sa, sb, sb - sa)
    q_all, k_all, v_all = _qkv(ua_l, ua_c, kvn, qn, wk, wv, wq, wqr, kg, kb, qb, tabs,
                               math.gcd(s, tc))
    bk = ATTN_K_BLOCK if sk % ATTN_K_BLOCK == 0 else tc
    o_l = _attention(q_all, k_all, v_all, static_ok, 0, s, 0, sk, ATTN_Q_BLOCK, bk)
    f_l = _fourier_latent(four_l, w_fnet)
    wo = w_out.astype(BF16)
    x_new = _merge(o_l, f_l, gate_l, x, mod_l[2], wo, OUT_TILE)
    ctx_new = ctx
    if need_ctx:
        o_c = _attention(q_all, k_all, v_all, static_ok, s, tc, s, tc, tc, tc)
        f_c = _fourier_dense(four_c, w_fnet)
        ctx_new = _merge(o_c, f_c, gate_c, ctx, mod_c[2], wo, OUT_TILE)
    return x_new, ctx_new


def _odd_layer(x, ctx, mod_l, mod_c, need_ctx, g, w_in, shift_w, w0, w2, a0, a2, k_k, k_a, r_k,
               gn_w, gn_b, w_out):
    b, s, d = x.shape
    w = k_k.shape[0]
    npair = w // LANE
    o_wd0 = 2 * w
    o_r0 = o_wd0 + 2 * DECAY_LORA + 2 * AAA_LORA
    conv_ch = o_r0 + w
    segs = ((0, o_wd0), (o_r0, w), (o_wd0, o_r0 - o_wd0))
    w_p = w_in.astype(BF16)
    sw = shift_w
    g2 = g.reshape(1, d)

    def pairs(vec2):
        return vec2.reshape(2, npair, LANE).transpose(1, 0, 2).reshape(1, npair * 2 * LANE)

    def pair_mats(m):
        rr = m.shape[1]
        mp = m.reshape(2, rr, npair, LANE).transpose(2, 0, 1, 3)
        z = jnp.zeros_like(mp[:, 0])
        top = jnp.concatenate([mp[:, 0], z], axis=2)
        bot = jnp.concatenate([z, mp[:, 1]], axis=2)
        full = jnp.concatenate([top, bot], axis=1)
        return full.transpose(1, 0, 2).reshape(2 * rr, npair * 2 * LANE).astype(BF16)

    w0p, a0p, w2p, a2p = pairs(w0), pairs(a0), pair_mats(w2), pair_mats(a2)
    kk2, ka2, rk2 = k_k.reshape(1, w), k_a.reshape(1, w), r_k.reshape(1, w)
    wo = w_out.astype(BF16)

    def mix(xin, mod, h0):
        z, gate = _proj_shift(xin, g2, mod[1], mod[0], w_p, sw, conv_ch, segs, TOKEN_TILE)
        mm, gg, qt, yl, bn = _rwkv_local(z, w0p, w2p, a0p, a2p, kk2, ka2, rk2)
        y0, y1, hfin = _rwkv_scan(mm, gg, qt, h0)
        return (y0, y1, yl, bn, gate), hfin

    h_zero = jnp.zeros((b, 2, npair, LANE, LANE), F32)
    parts_c, h_ctx = mix(ctx, mod_c, h_zero)
    parts_l, _ = mix(x, mod_l, h_ctx)
    gnw, gnb = gn_w.reshape(1, w), gn_b.reshape(1, w)
    x_new = _rwkv_out(*parts_l, x, mod_l[2], gnw, gnb, wo, OUT_TILE)
    ctx_new = ctx
    if need_ctx:
        ctx_new = _rwkv_out(*parts_c, ctx, mod_c[2], gnw, gnb, wo, OUT_TILE)
    return x_new, ctx_new


def kernel(x, c, ctx, c_ctx, ada_w, ada_b, norm_g, e_w_in, e_kv_norm, e_q_norm, e_w_uq, e_w_ukv,
           e_q_head_norm, e_k_head_norm, e_w_fnet, e_w_out, o_w_in, o_shift_w, o_w0, o_w2, o_a0,
           o_a2, o_k_k, o_k_a, o_r_k, o_gn_w, o_gn_b, o_w_out):
    b, s, d = x.shape
    depth = ada_w.shape[0]
    assert b + 1 <= 8
    cond8 = jnp.concatenate([c, c_ctx[None, :], jnp.zeros((8 - b - 1, d), F32)], axis=0)
    mod = _ada(cond8, ada_w, ada_b)
    for layer in range(depth):
        need_ctx = layer < depth - 1
        m = mod[layer]
        chunk = lambda rows, i: rows[:, None, i * d:(i + 1) * d]
        lat, cx = m[:b], jnp.broadcast_to(m[b:b + 1], (b, 3 * d))
        mod_l = (chunk(lat, 0), 1.0 + chunk(lat, 1), chunk(lat, 2))
        mod_c = (chunk(cx, 0), 1.0 + chunk(cx, 1), chunk(cx, 2))
        j = layer // 2
        if layer % 2 == 0:
            x, ctx = _even_layer(x, ctx, mod_l, mod_c, need_ctx, norm_g[layer], e_w_in[j],
                                 e_kv_norm[j], e_q_norm[j], e_w_uq[j], e_w_ukv[j],
                                 e_q_head_norm[j], e_k_head_norm[j], e_w_fnet[j], e_w_out[j])
        else:
            x, ctx = _odd_layer(x, ctx, mod_l, mod_c, need_ctx, norm_g[layer], o_w_in[j],
                                o_shift_w[j], o_w0[j], o_w2[j], o_a0[j], o_a2[j], o_k_k[j],
                                o_k_a[j], o_r_k[j].reshape(-1), o_gn_w[j], o_gn_b[j], o_w_out[j])
    return x
```

```python
import functools
import math

import numpy as np
import jax
import jax.numpy as jnp
from jax import lax
from jax.experimental import pallas as pl
from jax.experimental.pallas import tpu as pltpu

F32 = jnp.float32
BF16 = jnp.bfloat16
ACT = BF16

GRID_W = 64
NORM_EPS = 1e-6
MLA_HEADS = 8
QK_NOPE = 64
QK_ROPE = 32
QK_HEAD = QK_NOPE + QK_ROPE
V_HEAD = 64
Q_LORA = 384
KV_LORA = 256
ROPE_FREQS = QK_ROPE // 4
ROPE_BASE = 10000.0
FNET_GROUPS = 4
FNET_GROUP_DIM = 128
RWKV_HEAD = 64
DECAY_LORA = 64
AAA_LORA = 64
GN_EPS = 64e-5

LANE = 128
CHUNK = 64
HEAD_SLOT = 128
VMEM_LIMIT = 56 * 1024 * 1024

TOKEN_TILE = 512
OUT_TILE = 1024
ATTN_Q_BLOCK = 2048
ATTN_K_BLOCK = 768
FOUR_LANE_TILE = 4096
FOUR_ROWS_TILE = 32

NN = (((1,), (0,)), ((), ()))
NT = (((1,), (1,)), ((), ()))
TN = (((0,), (0,)), ((), ()))


def _cparams(sem):
    return pltpu.CompilerParams(dimension_semantics=sem, vmem_limit_bytes=VMEM_LIMIT)


def _mm(a, b, dn=NN):
    return lax.dot_general(a.astype(BF16), b.astype(BF16), dn, preferred_element_type=F32)


def _split(a):
    hi = a.astype(BF16)
    lo = (a - hi.astype(F32)).astype(BF16)
    return hi, lo


def _mm3(a, b, dn=NN):
    ah, al = _split(a)
    bh, bl = _split(b)
    d = lambda x, y: lax.dot_general(x, y, dn, preferred_element_type=F32)
    return d(ah, bh) + d(al, bh) + d(ah, bl)


def _mm2r(a, b_exact):
    ah, al = _split(a)
    bb = b_exact.astype(BF16)
    return jnp.dot(jnp.concatenate([ah, al], axis=1), jnp.concatenate([bb, bb], axis=0),
                   preferred_element_type=F32)


def _sigmoid(x):
    return 1.0 / (1.0 + jnp.exp(-x))


def _modnorm(x, g, sc1, sh):
    y = x * lax.rsqrt(jnp.mean(x * x, axis=-1, keepdims=True) + NORM_EPS)
    return (y * g) * sc1 + sh


def _iota2(shape, dim):
    return lax.broadcasted_iota(jnp.int32, shape, dim)


def _ada_body(c_ref, w_ref, b_ref, o_ref):
    c = c_ref[...]
    s = c * _sigmoid(c)
    o_ref[0] = _mm3(s, w_ref[0]) + b_ref[0]


def _ada(cond8, ada_w, ada_b):
    depth, d, n = ada_w.shape
    tn = 512
    return pl.pallas_call(
        _ada_body,
        grid=(depth, n // tn),
        in_specs=[
            pl.BlockSpec((8, d), lambda l, j: (0, 0)),
            pl.BlockSpec((1, d, tn), lambda l, j: (l, 0, j)),
            pl.BlockSpec((1, 1, tn), lambda l, j: (l, 0, j)),
        ],
        out_specs=pl.BlockSpec((1, 8, tn), lambda l, j: (l, 0, j)),
        out_shape=jax.ShapeDtypeStruct((depth, 8, n), F32),
        compiler_params=_cparams(("parallel", "parallel")),
        name="ada",
    )(cond8, ada_w, ada_b.reshape(depth, 1, n))


COL_CHUNK = 512


def _proj_body(x_ref, g_ref, sc_ref, sh_ref, w_ref, *o_refs, splits):
    h = _modnorm(x_ref[0], g_ref[...], sc_ref[0], sh_ref[0]).astype(BF16)
    mm = lambda c0, c1: jnp.dot(h, w_ref[:, c0:c1], preferred_element_type=F32)
    off = 0
    for o_ref, n in zip(o_refs, splits):
        if isinstance(n, tuple):
            groups, width = n
            y = mm(off, off + groups * width).astype(o_ref.dtype)
            for gi in range(groups):
                o_ref[0, gi] = y[:, gi * width:(gi + 1) * width]
            off += groups * width
            continue
        for c0 in range(0, n, COL_CHUNK):
            c1 = min(n, c0 + COL_CHUNK)
            o_ref[0, :, c0:c1] = mm(off + c0, off + c1).astype(o_ref.dtype)
        off += n


def _proj(x, g, sc1, sh, w, splits, tm):
    b, t, d = x.shape
    tm = min(tm, t)
    n = w.shape[1]
    vec = pl.BlockSpec((1, 1, d), lambda bi, i: (bi, 0, 0))
    specs, shapes = [], []
    for s in splits:
        if isinstance(s, tuple):
            specs.append(pl.BlockSpec((1, s[0], tm, s[1]), lambda bi, i: (bi, 0, i, 0)))
            shapes.append(jax.ShapeDtypeStruct((b, s[0], t, s[1]), ACT))
        else:
            specs.append(pl.BlockSpec((1, tm, s), lambda bi, i: (bi, i, 0)))
            shapes.append(jax.ShapeDtypeStruct((b, t, s), ACT))
    return pl.pallas_call(
        functools.partial(_proj_body, splits=splits),
        grid=(b, t // tm),
        in_specs=[
            pl.BlockSpec((1, tm, d), lambda bi, i: (bi, i, 0)),
            pl.BlockSpec((1, d), lambda bi, i: (0, 0)),
            vec, vec,
            pl.BlockSpec((d, n), lambda bi, i: (0, 0)),
        ],
        out_specs=specs,
        out_shape=shapes,
        compiler_params=_cparams(("parallel", "parallel")),
        name="proj",
    )(x, g, sc1, sh, w)


HALO = 16


def _proj_shift_body(x_ref, xp_ref, xn_ref, g_ref, sc_ref, sh_ref, w_ref, sw_ref, z_ref, gate_ref,
                     *, tm, n_conv, segs):
    i = pl.program_id(1)
    last = pl.num_programs(1) - 1
    g, sc1, sh = g_ref[...], sc_ref[0], sh_ref[0]
    h = _modnorm(x_ref[0], g, sc1, sh)
    hp = _modnorm(xp_ref[0], g, sc1, sh) * (i > 0).astype(F32)
    hn = _modnorm(xn_ref[0], g, sc1, sh) * (i < last).astype(F32)
    hb = jnp.concatenate([hp, h, hn], axis=0).astype(BF16)
    rows = tm + 2 * HALO
    dst = 0
    for src, width in segs:
        for c0 in range(0, width, COL_CHUNK):
            cw = min(COL_CHUNK, width - c0)
            cols = slice(src + c0, src + c0 + cw)
            u = jnp.dot(hb, w_ref[:, cols], preferred_element_type=F32)
            up = pltpu.roll(u, 1, 0)[HALO:HALO + tm]
            un = pltpu.roll(u, rows - 1, 0)[HALO:HALO + tm]
            um = u[HALO:HALO + tm]
            z_ref[0, :, dst + c0:dst + c0 + cw] = (
                sw_ref[0:1, cols] * up + sw_ref[1:2, cols] * um
                + sw_ref[2:3, cols] * un).astype(z_ref.dtype)
        dst += width
    hc = hb[HALO:HALO + tm]
    n_all = w_ref.shape[1]
    for c0 in range(n_conv, n_all, COL_CHUNK):
        c1 = min(n_all, c0 + COL_CHUNK)
        gate_ref[0, :, c0 - n_conv:c1 - n_conv] = jnp.dot(
            hc, w_ref[:, c0:c1], preferred_element_type=F32).astype(gate_ref.dtype)


def _proj_shift(x, g, sc1, sh, w, sw, n_conv, segs, tm):
    b, t, d = x.shape
    tm = min(tm, t)
    n = w.shape[1]
    hb = tm // HALO
    nhb = t // HALO
    vec = pl.BlockSpec((1, 1, d), lambda bi, i: (bi, 0, 0))
    return pl.pallas_call(
        functools.partial(_proj_shift_body, tm=tm, n_conv=n_conv, segs=segs),
        grid=(b, t // tm),
        in_specs=[
            pl.BlockSpec((1, tm, d), lambda bi, i: (bi, i, 0)),
            pl.BlockSpec((1, HALO, d), lambda bi, i: (bi, jnp.maximum(i * hb - 1, 0), 0)),
            pl.BlockSpec((1, HALO, d), lambda bi, i: (bi, jnp.minimum((i + 1) * hb, nhb - 1), 0)),
            pl.BlockSpec((1, d), lambda bi, i: (0, 0)),
            vec, vec,
            pl.BlockSpec((d, n), lambda bi, i: (0, 0)),
            pl.BlockSpec((3, n_conv), lambda bi, i: (0, 0)),
        ],
        out_specs=[pl.BlockSpec((1, tm, n_conv), lambda bi, i: (bi, i, 0)),
                   pl.BlockSpec((1, tm, n - n_conv), lambda bi, i: (bi, i, 0))],
        out_shape=[jax.ShapeDtypeStruct((b, t, n_conv), ACT),
                   jax.ShapeDtypeStruct((b, t, n - n_conv), ACT)],
        compiler_params=_cparams(("parallel", "parallel")),
        name="proj_shift",
    )(x, x, x, g, sc1, sh, w, sw)


def _rms(x, g):
    return x * lax.rsqrt(jnp.mean(x * x, axis=-1, keepdims=True) + NORM_EPS) * g


def _qkv_body(ual_ref, uac_ref, kvn_ref, qn_ref, wk_ref, wv_ref, wq_ref, wqr_ref, kg_ref, kb_ref,
              qb_ref, cosk_ref, cosq_ref, sa_ref, sb_ref, sinq_ref, q_ref, k_ref, v_ref, *, n_lat):
    ua = jnp.where(pl.program_id(1) < n_lat, ual_ref[0], uac_ref[0]).astype(F32)
    ckv = _rms(ua[:, :KV_LORA], kvn_ref[...]).astype(BF16)
    kr = ua[:, KV_LORA:KV_LORA + LANE]
    cq = _rms(ua[:, KV_LORA + LANE:], qn_ref[...]).astype(BF16)
    kn = jnp.dot(ckv, wk_ref[...], preferred_element_type=F32)
    vv = jnp.dot(ckv, wv_ref[...], preferred_element_type=F32)
    qq = jnp.dot(cq, wq_ref[...], preferred_element_type=F32)
    qr = jnp.dot(cq, wqr_ref[...], preferred_element_type=F32)
    ones_hi = (_iota2((1, HEAD_SLOT), 1) >= V_HEAD).astype(F32)
    pe = pltpu.roll(kr, QK_NOPE, 1)
    gp = pe * kg_ref[...]
    pe_rot = (pltpu.roll(gp, LANE - ROPE_FREQS, 1) * sa_ref[...]
              + pltpu.roll(gp, ROPE_FREQS, 1) * sb_ref[...])
    cosk, cosq, sinq = cosk_ref[...], cosq_ref[...], sinq_ref[...]
    inv_n = 1.0 / QK_HEAD
    scale = lambda x: lax.rsqrt(jnp.sum(x * x, axis=-1, keepdims=True) * inv_n + NORM_EPS)

    for h in range(MLA_HEADS):
        sl = slice(h * HEAD_SLOT, (h + 1) * HEAD_SLOT)
        kh = kn[:, sl] + pe
        k_ref[0, h] = (scale(kh) * (kh * cosk + pe_rot) + kb_ref[...]).astype(BF16)
        qh = qq[:, sl]
        q_ref[0, h] = (scale(qh) * (qh * cosq + qr[:, sl] * sinq) + qb_ref[...]).astype(BF16)
        v_ref[0, h] = (vv[:, sl] + ones_hi).astype(BF16)


def _qkv(ua_l, ua_c, kvn, qn, wk, wv, wq, wqr, kg, kb, qb, tabs, tm):
    b, s, wa = ua_l.shape
    tc = ua_c.shape[1]
    nl, ncx = s // tm, tc // tm
    full = lambda a: pl.BlockSpec(a.shape, lambda bi, i: (0,) * a.ndim)
    tab = pl.BlockSpec((tm, LANE), lambda bi, i: (i, 0))
    head = pl.BlockSpec((1, MLA_HEADS, tm, HEAD_SLOT), lambda bi, i: (bi, 0, i, 0))
    shape = jax.ShapeDtypeStruct((b, MLA_HEADS, s + tc, HEAD_SLOT), BF16)
    return pl.pallas_call(
        functools.partial(_qkv_body, n_lat=nl),
        grid=(b, nl + ncx),
        in_specs=[pl.BlockSpec((1, tm, wa), lambda bi, i: (bi, jnp.minimum(i, nl - 1), 0)),
                  pl.BlockSpec((1, tm, wa), lambda bi, i: (bi, jnp.maximum(i - nl, 0), 0)),
                  full(kvn), full(qn), full(wk), full(wv), full(wq), full(wqr), full(kg),
                  full(kb), full(qb)] + [tab] * len(tabs),
        out_specs=[head, head, head],
        out_shape=[shape, shape, shape],
        compiler_params=_cparams(("parallel", "parallel")),
        name="qkv",
    )(ua_l, ua_c, kvn, qn, wk, wv, wq, wqr, kg, kb, qb, *tabs)


BIAS_LANE = QK_HEAD
MAX_STATIC_BOUND = 50.0


def _attn_finish(acc_ref, o_ref):
    bq = acc_ref.shape[1]
    lane = _iota2((bq, LANE), 1)
    o0 = acc_ref[0] / pltpu.roll(acc_ref[0], V_HEAD, 1)
    o1 = acc_ref[1] / pltpu.roll(acc_ref[1], V_HEAD, 1)
    o_ref[0] = jnp.where(lane < V_HEAD, o0, pltpu.roll(o1, V_HEAD, 1)).astype(o_ref.dtype)


def _attn_static_body(q_ref, k_ref, v_ref, o_ref, acc_ref, p0_ref, p1_ref, *, bk):
    nk = k_ref.shape[2] // bk
    krows = lambda j: pl.ds(pl.multiple_of(j * bk, bk), bk)
    bufs = (p0_ref, p1_ref)

    def weights(j, slot):
        for hh in range(2):
            s = lax.dot_general(q_ref[0, hh], k_ref[0, hh, krows(j), :], NT,
                                preferred_element_type=F32)
            bufs[slot][hh] = jnp.exp2(s.astype(BF16))

    def values(j, slot):
        for hh in range(2):
            acc_ref[hh] += jnp.dot(bufs[slot][hh], v_ref[0, hh, krows(j), :],
                                   preferred_element_type=F32)

    acc_ref[...] = jnp.zeros(acc_ref.shape, F32)
    weights(0, 0)

    def two_blocks(jj, carry):
        j = 2 * jj
        weights(j + 1, 1)
        values(j, 0)
        weights(j + 2, 0)
        values(j + 1, 1)
        return carry

    pairs_done = (nk - 1) // 2
    lax.fori_loop(0, pairs_done, two_blocks, 0)
    j = 2 * pairs_done
    if (nk - 1) % 2:
        weights(j + 1, 1)
        values(j, 0)
        values(j + 1, 1)
    else:
        values(j, 0)
    _attn_finish(acc_ref, o_ref)


def _attn_online_body(q_ref, k_ref, v_ref, o_ref, acc_ref, m_ref, *, bk):
    acc_ref[...] = jnp.zeros(acc_ref.shape, F32)
    m_ref[...] = jnp.full(m_ref.shape, -jnp.inf, F32)

    def step(j, carry):
        rows = pl.ds(pl.multiple_of(j * bk, bk), bk)
        for hh in range(2):
            s = lax.dot_general(q_ref[0, hh], k_ref[0, hh, rows, :], NT,
                                preferred_element_type=F32)
            m_prev = m_ref[hh]
            m_new = jnp.maximum(m_prev, jnp.max(s, axis=-1, keepdims=True))
            p = jnp.exp2(s - m_new)
            acc_ref[hh] = (jnp.exp2(m_prev - m_new) * acc_ref[hh]
                           + jnp.dot(p.astype(BF16), v_ref[0, hh, rows, :],
                                     preferred_element_type=F32))
            m_ref[hh] = m_new
        return carry

    lax.fori_loop(0, k_ref.shape[2] // bk, step, 0)
    _attn_finish(acc_ref, o_ref)


def _attention(q, k, v, static_ok, q_start, q_rows, k_start, k_rows, bq, bk):
    b, h, _, e = q.shape
    bq, bk = min(bq, q_rows), min(bk, k_rows)
    qi0, kj0 = q_start // bq, k_start // k_rows
    kv_blk = pl.BlockSpec((1, 2, k_rows, e), lambda bi, p, i: (bi, p, kj0, 0))

    def call(online):
        scratch = [pltpu.VMEM((2, bq, LANE), F32)]
        if online:
            scratch.append(pltpu.VMEM((2, bq, 1), F32))
        else:
            scratch += [pltpu.VMEM((2, bq, bk), BF16)] * 2
        return pl.pallas_call(
            functools.partial(_attn_online_body if online else _attn_static_body, bk=bk),
            grid=(b, h // 2, q_rows // bq),
            in_specs=[pl.BlockSpec((1, 2, bq, e), lambda bi, p, i: (bi, p, qi0 + i, 0)),
                      kv_blk, kv_blk],
            out_specs=pl.BlockSpec((1, bq, 2 * V_HEAD), lambda bi, p, i: (bi, i, p)),
            out_shape=jax.ShapeDtypeStruct((b, q_rows, h * V_HEAD), ACT),
            scratch_shapes=scratch,
            compiler_params=_cparams(("parallel", "parallel", "arbitrary")),
            name="attention_online" if online else "attention",
        )(q, k, v)

    return lax.cond(static_ok, lambda: call(False), lambda: call(True))


def _dft_mats(n):
    idx = np.arange(n)
    ang = 2.0 * np.pi * ((idx[:, None] * idx[None, :]) % n) / n
    return np.cos(ang), np.sin(ang)


def _hilo(a):
    a = jnp.asarray(a, F32)
    hi = a.astype(BF16)
    return hi, (a - hi.astype(F32)).astype(BF16)


def _mm3c(ah, al, b, dn=NN):
    bh, bl = _split(b)
    d = lambda x, y: lax.dot_general(x, y, dn, preferred_element_type=F32)
    return d(ah, bh) + d(al, bh) + d(ah, bl)


def _four_rows_body(x_ref, w_ref, tc_ref, ts_ref, o_ref):
    r = tc_ref.shape[0]
    y = jnp.dot(w_ref[...], x_ref[0, 0], preferred_element_type=F32)
    yc, ys = y[:r], y[r:]
    tc, ts = tc_ref[...], ts_ref[...]
    zc = yc * tc - ys * ts
    zs = yc * ts + ys * tc
    gd = FNET_GROUP_DIM
    for j in range(zc.shape[1] // gd):
        o_ref[0, 0, 0, :, j, :] = zc[:, j * gd:(j + 1) * gd]
        o_ref[0, 0, 1, :, j, :] = zs[:, j * gd:(j + 1) * gd]


def _four_cols_body(y_ref, w_ref, cs_ref, wf_ref, o_ref, y3_scr, *, krt, scale):
    def one(j, carry):
        rows = pl.ds(pl.multiple_of(j * GRID_W, GRID_W), GRID_W)
        ycs = jnp.concatenate([y_ref[0, 0, 0, rows, :], y_ref[0, 0, 1, rows, :]], axis=0)
        y3 = jnp.dot(w_ref[...], ycs.astype(BF16),
                     preferred_element_type=F32)
        y3_scr[rows, :] = jnp.concatenate([y3[:GRID_W], y3[GRID_W:]], axis=1).astype(BF16)
        return carry

    lax.fori_loop(0, krt, one, 0, unroll=8)
    f = jnp.dot(y3_scr[...], cs_ref[...], preferred_element_type=F32) * scale
    o_ref[0, 0] = _mm(f, wf_ref[0]).astype(o_ref.dtype)


def _fourier_latent(xf, w_fnet):
    b, g, t, gd = xf.shape
    r = t // GRID_W
    wide = GRID_W * gd
    xv = xf.reshape(b, g, r, wide)
    cr, sr = _dft_mats(r)
    w_rows = jnp.asarray(np.concatenate([cr, sr], axis=0), BF16)
    kr_i, c_i = np.arange(r)[:, None], np.arange(GRID_W)[None, :]
    ang = 2.0 * np.pi * ((kr_i * c_i) % t) / t
    twc = jnp.repeat(jnp.asarray(np.cos(ang), F32), gd, axis=1)
    tws = jnp.repeat(jnp.asarray(np.sin(ang), F32), gd, axis=1)
    tl = min(FOUR_LANE_TILE, wide)
    y2 = pl.pallas_call(
        _four_rows_body,
        grid=(b, g, wide // tl),
        in_specs=[pl.BlockSpec((1, 1, r, tl), lambda bi, gi, l: (bi, gi, 0, l)),
                  pl.BlockSpec((2 * r, r), lambda bi, gi, l: (0, 0)),
                  pl.BlockSpec((r, tl), lambda bi, gi, l: (0, l)),
                  pl.BlockSpec((r, tl), lambda bi, gi, l: (0, l))],
        out_specs=pl.BlockSpec((1, 1, 2, r, tl // gd, gd), lambda bi, gi, l: (bi, gi, 0, 0, l, 0)),
        out_shape=jax.ShapeDtypeStruct((b, g, 2, r, GRID_W, gd), F32),
        compiler_params=_cparams(("parallel", "parallel", "parallel")),
        name="fourier_rows",
    )(xv, w_rows, twc, tws)
    y2v = y2.reshape(b, g, 2, r * GRID_W, gd)
    c64, s64 = _dft_mats(GRID_W)
    w_cols = jnp.asarray(np.block([[c64, -s64], [s64, c64]]), BF16)
    cc, sc = _dft_mats(gd)
    w_chan = jnp.asarray(np.concatenate([cc, -sc], axis=0), BF16)
    krt = min(FOUR_ROWS_TILE, r)
    const = lambda a: pl.BlockSpec(a.shape, lambda bi, gi, i: (0, 0))
    fo = pl.pallas_call(
        functools.partial(_four_cols_body, krt=krt, scale=1.0 / math.sqrt(t * gd)),
        grid=(b, g, r // krt),
        in_specs=[pl.BlockSpec((1, 1, 2, krt * GRID_W, gd), lambda bi, gi, i: (bi, gi, 0, i, 0)),
                  const(w_cols), const(w_chan),
                  pl.BlockSpec((1, gd, gd), lambda bi, gi, i: (gi, 0, 0))],
        out_specs=pl.BlockSpec((1, 1, krt * GRID_W, gd), lambda bi, gi, i: (bi, gi, i, 0)),
        out_shape=jax.ShapeDtypeStruct((b, g, r * GRID_W, gd), ACT),
        scratch_shapes=[pltpu.VMEM((krt * GRID_W, 2 * gd), BF16)],
        compiler_params=_cparams(("parallel", "parallel", "parallel")),
        name="fourier_cols",
    )(y2v, w_cols, w_chan, w_fnet)
    return fo.reshape(b, g, r, GRID_W, gd).transpose(0, 1, 3, 2, 4).reshape(b, g, t, gd)


def _four_dense_body(x_ref, ch_ref, cl_ref, th_ref, tl_ref, sh_ref, sl_ref, wf_ref, o_ref, *, scale):
    x = x_ref[0, 0]
    xh, xl = _split(x)
    d = lambda a, b: jnp.dot(a, b, preferred_element_type=F32)
    z = d(xh, ch_ref[...]) + d(xl, ch_ref[...]) + d(xh, cl_ref[...])
    zc, zs = z[:, :FNET_GROUP_DIM], z[:, FNET_GROUP_DIM:]
    f = (_mm3c(th_ref[...], tl_ref[...], zc) - _mm3c(sh_ref[...], sl_ref[...], zs)) * scale
    o_ref[0, 0] = _mm3(f, wf_ref[0]).astype(o_ref.dtype)


def _fourier_dense(xf, w_fnet):
    b, g, t, gd = xf.shape
    cc, sc = _dft_mats(gd)
    ch, cl = _hilo(np.concatenate([cc, sc], axis=1))
    ct, st = _dft_mats(t)
    cth, ctl = _hilo(ct)
    sth, stl = _hilo(st)
    sq = pl.BlockSpec((t, t), lambda bi, gi: (0, 0))
    cs = pl.BlockSpec((gd, 2 * gd), lambda bi, gi: (0, 0))
    return pl.pallas_call(
        functools.partial(_four_dense_body, scale=1.0 / math.sqrt(t * gd)),
        grid=(b, g),
        in_specs=[pl.BlockSpec((1, 1, t, gd), lambda bi, gi: (bi, gi, 0, 0)), cs, cs, sq, sq, sq, sq,
                  pl.BlockSpec((1, gd, gd), lambda bi, gi: (gi, 0, 0))],
        out_specs=pl.BlockSpec((1, 1, t, gd), lambda bi, gi: (bi, gi, 0, 0)),
        out_shape=jax.ShapeDtypeStruct((b, g, t, gd), ACT),
        compiler_params=_cparams(("parallel", "parallel")),
        name="fourier_dense",
    )(xf, ch, cl, cth, ctl, sth, stl, w_fnet)


def _merge_body(o_ref, f_ref, gate_ref, x_ref, gl_ref, w_ref, out_ref):
    gt = gate_ref[0].astype(F32)
    parts = [o_ref[0]] + [f_ref[0, gi] for gi in range(f_ref.shape[1])]
    mix = jnp.concatenate(parts, axis=-1).astype(F32) * (gt * _sigmoid(gt))
    y = jnp.dot(mix.astype(BF16), w_ref[...], preferred_element_type=F32)
    out_ref[0] = x_ref[0] + gl_ref[0] * y


def _merge(o, f, gate, x, gl, w, tm):
    b, t, d = x.shape
    tm = min(tm, t)
    half = o.shape[2]
    tok = lambda n: pl.BlockSpec((1, tm, n), lambda bi, i: (bi, i, 0))
    return pl.pallas_call(
        _merge_body,
        grid=(b, t // tm),
        in_specs=[tok(half),
                  pl.BlockSpec((1, f.shape[1], tm, f.shape[3]), lambda bi, i: (bi, 0, i, 0)),
                  tok(d), tok(d),
                  pl.BlockSpec((1, 1, d), lambda bi, i: (bi, 0, 0)),
                  pl.BlockSpec(w.shape, lambda bi, i: (0, 0))],
        out_specs=tok(d),
        out_shape=jax.ShapeDtypeStruct((b, t, d), F32),
        compiler_params=_cparams(("parallel", "parallel")),
        name="merge",
    )(o, f, gate, x, gl, w)


EXP_M05 = math.exp(-0.5)


PAIRS_PER_STEP = 8
PAIRS_PER_GROUP = 8
LOCAL_CHUNKS = 4
GROUP_LAG = 5


def _rwkv_local_body(*refs, cs):
    zwa_ref, w0_ref, w2_ref, a0_ref, a2_ref = refs[3:8]
    zwa = zwa_ref[0].astype(F32)
    lora = (_mm(jnp.tanh(zwa[:, :LANE]), w2_ref[...]) + w0_ref[...],
            _mm(zwa[:, LANE:], a2_ref[...]) + a0_ref[...])
    groups = [_rwkv_local_group(*refs, lora=lora, base=base, ck=ck) for ck in range(cs)
              for base in range(0, PAIRS_PER_STEP, PAIRS_PER_GROUP)]
    tick = 0
    while groups:
        live = groups[:tick // GROUP_LAG + 1]
        for g in live:
            if next(g, StopIteration) is StopIteration:
                groups.remove(g)
        tick += 1


def _rwkv_local_group(zk_ref, zv_ref, zr_ref, zwa_ref, w0_ref, w2_ref, a0_ref, a2_ref,
                      kk_ref, ka_ref, rk_ref, m_ref, g_ref, qt_ref, yl_ref, bn_ref,
                      *, lora, base, ck):
    c = CHUNK
    rows = slice(ck * CHUNK, (ck + 1) * CHUNK)

    head0 = _iota2((1, LANE), 1) < RWKV_HEAD
    r2 = _iota2((LANE, LANE), 0)
    c2 = _iota2((LANE, LANE), 1)
    same = (r2 // RWKV_HEAD) == (c2 // RWKV_HEAD)
    ones_bd = same.astype(F32)
    eye = r2 == c2

    def stack(x):
        z = jnp.zeros_like(x)
        return jnp.concatenate([jnp.where(head0, x, z), jnp.where(head0, z, x)], axis=0)

    stack_b = lambda x: stack(x.astype(BF16))
    fold = lambda x: x[:c] + x[c:]

    pairs = range(PAIRS_PER_GROUP)
    chains = [(q, d) for q in pairs for d in range(2)]
    qls = [slice((base + q) * LANE, (base + q + 1) * LANE) for q in pairs]
    ks = [zk_ref[0, rows, ql].astype(F32) for ql in qls]
    vs_ = [zv_ref[0, rows, ql].astype(F32) for ql in qls]
    rs = [zr_ref[0, rows, ql].astype(F32) for ql in qls]
    pcols = [slice((base + q) * 2 * LANE, (base + q + 1) * 2 * LANE) for q in pairs]
    wraw = [lora[0][rows, pc] for pc in pcols]
    araw = [lora[1][rows, pc] for pc in pcols]
    yield
    logw = [-EXP_M05 * _sigmoid(w) for w in wraw]
    a_all = [_sigmoid(a) for a in araw]
    kk0 = [ks[q] * kk_ref[:, qls[q]] for q in pairs]
    ss = [_mm2r(x * x, ones_bd) for x in kk0]
    yield
    kk = [kk0[q] / jnp.maximum(jnp.sqrt(ss[q]), 1e-12) for q in pairs]
    vstk = [stack_b(v) for v in vs_]

    dsl = [slice(d * LANE, (d + 1) * LANE) for d in range(2)]
    lw = [logw[q][:, dsl[d]] for q, d in chains]
    ad = [a_all[q][:, dsl[d]] for q, d in chains]
    kd = [ks[q] * (1.0 + (ad[i] - 1.0) * ka_ref[:, qls[q]]) for i, (q, d) in enumerate(chains)]
    bb = [kk[q] * ad[i] for i, (q, d) in enumerate(chains)]
    bonus = [_mm2r(rs[q] * (kd[2 * q] + kd[2 * q + 1]) * rk_ref[:, qls[q]], ones_bd) * vs_[q]
             for q in pairs]
    yield
    r3 = _iota2((c, 3 * c), 0)
    c3 = _iota2((c, 3 * c), 1) & (c - 1)
    tri3 = (c3 <= r3).astype(BF16)
    tt = _iota2((c, LANE), 0)
    ts = _iota2((c, LANE), 1) & (c - 1)
    strict = [ts < tt, ts > tt]
    incl = [ts <= tt, ts >= tt]
    eye_c = (ts == tt).astype(F32)

    def prefix(x):
        xh, xl = _split(x)
        xll = (x - xh.astype(F32) - xl.astype(F32)).astype(BF16)
        return jnp.dot(tri3, jnp.concatenate([xh, xl, xll], axis=0), preferred_element_type=F32)

    pre = [prefix(x) for x in logw]
    ltot = [pre[q][c - 1:c, dsl[d]] for q, d in chains]
    lc = [pre[q][:, dsl[0]] if d == 0 else ltot[i] - pre[q][:, dsl[1]] + lw[i]
          for i, (q, d) in enumerate(chains)]
    yield
    n = len(chains)
    bdot = lambda a, b: jnp.dot(a, b, preferred_element_type=F32)
    kkd = [(kk[q] * jnp.exp(lc[i] - lw[i])).astype(BF16) for i, (q, d) in enumerate(chains)]
    rd = [rs[q] * jnp.exp(lc[i]) for i, (q, d) in enumerate(chains)]
    e_inv = [jnp.exp(-x) for x in lc]
    inv_s = [jnp.concatenate([stack_b(bb[i] * e_inv[i]), stack_b(kd[i] * e_inv[i])], axis=0)
             for i in range(n)]
    yield
    amat = [lax.dot_general(jnp.concatenate([kkd[i], rd[i].astype(BF16)], axis=0), inv_s[i],
                            NT, preferred_element_type=F32).astype(BF16) for i in range(n)]
    zero_c = jnp.zeros((c, LANE), BF16)
    a_kb = [jnp.where(strict[d], amat[i][:c, :LANE], zero_c) for i, (q, d) in enumerate(chains)]
    a_kk = [jnp.where(strict[d], amat[i][:c, LANE:], zero_c) for i, (q, d) in enumerate(chains)]
    aq_b = [jnp.where(incl[d], amat[i][c:, :LANE], zero_c) for i, (q, d) in enumerate(chains)]
    aq_k = [jnp.where(incl[d], amat[i][c:, LANE:], zero_c) for i, (q, d) in enumerate(chains)]
    yield
    av = [bdot(jnp.concatenate([a_kk[i], aq_k[i]], axis=0), vstk[q])
          for i, (q, d) in enumerate(chains)]
    yield
    tinv = [eye_c - a.astype(F32) for a in a_kb]
    qpow = [bdot(a, stack(a)).astype(BF16) for a in a_kb]
    yield
    for _ in range(4):
        prod = [bdot(qpow[i], jnp.concatenate([stack(qpow[i]), stack_b(tinv[i])], axis=1))
                for i in range(n)]
        qpow = [x[:, :LANE].astype(BF16) for x in prod]
        tinv = [tinv[i] + prod[i][:, LANE:] for i in range(n)]
        yield
    tinv = [tinv[i] + bdot(qpow[i], stack_b(tinv[i])) for i in range(n)]
    yield
    tsplit = [_split(t) for t in tinv]
    ia_t = [bdot((eye_c + a_kb[i].astype(F32)).astype(BF16),
                 jnp.concatenate([stack(tsplit[i][0]), stack(tsplit[i][1])], axis=1))
            for i in range(n)]
    resid = [eye_c - ia_t[i][:, :LANE] - ia_t[i][:, LANE:] for i in range(n)]
    yield
    tinv = [tinv[i] + bdot(tinv[i].astype(BF16), stack_b(resid[i])) for i in range(n)]
    yield
    x = [bdot(tinv[i].astype(BF16),
              jnp.concatenate([stack(kkd[i]), stack_b(av[i][:c])], axis=1))
         for i in range(n)]
    xb = [v.astype(BF16) for v in x]
    yield
    qy = [jnp.concatenate([rd[i], av[i][c:]], axis=1)
          - bdot(aq_b[i], jnp.concatenate([stack(xb[i][:, :LANE]), stack(xb[i][:, LANE:])],
                                          axis=1)) for i in range(n)]
    yield
    e_end = [jnp.exp(ltot[i] - lc[i]) for i in range(n)]
    ends = [jnp.concatenate([(-bb[i] * e_end[i]).astype(BF16), (kd[i] * e_end[i]).astype(BF16)],
                            axis=0) for i in range(n)]
    wuv = [jnp.concatenate([xb[i], jnp.concatenate([zero_c, vs_[q].astype(BF16)], axis=1)],
                           axis=0) for i, (q, d) in enumerate(chains)]
    mg = [lax.dot_general(ends[i], wuv[i], TN, preferred_element_type=F32) for i in range(n)]
    yield
    for i, (q, d) in enumerate(chains):
        m_ref[0, ck, d, base + q] = fold(jnp.where(eye, jnp.exp(ltot[i]), 0.0)
                                         + jnp.where(same, mg[i][:, :LANE], 0.0)).astype(BF16)
        g_ref[0, ck, d, base + q] = fold(jnp.where(same, mg[i][:, LANE:], 0.0)
                                         ).astype(g_ref.dtype)
        qt_ref[0, d, rows, qls[q]] = qy[i][:, :LANE].astype(BF16)
    for q in pairs:
        yl_ref[0, rows, qls[q]] = (qy[2 * q][:, LANE:]
                                   + qy[2 * q + 1][:, LANE:]).astype(yl_ref.dtype)
        bn_ref[0, rows, qls[q]] = bonus[q].astype(bn_ref.dtype)


def _rwkv_local(z, w0p, w2p, a0p, a2p, k_k, k_a, r_k):
    b, t, _ = z.shape
    w = k_k.shape[1]
    npair = w // LANE
    pp = PAIRS_PER_STEP
    ng = npair // pp
    wl = pp * LANE
    nc = t // CHUNK
    cs = LOCAL_CHUNKS if nc % LOCAL_CHUNKS == 0 else 1
    rows = cs * CHUNK
    tokc = lambda base: pl.BlockSpec((1, rows, wl), lambda bi, ci, p: (bi, ci, base + p))
    perp3 = lambda n: pl.BlockSpec((n, pp * 2 * LANE), lambda bi, ci, p: (0, p))
    vecp = pl.BlockSpec((1, wl), lambda bi, ci, p: (0, p))
    mat = pl.BlockSpec((1, cs, 2, pp, CHUNK, LANE), lambda bi, ci, p: (bi, ci, 0, p, 0, 0))
    return pl.pallas_call(
        functools.partial(_rwkv_local_body, cs=cs),
        grid=(b, nc // cs, ng),
        in_specs=[tokc(0), tokc(ng), tokc(2 * ng),
                  pl.BlockSpec((1, rows, 2 * LANE), lambda bi, ci, p: (bi, ci, 3 * npair // 2)),
                  perp3(1), perp3(LANE), perp3(1), perp3(LANE), vecp, vecp, vecp],
        out_specs=[mat, mat,
                   pl.BlockSpec((1, 2, rows, wl), lambda bi, ci, p: (bi, 0, ci, p)),
                   pl.BlockSpec((1, rows, wl), lambda bi, ci, p: (bi, ci, p)),
                   pl.BlockSpec((1, rows, wl), lambda bi, ci, p: (bi, ci, p))],
        out_shape=[jax.ShapeDtypeStruct((b, nc, 2, npair, CHUNK, LANE), BF16),
                   jax.ShapeDtypeStruct((b, nc, 2, npair, CHUNK, LANE), ACT),
                   jax.ShapeDtypeStruct((b, 2, t, w), BF16),
                   jax.ShapeDtypeStruct((b, t, w), ACT),
                   jax.ShapeDtypeStruct((b, t, w), ACT)],
        compiler_params=_cparams(("parallel", "parallel", "parallel")),
        name="rwkv_local",
    )(z, z, z, z, w0p, w2p, a0p, a2p, k_k, k_a, r_k)


SCAN_CHUNKS = 16


def _rwkv_scan_body(m0_ref, g0_ref, q0_ref, m1_ref, g1_ref, q1_ref, h0_ref,
                    y0_ref, y1_ref, hfin_ref, h_scr, *, npair, cs):
    ci = pl.program_id(1)

    @pl.when(ci == 0)
    def _():
        h_scr[...] = h0_ref[0]

    head0 = _iota2((1, LANE), 1) < RWKV_HEAD

    def expand(x):
        z = jnp.zeros_like(x)
        return jnp.concatenate([jnp.where(head0, x, z), jnp.where(head0, z, x)], axis=0)

    refs = ((m0_ref, g0_ref, q0_ref, y0_ref), (m1_ref, g1_ref, q1_ref, y1_ref))
    chains = [(d, p) for d in range(2) for p in range(npair)]
    lanes = [slice(p * LANE, (p + 1) * LANE) for p in range(npair)]
    h = [h_scr[d, p] for d, p in chains]
    for step in range(cs):
        ck = (step, cs - 1 - step)
        rows = [slice(c * CHUNK, (c + 1) * CHUNK) for c in ck]
        hb = [x.astype(BF16) for x in h]
        res = [jnp.dot(jnp.concatenate([refs[d][2][0, 0, rows[d], lanes[p]],
                                        refs[d][0][0, ck[d], 0, p]], axis=0), hb[i],
                       preferred_element_type=F32) for i, (d, p) in enumerate(chains)]
        for i, (d, p) in enumerate(chains):
            refs[d][3][0, rows[d], lanes[p]] = res[i][:CHUNK].astype(refs[d][3].dtype)
        h = [expand(res[i][CHUNK:] + refs[d][1][0, ck[d], 0, p].astype(F32))
             for i, (d, p) in enumerate(chains)]
    for i, (d, p) in enumerate(chains):
        h_scr[d, p] = h[i]

    @pl.when(ci == pl.num_programs(1) - 1)
    def _():
        hfin_ref[0] = h_scr[...]


def _rwkv_scan(mm, gg, qt, h0):
    b, nc, _, npair, _, _ = mm.shape
    t, w = qt.shape[2], qt.shape[3]
    cs = SCAN_CHUNKS if nc % SCAN_CHUNKS == 0 else 1
    nb = nc // cs
    fwd = lambda bi, ci: (bi, ci, 0, 0, 0, 0)
    rev = lambda bi, ci: (bi, nb - 1 - ci, 1, 0, 0, 0)
    mblk = (1, cs, 1, npair, CHUNK, LANE)
    hspec = pl.BlockSpec((1, 2, npair, LANE, LANE), lambda bi, ci: (bi, 0, 0, 0, 0))
    return pl.pallas_call(
        functools.partial(_rwkv_scan_body, npair=npair, cs=cs),
        grid=(b, nb),
        in_specs=[pl.BlockSpec(mblk, fwd), pl.BlockSpec(mblk, fwd),
                  pl.BlockSpec((1, 1, cs * CHUNK, w), lambda bi, ci: (bi, 0, ci, 0)),
                  pl.BlockSpec(mblk, rev), pl.BlockSpec(mblk, rev),
                  pl.BlockSpec((1, 1, cs * CHUNK, w), lambda bi, ci: (bi, 1, nb - 1 - ci, 0)),
                  hspec],
        out_specs=[pl.BlockSpec((1, cs * CHUNK, w), lambda bi, ci: (bi, ci, 0)),
                   pl.BlockSpec((1, cs * CHUNK, w), lambda bi, ci: (bi, nb - 1 - ci, 0)),
                   hspec],
        out_shape=[jax.ShapeDtypeStruct((b, t, w), ACT), jax.ShapeDtypeStruct((b, t, w), ACT),
                   jax.ShapeDtypeStruct(h0.shape, F32)],
        scratch_shapes=[pltpu.VMEM((2, npair, LANE, LANE), F32)],
        compiler_params=_cparams(("parallel", "arbitrary")),
        name="rwkv_scan",
    )(mm, gg, qt, mm, gg, qt, h0)


def _rwkv_out_body(y0_ref, y1_ref, yl_ref, bn_ref, gate_ref, x_ref, gl_ref, gnw_ref, gnb_ref,
                   w_ref, o_ref):
    y = y0_ref[0].astype(F32) + y1_ref[0].astype(F32) + yl_ref[0].astype(F32)
    r2 = _iota2((LANE, LANE), 0)
    c2 = _iota2((LANE, LANE), 1)
    avg = ((r2 // RWKV_HEAD) == (c2 // RWKV_HEAD)).astype(F32) * (1.0 / RWKV_HEAD)
    parts = []
    for p in range(y.shape[1] // LANE):
        yp = y[:, p * LANE:(p + 1) * LANE]
        dl = yp - _mm2r(yp, avg)
        var = _mm2r(dl * dl, avg)
        parts.append(dl * lax.rsqrt(var + GN_EPS))
    yn = jnp.concatenate(parts, axis=1)
    gt = gate_ref[0].astype(F32)
    act = (yn * gnw_ref[...] + gnb_ref[...] + bn_ref[0].astype(F32)) * (gt * _sigmoid(gt))
    out = jnp.dot(act.astype(BF16), w_ref[...], preferred_element_type=F32)
    o_ref[0] = x_ref[0] + gl_ref[0] * out


def _rwkv_out(y0, y1, yl, bn, gate, x, gl, gnw, gnb, w, tm):
    b, t, d = x.shape
    tm = min(tm, t)
    wd = y0.shape[2]
    tok = lambda n: pl.BlockSpec((1, tm, n), lambda bi, i: (bi, i, 0))
    return pl.pallas_call(
        _rwkv_out_body,
        grid=(b, t // tm),
        in_specs=[tok(wd), tok(wd), tok(wd), tok(wd), tok(wd), tok(d),
                  pl.BlockSpec((1, 1, d), lambda bi, i: (bi, 0, 0)),
                  pl.BlockSpec((1, wd), lambda bi, i: (0, 0)),
                  pl.BlockSpec((1, wd), lambda bi, i: (0, 0)),
                  pl.BlockSpec(w.shape, lambda bi, i: (0, 0))],
        out_specs=tok(d),
        out_shape=jax.ShapeDtypeStruct((b, t, d), F32),
        compiler_params=_cparams(("parallel", "parallel")),
        name="rwkv_out",
    )(y0, y1, yl, bn, gate, x, gl, gnw, gnb, w)


def _rope_tables(t):
    rows = t // GRID_W
    row = jnp.repeat(jnp.arange(rows, dtype=F32), GRID_W)
    col = jnp.tile(jnp.arange(GRID_W, dtype=F32), rows)
    inv = 1.0 / (ROPE_BASE ** (jnp.arange(ROPE_FREQS, dtype=F32) / ROPE_FREQS))
    ang = jnp.stack([row[:, None] * inv, col[:, None] * inv], axis=1)
    cos, sin = jnp.cos(ang), jnp.sin(ang)
    zeros = jnp.zeros_like(sin)
    ones_lo = jnp.ones((t, QK_NOPE), F32)
    pad_hi = HEAD_SLOT - QK_HEAD
    cos_t = jnp.concatenate([ones_lo, jnp.concatenate([cos, cos], axis=2).reshape(t, QK_ROPE),
                             jnp.ones((t, pad_hi), F32)], axis=1)
    sa = jnp.concatenate([jnp.zeros((t, QK_NOPE), F32),
                          jnp.concatenate([-sin, zeros], axis=2).reshape(t, QK_ROPE),
                          jnp.zeros((t, pad_hi), F32)], axis=1)
    sb = jnp.concatenate([jnp.zeros((t, QK_NOPE), F32),
                          jnp.concatenate([zeros, sin], axis=2).reshape(t, QK_ROPE),
                          jnp.zeros((t, pad_hi), F32)], axis=1)
    return cos_t, sa, sb


def _even_layer(x, ctx, mod_l, mod_c, need_ctx, g, w_in, kv_norm, q_norm, w_uq, w_ukv,
                q_head_norm, k_head_norm, w_fnet, w_out):
    b, s, d = x.shape
    tc = ctx.shape[1]
    e_q0 = KV_LORA + QK_ROPE
    e_f0 = e_q0 + Q_LORA
    e_g0 = e_f0 + FNET_GROUPS * FNET_GROUP_DIM
    w_p = jnp.concatenate([w_in[:, e_g0:], w_in[:, e_f0:e_g0], w_in[:, :e_q0],
                           jnp.zeros((d, LANE - QK_ROPE), F32), w_in[:, e_q0:e_f0]],
                          axis=1).astype(BF16)
    splits = (d, (FNET_GROUPS, FNET_GROUP_DIM), KV_LORA + LANE + Q_LORA)
    kvw = w_ukv.reshape(KV_LORA, MLA_HEADS, QK_NOPE + V_HEAD)
    wk = jnp.pad(kvw[:, :, :QK_NOPE], ((0, 0), (0, 0), (0, HEAD_SLOT - QK_NOPE)))
    wk = wk.reshape(KV_LORA, MLA_HEADS * HEAD_SLOT).astype(BF16)
    wv = jnp.pad(kvw[:, :, QK_NOPE:], ((0, 0), (0, 0), (0, HEAD_SLOT - V_HEAD)))
    wv = wv.reshape(KV_LORA, MLA_HEADS * HEAD_SLOT).astype(BF16)
    wq3 = jnp.pad(w_uq.reshape(Q_LORA, MLA_HEADS, QK_HEAD), ((0, 0), (0, 0), (0, HEAD_SLOT - QK_HEAD)))
    wq = wq3.reshape(Q_LORA, MLA_HEADS * HEAD_SLOT).astype(BF16)
    kg = jnp.pad(k_head_norm, (0, HEAD_SLOT - QK_HEAD)).reshape(1, HEAD_SLOT)
    qg = (jnp.pad(q_head_norm, (0, HEAD_SLOT - QK_HEAD))
          * (QK_HEAD ** -0.5 * math.log2(math.e))).reshape(1, HEAD_SLOT)
    lane = np.arange(HEAD_SLOT)
    tail = (lane >= QK_NOPE) & (lane < QK_HEAD)
    first = tail & (((lane - QK_NOPE) // ROPE_FREQS) % 2 == 0)
    partner = np.where(first, lane + ROPE_FREQS, np.where(tail, lane - ROPE_FREQS, lane))
    sign = np.where(first, -1.0, np.where(tail, 1.0, 0.0)).astype(np.float32)
    wqr = (wq3[:, :, partner] * (sign * qg[0, partner])).reshape(Q_LORA, MLA_HEADS * HEAD_SLOT)
    wqr = wqr.astype(BF16)
    kvn, qn = kv_norm.reshape(1, -1), q_norm.reshape(1, -1)
    g2 = g.reshape(1, d)
    bound = (1.02 * QK_HEAD * jnp.max(jnp.abs(qg)) * jnp.max(jnp.abs(kg))).astype(BF16).astype(F32)
    static_ok = bound <= MAX_STATIC_BOUND
    bias_lane = (jnp.arange(HEAD_SLOT) == BIAS_LANE).astype(F32).reshape(1, HEAD_SLOT)
    kb = bias_lane * jnp.where(static_ok, -bound, 0.0)
    qb = bias_lane

    gate_l, four_l, ua_l = _proj(x, g2, mod_l[1], mod_l[0], w_p, splits, TOKEN_TILE)
    gate_c, four_c, ua_c = _proj(ctx, g2, mod_c[1], mod_c[0], w_p, splits, TOKEN_TILE)
    sk = s + tc
    cos_t, sa, sb = _rope_tables(s)
    cos_t = jnp.concatenate([cos_t, jnp.ones((tc, HEAD_SLOT), F32)], axis=0)
    sa = jnp.concatenate([sa, jnp.zeros((tc, HEAD_SLOT), F32)], axis=0)
    sb = jnp.concatenate([sb, jnp.zeros((tc, HEAD_SLOT), F32)], axis=0)
    tabs = (cos_t * kg, cos_t * qg, sa, sb, sb - sa)
    q_all, k_all, v_all = _qkv(ua_l, ua_c, kvn, qn, wk, wv, wq, wqr, kg, kb, qb, tabs,
                               math.gcd(s, tc))
    bk = ATTN_K_BLOCK if sk % ATTN_K_BLOCK == 0 else tc
    o_l = _attention(q_all, k_all, v_all, static_ok, 0, s, 0, sk, ATTN_Q_BLOCK, bk)
    f_l = _fourier_latent(four_l, w_fnet)
    wo = w_out.astype(BF16)
    x_new = _merge(o_l, f_l, gate_l, x, mod_l[2], wo, OUT_TILE)
    ctx_new = ctx
    if need_ctx:
        o_c = _attention(q_all, k_all, v_all, static_ok, s, tc, s, tc, tc, tc)
        f_c = _fourier_dense(four_c, w_fnet)
        ctx_new = _merge(o_c, f_c, gate_c, ctx, mod_c[2], wo, OUT_TILE)
    return x_new, ctx_new


def _odd_layer(x, ctx, mod_l, mod_c, need_ctx, g, w_in, shift_w, w0, w2, a0, a2, k_k, k_a, r_k,
               gn_w, gn_b, w_out):
    b, s, d = x.shape
    w = k_k.shape[0]
    npair = w // LANE
    o_wd0 = 2 * w
    o_r0 = o_wd0 + 2 * DECAY_LORA + 2 * AAA_LORA
    conv_ch = o_r0 + w
    segs = ((0, o_wd0), (o_r0, w), (o_wd0, o_r0 - o_wd0))
    w_p = w_in.astype(BF16)
    sw = shift_w
    g2 = g.reshape(1, d)

    def pairs(vec2):
        return vec2.reshape(2, npair, LANE).transpose(1, 0, 2).reshape(1, npair * 2 * LANE)

    def pair_mats(m):
        rr = m.shape[1]
        mp = m.reshape(2, rr, npair, LANE).transpose(2, 0, 1, 3)
        z = jnp.zeros_like(mp[:, 0])
        top = jnp.concatenate([mp[:, 0], z], axis=2)
        bot = jnp.concatenate([z, mp[:, 1]], axis=2)
        full = jnp.concatenate([top, bot], axis=1)
        return full.transpose(1, 0, 2).reshape(2 * rr, npair * 2 * LANE).astype(BF16)

    w0p, a0p, w2p, a2p = pairs(w0), pairs(a0), pair_mats(w2), pair_mats(a2)
    kk2, ka2, rk2 = k_k.reshape(1, w), k_a.reshape(1, w), r_k.reshape(1, w)
    wo = w_out.astype(BF16)

    def mix(xin, mod, h0):
        z, gate = _proj_shift(xin, g2, mod[1], mod[0], w_p, sw, conv_ch, segs, TOKEN_TILE)
        mm, gg, qt, yl, bn = _rwkv_local(z, w0p, w2p, a0p, a2p, kk2, ka2, rk2)
        y0, y1, hfin = _rwkv_scan(mm, gg, qt, h0)
        return (y0, y1, yl, bn, gate), hfin

    h_zero = jnp.zeros((b, 2, npair, LANE, LANE), F32)
    parts_c, h_ctx = mix(ctx, mod_c, h_zero)
    parts_l, _ = mix(x, mod_l, h_ctx)
    gnw, gnb = gn_w.reshape(1, w), gn_b.reshape(1, w)
    x_new = _rwkv_out(*parts_l, x, mod_l[2], gnw, gnb, wo, OUT_TILE)
    ctx_new = ctx
    if need_ctx:
        ctx_new = _rwkv_out(*parts_c, ctx, mod_c[2], gnw, gnb, wo, OUT_TILE)
    return x_new, ctx_new


def kernel(x, c, ctx, c_ctx, ada_w, ada_b, norm_g, e_w_in, e_kv_norm, e_q_norm, e_w_uq, e_w_ukv,
           e_q_head_norm, e_k_head_norm, e_w_fnet, e_w_out, o_w_in, o_shift_w, o_w0, o_w2, o_a0,
           o_a2, o_k_k, o_k_a, o_r_k, o_gn_w, o_gn_b, o_w_out):
    b, s, d = x.shape
    depth = ada_w.shape[0]
    assert b + 1 <= 8
    cond8 = jnp.concatenate([c, c_ctx[None, :], jnp.zeros((8 - b - 1, d), F32)], axis=0)
    mod = _ada(cond8, ada_w, ada_b)
    for layer in range(depth):
        need_ctx = layer < depth - 1
        m = mod[layer]
        chunk = lambda rows, i: rows[:, None, i * d:(i + 1) * d]
        lat, cx = m[:b], jnp.broadcast_to(m[b:b + 1], (b, 3 * d))
        mod_l = (chunk(lat, 0), 1.0 + chunk(lat, 1), chunk(lat, 2))
        mod_c = (chunk(cx, 0), 1.0 + chunk(cx, 1), chunk(cx, 2))
        j = layer // 2
        if layer % 2 == 0:
            x, ctx = _even_layer(x, ctx, mod_l, mod_c, need_ctx, norm_g[layer], e_w_in[j],
                                 e_kv_norm[j], e_q_norm[j], e_w_uq[j], e_w_ukv[j],
                                 e_q_head_norm[j], e_k_head_norm[j], e_w_fnet[j], e_w_out[j])
        else:
            x, ctx = _odd_layer(x, ctx, mod_l, mod_c, need_ctx, norm_g[layer], o_w_in[j],
                                o_shift_w[j], o_w0[j], o_w2[j], o_a0[j], o_a2[j], o_k_k[j],
                                o_k_a[j], o_r_k[j].reshape(-1), o_gn_w[j], o_gn_b[j], o_w_out[j])
    return x
```
